```python
import jax, jax.numpy as jnp
from jax import lax
import numpy as np

D_MODEL = 2048
BATCH = 8
SEQ = 4096
DEPTH = 1

N_META = 16
HEAD_DIM = 128
FOX_HEADS = 8
SB_HEADS = 8
FOX_WIDTH = FOX_HEADS * HEAD_DIM
SB_WIDTH = SB_HEADS * HEAD_DIM
D_FF = 5632
Q_BLOCK = 128
RMS_EPS = 1e-6
FFN_RESIDUAL_WEIGHT = 0.5
FORGET_BIAS_INIT = 3.0

IN_SIZES = [FOX_WIDTH, FOX_WIDTH, FOX_WIDTH, FOX_HEADS,
            SB_WIDTH, SB_WIDTH, SB_WIDTH,
            D_MODEL, D_MODEL]
IN_PROJ_WIDTH = sum(IN_SIZES)
IN_SPLIT_POINTS = [int(v) for v in np.cumsum(IN_SIZES)[:-1]]

kernel_name = "hybrid_fox_stickbreak_macaron"


def _rmsnorm(x, gain):
    x32 = x.astype(jnp.float32)
    y = x32 * lax.rsqrt(jnp.mean(x32 * x32, axis=-1, keepdims=True) + RMS_EPS)
    return (y * gain.astype(jnp.float32)).astype(x.dtype)


def _swiglu(x, w_gate, w_up, w_down):
    return (jax.nn.silu(x @ w_gate) * (x @ w_up)) @ w_down


def _query_blocks(total_len):
    blocks = [(0, N_META)]
    start = N_META
    while start < total_len:
        end = min(start + Q_BLOCK, total_len)
        blocks.append((start, end))
        start = end
    return blocks


def _fox_attention(q, k, v, log_f_cum):
    L = q.shape[1]
    scale = HEAD_DIM ** -0.5
    outs = []
    for q0, q1 in _query_blocks(L):
        qb, kb, vb = q[:, q0:q1], k[:, :q1], v[:, :q1]
        logits = jnp.einsum('bqhd,bkhd->bhqk', qb, kb).astype(jnp.float32) * scale
        decay = log_f_cum[:, :, q0:q1, None] - log_f_cum[:, :, None, :q1]
        t = jnp.arange(q0, q1)[:, None]
        s = jnp.arange(q1)[None, :]
        logits = jnp.where(s <= t, logits + decay, -jnp.inf)
        p = jax.nn.softmax(logits, axis=-1)
        outs.append(jnp.einsum('bhqk,bkhd->bqhd', p.astype(vb.dtype), vb))
    return jnp.concatenate(outs, axis=1)


def _stick_breaking_attention(q, k, v):
    L = q.shape[1]
    scale = HEAD_DIM ** -0.5
    outs = []
    for q0, q1 in _query_blocks(L):
        qb, kb, vb = q[:, q0:q1], k[:, :q1], v[:, :q1]
        z = jnp.einsum('bqhd,bkhd->bhqk', qb, kb).astype(jnp.float32) * scale
        t = jnp.arange(q0, q1)[:, None]
        s = jnp.arange(q1)[None, :]
        strict = s < t
        log_beta = jax.nn.log_sigmoid(z)
        log_one_minus = jnp.where(strict, log_beta - z, 0.0)
        later = lax.cumsum(log_one_minus, axis=3, reverse=True) - log_one_minus
        w = jnp.where(strict, jnp.exp(log_beta + later), 0.0)
        outs.append(jnp.einsum('bhqk,bkhd->bqhd', w.astype(vb.dtype), vb))
    return jnp.concatenate(outs, axis=1)


def _hybrid_mixer(xn, w_in, b_forget, fox_q_norm, fox_k_norm, w_branch_fox, w_branch_sb, w_out):
    B, L, _ = xn.shape
    proj = xn @ w_in
    fq, fk, fv, f_logit, sq, sk, sv, g_fox, g_sb = jnp.split(proj, IN_SPLIT_POINTS, axis=-1)
    heads = lambda a, n: a.reshape(B, L, n, HEAD_DIM)
    fq = _rmsnorm(heads(fq, FOX_HEADS), fox_q_norm)
    fk = _rmsnorm(heads(fk, FOX_HEADS), fox_k_norm)
    log_f = jax.nn.log_sigmoid((f_logit + b_forget).astype(jnp.float32))
    log_f_cum = jnp.transpose(lax.cumsum(log_f, axis=1), (0, 2, 1))
    o_fox = _fox_attention(fq, fk, heads(fv, FOX_HEADS), log_f_cum).reshape(B, L, FOX_WIDTH)
    o_sb = _stick_breaking_attention(heads(sq, SB_HEADS), heads(sk, SB_HEADS),
                                     heads(sv, SB_HEADS)).reshape(B, L, SB_WIDTH)
    merged = jax.nn.sigmoid(g_fox) * (o_fox @ w_branch_fox) + jax.nn.sigmoid(g_sb) * (o_sb @ w_branch_sb)
    return merged @ w_out


def _fwd_setup_inputs(seed: int = 0) -> dict:
    key = jax.random.key(seed)
    ks = jax.random.split(key, 20)
    f32 = jnp.float32
    nrm = lambda k, shape, fan_in: jax.random.normal(k, shape, f32) * (fan_in ** -0.5)
    gain = lambda k, shape: 1.0 + 0.02 * jax.random.normal(k, shape, f32)
    return {
        "x": jax.random.normal(ks[0], (BATCH, SEQ, D_MODEL), f32),
        "meta_tokens": jax.random.normal(ks[1], (N_META, D_MODEL), f32),
        "ffn1_norm": gain(ks[2], (DEPTH, D_MODEL)),
        "ffn1_w_gate": nrm(ks[3], (DEPTH, D_MODEL, D_FF), D_MODEL),
        "ffn1_w_up": nrm(ks[4], (DEPTH, D_MODEL, D_FF), D_MODEL),
        "ffn1_w_down": nrm(ks[5], (DEPTH, D_FF, D_MODEL), D_FF),
        "mix_norm": gain(ks[6], (DEPTH, D_MODEL)),
        "w_in": nrm(ks[7], (DEPTH, D_MODEL, IN_PROJ_WIDTH), D_MODEL),
        "b_forget": FORGET_BIAS_INIT + 0.1 * jax.random.normal(ks[8], (DEPTH, FOX_HEADS), f32),
        "fox_q_norm": gain(ks[9], (DEPTH, FOX_HEADS, HEAD_DIM)),
        "fox_k_norm": gain(ks[10], (DEPTH, FOX_HEADS, HEAD_DIM)),
        "w_branch_fox": nrm(ks[11], (DEPTH, FOX_WIDTH, D_MODEL), FOX_WIDTH),
        "w_branch_sb": nrm(ks[12], (DEPTH, SB_WIDTH, D_MODEL), SB_WIDTH),
        "w_out": nrm(ks[13], (DEPTH, D_MODEL, D_MODEL), D_MODEL),
        "ffn2_norm": gain(ks[14], (DEPTH, D_MODEL)),
        "ffn2_w_gate": nrm(ks[15], (DEPTH, D_MODEL, D_FF), D_MODEL),
        "ffn2_w_up": nrm(ks[16], (DEPTH, D_MODEL, D_FF), D_MODEL),
        "ffn2_w_down": nrm(ks[17], (DEPTH, D_FF, D_MODEL), D_FF),
    }


def _fwd_reference(x, meta_tokens, ffn1_norm, ffn1_w_gate, ffn1_w_up, ffn1_w_down, mix_norm, w_in, b_forget,
              fox_q_norm, fox_k_norm, w_branch_fox, w_branch_sb, w_out, ffn2_norm, ffn2_w_gate, ffn2_w_up,
              ffn2_w_down):
    B = x.shape[0]
    meta = jnp.broadcast_to(meta_tokens[None].astype(x.dtype), (B, N_META, D_MODEL))
    h = jnp.concatenate([meta, x], axis=1)
    for layer in range(DEPTH):
        h = h + FFN_RESIDUAL_WEIGHT * _swiglu(_rmsnorm(h, ffn1_norm[layer]), ffn1_w_gate[layer],
                                              ffn1_w_up[layer], ffn1_w_down[layer])
        h = h + _hybrid_mixer(_rmsnorm(h, mix_norm[layer]), w_in[layer], b_forget[layer],
                              fox_q_norm[layer], fox_k_norm[layer], w_branch_fox[layer],
                              w_branch_sb[layer], w_out[layer])
        h = h + FFN_RESIDUAL_WEIGHT * _swiglu(_rmsnorm(h, ffn2_norm[layer]), ffn2_w_gate[layer],
                                              ffn2_w_up[layer], ffn2_w_down[layer])
    return h[:, N_META:]


import jax as _jax
import jax.numpy as _jnp

TWIN_FORMAT = 'train_step'
FWD_PARAMS = ['x', 'meta_tokens', 'ffn1_norm', 'ffn1_w_gate', 'ffn1_w_up', 'ffn1_w_down', 'mix_norm', 'w_in', 'b_forget', 'fox_q_norm', 'fox_k_norm', 'w_branch_fox', 'w_branch_sb', 'w_out', 'ffn2_norm', 'ffn2_w_gate', 'ffn2_w_up', 'ffn2_w_down']
TWIN_WEIGHTS = ['meta_tokens', 'ffn1_norm', 'ffn1_w_gate', 'ffn1_w_up', 'ffn1_w_down', 'mix_norm', 'w_in', 'b_forget', 'fox_q_norm', 'fox_k_norm', 'w_branch_fox', 'w_branch_sb', 'w_out', 'ffn2_norm', 'ffn2_w_gate', 'ffn2_w_up', 'ffn2_w_down']
TWIN_DIFF_INPUT = 'x'
TWIN_INPUTS = ['x', 'meta_tokens', 'ffn1_norm', 'ffn1_w_gate', 'ffn1_w_up', 'ffn1_w_down', 'mix_norm', 'w_in', 'b_forget', 'fox_q_norm', 'fox_k_norm', 'w_branch_fox', 'w_branch_sb', 'w_out', 'ffn2_norm', 'ffn2_w_gate', 'ffn2_w_up', 'ffn2_w_down', 'loss_target', 'm_meta_tokens', 'm_ffn1_norm', 'm_ffn1_w_gate', 'm_ffn1_w_up', 'm_ffn1_w_down', 'm_mix_norm', 'm_w_in', 'm_b_forget', 'm_fox_q_norm', 'm_fox_k_norm', 'm_w_branch_fox', 'm_w_branch_sb', 'm_w_out', 'm_ffn2_norm', 'm_ffn2_w_gate', 'm_ffn2_w_up', 'm_ffn2_w_down', 'v_meta_tokens', 'v_ffn1_norm', 'v_ffn1_w_gate', 'v_ffn1_w_up', 'v_ffn1_w_down', 'v_mix_norm', 'v_w_in', 'v_b_forget', 'v_fox_q_norm', 'v_fox_k_norm', 'v_w_branch_fox', 'v_w_branch_sb', 'v_w_out', 'v_ffn2_norm', 'v_ffn2_w_gate', 'v_ffn2_w_up', 'v_ffn2_w_down']
TWIN_OUTPUTS = ['loss', 'grad_x', 'grad_meta_tokens', 'grad_ffn1_norm', 'grad_ffn1_w_gate', 'grad_ffn1_w_up', 'grad_ffn1_w_down', 'grad_mix_norm', 'grad_w_in', 'grad_b_forget', 'grad_fox_q_norm', 'grad_fox_k_norm', 'grad_w_branch_fox', 'grad_w_branch_sb', 'grad_w_out', 'grad_ffn2_norm', 'grad_ffn2_w_gate', 'grad_ffn2_w_up', 'grad_ffn2_w_down', 'delta_meta_tokens', 'delta_ffn1_norm', 'delta_ffn1_w_gate', 'delta_ffn1_w_up', 'delta_ffn1_w_down', 'delta_mix_norm', 'delta_w_in', 'delta_b_forget', 'delta_fox_q_norm', 'delta_fox_k_norm', 'delta_w_branch_fox', 'delta_w_branch_sb', 'delta_w_out', 'delta_ffn2_norm', 'delta_ffn2_w_gate', 'delta_ffn2_w_up', 'delta_ffn2_w_down', 'new_m_meta_tokens', 'new_m_ffn1_norm', 'new_m_ffn1_w_gate', 'new_m_ffn1_w_up', 'new_m_ffn1_w_down', 'new_m_mix_norm', 'new_m_w_in', 'new_m_b_forget', 'new_m_fox_q_norm', 'new_m_fox_k_norm', 'new_m_w_branch_fox', 'new_m_w_branch_sb', 'new_m_w_out', 'new_m_ffn2_norm', 'new_m_ffn2_w_gate', 'new_m_ffn2_w_up', 'new_m_ffn2_w_down', 'new_v_meta_tokens', 'new_v_ffn1_norm', 'new_v_ffn1_w_gate', 'new_v_ffn1_w_up', 'new_v_ffn1_w_down', 'new_v_mix_norm', 'new_v_w_in', 'new_v_b_forget', 'new_v_fox_q_norm', 'new_v_fox_k_norm', 'new_v_w_branch_fox', 'new_v_w_branch_sb', 'new_v_w_out', 'new_v_ffn2_norm', 'new_v_ffn2_w_gate', 'new_v_ffn2_w_up', 'new_v_ffn2_w_down']
TWIN_LEAF_KINDS = {'loss': 'loss', 'grad_x': 'grad_x', 'grad_meta_tokens': 'grad_w', 'grad_ffn1_norm': 'grad_w', 'grad_ffn1_w_gate': 'grad_w', 'grad_ffn1_w_up': 'grad_w', 'grad_ffn1_w_down': 'grad_w', 'grad_mix_norm': 'grad_w', 'grad_w_in': 'grad_w', 'grad_b_forget': 'grad_w', 'grad_fox_q_norm': 'grad_w', 'grad_fox_k_norm': 'grad_w', 'grad_w_branch_fox': 'grad_w', 'grad_w_branch_sb': 'grad_w', 'grad_w_out': 'grad_w', 'grad_ffn2_norm': 'grad_w', 'grad_ffn2_w_gate': 'grad_w', 'grad_ffn2_w_up': 'grad_w', 'grad_ffn2_w_down': 'grad_w', 'delta_meta_tokens': 'delta_w', 'delta_ffn1_norm': 'delta_w', 'delta_ffn1_w_gate': 'delta_w', 'delta_ffn1_w_up': 'delta_w', 'delta_ffn1_w_down': 'delta_w', 'delta_mix_norm': 'delta_w', 'delta_w_in': 'delta_w', 'delta_b_forget': 'delta_w', 'delta_fox_q_norm': 'delta_w', 'delta_fox_k_norm': 'delta_w', 'delta_w_branch_fox': 'delta_w', 'delta_w_branch_sb': 'delta_w', 'delta_w_out': 'delta_w', 'delta_ffn2_norm': 'delta_w', 'delta_ffn2_w_gate': 'delta_w', 'delta_ffn2_w_up': 'delta_w', 'delta_ffn2_w_down': 'delta_w', 'new_m_meta_tokens': 'new_m', 'new_m_ffn1_norm': 'new_m', 'new_m_ffn1_w_gate': 'new_m', 'new_m_ffn1_w_up': 'new_m', 'new_m_ffn1_w_down': 'new_m', 'new_m_mix_norm': 'new_m', 'new_m_w_in': 'new_m', 'new_m_b_forget': 'new_m', 'new_m_fox_q_norm': 'new_m', 'new_m_fox_k_norm': 'new_m', 'new_m_w_branch_fox': 'new_m', 'new_m_w_branch_sb': 'new_m', 'new_m_w_out': 'new_m', 'new_m_ffn2_norm': 'new_m', 'new_m_ffn2_w_gate': 'new_m', 'new_m_ffn2_w_up': 'new_m', 'new_m_ffn2_w_down': 'new_m', 'new_v_meta_tokens': 'new_v', 'new_v_ffn1_norm': 'new_v', 'new_v_ffn1_w_gate': 'new_v', 'new_v_ffn1_w_up': 'new_v', 'new_v_ffn1_w_down': 'new_v', 'new_v_mix_norm': 'new_v', 'new_v_w_in': 'new_v', 'new_v_b_forget': 'new_v', 'new_v_fox_q_norm': 'new_v', 'new_v_fox_k_norm': 'new_v', 'new_v_w_branch_fox': 'new_v', 'new_v_w_branch_sb': 'new_v', 'new_v_w_out': 'new_v', 'new_v_ffn2_norm': 'new_v', 'new_v_ffn2_w_gate': 'new_v', 'new_v_ffn2_w_up': 'new_v', 'new_v_ffn2_w_down': 'new_v'}


def _forward(args):
    return _fwd_reference(*[args[k] for k in FWD_PARAMS])


def _output_shape():
    def fwd():
        inp = _fwd_setup_inputs(0)
        return _fwd_reference(*[inp[k] for k in FWD_PARAMS])
    out = _jax.eval_shape(fwd)
    return out.shape, out.dtype

N_MICROBATCH = 1
ADAM_LR = 0.001
ADAM_B1 = 0.9
ADAM_B2 = 0.999
ADAM_EPS = 1e-08
ADAM_WD = 0.01
ADAM_STEP = 10
PER_EXAMPLE_BATCH_AXIS = {'x': 0, 'loss_target': 0}
SHARED_INPUTS = []
_WEIGHT_DTYPES = {'meta_tokens': _jnp.float32, 'ffn1_norm': _jnp.float32, 'ffn1_w_gate': _jnp.float32, 'ffn1_w_up': _jnp.float32, 'ffn1_w_down': _jnp.float32, 'mix_norm': _jnp.float32, 'w_in': _jnp.float32, 'b_forget': _jnp.float32, 'fox_q_norm': _jnp.float32, 'fox_k_norm': _jnp.float32, 'w_branch_fox': _jnp.float32, 'w_branch_sb': _jnp.float32, 'w_out': _jnp.float32, 'ffn2_norm': _jnp.float32, 'ffn2_w_gate': _jnp.float32, 'ffn2_w_up': _jnp.float32, 'ffn2_w_down': _jnp.float32}
MOMENT_SCALE = {'meta_tokens': 3.298795e-03, 'ffn1_norm': 3.090987e+00, 'ffn1_w_gate': 3.507630e-02, 'ffn1_w_up': 3.721023e-02, 'ffn1_w_down': 6.052969e-02, 'mix_norm': 3.302034e+00, 'w_in': 5.350524e-02, 'b_forget': 3.699731e+01, 'fox_q_norm': 4.334409e-01, 'fox_k_norm': 4.335376e-01, 'w_branch_fox': 2.934812e-02, 'w_branch_sb': 8.659090e-02, 'w_out': 8.118347e-02, 'ffn2_norm': 3.089253e+00, 'ffn2_w_gate': 2.803037e-02, 'ffn2_w_up': 3.188255e-02, 'ffn2_w_down': 5.131152e-02}


def _to_microbatches(a, axis):
    t = _jnp.moveaxis(a, axis, 0)
    t = t.reshape((N_MICROBATCH, t.shape[0] // N_MICROBATCH) + t.shape[1:])
    return _jnp.moveaxis(t, 1, axis + 1)


def setup_inputs(seed: int = 0) -> dict:
    inp = _fwd_setup_inputs(seed)
    key = _jax.random.fold_in(_jax.random.key(seed), 7919)
    shape, _ = _output_shape()
    out = dict(inp)
    out["loss_target"] = _jax.random.normal(_jax.random.fold_in(key, 0), shape, _jnp.float32)
    for i, name in enumerate(TWIN_WEIGHTS):
        w = inp[name].astype(_jnp.float32)
        if MOMENT_SCALE is None:
            s = _jnp.sqrt(_jnp.mean(_jnp.square(w)) + 1e-30)
        else:
            s = MOMENT_SCALE[name]
        km, kv = _jax.random.split(_jax.random.fold_in(key, i + 1))
        out[name] = w
        out["m_" + name] = s * _jax.random.normal(km, w.shape, _jnp.float32)
        out["v_" + name] = (s * s) * _jax.random.uniform(kv, w.shape, _jnp.float32, 0.5, 1.5)
    if N_MICROBATCH > 1:
        for name, axis in PER_EXAMPLE_BATCH_AXIS.items():
            out[name] = _to_microbatches(out[name], axis)
    return {'x': out['x'], 'meta_tokens': out['meta_tokens'], 'ffn1_norm': out['ffn1_norm'], 'ffn1_w_gate': out['ffn1_w_gate'], 'ffn1_w_up': out['ffn1_w_up'], 'ffn1_w_down': out['ffn1_w_down'], 'mix_norm': out['mix_norm'], 'w_in': out['w_in'], 'b_forget': out['b_forget'], 'fox_q_norm': out['fox_q_norm'], 'fox_k_norm': out['fox_k_norm'], 'w_branch_fox': out['w_branch_fox'], 'w_branch_sb': out['w_branch_sb'], 'w_out': out['w_out'], 'ffn2_norm': out['ffn2_norm'], 'ffn2_w_gate': out['ffn2_w_gate'], 'ffn2_w_up': out['ffn2_w_up'], 'ffn2_w_down': out['ffn2_w_down'], 'loss_target': out['loss_target'], 'm_meta_tokens': out['m_meta_tokens'], 'm_ffn1_norm': out['m_ffn1_norm'], 'm_ffn1_w_gate': out['m_ffn1_w_gate'], 'm_ffn1_w_up': out['m_ffn1_w_up'], 'm_ffn1_w_down': out['m_ffn1_w_down'], 'm_mix_norm': out['m_mix_norm'], 'm_w_in': out['m_w_in'], 'm_b_forget': out['m_b_forget'], 'm_fox_q_norm': out['m_fox_q_norm'], 'm_fox_k_norm': out['m_fox_k_norm'], 'm_w_branch_fox': out['m_w_branch_fox'], 'm_w_branch_sb': out['m_w_branch_sb'], 'm_w_out': out['m_w_out'], 'm_ffn2_norm': out['m_ffn2_norm'], 'm_ffn2_w_gate': out['m_ffn2_w_gate'], 'm_ffn2_w_up': out['m_ffn2_w_up'], 'm_ffn2_w_down': out['m_ffn2_w_down'], 'v_meta_tokens': out['v_meta_tokens'], 'v_ffn1_norm': out['v_ffn1_norm'], 'v_ffn1_w_gate': out['v_ffn1_w_gate'], 'v_ffn1_w_up': out['v_ffn1_w_up'], 'v_ffn1_w_down': out['v_ffn1_w_down'], 'v_mix_norm': out['v_mix_norm'], 'v_w_in': out['v_w_in'], 'v_b_forget': out['v_b_forget'], 'v_fox_q_norm': out['v_fox_q_norm'], 'v_fox_k_norm': out['v_fox_k_norm'], 'v_w_branch_fox': out['v_w_branch_fox'], 'v_w_branch_sb': out['v_w_branch_sb'], 'v_w_out': out['v_w_out'], 'v_ffn2_norm': out['v_ffn2_norm'], 'v_ffn2_w_gate': out['v_ffn2_w_gate'], 'v_ffn2_w_up': out['v_ffn2_w_up'], 'v_ffn2_w_down': out['v_ffn2_w_down']}


def _loss(weights, diff, rest, loss_target):
    with _jax.named_scope("forward"):
        args = {**rest, TWIN_DIFF_INPUT: diff, **{k: w.astype(_WEIGHT_DTYPES[k]) for k, w in weights.items()}}
        y = _forward(args)
    with _jax.named_scope("loss_head"):
        err = _jnp.square(y.astype(_jnp.float32) - loss_target)
        return 0.5 * _jnp.sum(_jnp.mean(err, axis=-1)) if err.ndim else 0.5 * err


def _adamw(w, g, m, v):
    m = ADAM_B1 * m + (1.0 - ADAM_B1) * g
    v = ADAM_B2 * v + (1.0 - ADAM_B2) * _jnp.square(g)
    m_hat = m / (1.0 - ADAM_B1 ** ADAM_STEP)
    v_hat = v / (1.0 - ADAM_B2 ** ADAM_STEP)
    delta = -ADAM_LR * (m_hat / (_jnp.sqrt(v_hat) + ADAM_EPS) + ADAM_WD * w)
    return delta, m, v


def reference(x, meta_tokens, ffn1_norm, ffn1_w_gate, ffn1_w_up, ffn1_w_down, mix_norm, w_in, b_forget, fox_q_norm, fox_k_norm, w_branch_fox, w_branch_sb, w_out, ffn2_norm, ffn2_w_gate, ffn2_w_up, ffn2_w_down, loss_target, m_meta_tokens, m_ffn1_norm, m_ffn1_w_gate, m_ffn1_w_up, m_ffn1_w_down, m_mix_norm, m_w_in, m_b_forget, m_fox_q_norm, m_fox_k_norm, m_w_branch_fox, m_w_branch_sb, m_w_out, m_ffn2_norm, m_ffn2_w_gate, m_ffn2_w_up, m_ffn2_w_down, v_meta_tokens, v_ffn1_norm, v_ffn1_w_gate, v_ffn1_w_up, v_ffn1_w_down, v_mix_norm, v_w_in, v_b_forget, v_fox_q_norm, v_fox_k_norm, v_w_branch_fox, v_w_branch_sb, v_w_out, v_ffn2_norm, v_ffn2_w_gate, v_ffn2_w_up, v_ffn2_w_down):
    given = dict(x=x, meta_tokens=meta_tokens, ffn1_norm=ffn1_norm, ffn1_w_gate=ffn1_w_gate, ffn1_w_up=ffn1_w_up, ffn1_w_down=ffn1_w_down, mix_norm=mix_norm, w_in=w_in, b_forget=b_forget, fox_q_norm=fox_q_norm, fox_k_norm=fox_k_norm, w_branch_fox=w_branch_fox, w_branch_sb=w_branch_sb, w_out=w_out, ffn2_norm=ffn2_norm, ffn2_w_gate=ffn2_w_gate, ffn2_w_up=ffn2_w_up, ffn2_w_down=ffn2_w_down, loss_target=loss_target, m_meta_tokens=m_meta_tokens, m_ffn1_norm=m_ffn1_norm, m_ffn1_w_gate=m_ffn1_w_gate, m_ffn1_w_up=m_ffn1_w_up, m_ffn1_w_down=m_ffn1_w_down, m_mix_norm=m_mix_norm, m_w_in=m_w_in, m_b_forget=m_b_forget, m_fox_q_norm=m_fox_q_norm, m_fox_k_norm=m_fox_k_norm, m_w_branch_fox=m_w_branch_fox, m_w_branch_sb=m_w_branch_sb, m_w_out=m_w_out, m_ffn2_norm=m_ffn2_norm, m_ffn2_w_gate=m_ffn2_w_gate, m_ffn2_w_up=m_ffn2_w_up, m_ffn2_w_down=m_ffn2_w_down, v_meta_tokens=v_meta_tokens, v_ffn1_norm=v_ffn1_norm, v_ffn1_w_gate=v_ffn1_w_gate, v_ffn1_w_up=v_ffn1_w_up, v_ffn1_w_down=v_ffn1_w_down, v_mix_norm=v_mix_norm, v_w_in=v_w_in, v_b_forget=v_b_forget, v_fox_q_norm=v_fox_q_norm, v_fox_k_norm=v_fox_k_norm, v_w_branch_fox=v_w_branch_fox, v_w_branch_sb=v_w_branch_sb, v_w_out=v_w_out, v_ffn2_norm=v_ffn2_norm, v_ffn2_w_gate=v_ffn2_w_gate, v_ffn2_w_up=v_ffn2_w_up, v_ffn2_w_down=v_ffn2_w_down)
    weights = {n: given[n] for n in TWIN_WEIGHTS}
    shared = {n: given[n] for n in SHARED_INPUTS}
    per_example = {n: given[n] for n in ['x']}
    grad_fn = _jax.value_and_grad(_loss, argnums=(0, 1))

    def one_microbatch(ex, loss_target):
        ex = dict(ex)
        diff = ex.pop(TWIN_DIFF_INPUT)
        return grad_fn(weights, diff, {**shared, **ex}, loss_target)

    if N_MICROBATCH == 1:
        loss, (grad_w, grad_x) = one_microbatch(per_example, given["loss_target"])
    else:
        def body(carry, xs):
            loss_sum, grad_sum = carry
            l_k, (gw_k, gx_k) = one_microbatch(xs[0], xs[1])
            with _jax.named_scope("update"):
                return (loss_sum + l_k, _jax.tree.map(_jnp.add, grad_sum, gw_k)), gx_k

        init = (_jnp.zeros((), _jnp.float32), _jax.tree.map(_jnp.zeros_like, weights))
        (loss, grad_w), grad_x = _jax.lax.scan(body, init, (per_example, given["loss_target"]))
    with _jax.named_scope("update"):
        delta_w, new_m, new_v = {}, {}, {}
        for n in TWIN_WEIGHTS:
            delta_w[n], new_m[n], new_v[n] = _adamw(weights[n], grad_w[n], given["m_" + n], given["v_" + n])
    return (loss, grad_x, *[grad_w[n] for n in TWIN_WEIGHTS], *[delta_w[n] for n in TWIN_WEIGHTS],
            *[new_m[n] for n in TWIN_WEIGHTS], *[new_v[n] for n in TWIN_WEIGHTS])
```

```python
import jax
import jax.numpy as jnp
from jax import lax
from jax.experimental import pallas as pl
from jax.experimental.pallas import tpu as pltpu

F32 = jnp.float32
BF16 = jnp.bfloat16
MESH = pl.DeviceIdType.MESH

NDEV = 8
N_META = 16
HEAD_DIM = 128
HEADS = 8
BRANCH_WIDTH = HEADS * HEAD_DIM
RMS_EPS = 1e-6
FFN_RESIDUAL_WEIGHT = 0.5
ATTN_SCALE = HEAD_DIM ** -0.5
MASKED_LOGIT = -1e30

ADAM_LR = 0.001
ADAM_B1 = 0.9
ADAM_B2 = 0.999
ADAM_EPS = 1e-08
ADAM_WD = 0.01
ADAM_STEP = 10

LANES = 128
PACK_WIDTH = 1024
SMALL_ROWS = 8
ADAMW_TILE_ELEMS = 160 * 1024


def _pick(n, prefs):
    for p in prefs:
        if p <= n and n % p == 0:
            return p
    return n


def _dot(a, b, dims):
    return lax.dot_general(a, b, (dims, ((), ())), preferred_element_type=F32)


NN = ((1,), (0,))
TN = ((0,), (0,))
NT = ((1,), (1,))


def _split_bf16(x):
    hi = x.astype(BF16)
    lo = (x - hi.astype(F32)).astype(BF16)
    return hi, lo


def _sigmoid_parts(z):
    e = jnp.exp(-jnp.abs(z))
    t = 1.0 + e
    return e, 1.0 / t, jnp.log(t)


def _my_place():
    return lax.axis_index("x"), lax.axis_index("y"), lax.axis_index("c")


def _flat_id(px, py, pc):
    return 4 * px + 2 * py + pc


def _peer(x, y, c, k):
    px = 1 - x if k & 4 else x
    py = 1 - y if k & 2 else y
    pc = 1 - c if k & 1 else c
    return px, py, pc


def _allgather_two_level(shard, in_vmem, name):
    rows, cols = shard.shape

    def body(x_ref, out_ref, send_sems, recv_sems, local_sem):
        x, y, c = _my_place()
        me, sibling = (x, y, c), (x, y, 1 - c)
        chips = [(1 - x, y), (x, 1 - y), (1 - x, 1 - y)]

        def block(place):
            return out_ref.at[_flat_id(*place)]

        def copy(k, place, to, src=None):
            return pltpu.make_async_remote_copy(
                src_ref=block(place) if src is None else src, dst_ref=block(place),
                send_sem=send_sems.at[k], recv_sem=recv_sems.at[k], device_id=to, device_id_type=MESH)

        mine = pltpu.make_async_copy(x_ref, block(me), local_sem)
        mine.start()
        first = [copy(0, me, sibling, src=x_ref)]
        first += [copy(1 + j, me, (*chip, c), src=x_ref) for j, chip in enumerate(chips)]
        for cp in first:
            cp.start()
        passed = [copy(4 + j, (*chip, c), sibling) for j, chip in enumerate(chips)]
        for j, chip in enumerate(chips):
            copy(1 + j, (*chip, c), me).wait_recv()
            passed[j].start()
        copy(0, sibling, me).wait_recv()
        for j, chip in enumerate(chips):
            copy(4 + j, (*chip, 1 - c), me).wait_recv()
        for cp in first + passed:
            cp.wait_send()
        mine.wait()

    space = pltpu.VMEM if in_vmem else pl.ANY
    return pl.pallas_call(
        body, name=name,
        out_shape=jax.ShapeDtypeStruct((NDEV, rows, cols), shard.dtype),
        in_specs=[pl.BlockSpec(memory_space=space)],
        out_specs=pl.BlockSpec(memory_space=space),
        scratch_shapes=[pltpu.SemaphoreType.DMA((7,)), pltpu.SemaphoreType.DMA((7,)), pltpu.SemaphoreType.DMA],
    )(shard)


def _exchange_blocks(g, name):
    _, rows, cols = g.shape

    def body(g_ref, out_ref, send_sems, recv_sems, local_sem):
        x, y, c = _my_place()
        me = _flat_id(x, y, c)
        own = pltpu.make_async_copy(g_ref.at[me], out_ref.at[me], local_sem)
        own.start()
        copies = []
        for k in range(1, NDEV):
            peer = _peer(x, y, c, k)
            copies.append(pltpu.make_async_remote_copy(
                src_ref=g_ref.at[_flat_id(*peer)], dst_ref=out_ref.at[me],
                send_sem=send_sems.at[k - 1], recv_sem=recv_sems.at[k - 1], device_id=peer, device_id_type=MESH))
        for cp in copies:
            cp.start()
        for cp in copies:
            cp.wait()
        own.wait()

    return pl.pallas_call(
        body, name=name,
        out_shape=jax.ShapeDtypeStruct(g.shape, g.dtype),
        in_specs=[pl.BlockSpec(memory_space=pl.ANY)],
        out_specs=pl.BlockSpec(memory_space=pl.ANY),
        scratch_shapes=[pltpu.SemaphoreType.DMA((7,)), pltpu.SemaphoreType.DMA((7,)), pltpu.SemaphoreType.DMA],
    )(g)


def _allsum_small(part, name):
    rows, cols = part.shape

    def body(p_ref, out_ref, buf, send_sems, recv_sems):
        x, y, c = _my_place()
        me = _flat_id(x, y, c)
        buf[me] = p_ref[...]
        copies = []
        for k in range(1, NDEV):
            copies.append(pltpu.make_async_remote_copy(
                src_ref=p_ref, dst_ref=buf.at[me], send_sem=send_sems.at[k - 1], recv_sem=recv_sems.at[k - 1],
                device_id=_peer(x, y, c, k), device_id_type=MESH))
        for cp in copies:
            cp.start()
        for cp in copies:
            cp.wait()
        total = buf[0]
        for j in range(1, NDEV):
            total = total + buf[j]
        out_ref[...] = total

    return pl.pallas_call(
        body, name=name,
        out_shape=jax.ShapeDtypeStruct((rows, cols), F32),
        in_specs=[pl.BlockSpec(memory_space=pltpu.VMEM)],
        out_specs=pl.BlockSpec(memory_space=pltpu.VMEM),
        scratch_shapes=[pltpu.VMEM((NDEV, rows, cols), F32),
                        pltpu.SemaphoreType.DMA((7,)), pltpu.SemaphoreType.DMA((7,))],
    )(part)


def _mm(a, bviews, n_cols, *, dims, tm, tn, tk, out_dtypes, epi=None, extras=(), name):
    m_rows, k_len = (a.shape[1], a.shape[0]) if dims == TN else a.shape
    assert m_rows % tm == 0 and n_cols % tn == 0 and k_len % tk == 0, (name, a.shape, n_cols, tm, tn, tk)
    nk = k_len // tk
    nb, ne = len(bviews), len(extras)
    if epi is None:
        epi = lambda accs, ex: [accs[0]]

    def body(*refs):
        a_ref, b_refs = refs[0], refs[1:1 + nb]
        e_refs = refs[1 + nb:1 + nb + ne]
        o_refs = refs[1 + nb + ne:1 + nb + ne + len(out_dtypes)]
        acc_refs = refs[1 + nb + ne + len(out_dtypes):]
        av = a_ref[...]
        parts = [_dot(av, b_ref[...], dims) for b_ref in b_refs]

        def finish(accs):
            for o_ref, tile in zip(o_refs, epi(accs, [e_ref[...] for e_ref in e_refs])):
                o_ref[...] = tile.astype(o_ref.dtype)

        if nk == 1:
            finish(parts)
        else:
            k = pl.program_id(2)

            @pl.when(k == 0)
            def _():
                for acc_ref, part in zip(acc_refs, parts):
                    acc_ref[...] = part

            @pl.when(k > 0)
            def _():
                for acc_ref, part in zip(acc_refs, parts):
                    acc_ref[...] += part

            @pl.when(k == nk - 1)
            def _():
                finish([acc_ref[...] for acc_ref in acc_refs])

    a_spec = pl.BlockSpec((tk, tm), lambda i, j, k: (k, i)) if dims == TN else pl.BlockSpec((tm, tk), lambda i, j, k: (i, k))

    def b_spec(n_off, k_off):
        if dims == NT:
            return pl.BlockSpec((tn, tk), lambda i, j, k: (j + n_off, k + k_off))
        return pl.BlockSpec((tk, tn), lambda i, j, k: (k + k_off, j + n_off))

    def tile_spec(off):
        return pl.BlockSpec((tm, tn), lambda i, j, k: (i, j + off))

    return pl.pallas_call(
        body, name=name,
        grid=(m_rows // tm, n_cols // tn, nk),
        in_specs=[a_spec] + [b_spec(n_off, k_off) for _, n_off, k_off in bviews] + [tile_spec(off) for _, off in extras],
        out_specs=[tile_spec(0) for _ in out_dtypes],
        out_shape=[jax.ShapeDtypeStruct((m_rows, n_cols), dt) for dt in out_dtypes],
        scratch_shapes=[pltpu.VMEM((tm, tn), F32) for _ in bviews] if nk > 1 else [],
        compiler_params=pltpu.CompilerParams(dimension_semantics=("parallel", "parallel", "arbitrary")),
    )(a, *[b for b, _, _ in bviews], *[e for e, _ in extras])


def _rowwise(fn, ins, consts, outs, sums, *, tm, name, row_maps=None):
    m_rows = outs[0][2] if len(outs[0]) == 3 else ins[0][0].shape[0]
    n = m_rows // tm
    ni, nc, no = len(ins), len(consts), len(outs)
    row_maps = row_maps or [None] * ni

    def body(*refs):
        i = pl.program_id(0)
        in_tiles = [r[...] for r in refs[:ni]]
        const_values = [r[...] for r in refs[ni:ni + nc]]
        o_refs = refs[ni + nc:ni + nc + no]
        s_refs = refs[ni + nc + no:]
        out_tiles, sum_terms = fn(i, in_tiles, const_values)
        for o_ref, tile in zip(o_refs, out_tiles):
            o_ref[...] = tile.astype(o_ref.dtype)
        if s_refs:
            @pl.when(i == 0)
            def _():
                for s_ref in s_refs:
                    s_ref[...] = jnp.zeros_like(s_ref)

            for s_ref, term in zip(s_refs, sum_terms):
                s_ref[...] += term

    def in_spec(width, col, rmap):
        if rmap is None:
            return pl.BlockSpec((tm, width), lambda i: (i, col))
        return pl.BlockSpec((tm, width), lambda i: (rmap(i), col))

    return pl.pallas_call(
        body, name=name,
        grid=(n,),
        in_specs=[in_spec(w, col, rmap) for (_, w, col), rmap in zip(ins, row_maps)]
        + [pl.BlockSpec(cst.shape, lambda i: (0, 0)) for cst in consts],
        out_specs=[pl.BlockSpec((tm, o[0]), lambda i: (i, 0)) for o in outs]
        + [pl.BlockSpec(s, lambda i: (0, 0)) for s in sums],
        out_shape=[jax.ShapeDtypeStruct((m_rows, o[0]), o[1]) for o in outs]
        + [jax.ShapeDtypeStruct(s, F32) for s in sums],
        compiler_params=pltpu.CompilerParams(dimension_semantics=("arbitrary",)),
    )(*[arr for arr, _, _ in ins], *consts)


def _whole(arr):
    return (arr, arr.shape[1], 0)


def _rms(x, gain):
    r = lax.rsqrt(jnp.mean(x * x, axis=-1, keepdims=True) + RMS_EPS)
    return x * r * gain


def _rms_bwd(x, gain, dy):
    r = lax.rsqrt(jnp.mean(x * x, axis=-1, keepdims=True) + RMS_EPS)
    u = dy * gain
    dx = r * u - x * (r * r * r) * jnp.mean(x * u, axis=-1, keepdims=True)
    return dx, dy * x * r


def _norm_fwd(h, gain, *, tm, name):
    def fn(i, tiles, consts):
        return [_rms(tiles[0], consts[0])], []
    return _rowwise(fn, [_whole(h)], [gain], [(h.shape[1], BF16)], [], tm=tm, name=name)[0]


def _norm_bwd(dn, h, gain, dh_in, *, tm, name):
    d = h.shape[1]

    def fn(i, tiles, consts):
        dx, dg_rows = _rms_bwd(tiles[1], consts[0], tiles[0])
        dh = tiles[2] + dx
        return [dh, dh, FFN_RESIDUAL_WEIGHT * dh], [jnp.sum(dg_rows, axis=0, keepdims=True)]

    return _rowwise(fn, [_whole(dn), _whole(h), _whole(dh_in)], [gain],
                    [(d, F32), (d, BF16), (d, BF16)], [(1, d)], tm=tm, name=name)


def _sigmoid(x):
    e = jnp.exp(-jnp.abs(x))
    r = 1.0 / (1.0 + e)
    return jnp.where(x >= 0, r, e * r)


def _swiglu_epi(accs, ex):
    a, b = accs
    return [a, b, a * _sigmoid(a) * b]


def _swiglu_bwd_epi(accs, ex):
    ds = accs[0]
    a, b = ex[0].astype(F32), ex[1].astype(F32)
    sig = _sigmoid(a)
    silu = a * sig
    dsilu = sig * (1.0 + a * (1.0 - sig))
    return [ds * b * dsilu, ds * silu]


def _attn_mask(i, j, tile, pad, strict):
    row = i * tile + lax.broadcasted_iota(jnp.int32, (tile, tile), 0)
    col = j * tile + lax.broadcasted_iota(jnp.int32, (tile, tile), 1)
    causal = (col < row) if strict else (col <= row)
    return causal & ((col >= pad) | (row < pad))


def _key_tile(ref, j, tile):
    return ref[pl.ds(pl.multiple_of(j * tile, tile), tile), :]


def _head_specs(lp, tile, q_off, k_off, v_off):
    q_spec = pl.BlockSpec((tile, HEAD_DIM), lambda h, i: (i, h + q_off))
    k_spec = pl.BlockSpec((lp, HEAD_DIM), lambda h, i: (0, h + k_off))
    v_spec = pl.BlockSpec((lp, HEAD_DIM), lambda h, i: (0, h + v_off))
    return q_spec, k_spec, v_spec


def _fox_fwd(q, k, v, v_off, c_rows, *, tile, pad, name):
    lp = q.shape[0]
    nb = lp // tile

    def body(q_ref, k_ref, v_ref, c_ref, o_ref, lse_ref):
        i = pl.program_id(1)
        qt = q_ref[...]

        def step(j, carry):
            m, l, acc = carry
            s = _dot(qt, _key_tile(k_ref, j, tile), NT) * ATTN_SCALE - c_ref[0, j]
            s = jnp.where(_attn_mask(i, j, tile, pad, False), s, MASKED_LOGIT)
            m_new = jnp.maximum(m, jnp.max(s, axis=1, keepdims=True))
            p = jnp.exp(s - m_new)
            alpha = jnp.exp(m - m_new)
            l = alpha * l + jnp.sum(p, axis=1, keepdims=True)
            acc = alpha * acc + _dot(p.astype(BF16), _key_tile(v_ref, j, tile), NN)
            return m_new, l, acc

        init = (jnp.full((tile, 1), MASKED_LOGIT, F32), jnp.zeros((tile, 1), F32), jnp.zeros((tile, HEAD_DIM), F32))
        m, l, acc = lax.fori_loop(0, i + 1, step, init)
        o_ref[...] = (acc / l).astype(o_ref.dtype)
        lse_ref[0] = jnp.broadcast_to(m + jnp.log(l), (tile, LANES))

    q_spec, k_spec, v_spec = _head_specs(lp, tile, 0, 0, v_off)
    return pl.pallas_call(
        body, name=name,
        grid=(HEADS, nb),
        in_specs=[q_spec, k_spec, v_spec, pl.BlockSpec((1, nb, 1, tile), lambda h, i: (h, 0, 0, 0))],
        out_specs=[pl.BlockSpec((tile, HEAD_DIM), lambda h, i: (i, h)),
                   pl.BlockSpec((1, tile, LANES), lambda h, i: (h, i, 0))],
        out_shape=[jax.ShapeDtypeStruct((lp, BRANCH_WIDTH), BF16), jax.ShapeDtypeStruct((HEADS, lp, LANES), F32)],
        compiler_params=pltpu.CompilerParams(dimension_semantics=("parallel", "arbitrary")),
    )(q, k, v, c_rows)


def _fox_bwd(q, k, v, v_off, c_rows, o, do, lse, *, tile, pad, name):
    lp = q.shape[0]
    nb = lp // tile

    def body(q_ref, k_ref, v_ref, c_ref, o_ref, do_ref, lse_ref, dq_ref, dk_ref, dv_ref, dc_ref, dk_acc, dv_acc, dc_acc):
        i = pl.program_id(1)

        @pl.when(i == 0)
        def _():
            dk_acc[...] = jnp.zeros_like(dk_acc)
            dv_acc[...] = jnp.zeros_like(dv_acc)
            dc_acc[...] = jnp.zeros_like(dc_acc)

        qt, dot_ = q_ref[...], do_ref[...]
        delta = jnp.sum(dot_.astype(F32) * o_ref[...].astype(F32), axis=1, keepdims=True)
        lse_col = lse_ref[0][:, :1]

        def step(j, dq):
            kt = _key_tile(k_ref, j, tile)
            rows = pl.ds(pl.multiple_of(j * tile, tile), tile)
            s = _dot(qt, kt, NT) * ATTN_SCALE - c_ref[0, j]
            p = jnp.where(_attn_mask(i, j, tile, pad, False), jnp.exp(s - lse_col), 0.0)
            dp = _dot(dot_, _key_tile(v_ref, j, tile), NT)
            ds = p * (dp - delta)
            dsb = ds.astype(BF16)
            dk_acc[rows, :] += _dot(dsb, qt, TN)
            dv_acc[rows, :] += _dot(p.astype(BF16), dot_, TN)
            dc_acc[j] += -jnp.sum(ds, axis=0, keepdims=True)
            return dq + _dot(dsb, kt, NN)

        dq = lax.fori_loop(0, i + 1, step, jnp.zeros((tile, HEAD_DIM), F32))
        dq_ref[...] = dq * ATTN_SCALE

        @pl.when(i == nb - 1)
        def _():
            dk_ref[...] = dk_acc[...] * ATTN_SCALE
            dv_ref[...] = dv_acc[...].astype(dv_ref.dtype)
            dc_ref[0] = dc_acc[...]

    q_spec, k_spec, v_spec = _head_specs(lp, tile, 0, 0, v_off)
    tile_spec = pl.BlockSpec((tile, HEAD_DIM), lambda h, i: (i, h))
    head_spec = pl.BlockSpec((lp, HEAD_DIM), lambda h, i: (0, h))
    c_spec = pl.BlockSpec((1, nb, 1, tile), lambda h, i: (h, 0, 0, 0))
    return pl.pallas_call(
        body, name=name,
        grid=(HEADS, nb),
        in_specs=[q_spec, k_spec, v_spec, c_spec, tile_spec, tile_spec,
                  pl.BlockSpec((1, tile, LANES), lambda h, i: (h, i, 0))],
        out_specs=[tile_spec, head_spec, head_spec, c_spec],
        out_shape=[jax.ShapeDtypeStruct((lp, BRANCH_WIDTH), F32), jax.ShapeDtypeStruct((lp, BRANCH_WIDTH), F32),
                   jax.ShapeDtypeStruct((lp, BRANCH_WIDTH), BF16), jax.ShapeDtypeStruct((HEADS, nb, 1, tile), F32)],
        scratch_shapes=[pltpu.VMEM((lp, HEAD_DIM), F32), pltpu.VMEM((lp, HEAD_DIM), F32), pltpu.VMEM((nb, 1, tile), F32)],
        compiler_params=pltpu.CompilerParams(dimension_semantics=("parallel", "arbitrary")),
    )(q, k, v, c_rows, o, do, lse)


def _later_matrix(tile):
    return (lax.broadcasted_iota(jnp.int32, (tile, tile), 0) > lax.broadcasted_iota(jnp.int32, (tile, tile), 1)).astype(BF16)


def _earlier_matrix(tile):
    return (lax.broadcasted_iota(jnp.int32, (tile, tile), 0) < lax.broadcasted_iota(jnp.int32, (tile, tile), 1)).astype(BF16)


def _sb_tile(qt, kt, ok, later):
    z = _dot(qt, kt, NT) * ATTN_SCALE
    e, r, lg = _sigmoid_parts(z)
    sp = jnp.maximum(z, 0.0) + lg
    spm = jnp.where(ok, sp, 0.0)
    hi, lo = _split_bf16(spm)
    within = _dot(hi, later, NN) + _dot(lo, later, NN)
    return z, e, r, sp, spm, within


def _sb_fwd(qkv, q_off, k_off, v_off, *, tile, pad, name):
    lp = qkv.shape[0]
    nb = lp // tile

    def body(q_ref, k_ref, v_ref, o_ref, tot_ref):
        i = pl.program_id(1)
        qt = q_ref[...]
        later = _later_matrix(tile)

        def step(t, carry):
            right, acc = carry
            j = i - t
            ok = _attn_mask(i, j, tile, pad, True)
            z, _, _, sp, spm, within = _sb_tile(qt, _key_tile(k_ref, j, tile), ok, later)
            w = jnp.where(ok, jnp.exp(z - sp - within - right), 0.0)
            acc = acc + _dot(w.astype(BF16), _key_tile(v_ref, j, tile), NN)
            return right + jnp.sum(spm, axis=1, keepdims=True), acc

        total, acc = lax.fori_loop(0, i + 1, step, (jnp.zeros((tile, 1), F32), jnp.zeros((tile, HEAD_DIM), F32)))
        o_ref[...] = acc.astype(o_ref.dtype)
        tot_ref[0] = jnp.broadcast_to(total, (tile, LANES))

    q_spec, k_spec, v_spec = _head_specs(lp, tile, q_off, k_off, v_off)
    return pl.pallas_call(
        body, name=name,
        grid=(HEADS, nb),
        in_specs=[q_spec, k_spec, v_spec],
        out_specs=[pl.BlockSpec((tile, HEAD_DIM), lambda h, i: (i, h)),
                   pl.BlockSpec((1, tile, LANES), lambda h, i: (h, i, 0))],
        out_shape=[jax.ShapeDtypeStruct((lp, BRANCH_WIDTH), BF16), jax.ShapeDtypeStruct((HEADS, lp, LANES), F32)],
        compiler_params=pltpu.CompilerParams(dimension_semantics=("parallel", "arbitrary")),
    )(qkv, qkv, qkv)


def _sb_bwd(qkv, q_off, k_off, v_off, do, total, *, tile, pad, name):
    lp = qkv.shape[0]
    nb = lp // tile

    def body(q_ref, k_ref, v_ref, do_ref, tot_ref, dq_ref, dk_ref, dv_ref, dk_acc, dv_acc):
        i = pl.program_id(1)

        @pl.when(i == 0)
        def _():
            dk_acc[...] = jnp.zeros_like(dk_acc)
            dv_acc[...] = jnp.zeros_like(dv_acc)

        qt, dot_ = q_ref[...], do_ref[...]
        total_col = tot_ref[0][:, :1]
        later, earlier = _later_matrix(tile), _earlier_matrix(tile)

        def step(j, carry):
            dq, sp_before, dlw_before = carry
            kt = _key_tile(k_ref, j, tile)
            rows = pl.ds(pl.multiple_of(j * tile, tile), tile)
            ok = _attn_mask(i, j, tile, pad, True)
            z, e, r, sp, spm, within = _sb_tile(qt, kt, ok, later)
            sp_here = jnp.sum(spm, axis=1, keepdims=True)
            right = total_col - sp_before - sp_here
            w = jnp.where(ok, jnp.exp(z - sp - within - right), 0.0)
            dlw = w * _dot(dot_, _key_tile(v_ref, j, tile), NT)
            hi, lo = _split_bf16(dlw)
            before = dlw_before + _dot(hi, earlier, NN) + _dot(lo, earlier, NN)
            sig = jnp.where(z >= 0, r, e * r)
            dz = jnp.where(ok, dlw * (1.0 - sig) - sig * before, 0.0)
            dzb = dz.astype(BF16)
            dk_acc[rows, :] += _dot(dzb, qt, TN)
            dv_acc[rows, :] += _dot(w.astype(BF16), dot_, TN)
            return dq + _dot(dzb, kt, NN), sp_before + sp_here, dlw_before + jnp.sum(dlw, axis=1, keepdims=True)

        zero_col = jnp.zeros((tile, 1), F32)
        dq, _, _ = lax.fori_loop(0, i + 1, step, (jnp.zeros((tile, HEAD_DIM), F32), zero_col, zero_col))
        dq_ref[...] = (dq * ATTN_SCALE).astype(dq_ref.dtype)

        @pl.when(i == nb - 1)
        def _():
            dk_ref[...] = (dk_acc[...] * ATTN_SCALE).astype(dk_ref.dtype)
            dv_ref[...] = dv_acc[...].astype(dv_ref.dtype)

    q_spec, k_spec, v_spec = _head_specs(lp, tile, q_off, k_off, v_off)
    tile_spec = pl.BlockSpec((tile, HEAD_DIM), lambda h, i: (i, h))
    head_spec = pl.BlockSpec((lp, HEAD_DIM), lambda h, i: (0, h))
    return pl.pallas_call(
        body, name=name,
        grid=(HEADS, nb),
        in_specs=[q_spec, k_spec, v_spec, tile_spec, pl.BlockSpec((1, tile, LANES), lambda h, i: (h, i, 0))],
        out_specs=[tile_spec, head_spec, head_spec],
        out_shape=[jax.ShapeDtypeStruct((lp, BRANCH_WIDTH), BF16)] * 3,
        scratch_shapes=[pltpu.VMEM((lp, HEAD_DIM), F32), pltpu.VMEM((lp, HEAD_DIM), F32)],
        compiler_params=pltpu.CompilerParams(dimension_semantics=("parallel", "arbitrary")),
    )(qkv, qkv, qkv, do, total)


def _cumsum_rows(x, *, tile, reverse, name):
    lp = x.shape[0]
    nb = lp // tile

    def body(x_ref, o_ref, carry):
        @pl.when(pl.program_id(0) == 0)
        def _():
            carry[...] = jnp.zeros_like(carry)

        r = lax.broadcasted_iota(jnp.int32, (tile, tile), 0)
        c = lax.broadcasted_iota(jnp.int32, (tile, tile), 1)
        tri = ((c >= r) if reverse else (c <= r)).astype(BF16)
        hi, lo = _split_bf16(x_ref[...])
        run = _dot(tri, hi, NN) + _dot(tri, lo, NN) + carry[...]
        o_ref[...] = run
        carry[...] = run[:1, :] if reverse else run[tile - 1:, :]

    order = (lambda i: (nb - 1 - i, 0)) if reverse else (lambda i: (i, 0))
    return pl.pallas_call(
        body, name=name,
        grid=(nb,),
        in_specs=[pl.BlockSpec((tile, LANES), order)],
        out_specs=pl.BlockSpec((tile, LANES), order),
        out_shape=jax.ShapeDtypeStruct((lp, LANES), F32),
        scratch_shapes=[pltpu.VMEM((1, LANES), F32)],
        compiler_params=pltpu.CompilerParams(dimension_semantics=("arbitrary",)),
    )(x)


def _log_sigmoid(x):
    return jnp.minimum(x, 0.0) - jnp.log(1.0 + jnp.exp(-jnp.abs(x)))


def _forget_mask(i, tm, pad):
    row = i * tm + lax.broadcasted_iota(jnp.int32, (tm, LANES), 0)
    lane = lax.broadcasted_iota(jnp.int32, (tm, LANES), 1)
    return (row >= pad) & (lane < HEADS)


def _fox_prep(proj_a, q_gain, k_gain, b_forget, *, tm, pad, name):
    w = BRANCH_WIDTH

    def fn(i, tiles, consts):
        pa = tiles[0]
        qs, ks = [], []
        for h in range(HEADS):
            lo = h * HEAD_DIM
            qs.append(_rms(pa[:, lo:lo + HEAD_DIM], consts[0][:, lo:lo + HEAD_DIM]))
            ks.append(_rms(pa[:, w + lo:w + lo + HEAD_DIM], consts[1][:, lo:lo + HEAD_DIM]))
        logf = jnp.where(_forget_mask(i, tm, pad), _log_sigmoid(pa[:, 2 * w:] + consts[2]), 0.0)
        return [jnp.concatenate(qs, axis=1), jnp.concatenate(ks, axis=1), logf], []

    return _rowwise(fn, [_whole(proj_a)], [q_gain, k_gain, b_forget],
                    [(w, BF16), (w, BF16), (LANES, F32)], [], tm=tm, name=name)


def _fox_prep_bwd(proj_a, dq, dk, dlogf, q_gain, k_gain, b_forget, *, tm, pad, name):
    w = BRANCH_WIDTH

    def fn(i, tiles, consts):
        pa, dqt, dkt, dlf = tiles
        dxs_q, dxs_k, dgs_q, dgs_k = [], [], [], []
        for h in range(HEADS):
            lo = h * HEAD_DIM
            dx, dg = _rms_bwd(pa[:, lo:lo + HEAD_DIM], consts[0][:, lo:lo + HEAD_DIM], dqt[:, lo:lo + HEAD_DIM])
            dxs_q.append(dx)
            dgs_q.append(jnp.sum(dg, axis=0, keepdims=True))
            dx, dg = _rms_bwd(pa[:, w + lo:w + lo + HEAD_DIM], consts[1][:, lo:lo + HEAD_DIM], dkt[:, lo:lo + HEAD_DIM])
            dxs_k.append(dx)
            dgs_k.append(jnp.sum(dg, axis=0, keepdims=True))
        xf = pa[:, 2 * w:] + consts[2]
        e, r, _ = _sigmoid_parts(xf)
        df = jnp.where(_forget_mask(i, tm, pad), dlf * jnp.where(xf >= 0, e * r, r), 0.0)
        return ([jnp.concatenate(dxs_q + dxs_k + [df], axis=1)],
                [jnp.concatenate(dgs_q, axis=1), jnp.concatenate(dgs_k, axis=1), jnp.sum(df, axis=0, keepdims=True)])

    return _rowwise(fn, [_whole(proj_a), _whole(dq), _whole(dk), _whole(dlogf)], [q_gain, k_gain, b_forget],
                    [(2 * w + LANES, BF16)], [(1, w), (1, w), (1, LANES)], tm=tm, name=name)


def _adamw_math(w, g, m, v):
    m = ADAM_B1 * m + (1.0 - ADAM_B1) * g
    v = ADAM_B2 * v + (1.0 - ADAM_B2) * (g * g)
    m_hat = m / (1.0 - ADAM_B1 ** ADAM_STEP)
    v_hat = v / (1.0 - ADAM_B2 ** ADAM_STEP)
    delta = -ADAM_LR * (m_hat / (jnp.sqrt(v_hat) + ADAM_EPS) + ADAM_WD * w)
    return delta, m, v


def _adamw_summed(parts, w, m, v, *, name):
    rows, cols = w.shape
    tr = _pick(rows, [t for t in (512, 256, 128, 64, 32, 16, 8) if t * cols <= ADAMW_TILE_ELEMS])

    def body(p_ref, w_ref, m_ref, v_ref, g_out, d_out, m_out, v_out):
        g = p_ref[0].astype(F32)
        for j in range(1, NDEV):
            g = g + p_ref[j].astype(F32)
        delta, m_new, v_new = _adamw_math(w_ref[...], g, m_ref[...], v_ref[...])
        g_out[...] = g
        d_out[...] = delta
        m_out[...] = m_new
        v_out[...] = v_new

    spec = pl.BlockSpec((tr, cols), lambda i: (i, 0))
    return pl.pallas_call(
        body, name=name,
        grid=(rows // tr,),
        in_specs=[pl.BlockSpec((NDEV, tr, cols), lambda i: (0, i, 0)), spec, spec, spec],
        out_specs=[spec] * 4,
        out_shape=[jax.ShapeDtypeStruct((rows, cols), F32)] * 4,
        compiler_params=pltpu.CompilerParams(dimension_semantics=("parallel",)),
    )(parts, w, m, v)


def _adamw_plain(g, w, m, v, *, name):
    def body(g_ref, w_ref, m_ref, v_ref, d_out, m_out, v_out):
        delta, m_new, v_new = _adamw_math(w_ref[...], g_ref[...], m_ref[...], v_ref[...])
        d_out[...] = delta
        m_out[...] = m_new
        v_out[...] = v_new

    return pl.pallas_call(body, name=name, out_shape=[jax.ShapeDtypeStruct(w.shape, F32)] * 3)(g, w, m, v)


def _pack_rows(arrays, width, row_align):
    pieces, spans, at = [], [], 0
    for arr in arrays:
        flat = arr.reshape(-1)
        rows = -(-flat.shape[0] // (width * row_align)) * row_align
        flat = jnp.pad(flat, (0, rows * width - flat.shape[0]))
        pieces.append(flat.reshape(rows, width))
        spans.append((at, rows))
        at += rows
    return jnp.concatenate(pieces, axis=0), spans


def _unpack(rows2d, span, shape):
    at, rows = span
    size = 1
    for s in shape:
        size *= s
    return rows2d[at:at + rows].reshape(-1)[:size].reshape(shape)


def _unpack_blocks(blocks, span, shape):
    at, rows = span
    size = 1
    for s in shape:
        size *= s
    return blocks[:, at:at + rows].reshape(NDEV, -1)[:, :size].reshape((NDEV,) + tuple(shape))


def _join_cols(blocks):
    n, rows, cols = blocks.shape
    return jnp.transpose(blocks, (1, 0, 2)).reshape(rows, n * cols)


def _split_cols(full):
    rows, cols = full.shape
    return jnp.transpose(full.reshape(rows, NDEV, cols // NDEV), (1, 0, 2))


def _ffn_fwd(h, gain, w_gu, w_down, d_ff, *, tm, tag):
    lp, d = h.shape
    n = _norm_fwd(h, gain, tm=_pick(lp, [256, 128]), name=f"{tag}_norm")
    tn = _pick(d_ff, [512, 256, 128])
    a, b, s = _mm(n, [(w_gu, 0, 0), (w_gu, d_ff // tn, 0)], d_ff, dims=NN, tm=tm, tn=tn, tk=d,
                  out_dtypes=[BF16, BF16, BF16], epi=_swiglu_epi, name=f"{tag}_up")
    h_out, = _mm(s, [(w_down, 0, 0)], d, dims=NN, tm=tm, tn=_pick(d, [1024, 512, 256, 128]), tk=_pick(d_ff, [512, 256, 128]),
                 out_dtypes=[F32], epi=lambda accs, ex: [ex[0] + FFN_RESIDUAL_WEIGHT * accs[0]], extras=[(h, 0)],
                 name=f"{tag}_down")
    return h_out, (n, a, b, s)


def _ffn_bwd(dh, dh_half, h_in, gain, saved, w_gu, w_down, d_ff, *, tm, tag):
    n, a, b, s = saved
    lp, d = h_in.shape
    tn = _pick(d_ff, [512, 256, 128])
    tk_rows = _pick(lp, [1088, 544, 256, 128])
    da, db = _mm(dh_half, [(w_down, 0, 0)], d_ff, dims=NT, tm=tm, tn=tn, tk=d, out_dtypes=[BF16, BF16],
                 epi=_swiglu_bwd_epi, extras=[(a, 0), (b, 0)], name=f"{tag}_down_dx")
    dw_down, = _mm(s, [(dh_half, 0, 0)], d, dims=TN, tm=_pick(d_ff, [512, 256, 128]), tn=_pick(d, [1024, 512, 256, 128]),
                   tk=tk_rows, out_dtypes=[BF16], name=f"{tag}_down_dw")
    tmw = _pick(d, [1024, 512, 256, 128])
    dw_gate, = _mm(n, [(da, 0, 0)], d_ff, dims=TN, tm=tmw, tn=tn, tk=tk_rows, out_dtypes=[BF16], name=f"{tag}_gate_dw")
    dw_up, = _mm(n, [(db, 0, 0)], d_ff, dims=TN, tm=tmw, tn=tn, tk=tk_rows, out_dtypes=[BF16], name=f"{tag}_up_dw")
    tkf = _pick(d_ff, [512, 256, 128])
    tnd = _pick(d, [1024, 512, 256, 128])
    dn_gate, = _mm(da, [(w_gu, 0, 0)], d, dims=NT, tm=tm, tn=tnd, tk=tkf, out_dtypes=[F32], name=f"{tag}_gate_dx")
    dn, = _mm(db, [(w_gu, 0, d_ff // tkf)], d, dims=NT, tm=tm, tn=tnd, tk=tkf, out_dtypes=[F32],
              epi=lambda accs, ex: [accs[0] + ex[0]], extras=[(dn_gate, 0)], name=f"{tag}_up_dx")
    dh_in, dh_in_bf, dh_in_half, dgain = _norm_bwd(dn, h_in, gain, dh, tm=_pick(lp, [256, 128]), name=f"{tag}_norm_bwd")
    return dh_in, dh_in_bf, dh_in_half, dgain, dw_gate, dw_up, dw_down


def kernel(x, meta_tokens, ffn1_norm, ffn1_w_gate, ffn1_w_up, ffn1_w_down, mix_norm, w_in, b_forget, fox_q_norm, fox_k_norm, w_branch_fox, w_branch_sb, w_out, ffn2_norm, ffn2_w_gate, ffn2_w_up, ffn2_w_down, loss_target, m_meta_tokens, m_ffn1_norm, m_ffn1_w_gate, m_ffn1_w_up, m_ffn1_w_down, m_mix_norm, m_w_in, m_b_forget, m_fox_q_norm, m_fox_k_norm, m_w_branch_fox, m_w_branch_sb, m_w_out, m_ffn2_norm, m_ffn2_w_gate, m_ffn2_w_up, m_ffn2_w_down, v_meta_tokens, v_ffn1_norm, v_ffn1_w_gate, v_ffn1_w_up, v_ffn1_w_down, v_mix_norm, v_w_in, v_b_forget, v_fox_q_norm, v_fox_k_norm, v_w_branch_fox, v_w_branch_sb, v_w_out, v_ffn2_norm, v_ffn2_w_gate, v_ffn2_w_up, v_ffn2_w_down):
    seq, d = x.shape[1], x.shape[2]
    d_ff = ffn1_w_gate.shape[2] * NDEV
    w = BRANCH_WIDTH
    tile = 256 if seq % 256 == 0 else 128
    pad = tile - N_META
    lp = tile + seq
    tm = _pick(lp, [1088, 544, 256, 128])
    tr = _pick(lp, [256, 128])
    nb = lp // tile
    me = _flat_id(*_my_place())

    big = dict(ffn1_w_gate=ffn1_w_gate, ffn1_w_up=ffn1_w_up, ffn1_w_down=ffn1_w_down, w_in=w_in,
               w_branch_fox=w_branch_fox, w_branch_sb=w_branch_sb, w_out=w_out,
               ffn2_w_gate=ffn2_w_gate, ffn2_w_up=ffn2_w_up, ffn2_w_down=ffn2_w_down)
    big_m = dict(ffn1_w_gate=m_ffn1_w_gate, ffn1_w_up=m_ffn1_w_up, ffn1_w_down=m_ffn1_w_down, w_in=m_w_in,
                 w_branch_fox=m_w_branch_fox, w_branch_sb=m_w_branch_sb, w_out=m_w_out,
                 ffn2_w_gate=m_ffn2_w_gate, ffn2_w_up=m_ffn2_w_up, ffn2_w_down=m_ffn2_w_down)
    big_v = dict(ffn1_w_gate=v_ffn1_w_gate, ffn1_w_up=v_ffn1_w_up, ffn1_w_down=v_ffn1_w_down, w_in=v_w_in,
                 w_branch_fox=v_w_branch_fox, w_branch_sb=v_w_branch_sb, w_out=v_w_out,
                 ffn2_w_gate=v_ffn2_w_gate, ffn2_w_up=v_ffn2_w_up, ffn2_w_down=v_ffn2_w_down)
    row_cut = ("ffn1_w_down", "w_out", "ffn2_w_down")
    names = list(big)
    shard_shapes = {k: big[k].shape[1:] for k in names}
    packed, spans = _pack_rows([big[k][0].astype(BF16) for k in names], PACK_WIDTH, 1)
    gathered = _allgather_two_level(packed, False, "gather_weights")
    full = {}
    for k, span in zip(names, spans):
        blocks = _unpack_blocks(gathered, span, shard_shapes[k])
        full[k] = blocks.reshape(-1, blocks.shape[2]) if k in row_cut else _join_cols(blocks)

    meta_full = _join_cols(_allgather_two_level(meta_tokens, True, "gather_meta").reshape(NDEV, N_META, d // NDEV))

    w_gu1 = jnp.concatenate([full["ffn1_w_gate"], full["ffn1_w_up"]], axis=1)
    w_gu2 = jnp.concatenate([full["ffn2_w_gate"], full["ffn2_w_up"]], axis=1)
    wi = full["w_in"]
    w_pa = jnp.concatenate([wi[:, :2 * w], jnp.pad(wi[:, 3 * w:3 * w + HEADS], ((0, 0), (0, LANES - HEADS)))], axis=1)
    w_pb = jnp.concatenate([wi[:, 2 * w:3 * w], wi[:, 3 * w + HEADS:]], axis=1)
    na, nbw = w_pa.shape[1], w_pb.shape[1]
    gate_blk = 4 * w // d

    h0 = jnp.concatenate([jnp.zeros((pad, d), F32), meta_full.astype(F32), x[0]], axis=0)
    h1, saved1 = _ffn_fwd(h0, ffn1_norm, w_gu1, full["ffn1_w_down"], d_ff, tm=tm, tag="ffn1")

    n2 = _norm_fwd(h1, mix_norm, tm=tr, name="mix_norm")
    tma = _pick(lp, [544, 256, 128])
    proj_a, = _mm(n2, [(w_pa, 0, 0)], na, dims=NN, tm=tma, tn=na, tk=d, out_dtypes=[F32], name="proj_a")
    proj_b, = _mm(n2, [(w_pb, 0, 0)], nbw, dims=NN, tm=tm, tn=_pick(nbw, [512, 256, 128]), tk=d, out_dtypes=[BF16], name="proj_b")
    b_pad = jnp.pad(b_forget, ((0, 0), (0, LANES - HEADS)))
    q_gain, k_gain = fox_q_norm.reshape(1, w), fox_k_norm.reshape(1, w)
    fq, fk, logf = _fox_prep(proj_a, q_gain, k_gain, b_pad, tm=tr, pad=pad, name="fox_prep")
    c = _cumsum_rows(logf, tile=tile, reverse=False, name="forget_cumsum")
    c_rows = jnp.transpose(c[:, :HEADS]).reshape(HEADS, nb, 1, tile)
    o_fox, lse = _fox_fwd(fq, fk, proj_b, 0, c_rows, tile=tile, pad=pad, name="fox_fwd")
    o_sb, sb_total = _sb_fwd(proj_b, HEADS, 2 * HEADS, 3 * HEADS, tile=tile, pad=pad, name="sb_fwd")

    tnd = _pick(d, [1024, 512, 256, 128])
    tkw = _pick(w, [512, 256, 128])
    br_fox, = _mm(o_fox, [(full["w_branch_fox"], 0, 0)], d, dims=NN, tm=tm, tn=tnd, tk=w, out_dtypes=[BF16], name="branch_fox")
    br_sb, = _mm(o_sb, [(full["w_branch_sb"], 0, 0)], d, dims=NN, tm=tm, tn=tnd, tk=w, out_dtypes=[BF16], name="branch_sb")

    def merge_fn(i, tiles, consts):
        bf_, bs_, gf_, gs_ = [t.astype(F32) for t in tiles]
        return [_sigmoid(gf_) * bf_ + _sigmoid(gs_) * bs_], []

    gates_in = [(proj_b, d, gate_blk), (proj_b, d, gate_blk + 1)]
    merged, = _rowwise(merge_fn, [_whole(br_fox), _whole(br_sb)] + gates_in, [], [(d, BF16)], [], tm=tr, name="merge")
    h2, = _mm(merged, [(full["w_out"], 0, 0)], d, dims=NN, tm=tm, tn=tnd, tk=d, out_dtypes=[F32],
              epi=lambda accs, ex: [ex[0] + accs[0]], extras=[(h1, 0)], name="out_proj")

    h3, saved3 = _ffn_fwd(h2, ffn2_norm, w_gu2, full["ffn2_w_down"], d_ff, tm=tm, tag="ffn2")

    skip = tile // tr

    def loss_fn(i, tiles, consts):
        real = i >= skip
        err = jnp.where(real, tiles[0] - tiles[1], 0.0)
        dy = err * (1.0 / d)
        part = 0.5 * jnp.sum(err * dy, axis=0, keepdims=True)
        return [dy, FFN_RESIDUAL_WEIGHT * dy], [part]

    dh3, dh3_half, loss_cols = _rowwise(
        loss_fn, [_whole(h3), _whole(loss_target[0])], [], [(d, F32, lp), (d, BF16, lp)], [(1, d)], tm=tr, name="loss",
        row_maps=[None, lambda i: jnp.maximum(i - skip, 0)])

    dh2, dh2_bf, _, dg_ffn2, dw_gate2, dw_up2, dw_down2 = _ffn_bwd(
        dh3, dh3_half, h2, ffn2_norm, saved3, w_gu2, full["ffn2_w_down"], d_ff, tm=tm, tag="ffn2")

    tk_rows = _pick(lp, [1088, 544, 256, 128])
    dmerged, = _mm(dh2_bf, [(full["w_out"], 0, 0)], d, dims=NT, tm=tm, tn=tnd, tk=d, out_dtypes=[BF16], name="out_proj_dx")
    dw_out, = _mm(merged, [(dh2_bf, 0, 0)], d, dims=TN, tm=tnd, tn=tnd, tk=tk_rows, out_dtypes=[BF16], name="out_proj_dw")

    def merge_bwd_fn(i, tiles, consts):
        dm, bf_, bs_, gf_, gs_ = [t.astype(F32) for t in tiles]
        sf, ss = _sigmoid(gf_), _sigmoid(gs_)
        return [dm * sf, dm * ss, dm * bf_ * sf * (1.0 - sf), dm * bs_ * ss * (1.0 - ss)], []

    dbr_fox, dbr_sb, dg_fox, dg_sb = _rowwise(
        merge_bwd_fn, [_whole(dmerged), _whole(br_fox), _whole(br_sb)] + gates_in, [], [(d, BF16)] * 4, [], tm=tr, name="merge_bwd")

    do_fox, = _mm(dbr_fox, [(full["w_branch_fox"], 0, 0)], w, dims=NT, tm=tm, tn=tkw, tk=d, out_dtypes=[BF16], name="branch_fox_dx")
    do_sb, = _mm(dbr_sb, [(full["w_branch_sb"], 0, 0)], w, dims=NT, tm=tm, tn=tkw, tk=d, out_dtypes=[BF16], name="branch_sb_dx")
    dw_br_fox, = _mm(o_fox, [(dbr_fox, 0, 0)], d, dims=TN, tm=tkw, tn=tnd, tk=tk_rows, out_dtypes=[BF16], name="branch_fox_dw")
    dw_br_sb, = _mm(o_sb, [(dbr_sb, 0, 0)], d, dims=TN, tm=tkw, tn=tnd, tk=tk_rows, out_dtypes=[BF16], name="branch_sb_dw")

    dfq, dfk, dfv, dc_rows = _fox_bwd(fq, fk, proj_b, 0, c_rows, o_fox, do_fox, lse, tile=tile, pad=pad, name="fox_bwd")
    dsq, dsk, dsv = _sb_bwd(proj_b, HEADS, 2 * HEADS, 3 * HEADS, do_sb, sb_total, tile=tile, pad=pad, name="sb_bwd")
    dc = jnp.pad(jnp.transpose(dc_rows.reshape(HEADS, lp)), ((0, 0), (0, LANES - HEADS)))
    dlogf = _cumsum_rows(dc, tile=tile, reverse=True, name="forget_cumsum_bwd")
    dproj_a, dg_q, dg_k, dg_b = _fox_prep_bwd(proj_a, dfq, dfk, dlogf, q_gain, k_gain, b_pad, tm=tr, pad=pad, name="fox_prep_bwd")
    dproj_b = jnp.concatenate([dfv, dsq, dsk, dsv, dg_fox, dg_sb], axis=1)

    tka = na
    tkb = _pick(nbw, [512, 256, 128])
    dw_pa, = _mm(n2, [(dproj_a, 0, 0)], na, dims=TN, tm=tnd, tn=na, tk=_pick(lp, [544, 256, 128]), out_dtypes=[BF16], name="proj_a_dw")
    dw_pb, = _mm(n2, [(dproj_b, 0, 0)], nbw, dims=TN, tm=tnd, tn=_pick(nbw, [512, 256, 128]), tk=tk_rows, out_dtypes=[BF16], name="proj_b_dw")
    dn2_a, = _mm(dproj_a, [(w_pa, 0, 0)], d, dims=NT, tm=tm, tn=tnd, tk=tka, out_dtypes=[F32], name="proj_a_dx")
    dn2, = _mm(dproj_b, [(w_pb, 0, 0)], d, dims=NT, tm=tm, tn=tnd, tk=tkb, out_dtypes=[F32],
               epi=lambda accs, ex: [accs[0] + ex[0]], extras=[(dn2_a, 0)], name="proj_b_dx")
    dh1, _, dh1_half, dg_mix = _norm_bwd(dn2, h1, mix_norm, dh2, tm=tr, name="mix_norm_bwd")

    dh0, _, _, dg_ffn1, dw_gate1, dw_up1, dw_down1 = _ffn_bwd(
        dh1, dh1_half, h0, ffn1_norm, saved1, w_gu1, full["ffn1_w_down"], d_ff, tm=tm, tag="ffn1")

    grad_x = dh0[tile:][None]
    dw_in = jnp.concatenate([dw_pa[:, :2 * w], dw_pb[:, :w], dw_pa[:, 2 * w:2 * w + HEADS], dw_pb[:, w:]], axis=1)

    local = dict(ffn1_w_gate=dw_gate1, ffn1_w_up=dw_up1, ffn1_w_down=dw_down1, w_in=dw_in,
                 w_branch_fox=dw_br_fox, w_branch_sb=dw_br_sb, w_out=dw_out,
                 ffn2_w_gate=dw_gate2, ffn2_w_up=dw_up2, ffn2_w_down=dw_down2)
    to_send = []
    for k in names:
        g = local[k]
        blocks = g.reshape((NDEV,) + tuple(shard_shapes[k])) if k in row_cut else _split_cols(g)
        flat = blocks.reshape(NDEV, -1)
        rows = -(-flat.shape[1] // PACK_WIDTH)
        to_send.append(jnp.pad(flat, ((0, 0), (0, rows * PACK_WIDTH - flat.shape[1]))).reshape(NDEV, rows, PACK_WIDTH))
    received = _exchange_blocks(jnp.concatenate(to_send, axis=1), "exchange_grads")
    grads, deltas, new_ms, new_vs = {}, {}, {}, {}
    for k, span in zip(names, spans):
        parts = _unpack_blocks(received, span, shard_shapes[k])
        g, dl, mn, vn = _adamw_summed(parts, big[k][0], big_m[k][0], big_v[k][0], name=f"adamw_{k}")
        grads[k], deltas[k], new_ms[k], new_vs[k] = g[None], dl[None], mn[None], vn[None]

    small_parts = [dh0[pad:tile], dg_ffn1, dg_mix, dg_ffn2, dg_b[:, :HEADS], dg_q, dg_k, loss_cols]
    small_packed, small_spans = _pack_rows(small_parts, LANES, SMALL_ROWS)
    small_sum = _allsum_small(small_packed, "sum_small")
    g_meta_full, g_ffn1n, g_mixn, g_ffn2n, g_bf, g_qn, g_kn, loss_vec = [
        _unpack(small_sum, span, part.shape) for span, part in zip(small_spans, small_parts)]
    loss = jnp.sum(loss_vec)
    g_meta = lax.dynamic_slice_in_dim(g_meta_full, me * (d // NDEV), d // NDEV, axis=1)
    g_qn, g_kn = g_qn.reshape(fox_q_norm.shape), g_kn.reshape(fox_k_norm.shape)

    small = dict(meta_tokens=(g_meta, meta_tokens, m_meta_tokens, v_meta_tokens),
                 ffn1_norm=(g_ffn1n, ffn1_norm, m_ffn1_norm, v_ffn1_norm),
                 mix_norm=(g_mixn, mix_norm, m_mix_norm, v_mix_norm),
                 b_forget=(g_bf, b_forget, m_b_forget, v_b_forget),
                 fox_q_norm=(g_qn, fox_q_norm, m_fox_q_norm, v_fox_q_norm),
                 fox_k_norm=(g_kn, fox_k_norm, m_fox_k_norm, v_fox_k_norm),
                 ffn2_norm=(g_ffn2n, ffn2_norm, m_ffn2_norm, v_ffn2_norm))
    for k, (g, wt, mt, vt) in small.items():
        flat = lambda t: t.reshape(-1, t.shape[-1])
        dl, mn, vn = _adamw_plain(flat(g), flat(wt), flat(mt), flat(vt), name=f"adamw_{k}")
        grads[k], deltas[k], new_ms[k], new_vs[k] = g, dl.reshape(wt.shape), mn.reshape(wt.shape), vn.reshape(wt.shape)

    order = ["meta_tokens", "ffn1_norm", "ffn1_w_gate", "ffn1_w_up", "ffn1_w_down", "mix_norm", "w_in", "b_forget",
             "fox_q_norm", "fox_k_norm", "w_branch_fox", "w_branch_sb", "w_out", "ffn2_norm", "ffn2_w_gate",
             "ffn2_w_up", "ffn2_w_down"]
    return (loss, grad_x, *[grads[k] for k in order], *[deltas[k] for k in order],
            *[new_ms[k] for k in order], *[new_vs[k] for k in order])
```

```python
import jax
import jax.numpy as jnp
from jax import lax
from jax.experimental import pallas as pl
from jax.experimental.pallas import tpu as pltpu

F32 = jnp.float32
BF16 = jnp.bfloat16
MESH = pl.DeviceIdType.MESH

NDEV = 8
N_META = 16
HEAD_DIM = 128
HEADS = 8
BRANCH_WIDTH = HEADS * HEAD_DIM
RMS_EPS = 1e-6
FFN_RESIDUAL_WEIGHT = 0.5
ATTN_SCALE = HEAD_DIM ** -0.5
MASKED_LOGIT = -1e30

ADAM_LR = 0.001
ADAM_B1 = 0.9
ADAM_B2 = 0.999
ADAM_EPS = 1e-08
ADAM_WD = 0.01
ADAM_STEP = 10

LANES = 128
SMALL_ROWS = 8
ADAMW_TILE_ELEMS = 160 * 1024


def _pick(n, prefs):
    for p in prefs:
        if p <= n and n % p == 0:
            return p
    return n


def _dot(a, b, dims):
    return lax.dot_general(a, b, (dims, ((), ())), preferred_element_type=F32)


NN = ((1,), (0,))
TN = ((0,), (0,))
NT = ((1,), (1,))


def _split_bf16(x):
    hi = x.astype(BF16)
    lo = (x - hi.astype(F32)).astype(BF16)
    return hi, lo


def _sigmoid_parts(z):
    e = jnp.exp(-jnp.abs(z))
    t = 1.0 + e
    return e, 1.0 / t, jnp.log(t)


def _sigmoid(x):
    e = jnp.exp(-jnp.abs(x))
    r = 1.0 / (1.0 + e)
    return jnp.where(x >= 0, r, e * r)


def _my_place():
    return lax.axis_index("x"), lax.axis_index("y"), lax.axis_index("c")


def _flat_id(px, py, pc):
    return 4 * px + 2 * py + pc


def _peer(x, y, c, k):
    px = 1 - x if k & 4 else x
    py = 1 - y if k & 2 else y
    pc = 1 - c if k & 1 else c
    return px, py, pc


def _allgather(shards, name, in_vmem=False):
    n = len(shards)

    def body(*refs):
        x_refs, out_refs = refs[:n], refs[n:2 * n]
        send_sems, recv_sems, local_sems = refs[2 * n:]
        x, y, c = _my_place()
        me, sibling = (x, y, c), (x, y, 1 - c)
        chips = [(1 - x, y), (x, 1 - y), (1 - x, 1 - y)]

        def block(a, place):
            return out_refs[a].at[_flat_id(*place)]

        def copy(a, k, place, to, src=None):
            return pltpu.make_async_remote_copy(
                src_ref=block(a, place) if src is None else src, dst_ref=block(a, place),
                send_sem=send_sems.at[7 * a + k], recv_sem=recv_sems.at[7 * a + k], device_id=to, device_id_type=MESH)

        mine = [pltpu.make_async_copy(x_refs[a], block(a, me), local_sems.at[a]) for a in range(n)]
        for cp in mine:
            cp.start()
        first = []
        for a in range(n):
            first.append(copy(a, 0, me, sibling, src=x_refs[a]))
            first += [copy(a, 1 + j, me, (*chip, c), src=x_refs[a]) for j, chip in enumerate(chips)]
        for cp in first:
            cp.start()
        passed = []
        for j, chip in enumerate(chips):
            for a in range(n):
                copy(a, 1 + j, (*chip, c), me).wait_recv()
                passed.append(copy(a, 4 + j, (*chip, c), sibling))
                passed[-1].start()
        for a in range(n):
            copy(a, 0, sibling, me).wait_recv()
        for j, chip in enumerate(chips):
            for a in range(n):
                copy(a, 4 + j, (*chip, 1 - c), me).wait_recv()
        for cp in first + passed:
            cp.wait_send()
        for cp in mine:
            cp.wait()

    space = pltpu.VMEM if in_vmem else pl.ANY
    return pl.pallas_call(
        body, name=name,
        out_shape=[jax.ShapeDtypeStruct((NDEV,) + s.shape, s.dtype) for s in shards],
        in_specs=[pl.BlockSpec(memory_space=space)] * n,
        out_specs=[pl.BlockSpec(memory_space=space)] * n,
        scratch_shapes=[pltpu.SemaphoreType.DMA((7 * n,)), pltpu.SemaphoreType.DMA((7 * n,)), pltpu.SemaphoreType.DMA((n,))],
    )(*shards)


def _exchange_blocks(arrays, name):
    n = len(arrays)

    def body(*refs):
        g_refs, out_refs = refs[:n], refs[n:2 * n]
        send_sems, recv_sems, local_sems = refs[2 * n:]
        x, y, c = _my_place()
        me = _flat_id(x, y, c)
        own = [pltpu.make_async_copy(g_refs[a].at[me], out_refs[a].at[me], local_sems.at[a]) for a in range(n)]
        for cp in own:
            cp.start()
        copies = []
        for a in range(n):
            for k in range(1, NDEV):
                peer = _peer(x, y, c, k)
                copies.append(pltpu.make_async_remote_copy(
                    src_ref=g_refs[a].at[_flat_id(*peer)], dst_ref=out_refs[a].at[me],
                    send_sem=send_sems.at[7 * a + k - 1], recv_sem=recv_sems.at[7 * a + k - 1],
                    device_id=peer, device_id_type=MESH))
        for cp in copies:
            cp.start()
        for cp in copies:
            cp.wait()
        for cp in own:
            cp.wait()

    return pl.pallas_call(
        body, name=name,
        out_shape=[jax.ShapeDtypeStruct(g.shape, g.dtype) for g in arrays],
        in_specs=[pl.BlockSpec(memory_space=pl.ANY)] * n,
        out_specs=[pl.BlockSpec(memory_space=pl.ANY)] * n,
        scratch_shapes=[pltpu.SemaphoreType.DMA((7 * n,)), pltpu.SemaphoreType.DMA((7 * n,)), pltpu.SemaphoreType.DMA((n,))],
    )(*arrays)


def _allsum_small(part, name):
    rows, cols = part.shape

    def body(p_ref, out_ref, buf, send_sems, recv_sems):
        x, y, c = _my_place()
        me = _flat_id(x, y, c)
        buf[me] = p_ref[...]
        copies = []
        for k in range(1, NDEV):
            copies.append(pltpu.make_async_remote_copy(
                src_ref=p_ref, dst_ref=buf.at[me], send_sem=send_sems.at[k - 1], recv_sem=recv_sems.at[k - 1],
                device_id=_peer(x, y, c, k), device_id_type=MESH))
        for cp in copies:
            cp.start()
        for cp in copies:
            cp.wait()
        total = buf[0]
        for j in range(1, NDEV):
            total = total + buf[j]
        out_ref[...] = total

    return pl.pallas_call(
        body, name=name,
        out_shape=jax.ShapeDtypeStruct((rows, cols), F32),
        in_specs=[pl.BlockSpec(memory_space=pltpu.VMEM)],
        out_specs=pl.BlockSpec(memory_space=pltpu.VMEM),
        scratch_shapes=[pltpu.VMEM((NDEV, rows, cols), F32),
                        pltpu.SemaphoreType.DMA((7,)), pltpu.SemaphoreType.DMA((7,))],
    )(part)


def _spec(block, index_map):
    return pl.BlockSpec(block, index_map)


def _mm(grid, a, a_spec, bs, extras, outs, *, dims, acc_shape=None, epi=None, name):
    nk = grid[2]
    nb, ne, no = len(bs), len(extras), len(outs)
    if epi is None:
        epi = lambda accs, ex: [accs[0]]

    def body(*refs):
        a_ref, b_refs = refs[0], refs[1:1 + nb]
        e_refs = refs[1 + nb:1 + nb + ne]
        o_refs = refs[1 + nb + ne:1 + nb + ne + no]
        acc_refs = refs[1 + nb + ne + no:]
        av = a_ref[...]
        parts = [_dot(av, b_ref[...], dims) for b_ref in b_refs]

        def finish(accs):
            for o_ref, tile in zip(o_refs, epi(accs, [e_ref[...] for e_ref in e_refs])):
                o_ref[...] = tile.astype(o_ref.dtype)

        if nk == 1:
            finish(parts)
        else:
            k = pl.program_id(2)

            @pl.when(k == 0)
            def _():
                for acc_ref, part in zip(acc_refs, parts):
                    acc_ref[...] = part

            @pl.when(k > 0)
            def _():
                for acc_ref, part in zip(acc_refs, parts):
                    acc_ref[...] += part

            @pl.when(k == nk - 1)
            def _():
                finish([acc_ref[...] for acc_ref in acc_refs])

    return pl.pallas_call(
        body, name=name,
        grid=grid,
        in_specs=[a_spec] + [s for _, s in bs] + [s for _, s in extras],
        out_specs=[s for _, _, s in outs],
        out_shape=[jax.ShapeDtypeStruct(shape, dt) for shape, dt, _ in outs],
        scratch_shapes=[pltpu.VMEM(acc_shape, F32) for _ in bs] if nk > 1 else [],
        compiler_params=pltpu.CompilerParams(dimension_semantics=("parallel", "parallel", "arbitrary")),
    )(a, *[b for b, _ in bs], *[e for e, _ in extras])


def _mm2d(a, b, n_cols, *, dims, tm, tn, tk, out_dtype, epi=None, extras=(), name):
    m_rows, k_len = (a.shape[1], a.shape[0]) if dims == TN else a.shape
    assert m_rows % tm == 0 and n_cols % tn == 0 and k_len % tk == 0, (name, a.shape, n_cols, tm, tn, tk)
    a_spec = _spec((tk, tm), lambda i, j, k: (k, i)) if dims == TN else _spec((tm, tk), lambda i, j, k: (i, k))
    b_spec = _spec((tn, tk), lambda i, j, k: (j, k)) if dims == NT else _spec((tk, tn), lambda i, j, k: (k, j))
    tile = _spec((tm, tn), lambda i, j, k: (i, j))
    return _mm((m_rows // tm, n_cols // tn, k_len // tk), a, a_spec, [(b, b_spec)], [(e, tile) for e in extras],
               [((m_rows, n_cols), out_dtype, tile)], dims=dims, acc_shape=(tm, tn), epi=epi, name=name)[0]


def _rowwise(fn, ins, consts, outs, sums, *, tm, name, row_maps=None):
    m_rows = outs[0][2] if len(outs[0]) == 3 else ins[0][0].shape[0]
    n = m_rows // tm
    ni, nc, no = len(ins), len(consts), len(outs)
    row_maps = row_maps or [None] * ni

    def body(*refs):
        i = pl.program_id(0)
        in_tiles = [r[...] for r in refs[:ni]]
        const_values = [r[...] for r in refs[ni:ni + nc]]
        o_refs = refs[ni + nc:ni + nc + no]
        s_refs = refs[ni + nc + no:]
        out_tiles, sum_terms = fn(i, in_tiles, const_values)
        for o_ref, tile in zip(o_refs, out_tiles):
            o_ref[...] = tile.astype(o_ref.dtype)
        if s_refs:
            @pl.when(i == 0)
            def _():
                for s_ref in s_refs:
                    s_ref[...] = jnp.zeros_like(s_ref)

            for s_ref, term in zip(s_refs, sum_terms):
                s_ref[...] += term

    def in_spec(width, col, rmap):
        if rmap is None:
            return pl.BlockSpec((tm, width), lambda i: (i, col))
        return pl.BlockSpec((tm, width), lambda i: (rmap(i), col))

    return pl.pallas_call(
        body, name=name,
        grid=(n,),
        in_specs=[in_spec(w, col, rmap) for (_, w, col), rmap in zip(ins, row_maps)]
        + [pl.BlockSpec(cst.shape, lambda i: (0, 0)) for cst in consts],
        out_specs=[pl.BlockSpec((tm, o[0]), lambda i: (i, 0)) for o in outs]
        + [pl.BlockSpec(s, lambda i: (0, 0)) for s in sums],
        out_shape=[jax.ShapeDtypeStruct((m_rows, o[0]), o[1]) for o in outs]
        + [jax.ShapeDtypeStruct(s, F32) for s in sums],
        compiler_params=pltpu.CompilerParams(dimension_semantics=("arbitrary",)),
    )(*[arr for arr, _, _ in ins], *consts)


def _whole(arr):
    return (arr, arr.shape[1], 0)


def _rms(x, gain):
    r = lax.rsqrt(jnp.mean(x * x, axis=-1, keepdims=True) + RMS_EPS)
    return x * r * gain


def _rms_bwd(x, gain, dy):
    r = lax.rsqrt(jnp.mean(x * x, axis=-1, keepdims=True) + RMS_EPS)
    u = dy * gain
    dx = r * u - x * (r * r * r) * jnp.mean(x * u, axis=-1, keepdims=True)
    return dx, dy * x * r


def _norm_fwd(h, gain, *, tm, name):
    def fn(i, tiles, consts):
        return [_rms(tiles[0], consts[0])], []
    return _rowwise(fn, [_whole(h)], [gain], [(h.shape[1], BF16)], [], tm=tm, name=name)[0]


def _norm_bwd(dn, h, gain, dh_in, *, tm, name):
    d = h.shape[1]

    def fn(i, tiles, consts):
        dx, dg_rows = _rms_bwd(tiles[1], consts[0], tiles[0])
        dh = tiles[2] + dx
        return [dh, dh, FFN_RESIDUAL_WEIGHT * dh], [jnp.sum(dg_rows, axis=0, keepdims=True)]

    return _rowwise(fn, [_whole(dn), _whole(h), _whole(dh_in)], [gain],
                    [(d, F32), (d, BF16), (d, BF16)], [(1, d)], tm=tm, name=name)


def _swiglu_epi(accs, ex):
    a, b = accs
    return [a, b, a * _sigmoid(a) * b]


def _swiglu_bwd_epi(accs, ex):
    ds = accs[0]
    a, b = ex[0].astype(F32), ex[1].astype(F32)
    sig = _sigmoid(a)
    silu = a * sig
    dsilu = sig * (1.0 + a * (1.0 - sig))
    return [jnp.stack([(ds * b * dsilu).astype(BF16), (ds * silu).astype(BF16)], axis=0)]


def _attn_mask(i, j, tile, pad, strict):
    row = i * tile + lax.broadcasted_iota(jnp.int32, (tile, tile), 0)
    col = j * tile + lax.broadcasted_iota(jnp.int32, (tile, tile), 1)
    causal = (col < row) if strict else (col <= row)
    return causal & ((col >= pad) | (row < pad))


def _key_tile(ref, j, tile):
    return ref[pl.ds(pl.multiple_of(j * tile, tile), tile), :]


def _head_specs(lp, tile, q_off, k_off, v_off):
    q_spec = pl.BlockSpec((tile, HEAD_DIM), lambda h, i: (i, h + q_off))
    k_spec = pl.BlockSpec((lp, HEAD_DIM), lambda h, i: (0, h + k_off))
    v_spec = pl.BlockSpec((lp, HEAD_DIM), lambda h, i: (0, h + v_off))
    return q_spec, k_spec, v_spec


def _fox_fwd(q, k, v, v_off, c_rows, *, tile, pad, name):
    lp = q.shape[0]
    nb = lp // tile

    def body(q_ref, k_ref, v_ref, c_ref, o_ref, lse_ref):
        i = pl.program_id(1)
        qt = q_ref[...]

        def step(j, carry):
            m, l, acc = carry
            s = _dot(qt, _key_tile(k_ref, j, tile), NT) * ATTN_SCALE - c_ref[0, j]
            s = jnp.where(_attn_mask(i, j, tile, pad, False), s, MASKED_LOGIT)
            m_new = jnp.maximum(m, jnp.max(s, axis=1, keepdims=True))
            p = jnp.exp(s - m_new)
            alpha = jnp.exp(m - m_new)
            l = alpha * l + jnp.sum(p, axis=1, keepdims=True)
            acc = alpha * acc + _dot(p.astype(BF16), _key_tile(v_ref, j, tile), NN)
            return m_new, l, acc

        init = (jnp.full((tile, 1), MASKED_LOGIT, F32), jnp.zeros((tile, 1), F32), jnp.zeros((tile, HEAD_DIM), F32))
        m, l, acc = lax.fori_loop(0, i + 1, step, init)
        o_ref[...] = (acc / l).astype(o_ref.dtype)
        lse_ref[0] = jnp.broadcast_to(m + jnp.log(l), (tile, LANES))

    q_spec, k_spec, v_spec = _head_specs(lp, tile, 0, 0, v_off)
    return pl.pallas_call(
        body, name=name,
        grid=(HEADS, nb),
        in_specs=[q_spec, k_spec, v_spec, pl.BlockSpec((1, nb, 1, tile), lambda h, i: (h, 0, 0, 0))],
        out_specs=[pl.BlockSpec((tile, HEAD_DIM), lambda h, i: (i, h)),
                   pl.BlockSpec((1, tile, LANES), lambda h, i: (h, i, 0))],
        out_shape=[jax.ShapeDtypeStruct((lp, BRANCH_WIDTH), BF16), jax.ShapeDtypeStruct((HEADS, lp, LANES), F32)],
        compiler_params=pltpu.CompilerParams(dimension_semantics=("parallel", "arbitrary")),
    )(q, k, v, c_rows)


def _fox_bwd(q, k, v, v_off, c_rows, o, do, lse, *, tile, pad, name):
    lp = q.shape[0]
    nb = lp // tile

    def body(q_ref, k_ref, v_ref, c_ref, o_ref, do_ref, lse_ref, dq_ref, dk_ref, dv_ref, dc_ref, dk_acc, dv_acc, dc_acc):
        i = pl.program_id(1)

        @pl.when(i == 0)
        def _():
            dk_acc[...] = jnp.zeros_like(dk_acc)
            dv_acc[...] = jnp.zeros_like(dv_acc)
            dc_acc[...] = jnp.zeros_like(dc_acc)

        qt, dot_ = q_ref[...], do_ref[...]
        delta = jnp.sum(dot_.astype(F32) * o_ref[...].astype(F32), axis=1, keepdims=True)
        lse_col = lse_ref[0][:, :1]

        def step(j, dq):
            kt = _key_tile(k_ref, j, tile)
            rows = pl.ds(pl.multiple_of(j * tile, tile), tile)
            s = _dot(qt, kt, NT) * ATTN_SCALE - c_ref[0, j]
            p = jnp.where(_attn_mask(i, j, tile, pad, False), jnp.exp(s - lse_col), 0.0)
            dp = _dot(dot_, _key_tile(v_ref, j, tile), NT)
            ds = p * (dp - delta)
            dsb = ds.astype(BF16)
            dk_acc[rows, :] += _dot(dsb, qt, TN)
            dv_acc[rows, :] += _dot(p.astype(BF16), dot_, TN)
            dc_acc[j] += -jnp.sum(ds, axis=0, keepdims=True)
            return dq + _dot(dsb, kt, NN)

        dq = lax.fori_loop(0, i + 1, step, jnp.zeros((tile, HEAD_DIM), F32))
        dq_ref[...] = dq * ATTN_SCALE

        @pl.when(i == nb - 1)
        def _():
            dk_ref[...] = dk_acc[...] * ATTN_SCALE
            dv_ref[...] = dv_acc[...].astype(dv_ref.dtype)
            dc_ref[0] = dc_acc[...]

    q_spec, k_spec, v_spec = _head_specs(lp, tile, 0, 0, v_off)
    tile_spec = pl.BlockSpec((tile, HEAD_DIM), lambda h, i: (i, h))
    head_spec = pl.BlockSpec((lp, HEAD_DIM), lambda h, i: (0, h))
    c_spec = pl.BlockSpec((1, nb, 1, tile), lambda h, i: (h, 0, 0, 0))
    return pl.pallas_call(
        body, name=name,
        grid=(HEADS, nb),
        in_specs=[q_spec, k_spec, v_spec, c_spec, tile_spec, tile_spec,
                  pl.BlockSpec((1, tile, LANES), lambda h, i: (h, i, 0))],
        out_specs=[tile_spec, head_spec, head_spec, c_spec],
        out_shape=[jax.ShapeDtypeStruct((lp, BRANCH_WIDTH), F32), jax.ShapeDtypeStruct((lp, BRANCH_WIDTH), F32),
                   jax.ShapeDtypeStruct((lp, BRANCH_WIDTH), BF16), jax.ShapeDtypeStruct((HEADS, nb, 1, tile), F32)],
        scratch_shapes=[pltpu.VMEM((lp, HEAD_DIM), F32), pltpu.VMEM((lp, HEAD_DIM), F32), pltpu.VMEM((nb, 1, tile), F32)],
        compiler_params=pltpu.CompilerParams(dimension_semantics=("parallel", "arbitrary")),
    )(q, k, v, c_rows, o, do, lse)


def _later_matrix(tile):
    return (lax.broadcasted_iota(jnp.int32, (tile, tile), 0) > lax.broadcasted_iota(jnp.int32, (tile, tile), 1)).astype(BF16)


def _earlier_matrix(tile):
    return (lax.broadcasted_iota(jnp.int32, (tile, tile), 0) < lax.broadcasted_iota(jnp.int32, (tile, tile), 1)).astype(BF16)


def _sb_tile(qt, kt, ok, later):
    z = _dot(qt, kt, NT) * ATTN_SCALE
    e, r, lg = _sigmoid_parts(z)
    sp = jnp.maximum(z, 0.0) + lg
    spm = jnp.where(ok, sp, 0.0)
    hi, lo = _split_bf16(spm)
    within = _dot(hi, later, NN) + _dot(lo, later, NN)
    return z, e, r, sp, spm, within


def _sb_fwd(qkv, q_off, k_off, v_off, *, tile, pad, name):
    lp = qkv.shape[0]
    nb = lp // tile

    def body(q_ref, k_ref, v_ref, o_ref, tot_ref):
        i = pl.program_id(1)
        qt = q_ref[...]
        later = _later_matrix(tile)

        def step(t, carry):
            right, acc = carry
            j = i - t
            ok = _attn_mask(i, j, tile, pad, True)
            z, _, _, sp, spm, within = _sb_tile(qt, _key_tile(k_ref, j, tile), ok, later)
            w = jnp.where(ok, jnp.exp(z - sp - within - right), 0.0)
            acc = acc + _dot(w.astype(BF16), _key_tile(v_ref, j, tile), NN)
            return right + jnp.sum(spm, axis=1, keepdims=True), acc

        total, acc = lax.fori_loop(0, i + 1, step, (jnp.zeros((tile, 1), F32), jnp.zeros((tile, HEAD_DIM), F32)))
        o_ref[...] = acc.astype(o_ref.dtype)
        tot_ref[0] = jnp.broadcast_to(total, (tile, LANES))

    q_spec, k_spec, v_spec = _head_specs(lp, tile, q_off, k_off, v_off)
    return pl.pallas_call(
        body, name=name,
        grid=(HEADS, nb),
        in_specs=[q_spec, k_spec, v_spec],
        out_specs=[pl.BlockSpec((tile, HEAD_DIM), lambda h, i: (i, h)),
                   pl.BlockSpec((1, tile, LANES), lambda h, i: (h, i, 0))],
        out_shape=[jax.ShapeDtypeStruct((lp, BRANCH_WIDTH), BF16), jax.ShapeDtypeStruct((HEADS, lp, LANES), F32)],
        compiler_params=pltpu.CompilerParams(dimension_semantics=("parallel", "arbitrary")),
    )(qkv, qkv, qkv)


def _sb_bwd(qkv, q_off, k_off, v_off, do, total, *, tile, pad, name):
    lp = qkv.shape[0]
    nb = lp // tile

    def body(q_ref, k_ref, v_ref, do_ref, tot_ref, dq_ref, dk_ref, dv_ref, dk_acc, dv_acc):
        i = pl.program_id(1)

        @pl.when(i == 0)
        def _():
            dk_acc[...] = jnp.zeros_like(dk_acc)
            dv_acc[...] = jnp.zeros_like(dv_acc)

        qt, dot_ = q_ref[...], do_ref[...]
        total_col = tot_ref[0][:, :1]
        later, earlier = _later_matrix(tile), _earlier_matrix(tile)

        def step(j, carry):
            dq, sp_before, dlw_before = carry
            kt = _key_tile(k_ref, j, tile)
            rows = pl.ds(pl.multiple_of(j * tile, tile), tile)
            ok = _attn_mask(i, j, tile, pad, True)
            z, e, r, sp, spm, within = _sb_tile(qt, kt, ok, later)
            sp_here = jnp.sum(spm, axis=1, keepdims=True)
            right = total_col - sp_before - sp_here
            w = jnp.where(ok, jnp.exp(z - sp - within - right), 0.0)
            dlw = w * _dot(dot_, _key_tile(v_ref, j, tile), NT)
            hi, lo = _split_bf16(dlw)
            before = dlw_before + _dot(hi, earlier, NN) + _dot(lo, earlier, NN)
            sig = jnp.where(z >= 0, r, e * r)
            dz = jnp.where(ok, dlw * (1.0 - sig) - sig * before, 0.0)
            dzb = dz.astype(BF16)
            dk_acc[rows, :] += _dot(dzb, qt, TN)
            dv_acc[rows, :] += _dot(w.astype(BF16), dot_, TN)
            return dq + _dot(dzb, kt, NN), sp_before + sp_here, dlw_before + jnp.sum(dlw, axis=1, keepdims=True)

        zero_col = jnp.zeros((tile, 1), F32)
        dq, _, _ = lax.fori_loop(0, i + 1, step, (jnp.zeros((tile, HEAD_DIM), F32), zero_col, zero_col))
        dq_ref[...] = (dq * ATTN_SCALE).astype(dq_ref.dtype)

        @pl.when(i == nb - 1)
        def _():
            dk_ref[...] = (dk_acc[...] * ATTN_SCALE).astype(dk_ref.dtype)
            dv_ref[...] = dv_acc[...].astype(dv_ref.dtype)

    q_spec, k_spec, v_spec = _head_specs(lp, tile, q_off, k_off, v_off)
    tile_spec = pl.BlockSpec((tile, HEAD_DIM), lambda h, i: (i, h))
    head_spec = pl.BlockSpec((lp, HEAD_DIM), lambda h, i: (0, h))
    return pl.pallas_call(
        body, name=name,
        grid=(HEADS, nb),
        in_specs=[q_spec, k_spec, v_spec, tile_spec, pl.BlockSpec((1, tile, LANES), lambda h, i: (h, i, 0))],
        out_specs=[tile_spec, head_spec, head_spec],
        out_shape=[jax.ShapeDtypeStruct((lp, BRANCH_WIDTH), BF16)] * 3,
        scratch_shapes=[pltpu.VMEM((lp, HEAD_DIM), F32), pltpu.VMEM((lp, HEAD_DIM), F32)],
        compiler_params=pltpu.CompilerParams(dimension_semantics=("parallel", "arbitrary")),
    )(qkv, qkv, qkv, do, total)


def _cumsum_rows(x, *, tile, reverse, name):
    lp = x.shape[0]
    nb = lp // tile

    def body(x_ref, o_ref, carry):
        @pl.when(pl.program_id(0) == 0)
        def _():
            carry[...] = jnp.zeros_like(carry)

        r = lax.broadcasted_iota(jnp.int32, (tile, tile), 0)
        c = lax.broadcasted_iota(jnp.int32, (tile, tile), 1)
        tri = ((c >= r) if reverse else (c <= r)).astype(BF16)
        hi, lo = _split_bf16(x_ref[...])
        run = _dot(tri, hi, NN) + _dot(tri, lo, NN) + carry[...]
        o_ref[...] = run
        carry[...] = run[:1, :] if reverse else run[tile - 1:, :]

    order = (lambda i: (nb - 1 - i, 0)) if reverse else (lambda i: (i, 0))
    return pl.pallas_call(
        body, name=name,
        grid=(nb,),
        in_specs=[pl.BlockSpec((tile, LANES), order)],
        out_specs=pl.BlockSpec((tile, LANES), order),
        out_shape=jax.ShapeDtypeStruct((lp, LANES), F32),
        scratch_shapes=[pltpu.VMEM((1, LANES), F32)],
        compiler_params=pltpu.CompilerParams(dimension_semantics=("arbitrary",)),
    )(x)


def _log_sigmoid(x):
    return jnp.minimum(x, 0.0) - jnp.log(1.0 + jnp.exp(-jnp.abs(x)))


def _forget_mask(i, tm, pad):
    row = i * tm + lax.broadcasted_iota(jnp.int32, (tm, LANES), 0)
    lane = lax.broadcasted_iota(jnp.int32, (tm, LANES), 1)
    return (row >= pad) & (lane < HEADS)


def _fox_prep(proj_a, q_gain, k_gain, b_forget, *, tm, pad, name):
    w = BRANCH_WIDTH

    def fn(i, tiles, consts):
        pa = tiles[0]
        qs, ks = [], []
        for h in range(HEADS):
            lo = h * HEAD_DIM
            qs.append(_rms(pa[:, lo:lo + HEAD_DIM], consts[0][:, lo:lo + HEAD_DIM]))
            ks.append(_rms(pa[:, w + lo:w + lo + HEAD_DIM], consts[1][:, lo:lo + HEAD_DIM]))
        logf = jnp.where(_forget_mask(i, tm, pad), _log_sigmoid(pa[:, 2 * w:] + consts[2]), 0.0)
        return [jnp.concatenate(qs, axis=1), jnp.concatenate(ks, axis=1), logf], []

    return _rowwise(fn, [_whole(proj_a)], [q_gain, k_gain, b_forget],
                    [(w, BF16), (w, BF16), (LANES, F32)], [], tm=tm, name=name)


def _fox_prep_bwd(proj_a, dq, dk, dlogf, q_gain, k_gain, b_forget, *, tm, pad, name):
    w = BRANCH_WIDTH

    def fn(i, tiles, consts):
        pa, dqt, dkt, dlf = tiles
        dxs_q, dxs_k, dgs_q, dgs_k = [], [], [], []
        for h in range(HEADS):
            lo = h * HEAD_DIM
            dx, dg = _rms_bwd(pa[:, lo:lo + HEAD_DIM], consts[0][:, lo:lo + HEAD_DIM], dqt[:, lo:lo + HEAD_DIM])
            dxs_q.append(dx)
            dgs_q.append(jnp.sum(dg, axis=0, keepdims=True))
            dx, dg = _rms_bwd(pa[:, w + lo:w + lo + HEAD_DIM], consts[1][:, lo:lo + HEAD_DIM], dkt[:, lo:lo + HEAD_DIM])
            dxs_k.append(dx)
            dgs_k.append(jnp.sum(dg, axis=0, keepdims=True))
        xf = pa[:, 2 * w:] + consts[2]
        e, r, _ = _sigmoid_parts(xf)
        df = jnp.where(_forget_mask(i, tm, pad), dlf * jnp.where(xf >= 0, e * r, r), 0.0)
        return ([jnp.concatenate(dxs_q + dxs_k + [df], axis=1)],
                [jnp.concatenate(dgs_q, axis=1), jnp.concatenate(dgs_k, axis=1), jnp.sum(df, axis=0, keepdims=True)])

    return _rowwise(fn, [_whole(proj_a), _whole(dq), _whole(dk), _whole(dlogf)], [q_gain, k_gain, b_forget],
                    [(2 * w + LANES, BF16)], [(1, w), (1, w), (1, LANES)], tm=tm, name=name)


def _adamw_math(w, g, m, v):
    m = ADAM_B1 * m + (1.0 - ADAM_B1) * g
    v = ADAM_B2 * v + (1.0 - ADAM_B2) * (g * g)
    m_hat = m / (1.0 - ADAM_B1 ** ADAM_STEP)
    v_hat = v / (1.0 - ADAM_B2 ** ADAM_STEP)
    delta = -ADAM_LR * (m_hat / (jnp.sqrt(v_hat) + ADAM_EPS) + ADAM_WD * w)
    return delta, m, v


def _adamw_summed(parts, sel, w, m, v, *, name):
    rows, cols = w.shape
    tr = _pick(rows, [t for t in (512, 256, 128, 64, 32, 16, 8) if t * cols <= ADAMW_TILE_ELEMS])

    def body(p_ref, w_ref, m_ref, v_ref, g_out, d_out, m_out, v_out):
        g = p_ref[0].astype(F32)
        for j in range(1, NDEV):
            g = g + p_ref[j].astype(F32)
        delta, m_new, v_new = _adamw_math(w_ref[...], g, m_ref[...], v_ref[...])
        g_out[...] = g
        d_out[...] = delta
        m_out[...] = m_new
        v_out[...] = v_new

    spec = pl.BlockSpec((tr, cols), lambda i: (i, 0))
    if parts.ndim == 3:
        p_spec = pl.BlockSpec((NDEV, tr, cols), lambda i: (0, i, 0))
    else:
        p_spec = pl.BlockSpec((NDEV, None, tr, cols), lambda i: (0, sel, i, 0))
    return pl.pallas_call(
        body, name=name,
        grid=(rows // tr,),
        in_specs=[p_spec, spec, spec, spec],
        out_specs=[spec] * 4,
        out_shape=[jax.ShapeDtypeStruct((rows, cols), F32)] * 4,
        compiler_params=pltpu.CompilerParams(dimension_semantics=("parallel",)),
    )(parts, w, m, v)


def _adamw_plain(g, w, m, v, *, name):
    def body(g_ref, w_ref, m_ref, v_ref, d_out, m_out, v_out):
        delta, m_new, v_new = _adamw_math(w_ref[...], g_ref[...], m_ref[...], v_ref[...])
        d_out[...] = delta
        m_out[...] = m_new
        v_out[...] = v_new

    return pl.pallas_call(body, name=name, out_shape=[jax.ShapeDtypeStruct(w.shape, F32)] * 3)(g, w, m, v)


def _pack_rows(arrays, width, row_align):
    pieces, spans, at = [], [], 0
    for arr in arrays:
        flat = arr.reshape(-1)
        rows = -(-flat.shape[0] // (width * row_align)) * row_align
        flat = jnp.pad(flat, (0, rows * width - flat.shape[0]))
        pieces.append(flat.reshape(rows, width))
        spans.append((at, rows))
        at += rows
    return jnp.concatenate(pieces, axis=0), spans


def _unpack(rows2d, span, shape):
    at, rows = span
    size = 1
    for s in shape:
        size *= s
    return rows2d[at:at + rows].reshape(-1)[:size].reshape(shape)


def _join_cols(blocks):
    n, rows, cols = blocks.shape
    return jnp.transpose(blocks, (1, 0, 2)).reshape(rows, n * cols)


def _split_cols(full):
    rows, cols = full.shape
    return jnp.transpose(full.reshape(rows, NDEV, cols // NDEV), (1, 0, 2))


def _ffn_fwd(h, gain, w_gu, w_down, *, tm, tag):
    lp, d = h.shape
    f8 = w_down.shape[1]
    n = _norm_fwd(h, gain, tm=_pick(lp, [256, 128]), name=f"{tag}_norm")
    hid = _spec((None, tm, f8), lambda i, j, k: (j, i, 0))
    a, b, s = _mm((lp // tm, NDEV, 1), n, _spec((tm, d), lambda i, j, k: (i, 0)),
                  [(w_gu, _spec((None, None, d, f8), lambda i, j, k: (j, 0, 0, 0))),
                   (w_gu, _spec((None, None, d, f8), lambda i, j, k: (j, 1, 0, 0)))],
                  [], [((NDEV, lp, f8), BF16, hid)] * 3, dims=NN, epi=_swiglu_epi, name=f"{tag}_up")
    tn = _pick(d, [1024, 512, 256, 128])
    tile = _spec((tm, tn), lambda i, j, k: (i, j))
    h_out, = _mm((lp // tm, d // tn, NDEV), s, _spec((None, tm, f8), lambda i, j, k: (k, i, 0)),
                 [(w_down, _spec((None, f8, tn), lambda i, j, k: (k, 0, j)))], [(h, tile)], [((lp, d), F32, tile)],
                 dims=NN, acc_shape=(tm, tn), epi=lambda accs, ex: [ex[0] + FFN_RESIDUAL_WEIGHT * accs[0]], name=f"{tag}_down")
    return h_out, (n, a, b, s)


def _ffn_bwd(dh, dh_half, h_in, gain, saved, w_gu, w_down, *, tm, tag):
    n, a, b, s = saved
    lp, d = h_in.shape
    f8 = w_down.shape[1]
    tkr = _pick(lp, [1088, 544, 256, 128])
    tn = _pick(d, [1024, 512, 256, 128])
    hid = _spec((None, tm, f8), lambda i, j, k: (j, i, 0))
    dab, = _mm((lp // tm, NDEV, 1), dh_half, _spec((tm, d), lambda i, j, k: (i, 0)),
               [(w_down, _spec((None, f8, d), lambda i, j, k: (j, 0, 0)))], [(a, hid), (b, hid)],
               [((2, NDEV, lp, f8), BF16, _spec((2, None, tm, f8), lambda i, j, k: (0, j, i, 0)))],
               dims=NT, epi=_swiglu_bwd_epi, name=f"{tag}_down_dx")
    dw_down, = _mm((NDEV, d // tn, lp // tkr), s, _spec((None, tkr, f8), lambda i, j, k: (i, k, 0)),
                   [(dh_half, _spec((tkr, tn), lambda i, j, k: (k, j)))], [],
                   [((NDEV, f8, d), BF16, _spec((None, f8, tn), lambda i, j, k: (i, 0, j)))],
                   dims=TN, acc_shape=(f8, tn), name=f"{tag}_down_dw")
    dw_gu, = _mm((d // tn, 2 * NDEV, lp // tkr), n, _spec((tkr, tn), lambda i, j, k: (k, i)),
                 [(dab, _spec((None, None, tkr, f8), lambda i, j, k: (j // NDEV, j % NDEV, k, 0)))], [],
                 [((NDEV, 2, d, f8), BF16, _spec((None, None, tn, f8), lambda i, j, k: (j % NDEV, j // NDEV, i, 0)))],
                 dims=TN, acc_shape=(tn, f8), name=f"{tag}_gate_up_dw")
    dn, = _mm((lp // tm, d // tn, 2 * NDEV), dab, _spec((None, None, tm, f8), lambda i, j, k: (k // NDEV, k % NDEV, i, 0)),
              [(w_gu, _spec((None, None, tn, f8), lambda i, j, k: (k % NDEV, k // NDEV, j, 0)))], [],
              [((lp, d), F32, _spec((tm, tn), lambda i, j, k: (i, j)))],
              dims=NT, acc_shape=(tm, tn), name=f"{tag}_gate_up_dx")
    dh_in, dh_in_bf, dh_in_half, dgain = _norm_bwd(dn, h_in, gain, dh, tm=_pick(lp, [256, 128]), name=f"{tag}_norm_bwd")
    return dh_in, dh_in_bf, dh_in_half, dgain, dw_gu, dw_down


def kernel(x, meta_tokens, ffn1_norm, ffn1_w_gate, ffn1_w_up, ffn1_w_down, mix_norm, w_in, b_forget, fox_q_norm, fox_k_norm, w_branch_fox, w_branch_sb, w_out, ffn2_norm, ffn2_w_gate, ffn2_w_up, ffn2_w_down, loss_target, m_meta_tokens, m_ffn1_norm, m_ffn1_w_gate, m_ffn1_w_up, m_ffn1_w_down, m_mix_norm, m_w_in, m_b_forget, m_fox_q_norm, m_fox_k_norm, m_w_branch_fox, m_w_branch_sb, m_w_out, m_ffn2_norm, m_ffn2_w_gate, m_ffn2_w_up, m_ffn2_w_down, v_meta_tokens, v_ffn1_norm, v_ffn1_w_gate, v_ffn1_w_up, v_ffn1_w_down, v_mix_norm, v_w_in, v_b_forget, v_fox_q_norm, v_fox_k_norm, v_w_branch_fox, v_w_branch_sb, v_w_out, v_ffn2_norm, v_ffn2_w_gate, v_ffn2_w_up, v_ffn2_w_down):
    seq, d = x.shape[1], x.shape[2]
    d8 = d // NDEV
    w = BRANCH_WIDTH
    tile = 256 if seq % 256 == 0 else 128
    pad = tile - N_META
    lp = tile + seq
    tm = _pick(lp, [1088, 544, 256, 128])
    tr = _pick(lp, [256, 128])
    tkr = _pick(lp, [1088, 544, 256, 128])
    nb = lp // tile
    me = _flat_id(*_my_place())

    shards = [jnp.stack([ffn1_w_gate[0], ffn1_w_up[0]]).astype(BF16), ffn1_w_down[0].astype(BF16), w_in[0].astype(BF16),
              w_branch_fox[0].astype(BF16), w_branch_sb[0].astype(BF16), w_out[0].astype(BF16),
              jnp.stack([ffn2_w_gate[0], ffn2_w_up[0]]).astype(BF16), ffn2_w_down[0].astype(BF16)]
    w_gu1, w_down1, w_in_blocks, w_br_fox, w_br_sb, w_out_blocks, w_gu2, w_down2 = _allgather(shards, "gather_weights")
    w_out_full = w_out_blocks.reshape(d, d)
    meta_full = _join_cols(_allgather([meta_tokens], "gather_meta", in_vmem=True)[0])

    wi = _join_cols(w_in_blocks)
    w_pa = jnp.concatenate([wi[:, :2 * w], jnp.pad(wi[:, 3 * w:3 * w + HEADS], ((0, 0), (0, LANES - HEADS)))], axis=1)
    w_pb = jnp.concatenate([wi[:, 2 * w:3 * w], wi[:, 3 * w + HEADS:]], axis=1)
    na, nbw = w_pa.shape[1], w_pb.shape[1]
    gate_blk = 4 * w // d

    h0 = jnp.concatenate([jnp.zeros((pad, d), F32), meta_full.astype(F32), x[0]], axis=0)
    h1, saved1 = _ffn_fwd(h0, ffn1_norm, w_gu1, w_down1, tm=tm, tag="ffn1")

    n2 = _norm_fwd(h1, mix_norm, tm=tr, name="mix_norm")
    tma = _pick(lp, [544, 256, 128])
    tnd = _pick(d, [1024, 512, 256, 128])
    tnb = _pick(nbw, [512, 256, 128])
    proj_a = _mm2d(n2, w_pa, na, dims=NN, tm=tma, tn=na, tk=d, out_dtype=F32, name="proj_a")
    proj_b = _mm2d(n2, w_pb, nbw, dims=NN, tm=tm, tn=tnb, tk=d, out_dtype=BF16, name="proj_b")
    b_pad = jnp.pad(b_forget, ((0, 0), (0, LANES - HEADS)))
    q_gain, k_gain = fox_q_norm.reshape(1, w), fox_k_norm.reshape(1, w)
    fq, fk, logf = _fox_prep(proj_a, q_gain, k_gain, b_pad, tm=tr, pad=pad, name="fox_prep")
    c = _cumsum_rows(logf, tile=tile, reverse=False, name="forget_cumsum")
    c_rows = jnp.transpose(c[:, :HEADS]).reshape(HEADS, nb, 1, tile)
    o_fox, lse = _fox_fwd(fq, fk, proj_b, 0, c_rows, tile=tile, pad=pad, name="fox_fwd")
    o_sb, sb_total = _sb_fwd(proj_b, HEADS, 2 * HEADS, 3 * HEADS, tile=tile, pad=pad, name="sb_fwd")

    def branch(o, w_blocks, name):
        return _mm((lp // tm, NDEV, 1), o, _spec((tm, w), lambda i, j, k: (i, 0)),
                   [(w_blocks, _spec((None, w, d8), lambda i, j, k: (j, 0, 0)))], [],
                   [((lp, d), BF16, _spec((tm, d8), lambda i, j, k: (i, j)))], dims=NN, name=name)[0]

    br_fox = branch(o_fox, w_br_fox, "branch_fox")
    br_sb = branch(o_sb, w_br_sb, "branch_sb")

    def merge_fn(i, tiles, consts):
        bf_, bs_, gf_, gs_ = [t.astype(F32) for t in tiles]
        return [_sigmoid(gf_) * bf_ + _sigmoid(gs_) * bs_], []

    gates_in = [(proj_b, d, gate_blk), (proj_b, d, gate_blk + 1)]
    merged, = _rowwise(merge_fn, [_whole(br_fox), _whole(br_sb)] + gates_in, [], [(d, BF16)], [], tm=tr, name="merge")
    h2 = _mm2d(merged, w_out_full, d, dims=NN, tm=tm, tn=tnd, tk=d, out_dtype=F32,
               epi=lambda accs, ex: [ex[0] + accs[0]], extras=[h1], name="out_proj")

    h3, saved3 = _ffn_fwd(h2, ffn2_norm, w_gu2, w_down2, tm=tm, tag="ffn2")

    skip = tile // tr

    def loss_fn(i, tiles, consts):
        real = i >= skip
        err = jnp.where(real, tiles[0] - tiles[1], 0.0)
        dy = err * (1.0 / d)
        part = 0.5 * jnp.sum(err * dy, axis=0, keepdims=True)
        return [dy, FFN_RESIDUAL_WEIGHT * dy], [part]

    dh3, dh3_half, loss_cols = _rowwise(
        loss_fn, [_whole(h3), _whole(loss_target[0])], [], [(d, F32, lp), (d, BF16, lp)], [(1, d)], tm=tr, name="loss",
        row_maps=[None, lambda i: jnp.maximum(i - skip, 0)])

    dh2, dh2_bf, _, dg_ffn2, dw_gu2, dw_down2 = _ffn_bwd(dh3, dh3_half, h2, ffn2_norm, saved3, w_gu2, w_down2, tm=tm, tag="ffn2")

    dmerged = _mm2d(dh2_bf, w_out_full, d, dims=NT, tm=tm, tn=tnd, tk=d, out_dtype=BF16, name="out_proj_dx")
    dw_out = _mm2d(merged, dh2_bf, d, dims=TN, tm=tnd, tn=tnd, tk=tkr, out_dtype=BF16, name="out_proj_dw")

    def merge_bwd_fn(i, tiles, consts):
        dm, bf_, bs_, gf_, gs_ = [t.astype(F32) for t in tiles]
        sf, ss = _sigmoid(gf_), _sigmoid(gs_)
        return [dm * sf, dm * ss, dm * bf_ * sf * (1.0 - sf), dm * bs_ * ss * (1.0 - ss)], []

    dbr_fox, dbr_sb, dg_fox, dg_sb = _rowwise(
        merge_bwd_fn, [_whole(dmerged), _whole(br_fox), _whole(br_sb)] + gates_in, [], [(d, BF16)] * 4, [], tm=tr, name="merge_bwd")

    tnw = _pick(w, [512, 256, 128])

    def branch_dx(dbr, w_blocks, name):
        return _mm((lp // tm, w // tnw, NDEV), dbr, _spec((tm, d8), lambda i, j, k: (i, k)),
                   [(w_blocks, _spec((None, tnw, d8), lambda i, j, k: (k, j, 0)))], [],
                   [((lp, w), BF16, _spec((tm, tnw), lambda i, j, k: (i, j)))], dims=NT, acc_shape=(tm, tnw), name=name)[0]

    def branch_dw(o, dbr, name):
        return _mm((w // tnw, NDEV, lp // tkr), o, _spec((tkr, tnw), lambda i, j, k: (k, i)),
                   [(dbr, _spec((tkr, d8), lambda i, j, k: (k, j)))], [],
                   [((NDEV, w, d8), BF16, _spec((None, tnw, d8), lambda i, j, k: (j, i, 0)))],
                   dims=TN, acc_shape=(tnw, d8), name=name)[0]

    do_fox = branch_dx(dbr_fox, w_br_fox, "branch_fox_dx")
    do_sb = branch_dx(dbr_sb, w_br_sb, "branch_sb_dx")
    dw_br_fox = branch_dw(o_fox, dbr_fox, "branch_fox_dw")
    dw_br_sb = branch_dw(o_sb, dbr_sb, "branch_sb_dw")

    dfq, dfk, dfv, dc_rows = _fox_bwd(fq, fk, proj_b, 0, c_rows, o_fox, do_fox, lse, tile=tile, pad=pad, name="fox_bwd")
    dsq, dsk, dsv = _sb_bwd(proj_b, HEADS, 2 * HEADS, 3 * HEADS, do_sb, sb_total, tile=tile, pad=pad, name="sb_bwd")
    dc = jnp.pad(jnp.transpose(dc_rows.reshape(HEADS, lp)), ((0, 0), (0, LANES - HEADS)))
    dlogf = _cumsum_rows(dc, tile=tile, reverse=True, name="forget_cumsum_bwd")
    dproj_a, dg_q, dg_k, dg_b = _fox_prep_bwd(proj_a, dfq, dfk, dlogf, q_gain, k_gain, b_pad, tm=tr, pad=pad, name="fox_prep_bwd")
    dproj_b = jnp.concatenate([dfv, dsq, dsk, dsv, dg_fox, dg_sb], axis=1)

    dw_pa = _mm2d(n2, dproj_a, na, dims=TN, tm=tnd, tn=na, tk=_pick(lp, [544, 256, 128]), out_dtype=BF16, name="proj_a_dw")
    dw_pb = _mm2d(n2, dproj_b, nbw, dims=TN, tm=tnd, tn=tnb, tk=tkr, out_dtype=BF16, name="proj_b_dw")
    dn2_a = _mm2d(dproj_a, w_pa, d, dims=NT, tm=tm, tn=tnd, tk=na, out_dtype=F32, name="proj_a_dx")
    dn2 = _mm2d(dproj_b, w_pb, d, dims=NT, tm=tm, tn=tnd, tk=tnb, out_dtype=F32,
                epi=lambda accs, ex: [accs[0] + ex[0]], extras=[dn2_a], name="proj_b_dx")
    dh1, _, dh1_half, dg_mix = _norm_bwd(dn2, h1, mix_norm, dh2, tm=tr, name="mix_norm_bwd")

    dh0, _, _, dg_ffn1, dw_gu1, dw_down1 = _ffn_bwd(dh1, dh1_half, h0, ffn1_norm, saved1, w_gu1, w_down1, tm=tm, tag="ffn1")

    grad_x = dh0[tile:][None]
    dw_in = jnp.concatenate([dw_pa[:, :2 * w], dw_pb[:, :w], dw_pa[:, 2 * w:2 * w + HEADS], dw_pb[:, w:]], axis=1)

    to_send = [dw_gu1, dw_down1, _split_cols(dw_in), dw_br_fox, dw_br_sb, dw_out.reshape(NDEV, d8, d), dw_gu2, dw_down2]
    r_gu1, r_down1, r_in, r_br_fox, r_br_sb, r_out, r_gu2, r_down2 = _exchange_blocks(to_send, "exchange_grads")
    big = dict(ffn1_w_gate=(r_gu1, 0, ffn1_w_gate, m_ffn1_w_gate, v_ffn1_w_gate),
               ffn1_w_up=(r_gu1, 1, ffn1_w_up, m_ffn1_w_up, v_ffn1_w_up),
               ffn1_w_down=(r_down1, 0, ffn1_w_down, m_ffn1_w_down, v_ffn1_w_down),
               w_in=(r_in, 0, w_in, m_w_in, v_w_in),
               w_branch_fox=(r_br_fox, 0, w_branch_fox, m_w_branch_fox, v_w_branch_fox),
               w_branch_sb=(r_br_sb, 0, w_branch_sb, m_w_branch_sb, v_w_branch_sb),
               w_out=(r_out, 0, w_out, m_w_out, v_w_out),
               ffn2_w_gate=(r_gu2, 0, ffn2_w_gate, m_ffn2_w_gate, v_ffn2_w_gate),
               ffn2_w_up=(r_gu2, 1, ffn2_w_up, m_ffn2_w_up, v_ffn2_w_up),
               ffn2_w_down=(r_down2, 0, ffn2_w_down, m_ffn2_w_down, v_ffn2_w_down))
    grads, deltas, new_ms, new_vs = {}, {}, {}, {}
    for k, (parts, sel, wt, mt, vt) in big.items():
        g, dl, mn, vn = _adamw_summed(parts, sel, wt[0], mt[0], vt[0], name=f"adamw_{k}")
        grads[k], deltas[k], new_ms[k], new_vs[k] = g[None], dl[None], mn[None], vn[None]

    small_parts = [dh0[pad:tile], dg_ffn1, dg_mix, dg_ffn2, dg_b[:, :HEADS], dg_q, dg_k, loss_cols]
    small_packed, small_spans = _pack_rows(small_parts, LANES, SMALL_ROWS)
    small_sum = _allsum_small(small_packed, "sum_small")
    g_meta_full, g_ffn1n, g_mixn, g_ffn2n, g_bf, g_qn, g_kn, loss_vec = [
        _unpack(small_sum, span, part.shape) for span, part in zip(small_spans, small_parts)]
    loss = jnp.sum(loss_vec)
    g_meta = lax.dynamic_slice_in_dim(g_meta_full, me * d8, d8, axis=1)
    g_qn, g_kn = g_qn.reshape(fox_q_norm.shape), g_kn.reshape(fox_k_norm.shape)

    small = dict(meta_tokens=(g_meta, meta_tokens, m_meta_tokens, v_meta_tokens),
                 ffn1_norm=(g_ffn1n, ffn1_norm, m_ffn1_norm, v_ffn1_norm),
                 mix_norm=(g_mixn, mix_norm, m_mix_norm, v_mix_norm),
                 b_forget=(g_bf, b_forget, m_b_forget, v_b_forget),
                 fox_q_norm=(g_qn, fox_q_norm, m_fox_q_norm, v_fox_q_norm),
                 fox_k_norm=(g_kn, fox_k_norm, m_fox_k_norm, v_fox_k_norm),
                 ffn2_norm=(g_ffn2n, ffn2_norm, m_ffn2_norm, v_ffn2_norm))
    for k, (g, wt, mt, vt) in small.items():
        flat = lambda t: t.reshape(-1, t.shape[-1])
        dl, mn, vn = _adamw_plain(flat(g), flat(wt), flat(mt), flat(vt), name=f"adamw_{k}")
        grads[k], deltas[k], new_ms[k], new_vs[k] = g, dl.reshape(wt.shape), mn.reshape(wt.shape), vn.reshape(wt.shape)

    order = ["meta_tokens", "ffn1_norm", "ffn1_w_gate", "ffn1_w_up", "ffn1_w_down", "mix_norm", "w_in", "b_forget",
             "fox_q_norm", "fox_k_norm", "w_branch_fox", "w_branch_sb", "w_out", "ffn2_norm", "ffn2_w_gate",
             "ffn2_w_up", "ffn2_w_down"]
    return (loss, grad_x, *[grads[k] for k in order], *[deltas[k] for k in order],
            *[new_ms[k] for k in order], *[new_vs[k] for k in order])
```

```python
import jax
import jax.numpy as jnp
from jax import lax
from jax.experimental import pallas as pl
from jax.experimental.pallas import tpu as pltpu

F32 = jnp.float32
BF16 = jnp.bfloat16
MESH = pl.DeviceIdType.MESH

NDEV = 8
N_META = 16
HEAD_DIM = 128
HEADS = 8
BRANCH_WIDTH = HEADS * HEAD_DIM
RMS_EPS = 1e-6
FFN_RESIDUAL_WEIGHT = 0.5
ATTN_SCALE = HEAD_DIM ** -0.5
MASKED_LOGIT = -1e30

ADAM_LR = 0.001
ADAM_B1 = 0.9
ADAM_B2 = 0.999
ADAM_EPS = 1e-08
ADAM_WD = 0.01
ADAM_STEP = 10

LANES = 128
SMALL_ROWS = 8
ADAMW_TILE_ELEMS = 160 * 1024


def _pick(n, prefs):
    for p in prefs:
        if p <= n and n % p == 0:
            return p
    return n


def _dot(a, b, dims):
    return lax.dot_general(a, b, (dims, ((), ())), preferred_element_type=F32)


NN = ((1,), (0,))
TN = ((0,), (0,))
NT = ((1,), (1,))


def _split_bf16(x):
    hi = x.astype(BF16)
    lo = (x - hi.astype(F32)).astype(BF16)
    return hi, lo


def _sigmoid_parts(z):
    e = jnp.exp(-jnp.abs(z))
    t = 1.0 + e
    return e, 1.0 / t, jnp.log(t)


def _sigmoid(x):
    e = jnp.exp(-jnp.abs(x))
    r = 1.0 / (1.0 + e)
    return jnp.where(x >= 0, r, e * r)


def _my_place():
    return lax.axis_index("x"), lax.axis_index("y"), lax.axis_index("c")


def _flat_id(px, py, pc):
    return 4 * px + 2 * py + pc


def _peer(x, y, c, k):
    px = 1 - x if k & 4 else x
    py = 1 - y if k & 2 else y
    pc = 1 - c if k & 1 else c
    return px, py, pc


def _allgather(shards, name, in_vmem=False):
    n = len(shards)

    def body(*refs):
        x_refs, out_refs = refs[:n], refs[n:2 * n]
        send_sems, recv_sems, local_sems = refs[2 * n:]
        x, y, c = _my_place()
        me, sibling = (x, y, c), (x, y, 1 - c)
        chips = [(1 - x, y), (x, 1 - y), (1 - x, 1 - y)]

        def block(a, place):
            return out_refs[a].at[_flat_id(*place)]

        def copy(a, k, place, to, src=None):
            return pltpu.make_async_remote_copy(
                src_ref=block(a, place) if src is None else src, dst_ref=block(a, place),
                send_sem=send_sems.at[7 * a + k], recv_sem=recv_sems.at[7 * a + k], device_id=to, device_id_type=MESH)

        mine = [pltpu.make_async_copy(x_refs[a], block(a, me), local_sems.at[a]) for a in range(n)]
        for cp in mine:
            cp.start()
        first = []
        for a in range(n):
            first.append(copy(a, 0, me, sibling, src=x_refs[a]))
            first += [copy(a, 1 + j, me, (*chip, c), src=x_refs[a]) for j, chip in enumerate(chips)]
        for cp in first:
            cp.start()
        passed = []
        for j, chip in enumerate(chips):
            for a in range(n):
                copy(a, 1 + j, (*chip, c), me).wait_recv()
                passed.append(copy(a, 4 + j, (*chip, c), sibling))
                passed[-1].start()
        for a in range(n):
            copy(a, 0, sibling, me).wait_recv()
        for j, chip in enumerate(chips):
            for a in range(n):
                copy(a, 4 + j, (*chip, 1 - c), me).wait_recv()
        for cp in first + passed:
            cp.wait_send()
        for cp in mine:
            cp.wait()

    space = pltpu.VMEM if in_vmem else pl.ANY
    return pl.pallas_call(
        body, name=name,
        out_shape=[jax.ShapeDtypeStruct((NDEV,) + s.shape, s.dtype) for s in shards],
        in_specs=[pl.BlockSpec(memory_space=space)] * n,
        out_specs=[pl.BlockSpec(memory_space=space)] * n,
        scratch_shapes=[pltpu.SemaphoreType.DMA((7 * n,)), pltpu.SemaphoreType.DMA((7 * n,)), pltpu.SemaphoreType.DMA((n,))],
    )(*shards)


def _allsum_small(part, name):
    rows, cols = part.shape

    def body(p_ref, out_ref, buf, send_sems, recv_sems):
        x, y, c = _my_place()
        me = _flat_id(x, y, c)
        buf[me] = p_ref[...]
        copies = []
        for k in range(1, NDEV):
            copies.append(pltpu.make_async_remote_copy(
                src_ref=p_ref, dst_ref=buf.at[me], send_sem=send_sems.at[k - 1], recv_sem=recv_sems.at[k - 1],
                device_id=_peer(x, y, c, k), device_id_type=MESH))
        for cp in copies:
            cp.start()
        for cp in copies:
            cp.wait()
        total = buf[0]
        for j in range(1, NDEV):
            total = total + buf[j]
        out_ref[...] = total

    return pl.pallas_call(
        body, name=name,
        out_shape=jax.ShapeDtypeStruct((rows, cols), F32),
        in_specs=[pl.BlockSpec(memory_space=pltpu.VMEM)],
        out_specs=pl.BlockSpec(memory_space=pltpu.VMEM),
        scratch_shapes=[pltpu.VMEM((NDEV, rows, cols), F32),
                        pltpu.SemaphoreType.DMA((7,)), pltpu.SemaphoreType.DMA((7,))],
    )(part)


_HBM = pl.BlockSpec(memory_space=pltpu.HBM)
_SEM = pl.BlockSpec(memory_space=pltpu.SEMAPHORE)
_DATAFLOW = pltpu.SideEffectType.DATAFLOW_SIDE_EFFECTING


def _send_copies(src_refs, land_refs, send_sems, recv_sems, per_peer):
    x, y, c = _my_place()
    me = _flat_id(x, y, c)
    copies = []
    for a, (src_ref, land_ref) in enumerate(zip(src_refs, land_refs)):
        for k in range(1, NDEV):
            peer = _peer(x, y, c, k)
            copies.append(pltpu.make_async_remote_copy(
                src_ref=src_ref.at[_flat_id(*peer)] if per_peer else src_ref, dst_ref=land_ref.at[me],
                send_sem=send_sems.at[7 * a + k - 1], recv_sem=recv_sems.at[7 * a + k - 1],
                device_id=peer, device_id_type=MESH))
    return copies


def _send_start(srcs, lands, per_peer, after, name):
    n = len(srcs)

    def body(*refs):
        for cp in _send_copies(refs[:n], refs[n:2 * n], refs[2 * n + 1], refs[2 * n + 2], per_peer):
            cp.start()
        refs[-1][...] = jnp.zeros_like(refs[-1])

    operands = [pltpu.with_memory_space_constraint(t, pltpu.HBM) for t in list(srcs) + list(lands)]
    outs = pl.pallas_call(
        body, name=name,
        out_shape=(pltpu.SemaphoreType.DMA((7 * n,)), pltpu.SemaphoreType.DMA((7 * n,)),
                   *[pltpu.HBM(t.shape, t.dtype) for t in operands], jax.ShapeDtypeStruct((SMALL_ROWS, LANES), F32)),
        in_specs=[_HBM] * (2 * n) + [pl.BlockSpec(memory_space=pl.ANY)],
        out_specs=(_SEM, _SEM, *[_HBM] * (2 * n), pl.BlockSpec(memory_space=pltpu.VMEM)),
        input_output_aliases={i: 2 + i for i in range(2 * n)},
        compiler_params=pltpu.CompilerParams(has_side_effects=_DATAFLOW),
    )(*operands, after)
    return (outs[0], outs[1], list(outs[2:2 + n]), list(outs[2 + n:2 + 2 * n])), outs[-1]


def _send_wait(handle, per_peer, after, name):
    send_sems, recv_sems, srcs, lands = handle
    n = len(srcs)

    def body(*refs):
        for cp in _send_copies(refs[:n], refs[n:2 * n], refs[2 * n], refs[2 * n + 1], per_peer):
            cp.wait_send()
            cp.wait_recv()

    outs = pl.pallas_call(
        body, name=name,
        out_shape=tuple(pltpu.HBM(t.shape, t.dtype) for t in srcs + lands),
        in_specs=[_HBM] * (2 * n) + [_SEM, _SEM, pl.BlockSpec(memory_space=pl.ANY)],
        out_specs=tuple([_HBM] * (2 * n)),
        input_output_aliases={i: i for i in range(2 * n)},
        compiler_params=pltpu.CompilerParams(has_side_effects=_DATAFLOW),
    )(*srcs, *lands, send_sems, recv_sems, after)
    return list(outs[n:])


def _landing(own_block, me):
    return lax.dynamic_update_index_in_dim(lax.empty((NDEV,) + own_block.shape, own_block.dtype), own_block, me, 0)


def _spec(block, index_map):
    return pl.BlockSpec(block, index_map)


def _mm(grid, a, a_spec, bs, extras, outs, *, dims, acc_shape=None, epi=None, after=None, name):
    nk = grid[2]
    nb, ne, no = len(bs), len(extras), len(outs)
    nafter = 0 if after is None else 1
    if epi is None:
        epi = lambda accs, ex: [accs[0]]

    def body(*refs):
        a_ref, b_refs = refs[0], refs[1:1 + nb]
        e_refs = refs[1 + nb:1 + nb + ne]
        o_refs = refs[1 + nb + ne + nafter:1 + nb + ne + nafter + no]
        acc_refs = refs[1 + nb + ne + nafter + no:]
        av = a_ref[...]
        parts = [_dot(av, b_ref[...], dims) for b_ref in b_refs]

        def finish(accs):
            for o_ref, tile in zip(o_refs, epi(accs, [e_ref[...] for e_ref in e_refs])):
                o_ref[...] = tile.astype(o_ref.dtype)

        if nk == 1:
            finish(parts)
        else:
            k = pl.program_id(2)

            @pl.when(k == 0)
            def _():
                for acc_ref, part in zip(acc_refs, parts):
                    acc_ref[...] = part

            @pl.when(k > 0)
            def _():
                for acc_ref, part in zip(acc_refs, parts):
                    acc_ref[...] += part

            @pl.when(k == nk - 1)
            def _():
                finish([acc_ref[...] for acc_ref in acc_refs])

    return pl.pallas_call(
        body, name=name,
        grid=grid,
        in_specs=[a_spec] + [s for _, s in bs] + [s for _, s in extras] + [pl.BlockSpec(memory_space=pl.ANY)] * nafter,
        out_specs=[s for _, _, s in outs],
        out_shape=[jax.ShapeDtypeStruct(shape, dt) for shape, dt, _ in outs],
        scratch_shapes=[pltpu.VMEM(acc_shape, F32) for _ in bs] if nk > 1 else [],
        compiler_params=pltpu.CompilerParams(dimension_semantics=("parallel", "parallel", "arbitrary")),
    )(a, *[b for b, _ in bs], *[e for e, _ in extras], *([after] if nafter else []))


def _mm2d(a, b, n_cols, *, dims, tm, tn, tk, out_dtype, epi=None, extras=(), name):
    m_rows, k_len = (a.shape[1], a.shape[0]) if dims == TN else a.shape
    assert m_rows % tm == 0 and n_cols % tn == 0 and k_len % tk == 0, (name, a.shape, n_cols, tm, tn, tk)
    a_spec = _spec((tk, tm), lambda i, j, k: (k, i)) if dims == TN else _spec((tm, tk), lambda i, j, k: (i, k))
    b_spec = _spec((tn, tk), lambda i, j, k: (j, k)) if dims == NT else _spec((tk, tn), lambda i, j, k: (k, j))
    tile = _spec((tm, tn), lambda i, j, k: (i, j))
    return _mm((m_rows // tm, n_cols // tn, k_len // tk), a, a_spec, [(b, b_spec)], [(e, tile) for e in extras],
               [((m_rows, n_cols), out_dtype, tile)], dims=dims, acc_shape=(tm, tn), epi=epi, name=name)[0]


def _rowwise(fn, ins, consts, outs, sums, *, tm, name, row_maps=None, after=None):
    m_rows = outs[0][2] if len(outs[0]) == 3 else ins[0][0].shape[0]
    n = m_rows // tm
    ni, nc, no = len(ins), len(consts), len(outs)
    nafter = 0 if after is None else 1
    row_maps = row_maps or [None] * ni

    def body(*refs):
        i = pl.program_id(0)
        in_tiles = [r[...] for r in refs[:ni]]
        const_values = [r[...] for r in refs[ni:ni + nc]]
        o_refs = refs[ni + nc + nafter:ni + nc + nafter + no]
        s_refs = refs[ni + nc + nafter + no:]
        out_tiles, sum_terms = fn(i, in_tiles, const_values)
        for o_ref, tile in zip(o_refs, out_tiles):
            o_ref[...] = tile.astype(o_ref.dtype)
        if s_refs:
            @pl.when(i == 0)
            def _():
                for s_ref in s_refs:
                    s_ref[...] = jnp.zeros_like(s_ref)

            for s_ref, term in zip(s_refs, sum_terms):
                s_ref[...] += term

    def in_spec(width, col, rmap):
        if rmap is None:
            return pl.BlockSpec((tm, width), lambda i: (i, col))
        return pl.BlockSpec((tm, width), lambda i: (rmap(i), col))

    return pl.pallas_call(
        body, name=name,
        grid=(n,),
        in_specs=[in_spec(w, col, rmap) for (_, w, col), rmap in zip(ins, row_maps)]
        + [pl.BlockSpec(cst.shape, lambda i: (0, 0)) for cst in consts] + [pl.BlockSpec(memory_space=pl.ANY)] * nafter,
        out_specs=[pl.BlockSpec((tm, o[0]), lambda i: (i, 0)) for o in outs]
        + [pl.BlockSpec(s, lambda i: (0, 0)) for s in sums],
        out_shape=[jax.ShapeDtypeStruct((m_rows, o[0]), o[1]) for o in outs]
        + [jax.ShapeDtypeStruct(s, F32) for s in sums],
        compiler_params=pltpu.CompilerParams(dimension_semantics=("arbitrary",)),
    )(*[arr for arr, _, _ in ins], *consts, *([after] if nafter else []))


def _whole(arr):
    return (arr, arr.shape[1], 0)


def _rms(x, gain):
    r = lax.rsqrt(jnp.mean(x * x, axis=-1, keepdims=True) + RMS_EPS)
    return x * r * gain


def _rms_bwd(x, gain, dy):
    r = lax.rsqrt(jnp.mean(x * x, axis=-1, keepdims=True) + RMS_EPS)
    u = dy * gain
    dx = r * u - x * (r * r * r) * jnp.mean(x * u, axis=-1, keepdims=True)
    return dx, dy * x * r


def _norm_fwd(h, gain, *, tm, name):
    def fn(i, tiles, consts):
        return [_rms(tiles[0], consts[0])], []
    return _rowwise(fn, [_whole(h)], [gain], [(h.shape[1], BF16)], [], tm=tm, name=name)[0]


def _norm_bwd(dn, h, gain, dh_in, *, tm, name, after=None):
    d = h.shape[1]

    def fn(i, tiles, consts):
        dx, dg_rows = _rms_bwd(tiles[1], consts[0], tiles[0])
        dh = tiles[2] + dx
        return [dh, dh, FFN_RESIDUAL_WEIGHT * dh], [jnp.sum(dg_rows, axis=0, keepdims=True)]

    return _rowwise(fn, [_whole(dn), _whole(h), _whole(dh_in)], [gain],
                    [(d, F32), (d, BF16), (d, BF16)], [(1, d)], tm=tm, name=name, after=after)


def _swiglu_epi(accs, ex):
    a, b = accs
    return [a, b, a * _sigmoid(a) * b]


def _swiglu_bwd_epi(accs, ex):
    ds = accs[0]
    a, b = ex[0].astype(F32), ex[1].astype(F32)
    sig = _sigmoid(a)
    silu = a * sig
    dsilu = sig * (1.0 + a * (1.0 - sig))
    return [jnp.stack([(ds * b * dsilu).astype(BF16), (ds * silu).astype(BF16)], axis=0)]


def _attn_mask(i, j, tile, pad, strict):
    row = i * tile + lax.broadcasted_iota(jnp.int32, (tile, tile), 0)
    col = j * tile + lax.broadcasted_iota(jnp.int32, (tile, tile), 1)
    causal = (col < row) if strict else (col <= row)
    return causal & ((col >= pad) | (row < pad))


def _key_tile(ref, j, tile):
    return ref[pl.ds(pl.multiple_of(j * tile, tile), tile), :]


def _head_specs(lp, tile, q_off, k_off, v_off):
    q_spec = pl.BlockSpec((tile, HEAD_DIM), lambda h, i: (i, h + q_off))
    k_spec = pl.BlockSpec((lp, HEAD_DIM), lambda h, i: (0, h + k_off))
    v_spec = pl.BlockSpec((lp, HEAD_DIM), lambda h, i: (0, h + v_off))
    return q_spec, k_spec, v_spec


def _fox_fwd(q, k, v, v_off, c_rows, *, tile, pad, name):
    lp = q.shape[0]
    nb = lp // tile

    def body(q_ref, k_ref, v_ref, c_ref, o_ref, lse_ref):
        i = pl.program_id(1)
        qt = q_ref[...]

        def step(j, carry):
            m, l, acc = carry
            s = _dot(qt, _key_tile(k_ref, j, tile), NT) * ATTN_SCALE - c_ref[0, j]
            s = jnp.where(_attn_mask(i, j, tile, pad, False), s, MASKED_LOGIT)
            m_new = jnp.maximum(m, jnp.max(s, axis=1, keepdims=True))
            p = jnp.exp(s - m_new)
            alpha = jnp.exp(m - m_new)
            l = alpha * l + jnp.sum(p, axis=1, keepdims=True)
            acc = alpha * acc + _dot(p.astype(BF16), _key_tile(v_ref, j, tile), NN)
            return m_new, l, acc

        init = (jnp.full((tile, 1), MASKED_LOGIT, F32), jnp.zeros((tile, 1), F32), jnp.zeros((tile, HEAD_DIM), F32))
        m, l, acc = lax.fori_loop(0, i + 1, step, init)
        o_ref[...] = (acc / l).astype(o_ref.dtype)
        lse_ref[0] = jnp.broadcast_to(m + jnp.log(l), (tile, LANES))

    q_spec, k_spec, v_spec = _head_specs(lp, tile, 0, 0, v_off)
    return pl.pallas_call(
        body, name=name,
        grid=(HEADS, nb),
        in_specs=[q_spec, k_spec, v_spec, pl.BlockSpec((1, nb, 1, tile), lambda h, i: (h, 0, 0, 0))],
        out_specs=[pl.BlockSpec((tile, HEAD_DIM), lambda h, i: (i, h)),
                   pl.BlockSpec((1, tile, LANES), lambda h, i: (h, i, 0))],
        out_shape=[jax.ShapeDtypeStruct((lp, BRANCH_WIDTH), BF16), jax.ShapeDtypeStruct((HEADS, lp, LANES), F32)],
        compiler_params=pltpu.CompilerParams(dimension_semantics=("parallel", "arbitrary")),
    )(q, k, v, c_rows)


def _fox_bwd(q, k, v, v_off, c_rows, o, do, lse, *, tile, pad, name):
    lp = q.shape[0]
    nb = lp // tile

    def body(q_ref, k_ref, v_ref, c_ref, o_ref, do_ref, lse_ref, dq_ref, dk_ref, dv_ref, dc_ref, dk_acc, dv_acc, dc_acc):
        i = pl.program_id(1)

        @pl.when(i == 0)
        def _():
            dk_acc[...] = jnp.zeros_like(dk_acc)
            dv_acc[...] = jnp.zeros_like(dv_acc)
            dc_acc[...] = jnp.zeros_like(dc_acc)

        qt, dot_ = q_ref[...], do_ref[...]
        delta = jnp.sum(dot_.astype(F32) * o_ref[...].astype(F32), axis=1, keepdims=True)
        lse_col = lse_ref[0][:, :1]

        def step(j, dq):
            kt = _key_tile(k_ref, j, tile)
            rows = pl.ds(pl.multiple_of(j * tile, tile), tile)
            s = _dot(qt, kt, NT) * ATTN_SCALE - c_ref[0, j]
            p = jnp.where(_attn_mask(i, j, tile, pad, False), jnp.exp(s - lse_col), 0.0)
            dp = _dot(dot_, _key_tile(v_ref, j, tile), NT)
            ds = p * (dp - delta)
            dsb = ds.astype(BF16)
            dk_acc[rows, :] += _dot(dsb, qt, TN)
            dv_acc[rows, :] += _dot(p.astype(BF16), dot_, TN)
            dc_acc[j] += -jnp.sum(ds, axis=0, keepdims=True)
            return dq + _dot(dsb, kt, NN)

        dq = lax.fori_loop(0, i + 1, step, jnp.zeros((tile, HEAD_DIM), F32))
        dq_ref[...] = dq * ATTN_SCALE

        @pl.when(i == nb - 1)
        def _():
            dk_ref[...] = dk_acc[...] * ATTN_SCALE
            dv_ref[...] = dv_acc[...].astype(dv_ref.dtype)
            dc_ref[0] = dc_acc[...]

    q_spec, k_spec, v_spec = _head_specs(lp, tile, 0, 0, v_off)
    tile_spec = pl.BlockSpec((tile, HEAD_DIM), lambda h, i: (i, h))
    head_spec = pl.BlockSpec((lp, HEAD_DIM), lambda h, i: (0, h))
    c_spec = pl.BlockSpec((1, nb, 1, tile), lambda h, i: (h, 0, 0, 0))
    return pl.pallas_call(
        body, name=name,
        grid=(HEADS, nb),
        in_specs=[q_spec, k_spec, v_spec, c_spec, tile_spec, tile_spec,
                  pl.BlockSpec((1, tile, LANES), lambda h, i: (h, i, 0))],
        out_specs=[tile_spec, head_spec, head_spec, c_spec],
        out_shape=[jax.ShapeDtypeStruct((lp, BRANCH_WIDTH), F32), jax.ShapeDtypeStruct((lp, BRANCH_WIDTH), F32),
                   jax.ShapeDtypeStruct((lp, BRANCH_WIDTH), BF16), jax.ShapeDtypeStruct((HEADS, nb, 1, tile), F32)],
        scratch_shapes=[pltpu.VMEM((lp, HEAD_DIM), F32), pltpu.VMEM((lp, HEAD_DIM), F32), pltpu.VMEM((nb, 1, tile), F32)],
        compiler_params=pltpu.CompilerParams(dimension_semantics=("parallel", "arbitrary")),
    )(q, k, v, c_rows, o, do, lse)


def _later_matrix(tile):
    return (lax.broadcasted_iota(jnp.int32, (tile, tile), 0) > lax.broadcasted_iota(jnp.int32, (tile, tile), 1)).astype(BF16)


def _earlier_matrix(tile):
    return (lax.broadcasted_iota(jnp.int32, (tile, tile), 0) < lax.broadcasted_iota(jnp.int32, (tile, tile), 1)).astype(BF16)


def _sb_tile(qt, kt, ok, later):
    z = _dot(qt, kt, NT) * ATTN_SCALE
    e, r, lg = _sigmoid_parts(z)
    sp = jnp.maximum(z, 0.0) + lg
    spm = jnp.where(ok, sp, 0.0)
    hi, lo = _split_bf16(spm)
    within = _dot(hi, later, NN) + _dot(lo, later, NN)
    return z, e, r, sp, spm, within


def _sb_fwd(qkv, q_off, k_off, v_off, *, tile, pad, name):
    lp = qkv.shape[0]
    nb = lp // tile

    def body(q_ref, k_ref, v_ref, o_ref, tot_ref):
        i = pl.program_id(1)
        qt = q_ref[...]
        later = _later_matrix(tile)

        def step(t, carry):
            right, acc = carry
            j = i - t
            ok = _attn_mask(i, j, tile, pad, True)
            z, _, _, sp, spm, within = _sb_tile(qt, _key_tile(k_ref, j, tile), ok, later)
            w = jnp.where(ok, jnp.exp(z - sp - within - right), 0.0)
            acc = acc + _dot(w.astype(BF16), _key_tile(v_ref, j, tile), NN)
            return right + jnp.sum(spm, axis=1, keepdims=True), acc

        total, acc = lax.fori_loop(0, i + 1, step, (jnp.zeros((tile, 1), F32), jnp.zeros((tile, HEAD_DIM), F32)))
        o_ref[...] = acc.astype(o_ref.dtype)
        tot_ref[0] = jnp.broadcast_to(total, (tile, LANES))

    q_spec, k_spec, v_spec = _head_specs(lp, tile, q_off, k_off, v_off)
    return pl.pallas_call(
        body, name=name,
        grid=(HEADS, nb),
        in_specs=[q_spec, k_spec, v_spec],
        out_specs=[pl.BlockSpec((tile, HEAD_DIM), lambda h, i: (i, h)),
                   pl.BlockSpec((1, tile, LANES), lambda h, i: (h, i, 0))],
        out_shape=[jax.ShapeDtypeStruct((lp, BRANCH_WIDTH), BF16), jax.ShapeDtypeStruct((HEADS, lp, LANES), F32)],
        compiler_params=pltpu.CompilerParams(dimension_semantics=("parallel", "arbitrary")),
    )(qkv, qkv, qkv)


def _sb_bwd(qkv, q_off, k_off, v_off, do, total, *, tile, pad, name):
    lp = qkv.shape[0]
    nb = lp // tile

    def body(q_ref, k_ref, v_ref, do_ref, tot_ref, dq_ref, dk_ref, dv_ref, dk_acc, dv_acc):
        i = pl.program_id(1)

        @pl.when(i == 0)
        def _():
            dk_acc[...] = jnp.zeros_like(dk_acc)
            dv_acc[...] = jnp.zeros_like(dv_acc)

        qt, dot_ = q_ref[...], do_ref[...]
        total_col = tot_ref[0][:, :1]
        later, earlier = _later_matrix(tile), _earlier_matrix(tile)

        def step(j, carry):
            dq, sp_before, dlw_before = carry
            kt = _key_tile(k_ref, j, tile)
            rows = pl.ds(pl.multiple_of(j * tile, tile), tile)
            ok = _attn_mask(i, j, tile, pad, True)
            z, e, r, sp, spm, within = _sb_tile(qt, kt, ok, later)
            sp_here = jnp.sum(spm, axis=1, keepdims=True)
            right = total_col - sp_before - sp_here
            w = jnp.where(ok, jnp.exp(z - sp - within - right), 0.0)
            dlw = w * _dot(dot_, _key_tile(v_ref, j, tile), NT)
            hi, lo = _split_bf16(dlw)
            before = dlw_before + _dot(hi, earlier, NN) + _dot(lo, earlier, NN)
            sig = jnp.where(z >= 0, r, e * r)
            dz = jnp.where(ok, dlw * (1.0 - sig) - sig * before, 0.0)
            dzb = dz.astype(BF16)
            dk_acc[rows, :] += _dot(dzb, qt, TN)
            dv_acc[rows, :] += _dot(w.astype(BF16), dot_, TN)
            return dq + _dot(dzb, kt, NN), sp_before + sp_here, dlw_before + jnp.sum(dlw, axis=1, keepdims=True)

        zero_col = jnp.zeros((tile, 1), F32)
        dq, _, _ = lax.fori_loop(0, i + 1, step, (jnp.zeros((tile, HEAD_DIM), F32), zero_col, zero_col))
        dq_ref[...] = (dq * ATTN_SCALE).astype(dq_ref.dtype)

        @pl.when(i == nb - 1)
        def _():
            dk_ref[...] = (dk_acc[...] * ATTN_SCALE).astype(dk_ref.dtype)
            dv_ref[...] = dv_acc[...].astype(dv_ref.dtype)

    q_spec, k_spec, v_spec = _head_specs(lp, tile, q_off, k_off, v_off)
    tile_spec = pl.BlockSpec((tile, HEAD_DIM), lambda h, i: (i, h))
    head_spec = pl.BlockSpec((lp, HEAD_DIM), lambda h, i: (0, h))
    return pl.pallas_call(
        body, name=name,
        grid=(HEADS, nb),
        in_specs=[q_spec, k_spec, v_spec, tile_spec, pl.BlockSpec((1, tile, LANES), lambda h, i: (h, i, 0))],
        out_specs=[tile_spec, head_spec, head_spec],
        out_shape=[jax.ShapeDtypeStruct((lp, BRANCH_WIDTH), BF16)] * 3,
        scratch_shapes=[pltpu.VMEM((lp, HEAD_DIM), F32), pltpu.VMEM((lp, HEAD_DIM), F32)],
        compiler_params=pltpu.CompilerParams(dimension_semantics=("parallel", "arbitrary")),
    )(qkv, qkv, qkv, do, total)


def _cumsum_rows(x, *, tile, reverse, name):
    lp = x.shape[0]
    nb = lp // tile

    def body(x_ref, o_ref, carry):
        @pl.when(pl.program_id(0) == 0)
        def _():
            carry[...] = jnp.zeros_like(carry)

        r = lax.broadcasted_iota(jnp.int32, (tile, tile), 0)
        c = lax.broadcasted_iota(jnp.int32, (tile, tile), 1)
        tri = ((c >= r) if reverse else (c <= r)).astype(BF16)
        hi, lo = _split_bf16(x_ref[...])
        run = _dot(tri, hi, NN) + _dot(tri, lo, NN) + carry[...]
        o_ref[...] = run
        carry[...] = run[:1, :] if reverse else run[tile - 1:, :]

    order = (lambda i: (nb - 1 - i, 0)) if reverse else (lambda i: (i, 0))
    return pl.pallas_call(
        body, name=name,
        grid=(nb,),
        in_specs=[pl.BlockSpec((tile, LANES), order)],
        out_specs=pl.BlockSpec((tile, LANES), order),
        out_shape=jax.ShapeDtypeStruct((lp, LANES), F32),
        scratch_shapes=[pltpu.VMEM((1, LANES), F32)],
        compiler_params=pltpu.CompilerParams(dimension_semantics=("arbitrary",)),
    )(x)


def _log_sigmoid(x):
    return jnp.minimum(x, 0.0) - jnp.log(1.0 + jnp.exp(-jnp.abs(x)))


def _forget_mask(i, tm, pad):
    row = i * tm + lax.broadcasted_iota(jnp.int32, (tm, LANES), 0)
    lane = lax.broadcasted_iota(jnp.int32, (tm, LANES), 1)
    return (row >= pad) & (lane < HEADS)


def _fox_prep(proj_a, q_gain, k_gain, b_forget, *, tm, pad, name):
    w = BRANCH_WIDTH

    def fn(i, tiles, consts):
        pa = tiles[0]
        qs, ks = [], []
        for h in range(HEADS):
            lo = h * HEAD_DIM
            qs.append(_rms(pa[:, lo:lo + HEAD_DIM], consts[0][:, lo:lo + HEAD_DIM]))
            ks.append(_rms(pa[:, w + lo:w + lo + HEAD_DIM], consts[1][:, lo:lo + HEAD_DIM]))
        logf = jnp.where(_forget_mask(i, tm, pad), _log_sigmoid(pa[:, 2 * w:] + consts[2]), 0.0)
        return [jnp.concatenate(qs, axis=1), jnp.concatenate(ks, axis=1), logf], []

    return _rowwise(fn, [_whole(proj_a)], [q_gain, k_gain, b_forget],
                    [(w, BF16), (w, BF16), (LANES, F32)], [], tm=tm, name=name)


def _fox_prep_bwd(proj_a, dq, dk, dlogf, q_gain, k_gain, b_forget, *, tm, pad, name):
    w = BRANCH_WIDTH

    def fn(i, tiles, consts):
        pa, dqt, dkt, dlf = tiles
        dxs_q, dxs_k, dgs_q, dgs_k = [], [], [], []
        for h in range(HEADS):
            lo = h * HEAD_DIM
            dx, dg = _rms_bwd(pa[:, lo:lo + HEAD_DIM], consts[0][:, lo:lo + HEAD_DIM], dqt[:, lo:lo + HEAD_DIM])
            dxs_q.append(dx)
            dgs_q.append(jnp.sum(dg, axis=0, keepdims=True))
            dx, dg = _rms_bwd(pa[:, w + lo:w + lo + HEAD_DIM], consts[1][:, lo:lo + HEAD_DIM], dkt[:, lo:lo + HEAD_DIM])
            dxs_k.append(dx)
            dgs_k.append(jnp.sum(dg, axis=0, keepdims=True))
        xf = pa[:, 2 * w:] + consts[2]
        e, r, _ = _sigmoid_parts(xf)
        df = jnp.where(_forget_mask(i, tm, pad), dlf * jnp.where(xf >= 0, e * r, r), 0.0)
        return ([jnp.concatenate(dxs_q + dxs_k + [df], axis=1)],
                [jnp.concatenate(dgs_q, axis=1), jnp.concatenate(dgs_k, axis=1), jnp.sum(df, axis=0, keepdims=True)])

    return _rowwise(fn, [_whole(proj_a), _whole(dq), _whole(dk), _whole(dlogf)], [q_gain, k_gain, b_forget],
                    [(2 * w + LANES, BF16)], [(1, w), (1, w), (1, LANES)], tm=tm, name=name)


def _adamw_math(w, g, m, v):
    m = ADAM_B1 * m + (1.0 - ADAM_B1) * g
    v = ADAM_B2 * v + (1.0 - ADAM_B2) * (g * g)
    m_hat = m / (1.0 - ADAM_B1 ** ADAM_STEP)
    v_hat = v / (1.0 - ADAM_B2 ** ADAM_STEP)
    delta = -ADAM_LR * (m_hat / (jnp.sqrt(v_hat) + ADAM_EPS) + ADAM_WD * w)
    return delta, m, v


def _adamw_summed(parts, sel, w, m, v, *, name):
    rows, cols = w.shape
    tr = _pick(rows, [t for t in (512, 256, 128, 64, 32, 16, 8) if t * cols <= ADAMW_TILE_ELEMS])

    def body(p_ref, w_ref, m_ref, v_ref, g_out, d_out, m_out, v_out):
        g = p_ref[0].astype(F32)
        for j in range(1, NDEV):
            g = g + p_ref[j].astype(F32)
        delta, m_new, v_new = _adamw_math(w_ref[...], g, m_ref[...], v_ref[...])
        g_out[...] = g
        d_out[...] = delta
        m_out[...] = m_new
        v_out[...] = v_new

    spec = pl.BlockSpec((tr, cols), lambda i: (i, 0))
    if parts.ndim == 3:
        p_spec = pl.BlockSpec((NDEV, tr, cols), lambda i: (0, i, 0))
    else:
        p_spec = pl.BlockSpec((NDEV, None, tr, cols), lambda i: (0, sel, i, 0))
    return pl.pallas_call(
        body, name=name,
        grid=(rows // tr,),
        in_specs=[p_spec, spec, spec, spec],
        out_specs=[spec] * 4,
        out_shape=[jax.ShapeDtypeStruct((rows, cols), F32)] * 4,
        compiler_params=pltpu.CompilerParams(dimension_semantics=("parallel",)),
    )(parts, w, m, v)


def _adamw_plain(g, w, m, v, *, name):
    def body(g_ref, w_ref, m_ref, v_ref, d_out, m_out, v_out):
        delta, m_new, v_new = _adamw_math(w_ref[...], g_ref[...], m_ref[...], v_ref[...])
        d_out[...] = delta
        m_out[...] = m_new
        v_out[...] = v_new

    return pl.pallas_call(body, name=name, out_shape=[jax.ShapeDtypeStruct(w.shape, F32)] * 3)(g, w, m, v)


def _pack_rows(arrays, width, row_align):
    pieces, spans, at = [], [], 0
    for arr in arrays:
        flat = arr.reshape(-1)
        rows = -(-flat.shape[0] // (width * row_align)) * row_align
        flat = jnp.pad(flat, (0, rows * width - flat.shape[0]))
        pieces.append(flat.reshape(rows, width))
        spans.append((at, rows))
        at += rows
    return jnp.concatenate(pieces, axis=0), spans


def _unpack(rows2d, span, shape):
    at, rows = span
    size = 1
    for s in shape:
        size *= s
    return rows2d[at:at + rows].reshape(-1)[:size].reshape(shape)


def _join_cols(blocks):
    n, rows, cols = blocks.shape
    return jnp.transpose(blocks, (1, 0, 2)).reshape(rows, n * cols)


def _split_cols(full):
    rows, cols = full.shape
    return jnp.transpose(full.reshape(rows, NDEV, cols // NDEV), (1, 0, 2))


def _ffn_up(h, gain, w_gu, after, *, tm, tag):
    lp, d = h.shape
    f8 = w_gu.shape[3]
    n = _norm_fwd(h, gain, tm=_pick(lp, [256, 128]), name=f"{tag}_norm")
    hid = _spec((None, tm, f8), lambda i, j, k: (j, i, 0))
    a, b, s = _mm((lp // tm, NDEV, 1), n, _spec((tm, d), lambda i, j, k: (i, 0)),
                  [(w_gu, _spec((None, None, d, f8), lambda i, j, k: (j, 0, 0, 0))),
                   (w_gu, _spec((None, None, d, f8), lambda i, j, k: (j, 1, 0, 0)))],
                  [], [((NDEV, lp, f8), BF16, hid)] * 3, dims=NN, epi=_swiglu_epi, after=after, name=f"{tag}_up")
    return n, a, b, s


def _ffn_down(h, s, w_down, *, tm, tag):
    lp, d = h.shape
    f8 = w_down.shape[1]
    tn = _pick(d, [1024, 512, 256, 128])
    tile = _spec((tm, tn), lambda i, j, k: (i, j))
    return _mm((lp // tm, d // tn, NDEV), s, _spec((None, tm, f8), lambda i, j, k: (k, i, 0)),
               [(w_down, _spec((None, f8, tn), lambda i, j, k: (k, 0, j)))], [(h, tile)], [((lp, d), F32, tile)],
               dims=NN, acc_shape=(tm, tn), epi=lambda accs, ex: [ex[0] + FFN_RESIDUAL_WEIGHT * accs[0]], name=f"{tag}_down")[0]


def _ffn_bwd_dw(dh_half, saved, w_down, *, tm, tag):
    n, a, b, s = saved
    lp, d = dh_half.shape
    f8 = w_down.shape[1]
    tkr = _pick(lp, [1088, 544, 256, 128])
    tn = _pick(d, [1024, 512, 256, 128])
    hid = _spec((None, tm, f8), lambda i, j, k: (j, i, 0))
    dab, = _mm((lp // tm, NDEV, 1), dh_half, _spec((tm, d), lambda i, j, k: (i, 0)),
               [(w_down, _spec((None, f8, d), lambda i, j, k: (j, 0, 0)))], [(a, hid), (b, hid)],
               [((2, NDEV, lp, f8), BF16, _spec((2, None, tm, f8), lambda i, j, k: (0, j, i, 0)))],
               dims=NT, epi=_swiglu_bwd_epi, name=f"{tag}_down_dx")
    dw_down, = _mm((NDEV, d // tn, lp // tkr), s, _spec((None, tkr, f8), lambda i, j, k: (i, k, 0)),
                   [(dh_half, _spec((tkr, tn), lambda i, j, k: (k, j)))], [],
                   [((NDEV, f8, d), BF16, _spec((None, f8, tn), lambda i, j, k: (i, 0, j)))],
                   dims=TN, acc_shape=(f8, tn), name=f"{tag}_down_dw")
    dw_gu, = _mm((d // tn, 2 * NDEV, lp // tkr), n, _spec((tkr, tn), lambda i, j, k: (k, i)),
                 [(dab, _spec((None, None, tkr, f8), lambda i, j, k: (j // NDEV, j % NDEV, k, 0)))], [],
                 [((NDEV, 2, d, f8), BF16, _spec((None, None, tn, f8), lambda i, j, k: (j % NDEV, j // NDEV, i, 0)))],
                 dims=TN, acc_shape=(tn, f8), name=f"{tag}_gate_up_dw")
    return dab, dw_gu, dw_down


def _ffn_bwd_dx(dab, w_gu, after, *, tm, tag):
    lp, f8 = dab.shape[2], dab.shape[3]
    d = w_gu.shape[2]
    tn = _pick(d, [1024, 512, 256, 128])
    return _mm((lp // tm, d // tn, 2 * NDEV), dab, _spec((None, None, tm, f8), lambda i, j, k: (k // NDEV, k % NDEV, i, 0)),
               [(w_gu, _spec((None, None, tn, f8), lambda i, j, k: (k % NDEV, k // NDEV, j, 0)))], [],
               [((lp, d), F32, _spec((tm, tn), lambda i, j, k: (i, j)))],
               dims=NT, acc_shape=(tm, tn), after=after, name=f"{tag}_gate_up_dx")[0]


def kernel(x, meta_tokens, ffn1_norm, ffn1_w_gate, ffn1_w_up, ffn1_w_down, mix_norm, w_in, b_forget, fox_q_norm, fox_k_norm, w_branch_fox, w_branch_sb, w_out, ffn2_norm, ffn2_w_gate, ffn2_w_up, ffn2_w_down, loss_target, m_meta_tokens, m_ffn1_norm, m_ffn1_w_gate, m_ffn1_w_up, m_ffn1_w_down, m_mix_norm, m_w_in, m_b_forget, m_fox_q_norm, m_fox_k_norm, m_w_branch_fox, m_w_branch_sb, m_w_out, m_ffn2_norm, m_ffn2_w_gate, m_ffn2_w_up, m_ffn2_w_down, v_meta_tokens, v_ffn1_norm, v_ffn1_w_gate, v_ffn1_w_up, v_ffn1_w_down, v_mix_norm, v_w_in, v_b_forget, v_fox_q_norm, v_fox_k_norm, v_w_branch_fox, v_w_branch_sb, v_w_out, v_ffn2_norm, v_ffn2_w_gate, v_ffn2_w_up, v_ffn2_w_down):
    seq, d = x.shape[1], x.shape[2]
    d8 = d // NDEV
    w = BRANCH_WIDTH
    tile = 256 if seq % 256 == 0 else 128
    pad = tile - N_META
    lp = tile + seq
    tm = _pick(lp, [1088, 544, 256, 128])
    tr = _pick(lp, [256, 128])
    tkr = _pick(lp, [1088, 544, 256, 128])
    nb = lp // tile
    me = _flat_id(*_my_place())

    shards = [jnp.stack([ffn1_w_gate[0], ffn1_w_up[0]]).astype(BF16), ffn1_w_down[0].astype(BF16), w_in[0].astype(BF16),
              w_branch_fox[0].astype(BF16), w_branch_sb[0].astype(BF16), w_out[0].astype(BF16),
              jnp.stack([ffn2_w_gate[0], ffn2_w_up[0]]).astype(BF16), ffn2_w_down[0].astype(BF16)]
    w_gu1, = _allgather(shards[:1], "gather_ffn1")
    down1_copies, token = _send_start(shards[1:2], [_landing(s, me) for s in shards[1:2]], False, w_gu1, "gather_down1_start")
    mixer_copies, token = _send_start(shards[2:6], [_landing(s, me) for s in shards[2:6]], False, token, "gather_mixer_start")
    ffn2_copies, token = _send_start(shards[6:], [_landing(s, me) for s in shards[6:]], False, token, "gather_ffn2_start")
    meta_full = _join_cols(_allgather([meta_tokens], "gather_meta", in_vmem=True)[0])

    h0 = jnp.concatenate([jnp.zeros((pad, d), F32), meta_full.astype(F32), x[0]], axis=0)
    saved1 = _ffn_up(h0, ffn1_norm, w_gu1, token, tm=tm, tag="ffn1")
    w_down1, = _send_wait(down1_copies, False, saved1[3], "gather_down1_wait")
    h1 = _ffn_down(h0, saved1[3], w_down1, tm=tm, tag="ffn1")

    w_in_blocks, w_br_fox, w_br_sb, w_out_blocks = _send_wait(mixer_copies, False, h1, "gather_mixer_wait")
    w_out_full = w_out_blocks.reshape(d, d)
    wi = _join_cols(w_in_blocks)
    w_pa = jnp.concatenate([wi[:, :2 * w], jnp.pad(wi[:, 3 * w:3 * w + HEADS], ((0, 0), (0, LANES - HEADS)))], axis=1)
    w_pb = jnp.concatenate([wi[:, 2 * w:3 * w], wi[:, 3 * w + HEADS:]], axis=1)
    na, nbw = w_pa.shape[1], w_pb.shape[1]
    gate_blk = 4 * w // d

    n2 = _norm_fwd(h1, mix_norm, tm=tr, name="mix_norm")
    tma = _pick(lp, [544, 256, 128])
    tnd = _pick(d, [1024, 512, 256, 128])
    tnb = _pick(nbw, [512, 256, 128])
    proj_a = _mm2d(n2, w_pa, na, dims=NN, tm=tma, tn=na, tk=d, out_dtype=F32, name="proj_a")
    proj_b = _mm2d(n2, w_pb, nbw, dims=NN, tm=tm, tn=tnb, tk=d, out_dtype=BF16, name="proj_b")
    b_pad = jnp.pad(b_forget, ((0, 0), (0, LANES - HEADS)))
    q_gain, k_gain = fox_q_norm.reshape(1, w), fox_k_norm.reshape(1, w)
    fq, fk, logf = _fox_prep(proj_a, q_gain, k_gain, b_pad, tm=tr, pad=pad, name="fox_prep")
    c = _cumsum_rows(logf, tile=tile, reverse=False, name="forget_cumsum")
    c_rows = jnp.transpose(c[:, :HEADS]).reshape(HEADS, nb, 1, tile)
    o_fox, lse = _fox_fwd(fq, fk, proj_b, 0, c_rows, tile=tile, pad=pad, name="fox_fwd")
    o_sb, sb_total = _sb_fwd(proj_b, HEADS, 2 * HEADS, 3 * HEADS, tile=tile, pad=pad, name="sb_fwd")

    def branch(o, w_blocks, name):
        return _mm((lp // tm, NDEV, 1), o, _spec((tm, w), lambda i, j, k: (i, 0)),
                   [(w_blocks, _spec((None, w, d8), lambda i, j, k: (j, 0, 0)))], [],
                   [((lp, d), BF16, _spec((tm, d8), lambda i, j, k: (i, j)))], dims=NN, name=name)[0]

    br_fox = branch(o_fox, w_br_fox, "branch_fox")
    br_sb = branch(o_sb, w_br_sb, "branch_sb")

    def merge_fn(i, tiles, consts):
        bf_, bs_, gf_, gs_ = [t.astype(F32) for t in tiles]
        return [_sigmoid(gf_) * bf_ + _sigmoid(gs_) * bs_], []

    gates_in = [(proj_b, d, gate_blk), (proj_b, d, gate_blk + 1)]
    merged, = _rowwise(merge_fn, [_whole(br_fox), _whole(br_sb)] + gates_in, [], [(d, BF16)], [], tm=tr, name="merge")
    h2 = _mm2d(merged, w_out_full, d, dims=NN, tm=tm, tn=tnd, tk=d, out_dtype=F32,
               epi=lambda accs, ex: [ex[0] + accs[0]], extras=[h1], name="out_proj")

    w_gu2, w_down2 = _send_wait(ffn2_copies, False, h2, "gather_ffn2_wait")
    saved3 = _ffn_up(h2, ffn2_norm, w_gu2, None, tm=tm, tag="ffn2")
    h3 = _ffn_down(h2, saved3[3], w_down2, tm=tm, tag="ffn2")

    skip = tile // tr

    def loss_fn(i, tiles, consts):
        real = i >= skip
        err = jnp.where(real, tiles[0] - tiles[1], 0.0)
        dy = err * (1.0 / d)
        part = 0.5 * jnp.sum(err * dy, axis=0, keepdims=True)
        return [dy, FFN_RESIDUAL_WEIGHT * dy], [part]

    dh3, dh3_half, loss_cols = _rowwise(
        loss_fn, [_whole(h3), _whole(loss_target[0])], [], [(d, F32, lp), (d, BF16, lp)], [(1, d)], tm=tr, name="loss",
        row_maps=[None, lambda i: jnp.maximum(i - skip, 0)])

    def own(g):
        return lax.dynamic_index_in_dim(g, me, 0, keepdims=False)

    def send_grads(grads_, name):
        return _send_start(grads_, [_landing(own(g), me) for g in grads_], True, grads_[-1], name)

    dab3, dw_gu2, dw_down2 = _ffn_bwd_dw(dh3_half, saved3, w_down2, tm=tm, tag="ffn2")
    ffn2_sends, token = send_grads([dw_gu2, dw_down2], "exchange_ffn2_start")
    dn3 = _ffn_bwd_dx(dab3, w_gu2, token, tm=tm, tag="ffn2")
    dh2, dh2_bf, _, dg_ffn2 = _norm_bwd(dn3, h2, ffn2_norm, dh3, tm=tr, name="ffn2_norm_bwd")

    dmerged = _mm2d(dh2_bf, w_out_full, d, dims=NT, tm=tm, tn=tnd, tk=d, out_dtype=BF16, name="out_proj_dx")
    dw_out = _mm2d(merged, dh2_bf, d, dims=TN, tm=tnd, tn=tnd, tk=tkr, out_dtype=BF16, name="out_proj_dw")

    def merge_bwd_fn(i, tiles, consts):
        dm, bf_, bs_, gf_, gs_ = [t.astype(F32) for t in tiles]
        sf, ss = _sigmoid(gf_), _sigmoid(gs_)
        return [dm * sf, dm * ss, dm * bf_ * sf * (1.0 - sf), dm * bs_ * ss * (1.0 - ss)], []

    dbr_fox, dbr_sb, dg_fox, dg_sb = _rowwise(
        merge_bwd_fn, [_whole(dmerged), _whole(br_fox), _whole(br_sb)] + gates_in, [], [(d, BF16)] * 4, [], tm=tr, name="merge_bwd")

    tnw = _pick(w, [512, 256, 128])

    def branch_dx(dbr, w_blocks, name):
        return _mm((lp // tm, w // tnw, NDEV), dbr, _spec((tm, d8), lambda i, j, k: (i, k)),
                   [(w_blocks, _spec((None, tnw, d8), lambda i, j, k: (k, j, 0)))], [],
                   [((lp, w), BF16, _spec((tm, tnw), lambda i, j, k: (i, j)))], dims=NT, acc_shape=(tm, tnw), name=name)[0]

    def branch_dw(o, dbr, name):
        return _mm((w // tnw, NDEV, lp // tkr), o, _spec((tkr, tnw), lambda i, j, k: (k, i)),
                   [(dbr, _spec((tkr, d8), lambda i, j, k: (k, j)))], [],
                   [((NDEV, w, d8), BF16, _spec((None, tnw, d8), lambda i, j, k: (j, i, 0)))],
                   dims=TN, acc_shape=(tnw, d8), name=name)[0]

    do_fox = branch_dx(dbr_fox, w_br_fox, "branch_fox_dx")
    do_sb = branch_dx(dbr_sb, w_br_sb, "branch_sb_dx")
    dw_br_fox = branch_dw(o_fox, dbr_fox, "branch_fox_dw")
    dw_br_sb = branch_dw(o_sb, dbr_sb, "branch_sb_dw")

    dfq, dfk, dfv, dc_rows = _fox_bwd(fq, fk, proj_b, 0, c_rows, o_fox, do_fox, lse, tile=tile, pad=pad, name="fox_bwd")
    dsq, dsk, dsv = _sb_bwd(proj_b, HEADS, 2 * HEADS, 3 * HEADS, do_sb, sb_total, tile=tile, pad=pad, name="sb_bwd")
    dc = jnp.pad(jnp.transpose(dc_rows.reshape(HEADS, lp)), ((0, 0), (0, LANES - HEADS)))
    dlogf = _cumsum_rows(dc, tile=tile, reverse=True, name="forget_cumsum_bwd")
    dproj_a, dg_q, dg_k, dg_b = _fox_prep_bwd(proj_a, dfq, dfk, dlogf, q_gain, k_gain, b_pad, tm=tr, pad=pad, name="fox_prep_bwd")
    dproj_b = jnp.concatenate([dfv, dsq, dsk, dsv, dg_fox, dg_sb], axis=1)

    dw_pa = _mm2d(n2, dproj_a, na, dims=TN, tm=tnd, tn=na, tk=_pick(lp, [544, 256, 128]), out_dtype=BF16, name="proj_a_dw")
    dw_pb = _mm2d(n2, dproj_b, nbw, dims=TN, tm=tnd, tn=tnb, tk=tkr, out_dtype=BF16, name="proj_b_dw")
    dn2_a = _mm2d(dproj_a, w_pa, d, dims=NT, tm=tm, tn=tnd, tk=na, out_dtype=F32, name="proj_a_dx")
    dn2 = _mm2d(dproj_b, w_pb, d, dims=NT, tm=tm, tn=tnd, tk=tnb, out_dtype=F32,
                epi=lambda accs, ex: [accs[0] + ex[0]], extras=[dn2_a], name="proj_b_dx")
    dw_in = jnp.concatenate([dw_pa[:, :2 * w], dw_pb[:, :w], dw_pa[:, 2 * w:2 * w + HEADS], dw_pb[:, w:]], axis=1)
    mixer_grads = [_split_cols(dw_in), dw_br_fox, dw_br_sb, dw_out.reshape(NDEV, d8, d)]
    mixer_sends, token = send_grads(mixer_grads, "exchange_mixer_start")
    dh1, _, dh1_half, dg_mix = _norm_bwd(dn2, h1, mix_norm, dh2, tm=tr, after=token, name="mix_norm_bwd")

    dab1, dw_gu1, dw_down1 = _ffn_bwd_dw(dh1_half, saved1, w_down1, tm=tm, tag="ffn1")
    ffn1_sends, token = send_grads([dw_gu1, dw_down1], "exchange_ffn1_start")
    dn1 = _ffn_bwd_dx(dab1, w_gu1, token, tm=tm, tag="ffn1")
    dh0, _, _, dg_ffn1 = _norm_bwd(dn1, h0, ffn1_norm, dh1, tm=tr, name="ffn1_norm_bwd")
    grad_x = dh0[tile:][None]

    r_gu2, r_down2 = _send_wait(ffn2_sends, True, dh0, "exchange_ffn2_wait")
    r_in, r_br_fox, r_br_sb, r_out = _send_wait(mixer_sends, True, dh0, "exchange_mixer_wait")
    grads, deltas, new_ms, new_vs = {}, {}, {}, {}

    def adamw_big(entries):
        for k, (parts, sel, wt, mt, vt) in entries.items():
            g, dl, mn, vn = _adamw_summed(parts, sel, wt[0], mt[0], vt[0], name=f"adamw_{k}")
            grads[k], deltas[k], new_ms[k], new_vs[k] = g[None], dl[None], mn[None], vn[None]

    adamw_big(dict(ffn2_w_gate=(r_gu2, 0, ffn2_w_gate, m_ffn2_w_gate, v_ffn2_w_gate),
                   ffn2_w_up=(r_gu2, 1, ffn2_w_up, m_ffn2_w_up, v_ffn2_w_up),
                   ffn2_w_down=(r_down2, 0, ffn2_w_down, m_ffn2_w_down, v_ffn2_w_down),
                   w_in=(r_in, 0, w_in, m_w_in, v_w_in),
                   w_branch_fox=(r_br_fox, 0, w_branch_fox, m_w_branch_fox, v_w_branch_fox),
                   w_branch_sb=(r_br_sb, 0, w_branch_sb, m_w_branch_sb, v_w_branch_sb),
                   w_out=(r_out, 0, w_out, m_w_out, v_w_out)))
    r_gu1, r_down1 = _send_wait(ffn1_sends, True, new_vs["w_out"], "exchange_ffn1_wait")
    adamw_big(dict(ffn1_w_gate=(r_gu1, 0, ffn1_w_gate, m_ffn1_w_gate, v_ffn1_w_gate),
                   ffn1_w_up=(r_gu1, 1, ffn1_w_up, m_ffn1_w_up, v_ffn1_w_up),
                   ffn1_w_down=(r_down1, 0, ffn1_w_down, m_ffn1_w_down, v_ffn1_w_down)))

    small_parts = [dh0[pad:tile], dg_ffn1, dg_mix, dg_ffn2, dg_b[:, :HEADS], dg_q, dg_k, loss_cols]
    small_packed, small_spans = _pack_rows(small_parts, LANES, SMALL_ROWS)
    small_sum = _allsum_small(small_packed, "sum_small")
    g_meta_full, g_ffn1n, g_mixn, g_ffn2n, g_bf, g_qn, g_kn, loss_vec = [
        _unpack(small_sum, span, part.shape) for span, part in zip(small_spans, small_parts)]
    loss = jnp.sum(loss_vec)
    g_meta = lax.dynamic_slice_in_dim(g_meta_full, me * d8, d8, axis=1)
    g_qn, g_kn = g_qn.reshape(fox_q_norm.shape), g_kn.reshape(fox_k_norm.shape)

    small = dict(meta_tokens=(g_meta, meta_tokens, m_meta_tokens, v_meta_tokens),
                 ffn1_norm=(g_ffn1n, ffn1_norm, m_ffn1_norm, v_ffn1_norm),
                 mix_norm=(g_mixn, mix_norm, m_mix_norm, v_mix_norm),
                 b_forget=(g_bf, b_forget, m_b_forget, v_b_forget),
                 fox_q_norm=(g_qn, fox_q_norm, m_fox_q_norm, v_fox_q_norm),
                 fox_k_norm=(g_kn, fox_k_norm, m_fox_k_norm, v_fox_k_norm),
                 ffn2_norm=(g_ffn2n, ffn2_norm, m_ffn2_norm, v_ffn2_norm))
    for k, (g, wt, mt, vt) in small.items():
        flat = lambda t: t.reshape(-1, t.shape[-1])
        dl, mn, vn = _adamw_plain(flat(g), flat(wt), flat(mt), flat(vt), name=f"adamw_{k}")
        grads[k], deltas[k], new_ms[k], new_vs[k] = g, dl.reshape(wt.shape), mn.reshape(wt.shape), vn.reshape(wt.shape)

    order = ["meta_tokens", "ffn1_norm", "ffn1_w_gate", "ffn1_w_up", "ffn1_w_down", "mix_norm", "w_in", "b_forget",
             "fox_q_norm", "fox_k_norm", "w_branch_fox", "w_branch_sb", "w_out", "ffn2_norm", "ffn2_w_gate",
             "ffn2_w_up", "ffn2_w_down"]
    return (loss, grad_x, *[grads[k] for k in order], *[deltas[k] for k in order],
            *[new_ms[k] for k in order], *[new_vs[k] for k in order])
```

```python
import jax
import jax.numpy as jnp
from jax import lax
from jax.experimental import pallas as pl
from jax.experimental.pallas import tpu as pltpu

F32 = jnp.float32
BF16 = jnp.bfloat16
MESH = pl.DeviceIdType.MESH

NDEV = 8
N_META = 16
HEAD_DIM = 128
HEADS = 8
BRANCH_WIDTH = HEADS * HEAD_DIM
RMS_EPS = 1e-6
FFN_RESIDUAL_WEIGHT = 0.5
ATTN_SCALE = HEAD_DIM ** -0.5
MASKED_LOGIT = -1e30

ADAM_LR = 0.001
ADAM_B1 = 0.9
ADAM_B2 = 0.999
ADAM_EPS = 1e-08
ADAM_WD = 0.01
ADAM_STEP = 10

LANES = 128
SMALL_ROWS = 8
ADAMW_TILE_ELEMS = 160 * 1024


def _pick(n, prefs):
    for p in prefs:
        if p <= n and n % p == 0:
            return p
    return n


def _dot(a, b, dims):
    return lax.dot_general(a, b, (dims, ((), ())), preferred_element_type=F32)


NN = ((1,), (0,))
TN = ((0,), (0,))
NT = ((1,), (1,))


def _split_bf16(x):
    hi = x.astype(BF16)
    lo = (x - hi.astype(F32)).astype(BF16)
    return hi, lo


def _sigmoid_parts(z):
    e = jnp.exp(-jnp.abs(z))
    t = 1.0 + e
    return e, 1.0 / t, jnp.log(t)


def _sigmoid(x):
    e = jnp.exp(-jnp.abs(x))
    r = 1.0 / (1.0 + e)
    return jnp.where(x >= 0, r, e * r)


def _my_place():
    return lax.axis_index("x"), lax.axis_index("y"), lax.axis_index("c")


def _flat_id(px, py, pc):
    return 4 * px + 2 * py + pc


def _peer(x, y, c, k):
    px = 1 - x if k & 4 else x
    py = 1 - y if k & 2 else y
    pc = 1 - c if k & 1 else c
    return px, py, pc


def _allgather(shards, name, in_vmem=False):
    n = len(shards)

    def body(*refs):
        x_refs, out_refs = refs[:n], refs[n:2 * n]
        send_sems, recv_sems, local_sems = refs[2 * n:]
        x, y, c = _my_place()
        me, sibling = (x, y, c), (x, y, 1 - c)
        chips = [(1 - x, y), (x, 1 - y), (1 - x, 1 - y)]

        def block(a, place):
            return out_refs[a].at[_flat_id(*place)]

        def copy(a, k, place, to, src=None):
            return pltpu.make_async_remote_copy(
                src_ref=block(a, place) if src is None else src, dst_ref=block(a, place),
                send_sem=send_sems.at[7 * a + k], recv_sem=recv_sems.at[7 * a + k], device_id=to, device_id_type=MESH)

        mine = [pltpu.make_async_copy(x_refs[a], block(a, me), local_sems.at[a]) for a in range(n)]
        for cp in mine:
            cp.start()
        first = []
        for a in range(n):
            first.append(copy(a, 0, me, sibling, src=x_refs[a]))
            first += [copy(a, 1 + j, me, (*chip, c), src=x_refs[a]) for j, chip in enumerate(chips)]
        for cp in first:
            cp.start()
        passed = []
        for j, chip in enumerate(chips):
            for a in range(n):
                copy(a, 1 + j, (*chip, c), me).wait_recv()
                passed.append(copy(a, 4 + j, (*chip, c), sibling))
                passed[-1].start()
        for a in range(n):
            copy(a, 0, sibling, me).wait_recv()
        for j, chip in enumerate(chips):
            for a in range(n):
                copy(a, 4 + j, (*chip, 1 - c), me).wait_recv()
        for cp in first + passed:
            cp.wait_send()
        for cp in mine:
            cp.wait()

    space = pltpu.VMEM if in_vmem else pl.ANY
    return pl.pallas_call(
        body, name=name,
        out_shape=[jax.ShapeDtypeStruct((NDEV,) + s.shape, s.dtype) for s in shards],
        in_specs=[pl.BlockSpec(memory_space=space)] * n,
        out_specs=[pl.BlockSpec(memory_space=space)] * n,
        scratch_shapes=[pltpu.SemaphoreType.DMA((7 * n,)), pltpu.SemaphoreType.DMA((7 * n,)), pltpu.SemaphoreType.DMA((n,))],
    )(*shards)


def _allsum_small(part, name):
    rows, cols = part.shape

    def body(p_ref, out_ref, buf, send_sems, recv_sems):
        x, y, c = _my_place()
        me = _flat_id(x, y, c)
        buf[me] = p_ref[...]
        copies = []
        for k in range(1, NDEV):
            copies.append(pltpu.make_async_remote_copy(
                src_ref=p_ref, dst_ref=buf.at[me], send_sem=send_sems.at[k - 1], recv_sem=recv_sems.at[k - 1],
                device_id=_peer(x, y, c, k), device_id_type=MESH))
        for cp in copies:
            cp.start()
        for cp in copies:
            cp.wait()
        total = buf[0]
        for j in range(1, NDEV):
            total = total + buf[j]
        out_ref[...] = total

    return pl.pallas_call(
        body, name=name,
        out_shape=jax.ShapeDtypeStruct((rows, cols), F32),
        in_specs=[pl.BlockSpec(memory_space=pltpu.VMEM)],
        out_specs=pl.BlockSpec(memory_space=pltpu.VMEM),
        scratch_shapes=[pltpu.VMEM((NDEV, rows, cols), F32),
                        pltpu.SemaphoreType.DMA((7,)), pltpu.SemaphoreType.DMA((7,))],
    )(part)


_HBM = pl.BlockSpec(memory_space=pltpu.HBM)
_SEM = pl.BlockSpec(memory_space=pltpu.SEMAPHORE)
_DATAFLOW = pltpu.SideEffectType.DATAFLOW_SIDE_EFFECTING


def _send_copies(src_refs, land_refs, send_sems, recv_sems, per_peer):
    x, y, c = _my_place()
    me = _flat_id(x, y, c)
    copies = []
    for a, (src_ref, land_ref) in enumerate(zip(src_refs, land_refs)):
        for k in range(1, NDEV):
            peer = _peer(x, y, c, k)
            copies.append(pltpu.make_async_remote_copy(
                src_ref=src_ref.at[_flat_id(*peer)] if per_peer else src_ref, dst_ref=land_ref.at[me],
                send_sem=send_sems.at[7 * a + k - 1], recv_sem=recv_sems.at[7 * a + k - 1],
                device_id=peer, device_id_type=MESH))
    return copies


def _send_start(srcs, lands, per_peer, after, name):
    n = len(srcs)

    def body(*refs):
        for cp in _send_copies(refs[:n], refs[n:2 * n], refs[2 * n + 1], refs[2 * n + 2], per_peer):
            cp.start()
        refs[-1][...] = jnp.zeros_like(refs[-1])

    operands = [pltpu.with_memory_space_constraint(t, pltpu.HBM) for t in list(srcs) + list(lands)]
    outs = pl.pallas_call(
        body, name=name,
        out_shape=(pltpu.SemaphoreType.DMA((7 * n,)), pltpu.SemaphoreType.DMA((7 * n,)),
                   *[pltpu.HBM(t.shape, t.dtype) for t in operands], jax.ShapeDtypeStruct((SMALL_ROWS, LANES), F32)),
        in_specs=[_HBM] * (2 * n) + [pl.BlockSpec(memory_space=pl.ANY)],
        out_specs=(_SEM, _SEM, *[_HBM] * (2 * n), pl.BlockSpec(memory_space=pltpu.VMEM)),
        input_output_aliases={i: 2 + i for i in range(2 * n)},
        compiler_params=pltpu.CompilerParams(has_side_effects=_DATAFLOW),
    )(*operands, after)
    return (outs[0], outs[1], list(outs[2:2 + n]), list(outs[2 + n:2 + 2 * n])), outs[-1]


def _send_wait(handle, per_peer, after, name):
    send_sems, recv_sems, srcs, lands = handle
    n = len(srcs)

    def body(*refs):
        for cp in _send_copies(refs[:n], refs[n:2 * n], refs[2 * n], refs[2 * n + 1], per_peer):
            cp.wait_send()
            cp.wait_recv()

    outs = pl.pallas_call(
        body, name=name,
        out_shape=tuple(pltpu.HBM(t.shape, t.dtype) for t in srcs + lands),
        in_specs=[_HBM] * (2 * n) + [_SEM, _SEM, pl.BlockSpec(memory_space=pl.ANY)],
        out_specs=tuple([_HBM] * (2 * n)),
        input_output_aliases={i: i for i in range(2 * n)},
        compiler_params=pltpu.CompilerParams(has_side_effects=_DATAFLOW),
    )(*srcs, *lands, send_sems, recv_sems, after)
    return list(outs[n:])


def _landing(own_block, me):
    return lax.dynamic_update_index_in_dim(lax.empty((NDEV,) + own_block.shape, own_block.dtype), own_block, me, 0)


def _spec(block, index_map):
    return pl.BlockSpec(block, index_map)


def _mm(grid, a, a_spec, bs, extras, outs, *, dims, acc_shape=None, epi=None, after=None, name):
    nk = grid[2]
    nb, ne, no = len(bs), len(extras), len(outs)
    nafter = 0 if after is None else 1
    if epi is None:
        epi = lambda accs, ex: [accs[0]]

    def body(*refs):
        a_ref, b_refs = refs[0], refs[1:1 + nb]
        e_refs = refs[1 + nb:1 + nb + ne]
        o_refs = refs[1 + nb + ne + nafter:1 + nb + ne + nafter + no]
        acc_refs = refs[1 + nb + ne + nafter + no:]
        def finish(accs):
            for o_ref, tile in zip(o_refs, epi(accs, [e_ref[...] for e_ref in e_refs])):
                o_ref[...] = tile.astype(o_ref.dtype)

        if nk == 1:
            av = a_ref[...]
            finish([_dot(av, b_ref[...], dims) for b_ref in b_refs])
        else:
            k = pl.program_id(2)

            @pl.when(k == 0)
            def _():
                for acc_ref in acc_refs:
                    acc_ref[...] = jnp.zeros_like(acc_ref)

            av = a_ref[...]
            for acc_ref, b_ref in zip(acc_refs, b_refs):
                acc_ref[...] += _dot(av, b_ref[...], dims)

            @pl.when(k == nk - 1)
            def _():
                finish([acc_ref[...] for acc_ref in acc_refs])

    return pl.pallas_call(
        body, name=name,
        grid=grid,
        in_specs=[a_spec] + [s for _, s in bs] + [s for _, s in extras] + [pl.BlockSpec(memory_space=pl.ANY)] * nafter,
        out_specs=[s for _, _, s in outs],
        out_shape=[jax.ShapeDtypeStruct(shape, dt) for shape, dt, _ in outs],
        scratch_shapes=[pltpu.VMEM(acc_shape, F32) for _ in bs] if nk > 1 else [],
        compiler_params=pltpu.CompilerParams(dimension_semantics=("parallel", "parallel", "arbitrary")),
    )(a, *[b for b, _ in bs], *[e for e, _ in extras], *([after] if nafter else []))


def _mm2d(a, b, n_cols, *, dims, tm, tn, tk, out_dtype, epi=None, extras=(), name):
    m_rows, k_len = (a.shape[1], a.shape[0]) if dims == TN else a.shape
    assert m_rows % tm == 0 and n_cols % tn == 0 and k_len % tk == 0, (name, a.shape, n_cols, tm, tn, tk)
    a_spec = _spec((tk, tm), lambda i, j, k: (k, i)) if dims == TN else _spec((tm, tk), lambda i, j, k: (i, k))
    b_spec = _spec((tn, tk), lambda i, j, k: (j, k)) if dims == NT else _spec((tk, tn), lambda i, j, k: (k, j))
    tile = _spec((tm, tn), lambda i, j, k: (i, j))
    return _mm((m_rows // tm, n_cols // tn, k_len // tk), a, a_spec, [(b, b_spec)], [(e, tile) for e in extras],
               [((m_rows, n_cols), out_dtype, tile)], dims=dims, acc_shape=(tm, tn), epi=epi, name=name)[0]


def _rowwise(fn, ins, consts, outs, sums, *, tm, name, row_maps=None, after=None):
    m_rows = outs[0][2] if len(outs[0]) == 3 else ins[0][0].shape[0]
    n = m_rows // tm
    ni, nc, no = len(ins), len(consts), len(outs)
    nafter = 0 if after is None else 1
    row_maps = row_maps or [None] * ni

    def body(*refs):
        i = pl.program_id(0)
        in_tiles = [r[...] for r in refs[:ni]]
        const_values = [r[...] for r in refs[ni:ni + nc]]
        o_refs = refs[ni + nc + nafter:ni + nc + nafter + no]
        s_refs = refs[ni + nc + nafter + no:]
        out_tiles, sum_terms = fn(i, in_tiles, const_values)
        for o_ref, tile in zip(o_refs, out_tiles):
            o_ref[...] = tile.astype(o_ref.dtype)
        if s_refs:
            @pl.when(i == 0)
            def _():
                for s_ref in s_refs:
                    s_ref[...] = jnp.zeros_like(s_ref)

            for s_ref, term in zip(s_refs, sum_terms):
                s_ref[...] += term

    def in_spec(width, col, rmap):
        if rmap is None:
            return pl.BlockSpec((tm, width), lambda i: (i, col))
        return pl.BlockSpec((tm, width), lambda i: (rmap(i), col))

    return pl.pallas_call(
        body, name=name,
        grid=(n,),
        in_specs=[in_spec(w, col, rmap) for (_, w, col), rmap in zip(ins, row_maps)]
        + [pl.BlockSpec(cst.shape, lambda i: (0, 0)) for cst in consts] + [pl.BlockSpec(memory_space=pl.ANY)] * nafter,
        out_specs=[pl.BlockSpec((tm, o[0]), lambda i: (i, 0)) for o in outs]
        + [pl.BlockSpec(s, lambda i: (0, 0)) for s in sums],
        out_shape=[jax.ShapeDtypeStruct((m_rows, o[0]), o[1]) for o in outs]
        + [jax.ShapeDtypeStruct(s, F32) for s in sums],
        compiler_params=pltpu.CompilerParams(dimension_semantics=("arbitrary",)),
    )(*[arr for arr, _, _ in ins], *consts, *([after] if nafter else []))


def _whole(arr):
    return (arr, arr.shape[1], 0)


def _rms(x, gain):
    r = lax.rsqrt(jnp.mean(x * x, axis=-1, keepdims=True) + RMS_EPS)
    return x * r * gain


def _rms_bwd(x, gain, dy):
    r = lax.rsqrt(jnp.mean(x * x, axis=-1, keepdims=True) + RMS_EPS)
    u = dy * gain
    dx = r * u - x * (r * r * r) * jnp.mean(x * u, axis=-1, keepdims=True)
    return dx, dy * x * r


def _norm_fwd(h, gain, *, tm, name):
    def fn(i, tiles, consts):
        return [_rms(tiles[0], consts[0])], []
    return _rowwise(fn, [_whole(h)], [gain], [(h.shape[1], BF16)], [], tm=tm, name=name)[0]


def _norm_bwd(dn, h, gain, dh_in, *, tm, name, after=None):
    d = h.shape[1]

    def fn(i, tiles, consts):
        dx, dg_rows = _rms_bwd(tiles[1], consts[0], tiles[0])
        dh = tiles[2] + dx
        return [dh, dh, FFN_RESIDUAL_WEIGHT * dh], [jnp.sum(dg_rows, axis=0, keepdims=True)]

    return _rowwise(fn, [_whole(dn), _whole(h), _whole(dh_in)], [gain],
                    [(d, F32), (d, BF16), (d, BF16)], [(1, d)], tm=tm, name=name, after=after)


def _swiglu_epi(accs, ex):
    a, b = accs
    return [a, b, a * _sigmoid(a) * b]


def _swiglu_bwd_epi(accs, ex):
    ds = accs[0]
    a, b = ex[0].astype(F32), ex[1].astype(F32)
    sig = _sigmoid(a)
    silu = a * sig
    dsilu = sig * (1.0 + a * (1.0 - sig))
    return [jnp.stack([(ds * b * dsilu).astype(BF16), (ds * silu).astype(BF16)], axis=0)]


def _attn_mask(i, j, tile, pad, strict):
    row = i * tile + lax.broadcasted_iota(jnp.int32, (tile, tile), 0)
    col = j * tile + lax.broadcasted_iota(jnp.int32, (tile, tile), 1)
    causal = (col < row) if strict else (col <= row)
    return causal & ((col >= pad) | (row < pad))


HEADS_PER_STEP = 2
GROUP_WIDTH = HEADS_PER_STEP * HEAD_DIM
HEAD_GROUPS = HEADS // HEADS_PER_STEP


def _head_cols(g):
    return pl.ds(g * HEAD_DIM, HEAD_DIM)


def _key_tile(ref, j, tile, g):
    return ref[pl.ds(pl.multiple_of(j * tile, tile), tile), _head_cols(g)]


def _head_specs(lp, tile, q_off, k_off, v_off):
    q_spec = pl.BlockSpec((tile, GROUP_WIDTH), lambda h, i: (i, h + q_off // HEADS_PER_STEP))
    k_spec = pl.BlockSpec((lp, GROUP_WIDTH), lambda h, i: (0, h + k_off // HEADS_PER_STEP))
    v_spec = pl.BlockSpec((lp, GROUP_WIDTH), lambda h, i: (0, h + v_off // HEADS_PER_STEP))
    return q_spec, k_spec, v_spec


def _fox_fwd(q, k, v, v_off, c_rows, *, tile, pad, name):
    lp = q.shape[0]
    nb = lp // tile

    def body(q_ref, k_ref, v_ref, c_ref, o_ref, lse_ref):
        i = pl.program_id(1)
        qts = [q_ref[:, _head_cols(g)] for g in range(HEADS_PER_STEP)]

        def step(j, carry):
            ok = _attn_mask(i, j, tile, pad, False)
            out = []
            for g, (m, l, acc) in enumerate(carry):
                s = _dot(qts[g], _key_tile(k_ref, j, tile, g), NT) * ATTN_SCALE - c_ref[g, j]
                s = jnp.where(ok, s, MASKED_LOGIT)
                m_new = jnp.maximum(m, jnp.max(s, axis=1, keepdims=True))
                p = jnp.exp(s - m_new)
                alpha = jnp.exp(m - m_new)
                l = alpha * l + jnp.sum(p, axis=1, keepdims=True)
                acc = alpha * acc + _dot(p.astype(BF16), _key_tile(v_ref, j, tile, g), NN)
                out.append((m_new, l, acc))
            return tuple(out)

        init = (jnp.full((tile, 1), MASKED_LOGIT, F32), jnp.zeros((tile, 1), F32), jnp.zeros((tile, HEAD_DIM), F32))
        final = lax.fori_loop(0, i + 1, step, (init,) * HEADS_PER_STEP)
        for g, (m, l, acc) in enumerate(final):
            o_ref[:, _head_cols(g)] = (acc / l).astype(o_ref.dtype)
            lse_ref[g] = jnp.broadcast_to(m + jnp.log(l), (tile, LANES))

    q_spec, k_spec, v_spec = _head_specs(lp, tile, 0, 0, v_off)
    return pl.pallas_call(
        body, name=name,
        grid=(HEAD_GROUPS, nb),
        in_specs=[q_spec, k_spec, v_spec, pl.BlockSpec((HEADS_PER_STEP, nb, 1, tile), lambda h, i: (h, 0, 0, 0))],
        out_specs=[pl.BlockSpec((tile, GROUP_WIDTH), lambda h, i: (i, h)),
                   pl.BlockSpec((HEADS_PER_STEP, tile, LANES), lambda h, i: (h, i, 0))],
        out_shape=[jax.ShapeDtypeStruct((lp, BRANCH_WIDTH), BF16), jax.ShapeDtypeStruct((HEADS, lp, LANES), F32)],
        compiler_params=pltpu.CompilerParams(dimension_semantics=("parallel", "arbitrary")),
    )(q, k, v, c_rows)


def _fox_bwd(q, k, v, v_off, c_rows, o, do, lse, *, tile, pad, name):
    lp = q.shape[0]
    nb = lp // tile

    def body(q_ref, k_ref, v_ref, c_ref, o_ref, do_ref, lse_ref, dq_ref, dk_ref, dv_ref, dc_ref, dk_acc, dv_acc, dc_acc):
        i = pl.program_id(1)

        @pl.when(i == 0)
        def _():
            dk_acc[...] = jnp.zeros_like(dk_acc)
            dv_acc[...] = jnp.zeros_like(dv_acc)
            dc_acc[...] = jnp.zeros_like(dc_acc)

        heads = range(HEADS_PER_STEP)
        qts = [q_ref[:, _head_cols(g)] for g in heads]
        dots = [do_ref[:, _head_cols(g)] for g in heads]
        deltas = [jnp.sum(dots[g].astype(F32) * o_ref[:, _head_cols(g)].astype(F32), axis=1, keepdims=True) for g in heads]
        lse_cols = [lse_ref[g][:, :1] for g in heads]

        def step(j, dqs):
            rows = pl.ds(pl.multiple_of(j * tile, tile), tile)
            ok = _attn_mask(i, j, tile, pad, False)
            out = []
            for g in heads:
                kt = _key_tile(k_ref, j, tile, g)
                s = _dot(qts[g], kt, NT) * ATTN_SCALE - c_ref[g, j]
                p = jnp.where(ok, jnp.exp(s - lse_cols[g]), 0.0)
                dp = _dot(dots[g], _key_tile(v_ref, j, tile, g), NT)
                ds = p * (dp - deltas[g])
                dsb = ds.astype(BF16)
                dk_acc[rows, _head_cols(g)] += _dot(dsb, qts[g], TN)
                dv_acc[rows, _head_cols(g)] += _dot(p.astype(BF16), dots[g], TN)
                dc_acc[g, j] += -jnp.sum(ds, axis=0, keepdims=True)
                out.append(dqs[g] + _dot(dsb, kt, NN))
            return tuple(out)

        dqs = lax.fori_loop(0, i + 1, step, (jnp.zeros((tile, HEAD_DIM), F32),) * HEADS_PER_STEP)
        for g in heads:
            dq_ref[:, _head_cols(g)] = dqs[g] * ATTN_SCALE

        @pl.when(i == nb - 1)
        def _():
            dk_ref[...] = dk_acc[...] * ATTN_SCALE
            dv_ref[...] = dv_acc[...].astype(dv_ref.dtype)
            dc_ref[...] = dc_acc[...]

    q_spec, k_spec, v_spec = _head_specs(lp, tile, 0, 0, v_off)
    tile_spec = pl.BlockSpec((tile, GROUP_WIDTH), lambda h, i: (i, h))
    head_spec = pl.BlockSpec((lp, GROUP_WIDTH), lambda h, i: (0, h))
    c_spec = pl.BlockSpec((HEADS_PER_STEP, nb, 1, tile), lambda h, i: (h, 0, 0, 0))
    return pl.pallas_call(
        body, name=name,
        grid=(HEAD_GROUPS, nb),
        in_specs=[q_spec, k_spec, v_spec, c_spec, tile_spec, tile_spec,
                  pl.BlockSpec((HEADS_PER_STEP, tile, LANES), lambda h, i: (h, i, 0))],
        out_specs=[tile_spec, head_spec, head_spec, c_spec],
        out_shape=[jax.ShapeDtypeStruct((lp, BRANCH_WIDTH), F32), jax.ShapeDtypeStruct((lp, BRANCH_WIDTH), F32),
                   jax.ShapeDtypeStruct((lp, BRANCH_WIDTH), BF16), jax.ShapeDtypeStruct((HEADS, nb, 1, tile), F32)],
        scratch_shapes=[pltpu.VMEM((lp, GROUP_WIDTH), F32), pltpu.VMEM((lp, GROUP_WIDTH), F32),
                        pltpu.VMEM((HEADS_PER_STEP, nb, 1, tile), F32)],
        compiler_params=pltpu.CompilerParams(dimension_semantics=("parallel", "arbitrary")),
    )(q, k, v, c_rows, o, do, lse)


def _later_matrix(tile):
    return (lax.broadcasted_iota(jnp.int32, (tile, tile), 0) > lax.broadcasted_iota(jnp.int32, (tile, tile), 1)).astype(BF16)


def _earlier_matrix(tile):
    return (lax.broadcasted_iota(jnp.int32, (tile, tile), 0) < lax.broadcasted_iota(jnp.int32, (tile, tile), 1)).astype(BF16)


def _sb_tile(qt, kt, ok, later):
    z = _dot(qt, kt, NT) * ATTN_SCALE
    e, r, lg = _sigmoid_parts(z)
    sp = jnp.maximum(z, 0.0) + lg
    spm = jnp.where(ok, sp, 0.0)
    hi, lo = _split_bf16(spm)
    within = _dot(hi, later, NN) + _dot(lo, later, NN)
    return z, e, r, sp, spm, within


def _sb_fwd(qkv, q_off, k_off, v_off, *, tile, pad, name):
    lp = qkv.shape[0]
    nb = lp // tile

    def body(q_ref, k_ref, v_ref, o_ref, tot_ref):
        i = pl.program_id(1)
        qts = [q_ref[:, _head_cols(g)] for g in range(HEADS_PER_STEP)]
        later = _later_matrix(tile)

        def step(t, carry):
            j = i - t
            ok = _attn_mask(i, j, tile, pad, True)
            out = []
            for g, (right, acc) in enumerate(carry):
                z, _, _, sp, spm, within = _sb_tile(qts[g], _key_tile(k_ref, j, tile, g), ok, later)
                w = jnp.where(ok, jnp.exp(z - sp - within - right), 0.0)
                acc = acc + _dot(w.astype(BF16), _key_tile(v_ref, j, tile, g), NN)
                out.append((right + jnp.sum(spm, axis=1, keepdims=True), acc))
            return tuple(out)

        init = (jnp.zeros((tile, 1), F32), jnp.zeros((tile, HEAD_DIM), F32))
        final = lax.fori_loop(0, i + 1, step, (init,) * HEADS_PER_STEP)
        for g, (total, acc) in enumerate(final):
            o_ref[:, _head_cols(g)] = acc.astype(o_ref.dtype)
            tot_ref[g] = jnp.broadcast_to(total, (tile, LANES))

    q_spec, k_spec, v_spec = _head_specs(lp, tile, q_off, k_off, v_off)
    return pl.pallas_call(
        body, name=name,
        grid=(HEAD_GROUPS, nb),
        in_specs=[q_spec, k_spec, v_spec],
        out_specs=[pl.BlockSpec((tile, GROUP_WIDTH), lambda h, i: (i, h)),
                   pl.BlockSpec((HEADS_PER_STEP, tile, LANES), lambda h, i: (h, i, 0))],
        out_shape=[jax.ShapeDtypeStruct((lp, BRANCH_WIDTH), BF16), jax.ShapeDtypeStruct((HEADS, lp, LANES), F32)],
        compiler_params=pltpu.CompilerParams(dimension_semantics=("parallel", "arbitrary")),
    )(qkv, qkv, qkv)


def _sb_bwd(qkv, q_off, k_off, v_off, do, total, *, tile, pad, name):
    lp = qkv.shape[0]
    nb = lp // tile

    def body(q_ref, k_ref, v_ref, do_ref, tot_ref, dq_ref, dk_ref, dv_ref, dk_acc, dv_acc):
        i = pl.program_id(1)

        @pl.when(i == 0)
        def _():
            dk_acc[...] = jnp.zeros_like(dk_acc)
            dv_acc[...] = jnp.zeros_like(dv_acc)

        heads = range(HEADS_PER_STEP)
        qts = [q_ref[:, _head_cols(g)] for g in heads]
        dots = [do_ref[:, _head_cols(g)] for g in heads]
        total_cols = [tot_ref[g][:, :1] for g in heads]
        later, earlier = _later_matrix(tile), _earlier_matrix(tile)

        def step(j, carry):
            rows = pl.ds(pl.multiple_of(j * tile, tile), tile)
            ok = _attn_mask(i, j, tile, pad, True)
            out = []
            for g, (dq, sp_before, dlw_before) in enumerate(carry):
                kt = _key_tile(k_ref, j, tile, g)
                z, e, r, sp, spm, within = _sb_tile(qts[g], kt, ok, later)
                sp_here = jnp.sum(spm, axis=1, keepdims=True)
                right = total_cols[g] - sp_before - sp_here
                w = jnp.where(ok, jnp.exp(z - sp - within - right), 0.0)
                dlw = w * _dot(dots[g], _key_tile(v_ref, j, tile, g), NT)
                hi, lo = _split_bf16(dlw)
                before = dlw_before + _dot(hi, earlier, NN) + _dot(lo, earlier, NN)
                sig = jnp.where(z >= 0, r, e * r)
                dz = jnp.where(ok, dlw * (1.0 - sig) - sig * before, 0.0)
                dzb = dz.astype(BF16)
                dk_acc[rows, _head_cols(g)] += _dot(dzb, qts[g], TN)
                dv_acc[rows, _head_cols(g)] += _dot(w.astype(BF16), dots[g], TN)
                out.append((dq + _dot(dzb, kt, NN), sp_before + sp_here, dlw_before + jnp.sum(dlw, axis=1, keepdims=True)))
            return tuple(out)

        zero_col = jnp.zeros((tile, 1), F32)
        final = lax.fori_loop(0, i + 1, step, ((jnp.zeros((tile, HEAD_DIM), F32), zero_col, zero_col),) * HEADS_PER_STEP)
        for g in heads:
            dq_ref[:, _head_cols(g)] = (final[g][0] * ATTN_SCALE).astype(dq_ref.dtype)

        @pl.when(i == nb - 1)
        def _():
            dk_ref[...] = (dk_acc[...] * ATTN_SCALE).astype(dk_ref.dtype)
            dv_ref[...] = dv_acc[...].astype(dv_ref.dtype)

    q_spec, k_spec, v_spec = _head_specs(lp, tile, q_off, k_off, v_off)
    tile_spec = pl.BlockSpec((tile, GROUP_WIDTH), lambda h, i: (i, h))
    head_spec = pl.BlockSpec((lp, GROUP_WIDTH), lambda h, i: (0, h))
    return pl.pallas_call(
        body, name=name,
        grid=(HEAD_GROUPS, nb),
        in_specs=[q_spec, k_spec, v_spec, tile_spec, pl.BlockSpec((HEADS_PER_STEP, tile, LANES), lambda h, i: (h, i, 0))],
        out_specs=[tile_spec, head_spec, head_spec],
        out_shape=[jax.ShapeDtypeStruct((lp, BRANCH_WIDTH), BF16)] * 3,
        scratch_shapes=[pltpu.VMEM((lp, GROUP_WIDTH), F32), pltpu.VMEM((lp, GROUP_WIDTH), F32)],
        compiler_params=pltpu.CompilerParams(dimension_semantics=("parallel", "arbitrary")),
    )(qkv, qkv, qkv, do, total)


def _cumsum_rows(x, *, tile, reverse, name):
    lp = x.shape[0]
    nb = lp // tile

    def body(x_ref, o_ref, carry):
        @pl.when(pl.program_id(0) == 0)
        def _():
            carry[...] = jnp.zeros_like(carry)

        r = lax.broadcasted_iota(jnp.int32, (tile, tile), 0)
        c = lax.broadcasted_iota(jnp.int32, (tile, tile), 1)
        tri = ((c >= r) if reverse else (c <= r)).astype(BF16)
        hi, lo = _split_bf16(x_ref[...])
        run = _dot(tri, hi, NN) + _dot(tri, lo, NN) + carry[...]
        o_ref[...] = run
        carry[...] = run[:1, :] if reverse else run[tile - 1:, :]

    order = (lambda i: (nb - 1 - i, 0)) if reverse else (lambda i: (i, 0))
    return pl.pallas_call(
        body, name=name,
        grid=(nb,),
        in_specs=[pl.BlockSpec((tile, LANES), order)],
        out_specs=pl.BlockSpec((tile, LANES), order),
        out_shape=jax.ShapeDtypeStruct((lp, LANES), F32),
        scratch_shapes=[pltpu.VMEM((1, LANES), F32)],
        compiler_params=pltpu.CompilerParams(dimension_semantics=("arbitrary",)),
    )(x)


def _log_sigmoid(x):
    return jnp.minimum(x, 0.0) - jnp.log(1.0 + jnp.exp(-jnp.abs(x)))


def _forget_mask(i, tm, pad):
    row = i * tm + lax.broadcasted_iota(jnp.int32, (tm, LANES), 0)
    lane = lax.broadcasted_iota(jnp.int32, (tm, LANES), 1)
    return (row >= pad) & (lane < HEADS)


def _fox_prep(proj_a, q_gain, k_gain, b_forget, *, tm, pad, name):
    w = BRANCH_WIDTH

    def fn(i, tiles, consts):
        pa = tiles[0]
        qs, ks = [], []
        for h in range(HEADS):
            lo = h * HEAD_DIM
            qs.append(_rms(pa[:, lo:lo + HEAD_DIM], consts[0][:, lo:lo + HEAD_DIM]))
            ks.append(_rms(pa[:, w + lo:w + lo + HEAD_DIM], consts[1][:, lo:lo + HEAD_DIM]))
        logf = jnp.where(_forget_mask(i, tm, pad), _log_sigmoid(pa[:, 2 * w:] + consts[2]), 0.0)
        return [jnp.concatenate(qs, axis=1), jnp.concatenate(ks, axis=1), logf], []

    return _rowwise(fn, [_whole(proj_a)], [q_gain, k_gain, b_forget],
                    [(w, BF16), (w, BF16), (LANES, F32)], [], tm=tm, name=name)


def _fox_prep_bwd(proj_a, dq, dk, dlogf, q_gain, k_gain, b_forget, *, tm, pad, name):
    w = BRANCH_WIDTH

    def fn(i, tiles, consts):
        pa, dqt, dkt, dlf = tiles
        dxs_q, dxs_k, dgs_q, dgs_k = [], [], [], []
        for h in range(HEADS):
            lo = h * HEAD_DIM
            dx, dg = _rms_bwd(pa[:, lo:lo + HEAD_DIM], consts[0][:, lo:lo + HEAD_DIM], dqt[:, lo:lo + HEAD_DIM])
            dxs_q.append(dx)
            dgs_q.append(jnp.sum(dg, axis=0, keepdims=True))
            dx, dg = _rms_bwd(pa[:, w + lo:w + lo + HEAD_DIM], consts[1][:, lo:lo + HEAD_DIM], dkt[:, lo:lo + HEAD_DIM])
            dxs_k.append(dx)
            dgs_k.append(jnp.sum(dg, axis=0, keepdims=True))
        xf = pa[:, 2 * w:] + consts[2]
        e, r, _ = _sigmoid_parts(xf)
        df = jnp.where(_forget_mask(i, tm, pad), dlf * jnp.where(xf >= 0, e * r, r), 0.0)
        return ([jnp.concatenate(dxs_q + dxs_k + [df], axis=1)],
                [jnp.concatenate(dgs_q, axis=1), jnp.concatenate(dgs_k, axis=1), jnp.sum(df, axis=0, keepdims=True)])

    return _rowwise(fn, [_whole(proj_a), _whole(dq), _whole(dk), _whole(dlogf)], [q_gain, k_gain, b_forget],
                    [(2 * w + LANES, BF16)], [(1, w), (1, w), (1, LANES)], tm=tm, name=name)


def _adamw_math(w, g, m, v):
    m = ADAM_B1 * m + (1.0 - ADAM_B1) * g
    v = ADAM_B2 * v + (1.0 - ADAM_B2) * (g * g)
    m_hat = m / (1.0 - ADAM_B1 ** ADAM_STEP)
    v_hat = v / (1.0 - ADAM_B2 ** ADAM_STEP)
    delta = -ADAM_LR * (m_hat / (jnp.sqrt(v_hat) + ADAM_EPS) + ADAM_WD * w)
    return delta, m, v


def _adamw_summed(parts, sel, w, m, v, *, name):
    rows, cols = w.shape
    tr = _pick(rows, [t for t in (512, 256, 128, 64, 32, 16, 8) if t * cols <= ADAMW_TILE_ELEMS])

    def body(p_ref, w_ref, m_ref, v_ref, g_out, d_out, m_out, v_out):
        g = p_ref[0].astype(F32)
        for j in range(1, NDEV):
            g = g + p_ref[j].astype(F32)
        delta, m_new, v_new = _adamw_math(w_ref[...], g, m_ref[...], v_ref[...])
        g_out[...] = g
        d_out[...] = delta
        m_out[...] = m_new
        v_out[...] = v_new

    spec = pl.BlockSpec((tr, cols), lambda i: (i, 0))
    if parts.ndim == 3:
        p_spec = pl.BlockSpec((NDEV, tr, cols), lambda i: (0, i, 0))
    else:
        p_spec = pl.BlockSpec((NDEV, None, tr, cols), lambda i: (0, sel, i, 0))
    return pl.pallas_call(
        body, name=name,
        grid=(rows // tr,),
        in_specs=[p_spec, spec, spec, spec],
        out_specs=[spec] * 4,
        out_shape=[jax.ShapeDtypeStruct((rows, cols), F32)] * 4,
        compiler_params=pltpu.CompilerParams(dimension_semantics=("parallel",)),
    )(parts, w, m, v)


def _adamw_plain(g, w, m, v, *, name):
    def body(g_ref, w_ref, m_ref, v_ref, d_out, m_out, v_out):
        delta, m_new, v_new = _adamw_math(w_ref[...], g_ref[...], m_ref[...], v_ref[...])
        d_out[...] = delta
        m_out[...] = m_new
        v_out[...] = v_new

    return pl.pallas_call(body, name=name, out_shape=[jax.ShapeDtypeStruct(w.shape, F32)] * 3)(g, w, m, v)


def _pack_rows(arrays, width, row_align):
    pieces, spans, at = [], [], 0
    for arr in arrays:
        flat = arr.reshape(-1)
        rows = -(-flat.shape[0] // (width * row_align)) * row_align
        flat = jnp.pad(flat, (0, rows * width - flat.shape[0]))
        pieces.append(flat.reshape(rows, width))
        spans.append((at, rows))
        at += rows
    return jnp.concatenate(pieces, axis=0), spans


def _unpack(rows2d, span, shape):
    at, rows = span
    size = 1
    for s in shape:
        size *= s
    return rows2d[at:at + rows].reshape(-1)[:size].reshape(shape)


def _join_cols(blocks):
    n, rows, cols = blocks.shape
    return jnp.transpose(blocks, (1, 0, 2)).reshape(rows, n * cols)


def _split_cols(full):
    rows, cols = full.shape
    return jnp.transpose(full.reshape(rows, NDEV, cols // NDEV), (1, 0, 2))


def _ffn_up(h, gain, w_gu, after, *, tm, tag):
    lp, d = h.shape
    f8 = w_gu.shape[3]
    n = _norm_fwd(h, gain, tm=_pick(lp, [256, 128]), name=f"{tag}_norm")
    hid = _spec((None, tm, f8), lambda i, j, k: (j, i, 0))
    a, b, s = _mm((lp // tm, NDEV, 1), n, _spec((tm, d), lambda i, j, k: (i, 0)),
                  [(w_gu, _spec((None, None, d, f8), lambda i, j, k: (j, 0, 0, 0))),
                   (w_gu, _spec((None, None, d, f8), lambda i, j, k: (j, 1, 0, 0)))],
                  [], [((NDEV, lp, f8), BF16, hid)] * 3, dims=NN, epi=_swiglu_epi, after=after, name=f"{tag}_up")
    return n, a, b, s


def _ffn_down(h, s, w_down, *, tm, tag):
    lp, d = h.shape
    f8 = w_down.shape[1]
    tn = _pick(d, [1024, 512, 256, 128])
    tile = _spec((tm, tn), lambda i, j, k: (i, j))
    return _mm((lp // tm, d // tn, NDEV), s, _spec((None, tm, f8), lambda i, j, k: (k, i, 0)),
               [(w_down, _spec((None, f8, tn), lambda i, j, k: (k, 0, j)))], [(h, tile)], [((lp, d), F32, tile)],
               dims=NN, acc_shape=(tm, tn), epi=lambda accs, ex: [ex[0] + FFN_RESIDUAL_WEIGHT * accs[0]], name=f"{tag}_down")[0]


def _ffn_bwd_dw(dh_half, saved, w_down, on_down, *, tm, tag):
    n, a, b, s = saved
    lp, d = dh_half.shape
    f8 = w_down.shape[1]
    tkr = _pick(lp, [1088, 544, 256, 128])
    tn = _pick(d, [1024, 512, 256, 128])
    hid = _spec((None, tm, f8), lambda i, j, k: (j, i, 0))
    dab, = _mm((lp // tm, NDEV, 1), dh_half, _spec((tm, d), lambda i, j, k: (i, 0)),
               [(w_down, _spec((None, f8, d), lambda i, j, k: (j, 0, 0)))], [(a, hid), (b, hid)],
               [((2, NDEV, lp, f8), BF16, _spec((2, None, tm, f8), lambda i, j, k: (0, j, i, 0)))],
               dims=NT, epi=_swiglu_bwd_epi, name=f"{tag}_down_dx")
    dw_down, = _mm((NDEV, d // tn, lp // tkr), s, _spec((None, tkr, f8), lambda i, j, k: (i, k, 0)),
                   [(dh_half, _spec((tkr, tn), lambda i, j, k: (k, j)))], [],
                   [((NDEV, f8, d), BF16, _spec((None, f8, tn), lambda i, j, k: (i, 0, j)))],
                   dims=TN, acc_shape=(f8, tn), name=f"{tag}_down_dw")
    dw_gu, = _mm((d // tn, 2 * NDEV, lp // tkr), n, _spec((tkr, tn), lambda i, j, k: (k, i)),
                 [(dab, _spec((None, None, tkr, f8), lambda i, j, k: (j // NDEV, j % NDEV, k, 0)))], [],
                 [((NDEV, 2, d, f8), BF16, _spec((None, None, tn, f8), lambda i, j, k: (j % NDEV, j // NDEV, i, 0)))],
                 dims=TN, acc_shape=(tn, f8), after=on_down(dw_down), name=f"{tag}_gate_up_dw")
    return dab, dw_gu, dw_down


def _ffn_bwd_dx(dab, w_gu, after, *, tm, tag):
    lp, f8 = dab.shape[2], dab.shape[3]
    d = w_gu.shape[2]
    tn = _pick(d, [1024, 512, 256, 128])
    return _mm((lp // tm, d // tn, 2 * NDEV), dab, _spec((None, None, tm, f8), lambda i, j, k: (k // NDEV, k % NDEV, i, 0)),
               [(w_gu, _spec((None, None, tn, f8), lambda i, j, k: (k % NDEV, k // NDEV, j, 0)))], [],
               [((lp, d), F32, _spec((tm, tn), lambda i, j, k: (i, j)))],
               dims=NT, acc_shape=(tm, tn), after=after, name=f"{tag}_gate_up_dx")[0]


def kernel(x, meta_tokens, ffn1_norm, ffn1_w_gate, ffn1_w_up, ffn1_w_down, mix_norm, w_in, b_forget, fox_q_norm, fox_k_norm, w_branch_fox, w_branch_sb, w_out, ffn2_norm, ffn2_w_gate, ffn2_w_up, ffn2_w_down, loss_target, m_meta_tokens, m_ffn1_norm, m_ffn1_w_gate, m_ffn1_w_up, m_ffn1_w_down, m_mix_norm, m_w_in, m_b_forget, m_fox_q_norm, m_fox_k_norm, m_w_branch_fox, m_w_branch_sb, m_w_out, m_ffn2_norm, m_ffn2_w_gate, m_ffn2_w_up, m_ffn2_w_down, v_meta_tokens, v_ffn1_norm, v_ffn1_w_gate, v_ffn1_w_up, v_ffn1_w_down, v_mix_norm, v_w_in, v_b_forget, v_fox_q_norm, v_fox_k_norm, v_w_branch_fox, v_w_branch_sb, v_w_out, v_ffn2_norm, v_ffn2_w_gate, v_ffn2_w_up, v_ffn2_w_down):
    seq, d = x.shape[1], x.shape[2]
    d8 = d // NDEV
    w = BRANCH_WIDTH
    tile = 256 if seq % 256 == 0 else 128
    pad = tile - N_META
    lp = tile + seq
    tm = _pick(lp, [1088, 544, 256, 128])
    tr = _pick(lp, [256, 128])
    tkr = _pick(lp, [1088, 544, 256, 128])
    nb = lp // tile
    me = _flat_id(*_my_place())

    shards = [jnp.stack([ffn1_w_gate[0], ffn1_w_up[0]]).astype(BF16), ffn1_w_down[0].astype(BF16), w_in[0].astype(BF16),
              w_branch_fox[0].astype(BF16), w_branch_sb[0].astype(BF16), w_out[0].astype(BF16),
              jnp.stack([ffn2_w_gate[0], ffn2_w_up[0]]).astype(BF16), ffn2_w_down[0].astype(BF16)]
    w_gu1, = _allgather(shards[:1], "gather_ffn1")
    down1_copies, token = _send_start(shards[1:2], [_landing(s, me) for s in shards[1:2]], False, w_gu1, "gather_down1_start")
    w_in_copies, token = _send_start(shards[2:3], [_landing(s, me) for s in shards[2:3]], False, token, "gather_w_in_start")
    mixer_copies, token = _send_start(shards[3:6], [_landing(s, me) for s in shards[3:6]], False, token, "gather_mixer_start")
    ffn2_copies, token = _send_start(shards[6:], [_landing(s, me) for s in shards[6:]], False, token, "gather_ffn2_start")
    meta_full = _join_cols(_allgather([meta_tokens], "gather_meta", in_vmem=True)[0])

    h0 = jnp.concatenate([jnp.zeros((pad, d), F32), meta_full.astype(F32), x[0]], axis=0)
    saved1 = _ffn_up(h0, ffn1_norm, w_gu1, token, tm=tm, tag="ffn1")
    w_down1, = _send_wait(down1_copies, False, saved1[3], "gather_down1_wait")
    h1 = _ffn_down(h0, saved1[3], w_down1, tm=tm, tag="ffn1")

    w_in_blocks, = _send_wait(w_in_copies, False, h1, "gather_w_in_wait")
    wi = _join_cols(w_in_blocks)
    w_pa = jnp.concatenate([wi[:, :2 * w], jnp.pad(wi[:, 3 * w:3 * w + HEADS], ((0, 0), (0, LANES - HEADS)))], axis=1)
    w_pb = jnp.concatenate([wi[:, 2 * w:3 * w], wi[:, 3 * w + HEADS:]], axis=1)
    na, nbw = w_pa.shape[1], w_pb.shape[1]
    gate_blk = 4 * w // d

    n2 = _norm_fwd(h1, mix_norm, tm=tr, name="mix_norm")
    tma = _pick(lp, [544, 256, 128])
    tnd = _pick(d, [1024, 512, 256, 128])
    tnb = _pick(nbw, [512, 256, 128])
    proj_a = _mm2d(n2, w_pa, na, dims=NN, tm=tma, tn=na, tk=d, out_dtype=F32, name="proj_a")
    proj_b = _mm2d(n2, w_pb, nbw, dims=NN, tm=tm, tn=tnb, tk=d, out_dtype=BF16, name="proj_b")
    b_pad = jnp.pad(b_forget, ((0, 0), (0, LANES - HEADS)))
    q_gain, k_gain = fox_q_norm.reshape(1, w), fox_k_norm.reshape(1, w)
    fq, fk, logf = _fox_prep(proj_a, q_gain, k_gain, b_pad, tm=tr, pad=pad, name="fox_prep")
    c = _cumsum_rows(logf, tile=tile, reverse=False, name="forget_cumsum")
    c_rows = jnp.transpose(c[:, :HEADS]).reshape(HEADS, nb, 1, tile)
    o_fox, lse = _fox_fwd(fq, fk, proj_b, 0, c_rows, tile=tile, pad=pad, name="fox_fwd")
    o_sb, sb_total = _sb_fwd(proj_b, HEADS, 2 * HEADS, 3 * HEADS, tile=tile, pad=pad, name="sb_fwd")
    w_br_fox, w_br_sb, w_out_blocks = _send_wait(mixer_copies, False, o_sb, "gather_mixer_wait")
    w_out_full = w_out_blocks.reshape(d, d)

    def branch(o, w_blocks, name):
        return _mm((lp // tm, NDEV, 1), o, _spec((tm, w), lambda i, j, k: (i, 0)),
                   [(w_blocks, _spec((None, w, d8), lambda i, j, k: (j, 0, 0)))], [],
                   [((lp, d), BF16, _spec((tm, d8), lambda i, j, k: (i, j)))], dims=NN, name=name)[0]

    br_fox = branch(o_fox, w_br_fox, "branch_fox")
    br_sb = branch(o_sb, w_br_sb, "branch_sb")

    def merge_fn(i, tiles, consts):
        bf_, bs_, gf_, gs_ = [t.astype(F32) for t in tiles]
        return [_sigmoid(gf_) * bf_ + _sigmoid(gs_) * bs_], []

    gates_in = [(proj_b, d, gate_blk), (proj_b, d, gate_blk + 1)]
    merged, = _rowwise(merge_fn, [_whole(br_fox), _whole(br_sb)] + gates_in, [], [(d, BF16)], [], tm=tr, name="merge")
    h2 = _mm2d(merged, w_out_full, d, dims=NN, tm=tm, tn=tnd, tk=d, out_dtype=F32,
               epi=lambda accs, ex: [ex[0] + accs[0]], extras=[h1], name="out_proj")

    w_gu2, w_down2 = _send_wait(ffn2_copies, False, h2, "gather_ffn2_wait")
    saved3 = _ffn_up(h2, ffn2_norm, w_gu2, None, tm=tm, tag="ffn2")
    h3 = _ffn_down(h2, saved3[3], w_down2, tm=tm, tag="ffn2")

    skip = tile // tr

    def loss_fn(i, tiles, consts):
        real = i >= skip
        err = jnp.where(real, tiles[0] - tiles[1], 0.0)
        dy = err * (1.0 / d)
        part = 0.5 * jnp.sum(err * dy, axis=0, keepdims=True)
        return [dy, FFN_RESIDUAL_WEIGHT * dy], [part]

    dh3, dh3_half, loss_cols = _rowwise(
        loss_fn, [_whole(h3), _whole(loss_target[0])], [], [(d, F32, lp), (d, BF16, lp)], [(1, d)], tm=tr, name="loss",
        row_maps=[None, lambda i: jnp.maximum(i - skip, 0)])

    def own(g):
        return lax.dynamic_index_in_dim(g, me, 0, keepdims=False)

    def send_grads(grads_, name):
        return _send_start(grads_, [_landing(own(g), me) for g in grads_], True, grads_[-1], name)

    sends = {}

    def send_piece(key):
        def on_ready(g):
            sends[key], token_ = send_grads([g], f"exchange_{key}_start")
            return token_
        return on_ready

    dab3, dw_gu2, _ = _ffn_bwd_dw(dh3_half, saved3, w_down2, send_piece("down2"), tm=tm, tag="ffn2")
    dn3 = _ffn_bwd_dx(dab3, w_gu2, send_piece("gu2")(dw_gu2), tm=tm, tag="ffn2")
    dh2, dh2_bf, _, dg_ffn2 = _norm_bwd(dn3, h2, ffn2_norm, dh3, tm=tr, name="ffn2_norm_bwd")

    dmerged = _mm2d(dh2_bf, w_out_full, d, dims=NT, tm=tm, tn=tnd, tk=d, out_dtype=BF16, name="out_proj_dx")
    dw_out = _mm2d(merged, dh2_bf, d, dims=TN, tm=tnd, tn=tnd, tk=tkr, out_dtype=BF16, name="out_proj_dw")

    def merge_bwd_fn(i, tiles, consts):
        dm, bf_, bs_, gf_, gs_ = [t.astype(F32) for t in tiles]
        sf, ss = _sigmoid(gf_), _sigmoid(gs_)
        return [dm * sf, dm * ss, dm * bf_ * sf * (1.0 - sf), dm * bs_ * ss * (1.0 - ss)], []

    dbr_fox, dbr_sb, dg_fox, dg_sb = _rowwise(
        merge_bwd_fn, [_whole(dmerged), _whole(br_fox), _whole(br_sb)] + gates_in, [], [(d, BF16)] * 4, [], tm=tr, name="merge_bwd")

    tnw = _pick(w, [512, 256, 128])

    def branch_dx(dbr, w_blocks, name):
        return _mm((lp // tm, w // tnw, NDEV), dbr, _spec((tm, d8), lambda i, j, k: (i, k)),
                   [(w_blocks, _spec((None, tnw, d8), lambda i, j, k: (k, j, 0)))], [],
                   [((lp, w), BF16, _spec((tm, tnw), lambda i, j, k: (i, j)))], dims=NT, acc_shape=(tm, tnw), name=name)[0]

    def branch_dw(o, dbr, name):
        return _mm((w // tnw, NDEV, lp // tkr), o, _spec((tkr, tnw), lambda i, j, k: (k, i)),
                   [(dbr, _spec((tkr, d8), lambda i, j, k: (k, j)))], [],
                   [((NDEV, w, d8), BF16, _spec((None, tnw, d8), lambda i, j, k: (j, i, 0)))],
                   dims=TN, acc_shape=(tnw, d8), name=name)[0]

    do_fox = branch_dx(dbr_fox, w_br_fox, "branch_fox_dx")
    do_sb = branch_dx(dbr_sb, w_br_sb, "branch_sb_dx")
    dw_br_fox = branch_dw(o_fox, dbr_fox, "branch_fox_dw")
    dw_br_sb = branch_dw(o_sb, dbr_sb, "branch_sb_dw")

    dfq, dfk, dfv, dc_rows = _fox_bwd(fq, fk, proj_b, 0, c_rows, o_fox, do_fox, lse, tile=tile, pad=pad, name="fox_bwd")
    dsq, dsk, dsv = _sb_bwd(proj_b, HEADS, 2 * HEADS, 3 * HEADS, do_sb, sb_total, tile=tile, pad=pad, name="sb_bwd")
    dc = jnp.pad(jnp.transpose(dc_rows.reshape(HEADS, lp)), ((0, 0), (0, LANES - HEADS)))
    dlogf = _cumsum_rows(dc, tile=tile, reverse=True, name="forget_cumsum_bwd")
    dproj_a, dg_q, dg_k, dg_b = _fox_prep_bwd(proj_a, dfq, dfk, dlogf, q_gain, k_gain, b_pad, tm=tr, pad=pad, name="fox_prep_bwd")
    dproj_b = jnp.concatenate([dfv, dsq, dsk, dsv, dg_fox, dg_sb], axis=1)

    dw_pa = _mm2d(n2, dproj_a, na, dims=TN, tm=tnd, tn=na, tk=_pick(lp, [544, 256, 128]), out_dtype=BF16, name="proj_a_dw")
    dw_pb = _mm2d(n2, dproj_b, nbw, dims=TN, tm=tnd, tn=tnb, tk=tkr, out_dtype=BF16, name="proj_b_dw")
    dn2_a = _mm2d(dproj_a, w_pa, d, dims=NT, tm=tm, tn=tnd, tk=na, out_dtype=F32, name="proj_a_dx")
    dn2 = _mm2d(dproj_b, w_pb, d, dims=NT, tm=tm, tn=tnd, tk=tnb, out_dtype=F32,
                epi=lambda accs, ex: [accs[0] + ex[0]], extras=[dn2_a], name="proj_b_dx")
    dw_in = jnp.concatenate([dw_pa[:, :2 * w], dw_pb[:, :w], dw_pa[:, 2 * w:2 * w + HEADS], dw_pb[:, w:]], axis=1)
    mixer_grads = [_split_cols(dw_in), dw_br_fox, dw_br_sb, dw_out.reshape(NDEV, d8, d)]
    mixer_sends, token = send_grads(mixer_grads, "exchange_mixer_start")
    dh1, _, dh1_half, dg_mix = _norm_bwd(dn2, h1, mix_norm, dh2, tm=tr, after=token, name="mix_norm_bwd")

    dab1, dw_gu1, _ = _ffn_bwd_dw(dh1_half, saved1, w_down1, send_piece("down1"), tm=tm, tag="ffn1")
    dn1 = _ffn_bwd_dx(dab1, w_gu1, send_piece("gu1")(dw_gu1), tm=tm, tag="ffn1")
    dh0, _, _, dg_ffn1 = _norm_bwd(dn1, h0, ffn1_norm, dh1, tm=tr, name="ffn1_norm_bwd")
    grad_x = dh0[tile:][None]

    r_down2, = _send_wait(sends["down2"], True, dh0, "exchange_down2_wait")
    r_gu2, = _send_wait(sends["gu2"], True, dh0, "exchange_gu2_wait")
    r_in, r_br_fox, r_br_sb, r_out = _send_wait(mixer_sends, True, dh0, "exchange_mixer_wait")
    grads, deltas, new_ms, new_vs = {}, {}, {}, {}

    def adamw_big(entries):
        for k, (parts, sel, wt, mt, vt) in entries.items():
            g, dl, mn, vn = _adamw_summed(parts, sel, wt[0], mt[0], vt[0], name=f"adamw_{k}")
            grads[k], deltas[k], new_ms[k], new_vs[k] = g[None], dl[None], mn[None], vn[None]

    adamw_big(dict(ffn2_w_gate=(r_gu2, 0, ffn2_w_gate, m_ffn2_w_gate, v_ffn2_w_gate),
                   ffn2_w_up=(r_gu2, 1, ffn2_w_up, m_ffn2_w_up, v_ffn2_w_up),
                   ffn2_w_down=(r_down2, 0, ffn2_w_down, m_ffn2_w_down, v_ffn2_w_down),
                   w_in=(r_in, 0, w_in, m_w_in, v_w_in),
                   w_branch_fox=(r_br_fox, 0, w_branch_fox, m_w_branch_fox, v_w_branch_fox),
                   w_branch_sb=(r_br_sb, 0, w_branch_sb, m_w_branch_sb, v_w_branch_sb),
                   w_out=(r_out, 0, w_out, m_w_out, v_w_out)))
    r_down1, = _send_wait(sends["down1"], True, new_vs["w_out"], "exchange_down1_wait")
    r_gu1, = _send_wait(sends["gu1"], True, r_down1, "exchange_gu1_wait")
    adamw_big(dict(ffn1_w_gate=(r_gu1, 0, ffn1_w_gate, m_ffn1_w_gate, v_ffn1_w_gate),
                   ffn1_w_up=(r_gu1, 1, ffn1_w_up, m_ffn1_w_up, v_ffn1_w_up),
                   ffn1_w_down=(r_down1, 0, ffn1_w_down, m_ffn1_w_down, v_ffn1_w_down)))

    small_parts = [dh0[pad:tile], dg_ffn1, dg_mix, dg_ffn2, dg_b[:, :HEADS], dg_q, dg_k, loss_cols]
    small_packed, small_spans = _pack_rows(small_parts, LANES, SMALL_ROWS)
    small_sum = _allsum_small(small_packed, "sum_small")
    g_meta_full, g_ffn1n, g_mixn, g_ffn2n, g_bf, g_qn, g_kn, loss_vec = [
        _unpack(small_sum, span, part.shape) for span, part in zip(small_spans, small_parts)]
    loss = jnp.sum(loss_vec)
    g_meta = lax.dynamic_slice_in_dim(g_meta_full, me * d8, d8, axis=1)
    g_qn, g_kn = g_qn.reshape(fox_q_norm.shape), g_kn.reshape(fox_k_norm.shape)

    small = dict(meta_tokens=(g_meta, meta_tokens, m_meta_tokens, v_meta_tokens),
                 ffn1_norm=(g_ffn1n, ffn1_norm, m_ffn1_norm, v_ffn1_norm),
                 mix_norm=(g_mixn, mix_norm, m_mix_norm, v_mix_norm),
                 b_forget=(g_bf, b_forget, m_b_forget, v_b_forget),
                 fox_q_norm=(g_qn, fox_q_norm, m_fox_q_norm, v_fox_q_norm),
                 fox_k_norm=(g_kn, fox_k_norm, m_fox_k_norm, v_fox_k_norm),
                 ffn2_norm=(g_ffn2n, ffn2_norm, m_ffn2_norm, v_ffn2_norm))
    for k, (g, wt, mt, vt) in small.items():
        flat = lambda t: t.reshape(-1, t.shape[-1])
        dl, mn, vn = _adamw_plain(flat(g), flat(wt), flat(mt), flat(vt), name=f"adamw_{k}")
        grads[k], deltas[k], new_ms[k], new_vs[k] = g, dl.reshape(wt.shape), mn.reshape(wt.shape), vn.reshape(wt.shape)

    order = ["meta_tokens", "ffn1_norm", "ffn1_w_gate", "ffn1_w_up", "ffn1_w_down", "mix_norm", "w_in", "b_forget",
             "fox_q_norm", "fox_k_norm", "w_branch_fox", "w_branch_sb", "w_out", "ffn2_norm", "ffn2_w_gate",
             "ffn2_w_up", "ffn2_w_down"]
    return (loss, grad_x, *[grads[k] for k in order], *[deltas[k] for k in order],
            *[new_ms[k] for k in order], *[new_vs[k] for k in order])
```

```python
import jax
import jax.numpy as jnp
from jax import lax
from jax.experimental import pallas as pl
from jax.experimental.pallas import tpu as pltpu

F32 = jnp.float32
BF16 = jnp.bfloat16
MESH = pl.DeviceIdType.MESH

NDEV = 8
N_META = 16
HEAD_DIM = 128
HEADS = 8
BRANCH_WIDTH = HEADS * HEAD_DIM
RMS_EPS = 1e-6
FFN_RESIDUAL_WEIGHT = 0.5
ATTN_SCALE = HEAD_DIM ** -0.5
MASKED_LOGIT = -1e30

ADAM_LR = 0.001
ADAM_B1 = 0.9
ADAM_B2 = 0.999
ADAM_EPS = 1e-08
ADAM_WD = 0.01
ADAM_STEP = 10

LANES = 128
SMALL_ROWS = 8
ADAMW_TILE_ELEMS = 160 * 1024
KSUB = 2


def _pick(n, prefs):
    for p in prefs:
        if p <= n and n % p == 0:
            return p
    return n


def _dot(a, b, dims):
    return lax.dot_general(a, b, (dims, ((), ())), preferred_element_type=F32)


NN = ((1,), (0,))
TN = ((0,), (0,))
NT = ((1,), (1,))


def _split_bf16(x):
    hi = x.astype(BF16)
    lo = (x - hi.astype(F32)).astype(BF16)
    return hi, lo


def _sigmoid_parts(z):
    e = jnp.exp(-jnp.abs(z))
    t = 1.0 + e
    return e, 1.0 / t, jnp.log(t)


def _sigmoid(x):
    e = jnp.exp(-jnp.abs(x))
    r = 1.0 / (1.0 + e)
    return jnp.where(x >= 0, r, e * r)


def _my_place():
    return lax.axis_index("x"), lax.axis_index("y"), lax.axis_index("c")


def _flat_id(px, py, pc):
    return 4 * px + 2 * py + pc


def _peer(x, y, c, k):
    px = 1 - x if k & 4 else x
    py = 1 - y if k & 2 else y
    pc = 1 - c if k & 1 else c
    return px, py, pc


def _allgather(shards, name, in_vmem=False):
    n = len(shards)

    def body(*refs):
        x_refs, out_refs = refs[:n], refs[n:2 * n]
        send_sems, recv_sems, local_sems = refs[2 * n:]
        x, y, c = _my_place()
        me, sibling = (x, y, c), (x, y, 1 - c)
        chips = [(1 - x, y), (x, 1 - y), (1 - x, 1 - y)]

        def block(a, place):
            return out_refs[a].at[_flat_id(*place)]

        def copy(a, k, place, to, src=None):
            return pltpu.make_async_remote_copy(
                src_ref=block(a, place) if src is None else src, dst_ref=block(a, place),
                send_sem=send_sems.at[7 * a + k], recv_sem=recv_sems.at[7 * a + k], device_id=to, device_id_type=MESH)

        mine = [pltpu.make_async_copy(x_refs[a], block(a, me), local_sems.at[a]) for a in range(n)]
        for cp in mine:
            cp.start()
        first = []
        for a in range(n):
            first.append(copy(a, 0, me, sibling, src=x_refs[a]))
            first += [copy(a, 1 + j, me, (*chip, c), src=x_refs[a]) for j, chip in enumerate(chips)]
        for cp in first:
            cp.start()
        passed = []
        for j, chip in enumerate(chips):
            for a in range(n):
                copy(a, 1 + j, (*chip, c), me).wait_recv()
                passed.append(copy(a, 4 + j, (*chip, c), sibling))
                passed[-1].start()
        for a in range(n):
            copy(a, 0, sibling, me).wait_recv()
        for j, chip in enumerate(chips):
            for a in range(n):
                copy(a, 4 + j, (*chip, 1 - c), me).wait_recv()
        for cp in first + passed:
            cp.wait_send()
        for cp in mine:
            cp.wait()

    space = pltpu.VMEM if in_vmem else pl.ANY
    return pl.pallas_call(
        body, name=name,
        out_shape=[jax.ShapeDtypeStruct((NDEV,) + s.shape, s.dtype) for s in shards],
        in_specs=[pl.BlockSpec(memory_space=space)] * n,
        out_specs=[pl.BlockSpec(memory_space=space)] * n,
        scratch_shapes=[pltpu.SemaphoreType.DMA((7 * n,)), pltpu.SemaphoreType.DMA((7 * n,)), pltpu.SemaphoreType.DMA((n,))],
    )(*shards)


def _allsum_small(part, name):
    rows, cols = part.shape

    def body(p_ref, out_ref, buf, send_sems, recv_sems):
        x, y, c = _my_place()
        me = _flat_id(x, y, c)
        buf[me] = p_ref[...]
        copies = []
        for k in range(1, NDEV):
            copies.append(pltpu.make_async_remote_copy(
                src_ref=p_ref, dst_ref=buf.at[me], send_sem=send_sems.at[k - 1], recv_sem=recv_sems.at[k - 1],
                device_id=_peer(x, y, c, k), device_id_type=MESH))
        for cp in copies:
            cp.start()
        for cp in copies:
            cp.wait()
        total = buf[0]
        for j in range(1, NDEV):
            total = total + buf[j]
        out_ref[...] = total

    return pl.pallas_call(
        body, name=name,
        out_shape=jax.ShapeDtypeStruct((rows, cols), F32),
        in_specs=[pl.BlockSpec(memory_space=pltpu.VMEM)],
        out_specs=pl.BlockSpec(memory_space=pltpu.VMEM),
        scratch_shapes=[pltpu.VMEM((NDEV, rows, cols), F32),
                        pltpu.SemaphoreType.DMA((7,)), pltpu.SemaphoreType.DMA((7,))],
    )(part)


_HBM = pl.BlockSpec(memory_space=pltpu.HBM)
_SEM = pl.BlockSpec(memory_space=pltpu.SEMAPHORE)
_DATAFLOW = pltpu.SideEffectType.DATAFLOW_SIDE_EFFECTING


def _send_copies(src_refs, land_refs, send_sems, recv_sems, per_peer):
    x, y, c = _my_place()
    me = _flat_id(x, y, c)
    copies = []
    for a, (src_ref, land_ref) in enumerate(zip(src_refs, land_refs)):
        for k in range(1, NDEV):
            peer = _peer(x, y, c, k)
            copies.append(pltpu.make_async_remote_copy(
                src_ref=src_ref.at[_flat_id(*peer)] if per_peer else src_ref, dst_ref=land_ref.at[me],
                send_sem=send_sems.at[7 * a + k - 1], recv_sem=recv_sems.at[7 * a + k - 1],
                device_id=peer, device_id_type=MESH))
    return copies


def _send_start(srcs, lands, per_peer, after, name):
    n = len(srcs)

    def body(*refs):
        for cp in _send_copies(refs[:n], refs[n:2 * n], refs[2 * n + 1], refs[2 * n + 2], per_peer):
            cp.start()
        refs[-1][...] = jnp.zeros_like(refs[-1])

    operands = [pltpu.with_memory_space_constraint(t, pltpu.HBM) for t in list(srcs) + list(lands)]
    outs = pl.pallas_call(
        body, name=name,
        out_shape=(pltpu.SemaphoreType.DMA((7 * n,)), pltpu.SemaphoreType.DMA((7 * n,)),
                   *[pltpu.HBM(t.shape, t.dtype) for t in operands], jax.ShapeDtypeStruct((SMALL_ROWS, LANES), F32)),
        in_specs=[_HBM] * (2 * n) + [pl.BlockSpec(memory_space=pl.ANY)],
        out_specs=(_SEM, _SEM, *[_HBM] * (2 * n), pl.BlockSpec(memory_space=pltpu.VMEM)),
        input_output_aliases={i: 2 + i for i in range(2 * n)},
        compiler_params=pltpu.CompilerParams(has_side_effects=_DATAFLOW),
    )(*operands, after)
    return (outs[0], outs[1], list(outs[2:2 + n]), list(outs[2 + n:2 + 2 * n])), outs[-1]


def _send_wait(handle, per_peer, after, name):
    send_sems, recv_sems, srcs, lands = handle
    n = len(srcs)

    def body(*refs):
        for cp in _send_copies(refs[:n], refs[n:2 * n], refs[2 * n], refs[2 * n + 1], per_peer):
            cp.wait_send()
            cp.wait_recv()

    outs = pl.pallas_call(
        body, name=name,
        out_shape=tuple(pltpu.HBM(t.shape, t.dtype) for t in srcs + lands),
        in_specs=[_HBM] * (2 * n) + [_SEM, _SEM, pl.BlockSpec(memory_space=pl.ANY)],
        out_specs=tuple([_HBM] * (2 * n)),
        input_output_aliases={i: i for i in range(2 * n)},
        compiler_params=pltpu.CompilerParams(has_side_effects=_DATAFLOW),
    )(*srcs, *lands, send_sems, recv_sems, after)
    return list(outs[n:])


def _landing(own_block, me):
    return lax.dynamic_update_index_in_dim(lax.empty((NDEV,) + own_block.shape, own_block.dtype), own_block, me, 0)


def _spec(block, index_map):
    return pl.BlockSpec(block, index_map)


def _mm(grid, a, a_spec, bs, extras, outs, *, dims, acc_shape=None, epi=None, after=None, name):
    nk = grid[2]
    nb, ne, no = len(bs), len(extras), len(outs)
    nafter = 0 if after is None else 1
    if epi is None:
        epi = lambda accs, ex: [accs[0]]

    def body(*refs):
        a_ref, b_refs = refs[0], refs[1:1 + nb]
        e_refs = refs[1 + nb:1 + nb + ne]
        o_refs = refs[1 + nb + ne + nafter:1 + nb + ne + nafter + no]
        acc_refs = refs[1 + nb + ne + nafter + no:]
        def finish(accs):
            for o_ref, tile in zip(o_refs, epi(accs, [e_ref[...] for e_ref in e_refs])):
                o_ref[...] = tile.astype(o_ref.dtype)

        def product(b_ref):
            if len(a_ref.shape) == 2:
                return _dot(a_ref[...], b_ref[...], dims)
            total = _dot(a_ref[0], b_ref[0], dims)
            for t in range(1, a_ref.shape[0]):
                total = total + _dot(a_ref[t], b_ref[t], dims)
            return total

        if nk == 1:
            finish([product(b_ref) for b_ref in b_refs])
        else:
            k = pl.program_id(2)

            @pl.when(k == 0)
            def _():
                for acc_ref in acc_refs:
                    acc_ref[...] = jnp.zeros_like(acc_ref)

            for acc_ref, b_ref in zip(acc_refs, b_refs):
                acc_ref[...] += product(b_ref)

            @pl.when(k == nk - 1)
            def _():
                finish([acc_ref[...] for acc_ref in acc_refs])

    return pl.pallas_call(
        body, name=name,
        grid=grid,
        in_specs=[a_spec] + [s for _, s in bs] + [s for _, s in extras] + [pl.BlockSpec(memory_space=pl.ANY)] * nafter,
        out_specs=[s for _, _, s in outs],
        out_shape=[jax.ShapeDtypeStruct(shape, dt) for shape, dt, _ in outs],
        scratch_shapes=[pltpu.VMEM(acc_shape, F32) for _ in bs] if nk > 1 else [],
        compiler_params=pltpu.CompilerParams(dimension_semantics=("parallel", "parallel", "arbitrary")),
    )(a, *[b for b, _ in bs], *[e for e, _ in extras], *([after] if nafter else []))


def _mm2d(a, b, n_cols, *, dims, tm, tn, tk, out_dtype, epi=None, extras=(), name):
    m_rows, k_len = (a.shape[1], a.shape[0]) if dims == TN else a.shape
    assert m_rows % tm == 0 and n_cols % tn == 0 and k_len % tk == 0, (name, a.shape, n_cols, tm, tn, tk)
    a_spec = _spec((tk, tm), lambda i, j, k: (k, i)) if dims == TN else _spec((tm, tk), lambda i, j, k: (i, k))
    b_spec = _spec((tn, tk), lambda i, j, k: (j, k)) if dims == NT else _spec((tk, tn), lambda i, j, k: (k, j))
    tile = _spec((tm, tn), lambda i, j, k: (i, j))
    return _mm((m_rows // tm, n_cols // tn, k_len // tk), a, a_spec, [(b, b_spec)], [(e, tile) for e in extras],
               [((m_rows, n_cols), out_dtype, tile)], dims=dims, acc_shape=(tm, tn), epi=epi, name=name)[0]


def _rowwise(fn, ins, consts, outs, sums, *, tm, name, row_maps=None, after=None):
    m_rows = outs[0][2] if len(outs[0]) == 3 else ins[0][0].shape[0]
    n = m_rows // tm
    ni, nc, no = len(ins), len(consts), len(outs)
    nafter = 0 if after is None else 1
    row_maps = row_maps or [None] * ni

    def body(*refs):
        i = pl.program_id(0)
        in_tiles = [r[...] for r in refs[:ni]]
        const_values = [r[...] for r in refs[ni:ni + nc]]
        o_refs = refs[ni + nc + nafter:ni + nc + nafter + no]
        s_refs = refs[ni + nc + nafter + no:]
        out_tiles, sum_terms = fn(i, in_tiles, const_values)
        for o_ref, tile in zip(o_refs, out_tiles):
            o_ref[...] = tile.astype(o_ref.dtype)
        if s_refs:
            @pl.when(i == 0)
            def _():
                for s_ref in s_refs:
                    s_ref[...] = jnp.zeros_like(s_ref)

            for s_ref, term in zip(s_refs, sum_terms):
                s_ref[...] += term

    def in_spec(width, col, rmap):
        if rmap is None:
            return pl.BlockSpec((tm, width), lambda i: (i, col))
        return pl.BlockSpec((tm, width), lambda i: (rmap(i), col))

    return pl.pallas_call(
        body, name=name,
        grid=(n,),
        in_specs=[in_spec(w, col, rmap) for (_, w, col), rmap in zip(ins, row_maps)]
        + [pl.BlockSpec(cst.shape, lambda i: (0, 0)) for cst in consts] + [pl.BlockSpec(memory_space=pl.ANY)] * nafter,
        out_specs=[pl.BlockSpec((tm, o[0]), lambda i: (i, 0)) for o in outs]
        + [pl.BlockSpec(s, lambda i: (0, 0)) for s in sums],
        out_shape=[jax.ShapeDtypeStruct((m_rows, o[0]), o[1]) for o in outs]
        + [jax.ShapeDtypeStruct(s, F32) for s in sums],
        compiler_params=pltpu.CompilerParams(dimension_semantics=("arbitrary",)),
    )(*[arr for arr, _, _ in ins], *consts, *([after] if nafter else []))


def _whole(arr):
    return (arr, arr.shape[1], 0)


def _rms(x, gain):
    r = lax.rsqrt(jnp.mean(x * x, axis=-1, keepdims=True) + RMS_EPS)
    return x * r * gain


def _rms_bwd(x, gain, dy):
    r = lax.rsqrt(jnp.mean(x * x, axis=-1, keepdims=True) + RMS_EPS)
    u = dy * gain
    dx = r * u - x * (r * r * r) * jnp.mean(x * u, axis=-1, keepdims=True)
    return dx, dy * x * r


def _norm_fwd(h, gain, *, tm, name):
    def fn(i, tiles, consts):
        return [_rms(tiles[0], consts[0])], []
    return _rowwise(fn, [_whole(h)], [gain], [(h.shape[1], BF16)], [], tm=tm, name=name)[0]


def _norm_bwd(dn, h, gain, dh_in, *, tm, name, after=None):
    d = h.shape[1]

    def fn(i, tiles, consts):
        dx, dg_rows = _rms_bwd(tiles[1], consts[0], tiles[0])
        dh = tiles[2] + dx
        return [dh, dh, FFN_RESIDUAL_WEIGHT * dh], [jnp.sum(dg_rows, axis=0, keepdims=True)]

    return _rowwise(fn, [_whole(dn), _whole(h), _whole(dh_in)], [gain],
                    [(d, F32), (d, BF16), (d, BF16)], [(1, d)], tm=tm, name=name, after=after)


def _swiglu_epi(accs, ex):
    a, b = accs
    return [a, b, a * _sigmoid(a) * b]


def _swiglu_bwd_epi(accs, ex):
    ds = accs[0]
    a, b = ex[0].astype(F32), ex[1].astype(F32)
    sig = _sigmoid(a)
    silu = a * sig
    dsilu = sig * (1.0 + a * (1.0 - sig))
    return [jnp.stack([(ds * b * dsilu).astype(BF16), (ds * silu).astype(BF16)], axis=0)]


def _attn_mask(i, j, tile, pad, strict):
    row = i * tile + lax.broadcasted_iota(jnp.int32, (tile, tile), 0)
    col = j * tile + lax.broadcasted_iota(jnp.int32, (tile, tile), 1)
    causal = (col < row) if strict else (col <= row)
    return causal & ((col >= pad) | (row < pad))


FWD_GROUP = 4
BWD_GROUP = 2


def _walk_key_tiles(i, step, carry, descending=False):
    if descending:
        carry = lax.fori_loop(jnp.maximum(i, 1), i + 1, lambda j, c: step(j, c, True), carry)
        carry = lax.fori_loop(0, jnp.maximum(i - 1, 0), lambda t, c: step(i - 1 - t, c, False), carry)
        return step(0, carry, True)
    carry = step(0, carry, True)
    carry = lax.fori_loop(1, i, lambda j, c: step(j, c, False), carry)
    return lax.fori_loop(jnp.maximum(i, 1), i + 1, lambda j, c: step(j, c, True), carry)


def _head_cols(g):
    return pl.ds(g * HEAD_DIM, HEAD_DIM)


def _key_tile(ref, j, tile, g):
    return ref[pl.ds(pl.multiple_of(j * tile, tile), tile), _head_cols(g)]


def _head_specs(lp, tile, q_off, k_off, v_off, hp):
    gw = hp * HEAD_DIM
    q_spec = pl.BlockSpec((tile, gw), lambda h, i: (i, h + q_off // hp))
    k_spec = pl.BlockSpec((lp, gw), lambda h, i: (0, h + k_off // hp))
    v_spec = pl.BlockSpec((lp, gw), lambda h, i: (0, h + v_off // hp))
    return q_spec, k_spec, v_spec


def _fox_fwd(q, k, v, v_off, c_rows, *, tile, pad, name):
    lp = q.shape[0]
    nb = lp // tile
    hp = FWD_GROUP
    gw = hp * HEAD_DIM

    def body(q_ref, k_ref, v_ref, c_ref, o_ref, lse_ref):
        i = pl.program_id(1)
        qts = [q_ref[:, _head_cols(g)] for g in range(hp)]

        def step(j, carry, masked):
            ok = _attn_mask(i, j, tile, pad, False) if masked else None
            out = []
            for g, (m, l, acc) in enumerate(carry):
                s = _dot(qts[g], _key_tile(k_ref, j, tile, g), NT) * ATTN_SCALE - c_ref[g, j]
                if masked:
                    s = jnp.where(ok, s, MASKED_LOGIT)
                m_new = jnp.maximum(m, jnp.max(s, axis=1, keepdims=True))
                p = jnp.exp(s - m_new)
                alpha = jnp.exp(m - m_new)
                l = alpha * l + jnp.sum(p, axis=1, keepdims=True)
                acc = alpha * acc + _dot(p.astype(BF16), _key_tile(v_ref, j, tile, g), NN)
                out.append((m_new, l, acc))
            return tuple(out)

        init = (jnp.full((tile, 1), MASKED_LOGIT, F32), jnp.zeros((tile, 1), F32), jnp.zeros((tile, HEAD_DIM), F32))
        final = _walk_key_tiles(i, step, (init,) * hp)
        for g, (m, l, acc) in enumerate(final):
            o_ref[:, _head_cols(g)] = (acc / l).astype(o_ref.dtype)
            lse_ref[g] = jnp.broadcast_to(m + jnp.log(l), (tile, LANES))

    q_spec, k_spec, v_spec = _head_specs(lp, tile, 0, 0, v_off, hp)
    return pl.pallas_call(
        body, name=name,
        grid=(HEADS // hp, nb),
        in_specs=[q_spec, k_spec, v_spec, pl.BlockSpec((hp, nb, 1, tile), lambda h, i: (h, 0, 0, 0))],
        out_specs=[pl.BlockSpec((tile, gw), lambda h, i: (i, h)),
                   pl.BlockSpec((hp, tile, LANES), lambda h, i: (h, i, 0))],
        out_shape=[jax.ShapeDtypeStruct((lp, BRANCH_WIDTH), BF16), jax.ShapeDtypeStruct((HEADS, lp, LANES), F32)],
        compiler_params=pltpu.CompilerParams(dimension_semantics=("parallel", "arbitrary")),
    )(q, k, v, c_rows)


def _fox_bwd(q, k, v, v_off, c_rows, o, do, lse, *, tile, pad, name):
    lp = q.shape[0]
    nb = lp // tile
    hp = BWD_GROUP
    gw = hp * HEAD_DIM

    def body(q_ref, k_ref, v_ref, c_ref, o_ref, do_ref, lse_ref, dq_ref, dk_ref, dv_ref, dc_ref, dk_acc, dv_acc, dc_acc):
        i = pl.program_id(1)

        @pl.when(i == 0)
        def _():
            dk_acc[...] = jnp.zeros_like(dk_acc)
            dv_acc[...] = jnp.zeros_like(dv_acc)
            dc_acc[...] = jnp.zeros_like(dc_acc)

        heads = range(hp)
        qts = [q_ref[:, _head_cols(g)] for g in heads]
        dots = [do_ref[:, _head_cols(g)] for g in heads]
        deltas = [jnp.sum(dots[g].astype(F32) * o_ref[:, _head_cols(g)].astype(F32), axis=1, keepdims=True) for g in heads]
        lse_cols = [lse_ref[g][:, :1] for g in heads]

        def step(j, dqs, masked):
            rows = pl.ds(pl.multiple_of(j * tile, tile), tile)
            ok = _attn_mask(i, j, tile, pad, False) if masked else None
            out = []
            for g in heads:
                kt = _key_tile(k_ref, j, tile, g)
                s = _dot(qts[g], kt, NT) * ATTN_SCALE - c_ref[g, j]
                p = jnp.exp(s - lse_cols[g])
                if masked:
                    p = jnp.where(ok, p, 0.0)
                dp = _dot(dots[g], _key_tile(v_ref, j, tile, g), NT)
                ds = p * (dp - deltas[g])
                dsb = ds.astype(BF16)
                dk_acc[rows, _head_cols(g)] += _dot(dsb, qts[g], TN)
                dv_acc[rows, _head_cols(g)] += _dot(p.astype(BF16), dots[g], TN)
                dc_acc[g, j] += -jnp.sum(ds, axis=0, keepdims=True)
                out.append(dqs[g] + _dot(dsb, kt, NN))
            return tuple(out)

        dqs = _walk_key_tiles(i, step, (jnp.zeros((tile, HEAD_DIM), F32),) * hp)
        for g in heads:
            dq_ref[:, _head_cols(g)] = dqs[g] * ATTN_SCALE

        @pl.when(i == nb - 1)
        def _():
            dk_ref[...] = dk_acc[...] * ATTN_SCALE
            dv_ref[...] = dv_acc[...].astype(dv_ref.dtype)
            dc_ref[...] = dc_acc[...]

    q_spec, k_spec, v_spec = _head_specs(lp, tile, 0, 0, v_off, hp)
    tile_spec = pl.BlockSpec((tile, gw), lambda h, i: (i, h))
    head_spec = pl.BlockSpec((lp, gw), lambda h, i: (0, h))
    c_spec = pl.BlockSpec((hp, nb, 1, tile), lambda h, i: (h, 0, 0, 0))
    return pl.pallas_call(
        body, name=name,
        grid=(HEADS // hp, nb),
        in_specs=[q_spec, k_spec, v_spec, c_spec, tile_spec, tile_spec,
                  pl.BlockSpec((hp, tile, LANES), lambda h, i: (h, i, 0))],
        out_specs=[tile_spec, head_spec, head_spec, c_spec],
        out_shape=[jax.ShapeDtypeStruct((lp, BRANCH_WIDTH), F32), jax.ShapeDtypeStruct((lp, BRANCH_WIDTH), F32),
                   jax.ShapeDtypeStruct((lp, BRANCH_WIDTH), BF16), jax.ShapeDtypeStruct((HEADS, nb, 1, tile), F32)],
        scratch_shapes=[pltpu.VMEM((lp, gw), F32), pltpu.VMEM((lp, gw), F32),
                        pltpu.VMEM((hp, nb, 1, tile), F32)],
        compiler_params=pltpu.CompilerParams(dimension_semantics=("parallel", "arbitrary")),
    )(q, k, v, c_rows, o, do, lse)


def _later_matrix(tile):
    return (lax.broadcasted_iota(jnp.int32, (tile, tile), 0) > lax.broadcasted_iota(jnp.int32, (tile, tile), 1)).astype(BF16)


def _earlier_matrix(tile):
    return (lax.broadcasted_iota(jnp.int32, (tile, tile), 0) < lax.broadcasted_iota(jnp.int32, (tile, tile), 1)).astype(BF16)


def _sb_tile(qt, kt, ok, later):
    z = _dot(qt, kt, NT) * ATTN_SCALE
    e, r, lg = _sigmoid_parts(z)
    sp = jnp.maximum(z, 0.0) + lg
    spm = sp if ok is None else jnp.where(ok, sp, 0.0)
    hi, lo = _split_bf16(spm)
    within = _dot(hi, later, NN) + _dot(lo, later, NN)
    return z, e, r, sp, spm, within


def _sb_fwd(qkv, q_off, k_off, v_off, *, tile, pad, name):
    lp = qkv.shape[0]
    nb = lp // tile
    hp = FWD_GROUP
    gw = hp * HEAD_DIM

    def body(q_ref, k_ref, v_ref, o_ref, tot_ref):
        i = pl.program_id(1)
        qts = [q_ref[:, _head_cols(g)] for g in range(hp)]
        later = _later_matrix(tile)

        def step(j, carry, masked):
            ok = _attn_mask(i, j, tile, pad, True) if masked else None
            out = []
            for g, (right, acc) in enumerate(carry):
                z, _, _, sp, spm, within = _sb_tile(qts[g], _key_tile(k_ref, j, tile, g), ok, later)
                w = jnp.exp(z - sp - within - right)
                if masked:
                    w = jnp.where(ok, w, 0.0)
                acc = acc + _dot(w.astype(BF16), _key_tile(v_ref, j, tile, g), NN)
                out.append((right + jnp.sum(spm, axis=1, keepdims=True), acc))
            return tuple(out)

        init = (jnp.zeros((tile, 1), F32), jnp.zeros((tile, HEAD_DIM), F32))
        final = _walk_key_tiles(i, step, (init,) * hp, descending=True)
        for g, (total, acc) in enumerate(final):
            o_ref[:, _head_cols(g)] = acc.astype(o_ref.dtype)
            tot_ref[g] = jnp.broadcast_to(total, (tile, LANES))

    q_spec, k_spec, v_spec = _head_specs(lp, tile, q_off, k_off, v_off, hp)
    return pl.pallas_call(
        body, name=name,
        grid=(HEADS // hp, nb),
        in_specs=[q_spec, k_spec, v_spec],
        out_specs=[pl.BlockSpec((tile, gw), lambda h, i: (i, h)),
                   pl.BlockSpec((hp, tile, LANES), lambda h, i: (h, i, 0))],
        out_shape=[jax.ShapeDtypeStruct((lp, BRANCH_WIDTH), BF16), jax.ShapeDtypeStruct((HEADS, lp, LANES), F32)],
        compiler_params=pltpu.CompilerParams(dimension_semantics=("parallel", "arbitrary")),
    )(qkv, qkv, qkv)


def _sb_bwd(qkv, q_off, k_off, v_off, do, total, *, tile, pad, name):
    lp = qkv.shape[0]
    nb = lp // tile
    hp = BWD_GROUP
    gw = hp * HEAD_DIM

    def body(q_ref, k_ref, v_ref, do_ref, tot_ref, dq_ref, dk_ref, dv_ref, dk_acc, dv_acc):
        i = pl.program_id(1)

        @pl.when(i == 0)
        def _():
            dk_acc[...] = jnp.zeros_like(dk_acc)
            dv_acc[...] = jnp.zeros_like(dv_acc)

        heads = range(hp)
        qts = [q_ref[:, _head_cols(g)] for g in heads]
        dots = [do_ref[:, _head_cols(g)] for g in heads]
        total_cols = [tot_ref[g][:, :1] for g in heads]
        later, earlier = _later_matrix(tile), _earlier_matrix(tile)

        def step(j, carry, masked):
            rows = pl.ds(pl.multiple_of(j * tile, tile), tile)
            ok = _attn_mask(i, j, tile, pad, True) if masked else None
            out = []
            for g, (dq, sp_before, dlw_before) in enumerate(carry):
                kt = _key_tile(k_ref, j, tile, g)
                z, e, r, sp, spm, within = _sb_tile(qts[g], kt, ok, later)
                sp_here = jnp.sum(spm, axis=1, keepdims=True)
                right = total_cols[g] - sp_before - sp_here
                w = jnp.exp(z - sp - within - right)
                if masked:
                    w = jnp.where(ok, w, 0.0)
                dlw = w * _dot(dots[g], _key_tile(v_ref, j, tile, g), NT)
                hi, lo = _split_bf16(dlw)
                before = dlw_before + _dot(hi, earlier, NN) + _dot(lo, earlier, NN)
                sig = jnp.where(z >= 0, r, e * r)
                dz = dlw * (1.0 - sig) - sig * before
                if masked:
                    dz = jnp.where(ok, dz, 0.0)
                dzb = dz.astype(BF16)
                dk_acc[rows, _head_cols(g)] += _dot(dzb, qts[g], TN)
                dv_acc[rows, _head_cols(g)] += _dot(w.astype(BF16), dots[g], TN)
                out.append((dq + _dot(dzb, kt, NN), sp_before + sp_here, dlw_before + jnp.sum(dlw, axis=1, keepdims=True)))
            return tuple(out)

        zero_col = jnp.zeros((tile, 1), F32)
        final = _walk_key_tiles(i, step, ((jnp.zeros((tile, HEAD_DIM), F32), zero_col, zero_col),) * hp)
        for g in heads:
            dq_ref[:, _head_cols(g)] = (final[g][0] * ATTN_SCALE).astype(dq_ref.dtype)

        @pl.when(i == nb - 1)
        def _():
            dk_ref[...] = (dk_acc[...] * ATTN_SCALE).astype(dk_ref.dtype)
            dv_ref[...] = dv_acc[...].astype(dv_ref.dtype)

    q_spec, k_spec, v_spec = _head_specs(lp, tile, q_off, k_off, v_off, hp)
    tile_spec = pl.BlockSpec((tile, gw), lambda h, i: (i, h))
    head_spec = pl.BlockSpec((lp, gw), lambda h, i: (0, h))
    return pl.pallas_call(
        body, name=name,
        grid=(HEADS // hp, nb),
        in_specs=[q_spec, k_spec, v_spec, tile_spec, pl.BlockSpec((hp, tile, LANES), lambda h, i: (h, i, 0))],
        out_specs=[tile_spec, head_spec, head_spec],
        out_shape=[jax.ShapeDtypeStruct((lp, BRANCH_WIDTH), BF16)] * 3,
        scratch_shapes=[pltpu.VMEM((lp, gw), F32), pltpu.VMEM((lp, gw), F32)],
        compiler_params=pltpu.CompilerParams(dimension_semantics=("parallel", "arbitrary")),
    )(qkv, qkv, qkv, do, total)


def _cumsum_rows(x, *, tile, reverse, name):
    lp = x.shape[0]
    nb = lp // tile

    def body(x_ref, o_ref, carry):
        @pl.when(pl.program_id(0) == 0)
        def _():
            carry[...] = jnp.zeros_like(carry)

        r = lax.broadcasted_iota(jnp.int32, (tile, tile), 0)
        c = lax.broadcasted_iota(jnp.int32, (tile, tile), 1)
        tri = ((c >= r) if reverse else (c <= r)).astype(BF16)
        hi, lo = _split_bf16(x_ref[...])
        run = _dot(tri, hi, NN) + _dot(tri, lo, NN) + carry[...]
        o_ref[...] = run
        carry[...] = run[:1, :] if reverse else run[tile - 1:, :]

    order = (lambda i: (nb - 1 - i, 0)) if reverse else (lambda i: (i, 0))
    return pl.pallas_call(
        body, name=name,
        grid=(nb,),
        in_specs=[pl.BlockSpec((tile, LANES), order)],
        out_specs=pl.BlockSpec((tile, LANES), order),
        out_shape=jax.ShapeDtypeStruct((lp, LANES), F32),
        scratch_shapes=[pltpu.VMEM((1, LANES), F32)],
        compiler_params=pltpu.CompilerParams(dimension_semantics=("arbitrary",)),
    )(x)


def _log_sigmoid(x):
    return jnp.minimum(x, 0.0) - jnp.log(1.0 + jnp.exp(-jnp.abs(x)))


def _forget_mask(i, tm, pad):
    row = i * tm + lax.broadcasted_iota(jnp.int32, (tm, LANES), 0)
    lane = lax.broadcasted_iota(jnp.int32, (tm, LANES), 1)
    return (row >= pad) & (lane < HEADS)


def _fox_prep(proj_a, q_gain, k_gain, b_forget, *, tm, pad, name):
    w = BRANCH_WIDTH

    def fn(i, tiles, consts):
        pa = tiles[0]
        qs, ks = [], []
        for h in range(HEADS):
            lo = h * HEAD_DIM
            qs.append(_rms(pa[:, lo:lo + HEAD_DIM], consts[0][:, lo:lo + HEAD_DIM]))
            ks.append(_rms(pa[:, w + lo:w + lo + HEAD_DIM], consts[1][:, lo:lo + HEAD_DIM]))
        logf = jnp.where(_forget_mask(i, tm, pad), _log_sigmoid(pa[:, 2 * w:] + consts[2]), 0.0)
        return [jnp.concatenate(qs, axis=1), jnp.concatenate(ks, axis=1), logf], []

    return _rowwise(fn, [_whole(proj_a)], [q_gain, k_gain, b_forget],
                    [(w, BF16), (w, BF16), (LANES, F32)], [], tm=tm, name=name)


def _fox_prep_bwd(proj_a, dq, dk, dlogf, q_gain, k_gain, b_forget, *, tm, pad, name):
    w = BRANCH_WIDTH

    def fn(i, tiles, consts):
        pa, dqt, dkt, dlf = tiles
        dxs_q, dxs_k, dgs_q, dgs_k = [], [], [], []
        for h in range(HEADS):
            lo = h * HEAD_DIM
            dx, dg = _rms_bwd(pa[:, lo:lo + HEAD_DIM], consts[0][:, lo:lo + HEAD_DIM], dqt[:, lo:lo + HEAD_DIM])
            dxs_q.append(dx)
            dgs_q.append(jnp.sum(dg, axis=0, keepdims=True))
            dx, dg = _rms_bwd(pa[:, w + lo:w + lo + HEAD_DIM], consts[1][:, lo:lo + HEAD_DIM], dkt[:, lo:lo + HEAD_DIM])
            dxs_k.append(dx)
            dgs_k.append(jnp.sum(dg, axis=0, keepdims=True))
        xf = pa[:, 2 * w:] + consts[2]
        e, r, _ = _sigmoid_parts(xf)
        df = jnp.where(_forget_mask(i, tm, pad), dlf * jnp.where(xf >= 0, e * r, r), 0.0)
        return ([jnp.concatenate(dxs_q + dxs_k + [df], axis=1)],
                [jnp.concatenate(dgs_q, axis=1), jnp.concatenate(dgs_k, axis=1), jnp.sum(df, axis=0, keepdims=True)])

    return _rowwise(fn, [_whole(proj_a), _whole(dq), _whole(dk), _whole(dlogf)], [q_gain, k_gain, b_forget],
                    [(2 * w + LANES, BF16)], [(1, w), (1, w), (1, LANES)], tm=tm, name=name)


def _adamw_math(w, g, m, v):
    m = ADAM_B1 * m + (1.0 - ADAM_B1) * g
    v = ADAM_B2 * v + (1.0 - ADAM_B2) * (g * g)
    m_hat = m / (1.0 - ADAM_B1 ** ADAM_STEP)
    v_hat = v / (1.0 - ADAM_B2 ** ADAM_STEP)
    delta = -ADAM_LR * (m_hat / (jnp.sqrt(v_hat) + ADAM_EPS) + ADAM_WD * w)
    return delta, m, v


def _adamw_summed(parts, sel, w, m, v, *, name):
    rows, cols = w.shape
    tr = _pick(rows, [t for t in (512, 256, 128, 64, 32, 16, 8) if t * cols <= ADAMW_TILE_ELEMS])

    def body(p_ref, w_ref, m_ref, v_ref, g_out, d_out, m_out, v_out):
        g = p_ref[0].astype(F32)
        for j in range(1, NDEV):
            g = g + p_ref[j].astype(F32)
        delta, m_new, v_new = _adamw_math(w_ref[...], g, m_ref[...], v_ref[...])
        g_out[...] = g
        d_out[...] = delta
        m_out[...] = m_new
        v_out[...] = v_new

    spec = pl.BlockSpec((tr, cols), lambda i: (i, 0))
    if parts.ndim == 3:
        p_spec = pl.BlockSpec((NDEV, tr, cols), lambda i: (0, i, 0))
    else:
        p_spec = pl.BlockSpec((NDEV, None, tr, cols), lambda i: (0, sel, i, 0))
    return pl.pallas_call(
        body, name=name,
        grid=(rows // tr,),
        in_specs=[p_spec, spec, spec, spec],
        out_specs=[spec] * 4,
        out_shape=[jax.ShapeDtypeStruct((rows, cols), F32)] * 4,
        compiler_params=pltpu.CompilerParams(dimension_semantics=("parallel",)),
    )(parts, w, m, v)


def _adamw_plain(g, w, m, v, *, name):
    def body(g_ref, w_ref, m_ref, v_ref, d_out, m_out, v_out):
        delta, m_new, v_new = _adamw_math(w_ref[...], g_ref[...], m_ref[...], v_ref[...])
        d_out[...] = delta
        m_out[...] = m_new
        v_out[...] = v_new

    return pl.pallas_call(body, name=name, out_shape=[jax.ShapeDtypeStruct(w.shape, F32)] * 3)(g, w, m, v)


def _pack_rows(arrays, width, row_align):
    pieces, spans, at = [], [], 0
    for arr in arrays:
        flat = arr.reshape(-1)
        rows = -(-flat.shape[0] // (width * row_align)) * row_align
        flat = jnp.pad(flat, (0, rows * width - flat.shape[0]))
        pieces.append(flat.reshape(rows, width))
        spans.append((at, rows))
        at += rows
    return jnp.concatenate(pieces, axis=0), spans


def _unpack(rows2d, span, shape):
    at, rows = span
    size = 1
    for s in shape:
        size *= s
    return rows2d[at:at + rows].reshape(-1)[:size].reshape(shape)


def _join_cols(blocks):
    n, rows, cols = blocks.shape
    return jnp.transpose(blocks, (1, 0, 2)).reshape(rows, n * cols)


def _split_cols(full):
    rows, cols = full.shape
    return jnp.transpose(full.reshape(rows, NDEV, cols // NDEV), (1, 0, 2))


def _ffn_up(h, gain, w_gu, after, *, tm, tag):
    lp, d = h.shape
    f8 = w_gu.shape[3]
    n = _norm_fwd(h, gain, tm=_pick(lp, [256, 128]), name=f"{tag}_norm")
    hid = _spec((None, tm, f8), lambda i, j, k: (j, i, 0))
    a, b, s = _mm((lp // tm, NDEV, 1), n, _spec((tm, d), lambda i, j, k: (i, 0)),
                  [(w_gu, _spec((None, None, d, f8), lambda i, j, k: (j, 0, 0, 0))),
                   (w_gu, _spec((None, None, d, f8), lambda i, j, k: (j, 1, 0, 0)))],
                  [], [((NDEV, lp, f8), BF16, hid)] * 3, dims=NN, epi=_swiglu_epi, after=after, name=f"{tag}_up")
    return n, a, b, s


def _ffn_down(h, s, w_down, *, tm, tag):
    lp, d = h.shape
    f8 = w_down.shape[1]
    tn = _pick(d, [1024, 512, 256, 128])
    tile = _spec((tm, tn), lambda i, j, k: (i, j))
    return _mm((lp // tm, d // tn, NDEV // KSUB), s, _spec((KSUB, tm, f8), lambda i, j, k: (k, i, 0)),
               [(w_down, _spec((KSUB, f8, tn), lambda i, j, k: (k, 0, j)))], [(h, tile)], [((lp, d), F32, tile)],
               dims=NN, acc_shape=(tm, tn), epi=lambda accs, ex: [ex[0] + FFN_RESIDUAL_WEIGHT * accs[0]], name=f"{tag}_down")[0]


def _ffn_bwd_dw(dh_half, saved, w_down, on_down, *, tm, tag):
    n, a, b, s = saved
    lp, d = dh_half.shape
    f8 = w_down.shape[1]
    tkr = _pick(lp, [2176, 1088, 544, 256, 128])
    tn = _pick(d, [1024, 512, 256, 128])
    hid = _spec((None, tm, f8), lambda i, j, k: (j, i, 0))
    dab, = _mm((lp // tm, NDEV, 1), dh_half, _spec((tm, d), lambda i, j, k: (i, 0)),
               [(w_down, _spec((None, f8, d), lambda i, j, k: (j, 0, 0)))], [(a, hid), (b, hid)],
               [((2, NDEV, lp, f8), BF16, _spec((2, None, tm, f8), lambda i, j, k: (0, j, i, 0)))],
               dims=NT, epi=_swiglu_bwd_epi, name=f"{tag}_down_dx")
    dw_down, = _mm((NDEV, d // tn, lp // tkr), s, _spec((None, tkr, f8), lambda i, j, k: (i, k, 0)),
                   [(dh_half, _spec((tkr, tn), lambda i, j, k: (k, j)))], [],
                   [((NDEV, f8, d), BF16, _spec((None, f8, tn), lambda i, j, k: (i, 0, j)))],
                   dims=TN, acc_shape=(f8, tn), name=f"{tag}_down_dw")
    dw_gu, = _mm((d // tn, 2 * NDEV, lp // tkr), n, _spec((tkr, tn), lambda i, j, k: (k, i)),
                 [(dab, _spec((None, None, tkr, f8), lambda i, j, k: (j // NDEV, j % NDEV, k, 0)))], [],
                 [((NDEV, 2, d, f8), BF16, _spec((None, None, tn, f8), lambda i, j, k: (j % NDEV, j // NDEV, i, 0)))],
                 dims=TN, acc_shape=(tn, f8), after=on_down(dw_down), name=f"{tag}_gate_up_dw")
    return dab, dw_gu, dw_down


def _ffn_bwd_dx(dab, w_gu, after, *, tm, tag):
    lp, f8 = dab.shape[2], dab.shape[3]
    d = w_gu.shape[2]
    tn = _pick(d, [1024, 512, 256, 128])
    nsub = NDEV // KSUB
    return _mm((lp // tm, d // tn, 2 * nsub), dab, _spec((None, KSUB, tm, f8), lambda i, j, k: (k // nsub, k % nsub, i, 0)),
               [(w_gu, _spec((KSUB, None, tn, f8), lambda i, j, k: (k % nsub, k // nsub, j, 0)))], [],
               [((lp, d), F32, _spec((tm, tn), lambda i, j, k: (i, j)))],
               dims=NT, acc_shape=(tm, tn), after=after, name=f"{tag}_gate_up_dx")[0]


def kernel(x, meta_tokens, ffn1_norm, ffn1_w_gate, ffn1_w_up, ffn1_w_down, mix_norm, w_in, b_forget, fox_q_norm, fox_k_norm, w_branch_fox, w_branch_sb, w_out, ffn2_norm, ffn2_w_gate, ffn2_w_up, ffn2_w_down, loss_target, m_meta_tokens, m_ffn1_norm, m_ffn1_w_gate, m_ffn1_w_up, m_ffn1_w_down, m_mix_norm, m_w_in, m_b_forget, m_fox_q_norm, m_fox_k_norm, m_w_branch_fox, m_w_branch_sb, m_w_out, m_ffn2_norm, m_ffn2_w_gate, m_ffn2_w_up, m_ffn2_w_down, v_meta_tokens, v_ffn1_norm, v_ffn1_w_gate, v_ffn1_w_up, v_ffn1_w_down, v_mix_norm, v_w_in, v_b_forget, v_fox_q_norm, v_fox_k_norm, v_w_branch_fox, v_w_branch_sb, v_w_out, v_ffn2_norm, v_ffn2_w_gate, v_ffn2_w_up, v_ffn2_w_down):
    seq, d = x.shape[1], x.shape[2]
    d8 = d // NDEV
    w = BRANCH_WIDTH
    tile = 256 if seq % 256 == 0 else 128
    pad = tile - N_META
    lp = tile + seq
    tm = _pick(lp, [1088, 544, 256, 128])
    tr = _pick(lp, [256, 128])
    tkr = _pick(lp, [2176, 1088, 544, 256, 128])
    nb = lp // tile
    me = _flat_id(*_my_place())

    shards = [jnp.stack([ffn1_w_gate[0], ffn1_w_up[0]]).astype(BF16), ffn1_w_down[0].astype(BF16), w_in[0].astype(BF16),
              w_branch_fox[0].astype(BF16), w_branch_sb[0].astype(BF16), w_out[0].astype(BF16),
              jnp.stack([ffn2_w_gate[0], ffn2_w_up[0]]).astype(BF16), ffn2_w_down[0].astype(BF16)]
    w_gu1, = _allgather(shards[:1], "gather_ffn1")
    down1_copies, token = _send_start(shards[1:2], [_landing(s, me) for s in shards[1:2]], False, w_gu1, "gather_down1_start")
    w_in_copies, token = _send_start(shards[2:3], [_landing(s, me) for s in shards[2:3]], False, token, "gather_w_in_start")
    mixer_copies, token = _send_start(shards[3:6], [_landing(s, me) for s in shards[3:6]], False, token, "gather_mixer_start")
    ffn2_copies, token = _send_start(shards[6:], [_landing(s, me) for s in shards[6:]], False, token, "gather_ffn2_start")
    meta_full = _join_cols(_allgather([meta_tokens], "gather_meta", in_vmem=True)[0])

    h0 = jnp.concatenate([jnp.zeros((pad, d), F32), meta_full.astype(F32), x[0]], axis=0)
    saved1 = _ffn_up(h0, ffn1_norm, w_gu1, token, tm=tm, tag="ffn1")
    w_down1, = _send_wait(down1_copies, False, saved1[3], "gather_down1_wait")
    h1 = _ffn_down(h0, saved1[3], w_down1, tm=tm, tag="ffn1")

    w_in_blocks, = _send_wait(w_in_copies, False, h1, "gather_w_in_wait")
    wi = _join_cols(w_in_blocks)
    w_pa = jnp.concatenate([wi[:, :2 * w], jnp.pad(wi[:, 3 * w:3 * w + HEADS], ((0, 0), (0, LANES - HEADS)))], axis=1)
    w_pb = jnp.concatenate([wi[:, 2 * w:3 * w], wi[:, 3 * w + HEADS:]], axis=1)
    na, nbw = w_pa.shape[1], w_pb.shape[1]
    gate_blk = 4 * w // d

    n2 = _norm_fwd(h1, mix_norm, tm=tr, name="mix_norm")
    tma = _pick(lp, [544, 256, 128])
    tnd = _pick(d, [1024, 512, 256, 128])
    tnb = _pick(nbw, [512, 256, 128])
    proj_a = _mm2d(n2, w_pa, na, dims=NN, tm=tma, tn=na, tk=d, out_dtype=F32, name="proj_a")
    proj_b = _mm2d(n2, w_pb, nbw, dims=NN, tm=tm, tn=tnb, tk=d, out_dtype=BF16, name="proj_b")
    b_pad = jnp.pad(b_forget, ((0, 0), (0, LANES - HEADS)))
    q_gain, k_gain = fox_q_norm.reshape(1, w), fox_k_norm.reshape(1, w)
    fq, fk, logf = _fox_prep(proj_a, q_gain, k_gain, b_pad, tm=tr, pad=pad, name="fox_prep")
    c = _cumsum_rows(logf, tile=tile, reverse=False, name="forget_cumsum")
    c_rows = jnp.transpose(c[:, :HEADS]).reshape(HEADS, nb, 1, tile)
    o_fox, lse = _fox_fwd(fq, fk, proj_b, 0, c_rows, tile=tile, pad=pad, name="fox_fwd")
    o_sb, sb_total = _sb_fwd(proj_b, HEADS, 2 * HEADS, 3 * HEADS, tile=tile, pad=pad, name="sb_fwd")
    w_br_fox, w_br_sb, w_out_blocks = _send_wait(mixer_copies, False, o_sb, "gather_mixer_wait")
    w_out_full = w_out_blocks.reshape(d, d)

    def branch(o, w_blocks, name):
        return _mm((lp // tm, NDEV, 1), o, _spec((tm, w), lambda i, j, k: (i, 0)),
                   [(w_blocks, _spec((None, w, d8), lambda i, j, k: (j, 0, 0)))], [],
                   [((lp, d), BF16, _spec((tm, d8), lambda i, j, k: (i, j)))], dims=NN, name=name)[0]

    br_fox = branch(o_fox, w_br_fox, "branch_fox")
    br_sb = branch(o_sb, w_br_sb, "branch_sb")

    def merge_fn(i, tiles, consts):
        bf_, bs_, gf_, gs_ = [t.astype(F32) for t in tiles]
        return [_sigmoid(gf_) * bf_ + _sigmoid(gs_) * bs_], []

    gates_in = [(proj_b, d, gate_blk), (proj_b, d, gate_blk + 1)]
    merged, = _rowwise(merge_fn, [_whole(br_fox), _whole(br_sb)] + gates_in, [], [(d, BF16)], [], tm=tr, name="merge")
    h2 = _mm2d(merged, w_out_full, d, dims=NN, tm=tm, tn=tnd, tk=d, out_dtype=F32,
               epi=lambda accs, ex: [ex[0] + accs[0]], extras=[h1], name="out_proj")

    w_gu2, w_down2 = _send_wait(ffn2_copies, False, h2, "gather_ffn2_wait")
    saved3 = _ffn_up(h2, ffn2_norm, w_gu2, None, tm=tm, tag="ffn2")
    h3 = _ffn_down(h2, saved3[3], w_down2, tm=tm, tag="ffn2")

    skip = tile // tr

    def loss_fn(i, tiles, consts):
        real = i >= skip
        err = jnp.where(real, tiles[0] - tiles[1], 0.0)
        dy = err * (1.0 / d)
        part = 0.5 * jnp.sum(err * dy, axis=0, keepdims=True)
        return [dy, FFN_RESIDUAL_WEIGHT * dy], [part]

    dh3, dh3_half, loss_cols = _rowwise(
        loss_fn, [_whole(h3), _whole(loss_target[0])], [], [(d, F32, lp), (d, BF16, lp)], [(1, d)], tm=tr, name="loss",
        row_maps=[None, lambda i: jnp.maximum(i - skip, 0)])

    def own(g):
        return lax.dynamic_index_in_dim(g, me, 0, keepdims=False)

    def send_grads(grads_, name):
        return _send_start(grads_, [_landing(own(g), me) for g in grads_], True, grads_[-1], name)

    sends = {}

    def send_piece(key):
        def on_ready(g):
            sends[key], token_ = send_grads([g], f"exchange_{key}_start")
            return token_
        return on_ready

    dab3, dw_gu2, _ = _ffn_bwd_dw(dh3_half, saved3, w_down2, send_piece("down2"), tm=tm, tag="ffn2")
    dn3 = _ffn_bwd_dx(dab3, w_gu2, send_piece("gu2")(dw_gu2), tm=tm, tag="ffn2")
    dh2, dh2_bf, _, dg_ffn2 = _norm_bwd(dn3, h2, ffn2_norm, dh3, tm=tr, name="ffn2_norm_bwd")

    dmerged = _mm2d(dh2_bf, w_out_full, d, dims=NT, tm=tm, tn=tnd, tk=d, out_dtype=BF16, name="out_proj_dx")
    dw_out = _mm2d(merged, dh2_bf, d, dims=TN, tm=tnd, tn=tnd, tk=tkr, out_dtype=BF16, name="out_proj_dw")

    def merge_bwd_fn(i, tiles, consts):
        dm, bf_, bs_, gf_, gs_ = [t.astype(F32) for t in tiles]
        sf, ss = _sigmoid(gf_), _sigmoid(gs_)
        return [dm * sf, dm * ss, dm * bf_ * sf * (1.0 - sf), dm * bs_ * ss * (1.0 - ss)], []

    dbr_fox, dbr_sb, dg_fox, dg_sb = _rowwise(
        merge_bwd_fn, [_whole(dmerged), _whole(br_fox), _whole(br_sb)] + gates_in, [], [(d, BF16)] * 4, [], tm=tr, name="merge_bwd")

    tnw = _pick(w, [512, 256, 128])

    def branch_dx(dbr, w_blocks, name):
        return _mm((lp // tm, w // tnw, NDEV), dbr, _spec((tm, d8), lambda i, j, k: (i, k)),
                   [(w_blocks, _spec((None, tnw, d8), lambda i, j, k: (k, j, 0)))], [],
                   [((lp, w), BF16, _spec((tm, tnw), lambda i, j, k: (i, j)))], dims=NT, acc_shape=(tm, tnw), name=name)[0]

    def branch_dw(o, dbr, name):
        return _mm((w // tnw, NDEV, lp // tkr), o, _spec((tkr, tnw), lambda i, j, k: (k, i)),
                   [(dbr, _spec((tkr, d8), lambda i, j, k: (k, j)))], [],
                   [((NDEV, w, d8), BF16, _spec((None, tnw, d8), lambda i, j, k: (j, i, 0)))],
                   dims=TN, acc_shape=(tnw, d8), name=name)[0]

    do_fox = branch_dx(dbr_fox, w_br_fox, "branch_fox_dx")
    do_sb = branch_dx(dbr_sb, w_br_sb, "branch_sb_dx")
    dw_br_fox = branch_dw(o_fox, dbr_fox, "branch_fox_dw")
    dw_br_sb = branch_dw(o_sb, dbr_sb, "branch_sb_dw")

    dfq, dfk, dfv, dc_rows = _fox_bwd(fq, fk, proj_b, 0, c_rows, o_fox, do_fox, lse, tile=tile, pad=pad, name="fox_bwd")
    dsq, dsk, dsv = _sb_bwd(proj_b, HEADS, 2 * HEADS, 3 * HEADS, do_sb, sb_total, tile=tile, pad=pad, name="sb_bwd")
    dc = jnp.pad(jnp.transpose(dc_rows.reshape(HEADS, lp)), ((0, 0), (0, LANES - HEADS)))
    dlogf = _cumsum_rows(dc, tile=tile, reverse=True, name="forget_cumsum_bwd")
    dproj_a, dg_q, dg_k, dg_b = _fox_prep_bwd(proj_a, dfq, dfk, dlogf, q_gain, k_gain, b_pad, tm=tr, pad=pad, name="fox_prep_bwd")
    dproj_b = jnp.concatenate([dfv, dsq, dsk, dsv, dg_fox, dg_sb], axis=1)

    dw_pa = _mm2d(n2, dproj_a, na, dims=TN, tm=tnd, tn=na, tk=_pick(lp, [544, 256, 128]), out_dtype=BF16, name="proj_a_dw")
    dw_pb = _mm2d(n2, dproj_b, nbw, dims=TN, tm=tnd, tn=tnb, tk=tkr, out_dtype=BF16, name="proj_b_dw")
    dn2_a = _mm2d(dproj_a, w_pa, d, dims=NT, tm=tm, tn=tnd, tk=na, out_dtype=F32, name="proj_a_dx")
    dn2 = _mm2d(dproj_b, w_pb, d, dims=NT, tm=tm, tn=tnd, tk=_pick(nbw, [2048, 1024, 512, 256, 128]), out_dtype=F32,
                epi=lambda accs, ex: [accs[0] + ex[0]], extras=[dn2_a], name="proj_b_dx")
    dw_in = jnp.concatenate([dw_pa[:, :2 * w], dw_pb[:, :w], dw_pa[:, 2 * w:2 * w + HEADS], dw_pb[:, w:]], axis=1)
    mixer_grads = [_split_cols(dw_in), dw_br_fox, dw_br_sb, dw_out.reshape(NDEV, d8, d)]
    mixer_sends, token = send_grads(mixer_grads, "exchange_mixer_start")
    dh1, _, dh1_half, dg_mix = _norm_bwd(dn2, h1, mix_norm, dh2, tm=tr, after=token, name="mix_norm_bwd")

    dab1, dw_gu1, _ = _ffn_bwd_dw(dh1_half, saved1, w_down1, send_piece("down1"), tm=tm, tag="ffn1")
    dn1 = _ffn_bwd_dx(dab1, w_gu1, send_piece("gu1")(dw_gu1), tm=tm, tag="ffn1")
    dh0, _, _, dg_ffn1 = _norm_bwd(dn1, h0, ffn1_norm, dh1, tm=tr, name="ffn1_norm_bwd")
    grad_x = dh0[tile:][None]

    r_down2, = _send_wait(sends["down2"], True, dh0, "exchange_down2_wait")
    r_gu2, = _send_wait(sends["gu2"], True, dh0, "exchange_gu2_wait")
    r_in, r_br_fox, r_br_sb, r_out = _send_wait(mixer_sends, True, dh0, "exchange_mixer_wait")
    grads, deltas, new_ms, new_vs = {}, {}, {}, {}

    def adamw_big(entries):
        for k, (parts, sel, wt, mt, vt) in entries.items():
            g, dl, mn, vn = _adamw_summed(parts, sel, wt[0], mt[0], vt[0], name=f"adamw_{k}")
            grads[k], deltas[k], new_ms[k], new_vs[k] = g[None], dl[None], mn[None], vn[None]

    adamw_big(dict(ffn2_w_gate=(r_gu2, 0, ffn2_w_gate, m_ffn2_w_gate, v_ffn2_w_gate),
                   ffn2_w_up=(r_gu2, 1, ffn2_w_up, m_ffn2_w_up, v_ffn2_w_up),
                   ffn2_w_down=(r_down2, 0, ffn2_w_down, m_ffn2_w_down, v_ffn2_w_down),
                   w_in=(r_in, 0, w_in, m_w_in, v_w_in),
                   w_branch_fox=(r_br_fox, 0, w_branch_fox, m_w_branch_fox, v_w_branch_fox),
                   w_branch_sb=(r_br_sb, 0, w_branch_sb, m_w_branch_sb, v_w_branch_sb),
                   w_out=(r_out, 0, w_out, m_w_out, v_w_out)))
    r_down1, = _send_wait(sends["down1"], True, new_vs["w_out"], "exchange_down1_wait")
    r_gu1, = _send_wait(sends["gu1"], True, r_down1, "exchange_gu1_wait")
    adamw_big(dict(ffn1_w_gate=(r_gu1, 0, ffn1_w_gate, m_ffn1_w_gate, v_ffn1_w_gate),
                   ffn1_w_up=(r_gu1, 1, ffn1_w_up, m_ffn1_w_up, v_ffn1_w_up),
                   ffn1_w_down=(r_down1, 0, ffn1_w_down, m_ffn1_w_down, v_ffn1_w_down)))

    small_parts = [dh0[pad:tile], dg_ffn1, dg_mix, dg_ffn2, dg_b[:, :HEADS], dg_q, dg_k, loss_cols]
    small_packed, small_spans = _pack_rows(small_parts, LANES, SMALL_ROWS)
    small_sum = _allsum_small(small_packed, "sum_small")
    g_meta_full, g_ffn1n, g_mixn, g_ffn2n, g_bf, g_qn, g_kn, loss_vec = [
        _unpack(small_sum, span, part.shape) for span, part in zip(small_spans, small_parts)]
    loss = jnp.sum(loss_vec)
    g_meta = lax.dynamic_slice_in_dim(g_meta_full, me * d8, d8, axis=1)
    g_qn, g_kn = g_qn.reshape(fox_q_norm.shape), g_kn.reshape(fox_k_norm.shape)

    small = dict(meta_tokens=(g_meta, meta_tokens, m_meta_tokens, v_meta_tokens),
                 ffn1_norm=(g_ffn1n, ffn1_norm, m_ffn1_norm, v_ffn1_norm),
                 mix_norm=(g_mixn, mix_norm, m_mix_norm, v_mix_norm),
                 b_forget=(g_bf, b_forget, m_b_forget, v_b_forget),
                 fox_q_norm=(g_qn, fox_q_norm, m_fox_q_norm, v_fox_q_norm),
                 fox_k_norm=(g_kn, fox_k_norm, m_fox_k_norm, v_fox_k_norm),
                 ffn2_norm=(g_ffn2n, ffn2_norm, m_ffn2_norm, v_ffn2_norm))
    for k, (g, wt, mt, vt) in small.items():
        flat = lambda t: t.reshape(-1, t.shape[-1])
        dl, mn, vn = _adamw_plain(flat(g), flat(wt), flat(mt), flat(vt), name=f"adamw_{k}")
        grads[k], deltas[k], new_ms[k], new_vs[k] = g, dl.reshape(wt.shape), mn.reshape(wt.shape), vn.reshape(wt.shape)

    order = ["meta_tokens", "ffn1_norm", "ffn1_w_gate", "ffn1_w_up", "ffn1_w_down", "mix_norm", "w_in", "b_forget",
             "fox_q_norm", "fox_k_norm", "w_branch_fox", "w_branch_sb", "w_out", "ffn2_norm", "ffn2_w_gate",
             "ffn2_w_up", "ffn2_w_down"]
    return (loss, grad_x, *[grads[k] for k in order], *[deltas[k] for k in order],
            *[new_ms[k] for k in order], *[new_vs[k] for k in order])
```

```python
import jax
import jax.numpy as jnp
from jax import lax
from jax.experimental import pallas as pl
from jax.experimental.pallas import tpu as pltpu

F32 = jnp.float32
BF16 = jnp.bfloat16
MESH = pl.DeviceIdType.MESH

NDEV = 8
N_META = 16
HEAD_DIM = 128
HEADS = 8
BRANCH_WIDTH = HEADS * HEAD_DIM
RMS_EPS = 1e-6
FFN_RESIDUAL_WEIGHT = 0.5
ATTN_SCALE = HEAD_DIM ** -0.5
MASKED_LOGIT = -1e30

ADAM_LR = 0.001
ADAM_B1 = 0.9
ADAM_B2 = 0.999
ADAM_EPS = 1e-08
ADAM_WD = 0.01
ADAM_STEP = 10

LANES = 128
SMALL_ROWS = 8
ADAMW_TILE_ELEMS = 160 * 1024
KSUB = 2


def _pick(n, prefs):
    for p in prefs:
        if p <= n and n % p == 0:
            return p
    return n


def _dot(a, b, dims):
    return lax.dot_general(a, b, (dims, ((), ())), preferred_element_type=F32)


NN = ((1,), (0,))
TN = ((0,), (0,))
NT = ((1,), (1,))


def _split_bf16(x):
    hi = x.astype(BF16)
    lo = (x - hi.astype(F32)).astype(BF16)
    return hi, lo


def _sigmoid_parts(z):
    e = jnp.exp(-jnp.abs(z))
    t = 1.0 + e
    return e, 1.0 / t, jnp.log(t)


def _sigmoid(x):
    return 0.5 * jnp.tanh(0.5 * x) + 0.5


def _my_place():
    return lax.axis_index("x"), lax.axis_index("y"), lax.axis_index("c")


def _flat_id(px, py, pc):
    return 4 * px + 2 * py + pc


def _peer(x, y, c, k):
    px = 1 - x if k & 4 else x
    py = 1 - y if k & 2 else y
    pc = 1 - c if k & 1 else c
    return px, py, pc


def _allgather(shards, name, in_vmem=False):
    n = len(shards)

    def body(*refs):
        x_refs, out_refs = refs[:n], refs[n:2 * n]
        send_sems, recv_sems, local_sems = refs[2 * n:]
        x, y, c = _my_place()
        me, sibling = (x, y, c), (x, y, 1 - c)
        chips = [(1 - x, y), (x, 1 - y), (1 - x, 1 - y)]

        def block(a, place):
            return out_refs[a].at[_flat_id(*place)]

        def copy(a, k, place, to, src=None):
            return pltpu.make_async_remote_copy(
                src_ref=block(a, place) if src is None else src, dst_ref=block(a, place),
                send_sem=send_sems.at[7 * a + k], recv_sem=recv_sems.at[7 * a + k], device_id=to, device_id_type=MESH)

        mine = [pltpu.make_async_copy(x_refs[a], block(a, me), local_sems.at[a]) for a in range(n)]
        for cp in mine:
            cp.start()
        first = []
        for a in range(n):
            first.append(copy(a, 0, me, sibling, src=x_refs[a]))
            first += [copy(a, 1 + j, me, (*chip, c), src=x_refs[a]) for j, chip in enumerate(chips)]
        for cp in first:
            cp.start()
        passed = []
        for j, chip in enumerate(chips):
            for a in range(n):
                copy(a, 1 + j, (*chip, c), me).wait_recv()
                passed.append(copy(a, 4 + j, (*chip, c), sibling))
                passed[-1].start()
        for a in range(n):
            copy(a, 0, sibling, me).wait_recv()
        for j, chip in enumerate(chips):
            for a in range(n):
                copy(a, 4 + j, (*chip, 1 - c), me).wait_recv()
        for cp in first + passed:
            cp.wait_send()
        for cp in mine:
            cp.wait()

    space = pltpu.VMEM if in_vmem else pl.ANY
    return pl.pallas_call(
        body, name=name,
        out_shape=[jax.ShapeDtypeStruct((NDEV,) + s.shape, s.dtype) for s in shards],
        in_specs=[pl.BlockSpec(memory_space=space)] * n,
        out_specs=[pl.BlockSpec(memory_space=space)] * n,
        scratch_shapes=[pltpu.SemaphoreType.DMA((7 * n,)), pltpu.SemaphoreType.DMA((7 * n,)), pltpu.SemaphoreType.DMA((n,))],
    )(*shards)


def _allsum_small(part, name):
    rows, cols = part.shape

    def body(p_ref, out_ref, buf, send_sems, recv_sems):
        x, y, c = _my_place()
        me = _flat_id(x, y, c)
        buf[me] = p_ref[...]
        copies = []
        for k in range(1, NDEV):
            copies.append(pltpu.make_async_remote_copy(
                src_ref=p_ref, dst_ref=buf.at[me], send_sem=send_sems.at[k - 1], recv_sem=recv_sems.at[k - 1],
                device_id=_peer(x, y, c, k), device_id_type=MESH))
        for cp in copies:
            cp.start()
        for cp in copies:
            cp.wait()
        total = buf[0]
        for j in range(1, NDEV):
            total = total + buf[j]
        out_ref[...] = total

    return pl.pallas_call(
        body, name=name,
        out_shape=jax.ShapeDtypeStruct((rows, cols), F32),
        in_specs=[pl.BlockSpec(memory_space=pltpu.VMEM)],
        out_specs=pl.BlockSpec(memory_space=pltpu.VMEM),
        scratch_shapes=[pltpu.VMEM((NDEV, rows, cols), F32),
                        pltpu.SemaphoreType.DMA((7,)), pltpu.SemaphoreType.DMA((7,))],
    )(part)


_HBM = pl.BlockSpec(memory_space=pltpu.HBM)
_SEM = pl.BlockSpec(memory_space=pltpu.SEMAPHORE)
_DATAFLOW = pltpu.SideEffectType.DATAFLOW_SIDE_EFFECTING


COPIES_PER_ARRAY = {"gather": 7, "exchange": 7, "first": 4, "forward": 3}


def _send_copies(plan, src_refs, land_refs, send_sems, recv_sems):
    x, y, c = _my_place()
    me = _flat_id(x, y, c)
    per = COPIES_PER_ARRAY[plan]
    copies = []

    def add(a, slot, src, dst, to):
        copies.append(pltpu.make_async_remote_copy(
            src_ref=src, dst_ref=dst, send_sem=send_sems.at[per * a + slot], recv_sem=recv_sems.at[per * a + slot],
            device_id=to, device_id_type=MESH))

    for a, land_ref in enumerate(land_refs):
        if plan in ("gather", "exchange"):
            for k in range(1, NDEV):
                peer = _peer(x, y, c, k)
                add(a, k - 1, src_refs[a].at[_flat_id(*peer)] if plan == "exchange" else src_refs[a], land_ref.at[me], peer)
        elif plan == "first":
            for slot, k in enumerate((1, 2, 4, 6)):
                add(a, slot, src_refs[a], land_ref.at[me], _peer(x, y, c, k))
        else:
            for slot, k in enumerate((2, 4, 6)):
                block = land_ref.at[_flat_id(*_peer(x, y, c, k))]
                add(a, slot, block, block, (x, y, 1 - c))
    return copies


def _send_start(srcs, lands, plan, after, name):
    ns, nl = len(srcs), len(lands)
    nsem = COPIES_PER_ARRAY[plan] * nl

    def body(*refs):
        for cp in _send_copies(plan, refs[:ns], refs[ns:ns + nl], refs[ns + nl + 1], refs[ns + nl + 2]):
            cp.start()
        refs[-1][...] = jnp.zeros_like(refs[-1])

    operands = [pltpu.with_memory_space_constraint(t, pltpu.HBM) for t in list(srcs) + list(lands)]
    outs = pl.pallas_call(
        body, name=name,
        out_shape=(pltpu.SemaphoreType.DMA((nsem,)), pltpu.SemaphoreType.DMA((nsem,)),
                   *[pltpu.HBM(t.shape, t.dtype) for t in operands], jax.ShapeDtypeStruct((SMALL_ROWS, LANES), F32)),
        in_specs=[_HBM] * (ns + nl) + [pl.BlockSpec(memory_space=pl.ANY)],
        out_specs=(_SEM, _SEM, *[_HBM] * (ns + nl), pl.BlockSpec(memory_space=pltpu.VMEM)),
        input_output_aliases={i: 2 + i for i in range(ns + nl)},
        compiler_params=pltpu.CompilerParams(has_side_effects=_DATAFLOW),
    )(*operands, after)
    return (plan, outs[0], outs[1], list(outs[2:2 + ns]), list(outs[2 + ns:2 + ns + nl])), outs[-1]


def _send_wait(handle, after, name):
    plan, send_sems, recv_sems, srcs, lands = handle
    ns, nl = len(srcs), len(lands)

    def body(*refs):
        for cp in _send_copies(plan, refs[:ns], refs[ns:ns + nl], refs[ns + nl], refs[ns + nl + 1]):
            cp.wait_send()
            cp.wait_recv()

    outs = pl.pallas_call(
        body, name=name,
        out_shape=tuple(pltpu.HBM(t.shape, t.dtype) for t in srcs + lands),
        in_specs=[_HBM] * (ns + nl) + [_SEM, _SEM, pl.BlockSpec(memory_space=pl.ANY)],
        out_specs=tuple([_HBM] * (ns + nl)),
        input_output_aliases={i: i for i in range(ns + nl)},
        compiler_params=pltpu.CompilerParams(has_side_effects=_DATAFLOW),
    )(*srcs, *lands, send_sems, recv_sems, after)
    return list(outs[ns:])


def _gather_two_level(first_handle, after, name):
    lands = _send_wait(first_handle, after, f"{name}_wait")
    forward, _ = _send_start([], lands, "forward", after, f"{name}_forward_start")
    return _send_wait(forward, after, f"{name}_forward_wait")


def _landing(own_block, me):
    return lax.dynamic_update_index_in_dim(lax.empty((NDEV,) + own_block.shape, own_block.dtype), own_block, me, 0)


def _spec(block, index_map):
    return pl.BlockSpec(block, index_map)


def _mm(grid, a, a_spec, bs, extras, outs, *, dims, acc_shape=None, epi=None, after=None, name):
    nk = grid[2]
    nb, ne, no = len(bs), len(extras), len(outs)
    nafter = 0 if after is None else 1
    if epi is None:
        epi = lambda accs, ex: [accs[0]]

    def body(*refs):
        a_ref, b_refs = refs[0], refs[1:1 + nb]
        e_refs = refs[1 + nb:1 + nb + ne]
        o_refs = refs[1 + nb + ne + nafter:1 + nb + ne + nafter + no]
        acc_refs = refs[1 + nb + ne + nafter + no:]
        def finish(accs):
            for o_ref, tile in zip(o_refs, epi(accs, [e_ref[...] for e_ref in e_refs])):
                o_ref[...] = tile.astype(o_ref.dtype)

        def product(b_ref):
            if len(a_ref.shape) == 2:
                return _dot(a_ref[...], b_ref[...], dims)
            total = _dot(a_ref[0], b_ref[0], dims)
            for t in range(1, a_ref.shape[0]):
                total = total + _dot(a_ref[t], b_ref[t], dims)
            return total

        if nk == 1:
            finish([product(b_ref) for b_ref in b_refs])
        else:
            k = pl.program_id(2)

            @pl.when(k == 0)
            def _():
                for acc_ref in acc_refs:
                    acc_ref[...] = jnp.zeros_like(acc_ref)

            for acc_ref, b_ref in zip(acc_refs, b_refs):
                acc_ref[...] += product(b_ref)

            @pl.when(k == nk - 1)
            def _():
                finish([acc_ref[...] for acc_ref in acc_refs])

    return pl.pallas_call(
        body, name=name,
        grid=grid,
        in_specs=[a_spec] + [s for _, s in bs] + [s for _, s in extras] + [pl.BlockSpec(memory_space=pl.ANY)] * nafter,
        out_specs=[s for _, _, s in outs],
        out_shape=[jax.ShapeDtypeStruct(shape, dt) for shape, dt, _ in outs],
        scratch_shapes=[pltpu.VMEM(acc_shape, F32) for _ in bs] if nk > 1 else [],
        compiler_params=pltpu.CompilerParams(dimension_semantics=("parallel", "parallel", "arbitrary")),
    )(a, *[b for b, _ in bs], *[e for e, _ in extras], *([after] if nafter else []))


def _mm2d(a, b, n_cols, *, dims, tm, tn, tk, out_dtype, epi=None, extras=(), after=None, name):
    m_rows, k_len = (a.shape[1], a.shape[0]) if dims == TN else a.shape
    assert m_rows % tm == 0 and n_cols % tn == 0 and k_len % tk == 0, (name, a.shape, n_cols, tm, tn, tk)
    a_spec = _spec((tk, tm), lambda i, j, k: (k, i)) if dims == TN else _spec((tm, tk), lambda i, j, k: (i, k))
    b_spec = _spec((tn, tk), lambda i, j, k: (j, k)) if dims == NT else _spec((tk, tn), lambda i, j, k: (k, j))
    tile = _spec((tm, tn), lambda i, j, k: (i, j))
    return _mm((m_rows // tm, n_cols // tn, k_len // tk), a, a_spec, [(b, b_spec)], [(e, tile) for e in extras],
               [((m_rows, n_cols), out_dtype, tile)], dims=dims, acc_shape=(tm, tn), epi=epi, after=after, name=name)[0]


def _rowwise(fn, ins, consts, outs, sums, *, tm, name, row_maps=None, after=None):
    m_rows = outs[0][2] if len(outs[0]) == 3 else ins[0][0].shape[0]
    n = m_rows // tm
    ni, nc, no = len(ins), len(consts), len(outs)
    nafter = 0 if after is None else 1
    row_maps = row_maps or [None] * ni

    def body(*refs):
        i = pl.program_id(0)
        in_tiles = [r[...] for r in refs[:ni]]
        const_values = [r[...] for r in refs[ni:ni + nc]]
        o_refs = refs[ni + nc + nafter:ni + nc + nafter + no]
        s_refs = refs[ni + nc + nafter + no:]
        out_tiles, sum_terms = fn(i, in_tiles, const_values)
        for o_ref, tile in zip(o_refs, out_tiles):
            o_ref[...] = tile.astype(o_ref.dtype)
        if s_refs:
            @pl.when(i == 0)
            def _():
                for s_ref in s_refs:
                    s_ref[...] = jnp.zeros_like(s_ref)

            for s_ref, term in zip(s_refs, sum_terms):
                s_ref[...] += term

    def in_spec(width, col, rmap):
        if rmap is None:
            return pl.BlockSpec((tm, width), lambda i: (i, col))
        return pl.BlockSpec((tm, width), lambda i: (rmap(i), col))

    return pl.pallas_call(
        body, name=name,
        grid=(n,),
        in_specs=[in_spec(w, col, rmap) for (_, w, col), rmap in zip(ins, row_maps)]
        + [pl.BlockSpec(cst.shape, lambda i: (0, 0)) for cst in consts] + [pl.BlockSpec(memory_space=pl.ANY)] * nafter,
        out_specs=[pl.BlockSpec((tm, o[0]), lambda i: (i, 0)) for o in outs]
        + [pl.BlockSpec(s, lambda i: (0, 0)) for s in sums],
        out_shape=[jax.ShapeDtypeStruct((m_rows, o[0]), o[1]) for o in outs]
        + [jax.ShapeDtypeStruct(s, F32) for s in sums],
        compiler_params=pltpu.CompilerParams(dimension_semantics=("arbitrary",)),
    )(*[arr for arr, _, _ in ins], *consts, *([after] if nafter else []))


def _whole(arr):
    return (arr, arr.shape[1], 0)


def _rms(x, gain):
    r = lax.rsqrt(jnp.mean(x * x, axis=-1, keepdims=True) + RMS_EPS)
    return x * r * gain


def _rms_bwd(x, gain, dy):
    r = lax.rsqrt(jnp.mean(x * x, axis=-1, keepdims=True) + RMS_EPS)
    u = dy * gain
    dx = r * u - x * (r * r * r) * jnp.mean(x * u, axis=-1, keepdims=True)
    return dx, dy * x * r


def _norm_fwd(h, gain, *, tm, name):
    def fn(i, tiles, consts):
        return [_rms(tiles[0], consts[0])], []
    return _rowwise(fn, [_whole(h)], [gain], [(h.shape[1], BF16)], [], tm=tm, name=name)[0]


def _norm_bwd(dn, h, gain, dh_in, *, tm, name, after=None):
    d = h.shape[1]

    def fn(i, tiles, consts):
        dx, dg_rows = _rms_bwd(tiles[1], consts[0], tiles[0])
        dh = tiles[2] + dx
        return [dh, dh, FFN_RESIDUAL_WEIGHT * dh], [jnp.sum(dg_rows, axis=0, keepdims=True)]

    return _rowwise(fn, [_whole(dn), _whole(h), _whole(dh_in)], [gain],
                    [(d, F32), (d, BF16), (d, BF16)], [(1, d)], tm=tm, name=name, after=after)


def _swiglu_epi(accs, ex):
    a, b = accs
    return [a, b, a * _sigmoid(a) * b]


def _swiglu_bwd_epi(accs, ex):
    ds = accs[0]
    a, b = ex[0].astype(F32), ex[1].astype(F32)
    sig = _sigmoid(a)
    silu = a * sig
    dsilu = sig * (1.0 + a * (1.0 - sig))
    return [jnp.stack([(ds * b * dsilu).astype(BF16), (ds * silu).astype(BF16)], axis=0)]


def _attn_mask(i, j, tile, pad, strict):
    row = i * tile + lax.broadcasted_iota(jnp.int32, (tile, tile), 0)
    col = j * tile + lax.broadcasted_iota(jnp.int32, (tile, tile), 1)
    causal = (col < row) if strict else (col <= row)
    return causal & ((col >= pad) | (row < pad))


FWD_GROUP = 4
BWD_GROUP = 2


def _walk_key_tiles(i, step, carry, descending=False):
    if descending:
        carry = lax.fori_loop(jnp.maximum(i, 1), i + 1, lambda j, c: step(j, c, True), carry)
        carry = lax.fori_loop(0, jnp.maximum(i - 1, 0), lambda t, c: step(i - 1 - t, c, False), carry)
        return step(0, carry, True)
    carry = step(0, carry, True)
    carry = lax.fori_loop(1, i, lambda j, c: step(j, c, False), carry)
    return lax.fori_loop(jnp.maximum(i, 1), i + 1, lambda j, c: step(j, c, True), carry)


def _head_cols(g):
    return pl.ds(g * HEAD_DIM, HEAD_DIM)


def _key_tile(ref, j, tile, g):
    return ref[pl.ds(pl.multiple_of(j * tile, tile), tile), _head_cols(g)]


def _head_specs(lp, tile, q_off, k_off, v_off, hp):
    gw = hp * HEAD_DIM
    q_spec = pl.BlockSpec((tile, gw), lambda h, i: (i, h + q_off // hp))
    k_spec = pl.BlockSpec((lp, gw), lambda h, i: (0, h + k_off // hp))
    v_spec = pl.BlockSpec((lp, gw), lambda h, i: (0, h + v_off // hp))
    return q_spec, k_spec, v_spec


def _fox_fwd(q, k, v, v_off, c_rows, *, tile, pad, name):
    lp = q.shape[0]
    nb = lp // tile
    hp = FWD_GROUP
    gw = hp * HEAD_DIM

    def body(q_ref, k_ref, v_ref, c_ref, o_ref, lse_ref):
        i = pl.program_id(1)
        qts = [q_ref[:, _head_cols(g)] for g in range(hp)]

        def step(j, carry, masked):
            ok = _attn_mask(i, j, tile, pad, False) if masked else None
            out = []
            for g, (m, l, acc) in enumerate(carry):
                s = _dot(qts[g], _key_tile(k_ref, j, tile, g), NT) * ATTN_SCALE - c_ref[g, j]
                if masked:
                    s = jnp.where(ok, s, MASKED_LOGIT)
                m_new = jnp.maximum(m, jnp.max(s, axis=1, keepdims=True))
                p = jnp.exp(s - m_new)
                alpha = jnp.exp(m - m_new)
                l = alpha * l + jnp.sum(p, axis=1, keepdims=True)
                acc = alpha * acc + _dot(p.astype(BF16), _key_tile(v_ref, j, tile, g), NN)
                out.append((m_new, l, acc))
            return tuple(out)

        init = (jnp.full((tile, 1), MASKED_LOGIT, F32), jnp.zeros((tile, 1), F32), jnp.zeros((tile, HEAD_DIM), F32))
        final = _walk_key_tiles(i, step, (init,) * hp)
        for g, (m, l, acc) in enumerate(final):
            o_ref[:, _head_cols(g)] = (acc / l).astype(o_ref.dtype)
            lse_ref[g] = jnp.broadcast_to(m + jnp.log(l), (tile, LANES))

    q_spec, k_spec, v_spec = _head_specs(lp, tile, 0, 0, v_off, hp)
    return pl.pallas_call(
        body, name=name,
        grid=(HEADS // hp, nb),
        in_specs=[q_spec, k_spec, v_spec, pl.BlockSpec((hp, nb, 1, tile), lambda h, i: (h, 0, 0, 0))],
        out_specs=[pl.BlockSpec((tile, gw), lambda h, i: (i, h)),
                   pl.BlockSpec((hp, tile, LANES), lambda h, i: (h, i, 0))],
        out_shape=[jax.ShapeDtypeStruct((lp, BRANCH_WIDTH), BF16), jax.ShapeDtypeStruct((HEADS, lp, LANES), F32)],
        compiler_params=pltpu.CompilerParams(dimension_semantics=("parallel", "arbitrary")),
    )(q, k, v, c_rows)


def _fox_bwd(q, k, v, v_off, c_rows, o, do, lse, *, tile, pad, name):
    lp = q.shape[0]
    nb = lp // tile
    hp = BWD_GROUP
    gw = hp * HEAD_DIM

    def body(q_ref, k_ref, v_ref, c_ref, o_ref, do_ref, lse_ref, dq_ref, dk_ref, dv_ref, dc_ref, dk_acc, dv_acc, dc_acc):
        i = pl.program_id(1)

        @pl.when(i == 0)
        def _():
            dk_acc[...] = jnp.zeros_like(dk_acc)
            dv_acc[...] = jnp.zeros_like(dv_acc)
            dc_acc[...] = jnp.zeros_like(dc_acc)

        heads = range(hp)
        qts = [q_ref[:, _head_cols(g)] for g in heads]
        dots = [do_ref[:, _head_cols(g)] for g in heads]
        deltas = [jnp.sum(dots[g].astype(F32) * o_ref[:, _head_cols(g)].astype(F32), axis=1, keepdims=True) for g in heads]
        lse_cols = [lse_ref[g][:, :1] for g in heads]

        def step(j, dqs, masked):
            rows = pl.ds(pl.multiple_of(j * tile, tile), tile)
            ok = _attn_mask(i, j, tile, pad, False) if masked else None
            out = []
            for g in heads:
                kt = _key_tile(k_ref, j, tile, g)
                s = _dot(qts[g], kt, NT) * ATTN_SCALE - c_ref[g, j]
                p = jnp.exp(s - lse_cols[g])
                if masked:
                    p = jnp.where(ok, p, 0.0)
                dp = _dot(dots[g], _key_tile(v_ref, j, tile, g), NT)
                ds = p * (dp - deltas[g])
                dsb = ds.astype(BF16)
                dk_acc[rows, _head_cols(g)] += _dot(dsb, qts[g], TN)
                dv_acc[rows, _head_cols(g)] += _dot(p.astype(BF16), dots[g], TN)
                dc_acc[g, j] += -jnp.sum(ds, axis=0, keepdims=True)
                out.append(dqs[g] + _dot(dsb, kt, NN))
            return tuple(out)

        dqs = _walk_key_tiles(i, step, (jnp.zeros((tile, HEAD_DIM), F32),) * hp)
        for g in heads:
            dq_ref[:, _head_cols(g)] = dqs[g] * ATTN_SCALE

        @pl.when(i == nb - 1)
        def _():
            dk_ref[...] = dk_acc[...] * ATTN_SCALE
            dv_ref[...] = dv_acc[...].astype(dv_ref.dtype)
            dc_ref[...] = dc_acc[...]

    q_spec, k_spec, v_spec = _head_specs(lp, tile, 0, 0, v_off, hp)
    tile_spec = pl.BlockSpec((tile, gw), lambda h, i: (i, h))
    head_spec = pl.BlockSpec((lp, gw), lambda h, i: (0, h))
    c_spec = pl.BlockSpec((hp, nb, 1, tile), lambda h, i: (h, 0, 0, 0))
    return pl.pallas_call(
        body, name=name,
        grid=(HEADS // hp, nb),
        in_specs=[q_spec, k_spec, v_spec, c_spec, tile_spec, tile_spec,
                  pl.BlockSpec((hp, tile, LANES), lambda h, i: (h, i, 0))],
        out_specs=[tile_spec, head_spec, head_spec, c_spec],
        out_shape=[jax.ShapeDtypeStruct((lp, BRANCH_WIDTH), F32), jax.ShapeDtypeStruct((lp, BRANCH_WIDTH), F32),
                   jax.ShapeDtypeStruct((lp, BRANCH_WIDTH), BF16), jax.ShapeDtypeStruct((HEADS, nb, 1, tile), F32)],
        scratch_shapes=[pltpu.VMEM((lp, gw), F32), pltpu.VMEM((lp, gw), F32),
                        pltpu.VMEM((hp, nb, 1, tile), F32)],
        compiler_params=pltpu.CompilerParams(dimension_semantics=("parallel", "arbitrary")),
    )(q, k, v, c_rows, o, do, lse)


def _later_matrix(tile):
    return (lax.broadcasted_iota(jnp.int32, (tile, tile), 0) > lax.broadcasted_iota(jnp.int32, (tile, tile), 1)).astype(BF16)


def _earlier_matrix(tile):
    return (lax.broadcasted_iota(jnp.int32, (tile, tile), 0) < lax.broadcasted_iota(jnp.int32, (tile, tile), 1)).astype(BF16)


def _sb_tile(qt, kt, ok, later):
    z = _dot(qt, kt, NT) * ATTN_SCALE
    e, r, lg = _sigmoid_parts(z)
    sp = jnp.maximum(z, 0.0) + lg
    spm = sp if ok is None else jnp.where(ok, sp, 0.0)
    hi, lo = _split_bf16(spm)
    within = _dot(hi, later, NN) + _dot(lo, later, NN)
    return z, e, r, sp, spm, within


def _sb_fwd(qkv, q_off, k_off, v_off, *, tile, pad, name):
    lp = qkv.shape[0]
    nb = lp // tile
    hp = FWD_GROUP
    gw = hp * HEAD_DIM

    def body(q_ref, k_ref, v_ref, o_ref, tot_ref):
        i = pl.program_id(1)
        qts = [q_ref[:, _head_cols(g)] for g in range(hp)]
        later = _later_matrix(tile)

        def step(j, carry, masked):
            ok = _attn_mask(i, j, tile, pad, True) if masked else None
            out = []
            for g, (right, acc) in enumerate(carry):
                z, _, _, sp, spm, within = _sb_tile(qts[g], _key_tile(k_ref, j, tile, g), ok, later)
                w = jnp.exp(z - sp - within - right)
                if masked:
                    w = jnp.where(ok, w, 0.0)
                acc = acc + _dot(w.astype(BF16), _key_tile(v_ref, j, tile, g), NN)
                out.append((right + jnp.sum(spm, axis=1, keepdims=True), acc))
            return tuple(out)

        init = (jnp.zeros((tile, 1), F32), jnp.zeros((tile, HEAD_DIM), F32))
        final = _walk_key_tiles(i, step, (init,) * hp, descending=True)
        for g, (total, acc) in enumerate(final):
            o_ref[:, _head_cols(g)] = acc.astype(o_ref.dtype)
            tot_ref[g] = jnp.broadcast_to(total, (tile, LANES))

    q_spec, k_spec, v_spec = _head_specs(lp, tile, q_off, k_off, v_off, hp)
    return pl.pallas_call(
        body, name=name,
        grid=(HEADS // hp, nb),
        in_specs=[q_spec, k_spec, v_spec],
        out_specs=[pl.BlockSpec((tile, gw), lambda h, i: (i, h)),
                   pl.BlockSpec((hp, tile, LANES), lambda h, i: (h, i, 0))],
        out_shape=[jax.ShapeDtypeStruct((lp, BRANCH_WIDTH), BF16), jax.ShapeDtypeStruct((HEADS, lp, LANES), F32)],
        compiler_params=pltpu.CompilerParams(dimension_semantics=("parallel", "arbitrary")),
    )(qkv, qkv, qkv)


def _sb_bwd(qkv, q_off, k_off, v_off, do, total, *, tile, pad, name):
    lp = qkv.shape[0]
    nb = lp // tile
    hp = BWD_GROUP
    gw = hp * HEAD_DIM

    def body(q_ref, k_ref, v_ref, do_ref, tot_ref, dq_ref, dk_ref, dv_ref, dk_acc, dv_acc):
        i = pl.program_id(1)

        @pl.when(i == 0)
        def _():
            dk_acc[...] = jnp.zeros_like(dk_acc)
            dv_acc[...] = jnp.zeros_like(dv_acc)

        heads = range(hp)
        qts = [q_ref[:, _head_cols(g)] for g in heads]
        dots = [do_ref[:, _head_cols(g)] for g in heads]
        total_cols = [tot_ref[g][:, :1] for g in heads]
        later, earlier = _later_matrix(tile), _earlier_matrix(tile)

        def step(j, carry, masked):
            rows = pl.ds(pl.multiple_of(j * tile, tile), tile)
            ok = _attn_mask(i, j, tile, pad, True) if masked else None
            out = []
            for g, (dq, sp_before, dlw_before) in enumerate(carry):
                kt = _key_tile(k_ref, j, tile, g)
                z, e, r, sp, spm, within = _sb_tile(qts[g], kt, ok, later)
                sp_here = jnp.sum(spm, axis=1, keepdims=True)
                right = total_cols[g] - sp_before - sp_here
                w = jnp.exp(z - sp - within - right)
                if masked:
                    w = jnp.where(ok, w, 0.0)
                dlw = w * _dot(dots[g], _key_tile(v_ref, j, tile, g), NT)
                hi, lo = _split_bf16(dlw)
                before = dlw_before + _dot(hi, earlier, NN) + _dot(lo, earlier, NN)
                sig = jnp.where(z >= 0, r, e * r)
                dz = dlw * (1.0 - sig) - sig * before
                if masked:
                    dz = jnp.where(ok, dz, 0.0)
                dzb = dz.astype(BF16)
                dk_acc[rows, _head_cols(g)] += _dot(dzb, qts[g], TN)
                dv_acc[rows, _head_cols(g)] += _dot(w.astype(BF16), dots[g], TN)
                out.append((dq + _dot(dzb, kt, NN), sp_before + sp_here, dlw_before + jnp.sum(dlw, axis=1, keepdims=True)))
            return tuple(out)

        zero_col = jnp.zeros((tile, 1), F32)
        final = _walk_key_tiles(i, step, ((jnp.zeros((tile, HEAD_DIM), F32), zero_col, zero_col),) * hp)
        for g in heads:
            dq_ref[:, _head_cols(g)] = (final[g][0] * ATTN_SCALE).astype(dq_ref.dtype)

        @pl.when(i == nb - 1)
        def _():
            dk_ref[...] = (dk_acc[...] * ATTN_SCALE).astype(dk_ref.dtype)
            dv_ref[...] = dv_acc[...].astype(dv_ref.dtype)

    q_spec, k_spec, v_spec = _head_specs(lp, tile, q_off, k_off, v_off, hp)
    tile_spec = pl.BlockSpec((tile, gw), lambda h, i: (i, h))
    head_spec = pl.BlockSpec((lp, gw), lambda h, i: (0, h))
    return pl.pallas_call(
        body, name=name,
        grid=(HEADS // hp, nb),
        in_specs=[q_spec, k_spec, v_spec, tile_spec, pl.BlockSpec((hp, tile, LANES), lambda h, i: (h, i, 0))],
        out_specs=[tile_spec, head_spec, head_spec],
        out_shape=[jax.ShapeDtypeStruct((lp, BRANCH_WIDTH), BF16)] * 3,
        scratch_shapes=[pltpu.VMEM((lp, gw), F32), pltpu.VMEM((lp, gw), F32)],
        compiler_params=pltpu.CompilerParams(dimension_semantics=("parallel", "arbitrary")),
    )(qkv, qkv, qkv, do, total)


def _cumsum_rows(x, *, tile, reverse, name):
    lp = x.shape[0]
    nb = lp // tile

    def body(x_ref, o_ref, carry):
        @pl.when(pl.program_id(0) == 0)
        def _():
            carry[...] = jnp.zeros_like(carry)

        r = lax.broadcasted_iota(jnp.int32, (tile, tile), 0)
        c = lax.broadcasted_iota(jnp.int32, (tile, tile), 1)
        tri = ((c >= r) if reverse else (c <= r)).astype(BF16)
        hi, lo = _split_bf16(x_ref[...])
        run = _dot(tri, hi, NN) + _dot(tri, lo, NN) + carry[...]
        o_ref[...] = run
        carry[...] = run[:1, :] if reverse else run[tile - 1:, :]

    order = (lambda i: (nb - 1 - i, 0)) if reverse else (lambda i: (i, 0))
    return pl.pallas_call(
        body, name=name,
        grid=(nb,),
        in_specs=[pl.BlockSpec((tile, LANES), order)],
        out_specs=pl.BlockSpec((tile, LANES), order),
        out_shape=jax.ShapeDtypeStruct((lp, LANES), F32),
        scratch_shapes=[pltpu.VMEM((1, LANES), F32)],
        compiler_params=pltpu.CompilerParams(dimension_semantics=("arbitrary",)),
    )(x)


def _log_sigmoid(x):
    return jnp.minimum(x, 0.0) - jnp.log(1.0 + jnp.exp(-jnp.abs(x)))


def _forget_mask(i, tm, pad):
    row = i * tm + lax.broadcasted_iota(jnp.int32, (tm, LANES), 0)
    lane = lax.broadcasted_iota(jnp.int32, (tm, LANES), 1)
    return (row >= pad) & (lane < HEADS)


def _fox_prep(proj_a, q_gain, k_gain, b_forget, *, tm, pad, name):
    w = BRANCH_WIDTH

    def fn(i, tiles, consts):
        pa = tiles[0]
        qs, ks = [], []
        for h in range(HEADS):
            lo = h * HEAD_DIM
            qs.append(_rms(pa[:, lo:lo + HEAD_DIM], consts[0][:, lo:lo + HEAD_DIM]))
            ks.append(_rms(pa[:, w + lo:w + lo + HEAD_DIM], consts[1][:, lo:lo + HEAD_DIM]))
        logf = jnp.where(_forget_mask(i, tm, pad), _log_sigmoid(pa[:, 2 * w:] + consts[2]), 0.0)
        return [jnp.concatenate(qs, axis=1), jnp.concatenate(ks, axis=1), logf], []

    return _rowwise(fn, [_whole(proj_a)], [q_gain, k_gain, b_forget],
                    [(w, BF16), (w, BF16), (LANES, F32)], [], tm=tm, name=name)


def _fox_prep_bwd(proj_a, dq, dk, dlogf, q_gain, k_gain, b_forget, *, tm, pad, name):
    w = BRANCH_WIDTH

    def fn(i, tiles, consts):
        pa, dqt, dkt, dlf = tiles
        dxs_q, dxs_k, dgs_q, dgs_k = [], [], [], []
        for h in range(HEADS):
            lo = h * HEAD_DIM
            dx, dg = _rms_bwd(pa[:, lo:lo + HEAD_DIM], consts[0][:, lo:lo + HEAD_DIM], dqt[:, lo:lo + HEAD_DIM])
            dxs_q.append(dx)
            dgs_q.append(jnp.sum(dg, axis=0, keepdims=True))
            dx, dg = _rms_bwd(pa[:, w + lo:w + lo + HEAD_DIM], consts[1][:, lo:lo + HEAD_DIM], dkt[:, lo:lo + HEAD_DIM])
            dxs_k.append(dx)
            dgs_k.append(jnp.sum(dg, axis=0, keepdims=True))
        xf = pa[:, 2 * w:] + consts[2]
        e, r, _ = _sigmoid_parts(xf)
        df = jnp.where(_forget_mask(i, tm, pad), dlf * jnp.where(xf >= 0, e * r, r), 0.0)
        return ([jnp.concatenate(dxs_q + dxs_k + [df], axis=1)],
                [jnp.concatenate(dgs_q, axis=1), jnp.concatenate(dgs_k, axis=1), jnp.sum(df, axis=0, keepdims=True)])

    return _rowwise(fn, [_whole(proj_a), _whole(dq), _whole(dk), _whole(dlogf)], [q_gain, k_gain, b_forget],
                    [(2 * w + LANES, BF16)], [(1, w), (1, w), (1, LANES)], tm=tm, name=name)


def _adamw_math(w, g, m, v):
    m = ADAM_B1 * m + (1.0 - ADAM_B1) * g
    v = ADAM_B2 * v + (1.0 - ADAM_B2) * (g * g)
    m_hat = m / (1.0 - ADAM_B1 ** ADAM_STEP)
    v_hat = v / (1.0 - ADAM_B2 ** ADAM_STEP)
    delta = -ADAM_LR * (m_hat / (jnp.sqrt(v_hat) + ADAM_EPS) + ADAM_WD * w)
    return delta, m, v


def _adamw_summed(parts, sel, w, m, v, *, name):
    rows, cols = w.shape
    tr = _pick(rows, [t for t in (512, 256, 128, 64, 32, 16, 8) if t * cols <= ADAMW_TILE_ELEMS])

    def body(p_ref, w_ref, m_ref, v_ref, g_out, d_out, m_out, v_out):
        g = p_ref[0].astype(F32)
        for j in range(1, NDEV):
            g = g + p_ref[j].astype(F32)
        delta, m_new, v_new = _adamw_math(w_ref[...], g, m_ref[...], v_ref[...])
        g_out[...] = g
        d_out[...] = delta
        m_out[...] = m_new
        v_out[...] = v_new

    spec = pl.BlockSpec((tr, cols), lambda i: (i, 0))
    if parts.ndim == 3:
        p_spec = pl.BlockSpec((NDEV, tr, cols), lambda i: (0, i, 0))
    else:
        p_spec = pl.BlockSpec((NDEV, None, tr, cols), lambda i: (0, sel, i, 0))
    return pl.pallas_call(
        body, name=name,
        grid=(rows // tr,),
        in_specs=[p_spec, spec, spec, spec],
        out_specs=[spec] * 4,
        out_shape=[jax.ShapeDtypeStruct((rows, cols), F32)] * 4,
        compiler_params=pltpu.CompilerParams(dimension_semantics=("parallel",)),
    )(parts, w, m, v)


def _adamw_plain(g, w, m, v, *, name):
    def body(g_ref, w_ref, m_ref, v_ref, d_out, m_out, v_out):
        delta, m_new, v_new = _adamw_math(w_ref[...], g_ref[...], m_ref[...], v_ref[...])
        d_out[...] = delta
        m_out[...] = m_new
        v_out[...] = v_new

    return pl.pallas_call(body, name=name, out_shape=[jax.ShapeDtypeStruct(w.shape, F32)] * 3)(g, w, m, v)


def _pack_rows(arrays, width, row_align):
    pieces, spans, at = [], [], 0
    for arr in arrays:
        flat = arr.reshape(-1)
        rows = -(-flat.shape[0] // (width * row_align)) * row_align
        flat = jnp.pad(flat, (0, rows * width - flat.shape[0]))
        pieces.append(flat.reshape(rows, width))
        spans.append((at, rows))
        at += rows
    return jnp.concatenate(pieces, axis=0), spans


def _unpack(rows2d, span, shape):
    at, rows = span
    size = 1
    for s in shape:
        size *= s
    return rows2d[at:at + rows].reshape(-1)[:size].reshape(shape)


def _join_cols(blocks):
    n, rows, cols = blocks.shape
    return jnp.transpose(blocks, (1, 0, 2)).reshape(rows, n * cols)


def _split_cols(full):
    rows, cols = full.shape
    return jnp.transpose(full.reshape(rows, NDEV, cols // NDEV), (1, 0, 2))


def _ffn_up(h, gain, w_gu, after, *, tm, tag):
    lp, d = h.shape
    f8 = w_gu.shape[3]
    n = _norm_fwd(h, gain, tm=_pick(lp, [256, 128]), name=f"{tag}_norm")
    hid = _spec((None, tm, f8), lambda i, j, k: (j, i, 0))
    a, b, s = _mm((lp // tm, NDEV, 1), n, _spec((tm, d), lambda i, j, k: (i, 0)),
                  [(w_gu, _spec((None, None, d, f8), lambda i, j, k: (j, 0, 0, 0))),
                   (w_gu, _spec((None, None, d, f8), lambda i, j, k: (j, 1, 0, 0)))],
                  [], [((NDEV, lp, f8), BF16, hid)] * 3, dims=NN, epi=_swiglu_epi, after=after, name=f"{tag}_up")
    return n, a, b, s


def _ffn_down(h, s, w_down, *, tm, tag):
    lp, d = h.shape
    f8 = w_down.shape[1]
    tn = _pick(d, [1024, 512, 256, 128])
    tile = _spec((tm, tn), lambda i, j, k: (i, j))
    return _mm((lp // tm, d // tn, NDEV // KSUB), s, _spec((KSUB, tm, f8), lambda i, j, k: (k, i, 0)),
               [(w_down, _spec((KSUB, f8, tn), lambda i, j, k: (k, 0, j)))], [(h, tile)], [((lp, d), F32, tile)],
               dims=NN, acc_shape=(tm, tn), epi=lambda accs, ex: [ex[0] + FFN_RESIDUAL_WEIGHT * accs[0]], name=f"{tag}_down")[0]


def _ffn_bwd_dw(dh_half, saved, w_down, on_down, *, tm, tag):
    n, a, b, s = saved
    lp, d = dh_half.shape
    f8 = w_down.shape[1]
    tkr = _pick(lp, [2176, 1088, 544, 256, 128])
    tn = _pick(d, [1024, 512, 256, 128])
    hid = _spec((None, tm, f8), lambda i, j, k: (j, i, 0))
    dab, = _mm((lp // tm, NDEV, 1), dh_half, _spec((tm, d), lambda i, j, k: (i, 0)),
               [(w_down, _spec((None, f8, d), lambda i, j, k: (j, 0, 0)))], [(a, hid), (b, hid)],
               [((2, NDEV, lp, f8), BF16, _spec((2, None, tm, f8), lambda i, j, k: (0, j, i, 0)))],
               dims=NT, epi=_swiglu_bwd_epi, name=f"{tag}_down_dx")
    dw_down, = _mm((NDEV, d // tn, lp // tkr), s, _spec((None, tkr, f8), lambda i, j, k: (i, k, 0)),
                   [(dh_half, _spec((tkr, tn), lambda i, j, k: (k, j)))], [],
                   [((NDEV, f8, d), BF16, _spec((None, f8, tn), lambda i, j, k: (i, 0, j)))],
                   dims=TN, acc_shape=(f8, tn), name=f"{tag}_down_dw")
    dw_gu, = _mm((d // tn, 2 * NDEV, lp // tkr), n, _spec((tkr, tn), lambda i, j, k: (k, i)),
                 [(dab, _spec((None, None, tkr, f8), lambda i, j, k: (j // NDEV, j % NDEV, k, 0)))], [],
                 [((NDEV, 2, d, f8), BF16, _spec((None, None, tn, f8), lambda i, j, k: (j % NDEV, j // NDEV, i, 0)))],
                 dims=TN, acc_shape=(tn, f8), after=on_down(dw_down), name=f"{tag}_gate_up_dw")
    return dab, dw_gu, dw_down


def _ffn_bwd_dx(dab, w_gu, after, *, tm, tag):
    lp, f8 = dab.shape[2], dab.shape[3]
    d = w_gu.shape[2]
    tn = _pick(d, [1024, 512, 256, 128])
    nsub = NDEV // KSUB
    return _mm((lp // tm, d // tn, 2 * nsub), dab, _spec((None, KSUB, tm, f8), lambda i, j, k: (k // nsub, k % nsub, i, 0)),
               [(w_gu, _spec((KSUB, None, tn, f8), lambda i, j, k: (k % nsub, k // nsub, j, 0)))], [],
               [((lp, d), F32, _spec((tm, tn), lambda i, j, k: (i, j)))],
               dims=NT, acc_shape=(tm, tn), after=after, name=f"{tag}_gate_up_dx")[0]


def kernel(x, meta_tokens, ffn1_norm, ffn1_w_gate, ffn1_w_up, ffn1_w_down, mix_norm, w_in, b_forget, fox_q_norm, fox_k_norm, w_branch_fox, w_branch_sb, w_out, ffn2_norm, ffn2_w_gate, ffn2_w_up, ffn2_w_down, loss_target, m_meta_tokens, m_ffn1_norm, m_ffn1_w_gate, m_ffn1_w_up, m_ffn1_w_down, m_mix_norm, m_w_in, m_b_forget, m_fox_q_norm, m_fox_k_norm, m_w_branch_fox, m_w_branch_sb, m_w_out, m_ffn2_norm, m_ffn2_w_gate, m_ffn2_w_up, m_ffn2_w_down, v_meta_tokens, v_ffn1_norm, v_ffn1_w_gate, v_ffn1_w_up, v_ffn1_w_down, v_mix_norm, v_w_in, v_b_forget, v_fox_q_norm, v_fox_k_norm, v_w_branch_fox, v_w_branch_sb, v_w_out, v_ffn2_norm, v_ffn2_w_gate, v_ffn2_w_up, v_ffn2_w_down):
    seq, d = x.shape[1], x.shape[2]
    d8 = d // NDEV
    w = BRANCH_WIDTH
    tile = 256 if seq % 256 == 0 else 128
    pad = tile - N_META
    lp = tile + seq
    tm = _pick(lp, [1088, 544, 256, 128])
    tr = _pick(lp, [256, 128])
    tkr = _pick(lp, [2176, 1088, 544, 256, 128])
    nb = lp // tile
    me = _flat_id(*_my_place())

    shards = [jnp.stack([ffn1_w_gate[0], ffn1_w_up[0]]).astype(BF16), ffn1_w_down[0].astype(BF16), w_in[0].astype(BF16),
              w_branch_fox[0].astype(BF16), w_branch_sb[0].astype(BF16), w_out[0].astype(BF16),
              jnp.stack([ffn2_w_gate[0], ffn2_w_up[0]]).astype(BF16), ffn2_w_down[0].astype(BF16)]
    w_gu1, = _allgather(shards[:1], "gather_ffn1")
    down1_copies, token = _send_start(shards[1:2], [_landing(s, me) for s in shards[1:2]], "first", w_gu1, "gather_down1_start")
    w_in_copies, token = _send_start(shards[2:3], [_landing(s, me) for s in shards[2:3]], "first", token, "gather_w_in_start")
    mixer_copies, token = _send_start(shards[3:6], [_landing(s, me) for s in shards[3:6]], "gather", token, "gather_mixer_start")
    ffn2_copies, token = _send_start(shards[6:], [_landing(s, me) for s in shards[6:]], "gather", token, "gather_ffn2_start")
    meta_full = _join_cols(_allgather([meta_tokens], "gather_meta", in_vmem=True)[0])

    h0 = jnp.concatenate([jnp.zeros((pad, d), F32), meta_full.astype(F32), x[0]], axis=0)
    saved1 = _ffn_up(h0, ffn1_norm, w_gu1, token, tm=tm, tag="ffn1")
    w_down1, = _gather_two_level(down1_copies, saved1[3], "gather_down1")
    h1 = _ffn_down(h0, saved1[3], w_down1, tm=tm, tag="ffn1")

    w_in_blocks, = _gather_two_level(w_in_copies, h1, "gather_w_in")
    wi = _join_cols(w_in_blocks)
    w_pa = jnp.concatenate([wi[:, :2 * w], jnp.pad(wi[:, 3 * w:3 * w + HEADS], ((0, 0), (0, LANES - HEADS)))], axis=1)
    w_pb = jnp.concatenate([wi[:, 2 * w:3 * w], wi[:, 3 * w + HEADS:]], axis=1)
    na, nbw = w_pa.shape[1], w_pb.shape[1]
    gate_blk = 4 * w // d

    n2 = _norm_fwd(h1, mix_norm, tm=tr, name="mix_norm")
    tma = _pick(lp, [544, 256, 128])
    tnd = _pick(d, [1024, 512, 256, 128])
    tnb = _pick(nbw, [512, 256, 128])
    proj_a = _mm2d(n2, w_pa, na, dims=NN, tm=tma, tn=na, tk=d, out_dtype=F32, name="proj_a")
    proj_b = _mm2d(n2, w_pb, nbw, dims=NN, tm=tm, tn=tnb, tk=d, out_dtype=BF16, name="proj_b")
    b_pad = jnp.pad(b_forget, ((0, 0), (0, LANES - HEADS)))
    q_gain, k_gain = fox_q_norm.reshape(1, w), fox_k_norm.reshape(1, w)
    fq, fk, logf = _fox_prep(proj_a, q_gain, k_gain, b_pad, tm=tr, pad=pad, name="fox_prep")
    c = _cumsum_rows(logf, tile=tile, reverse=False, name="forget_cumsum")
    c_rows = jnp.transpose(c[:, :HEADS]).reshape(HEADS, nb, 1, tile)
    o_fox, lse = _fox_fwd(fq, fk, proj_b, 0, c_rows, tile=tile, pad=pad, name="fox_fwd")
    o_sb, sb_total = _sb_fwd(proj_b, HEADS, 2 * HEADS, 3 * HEADS, tile=tile, pad=pad, name="sb_fwd")
    w_br_fox, w_br_sb, w_out_blocks = _send_wait(mixer_copies, o_sb, "gather_mixer_wait")
    w_out_full = w_out_blocks.reshape(d, d)

    def branch(o, w_blocks, name):
        return _mm((lp // tm, NDEV, 1), o, _spec((tm, w), lambda i, j, k: (i, 0)),
                   [(w_blocks, _spec((None, w, d8), lambda i, j, k: (j, 0, 0)))], [],
                   [((lp, d), BF16, _spec((tm, d8), lambda i, j, k: (i, j)))], dims=NN, name=name)[0]

    br_fox = branch(o_fox, w_br_fox, "branch_fox")
    br_sb = branch(o_sb, w_br_sb, "branch_sb")

    def merge_fn(i, tiles, consts):
        bf_, bs_, gf_, gs_ = [t.astype(F32) for t in tiles]
        return [_sigmoid(gf_) * bf_ + _sigmoid(gs_) * bs_], []

    gates_in = [(proj_b, d, gate_blk), (proj_b, d, gate_blk + 1)]
    merged, = _rowwise(merge_fn, [_whole(br_fox), _whole(br_sb)] + gates_in, [], [(d, BF16)], [], tm=tr, name="merge")
    h2 = _mm2d(merged, w_out_full, d, dims=NN, tm=tm, tn=tnd, tk=d, out_dtype=F32,
               epi=lambda accs, ex: [ex[0] + accs[0]], extras=[h1], name="out_proj")

    w_gu2, w_down2 = _send_wait(ffn2_copies, h2, "gather_ffn2_wait")
    saved3 = _ffn_up(h2, ffn2_norm, w_gu2, None, tm=tm, tag="ffn2")
    h3 = _ffn_down(h2, saved3[3], w_down2, tm=tm, tag="ffn2")

    skip = tile // tr

    def loss_fn(i, tiles, consts):
        real = i >= skip
        err = jnp.where(real, tiles[0] - tiles[1], 0.0)
        dy = err * (1.0 / d)
        part = 0.5 * jnp.sum(err * dy, axis=0, keepdims=True)
        return [dy, FFN_RESIDUAL_WEIGHT * dy], [part]

    dh3, dh3_half, loss_cols = _rowwise(
        loss_fn, [_whole(h3), _whole(loss_target[0])], [], [(d, F32, lp), (d, BF16, lp)], [(1, d)], tm=tr, name="loss",
        row_maps=[None, lambda i: jnp.maximum(i - skip, 0)])

    def own(g):
        return lax.dynamic_index_in_dim(g, me, 0, keepdims=False)

    def send_grads(grads_, name):
        return _send_start(grads_, [_landing(own(g), me) for g in grads_], "exchange", grads_[-1], name)

    sends = {}

    def send_piece(key):
        def on_ready(*gs):
            sends[key], token_ = send_grads(list(gs), f"exchange_{key}_start")
            return token_
        return on_ready

    dab3, dw_gu2, _ = _ffn_bwd_dw(dh3_half, saved3, w_down2, send_piece("down2"), tm=tm, tag="ffn2")
    dn3 = _ffn_bwd_dx(dab3, w_gu2, send_piece("gu2")(dw_gu2), tm=tm, tag="ffn2")
    dh2, dh2_bf, _, dg_ffn2 = _norm_bwd(dn3, h2, ffn2_norm, dh3, tm=tr, name="ffn2_norm_bwd")

    dmerged = _mm2d(dh2_bf, w_out_full, d, dims=NT, tm=tm, tn=tnd, tk=d, out_dtype=BF16, name="out_proj_dx")
    dw_out = _mm2d(merged, dh2_bf, d, dims=TN, tm=tnd, tn=tnd, tk=tkr, out_dtype=BF16, name="out_proj_dw")
    token = send_piece("out")(dw_out.reshape(NDEV, d8, d))

    def merge_bwd_fn(i, tiles, consts):
        dm, bf_, bs_, gf_, gs_ = [t.astype(F32) for t in tiles]
        sf, ss = _sigmoid(gf_), _sigmoid(gs_)
        return [dm * sf, dm * ss, dm * bf_ * sf * (1.0 - sf), dm * bs_ * ss * (1.0 - ss)], []

    dbr_fox, dbr_sb, dg_fox, dg_sb = _rowwise(
        merge_bwd_fn, [_whole(dmerged), _whole(br_fox), _whole(br_sb)] + gates_in, [], [(d, BF16)] * 4, [], tm=tr, after=token,
        name="merge_bwd")

    tnw = _pick(w, [512, 256, 128])

    def branch_dx(dbr, w_blocks, after, name):
        return _mm((lp // tm, w // tnw, NDEV), dbr, _spec((tm, d8), lambda i, j, k: (i, k)),
                   [(w_blocks, _spec((None, tnw, d8), lambda i, j, k: (k, j, 0)))], [],
                   [((lp, w), BF16, _spec((tm, tnw), lambda i, j, k: (i, j)))], dims=NT, acc_shape=(tm, tnw), after=after,
                   name=name)[0]

    def branch_dw(o, dbr, name):
        return _mm((w // tnw, NDEV, lp // tkr), o, _spec((tkr, tnw), lambda i, j, k: (k, i)),
                   [(dbr, _spec((tkr, d8), lambda i, j, k: (k, j)))], [],
                   [((NDEV, w, d8), BF16, _spec((None, tnw, d8), lambda i, j, k: (j, i, 0)))],
                   dims=TN, acc_shape=(tnw, d8), name=name)[0]

    token = send_piece("branch")(branch_dw(o_fox, dbr_fox, "branch_fox_dw"), branch_dw(o_sb, dbr_sb, "branch_sb_dw"))
    do_fox = branch_dx(dbr_fox, w_br_fox, token, "branch_fox_dx")
    do_sb = branch_dx(dbr_sb, w_br_sb, token, "branch_sb_dx")

    dfq, dfk, dfv, dc_rows = _fox_bwd(fq, fk, proj_b, 0, c_rows, o_fox, do_fox, lse, tile=tile, pad=pad, name="fox_bwd")
    dsq, dsk, dsv = _sb_bwd(proj_b, HEADS, 2 * HEADS, 3 * HEADS, do_sb, sb_total, tile=tile, pad=pad, name="sb_bwd")
    dc = jnp.pad(jnp.transpose(dc_rows.reshape(HEADS, lp)), ((0, 0), (0, LANES - HEADS)))
    dlogf = _cumsum_rows(dc, tile=tile, reverse=True, name="forget_cumsum_bwd")
    dproj_a, dg_q, dg_k, dg_b = _fox_prep_bwd(proj_a, dfq, dfk, dlogf, q_gain, k_gain, b_pad, tm=tr, pad=pad, name="fox_prep_bwd")
    dproj_b = jnp.concatenate([dfv, dsq, dsk, dsv, dg_fox, dg_sb], axis=1)

    dw_pa = _mm2d(n2, dproj_a, na, dims=TN, tm=tnd, tn=na, tk=_pick(lp, [544, 256, 128]), out_dtype=BF16, name="proj_a_dw")
    dw_pb = _mm2d(n2, dproj_b, nbw, dims=TN, tm=tnd, tn=tnb, tk=tkr, out_dtype=BF16, name="proj_b_dw")
    dw_in = jnp.concatenate([dw_pa[:, :2 * w], dw_pb[:, :w], dw_pa[:, 2 * w:2 * w + HEADS], dw_pb[:, w:]], axis=1)
    token = send_piece("w_in")(_split_cols(dw_in))
    dn2_a = _mm2d(dproj_a, w_pa, d, dims=NT, tm=tm, tn=tnd, tk=na, out_dtype=F32, after=token, name="proj_a_dx")
    dn2 = _mm2d(dproj_b, w_pb, d, dims=NT, tm=tm, tn=tnd, tk=_pick(nbw, [2048, 1024, 512, 256, 128]), out_dtype=F32,
                epi=lambda accs, ex: [accs[0] + ex[0]], extras=[dn2_a], name="proj_b_dx")
    dh1, _, dh1_half, dg_mix = _norm_bwd(dn2, h1, mix_norm, dh2, tm=tr, name="mix_norm_bwd")

    dab1, dw_gu1, _ = _ffn_bwd_dw(dh1_half, saved1, w_down1, send_piece("down1"), tm=tm, tag="ffn1")
    dn1 = _ffn_bwd_dx(dab1, w_gu1, send_piece("gu1")(dw_gu1), tm=tm, tag="ffn1")
    dh0, _, _, dg_ffn1 = _norm_bwd(dn1, h0, ffn1_norm, dh1, tm=tr, name="ffn1_norm_bwd")
    grad_x = dh0[tile:][None]

    r_down2, = _send_wait(sends["down2"], dh0, "exchange_down2_wait")
    r_gu2, = _send_wait(sends["gu2"], r_down2, "exchange_gu2_wait")
    r_out, = _send_wait(sends["out"], r_gu2, "exchange_out_wait")
    r_br_fox, r_br_sb = _send_wait(sends["branch"], r_out, "exchange_branch_wait")
    r_in, = _send_wait(sends["w_in"], r_br_sb, "exchange_w_in_wait")
    grads, deltas, new_ms, new_vs = {}, {}, {}, {}

    def adamw_big(entries):
        for k, (parts, sel, wt, mt, vt) in entries.items():
            g, dl, mn, vn = _adamw_summed(parts, sel, wt[0], mt[0], vt[0], name=f"adamw_{k}")
            grads[k], deltas[k], new_ms[k], new_vs[k] = g[None], dl[None], mn[None], vn[None]

    adamw_big(dict(ffn2_w_gate=(r_gu2, 0, ffn2_w_gate, m_ffn2_w_gate, v_ffn2_w_gate),
                   ffn2_w_up=(r_gu2, 1, ffn2_w_up, m_ffn2_w_up, v_ffn2_w_up),
                   ffn2_w_down=(r_down2, 0, ffn2_w_down, m_ffn2_w_down, v_ffn2_w_down),
                   w_in=(r_in, 0, w_in, m_w_in, v_w_in),
                   w_branch_fox=(r_br_fox, 0, w_branch_fox, m_w_branch_fox, v_w_branch_fox),
                   w_branch_sb=(r_br_sb, 0, w_branch_sb, m_w_branch_sb, v_w_branch_sb),
                   w_out=(r_out, 0, w_out, m_w_out, v_w_out)))
    done = sum(v[0, 0, :1] for v in new_vs.values())
    r_down1, = _send_wait(sends["down1"], done, "exchange_down1_wait")
    r_gu1, = _send_wait(sends["gu1"], r_down1, "exchange_gu1_wait")
    adamw_big(dict(ffn1_w_gate=(r_gu1, 0, ffn1_w_gate, m_ffn1_w_gate, v_ffn1_w_gate),
                   ffn1_w_up=(r_gu1, 1, ffn1_w_up, m_ffn1_w_up, v_ffn1_w_up),
                   ffn1_w_down=(r_down1, 0, ffn1_w_down, m_ffn1_w_down, v_ffn1_w_down)))

    small_parts = [dh0[pad:tile], dg_ffn1, dg_mix, dg_ffn2, dg_b[:, :HEADS], dg_q, dg_k, loss_cols]
    small_packed, small_spans = _pack_rows(small_parts, LANES, SMALL_ROWS)
    small_sum = _allsum_small(small_packed, "sum_small")
    g_meta_full, g_ffn1n, g_mixn, g_ffn2n, g_bf, g_qn, g_kn, loss_vec = [
        _unpack(small_sum, span, part.shape) for span, part in zip(small_spans, small_parts)]
    loss = jnp.sum(loss_vec)
    g_meta = lax.dynamic_slice_in_dim(g_meta_full, me * d8, d8, axis=1)
    g_qn, g_kn = g_qn.reshape(fox_q_norm.shape), g_kn.reshape(fox_k_norm.shape)

    small = dict(meta_tokens=(g_meta, meta_tokens, m_meta_tokens, v_meta_tokens),
                 ffn1_norm=(g_ffn1n, ffn1_norm, m_ffn1_norm, v_ffn1_norm),
                 mix_norm=(g_mixn, mix_norm, m_mix_norm, v_mix_norm),
                 b_forget=(g_bf, b_forget, m_b_forget, v_b_forget),
                 fox_q_norm=(g_qn, fox_q_norm, m_fox_q_norm, v_fox_q_norm),
                 fox_k_norm=(g_kn, fox_k_norm, m_fox_k_norm, v_fox_k_norm),
                 ffn2_norm=(g_ffn2n, ffn2_norm, m_ffn2_norm, v_ffn2_norm))
    for k, (g, wt, mt, vt) in small.items():
        flat = lambda t: t.reshape(-1, t.shape[-1])
        dl, mn, vn = _adamw_plain(flat(g), flat(wt), flat(mt), flat(vt), name=f"adamw_{k}")
        grads[k], deltas[k], new_ms[k], new_vs[k] = g, dl.reshape(wt.shape), mn.reshape(wt.shape), vn.reshape(wt.shape)

    order = ["meta_tokens", "ffn1_norm", "ffn1_w_gate", "ffn1_w_up", "ffn1_w_down", "mix_norm", "w_in", "b_forget",
             "fox_q_norm", "fox_k_norm", "w_branch_fox", "w_branch_sb", "w_out", "ffn2_norm", "ffn2_w_gate",
             "ffn2_w_up", "ffn2_w_down"]
    return (loss, grad_x, *[grads[k] for k in order], *[deltas[k] for k in order],
            *[new_ms[k] for k in order], *[new_vs[k] for k in order])
```

```python
import jax
import jax.numpy as jnp
from jax import lax
from jax.experimental import pallas as pl
from jax.experimental.pallas import tpu as pltpu

F32 = jnp.float32
BF16 = jnp.bfloat16
MESH = pl.DeviceIdType.MESH

NDEV = 8
N_META = 16
HEAD_DIM = 128
HEADS = 8
BRANCH_WIDTH = HEADS * HEAD_DIM
RMS_EPS = 1e-6
FFN_RESIDUAL_WEIGHT = 0.5
ATTN_SCALE = HEAD_DIM ** -0.5
MASKED_LOGIT = -1e30

ADAM_LR = 0.001
ADAM_B1 = 0.9
ADAM_B2 = 0.999
ADAM_EPS = 1e-08
ADAM_WD = 0.01
ADAM_STEP = 10

LANES = 128
SMALL_ROWS = 8
ADAMW_TILE_ELEMS = 160 * 1024
KSUB = 2


def _pick(n, prefs):
    for p in prefs:
        if p <= n and n % p == 0:
            return p
    return n


def _dot(a, b, dims):
    return lax.dot_general(a, b, (dims, ((), ())), preferred_element_type=F32)


NN = ((1,), (0,))
TN = ((0,), (0,))
NT = ((1,), (1,))


def _split_bf16(x):
    hi = x.astype(BF16)
    lo = (x - hi.astype(F32)).astype(BF16)
    return hi, lo


def _sigmoid_parts(z):
    e = jnp.exp(-jnp.abs(z))
    t = 1.0 + e
    return e, 1.0 / t, jnp.log(t)


def _sigmoid(x):
    return 0.5 * jnp.tanh(0.5 * x) + 0.5


def _my_place():
    return lax.axis_index("x"), lax.axis_index("y"), lax.axis_index("c")


def _flat_id(px, py, pc):
    return 4 * px + 2 * py + pc


def _peer(x, y, c, k):
    px = 1 - x if k & 4 else x
    py = 1 - y if k & 2 else y
    pc = 1 - c if k & 1 else c
    return px, py, pc


def _allgather(shards, name, in_vmem=False):
    n = len(shards)

    def body(*refs):
        x_refs, out_refs = refs[:n], refs[n:2 * n]
        send_sems, recv_sems, local_sems = refs[2 * n:]
        x, y, c = _my_place()
        me, sibling = (x, y, c), (x, y, 1 - c)
        chips = [(1 - x, y), (x, 1 - y), (1 - x, 1 - y)]

        def block(a, place):
            return out_refs[a].at[_flat_id(*place)]

        def copy(a, k, place, to, src=None):
            return pltpu.make_async_remote_copy(
                src_ref=block(a, place) if src is None else src, dst_ref=block(a, place),
                send_sem=send_sems.at[7 * a + k], recv_sem=recv_sems.at[7 * a + k], device_id=to, device_id_type=MESH)

        mine = [pltpu.make_async_copy(x_refs[a], block(a, me), local_sems.at[a]) for a in range(n)]
        for cp in mine:
            cp.start()
        first = []
        for a in range(n):
            first.append(copy(a, 0, me, sibling, src=x_refs[a]))
            first += [copy(a, 1 + j, me, (*chip, c), src=x_refs[a]) for j, chip in enumerate(chips)]
        for cp in first:
            cp.start()
        passed = []
        for j, chip in enumerate(chips):
            for a in range(n):
                copy(a, 1 + j, (*chip, c), me).wait_recv()
                passed.append(copy(a, 4 + j, (*chip, c), sibling))
                passed[-1].start()
        for a in range(n):
            copy(a, 0, sibling, me).wait_recv()
        for j, chip in enumerate(chips):
            for a in range(n):
                copy(a, 4 + j, (*chip, 1 - c), me).wait_recv()
        for cp in first + passed:
            cp.wait_send()
        for cp in mine:
            cp.wait()

    space = pltpu.VMEM if in_vmem else pl.ANY
    return pl.pallas_call(
        body, name=name,
        out_shape=[jax.ShapeDtypeStruct((NDEV,) + s.shape, s.dtype) for s in shards],
        in_specs=[pl.BlockSpec(memory_space=space)] * n,
        out_specs=[pl.BlockSpec(memory_space=space)] * n,
        scratch_shapes=[pltpu.SemaphoreType.DMA((7 * n,)), pltpu.SemaphoreType.DMA((7 * n,)), pltpu.SemaphoreType.DMA((n,))],
    )(*shards)


def _allsum_small(part, name):
    rows, cols = part.shape

    def body(p_ref, out_ref, buf, send_sems, recv_sems):
        x, y, c = _my_place()
        me = _flat_id(x, y, c)
        buf[me] = p_ref[...]
        copies = []
        for k in range(1, NDEV):
            copies.append(pltpu.make_async_remote_copy(
                src_ref=p_ref, dst_ref=buf.at[me], send_sem=send_sems.at[k - 1], recv_sem=recv_sems.at[k - 1],
                device_id=_peer(x, y, c, k), device_id_type=MESH))
        for cp in copies:
            cp.start()
        for cp in copies:
            cp.wait()
        total = buf[0]
        for j in range(1, NDEV):
            total = total + buf[j]
        out_ref[...] = total

    return pl.pallas_call(
        body, name=name,
        out_shape=jax.ShapeDtypeStruct((rows, cols), F32),
        in_specs=[pl.BlockSpec(memory_space=pltpu.VMEM)],
        out_specs=pl.BlockSpec(memory_space=pltpu.VMEM),
        scratch_shapes=[pltpu.VMEM((NDEV, rows, cols), F32),
                        pltpu.SemaphoreType.DMA((7,)), pltpu.SemaphoreType.DMA((7,))],
    )(part)


_HBM = pl.BlockSpec(memory_space=pltpu.HBM)
_SEM = pl.BlockSpec(memory_space=pltpu.SEMAPHORE)
_DATAFLOW = pltpu.SideEffectType.DATAFLOW_SIDE_EFFECTING


COPIES_PER_ARRAY = {"gather": 7, "exchange": 7, "first": 4, "forward": 3}


def _send_copies(plan, src_refs, land_refs, send_sems, recv_sems):
    x, y, c = _my_place()
    me = _flat_id(x, y, c)
    per = COPIES_PER_ARRAY[plan]
    copies = []

    def add(a, slot, src, dst, to):
        copies.append(pltpu.make_async_remote_copy(
            src_ref=src, dst_ref=dst, send_sem=send_sems.at[per * a + slot], recv_sem=recv_sems.at[per * a + slot],
            device_id=to, device_id_type=MESH))

    for a, land_ref in enumerate(land_refs):
        if plan in ("gather", "exchange"):
            for k in range(1, NDEV):
                peer = _peer(x, y, c, k)
                add(a, k - 1, src_refs[a].at[_flat_id(*peer)] if plan == "exchange" else src_refs[a], land_ref.at[me], peer)
        elif plan == "first":
            for slot, k in enumerate((1, 2, 4, 6)):
                add(a, slot, src_refs[a], land_ref.at[me], _peer(x, y, c, k))
        else:
            for slot, k in enumerate((2, 4, 6)):
                block = land_ref.at[_flat_id(*_peer(x, y, c, k))]
                add(a, slot, block, block, (x, y, 1 - c))
    return copies


def _send_start(srcs, lands, plan, after, name):
    ns, nl = len(srcs), len(lands)
    nsem = COPIES_PER_ARRAY[plan] * nl

    def body(*refs):
        for cp in _send_copies(plan, refs[:ns], refs[ns:ns + nl], refs[ns + nl + 1], refs[ns + nl + 2]):
            cp.start()
        refs[-1][...] = jnp.zeros_like(refs[-1])

    operands = [pltpu.with_memory_space_constraint(t, pltpu.HBM) for t in list(srcs) + list(lands)]
    outs = pl.pallas_call(
        body, name=name,
        out_shape=(pltpu.SemaphoreType.DMA((nsem,)), pltpu.SemaphoreType.DMA((nsem,)),
                   *[pltpu.HBM(t.shape, t.dtype) for t in operands], jax.ShapeDtypeStruct((SMALL_ROWS, LANES), F32)),
        in_specs=[_HBM] * (ns + nl) + [pl.BlockSpec(memory_space=pl.ANY)],
        out_specs=(_SEM, _SEM, *[_HBM] * (ns + nl), pl.BlockSpec(memory_space=pltpu.VMEM)),
        input_output_aliases={i: 2 + i for i in range(ns + nl)},
        compiler_params=pltpu.CompilerParams(has_side_effects=_DATAFLOW),
    )(*operands, after)
    return (plan, outs[0], outs[1], list(outs[2:2 + ns]), list(outs[2 + ns:2 + ns + nl])), outs[-1]


def _send_wait(handle, after, name):
    plan, send_sems, recv_sems, srcs, lands = handle
    ns, nl = len(srcs), len(lands)

    def body(*refs):
        for cp in _send_copies(plan, refs[:ns], refs[ns:ns + nl], refs[ns + nl], refs[ns + nl + 1]):
            cp.wait_send()
            cp.wait_recv()

    outs = pl.pallas_call(
        body, name=name,
        out_shape=tuple(pltpu.HBM(t.shape, t.dtype) for t in srcs + lands),
        in_specs=[_HBM] * (ns + nl) + [_SEM, _SEM, pl.BlockSpec(memory_space=pl.ANY)],
        out_specs=tuple([_HBM] * (ns + nl)),
        input_output_aliases={i: i for i in range(ns + nl)},
        compiler_params=pltpu.CompilerParams(has_side_effects=_DATAFLOW),
    )(*srcs, *lands, send_sems, recv_sems, after)
    return list(outs[ns:])


def _gather_two_level(first_handle, after, name):
    lands = _send_wait(first_handle, after, f"{name}_wait")
    forward, _ = _send_start([], lands, "forward", after, f"{name}_forward_start")
    return _send_wait(forward, after, f"{name}_forward_wait")


def _landing(own_block, me):
    return lax.dynamic_update_index_in_dim(lax.empty((NDEV,) + own_block.shape, own_block.dtype), own_block, me, 0)


def _spec(block, index_map):
    return pl.BlockSpec(block, index_map)


def _mm(grid, a, a_spec, bs, extras, outs, *, dims, acc_shape=None, epi=None, after=None, name):
    nk = grid[2]
    nb, ne, no = len(bs), len(extras), len(outs)
    nafter = 0 if after is None else 1
    if epi is None:
        epi = lambda accs, ex: [accs[0]]

    def body(*refs):
        a_ref, b_refs = refs[0], refs[1:1 + nb]
        e_refs = refs[1 + nb:1 + nb + ne]
        o_refs = refs[1 + nb + ne + nafter:1 + nb + ne + nafter + no]
        acc_refs = refs[1 + nb + ne + nafter + no:]
        def finish(accs):
            for o_ref, tile in zip(o_refs, epi(accs, [e_ref[...] for e_ref in e_refs])):
                o_ref[...] = tile.astype(o_ref.dtype)

        def product(b_ref):
            if len(a_ref.shape) == 2:
                return _dot(a_ref[...], b_ref[...], dims)
            total = _dot(a_ref[0], b_ref[0], dims)
            for t in range(1, a_ref.shape[0]):
                total = total + _dot(a_ref[t], b_ref[t], dims)
            return total

        if nk == 1:
            finish([product(b_ref) for b_ref in b_refs])
        else:
            k = pl.program_id(2)

            @pl.when(k == 0)
            def _():
                for acc_ref in acc_refs:
                    acc_ref[...] = jnp.zeros_like(acc_ref)

            for acc_ref, b_ref in zip(acc_refs, b_refs):
                acc_ref[...] += product(b_ref)

            @pl.when(k == nk - 1)
            def _():
                finish([acc_ref[...] for acc_ref in acc_refs])

    return pl.pallas_call(
        body, name=name,
        grid=grid,
        in_specs=[a_spec] + [s for _, s in bs] + [s for _, s in extras] + [pl.BlockSpec(memory_space=pl.ANY)] * nafter,
        out_specs=[s for _, _, s in outs],
        out_shape=[jax.ShapeDtypeStruct(shape, dt) for shape, dt, _ in outs],
        scratch_shapes=[pltpu.VMEM(acc_shape, F32) for _ in bs] if nk > 1 else [],
        compiler_params=pltpu.CompilerParams(dimension_semantics=("parallel", "parallel", "arbitrary")),
    )(a, *[b for b, _ in bs], *[e for e, _ in extras], *([after] if nafter else []))


def _mm2d(a, b, n_cols, *, dims, tm, tn, tk, out_dtype, epi=None, extras=(), after=None, name):
    m_rows, k_len = (a.shape[1], a.shape[0]) if dims == TN else a.shape
    assert m_rows % tm == 0 and n_cols % tn == 0 and k_len % tk == 0, (name, a.shape, n_cols, tm, tn, tk)
    a_spec = _spec((tk, tm), lambda i, j, k: (k, i)) if dims == TN else _spec((tm, tk), lambda i, j, k: (i, k))
    b_spec = _spec((tn, tk), lambda i, j, k: (j, k)) if dims == NT else _spec((tk, tn), lambda i, j, k: (k, j))
    tile = _spec((tm, tn), lambda i, j, k: (i, j))
    return _mm((m_rows // tm, n_cols // tn, k_len // tk), a, a_spec, [(b, b_spec)], [(e, tile) for e in extras],
               [((m_rows, n_cols), out_dtype, tile)], dims=dims, acc_shape=(tm, tn), epi=epi, after=after, name=name)[0]


def _rowwise(fn, ins, consts, outs, sums, *, tm, name, row_maps=None, after=None):
    m_rows = outs[0][2] if len(outs[0]) == 3 else ins[0][0].shape[0]
    n = m_rows // tm
    ni, nc, no = len(ins), len(consts), len(outs)
    nafter = 0 if after is None else 1
    row_maps = row_maps or [None] * ni

    def body(*refs):
        i = pl.program_id(0)
        in_tiles = [r[...] for r in refs[:ni]]
        const_values = [r[...] for r in refs[ni:ni + nc]]
        o_refs = refs[ni + nc + nafter:ni + nc + nafter + no]
        s_refs = refs[ni + nc + nafter + no:]
        out_tiles, sum_terms = fn(i, in_tiles, const_values)
        for o_ref, tile in zip(o_refs, out_tiles):
            o_ref[...] = tile.astype(o_ref.dtype)
        if s_refs:
            @pl.when(i == 0)
            def _():
                for s_ref in s_refs:
                    s_ref[...] = jnp.zeros_like(s_ref)

            for s_ref, term in zip(s_refs, sum_terms):
                s_ref[...] += term

    def in_spec(width, col, rmap):
        if rmap is None:
            return pl.BlockSpec((tm, width), lambda i: (i, col))
        return pl.BlockSpec((tm, width), lambda i: (rmap(i), col))

    return pl.pallas_call(
        body, name=name,
        grid=(n,),
        in_specs=[in_spec(w, col, rmap) for (_, w, col), rmap in zip(ins, row_maps)]
        + [pl.BlockSpec(cst.shape, lambda i: (0, 0)) for cst in consts] + [pl.BlockSpec(memory_space=pl.ANY)] * nafter,
        out_specs=[pl.BlockSpec((tm, o[0]), lambda i: (i, 0)) for o in outs]
        + [pl.BlockSpec(s, lambda i: (0, 0)) for s in sums],
        out_shape=[jax.ShapeDtypeStruct((m_rows, o[0]), o[1]) for o in outs]
        + [jax.ShapeDtypeStruct(s, F32) for s in sums],
        compiler_params=pltpu.CompilerParams(dimension_semantics=("arbitrary",)),
    )(*[arr for arr, _, _ in ins], *consts, *([after] if nafter else []))


def _whole(arr):
    return (arr, arr.shape[1], 0)


def _rms(x, gain):
    r = lax.rsqrt(jnp.mean(x * x, axis=-1, keepdims=True) + RMS_EPS)
    return x * r * gain


def _rms_bwd(x, gain, dy):
    r = lax.rsqrt(jnp.mean(x * x, axis=-1, keepdims=True) + RMS_EPS)
    u = dy * gain
    dx = r * u - x * (r * r * r) * jnp.mean(x * u, axis=-1, keepdims=True)
    return dx, dy * x * r


def _norm_fwd(h, gain, *, tm, name):
    def fn(i, tiles, consts):
        return [_rms(tiles[0], consts[0])], []
    return _rowwise(fn, [_whole(h)], [gain], [(h.shape[1], BF16)], [], tm=tm, name=name)[0]


def _norm_bwd(dn, h, gain, dh_in, *, tm, name, after=None):
    d = h.shape[1]

    def fn(i, tiles, consts):
        dx, dg_rows = _rms_bwd(tiles[1], consts[0], tiles[0])
        dh = tiles[2] + dx
        return [dh, dh, FFN_RESIDUAL_WEIGHT * dh], [jnp.sum(dg_rows, axis=0, keepdims=True)]

    return _rowwise(fn, [_whole(dn), _whole(h), _whole(dh_in)], [gain],
                    [(d, F32), (d, BF16), (d, BF16)], [(1, d)], tm=tm, name=name, after=after)


def _swiglu_epi(accs, ex):
    a, b = accs
    return [a, b, a * _sigmoid(a) * b]


def _swiglu_bwd_epi(accs, ex):
    ds = accs[0]
    a, b = ex[0].astype(F32), ex[1].astype(F32)
    sig = _sigmoid(a)
    silu = a * sig
    dsilu = sig * (1.0 + a * (1.0 - sig))
    return [jnp.stack([(ds * b * dsilu).astype(BF16), (ds * silu).astype(BF16)], axis=0)]


def _attn_mask(i, j, tile, pad, strict):
    row = i * tile + lax.broadcasted_iota(jnp.int32, (tile, tile), 0)
    col = j * tile + lax.broadcasted_iota(jnp.int32, (tile, tile), 1)
    causal = (col < row) if strict else (col <= row)
    return causal & ((col >= pad) | (row < pad))


FWD_GROUP = 4
BWD_GROUP = 2


def _walk_key_tiles(i, step, carry, descending=False):
    inner = jnp.maximum(i - 1, 0)
    pairs = inner // 2
    single = lambda j, c: step(j, c, False, 1)
    diagonal = lambda j, c: step(j, c, True, 1)
    if descending:
        carry = lax.fori_loop(jnp.maximum(i, 1), i + 1, diagonal, carry)
        carry = lax.fori_loop(0, pairs, lambda t, c: step(i - 2 - 2 * t, c, False, 2), carry)
        carry = lax.fori_loop(1, 1 + inner - 2 * pairs, single, carry)
        return step(0, carry, True, 1)
    carry = step(0, carry, True, 1)
    carry = lax.fori_loop(0, pairs, lambda t, c: step(1 + 2 * t, c, False, 2), carry)
    carry = lax.fori_loop(1 + 2 * pairs, i, single, carry)
    return lax.fori_loop(jnp.maximum(i, 1), i + 1, diagonal, carry)


def _head_cols(g):
    return pl.ds(g * HEAD_DIM, HEAD_DIM)


def _key_rows(j, tile, nt):
    return pl.ds(pl.multiple_of(j * tile, tile), nt * tile)


def _key_tile(ref, j, tile, g, nt=1):
    return ref[_key_rows(j, tile, nt), _head_cols(g)]


def _key_bias(c_ref, g, j, nt):
    return c_ref[g, j] if nt == 1 else jnp.concatenate([c_ref[g, j + t] for t in range(nt)], axis=1)


def _head_specs(lp, tile, q_off, k_off, v_off, hp):
    gw = hp * HEAD_DIM
    q_spec = pl.BlockSpec((tile, gw), lambda h, i: (i, h + q_off // hp))
    k_spec = pl.BlockSpec((lp, gw), lambda h, i: (0, h + k_off // hp))
    v_spec = pl.BlockSpec((lp, gw), lambda h, i: (0, h + v_off // hp))
    return q_spec, k_spec, v_spec


def _fox_fwd(q, k, v, v_off, c_rows, *, tile, pad, name):
    lp = q.shape[0]
    nb = lp // tile
    hp = FWD_GROUP
    gw = hp * HEAD_DIM

    def body(q_ref, k_ref, v_ref, c_ref, o_ref, lse_ref):
        i = pl.program_id(1)
        qts = [q_ref[:, _head_cols(g)] for g in range(hp)]

        def step(j, carry, masked, nt):
            ok = _attn_mask(i, j, tile, pad, False) if masked else None
            out = []
            for g, (m, l, acc) in enumerate(carry):
                s = _dot(qts[g], _key_tile(k_ref, j, tile, g, nt), NT) * ATTN_SCALE - _key_bias(c_ref, g, j, nt)
                if masked:
                    s = jnp.where(ok, s, MASKED_LOGIT)
                m_new = jnp.maximum(m, jnp.max(s, axis=1, keepdims=True))
                p = jnp.exp(s - m_new)
                alpha = jnp.exp(m - m_new)
                l = alpha * l + jnp.sum(p, axis=1, keepdims=True)
                acc = alpha * acc + _dot(p.astype(BF16), _key_tile(v_ref, j, tile, g, nt), NN)
                out.append((m_new, l, acc))
            return tuple(out)

        init = (jnp.full((tile, 1), MASKED_LOGIT, F32), jnp.zeros((tile, 1), F32), jnp.zeros((tile, HEAD_DIM), F32))
        final = _walk_key_tiles(i, step, (init,) * hp)
        for g, (m, l, acc) in enumerate(final):
            o_ref[:, _head_cols(g)] = (acc / l).astype(o_ref.dtype)
            lse_ref[g] = jnp.broadcast_to(m + jnp.log(l), (tile, LANES))

    q_spec, k_spec, v_spec = _head_specs(lp, tile, 0, 0, v_off, hp)
    return pl.pallas_call(
        body, name=name,
        grid=(HEADS // hp, nb),
        in_specs=[q_spec, k_spec, v_spec, pl.BlockSpec((hp, nb, 1, tile), lambda h, i: (h, 0, 0, 0))],
        out_specs=[pl.BlockSpec((tile, gw), lambda h, i: (i, h)),
                   pl.BlockSpec((hp, tile, LANES), lambda h, i: (h, i, 0))],
        out_shape=[jax.ShapeDtypeStruct((lp, BRANCH_WIDTH), BF16), jax.ShapeDtypeStruct((HEADS, lp, LANES), F32)],
        compiler_params=pltpu.CompilerParams(dimension_semantics=("parallel", "arbitrary")),
    )(q, k, v, c_rows)


def _fox_bwd(q, k, v, v_off, c_rows, o, do, lse, *, tile, pad, name):
    lp = q.shape[0]
    nb = lp // tile
    hp = BWD_GROUP
    gw = hp * HEAD_DIM

    def body(q_ref, k_ref, v_ref, c_ref, o_ref, do_ref, lse_ref, dq_ref, dk_ref, dv_ref, dc_ref, dk_acc, dv_acc, dc_acc):
        i = pl.program_id(1)

        @pl.when(i == 0)
        def _():
            dk_acc[...] = jnp.zeros_like(dk_acc)
            dv_acc[...] = jnp.zeros_like(dv_acc)
            dc_acc[...] = jnp.zeros_like(dc_acc)

        heads = range(hp)
        qts = [q_ref[:, _head_cols(g)] for g in heads]
        dots = [do_ref[:, _head_cols(g)] for g in heads]
        deltas = [jnp.sum(dots[g].astype(F32) * o_ref[:, _head_cols(g)].astype(F32), axis=1, keepdims=True) for g in heads]
        lse_cols = [lse_ref[g][:, :1] for g in heads]

        def step(j, dqs, masked, nt):
            rows = _key_rows(j, tile, nt)
            ok = _attn_mask(i, j, tile, pad, False) if masked else None
            out = []
            for g in heads:
                kt = _key_tile(k_ref, j, tile, g, nt)
                s = _dot(qts[g], kt, NT) * ATTN_SCALE - _key_bias(c_ref, g, j, nt)
                p = jnp.exp(s - lse_cols[g])
                if masked:
                    p = jnp.where(ok, p, 0.0)
                dp = _dot(dots[g], _key_tile(v_ref, j, tile, g, nt), NT)
                ds = p * (dp - deltas[g])
                dsb = ds.astype(BF16)
                dk_acc[rows, _head_cols(g)] += _dot(dsb, qts[g], TN)
                dv_acc[rows, _head_cols(g)] += _dot(p.astype(BF16), dots[g], TN)
                dc = -jnp.sum(ds, axis=0, keepdims=True)
                for t in range(nt):
                    dc_acc[g, j + t] += dc[:, t * tile:(t + 1) * tile]
                out.append(dqs[g] + _dot(dsb, kt, NN))
            return tuple(out)

        dqs = _walk_key_tiles(i, step, (jnp.zeros((tile, HEAD_DIM), F32),) * hp)
        for g in heads:
            dq_ref[:, _head_cols(g)] = dqs[g] * ATTN_SCALE

        @pl.when(i == nb - 1)
        def _():
            dk_ref[...] = dk_acc[...] * ATTN_SCALE
            dv_ref[...] = dv_acc[...].astype(dv_ref.dtype)
            dc_ref[...] = dc_acc[...]

    q_spec, k_spec, v_spec = _head_specs(lp, tile, 0, 0, v_off, hp)
    tile_spec = pl.BlockSpec((tile, gw), lambda h, i: (i, h))
    head_spec = pl.BlockSpec((lp, gw), lambda h, i: (0, h))
    c_spec = pl.BlockSpec((hp, nb, 1, tile), lambda h, i: (h, 0, 0, 0))
    return pl.pallas_call(
        body, name=name,
        grid=(HEADS // hp, nb),
        in_specs=[q_spec, k_spec, v_spec, c_spec, tile_spec, tile_spec,
                  pl.BlockSpec((hp, tile, LANES), lambda h, i: (h, i, 0))],
        out_specs=[tile_spec, head_spec, head_spec, c_spec],
        out_shape=[jax.ShapeDtypeStruct((lp, BRANCH_WIDTH), F32), jax.ShapeDtypeStruct((lp, BRANCH_WIDTH), F32),
                   jax.ShapeDtypeStruct((lp, BRANCH_WIDTH), BF16), jax.ShapeDtypeStruct((HEADS, nb, 1, tile), F32)],
        scratch_shapes=[pltpu.VMEM((lp, gw), F32), pltpu.VMEM((lp, gw), F32),
                        pltpu.VMEM((hp, nb, 1, tile), F32)],
        compiler_params=pltpu.CompilerParams(dimension_semantics=("parallel", "arbitrary")),
    )(q, k, v, c_rows, o, do, lse)


def _later_matrix(tile):
    return (lax.broadcasted_iota(jnp.int32, (tile, tile), 0) > lax.broadcasted_iota(jnp.int32, (tile, tile), 1)).astype(BF16)


def _earlier_matrix(tile):
    return (lax.broadcasted_iota(jnp.int32, (tile, tile), 0) < lax.broadcasted_iota(jnp.int32, (tile, tile), 1)).astype(BF16)


def _sb_tile(qt, kt, ok, later):
    z = _dot(qt, kt, NT) * ATTN_SCALE
    e, r, lg = _sigmoid_parts(z)
    sp = jnp.maximum(z, 0.0) + lg
    spm = sp if ok is None else jnp.where(ok, sp, 0.0)
    hi, lo = _split_bf16(spm)
    within = _dot(hi, later, NN) + _dot(lo, later, NN)
    return z, e, r, sp, spm, within


def _sb_fwd(qkv, q_off, k_off, v_off, *, tile, pad, name):
    lp = qkv.shape[0]
    nb = lp // tile
    hp = FWD_GROUP
    gw = hp * HEAD_DIM

    def body(q_ref, k_ref, v_ref, o_ref, tot_ref):
        i = pl.program_id(1)
        qts = [q_ref[:, _head_cols(g)] for g in range(hp)]
        later = {nt: _later_matrix(nt * tile) for nt in (1, 2)}

        def step(j, carry, masked, nt):
            ok = _attn_mask(i, j, tile, pad, True) if masked else None
            out = []
            for g, (right, acc) in enumerate(carry):
                z, _, _, sp, spm, within = _sb_tile(qts[g], _key_tile(k_ref, j, tile, g, nt), ok, later[nt])
                w = jnp.exp(z - sp - within - right)
                if masked:
                    w = jnp.where(ok, w, 0.0)
                acc = acc + _dot(w.astype(BF16), _key_tile(v_ref, j, tile, g, nt), NN)
                out.append((right + jnp.sum(spm, axis=1, keepdims=True), acc))
            return tuple(out)

        init = (jnp.zeros((tile, 1), F32), jnp.zeros((tile, HEAD_DIM), F32))
        final = _walk_key_tiles(i, step, (init,) * hp, descending=True)
        for g, (total, acc) in enumerate(final):
            o_ref[:, _head_cols(g)] = acc.astype(o_ref.dtype)
            tot_ref[g] = jnp.broadcast_to(total, (tile, LANES))

    q_spec, k_spec, v_spec = _head_specs(lp, tile, q_off, k_off, v_off, hp)
    return pl.pallas_call(
        body, name=name,
        grid=(HEADS // hp, nb),
        in_specs=[q_spec, k_spec, v_spec],
        out_specs=[pl.BlockSpec((tile, gw), lambda h, i: (i, h)),
                   pl.BlockSpec((hp, tile, LANES), lambda h, i: (h, i, 0))],
        out_shape=[jax.ShapeDtypeStruct((lp, BRANCH_WIDTH), BF16), jax.ShapeDtypeStruct((HEADS, lp, LANES), F32)],
        compiler_params=pltpu.CompilerParams(dimension_semantics=("parallel", "arbitrary")),
    )(qkv, qkv, qkv)


def _sb_bwd(qkv, q_off, k_off, v_off, do, total, *, tile, pad, name):
    lp = qkv.shape[0]
    nb = lp // tile
    hp = BWD_GROUP
    gw = hp * HEAD_DIM

    def body(q_ref, k_ref, v_ref, do_ref, tot_ref, dq_ref, dk_ref, dv_ref, dk_acc, dv_acc):
        i = pl.program_id(1)

        @pl.when(i == 0)
        def _():
            dk_acc[...] = jnp.zeros_like(dk_acc)
            dv_acc[...] = jnp.zeros_like(dv_acc)

        heads = range(hp)
        qts = [q_ref[:, _head_cols(g)] for g in heads]
        dots = [do_ref[:, _head_cols(g)] for g in heads]
        total_cols = [tot_ref[g][:, :1] for g in heads]
        later = {nt: _later_matrix(nt * tile) for nt in (1, 2)}
        earlier = {nt: _earlier_matrix(nt * tile) for nt in (1, 2)}

        def step(j, carry, masked, nt):
            rows = _key_rows(j, tile, nt)
            ok = _attn_mask(i, j, tile, pad, True) if masked else None
            out = []
            for g, (dq, sp_before, dlw_before) in enumerate(carry):
                kt = _key_tile(k_ref, j, tile, g, nt)
                z, e, r, sp, spm, within = _sb_tile(qts[g], kt, ok, later[nt])
                sp_here = jnp.sum(spm, axis=1, keepdims=True)
                right = total_cols[g] - sp_before - sp_here
                w = jnp.exp(z - sp - within - right)
                if masked:
                    w = jnp.where(ok, w, 0.0)
                dlw = w * _dot(dots[g], _key_tile(v_ref, j, tile, g, nt), NT)
                hi, lo = _split_bf16(dlw)
                before = dlw_before + _dot(hi, earlier[nt], NN) + _dot(lo, earlier[nt], NN)
                sig = jnp.where(z >= 0, r, e * r)
                dz = dlw * (1.0 - sig) - sig * before
                if masked:
                    dz = jnp.where(ok, dz, 0.0)
                dzb = dz.astype(BF16)
                dk_acc[rows, _head_cols(g)] += _dot(dzb, qts[g], TN)
                dv_acc[rows, _head_cols(g)] += _dot(w.astype(BF16), dots[g], TN)
                out.append((dq + _dot(dzb, kt, NN), sp_before + sp_here, dlw_before + jnp.sum(dlw, axis=1, keepdims=True)))
            return tuple(out)

        zero_col = jnp.zeros((tile, 1), F32)
        final = _walk_key_tiles(i, step, ((jnp.zeros((tile, HEAD_DIM), F32), zero_col, zero_col),) * hp)
        for g in heads:
            dq_ref[:, _head_cols(g)] = (final[g][0] * ATTN_SCALE).astype(dq_ref.dtype)

        @pl.when(i == nb - 1)
        def _():
            dk_ref[...] = (dk_acc[...] * ATTN_SCALE).astype(dk_ref.dtype)
            dv_ref[...] = dv_acc[...].astype(dv_ref.dtype)

    q_spec, k_spec, v_spec = _head_specs(lp, tile, q_off, k_off, v_off, hp)
    tile_spec = pl.BlockSpec((tile, gw), lambda h, i: (i, h))
    head_spec = pl.BlockSpec((lp, gw), lambda h, i: (0, h))
    return pl.pallas_call(
        body, name=name,
        grid=(HEADS // hp, nb),
        in_specs=[q_spec, k_spec, v_spec, tile_spec, pl.BlockSpec((hp, tile, LANES), lambda h, i: (h, i, 0))],
        out_specs=[tile_spec, head_spec, head_spec],
        out_shape=[jax.ShapeDtypeStruct((lp, BRANCH_WIDTH), BF16)] * 3,
        scratch_shapes=[pltpu.VMEM((lp, gw), F32), pltpu.VMEM((lp, gw), F32)],
        compiler_params=pltpu.CompilerParams(dimension_semantics=("parallel", "arbitrary")),
    )(qkv, qkv, qkv, do, total)


def _cumsum_rows(x, *, tile, reverse, name):
    lp = x.shape[0]
    nb = lp // tile

    def body(x_ref, o_ref, carry):
        @pl.when(pl.program_id(0) == 0)
        def _():
            carry[...] = jnp.zeros_like(carry)

        r = lax.broadcasted_iota(jnp.int32, (tile, tile), 0)
        c = lax.broadcasted_iota(jnp.int32, (tile, tile), 1)
        tri = ((c >= r) if reverse else (c <= r)).astype(BF16)
        hi, lo = _split_bf16(x_ref[...])
        run = _dot(tri, hi, NN) + _dot(tri, lo, NN) + carry[...]
        o_ref[...] = run
        carry[...] = run[:1, :] if reverse else run[tile - 1:, :]

    order = (lambda i: (nb - 1 - i, 0)) if reverse else (lambda i: (i, 0))
    return pl.pallas_call(
        body, name=name,
        grid=(nb,),
        in_specs=[pl.BlockSpec((tile, LANES), order)],
        out_specs=pl.BlockSpec((tile, LANES), order),
        out_shape=jax.ShapeDtypeStruct((lp, LANES), F32),
        scratch_shapes=[pltpu.VMEM((1, LANES), F32)],
        compiler_params=pltpu.CompilerParams(dimension_semantics=("arbitrary",)),
    )(x)


def _log_sigmoid(x):
    return jnp.minimum(x, 0.0) - jnp.log(1.0 + jnp.exp(-jnp.abs(x)))


def _forget_mask(i, tm, pad):
    row = i * tm + lax.broadcasted_iota(jnp.int32, (tm, LANES), 0)
    lane = lax.broadcasted_iota(jnp.int32, (tm, LANES), 1)
    return (row >= pad) & (lane < HEADS)


def _fox_prep(proj_a, q_gain, k_gain, b_forget, *, tm, pad, name):
    w = BRANCH_WIDTH

    def fn(i, tiles, consts):
        pa = tiles[0]
        qs, ks = [], []
        for h in range(HEADS):
            lo = h * HEAD_DIM
            qs.append(_rms(pa[:, lo:lo + HEAD_DIM], consts[0][:, lo:lo + HEAD_DIM]))
            ks.append(_rms(pa[:, w + lo:w + lo + HEAD_DIM], consts[1][:, lo:lo + HEAD_DIM]))
        logf = jnp.where(_forget_mask(i, tm, pad), _log_sigmoid(pa[:, 2 * w:] + consts[2]), 0.0)
        return [jnp.concatenate(qs, axis=1), jnp.concatenate(ks, axis=1), logf], []

    return _rowwise(fn, [_whole(proj_a)], [q_gain, k_gain, b_forget],
                    [(w, BF16), (w, BF16), (LANES, F32)], [], tm=tm, name=name)


def _fox_prep_bwd(proj_a, dq, dk, dlogf, q_gain, k_gain, b_forget, *, tm, pad, name):
    w = BRANCH_WIDTH

    def fn(i, tiles, consts):
        pa, dqt, dkt, dlf = tiles
        dxs_q, dxs_k, dgs_q, dgs_k = [], [], [], []
        for h in range(HEADS):
            lo = h * HEAD_DIM
            dx, dg = _rms_bwd(pa[:, lo:lo + HEAD_DIM], consts[0][:, lo:lo + HEAD_DIM], dqt[:, lo:lo + HEAD_DIM])
            dxs_q.append(dx)
            dgs_q.append(jnp.sum(dg, axis=0, keepdims=True))
            dx, dg = _rms_bwd(pa[:, w + lo:w + lo + HEAD_DIM], consts[1][:, lo:lo + HEAD_DIM], dkt[:, lo:lo + HEAD_DIM])
            dxs_k.append(dx)
            dgs_k.append(jnp.sum(dg, axis=0, keepdims=True))
        xf = pa[:, 2 * w:] + consts[2]
        e, r, _ = _sigmoid_parts(xf)
        df = jnp.where(_forget_mask(i, tm, pad), dlf * jnp.where(xf >= 0, e * r, r), 0.0)
        return ([jnp.concatenate(dxs_q + dxs_k + [df], axis=1)],
                [jnp.concatenate(dgs_q, axis=1), jnp.concatenate(dgs_k, axis=1), jnp.sum(df, axis=0, keepdims=True)])

    return _rowwise(fn, [_whole(proj_a), _whole(dq), _whole(dk), _whole(dlogf)], [q_gain, k_gain, b_forget],
                    [(2 * w + LANES, BF16)], [(1, w), (1, w), (1, LANES)], tm=tm, name=name)


def _adamw_math(w, g, m, v):
    m = ADAM_B1 * m + (1.0 - ADAM_B1) * g
    v = ADAM_B2 * v + (1.0 - ADAM_B2) * (g * g)
    m_hat = m / (1.0 - ADAM_B1 ** ADAM_STEP)
    v_hat = v / (1.0 - ADAM_B2 ** ADAM_STEP)
    delta = -ADAM_LR * (m_hat / (jnp.sqrt(v_hat) + ADAM_EPS) + ADAM_WD * w)
    return delta, m, v


def _adamw_summed(parts, sel, w, m, v, *, name):
    rows, cols = w.shape
    tr = _pick(rows, [t for t in (512, 256, 128, 64, 32, 16, 8) if t * cols <= ADAMW_TILE_ELEMS])

    def body(p_ref, w_ref, m_ref, v_ref, g_out, d_out, m_out, v_out):
        g = p_ref[0].astype(F32)
        for j in range(1, NDEV):
            g = g + p_ref[j].astype(F32)
        delta, m_new, v_new = _adamw_math(w_ref[...], g, m_ref[...], v_ref[...])
        g_out[...] = g
        d_out[...] = delta
        m_out[...] = m_new
        v_out[...] = v_new

    spec = pl.BlockSpec((tr, cols), lambda i: (i, 0))
    if parts.ndim == 3:
        p_spec = pl.BlockSpec((NDEV, tr, cols), lambda i: (0, i, 0))
    else:
        p_spec = pl.BlockSpec((NDEV, None, tr, cols), lambda i: (0, sel, i, 0))
    return pl.pallas_call(
        body, name=name,
        grid=(rows // tr,),
        in_specs=[p_spec, spec, spec, spec],
        out_specs=[spec] * 4,
        out_shape=[jax.ShapeDtypeStruct((rows, cols), F32)] * 4,
        compiler_params=pltpu.CompilerParams(dimension_semantics=("parallel",)),
    )(parts, w, m, v)


def _adamw_plain(g, w, m, v, *, name):
    def body(g_ref, w_ref, m_ref, v_ref, d_out, m_out, v_out):
        delta, m_new, v_new = _adamw_math(w_ref[...], g_ref[...], m_ref[...], v_ref[...])
        d_out[...] = delta
        m_out[...] = m_new
        v_out[...] = v_new

    return pl.pallas_call(body, name=name, out_shape=[jax.ShapeDtypeStruct(w.shape, F32)] * 3)(g, w, m, v)


def _pack_rows(arrays, width, row_align):
    pieces, spans, at = [], [], 0
    for arr in arrays:
        flat = arr.reshape(-1)
        rows = -(-flat.shape[0] // (width * row_align)) * row_align
        flat = jnp.pad(flat, (0, rows * width - flat.shape[0]))
        pieces.append(flat.reshape(rows, width))
        spans.append((at, rows))
        at += rows
    return jnp.concatenate(pieces, axis=0), spans


def _unpack(rows2d, span, shape):
    at, rows = span
    size = 1
    for s in shape:
        size *= s
    return rows2d[at:at + rows].reshape(-1)[:size].reshape(shape)


def _join_cols(blocks):
    n, rows, cols = blocks.shape
    return jnp.transpose(blocks, (1, 0, 2)).reshape(rows, n * cols)


def _split_cols(full):
    rows, cols = full.shape
    return jnp.transpose(full.reshape(rows, NDEV, cols // NDEV), (1, 0, 2))


def _ffn_up(h, gain, w_gu, after, *, tm, tag):
    lp, d = h.shape
    f8 = w_gu.shape[3]
    n = _norm_fwd(h, gain, tm=_pick(lp, [256, 128]), name=f"{tag}_norm")
    hid = _spec((None, tm, f8), lambda i, j, k: (j, i, 0))
    a, b, s = _mm((lp // tm, NDEV, 1), n, _spec((tm, d), lambda i, j, k: (i, 0)),
                  [(w_gu, _spec((None, None, d, f8), lambda i, j, k: (j, 0, 0, 0))),
                   (w_gu, _spec((None, None, d, f8), lambda i, j, k: (j, 1, 0, 0)))],
                  [], [((NDEV, lp, f8), BF16, hid)] * 3, dims=NN, epi=_swiglu_epi, after=after, name=f"{tag}_up")
    return n, a, b, s


def _ffn_down(h, s, w_down, *, tm, tag):
    lp, d = h.shape
    f8 = w_down.shape[1]
    tn = _pick(d, [1024, 512, 256, 128])
    tile = _spec((tm, tn), lambda i, j, k: (i, j))
    return _mm((lp // tm, d // tn, NDEV // KSUB), s, _spec((KSUB, tm, f8), lambda i, j, k: (k, i, 0)),
               [(w_down, _spec((KSUB, f8, tn), lambda i, j, k: (k, 0, j)))], [(h, tile)], [((lp, d), F32, tile)],
               dims=NN, acc_shape=(tm, tn), epi=lambda accs, ex: [ex[0] + FFN_RESIDUAL_WEIGHT * accs[0]], name=f"{tag}_down")[0]


def _ffn_bwd_dw(dh_half, saved, w_down, on_down, *, tm, tag):
    n, a, b, s = saved
    lp, d = dh_half.shape
    f8 = w_down.shape[1]
    tkr = _pick(lp, [2176, 1088, 544, 256, 128])
    tn = _pick(d, [1024, 512, 256, 128])
    hid = _spec((None, tm, f8), lambda i, j, k: (j, i, 0))
    dab, = _mm((lp // tm, NDEV, 1), dh_half, _spec((tm, d), lambda i, j, k: (i, 0)),
               [(w_down, _spec((None, f8, d), lambda i, j, k: (j, 0, 0)))], [(a, hid), (b, hid)],
               [((2, NDEV, lp, f8), BF16, _spec((2, None, tm, f8), lambda i, j, k: (0, j, i, 0)))],
               dims=NT, epi=_swiglu_bwd_epi, name=f"{tag}_down_dx")
    dw_down, = _mm((NDEV, d // tn, lp // tkr), s, _spec((None, tkr, f8), lambda i, j, k: (i, k, 0)),
                   [(dh_half, _spec((tkr, tn), lambda i, j, k: (k, j)))], [],
                   [((NDEV, f8, d), BF16, _spec((None, f8, tn), lambda i, j, k: (i, 0, j)))],
                   dims=TN, acc_shape=(f8, tn), name=f"{tag}_down_dw")
    dw_gu, = _mm((d // tn, 2 * NDEV, lp // tkr), n, _spec((tkr, tn), lambda i, j, k: (k, i)),
                 [(dab, _spec((None, None, tkr, f8), lambda i, j, k: (j // NDEV, j % NDEV, k, 0)))], [],
                 [((NDEV, 2, d, f8), BF16, _spec((None, None, tn, f8), lambda i, j, k: (j % NDEV, j // NDEV, i, 0)))],
                 dims=TN, acc_shape=(tn, f8), after=on_down(dw_down), name=f"{tag}_gate_up_dw")
    return dab, dw_gu, dw_down


def _ffn_bwd_dx(dab, w_gu, after, *, tm, tag):
    lp, f8 = dab.shape[2], dab.shape[3]
    d = w_gu.shape[2]
    tn = _pick(d, [1024, 512, 256, 128])
    nsub = NDEV // KSUB
    return _mm((lp // tm, d // tn, 2 * nsub), dab, _spec((None, KSUB, tm, f8), lambda i, j, k: (k // nsub, k % nsub, i, 0)),
               [(w_gu, _spec((KSUB, None, tn, f8), lambda i, j, k: (k % nsub, k // nsub, j, 0)))], [],
               [((lp, d), F32, _spec((tm, tn), lambda i, j, k: (i, j)))],
               dims=NT, acc_shape=(tm, tn), after=after, name=f"{tag}_gate_up_dx")[0]


def kernel(x, meta_tokens, ffn1_norm, ffn1_w_gate, ffn1_w_up, ffn1_w_down, mix_norm, w_in, b_forget, fox_q_norm, fox_k_norm, w_branch_fox, w_branch_sb, w_out, ffn2_norm, ffn2_w_gate, ffn2_w_up, ffn2_w_down, loss_target, m_meta_tokens, m_ffn1_norm, m_ffn1_w_gate, m_ffn1_w_up, m_ffn1_w_down, m_mix_norm, m_w_in, m_b_forget, m_fox_q_norm, m_fox_k_norm, m_w_branch_fox, m_w_branch_sb, m_w_out, m_ffn2_norm, m_ffn2_w_gate, m_ffn2_w_up, m_ffn2_w_down, v_meta_tokens, v_ffn1_norm, v_ffn1_w_gate, v_ffn1_w_up, v_ffn1_w_down, v_mix_norm, v_w_in, v_b_forget, v_fox_q_norm, v_fox_k_norm, v_w_branch_fox, v_w_branch_sb, v_w_out, v_ffn2_norm, v_ffn2_w_gate, v_ffn2_w_up, v_ffn2_w_down):
    seq, d = x.shape[1], x.shape[2]
    d8 = d // NDEV
    w = BRANCH_WIDTH
    tile = 256 if seq % 256 == 0 else 128
    pad = tile - N_META
    lp = tile + seq
    tm = _pick(lp, [1088, 544, 256, 128])
    tr = _pick(tile, [256, 128])
    tkr = _pick(lp, [2176, 1088, 544, 256, 128])
    nb = lp // tile
    me = _flat_id(*_my_place())

    shards = [jnp.stack([ffn1_w_gate[0], ffn1_w_up[0]]).astype(BF16), ffn1_w_down[0].astype(BF16), w_in[0].astype(BF16),
              w_branch_fox[0].astype(BF16), w_branch_sb[0].astype(BF16), w_out[0].astype(BF16),
              jnp.stack([ffn2_w_gate[0], ffn2_w_up[0]]).astype(BF16), ffn2_w_down[0].astype(BF16)]
    w_gu1, = _allgather(shards[:1], "gather_ffn1")
    down1_copies, token = _send_start(shards[1:2], [_landing(s, me) for s in shards[1:2]], "first", w_gu1, "gather_down1_start")
    w_in_copies, token = _send_start(shards[2:3], [_landing(s, me) for s in shards[2:3]], "first", token, "gather_w_in_start")
    mixer_copies, token = _send_start(shards[3:6], [_landing(s, me) for s in shards[3:6]], "gather", token, "gather_mixer_start")
    ffn2_copies, token = _send_start(shards[6:], [_landing(s, me) for s in shards[6:]], "gather", token, "gather_ffn2_start")
    meta_full = _join_cols(_allgather([meta_tokens], "gather_meta", in_vmem=True)[0])

    h0 = jnp.concatenate([jnp.zeros((pad, d), F32), meta_full.astype(F32), x[0]], axis=0)
    saved1 = _ffn_up(h0, ffn1_norm, w_gu1, token, tm=tm, tag="ffn1")
    w_down1, = _gather_two_level(down1_copies, saved1[3], "gather_down1")
    h1 = _ffn_down(h0, saved1[3], w_down1, tm=tm, tag="ffn1")

    w_in_blocks, = _gather_two_level(w_in_copies, h1, "gather_w_in")
    wi = _join_cols(w_in_blocks)
    w_pa = jnp.concatenate([wi[:, :2 * w], jnp.pad(wi[:, 3 * w:3 * w + HEADS], ((0, 0), (0, LANES - HEADS)))], axis=1)
    w_pb = jnp.concatenate([wi[:, 2 * w:3 * w], wi[:, 3 * w + HEADS:]], axis=1)
    na, nbw = w_pa.shape[1], w_pb.shape[1]
    gate_blk = 4 * w // d

    n2 = _norm_fwd(h1, mix_norm, tm=tr, name="mix_norm")
    tma = _pick(lp, [544, 256, 128])
    tnd = _pick(d, [1024, 512, 256, 128])
    tnb = _pick(nbw, [512, 256, 128])
    proj_a = _mm2d(n2, w_pa, na, dims=NN, tm=tma, tn=na, tk=d, out_dtype=F32, name="proj_a")
    proj_b = _mm2d(n2, w_pb, nbw, dims=NN, tm=tm, tn=tnb, tk=d, out_dtype=BF16, name="proj_b")
    b_pad = jnp.pad(b_forget, ((0, 0), (0, LANES - HEADS)))
    q_gain, k_gain = fox_q_norm.reshape(1, w), fox_k_norm.reshape(1, w)
    fq, fk, logf = _fox_prep(proj_a, q_gain, k_gain, b_pad, tm=tr, pad=pad, name="fox_prep")
    c = _cumsum_rows(logf, tile=tile, reverse=False, name="forget_cumsum")
    c_rows = jnp.transpose(c[:, :HEADS]).reshape(HEADS, nb, 1, tile)
    o_fox, lse = _fox_fwd(fq, fk, proj_b, 0, c_rows, tile=tile, pad=pad, name="fox_fwd")
    o_sb, sb_total = _sb_fwd(proj_b, HEADS, 2 * HEADS, 3 * HEADS, tile=tile, pad=pad, name="sb_fwd")
    w_br_fox, w_br_sb, w_out_blocks = _send_wait(mixer_copies, o_sb, "gather_mixer_wait")
    w_out_full = w_out_blocks.reshape(d, d)

    def branch(o, w_blocks, name):
        return _mm((lp // tm, NDEV, 1), o, _spec((tm, w), lambda i, j, k: (i, 0)),
                   [(w_blocks, _spec((None, w, d8), lambda i, j, k: (j, 0, 0)))], [],
                   [((lp, d), BF16, _spec((tm, d8), lambda i, j, k: (i, j)))], dims=NN, name=name)[0]

    br_fox = branch(o_fox, w_br_fox, "branch_fox")
    br_sb = branch(o_sb, w_br_sb, "branch_sb")

    def merge_fn(i, tiles, consts):
        bf_, bs_, gf_, gs_ = [t.astype(F32) for t in tiles]
        return [_sigmoid(gf_) * bf_ + _sigmoid(gs_) * bs_], []

    gates_in = [(proj_b, d, gate_blk), (proj_b, d, gate_blk + 1)]
    merged, = _rowwise(merge_fn, [_whole(br_fox), _whole(br_sb)] + gates_in, [], [(d, BF16)], [], tm=tr, name="merge")
    h2 = _mm2d(merged, w_out_full, d, dims=NN, tm=tm, tn=tnd, tk=d, out_dtype=F32,
               epi=lambda accs, ex: [ex[0] + accs[0]], extras=[h1], name="out_proj")

    w_gu2, w_down2 = _send_wait(ffn2_copies, h2, "gather_ffn2_wait")
    saved3 = _ffn_up(h2, ffn2_norm, w_gu2, None, tm=tm, tag="ffn2")
    h3 = _ffn_down(h2, saved3[3], w_down2, tm=tm, tag="ffn2")

    skip = tile // tr

    def loss_fn(i, tiles, consts):
        real = i >= skip
        err = jnp.where(real, tiles[0] - tiles[1], 0.0)
        dy = err * (1.0 / d)
        part = 0.5 * jnp.sum(err * dy, axis=0, keepdims=True)
        return [dy, FFN_RESIDUAL_WEIGHT * dy], [part]

    dh3, dh3_half, loss_cols = _rowwise(
        loss_fn, [_whole(h3), _whole(loss_target[0])], [], [(d, F32, lp), (d, BF16, lp)], [(1, d)], tm=tr, name="loss",
        row_maps=[None, lambda i: jnp.maximum(i - skip, 0)])

    def own(g):
        return lax.dynamic_index_in_dim(g, me, 0, keepdims=False)

    def send_grads(grads_, name):
        return _send_start(grads_, [_landing(own(g), me) for g in grads_], "exchange", grads_[-1], name)

    sends = {}

    def send_piece(key):
        def on_ready(*gs):
            sends[key], token_ = send_grads(list(gs), f"exchange_{key}_start")
            return token_
        return on_ready

    dab3, dw_gu2, _ = _ffn_bwd_dw(dh3_half, saved3, w_down2, send_piece("down2"), tm=tm, tag="ffn2")
    dn3 = _ffn_bwd_dx(dab3, w_gu2, send_piece("gu2")(dw_gu2), tm=tm, tag="ffn2")
    dh2, dh2_bf, _, dg_ffn2 = _norm_bwd(dn3, h2, ffn2_norm, dh3, tm=tr, name="ffn2_norm_bwd")

    dmerged = _mm2d(dh2_bf, w_out_full, d, dims=NT, tm=tm, tn=tnd, tk=d, out_dtype=BF16, name="out_proj_dx")
    dw_out = _mm2d(merged, dh2_bf, d, dims=TN, tm=tnd, tn=tnd, tk=tkr, out_dtype=BF16, name="out_proj_dw")
    token = send_piece("out")(dw_out.reshape(NDEV, d8, d))

    def merge_bwd_fn(i, tiles, consts):
        dm, bf_, bs_, gf_, gs_ = [t.astype(F32) for t in tiles]
        sf, ss = _sigmoid(gf_), _sigmoid(gs_)
        return [dm * sf, dm * ss, dm * bf_ * sf * (1.0 - sf), dm * bs_ * ss * (1.0 - ss)], []

    dbr_fox, dbr_sb, dg_fox, dg_sb = _rowwise(
        merge_bwd_fn, [_whole(dmerged), _whole(br_fox), _whole(br_sb)] + gates_in, [], [(d, BF16)] * 4, [], tm=tr, after=token,
        name="merge_bwd")

    tnw = _pick(w, [512, 256, 128])

    def branch_dx(dbr, w_blocks, after, name):
        return _mm((lp // tm, w // tnw, NDEV), dbr, _spec((tm, d8), lambda i, j, k: (i, k)),
                   [(w_blocks, _spec((None, tnw, d8), lambda i, j, k: (k, j, 0)))], [],
                   [((lp, w), BF16, _spec((tm, tnw), lambda i, j, k: (i, j)))], dims=NT, acc_shape=(tm, tnw), after=after,
                   name=name)[0]

    def branch_dw(o, dbr, name):
        return _mm((w // tnw, NDEV, lp // tkr), o, _spec((tkr, tnw), lambda i, j, k: (k, i)),
                   [(dbr, _spec((tkr, d8), lambda i, j, k: (k, j)))], [],
                   [((NDEV, w, d8), BF16, _spec((None, tnw, d8), lambda i, j, k: (j, i, 0)))],
                   dims=TN, acc_shape=(tnw, d8), name=name)[0]

    token = send_piece("branch")(branch_dw(o_fox, dbr_fox, "branch_fox_dw"), branch_dw(o_sb, dbr_sb, "branch_sb_dw"))
    do_fox = branch_dx(dbr_fox, w_br_fox, token, "branch_fox_dx")
    do_sb = branch_dx(dbr_sb, w_br_sb, token, "branch_sb_dx")

    dfq, dfk, dfv, dc_rows = _fox_bwd(fq, fk, proj_b, 0, c_rows, o_fox, do_fox, lse, tile=tile, pad=pad, name="fox_bwd")
    dsq, dsk, dsv = _sb_bwd(proj_b, HEADS, 2 * HEADS, 3 * HEADS, do_sb, sb_total, tile=tile, pad=pad, name="sb_bwd")
    dc = jnp.pad(jnp.transpose(dc_rows.reshape(HEADS, lp)), ((0, 0), (0, LANES - HEADS)))
    dlogf = _cumsum_rows(dc, tile=tile, reverse=True, name="forget_cumsum_bwd")
    dproj_a, dg_q, dg_k, dg_b = _fox_prep_bwd(proj_a, dfq, dfk, dlogf, q_gain, k_gain, b_pad, tm=tr, pad=pad, name="fox_prep_bwd")
    dproj_b = jnp.concatenate([dfv, dsq, dsk, dsv, dg_fox, dg_sb], axis=1)

    dw_pa = _mm2d(n2, dproj_a, na, dims=TN, tm=tnd, tn=na, tk=_pick(lp, [544, 256, 128]), out_dtype=BF16, name="proj_a_dw")
    dw_pb = _mm2d(n2, dproj_b, nbw, dims=TN, tm=tnd, tn=tnb, tk=tkr, out_dtype=BF16, name="proj_b_dw")
    dw_in = jnp.concatenate([dw_pa[:, :2 * w], dw_pb[:, :w], dw_pa[:, 2 * w:2 * w + HEADS], dw_pb[:, w:]], axis=1)
    token = send_piece("w_in")(_split_cols(dw_in))
    dn2_a = _mm2d(dproj_a, w_pa, d, dims=NT, tm=tm, tn=tnd, tk=na, out_dtype=F32, after=token, name="proj_a_dx")
    dn2 = _mm2d(dproj_b, w_pb, d, dims=NT, tm=tm, tn=tnd, tk=_pick(nbw, [2048, 1024, 512, 256, 128]), out_dtype=F32,
                epi=lambda accs, ex: [accs[0] + ex[0]], extras=[dn2_a], name="proj_b_dx")
    dh1, _, dh1_half, dg_mix = _norm_bwd(dn2, h1, mix_norm, dh2, tm=tr, name="mix_norm_bwd")

    dab1, dw_gu1, _ = _ffn_bwd_dw(dh1_half, saved1, w_down1, send_piece("down1"), tm=tm, tag="ffn1")
    dn1 = _ffn_bwd_dx(dab1, w_gu1, send_piece("gu1")(dw_gu1), tm=tm, tag="ffn1")
    dh0, _, _, dg_ffn1 = _norm_bwd(dn1, h0, ffn1_norm, dh1, tm=tr, name="ffn1_norm_bwd")
    grad_x = dh0[tile:][None]

    r_down2, = _send_wait(sends["down2"], dh0, "exchange_down2_wait")
    r_gu2, = _send_wait(sends["gu2"], r_down2, "exchange_gu2_wait")
    r_out, = _send_wait(sends["out"], r_gu2, "exchange_out_wait")
    r_br_fox, r_br_sb = _send_wait(sends["branch"], r_out, "exchange_branch_wait")
    r_in, = _send_wait(sends["w_in"], r_br_sb, "exchange_w_in_wait")
    grads, deltas, new_ms, new_vs = {}, {}, {}, {}

    def adamw_big(entries):
        for k, (parts, sel, wt, mt, vt) in entries.items():
            g, dl, mn, vn = _adamw_summed(parts, sel, wt[0], mt[0], vt[0], name=f"adamw_{k}")
            grads[k], deltas[k], new_ms[k], new_vs[k] = g[None], dl[None], mn[None], vn[None]

    adamw_big(dict(ffn2_w_gate=(r_gu2, 0, ffn2_w_gate, m_ffn2_w_gate, v_ffn2_w_gate),
                   ffn2_w_up=(r_gu2, 1, ffn2_w_up, m_ffn2_w_up, v_ffn2_w_up),
                   ffn2_w_down=(r_down2, 0, ffn2_w_down, m_ffn2_w_down, v_ffn2_w_down),
                   w_in=(r_in, 0, w_in, m_w_in, v_w_in),
                   w_branch_fox=(r_br_fox, 0, w_branch_fox, m_w_branch_fox, v_w_branch_fox),
                   w_branch_sb=(r_br_sb, 0, w_branch_sb, m_w_branch_sb, v_w_branch_sb),
                   w_out=(r_out, 0, w_out, m_w_out, v_w_out)))
    done = sum(v[0, 0, :1] for v in new_vs.values())
    r_down1, = _send_wait(sends["down1"], done, "exchange_down1_wait")
    r_gu1, = _send_wait(sends["gu1"], r_down1, "exchange_gu1_wait")
    adamw_big(dict(ffn1_w_gate=(r_gu1, 0, ffn1_w_gate, m_ffn1_w_gate, v_ffn1_w_gate),
                   ffn1_w_up=(r_gu1, 1, ffn1_w_up, m_ffn1_w_up, v_ffn1_w_up),
                   ffn1_w_down=(r_down1, 0, ffn1_w_down, m_ffn1_w_down, v_ffn1_w_down)))

    small_parts = [dh0[pad:tile], dg_ffn1, dg_mix, dg_ffn2, dg_b[:, :HEADS], dg_q, dg_k, loss_cols]
    small_packed, small_spans = _pack_rows(small_parts, LANES, SMALL_ROWS)
    small_sum = _allsum_small(small_packed, "sum_small")
    g_meta_full, g_ffn1n, g_mixn, g_ffn2n, g_bf, g_qn, g_kn, loss_vec = [
        _unpack(small_sum, span, part.shape) for span, part in zip(small_spans, small_parts)]
    loss = jnp.sum(loss_vec)
    g_meta = lax.dynamic_slice_in_dim(g_meta_full, me * d8, d8, axis=1)
    g_qn, g_kn = g_qn.reshape(fox_q_norm.shape), g_kn.reshape(fox_k_norm.shape)

    small = dict(meta_tokens=(g_meta, meta_tokens, m_meta_tokens, v_meta_tokens),
                 ffn1_norm=(g_ffn1n, ffn1_norm, m_ffn1_norm, v_ffn1_norm),
                 mix_norm=(g_mixn, mix_norm, m_mix_norm, v_mix_norm),
                 b_forget=(g_bf, b_forget, m_b_forget, v_b_forget),
                 fox_q_norm=(g_qn, fox_q_norm, m_fox_q_norm, v_fox_q_norm),
                 fox_k_norm=(g_kn, fox_k_norm, m_fox_k_norm, v_fox_k_norm),
                 ffn2_norm=(g_ffn2n, ffn2_norm, m_ffn2_norm, v_ffn2_norm))
    for k, (g, wt, mt, vt) in small.items():
        flat = lambda t: t.reshape(-1, t.shape[-1])
        dl, mn, vn = _adamw_plain(flat(g), flat(wt), flat(mt), flat(vt), name=f"adamw_{k}")
        grads[k], deltas[k], new_ms[k], new_vs[k] = g, dl.reshape(wt.shape), mn.reshape(wt.shape), vn.reshape(wt.shape)

    order = ["meta_tokens", "ffn1_norm", "ffn1_w_gate", "ffn1_w_up", "ffn1_w_down", "mix_norm", "w_in", "b_forget",
             "fox_q_norm", "fox_k_norm", "w_branch_fox", "w_branch_sb", "w_out", "ffn2_norm", "ffn2_w_gate",
             "ffn2_w_up", "ffn2_w_down"]
    return (loss, grad_x, *[grads[k] for k in order], *[deltas[k] for k in order],
            *[new_ms[k] for k in order], *[new_vs[k] for k in order])
```

```python
import jax
import jax.numpy as jnp
from jax import lax
from jax.experimental import pallas as pl
from jax.experimental.pallas import tpu as pltpu

F32 = jnp.float32
BF16 = jnp.bfloat16
MESH = pl.DeviceIdType.MESH

NDEV = 8
N_META = 16
HEAD_DIM = 128
HEADS = 8
BRANCH_WIDTH = HEADS * HEAD_DIM
RMS_EPS = 1e-6
FFN_RESIDUAL_WEIGHT = 0.5
ATTN_SCALE = HEAD_DIM ** -0.5
MASKED_LOGIT = -1e30

ADAM_LR = 0.001
ADAM_B1 = 0.9
ADAM_B2 = 0.999
ADAM_EPS = 1e-08
ADAM_WD = 0.01
ADAM_STEP = 10

LANES = 128
SMALL_ROWS = 8
ADAMW_TILE_ELEMS = 160 * 1024
KSUB = 2


def _pick(n, prefs):
    for p in prefs:
        if p <= n and n % p == 0:
            return p
    return n


def _dot(a, b, dims):
    return lax.dot_general(a, b, (dims, ((), ())), preferred_element_type=F32)


NN = ((1,), (0,))
TN = ((0,), (0,))
NT = ((1,), (1,))


def _split_bf16(x):
    hi = x.astype(BF16)
    lo = (x - hi.astype(F32)).astype(BF16)
    return hi, lo


def _sigmoid_parts(z):
    e = jnp.exp(-jnp.abs(z))
    t = 1.0 + e
    return e, 1.0 / t, jnp.log(t)


def _sigmoid(x):
    return 0.5 * jnp.tanh(0.5 * x) + 0.5


def _my_place():
    return lax.axis_index("x"), lax.axis_index("y"), lax.axis_index("c")


def _flat_id(px, py, pc):
    return 4 * px + 2 * py + pc


def _peer(x, y, c, k):
    px = 1 - x if k & 4 else x
    py = 1 - y if k & 2 else y
    pc = 1 - c if k & 1 else c
    return px, py, pc


def _allgather(shards, name, in_vmem=False):
    n = len(shards)

    def body(*refs):
        x_refs, out_refs = refs[:n], refs[n:2 * n]
        send_sems, recv_sems, local_sems = refs[2 * n:]
        x, y, c = _my_place()
        me, sibling = (x, y, c), (x, y, 1 - c)
        chips = [(1 - x, y), (x, 1 - y), (1 - x, 1 - y)]

        def block(a, place):
            return out_refs[a].at[_flat_id(*place)]

        def copy(a, k, place, to, src=None):
            return pltpu.make_async_remote_copy(
                src_ref=block(a, place) if src is None else src, dst_ref=block(a, place),
                send_sem=send_sems.at[7 * a + k], recv_sem=recv_sems.at[7 * a + k], device_id=to, device_id_type=MESH)

        mine = [pltpu.make_async_copy(x_refs[a], block(a, me), local_sems.at[a]) for a in range(n)]
        for cp in mine:
            cp.start()
        first = []
        for a in range(n):
            first.append(copy(a, 0, me, sibling, src=x_refs[a]))
            first += [copy(a, 1 + j, me, (*chip, c), src=x_refs[a]) for j, chip in enumerate(chips)]
        for cp in first:
            cp.start()
        passed = []
        for j, chip in enumerate(chips):
            for a in range(n):
                copy(a, 1 + j, (*chip, c), me).wait_recv()
                passed.append(copy(a, 4 + j, (*chip, c), sibling))
                passed[-1].start()
        for a in range(n):
            copy(a, 0, sibling, me).wait_recv()
        for j, chip in enumerate(chips):
            for a in range(n):
                copy(a, 4 + j, (*chip, 1 - c), me).wait_recv()
        for cp in first + passed:
            cp.wait_send()
        for cp in mine:
            cp.wait()

    space = pltpu.VMEM if in_vmem else pl.ANY
    return pl.pallas_call(
        body, name=name,
        out_shape=[jax.ShapeDtypeStruct((NDEV,) + s.shape, s.dtype) for s in shards],
        in_specs=[pl.BlockSpec(memory_space=space)] * n,
        out_specs=[pl.BlockSpec(memory_space=space)] * n,
        scratch_shapes=[pltpu.SemaphoreType.DMA((7 * n,)), pltpu.SemaphoreType.DMA((7 * n,)), pltpu.SemaphoreType.DMA((n,))],
    )(*shards)


def _allsum_small(part, name):
    rows, cols = part.shape

    def body(p_ref, out_ref, buf, send_sems, recv_sems):
        x, y, c = _my_place()
        me = _flat_id(x, y, c)
        buf[me] = p_ref[...]
        copies = []
        for k in range(1, NDEV):
            copies.append(pltpu.make_async_remote_copy(
                src_ref=p_ref, dst_ref=buf.at[me], send_sem=send_sems.at[k - 1], recv_sem=recv_sems.at[k - 1],
                device_id=_peer(x, y, c, k), device_id_type=MESH))
        for cp in copies:
            cp.start()
        for cp in copies:
            cp.wait()
        total = buf[0]
        for j in range(1, NDEV):
            total = total + buf[j]
        out_ref[...] = total

    return pl.pallas_call(
        body, name=name,
        out_shape=jax.ShapeDtypeStruct((rows, cols), F32),
        in_specs=[pl.BlockSpec(memory_space=pltpu.VMEM)],
        out_specs=pl.BlockSpec(memory_space=pltpu.VMEM),
        scratch_shapes=[pltpu.VMEM((NDEV, rows, cols), F32),
                        pltpu.SemaphoreType.DMA((7,)), pltpu.SemaphoreType.DMA((7,))],
    )(part)


_HBM = pl.BlockSpec(memory_space=pltpu.HBM)
_SEM = pl.BlockSpec(memory_space=pltpu.SEMAPHORE)
_DATAFLOW = pltpu.SideEffectType.DATAFLOW_SIDE_EFFECTING


COPIES_PER_ARRAY = {"gather": 7, "exchange": 7, "first": 4, "forward": 3}


def _send_copies(plan, src_refs, land_refs, send_sems, recv_sems):
    x, y, c = _my_place()
    me = _flat_id(x, y, c)
    per = COPIES_PER_ARRAY[plan]
    copies = []

    def add(a, slot, src, dst, to):
        copies.append(pltpu.make_async_remote_copy(
            src_ref=src, dst_ref=dst, send_sem=send_sems.at[per * a + slot], recv_sem=recv_sems.at[per * a + slot],
            device_id=to, device_id_type=MESH))

    for a, land_ref in enumerate(land_refs):
        if plan in ("gather", "exchange"):
            for k in range(1, NDEV):
                peer = _peer(x, y, c, k)
                add(a, k - 1, src_refs[a].at[_flat_id(*peer)] if plan == "exchange" else src_refs[a], land_ref.at[me], peer)
        elif plan == "first":
            for slot, k in enumerate((1, 2, 4, 6)):
                add(a, slot, src_refs[a], land_ref.at[me], _peer(x, y, c, k))
        else:
            for slot, k in enumerate((2, 4, 6)):
                block = land_ref.at[_flat_id(*_peer(x, y, c, k))]
                add(a, slot, block, block, (x, y, 1 - c))
    return copies


def _send_start(srcs, lands, plan, after, name):
    ns, nl = len(srcs), len(lands)
    nsem = COPIES_PER_ARRAY[plan] * nl

    def body(*refs):
        for cp in _send_copies(plan, refs[:ns], refs[ns:ns + nl], refs[ns + nl + 1], refs[ns + nl + 2]):
            cp.start()
        refs[-1][...] = jnp.zeros_like(refs[-1])

    operands = [pltpu.with_memory_space_constraint(t, pltpu.HBM) for t in list(srcs) + list(lands)]
    outs = pl.pallas_call(
        body, name=name,
        out_shape=(pltpu.SemaphoreType.DMA((nsem,)), pltpu.SemaphoreType.DMA((nsem,)),
                   *[pltpu.HBM(t.shape, t.dtype) for t in operands[ns:]], jax.ShapeDtypeStruct((SMALL_ROWS, LANES), F32)),
        in_specs=[_HBM] * (ns + nl) + [pl.BlockSpec(memory_space=pl.ANY)],
        out_specs=(_SEM, _SEM, *[_HBM] * nl, pl.BlockSpec(memory_space=pltpu.VMEM)),
        input_output_aliases={ns + i: 2 + i for i in range(nl)},
        compiler_params=pltpu.CompilerParams(has_side_effects=_DATAFLOW),
    )(*operands, after)
    return (plan, outs[0], outs[1], operands[:ns], list(outs[2:2 + nl])), outs[-1]


def _send_wait(handle, after, name):
    plan, send_sems, recv_sems, srcs, lands = handle
    ns, nl = len(srcs), len(lands)

    def body(*refs):
        for cp in _send_copies(plan, refs[:ns], refs[ns:ns + nl], refs[ns + nl], refs[ns + nl + 1]):
            cp.wait_send()
            cp.wait_recv()

    outs = pl.pallas_call(
        body, name=name,
        out_shape=tuple(pltpu.HBM(t.shape, t.dtype) for t in lands),
        in_specs=[_HBM] * (ns + nl) + [_SEM, _SEM, pl.BlockSpec(memory_space=pl.ANY)],
        out_specs=tuple([_HBM] * nl),
        input_output_aliases={ns + i: i for i in range(nl)},
        compiler_params=pltpu.CompilerParams(has_side_effects=_DATAFLOW),
    )(*srcs, *lands, send_sems, recv_sems, after)
    return list(outs)


def _gather_two_level(first_handle, after, name):
    lands = _send_wait(first_handle, after, f"{name}_wait")
    forward, _ = _send_start([], lands, "forward", after, f"{name}_forward_start")
    return _send_wait(forward, after, f"{name}_forward_wait")


def _landing(own_block, me):
    return lax.dynamic_update_index_in_dim(lax.empty((NDEV,) + own_block.shape, own_block.dtype), own_block, me, 0)


def _spec(block, index_map):
    return pl.BlockSpec(block, index_map)


def _mm(grid, a, a_spec, bs, extras, outs, *, dims, acc_shape=None, epi=None, after=None, name):
    nk = grid[2]
    nb, ne, no = len(bs), len(extras), len(outs)
    nafter = 0 if after is None else 1
    if epi is None:
        epi = lambda accs, ex: [accs[0]]

    def body(*refs):
        a_ref, b_refs = refs[0], refs[1:1 + nb]
        e_refs = refs[1 + nb:1 + nb + ne]
        o_refs = refs[1 + nb + ne + nafter:1 + nb + ne + nafter + no]
        acc_refs = refs[1 + nb + ne + nafter + no:]
        def finish(accs):
            for o_ref, tile in zip(o_refs, epi(accs, [e_ref[...] for e_ref in e_refs])):
                o_ref[...] = tile.astype(o_ref.dtype)

        def product(b_ref):
            if len(a_ref.shape) == 2:
                return _dot(a_ref[...], b_ref[...], dims)
            total = _dot(a_ref[0], b_ref[0], dims)
            for t in range(1, a_ref.shape[0]):
                total = total + _dot(a_ref[t], b_ref[t], dims)
            return total

        if nk == 1:
            finish([product(b_ref) for b_ref in b_refs])
        else:
            k = pl.program_id(2)

            @pl.when(k == 0)
            def _():
                for acc_ref in acc_refs:
                    acc_ref[...] = jnp.zeros_like(acc_ref)

            for acc_ref, b_ref in zip(acc_refs, b_refs):
                acc_ref[...] += product(b_ref)

            @pl.when(k == nk - 1)
            def _():
                finish([acc_ref[...] for acc_ref in acc_refs])

    return pl.pallas_call(
        body, name=name,
        grid=grid,
        in_specs=[a_spec] + [s for _, s in bs] + [s for _, s in extras] + [pl.BlockSpec(memory_space=pl.ANY)] * nafter,
        out_specs=[s for _, _, s in outs],
        out_shape=[jax.ShapeDtypeStruct(shape, dt) for shape, dt, _ in outs],
        scratch_shapes=[pltpu.VMEM(acc_shape, F32) for _ in bs] if nk > 1 else [],
        compiler_params=pltpu.CompilerParams(dimension_semantics=("parallel", "parallel", "arbitrary")),
    )(a, *[b for b, _ in bs], *[e for e, _ in extras], *([after] if nafter else []))


def _mm2d(a, b, n_cols, *, dims, tm, tn, tk, out_dtype, epi=None, extras=(), after=None, name):
    m_rows, k_len = (a.shape[1], a.shape[0]) if dims == TN else a.shape
    assert m_rows % tm == 0 and n_cols % tn == 0 and k_len % tk == 0, (name, a.shape, n_cols, tm, tn, tk)
    a_spec = _spec((tk, tm), lambda i, j, k: (k, i)) if dims == TN else _spec((tm, tk), lambda i, j, k: (i, k))
    b_spec = _spec((tn, tk), lambda i, j, k: (j, k)) if dims == NT else _spec((tk, tn), lambda i, j, k: (k, j))
    tile = _spec((tm, tn), lambda i, j, k: (i, j))
    return _mm((m_rows // tm, n_cols // tn, k_len // tk), a, a_spec, [(b, b_spec)], [(e, tile) for e in extras],
               [((m_rows, n_cols), out_dtype, tile)], dims=dims, acc_shape=(tm, tn), epi=epi, after=after, name=name)[0]


def _rowwise(fn, ins, consts, outs, sums, *, tm, name, row_maps=None, after=None):
    m_rows = outs[0][2] if len(outs[0]) == 3 else ins[0][0].shape[0]
    n = m_rows // tm
    ni, nc, no = len(ins), len(consts), len(outs)
    nafter = 0 if after is None else 1
    row_maps = row_maps or [None] * ni

    def body(*refs):
        i = pl.program_id(0)
        in_tiles = [r[...] for r in refs[:ni]]
        const_values = [r[...] for r in refs[ni:ni + nc]]
        o_refs = refs[ni + nc + nafter:ni + nc + nafter + no]
        s_refs = refs[ni + nc + nafter + no:]
        out_tiles, sum_terms = fn(i, in_tiles, const_values)
        for o_ref, tile in zip(o_refs, out_tiles):
            o_ref[...] = tile.astype(o_ref.dtype)
        if s_refs:
            @pl.when(i == 0)
            def _():
                for s_ref in s_refs:
                    s_ref[...] = jnp.zeros_like(s_ref)

            for s_ref, term in zip(s_refs, sum_terms):
                s_ref[...] += term

    def in_spec(width, col, rmap):
        if rmap is None:
            return pl.BlockSpec((tm, width), lambda i: (i, col))
        return pl.BlockSpec((tm, width), lambda i: (rmap(i), col))

    return pl.pallas_call(
        body, name=name,
        grid=(n,),
        in_specs=[in_spec(w, col, rmap) for (_, w, col), rmap in zip(ins, row_maps)]
        + [pl.BlockSpec(cst.shape, lambda i: (0, 0)) for cst in consts] + [pl.BlockSpec(memory_space=pl.ANY)] * nafter,
        out_specs=[pl.BlockSpec((tm, o[0]), lambda i: (i, 0)) for o in outs]
        + [pl.BlockSpec(s, lambda i: (0, 0)) for s in sums],
        out_shape=[jax.ShapeDtypeStruct((m_rows, o[0]), o[1]) for o in outs]
        + [jax.ShapeDtypeStruct(s, F32) for s in sums],
        compiler_params=pltpu.CompilerParams(dimension_semantics=("arbitrary",)),
    )(*[arr for arr, _, _ in ins], *consts, *([after] if nafter else []))


def _whole(arr):
    return (arr, arr.shape[1], 0)


def _rms(x, gain):
    r = lax.rsqrt(jnp.mean(x * x, axis=-1, keepdims=True) + RMS_EPS)
    return x * r * gain


def _rms_bwd(x, gain, dy):
    r = lax.rsqrt(jnp.mean(x * x, axis=-1, keepdims=True) + RMS_EPS)
    u = dy * gain
    dx = r * u - x * (r * r * r) * jnp.mean(x * u, axis=-1, keepdims=True)
    return dx, dy * x * r


def _norm_fwd(h, gain, *, tm, name):
    def fn(i, tiles, consts):
        return [_rms(tiles[0], consts[0])], []
    return _rowwise(fn, [_whole(h)], [gain], [(h.shape[1], BF16)], [], tm=tm, name=name)[0]


def _norm_bwd(dn, h, gain, dh_in, *, tm, name, after=None):
    d = h.shape[1]

    def fn(i, tiles, consts):
        dx, dg_rows = _rms_bwd(tiles[1], consts[0], tiles[0])
        dh = tiles[2] + dx
        return [dh, dh, FFN_RESIDUAL_WEIGHT * dh], [jnp.sum(dg_rows, axis=0, keepdims=True)]

    return _rowwise(fn, [_whole(dn), _whole(h), _whole(dh_in)], [gain],
                    [(d, F32), (d, BF16), (d, BF16)], [(1, d)], tm=tm, name=name, after=after)


def _swiglu_epi(accs, ex):
    a, b = accs
    return [a, b, a * _sigmoid(a) * b]


def _swiglu_bwd_epi(accs, ex):
    ds = accs[0]
    a, b = ex[0].astype(F32), ex[1].astype(F32)
    sig = _sigmoid(a)
    silu = a * sig
    dsilu = sig * (1.0 + a * (1.0 - sig))
    return [jnp.stack([(ds * b * dsilu).astype(BF16), (ds * silu).astype(BF16)], axis=0)]


def _attn_mask(i, j, tile, pad, strict):
    row = i * tile + lax.broadcasted_iota(jnp.int32, (tile, tile), 0)
    col = j * tile + lax.broadcasted_iota(jnp.int32, (tile, tile), 1)
    causal = (col < row) if strict else (col <= row)
    return causal & ((col >= pad) | (row < pad))


FWD_GROUP = 4
BWD_GROUP = 2


def _walk_key_tiles(i, step, carry, widths, descending=False):
    diagonal = lambda j, c: step(j, c, True, 1)
    left = jnp.maximum(i - 1, 0)
    lo = 1
    if descending:
        carry = lax.fori_loop(jnp.maximum(i, 1), i + 1, diagonal, carry)
    else:
        carry = step(0, carry, True, 1)
    for nt in widths:
        count = left // nt
        if descending:
            top = lo + left
            carry = lax.fori_loop(0, count, lambda t, c, nt=nt, top=top: step(top - nt * (t + 1), c, False, nt), carry)
        else:
            carry = lax.fori_loop(0, count, lambda t, c, nt=nt, lo=lo: step(lo + nt * t, c, False, nt), carry)
            lo = lo + nt * count
        left = left - nt * count
    if descending:
        return step(0, carry, True, 1)
    return lax.fori_loop(jnp.maximum(i, 1), i + 1, diagonal, carry)


def _head_cols(g):
    return pl.ds(g * HEAD_DIM, HEAD_DIM)


def _key_rows(j, tile, nt):
    return pl.ds(pl.multiple_of(j * tile, tile), nt * tile)


def _key_tile(ref, j, tile, g, nt=1):
    return ref[_key_rows(j, tile, nt), _head_cols(g)]


def _key_bias(c_ref, g, j, nt):
    return c_ref[g, j] if nt == 1 else jnp.concatenate([c_ref[g, j + t] for t in range(nt)], axis=1)


def _head_specs(lp, tile, q_off, k_off, v_off, hp):
    gw = hp * HEAD_DIM
    q_spec = pl.BlockSpec((tile, gw), lambda h, i: (i, h + q_off // hp))
    k_spec = pl.BlockSpec((lp, gw), lambda h, i: (0, h + k_off // hp))
    v_spec = pl.BlockSpec((lp, gw), lambda h, i: (0, h + v_off // hp))
    return q_spec, k_spec, v_spec


def _fox_fwd(q, k, v, v_off, c_rows, *, tile, pad, name):
    lp = q.shape[0]
    nb = lp // tile
    hp = FWD_GROUP
    gw = hp * HEAD_DIM

    def body(q_ref, k_ref, v_ref, c_ref, o_ref, lse_ref):
        i = pl.program_id(1)
        qts = [q_ref[:, _head_cols(g)] for g in range(hp)]

        def step(j, carry, masked, nt):
            ok = _attn_mask(i, j, tile, pad, False) if masked else None
            out = []
            for g, (m, l, acc) in enumerate(carry):
                s = _dot(qts[g], _key_tile(k_ref, j, tile, g, nt), NT) * ATTN_SCALE - _key_bias(c_ref, g, j, nt)
                if masked:
                    s = jnp.where(ok, s, MASKED_LOGIT)
                m_new = jnp.maximum(m, jnp.max(s, axis=1, keepdims=True))
                p = jnp.exp(s - m_new)
                alpha = jnp.exp(m - m_new)
                l = alpha * l + jnp.sum(p, axis=1, keepdims=True)
                acc = alpha * acc + _dot(p.astype(BF16), _key_tile(v_ref, j, tile, g, nt), NN)
                out.append((m_new, l, acc))
            return tuple(out)

        init = (jnp.full((tile, 1), MASKED_LOGIT, F32), jnp.zeros((tile, 1), F32), jnp.zeros((tile, HEAD_DIM), F32))
        final = _walk_key_tiles(i, step, (init,) * hp, (4, 2, 1))
        for g, (m, l, acc) in enumerate(final):
            o_ref[:, _head_cols(g)] = (acc / l).astype(o_ref.dtype)
            lse_ref[g] = jnp.broadcast_to(m + jnp.log(l), (tile, LANES))

    q_spec, k_spec, v_spec = _head_specs(lp, tile, 0, 0, v_off, hp)
    return pl.pallas_call(
        body, name=name,
        grid=(HEADS // hp, nb),
        in_specs=[q_spec, k_spec, v_spec, pl.BlockSpec((hp, nb, 1, tile), lambda h, i: (h, 0, 0, 0))],
        out_specs=[pl.BlockSpec((tile, gw), lambda h, i: (i, h)),
                   pl.BlockSpec((hp, tile, LANES), lambda h, i: (h, i, 0))],
        out_shape=[jax.ShapeDtypeStruct((lp, BRANCH_WIDTH), BF16), jax.ShapeDtypeStruct((HEADS, lp, LANES), F32)],
        compiler_params=pltpu.CompilerParams(dimension_semantics=("parallel", "arbitrary")),
    )(q, k, v, c_rows)


def _fox_bwd(q, k, v, v_off, c_rows, o, do, lse, *, tile, pad, name):
    lp = q.shape[0]
    nb = lp // tile
    hp = BWD_GROUP
    gw = hp * HEAD_DIM

    def body(q_ref, k_ref, v_ref, c_ref, o_ref, do_ref, lse_ref, dq_ref, dk_ref, dv_ref, dc_ref, dk_acc, dv_acc, dc_acc):
        i = pl.program_id(1)

        @pl.when(i == 0)
        def _():
            dk_acc[...] = jnp.zeros_like(dk_acc)
            dv_acc[...] = jnp.zeros_like(dv_acc)
            dc_acc[...] = jnp.zeros_like(dc_acc)

        heads = range(hp)
        qts = [q_ref[:, _head_cols(g)] for g in heads]
        dots = [do_ref[:, _head_cols(g)] for g in heads]
        deltas = [jnp.sum(dots[g].astype(F32) * o_ref[:, _head_cols(g)].astype(F32), axis=1, keepdims=True) for g in heads]
        lse_cols = [lse_ref[g][:, :1] for g in heads]

        def step(j, dqs, masked, nt):
            rows = _key_rows(j, tile, nt)
            ok = _attn_mask(i, j, tile, pad, False) if masked else None
            out = []
            for g in heads:
                kt = _key_tile(k_ref, j, tile, g, nt)
                s = _dot(qts[g], kt, NT) * ATTN_SCALE - _key_bias(c_ref, g, j, nt)
                p = jnp.exp(s - lse_cols[g])
                if masked:
                    p = jnp.where(ok, p, 0.0)
                dp = _dot(dots[g], _key_tile(v_ref, j, tile, g, nt), NT)
                ds = p * (dp - deltas[g])
                dsb = ds.astype(BF16)
                dk_acc[rows, _head_cols(g)] += _dot(dsb, qts[g], TN)
                dv_acc[rows, _head_cols(g)] += _dot(p.astype(BF16), dots[g], TN)
                dc = -jnp.sum(ds, axis=0, keepdims=True)
                for t in range(nt):
                    dc_acc[g, j + t] += dc[:, t * tile:(t + 1) * tile]
                out.append(dqs[g] + _dot(dsb, kt, NN))
            return tuple(out)

        dqs = _walk_key_tiles(i, step, (jnp.zeros((tile, HEAD_DIM), F32),) * hp, (2, 1))
        for g in heads:
            dq_ref[:, _head_cols(g)] = dqs[g] * ATTN_SCALE

        @pl.when(i == nb - 1)
        def _():
            dk_ref[...] = dk_acc[...] * ATTN_SCALE
            dv_ref[...] = dv_acc[...].astype(dv_ref.dtype)
            dc_ref[...] = dc_acc[...]

    q_spec, k_spec, v_spec = _head_specs(lp, tile, 0, 0, v_off, hp)
    tile_spec = pl.BlockSpec((tile, gw), lambda h, i: (i, h))
    head_spec = pl.BlockSpec((lp, gw), lambda h, i: (0, h))
    c_spec = pl.BlockSpec((hp, nb, 1, tile), lambda h, i: (h, 0, 0, 0))
    return pl.pallas_call(
        body, name=name,
        grid=(HEADS // hp, nb),
        in_specs=[q_spec, k_spec, v_spec, c_spec, tile_spec, tile_spec,
                  pl.BlockSpec((hp, tile, LANES), lambda h, i: (h, i, 0))],
        out_specs=[tile_spec, head_spec, head_spec, c_spec],
        out_shape=[jax.ShapeDtypeStruct((lp, BRANCH_WIDTH), F32), jax.ShapeDtypeStruct((lp, BRANCH_WIDTH), F32),
                   jax.ShapeDtypeStruct((lp, BRANCH_WIDTH), BF16), jax.ShapeDtypeStruct((HEADS, nb, 1, tile), F32)],
        scratch_shapes=[pltpu.VMEM((lp, gw), F32), pltpu.VMEM((lp, gw), F32),
                        pltpu.VMEM((hp, nb, 1, tile), F32)],
        compiler_params=pltpu.CompilerParams(dimension_semantics=("parallel", "arbitrary")),
    )(q, k, v, c_rows, o, do, lse)


def _later_matrix(tile):
    return (lax.broadcasted_iota(jnp.int32, (tile, tile), 0) > lax.broadcasted_iota(jnp.int32, (tile, tile), 1)).astype(BF16)


def _earlier_matrix(tile):
    return (lax.broadcasted_iota(jnp.int32, (tile, tile), 0) < lax.broadcasted_iota(jnp.int32, (tile, tile), 1)).astype(BF16)


def _running_sums(x, tri, suffix):
    tile = tri.shape[0]
    blocks = [x[:, b:b + tile] for b in range(0, x.shape[1], tile)]
    sums = [jnp.sum(blk, axis=1, keepdims=True) for blk in blocks]
    out = []
    for b, blk in enumerate(blocks):
        hi, lo = _split_bf16(blk)
        inside = _dot(hi, tri, NN) + _dot(lo, tri, NN)
        for other in (sums[b + 1:] if suffix else sums[:b]):
            inside = inside + other
        out.append(inside)
    total = sums[0]
    for other in sums[1:]:
        total = total + other
    return (out[0] if len(out) == 1 else jnp.concatenate(out, axis=1)), total


def _sb_tile(qt, kt, ok, later):
    z = _dot(qt, kt, NT) * ATTN_SCALE
    e, r, lg = _sigmoid_parts(z)
    sp = jnp.maximum(z, 0.0) + lg
    spm = sp if ok is None else jnp.where(ok, sp, 0.0)
    within, sp_here = _running_sums(spm, later, True)
    return z, e, r, sp, sp_here, within


def _sb_fwd(qkv, q_off, k_off, v_off, *, tile, pad, name):
    lp = qkv.shape[0]
    nb = lp // tile
    hp = FWD_GROUP
    gw = hp * HEAD_DIM

    def body(q_ref, k_ref, v_ref, o_ref, tot_ref):
        i = pl.program_id(1)
        qts = [q_ref[:, _head_cols(g)] for g in range(hp)]
        later = _later_matrix(tile)

        def step(j, carry, masked, nt):
            ok = _attn_mask(i, j, tile, pad, True) if masked else None
            out = []
            for g, (right, acc) in enumerate(carry):
                z, _, _, sp, sp_here, within = _sb_tile(qts[g], _key_tile(k_ref, j, tile, g, nt), ok, later)
                w = jnp.exp(z - sp - within - right)
                if masked:
                    w = jnp.where(ok, w, 0.0)
                acc = acc + _dot(w.astype(BF16), _key_tile(v_ref, j, tile, g, nt), NN)
                out.append((right + sp_here, acc))
            return tuple(out)

        init = (jnp.zeros((tile, 1), F32), jnp.zeros((tile, HEAD_DIM), F32))
        final = _walk_key_tiles(i, step, (init,) * hp, (2, 1), descending=True)
        for g, (total, acc) in enumerate(final):
            o_ref[:, _head_cols(g)] = acc.astype(o_ref.dtype)
            tot_ref[g] = jnp.broadcast_to(total, (tile, LANES))

    q_spec, k_spec, v_spec = _head_specs(lp, tile, q_off, k_off, v_off, hp)
    return pl.pallas_call(
        body, name=name,
        grid=(HEADS // hp, nb),
        in_specs=[q_spec, k_spec, v_spec],
        out_specs=[pl.BlockSpec((tile, gw), lambda h, i: (i, h)),
                   pl.BlockSpec((hp, tile, LANES), lambda h, i: (h, i, 0))],
        out_shape=[jax.ShapeDtypeStruct((lp, BRANCH_WIDTH), BF16), jax.ShapeDtypeStruct((HEADS, lp, LANES), F32)],
        compiler_params=pltpu.CompilerParams(dimension_semantics=("parallel", "arbitrary")),
    )(qkv, qkv, qkv)


def _sb_bwd(qkv, q_off, k_off, v_off, do, total, *, tile, pad, name):
    lp = qkv.shape[0]
    nb = lp // tile
    hp = BWD_GROUP
    gw = hp * HEAD_DIM

    def body(q_ref, k_ref, v_ref, do_ref, tot_ref, dq_ref, dk_ref, dv_ref, dk_acc, dv_acc):
        i = pl.program_id(1)

        @pl.when(i == 0)
        def _():
            dk_acc[...] = jnp.zeros_like(dk_acc)
            dv_acc[...] = jnp.zeros_like(dv_acc)

        heads = range(hp)
        qts = [q_ref[:, _head_cols(g)] for g in heads]
        dots = [do_ref[:, _head_cols(g)] for g in heads]
        total_cols = [tot_ref[g][:, :1] for g in heads]
        later, earlier = _later_matrix(tile), _earlier_matrix(tile)

        def step(j, carry, masked, nt):
            rows = _key_rows(j, tile, nt)
            ok = _attn_mask(i, j, tile, pad, True) if masked else None
            out = []
            for g, (dq, sp_before, dlw_before) in enumerate(carry):
                kt = _key_tile(k_ref, j, tile, g, nt)
                z, e, r, sp, sp_here, within = _sb_tile(qts[g], kt, ok, later)
                right = total_cols[g] - sp_before - sp_here
                w = jnp.exp(z - sp - within - right)
                if masked:
                    w = jnp.where(ok, w, 0.0)
                dlw = w * _dot(dots[g], _key_tile(v_ref, j, tile, g, nt), NT)
                before, dlw_here = _running_sums(dlw, earlier, False)
                sig = jnp.where(z >= 0, r, e * r)
                dz = dlw * (1.0 - sig) - sig * (before + dlw_before)
                if masked:
                    dz = jnp.where(ok, dz, 0.0)
                dzb = dz.astype(BF16)
                dk_acc[rows, _head_cols(g)] += _dot(dzb, qts[g], TN)
                dv_acc[rows, _head_cols(g)] += _dot(w.astype(BF16), dots[g], TN)
                out.append((dq + _dot(dzb, kt, NN), sp_before + sp_here, dlw_before + dlw_here))
            return tuple(out)

        zero_col = jnp.zeros((tile, 1), F32)
        final = _walk_key_tiles(i, step, ((jnp.zeros((tile, HEAD_DIM), F32), zero_col, zero_col),) * hp, (2, 1))
        for g in heads:
            dq_ref[:, _head_cols(g)] = (final[g][0] * ATTN_SCALE).astype(dq_ref.dtype)

        @pl.when(i == nb - 1)
        def _():
            dk_ref[...] = (dk_acc[...] * ATTN_SCALE).astype(dk_ref.dtype)
            dv_ref[...] = dv_acc[...].astype(dv_ref.dtype)

    q_spec, k_spec, v_spec = _head_specs(lp, tile, q_off, k_off, v_off, hp)
    tile_spec = pl.BlockSpec((tile, gw), lambda h, i: (i, h))
    head_spec = pl.BlockSpec((lp, gw), lambda h, i: (0, h))
    return pl.pallas_call(
        body, name=name,
        grid=(HEADS // hp, nb),
        in_specs=[q_spec, k_spec, v_spec, tile_spec, pl.BlockSpec((hp, tile, LANES), lambda h, i: (h, i, 0))],
        out_specs=[tile_spec, head_spec, head_spec],
        out_shape=[jax.ShapeDtypeStruct((lp, BRANCH_WIDTH), BF16)] * 3,
        scratch_shapes=[pltpu.VMEM((lp, gw), F32), pltpu.VMEM((lp, gw), F32)],
        compiler_params=pltpu.CompilerParams(dimension_semantics=("parallel", "arbitrary")),
    )(qkv, qkv, qkv, do, total)


def _cumsum_rows(x, *, tile, reverse, name):
    lp = x.shape[0]
    nb = lp // tile

    def body(x_ref, o_ref, carry):
        @pl.when(pl.program_id(0) == 0)
        def _():
            carry[...] = jnp.zeros_like(carry)

        r = lax.broadcasted_iota(jnp.int32, (tile, tile), 0)
        c = lax.broadcasted_iota(jnp.int32, (tile, tile), 1)
        tri = ((c >= r) if reverse else (c <= r)).astype(BF16)
        hi, lo = _split_bf16(x_ref[...])
        run = _dot(tri, hi, NN) + _dot(tri, lo, NN) + carry[...]
        o_ref[...] = run
        carry[...] = run[:1, :] if reverse else run[tile - 1:, :]

    order = (lambda i: (nb - 1 - i, 0)) if reverse else (lambda i: (i, 0))
    return pl.pallas_call(
        body, name=name,
        grid=(nb,),
        in_specs=[pl.BlockSpec((tile, LANES), order)],
        out_specs=pl.BlockSpec((tile, LANES), order),
        out_shape=jax.ShapeDtypeStruct((lp, LANES), F32),
        scratch_shapes=[pltpu.VMEM((1, LANES), F32)],
        compiler_params=pltpu.CompilerParams(dimension_semantics=("arbitrary",)),
    )(x)


def _log_sigmoid(x):
    return jnp.minimum(x, 0.0) - jnp.log(1.0 + jnp.exp(-jnp.abs(x)))


def _forget_mask(i, tm, pad):
    row = i * tm + lax.broadcasted_iota(jnp.int32, (tm, LANES), 0)
    lane = lax.broadcasted_iota(jnp.int32, (tm, LANES), 1)
    return (row >= pad) & (lane < HEADS)


def _fox_prep(proj_a, q_gain, k_gain, b_forget, *, tm, pad, name):
    w = BRANCH_WIDTH

    def fn(i, tiles, consts):
        pa = tiles[0]
        qs, ks = [], []
        for h in range(HEADS):
            lo = h * HEAD_DIM
            qs.append(_rms(pa[:, lo:lo + HEAD_DIM], consts[0][:, lo:lo + HEAD_DIM]))
            ks.append(_rms(pa[:, w + lo:w + lo + HEAD_DIM], consts[1][:, lo:lo + HEAD_DIM]))
        logf = jnp.where(_forget_mask(i, tm, pad), _log_sigmoid(pa[:, 2 * w:] + consts[2]), 0.0)
        return [jnp.concatenate(qs, axis=1), jnp.concatenate(ks, axis=1), logf], []

    return _rowwise(fn, [_whole(proj_a)], [q_gain, k_gain, b_forget],
                    [(w, BF16), (w, BF16), (LANES, F32)], [], tm=tm, name=name)


def _fox_prep_bwd(proj_a, dq, dk, dlogf, q_gain, k_gain, b_forget, *, tm, pad, name):
    w = BRANCH_WIDTH

    def fn(i, tiles, consts):
        pa, dqt, dkt, dlf = tiles
        dxs_q, dxs_k, dgs_q, dgs_k = [], [], [], []
        for h in range(HEADS):
            lo = h * HEAD_DIM
            dx, dg = _rms_bwd(pa[:, lo:lo + HEAD_DIM], consts[0][:, lo:lo + HEAD_DIM], dqt[:, lo:lo + HEAD_DIM])
            dxs_q.append(dx)
            dgs_q.append(jnp.sum(dg, axis=0, keepdims=True))
            dx, dg = _rms_bwd(pa[:, w + lo:w + lo + HEAD_DIM], consts[1][:, lo:lo + HEAD_DIM], dkt[:, lo:lo + HEAD_DIM])
            dxs_k.append(dx)
            dgs_k.append(jnp.sum(dg, axis=0, keepdims=True))
        xf = pa[:, 2 * w:] + consts[2]
        e, r, _ = _sigmoid_parts(xf)
        df = jnp.where(_forget_mask(i, tm, pad), dlf * jnp.where(xf >= 0, e * r, r), 0.0)
        return ([jnp.concatenate(dxs_q + dxs_k + [df], axis=1)],
                [jnp.concatenate(dgs_q, axis=1), jnp.concatenate(dgs_k, axis=1), jnp.sum(df, axis=0, keepdims=True)])

    return _rowwise(fn, [_whole(proj_a), _whole(dq), _whole(dk), _whole(dlogf)], [q_gain, k_gain, b_forget],
                    [(2 * w + LANES, BF16)], [(1, w), (1, w), (1, LANES)], tm=tm, name=name)


def _adamw_math(w, g, m, v):
    m = ADAM_B1 * m + (1.0 - ADAM_B1) * g
    v = ADAM_B2 * v + (1.0 - ADAM_B2) * (g * g)
    m_hat = m / (1.0 - ADAM_B1 ** ADAM_STEP)
    v_hat = v / (1.0 - ADAM_B2 ** ADAM_STEP)
    delta = -ADAM_LR * (m_hat / (jnp.sqrt(v_hat) + ADAM_EPS) + ADAM_WD * w)
    return delta, m, v


def _adamw_summed(parts, sel, w, m, v, *, name):
    rows, cols = w.shape
    tr = _pick(rows, [t for t in (512, 256, 128, 64, 32, 16, 8) if t * cols <= ADAMW_TILE_ELEMS])

    def body(p_ref, w_ref, m_ref, v_ref, g_out, d_out, m_out, v_out):
        g = p_ref[0].astype(F32)
        for j in range(1, NDEV):
            g = g + p_ref[j].astype(F32)
        delta, m_new, v_new = _adamw_math(w_ref[...], g, m_ref[...], v_ref[...])
        g_out[...] = g
        d_out[...] = delta
        m_out[...] = m_new
        v_out[...] = v_new

    spec = pl.BlockSpec((tr, cols), lambda i: (i, 0))
    if parts.ndim == 3:
        p_spec = pl.BlockSpec((NDEV, tr, cols), lambda i: (0, i, 0))
    else:
        p_spec = pl.BlockSpec((NDEV, None, tr, cols), lambda i: (0, sel, i, 0))
    return pl.pallas_call(
        body, name=name,
        grid=(rows // tr,),
        in_specs=[p_spec, spec, spec, spec],
        out_specs=[spec] * 4,
        out_shape=[jax.ShapeDtypeStruct((rows, cols), F32)] * 4,
        compiler_params=pltpu.CompilerParams(dimension_semantics=("parallel",)),
    )(parts, w, m, v)


def _adamw_plain(g, w, m, v, *, name):
    def body(g_ref, w_ref, m_ref, v_ref, d_out, m_out, v_out):
        delta, m_new, v_new = _adamw_math(w_ref[...], g_ref[...], m_ref[...], v_ref[...])
        d_out[...] = delta
        m_out[...] = m_new
        v_out[...] = v_new

    return pl.pallas_call(body, name=name, out_shape=[jax.ShapeDtypeStruct(w.shape, F32)] * 3)(g, w, m, v)


def _pack_rows(arrays, width, row_align):
    pieces, spans, at = [], [], 0
    for arr in arrays:
        flat = arr.reshape(-1)
        rows = -(-flat.shape[0] // (width * row_align)) * row_align
        flat = jnp.pad(flat, (0, rows * width - flat.shape[0]))
        pieces.append(flat.reshape(rows, width))
        spans.append((at, rows))
        at += rows
    return jnp.concatenate(pieces, axis=0), spans


def _unpack(rows2d, span, shape):
    at, rows = span
    size = 1
    for s in shape:
        size *= s
    return rows2d[at:at + rows].reshape(-1)[:size].reshape(shape)


def _join_cols(blocks):
    n, rows, cols = blocks.shape
    return jnp.transpose(blocks, (1, 0, 2)).reshape(rows, n * cols)


def _split_cols(full):
    rows, cols = full.shape
    return jnp.transpose(full.reshape(rows, NDEV, cols // NDEV), (1, 0, 2))


def _ffn_up(h, gain, w_gu, after, *, tm, tag):
    lp, d = h.shape
    f8 = w_gu.shape[3]
    n = _norm_fwd(h, gain, tm=_pick(lp, [256, 128]), name=f"{tag}_norm")
    hid = _spec((None, tm, f8), lambda i, j, k: (j, i, 0))
    a, b, s = _mm((lp // tm, NDEV, 1), n, _spec((tm, d), lambda i, j, k: (i, 0)),
                  [(w_gu, _spec((None, None, d, f8), lambda i, j, k: (j, 0, 0, 0))),
                   (w_gu, _spec((None, None, d, f8), lambda i, j, k: (j, 1, 0, 0)))],
                  [], [((NDEV, lp, f8), BF16, hid)] * 3, dims=NN, epi=_swiglu_epi, after=after, name=f"{tag}_up")
    return n, a, b, s


def _ffn_down(h, s, w_down, *, tm, tag):
    lp, d = h.shape
    f8 = w_down.shape[1]
    tn = _pick(d, [1024, 512, 256, 128])
    tile = _spec((tm, tn), lambda i, j, k: (i, j))
    return _mm((lp // tm, d // tn, NDEV // KSUB), s, _spec((KSUB, tm, f8), lambda i, j, k: (k, i, 0)),
               [(w_down, _spec((KSUB, f8, tn), lambda i, j, k: (k, 0, j)))], [(h, tile)], [((lp, d), F32, tile)],
               dims=NN, acc_shape=(tm, tn), epi=lambda accs, ex: [ex[0] + FFN_RESIDUAL_WEIGHT * accs[0]], name=f"{tag}_down")[0]


def _ffn_bwd_dw(dh_half, saved, w_down, on_down, *, tm, tag):
    n, a, b, s = saved
    lp, d = dh_half.shape
    f8 = w_down.shape[1]
    tkr = _pick(lp, [2176, 1088, 544, 256, 128])
    tn = _pick(d, [1024, 512, 256, 128])
    hid = _spec((None, tm, f8), lambda i, j, k: (j, i, 0))
    dab, = _mm((lp // tm, NDEV, 1), dh_half, _spec((tm, d), lambda i, j, k: (i, 0)),
               [(w_down, _spec((None, f8, d), lambda i, j, k: (j, 0, 0)))], [(a, hid), (b, hid)],
               [((2, NDEV, lp, f8), BF16, _spec((2, None, tm, f8), lambda i, j, k: (0, j, i, 0)))],
               dims=NT, epi=_swiglu_bwd_epi, name=f"{tag}_down_dx")
    dw_down, = _mm((NDEV, d // tn, lp // tkr), s, _spec((None, tkr, f8), lambda i, j, k: (i, k, 0)),
                   [(dh_half, _spec((tkr, tn), lambda i, j, k: (k, j)))], [],
                   [((NDEV, f8, d), BF16, _spec((None, f8, tn), lambda i, j, k: (i, 0, j)))],
                   dims=TN, acc_shape=(f8, tn), name=f"{tag}_down_dw")
    dw_gu, = _mm((d // tn, 2 * NDEV, lp // tkr), n, _spec((tkr, tn), lambda i, j, k: (k, i)),
                 [(dab, _spec((None, None, tkr, f8), lambda i, j, k: (j // NDEV, j % NDEV, k, 0)))], [],
                 [((NDEV, 2, d, f8), BF16, _spec((None, None, tn, f8), lambda i, j, k: (j % NDEV, j // NDEV, i, 0)))],
                 dims=TN, acc_shape=(tn, f8), after=on_down(dw_down), name=f"{tag}_gate_up_dw")
    return dab, dw_gu, dw_down


def _ffn_bwd_dx(dab, w_gu, after, *, tm, tag):
    lp, f8 = dab.shape[2], dab.shape[3]
    d = w_gu.shape[2]
    tn = _pick(d, [1024, 512, 256, 128])
    nsub = NDEV // KSUB
    return _mm((lp // tm, d // tn, 2 * nsub), dab, _spec((None, KSUB, tm, f8), lambda i, j, k: (k // nsub, k % nsub, i, 0)),
               [(w_gu, _spec((KSUB, None, tn, f8), lambda i, j, k: (k % nsub, k // nsub, j, 0)))], [],
               [((lp, d), F32, _spec((tm, tn), lambda i, j, k: (i, j)))],
               dims=NT, acc_shape=(tm, tn), after=after, name=f"{tag}_gate_up_dx")[0]


def kernel(x, meta_tokens, ffn1_norm, ffn1_w_gate, ffn1_w_up, ffn1_w_down, mix_norm, w_in, b_forget, fox_q_norm, fox_k_norm, w_branch_fox, w_branch_sb, w_out, ffn2_norm, ffn2_w_gate, ffn2_w_up, ffn2_w_down, loss_target, m_meta_tokens, m_ffn1_norm, m_ffn1_w_gate, m_ffn1_w_up, m_ffn1_w_down, m_mix_norm, m_w_in, m_b_forget, m_fox_q_norm, m_fox_k_norm, m_w_branch_fox, m_w_branch_sb, m_w_out, m_ffn2_norm, m_ffn2_w_gate, m_ffn2_w_up, m_ffn2_w_down, v_meta_tokens, v_ffn1_norm, v_ffn1_w_gate, v_ffn1_w_up, v_ffn1_w_down, v_mix_norm, v_w_in, v_b_forget, v_fox_q_norm, v_fox_k_norm, v_w_branch_fox, v_w_branch_sb, v_w_out, v_ffn2_norm, v_ffn2_w_gate, v_ffn2_w_up, v_ffn2_w_down):
    seq, d = x.shape[1], x.shape[2]
    d8 = d // NDEV
    w = BRANCH_WIDTH
    tile = 256 if seq % 256 == 0 else 128
    pad = tile - N_META
    lp = tile + seq
    tm = _pick(lp, [1088, 544, 256, 128])
    tr = _pick(tile, [256, 128])
    tkr = _pick(lp, [2176, 1088, 544, 256, 128])
    nb = lp // tile
    me = _flat_id(*_my_place())

    shards = [jnp.stack([ffn1_w_gate[0], ffn1_w_up[0]]).astype(BF16), ffn1_w_down[0].astype(BF16), w_in[0].astype(BF16),
              w_branch_fox[0].astype(BF16), w_branch_sb[0].astype(BF16), w_out[0].astype(BF16),
              jnp.stack([ffn2_w_gate[0], ffn2_w_up[0]]).astype(BF16), ffn2_w_down[0].astype(BF16)]
    w_gu1, = _allgather(shards[:1], "gather_ffn1")
    down1_copies, token = _send_start(shards[1:2], [_landing(s, me) for s in shards[1:2]], "first", w_gu1, "gather_down1_start")
    w_in_copies, token = _send_start(shards[2:3], [_landing(s, me) for s in shards[2:3]], "first", token, "gather_w_in_start")
    mixer_copies, token = _send_start(shards[3:6], [_landing(s, me) for s in shards[3:6]], "gather", token, "gather_mixer_start")
    ffn2_copies, token = _send_start(shards[6:], [_landing(s, me) for s in shards[6:]], "gather", token, "gather_ffn2_start")
    meta_full = _join_cols(_allgather([meta_tokens], "gather_meta", in_vmem=True)[0])

    h0 = jnp.concatenate([jnp.zeros((pad, d), F32), meta_full.astype(F32), x[0]], axis=0)
    saved1 = _ffn_up(h0, ffn1_norm, w_gu1, token, tm=tm, tag="ffn1")
    w_down1, = _gather_two_level(down1_copies, saved1[3], "gather_down1")
    h1 = _ffn_down(h0, saved1[3], w_down1, tm=tm, tag="ffn1")

    w_in_blocks, = _gather_two_level(w_in_copies, h1, "gather_w_in")
    wi = _join_cols(w_in_blocks)
    w_pa = jnp.concatenate([wi[:, :2 * w], jnp.pad(wi[:, 3 * w:3 * w + HEADS], ((0, 0), (0, LANES - HEADS)))], axis=1)
    w_pb = jnp.concatenate([wi[:, 2 * w:3 * w], wi[:, 3 * w + HEADS:]], axis=1)
    na, nbw = w_pa.shape[1], w_pb.shape[1]
    gate_blk = 4 * w // d

    n2 = _norm_fwd(h1, mix_norm, tm=tr, name="mix_norm")
    tma = _pick(lp, [544, 256, 128])
    tnd = _pick(d, [1024, 512, 256, 128])
    tnb = _pick(nbw, [512, 256, 128])
    proj_a = _mm2d(n2, w_pa, na, dims=NN, tm=tma, tn=na, tk=d, out_dtype=F32, name="proj_a")
    proj_b = _mm2d(n2, w_pb, nbw, dims=NN, tm=tm, tn=tnb, tk=d, out_dtype=BF16, name="proj_b")
    b_pad = jnp.pad(b_forget, ((0, 0), (0, LANES - HEADS)))
    q_gain, k_gain = fox_q_norm.reshape(1, w), fox_k_norm.reshape(1, w)
    fq, fk, logf = _fox_prep(proj_a, q_gain, k_gain, b_pad, tm=tr, pad=pad, name="fox_prep")
    c = _cumsum_rows(logf, tile=tile, reverse=False, name="forget_cumsum")
    c_rows = jnp.transpose(c[:, :HEADS]).reshape(HEADS, nb, 1, tile)
    o_fox, lse = _fox_fwd(fq, fk, proj_b, 0, c_rows, tile=tile, pad=pad, name="fox_fwd")
    o_sb, sb_total = _sb_fwd(proj_b, HEADS, 2 * HEADS, 3 * HEADS, tile=tile, pad=pad, name="sb_fwd")
    w_br_fox, w_br_sb, w_out_blocks = _send_wait(mixer_copies, o_sb, "gather_mixer_wait")
    w_out_full = w_out_blocks.reshape(d, d)

    def branch(o, w_blocks, name):
        return _mm((lp // tm, NDEV, 1), o, _spec((tm, w), lambda i, j, k: (i, 0)),
                   [(w_blocks, _spec((None, w, d8), lambda i, j, k: (j, 0, 0)))], [],
                   [((lp, d), BF16, _spec((tm, d8), lambda i, j, k: (i, j)))], dims=NN, name=name)[0]

    br_fox = branch(o_fox, w_br_fox, "branch_fox")
    br_sb = branch(o_sb, w_br_sb, "branch_sb")

    def merge_fn(i, tiles, consts):
        bf_, bs_, gf_, gs_ = [t.astype(F32) for t in tiles]
        return [_sigmoid(gf_) * bf_ + _sigmoid(gs_) * bs_], []

    gates_in = [(proj_b, d, gate_blk), (proj_b, d, gate_blk + 1)]
    merged, = _rowwise(merge_fn, [_whole(br_fox), _whole(br_sb)] + gates_in, [], [(d, BF16)], [], tm=tr, name="merge")
    h2 = _mm2d(merged, w_out_full, d, dims=NN, tm=tm, tn=tnd, tk=d, out_dtype=F32,
               epi=lambda accs, ex: [ex[0] + accs[0]], extras=[h1], name="out_proj")

    w_gu2, w_down2 = _send_wait(ffn2_copies, h2, "gather_ffn2_wait")
    saved3 = _ffn_up(h2, ffn2_norm, w_gu2, None, tm=tm, tag="ffn2")
    h3 = _ffn_down(h2, saved3[3], w_down2, tm=tm, tag="ffn2")

    skip = tile // tr

    def loss_fn(i, tiles, consts):
        real = i >= skip
        err = jnp.where(real, tiles[0] - tiles[1], 0.0)
        dy = err * (1.0 / d)
        part = 0.5 * jnp.sum(err * dy, axis=0, keepdims=True)
        return [dy, FFN_RESIDUAL_WEIGHT * dy], [part]

    dh3, dh3_half, loss_cols = _rowwise(
        loss_fn, [_whole(h3), _whole(loss_target[0])], [], [(d, F32, lp), (d, BF16, lp)], [(1, d)], tm=tr, name="loss",
        row_maps=[None, lambda i: jnp.maximum(i - skip, 0)])

    def own(g):
        return lax.dynamic_index_in_dim(g, me, 0, keepdims=False)

    def send_grads(grads_, name):
        return _send_start(grads_, [_landing(own(g), me) for g in grads_], "exchange", grads_[-1], name)

    sends = {}

    def send_piece(key):
        def on_ready(*gs):
            sends[key], token_ = send_grads(list(gs), f"exchange_{key}_start")
            return token_
        return on_ready

    dab3, dw_gu2, _ = _ffn_bwd_dw(dh3_half, saved3, w_down2, send_piece("down2"), tm=tm, tag="ffn2")
    dn3 = _ffn_bwd_dx(dab3, w_gu2, send_piece("gu2")(dw_gu2), tm=tm, tag="ffn2")
    dh2, dh2_bf, _, dg_ffn2 = _norm_bwd(dn3, h2, ffn2_norm, dh3, tm=tr, name="ffn2_norm_bwd")

    dmerged = _mm2d(dh2_bf, w_out_full, d, dims=NT, tm=tm, tn=tnd, tk=d, out_dtype=BF16, name="out_proj_dx")
    dw_out = _mm2d(merged, dh2_bf, d, dims=TN, tm=tnd, tn=tnd, tk=tkr, out_dtype=BF16, name="out_proj_dw")
    token = send_piece("out")(dw_out.reshape(NDEV, d8, d))

    def merge_bwd_fn(i, tiles, consts):
        dm, bf_, bs_, gf_, gs_ = [t.astype(F32) for t in tiles]
        sf, ss = _sigmoid(gf_), _sigmoid(gs_)
        return [dm * sf, dm * ss, dm * bf_ * sf * (1.0 - sf), dm * bs_ * ss * (1.0 - ss)], []

    dbr_fox, dbr_sb, dg_fox, dg_sb = _rowwise(
        merge_bwd_fn, [_whole(dmerged), _whole(br_fox), _whole(br_sb)] + gates_in, [], [(d, BF16)] * 4, [], tm=tr, after=token,
        name="merge_bwd")

    tnw = _pick(w, [512, 256, 128])

    def branch_dx(dbr, w_blocks, after, name):
        return _mm((lp // tm, w // tnw, NDEV), dbr, _spec((tm, d8), lambda i, j, k: (i, k)),
                   [(w_blocks, _spec((None, tnw, d8), lambda i, j, k: (k, j, 0)))], [],
                   [((lp, w), BF16, _spec((tm, tnw), lambda i, j, k: (i, j)))], dims=NT, acc_shape=(tm, tnw), after=after,
                   name=name)[0]

    def branch_dw(o, dbr, name):
        return _mm((w // tnw, NDEV, lp // tkr), o, _spec((tkr, tnw), lambda i, j, k: (k, i)),
                   [(dbr, _spec((tkr, d8), lambda i, j, k: (k, j)))], [],
                   [((NDEV, w, d8), BF16, _spec((None, tnw, d8), lambda i, j, k: (j, i, 0)))],
                   dims=TN, acc_shape=(tnw, d8), name=name)[0]

    token = send_piece("branch")(branch_dw(o_fox, dbr_fox, "branch_fox_dw"), branch_dw(o_sb, dbr_sb, "branch_sb_dw"))
    do_fox = branch_dx(dbr_fox, w_br_fox, token, "branch_fox_dx")
    do_sb = branch_dx(dbr_sb, w_br_sb, token, "branch_sb_dx")

    dfq, dfk, dfv, dc_rows = _fox_bwd(fq, fk, proj_b, 0, c_rows, o_fox, do_fox, lse, tile=tile, pad=pad, name="fox_bwd")
    dsq, dsk, dsv = _sb_bwd(proj_b, HEADS, 2 * HEADS, 3 * HEADS, do_sb, sb_total, tile=tile, pad=pad, name="sb_bwd")
    dc = jnp.pad(jnp.transpose(dc_rows.reshape(HEADS, lp)), ((0, 0), (0, LANES - HEADS)))
    dlogf = _cumsum_rows(dc, tile=tile, reverse=True, name="forget_cumsum_bwd")
    dproj_a, dg_q, dg_k, dg_b = _fox_prep_bwd(proj_a, dfq, dfk, dlogf, q_gain, k_gain, b_pad, tm=tr, pad=pad, name="fox_prep_bwd")
    dproj_b = jnp.concatenate([dfv, dsq, dsk, dsv, dg_fox, dg_sb], axis=1)

    dw_pa = _mm2d(n2, dproj_a, na, dims=TN, tm=tnd, tn=na, tk=_pick(lp, [544, 256, 128]), out_dtype=BF16, name="proj_a_dw")
    dw_pb = _mm2d(n2, dproj_b, nbw, dims=TN, tm=tnd, tn=tnb, tk=tkr, out_dtype=BF16, name="proj_b_dw")
    dw_in = jnp.concatenate([dw_pa[:, :2 * w], dw_pb[:, :w], dw_pa[:, 2 * w:2 * w + HEADS], dw_pb[:, w:]], axis=1)
    token = send_piece("w_in")(_split_cols(dw_in))
    dn2_a = _mm2d(dproj_a, w_pa, d, dims=NT, tm=tm, tn=tnd, tk=na, out_dtype=F32, after=token, name="proj_a_dx")
    dn2 = _mm2d(dproj_b, w_pb, d, dims=NT, tm=tm, tn=tnd, tk=_pick(nbw, [2048, 1024, 512, 256, 128]), out_dtype=F32,
                epi=lambda accs, ex: [accs[0] + ex[0]], extras=[dn2_a], name="proj_b_dx")
    dh1, _, dh1_half, dg_mix = _norm_bwd(dn2, h1, mix_norm, dh2, tm=tr, name="mix_norm_bwd")

    dab1, dw_gu1, _ = _ffn_bwd_dw(dh1_half, saved1, w_down1, send_piece("down1"), tm=tm, tag="ffn1")
    dn1 = _ffn_bwd_dx(dab1, w_gu1, send_piece("gu1")(dw_gu1), tm=tm, tag="ffn1")
    dh0, _, _, dg_ffn1 = _norm_bwd(dn1, h0, ffn1_norm, dh1, tm=tr, name="ffn1_norm_bwd")
    grad_x = dh0[tile:][None]

    r_down2, = _send_wait(sends["down2"], dh0, "exchange_down2_wait")
    r_gu2, = _send_wait(sends["gu2"], r_down2, "exchange_gu2_wait")
    r_out, = _send_wait(sends["out"], r_gu2, "exchange_out_wait")
    r_br_fox, r_br_sb = _send_wait(sends["branch"], r_out, "exchange_branch_wait")
    r_in, = _send_wait(sends["w_in"], r_br_sb, "exchange_w_in_wait")
    grads, deltas, new_ms, new_vs = {}, {}, {}, {}

    def adamw_big(entries):
        for k, (parts, sel, wt, mt, vt) in entries.items():
            g, dl, mn, vn = _adamw_summed(parts, sel, wt[0], mt[0], vt[0], name=f"adamw_{k}")
            grads[k], deltas[k], new_ms[k], new_vs[k] = g[None], dl[None], mn[None], vn[None]

    adamw_big(dict(ffn2_w_gate=(r_gu2, 0, ffn2_w_gate, m_ffn2_w_gate, v_ffn2_w_gate),
                   ffn2_w_up=(r_gu2, 1, ffn2_w_up, m_ffn2_w_up, v_ffn2_w_up),
                   ffn2_w_down=(r_down2, 0, ffn2_w_down, m_ffn2_w_down, v_ffn2_w_down),
                   w_in=(r_in, 0, w_in, m_w_in, v_w_in),
                   w_branch_fox=(r_br_fox, 0, w_branch_fox, m_w_branch_fox, v_w_branch_fox),
                   w_branch_sb=(r_br_sb, 0, w_branch_sb, m_w_branch_sb, v_w_branch_sb),
                   w_out=(r_out, 0, w_out, m_w_out, v_w_out)))
    done = sum(v[0, 0, :1] for v in new_vs.values())
    r_down1, = _send_wait(sends["down1"], done, "exchange_down1_wait")
    r_gu1, = _send_wait(sends["gu1"], r_down1, "exchange_gu1_wait")
    adamw_big(dict(ffn1_w_gate=(r_gu1, 0, ffn1_w_gate, m_ffn1_w_gate, v_ffn1_w_gate),
                   ffn1_w_up=(r_gu1, 1, ffn1_w_up, m_ffn1_w_up, v_ffn1_w_up),
                   ffn1_w_down=(r_down1, 0, ffn1_w_down, m_ffn1_w_down, v_ffn1_w_down)))

    small_parts = [dh0[pad:tile], dg_ffn1, dg_mix, dg_ffn2, dg_b[:, :HEADS], dg_q, dg_k, loss_cols]
    small_packed, small_spans = _pack_rows(small_parts, LANES, SMALL_ROWS)
    small_sum = _allsum_small(small_packed, "sum_small")
    g_meta_full, g_ffn1n, g_mixn, g_ffn2n, g_bf, g_qn, g_kn, loss_vec = [
        _unpack(small_sum, span, part.shape) for span, part in zip(small_spans, small_parts)]
    loss = jnp.sum(loss_vec)
    g_meta = lax.dynamic_slice_in_dim(g_meta_full, me * d8, d8, axis=1)
    g_qn, g_kn = g_qn.reshape(fox_q_norm.shape), g_kn.reshape(fox_k_norm.shape)

    small = dict(meta_tokens=(g_meta, meta_tokens, m_meta_tokens, v_meta_tokens),
                 ffn1_norm=(g_ffn1n, ffn1_norm, m_ffn1_norm, v_ffn1_norm),
                 mix_norm=(g_mixn, mix_norm, m_mix_norm, v_mix_norm),
                 b_forget=(g_bf, b_forget, m_b_forget, v_b_forget),
                 fox_q_norm=(g_qn, fox_q_norm, m_fox_q_norm, v_fox_q_norm),
                 fox_k_norm=(g_kn, fox_k_norm, m_fox_k_norm, v_fox_k_norm),
                 ffn2_norm=(g_ffn2n, ffn2_norm, m_ffn2_norm, v_ffn2_norm))
    for k, (g, wt, mt, vt) in small.items():
        flat = lambda t: t.reshape(-1, t.shape[-1])
        dl, mn, vn = _adamw_plain(flat(g), flat(wt), flat(mt), flat(vt), name=f"adamw_{k}")
        grads[k], deltas[k], new_ms[k], new_vs[k] = g, dl.reshape(wt.shape), mn.reshape(wt.shape), vn.reshape(wt.shape)

    order = ["meta_tokens", "ffn1_norm", "ffn1_w_gate", "ffn1_w_up", "ffn1_w_down", "mix_norm", "w_in", "b_forget",
             "fox_q_norm", "fox_k_norm", "w_branch_fox", "w_branch_sb", "w_out", "ffn2_norm", "ffn2_w_gate",
             "ffn2_w_up", "ffn2_w_down"]
    return (loss, grad_x, *[grads[k] for k in order], *[deltas[k] for k in order],
            *[new_ms[k] for k in order], *[new_vs[k] for k in order])
```

```python
import jax
import jax.numpy as jnp
from jax import lax
from jax.experimental import pallas as pl
from jax.experimental.pallas import tpu as pltpu

F32 = jnp.float32
BF16 = jnp.bfloat16
MESH = pl.DeviceIdType.MESH

NDEV = 8
N_META = 16
HEAD_DIM = 128
HEADS = 8
BRANCH_WIDTH = HEADS * HEAD_DIM
RMS_EPS = 1e-6
FFN_RESIDUAL_WEIGHT = 0.5
ATTN_SCALE = HEAD_DIM ** -0.5
MASKED_LOGIT = -1e30

ADAM_LR = 0.001
ADAM_B1 = 0.9
ADAM_B2 = 0.999
ADAM_EPS = 1e-08
ADAM_WD = 0.01
ADAM_STEP = 10

LANES = 128
SMALL_ROWS = 8
ADAMW_TILE_ELEMS = 160 * 1024
KSUB = 2


def _pick(n, prefs):
    for p in prefs:
        if p <= n and n % p == 0:
            return p
    return n


def _dot(a, b, dims):
    return lax.dot_general(a, b, (dims, ((), ())), preferred_element_type=F32)


NN = ((1,), (0,))
TN = ((0,), (0,))
NT = ((1,), (1,))


def _split_bf16(x):
    hi = x.astype(BF16)
    lo = (x - hi.astype(F32)).astype(BF16)
    return hi, lo


def _sigmoid_parts(z):
    e = jnp.exp(-jnp.abs(z))
    t = 1.0 + e
    return e, 1.0 / t, jnp.log(t)


def _sigmoid(x):
    return 0.5 * jnp.tanh(0.5 * x) + 0.5


def _my_place():
    return lax.axis_index("x"), lax.axis_index("y"), lax.axis_index("c")


def _flat_id(px, py, pc):
    return 4 * px + 2 * py + pc


def _peer(x, y, c, k):
    px = 1 - x if k & 4 else x
    py = 1 - y if k & 2 else y
    pc = 1 - c if k & 1 else c
    return px, py, pc


def _allgather(shards, name, in_vmem=False):
    n = len(shards)

    def body(*refs):
        x_refs, out_refs = refs[:n], refs[n:2 * n]
        send_sems, recv_sems, local_sems = refs[2 * n:]
        x, y, c = _my_place()
        me, sibling = (x, y, c), (x, y, 1 - c)
        chips = [(1 - x, y), (x, 1 - y), (1 - x, 1 - y)]

        def block(a, place):
            return out_refs[a].at[_flat_id(*place)]

        def copy(a, k, place, to, src=None):
            return pltpu.make_async_remote_copy(
                src_ref=block(a, place) if src is None else src, dst_ref=block(a, place),
                send_sem=send_sems.at[7 * a + k], recv_sem=recv_sems.at[7 * a + k], device_id=to, device_id_type=MESH)

        mine = [pltpu.make_async_copy(x_refs[a], block(a, me), local_sems.at[a]) for a in range(n)]
        for cp in mine:
            cp.start()
        first = []
        for a in range(n):
            first.append(copy(a, 0, me, sibling, src=x_refs[a]))
            first += [copy(a, 1 + j, me, (*chip, c), src=x_refs[a]) for j, chip in enumerate(chips)]
        for cp in first:
            cp.start()
        passed = []
        for j, chip in enumerate(chips):
            for a in range(n):
                copy(a, 1 + j, (*chip, c), me).wait_recv()
                passed.append(copy(a, 4 + j, (*chip, c), sibling))
                passed[-1].start()
        for a in range(n):
            copy(a, 0, sibling, me).wait_recv()
        for j, chip in enumerate(chips):
            for a in range(n):
                copy(a, 4 + j, (*chip, 1 - c), me).wait_recv()
        for cp in first + passed:
            cp.wait_send()
        for cp in mine:
            cp.wait()

    space = pltpu.VMEM if in_vmem else pl.ANY
    return pl.pallas_call(
        body, name=name,
        out_shape=[jax.ShapeDtypeStruct((NDEV,) + s.shape, s.dtype) for s in shards],
        in_specs=[pl.BlockSpec(memory_space=space)] * n,
        out_specs=[pl.BlockSpec(memory_space=space)] * n,
        scratch_shapes=[pltpu.SemaphoreType.DMA((7 * n,)), pltpu.SemaphoreType.DMA((7 * n,)), pltpu.SemaphoreType.DMA((n,))],
    )(*shards)


def _allsum_small(part, name):
    rows, cols = part.shape

    def body(p_ref, out_ref, buf, send_sems, recv_sems):
        x, y, c = _my_place()
        me = _flat_id(x, y, c)
        buf[me] = p_ref[...]
        copies = []
        for k in range(1, NDEV):
            copies.append(pltpu.make_async_remote_copy(
                src_ref=p_ref, dst_ref=buf.at[me], send_sem=send_sems.at[k - 1], recv_sem=recv_sems.at[k - 1],
                device_id=_peer(x, y, c, k), device_id_type=MESH))
        for cp in copies:
            cp.start()
        for cp in copies:
            cp.wait()
        total = buf[0]
        for j in range(1, NDEV):
            total = total + buf[j]
        out_ref[...] = total

    return pl.pallas_call(
        body, name=name,
        out_shape=jax.ShapeDtypeStruct((rows, cols), F32),
        in_specs=[pl.BlockSpec(memory_space=pltpu.VMEM)],
        out_specs=pl.BlockSpec(memory_space=pltpu.VMEM),
        scratch_shapes=[pltpu.VMEM((NDEV, rows, cols), F32),
                        pltpu.SemaphoreType.DMA((7,)), pltpu.SemaphoreType.DMA((7,))],
    )(part)


_HBM = pl.BlockSpec(memory_space=pltpu.HBM)
_SEM = pl.BlockSpec(memory_space=pltpu.SEMAPHORE)
_DATAFLOW = pltpu.SideEffectType.DATAFLOW_SIDE_EFFECTING


COPIES_PER_ARRAY = {"gather": 7, "exchange": 7, "first": 4, "forward": 3}


def _send_copies(plan, src_refs, land_refs, send_sems, recv_sems):
    x, y, c = _my_place()
    me = _flat_id(x, y, c)
    per = COPIES_PER_ARRAY[plan]
    copies = []

    def add(a, slot, src, dst, to):
        copies.append(pltpu.make_async_remote_copy(
            src_ref=src, dst_ref=dst, send_sem=send_sems.at[per * a + slot], recv_sem=recv_sems.at[per * a + slot],
            device_id=to, device_id_type=MESH))

    for a, land_ref in enumerate(land_refs):
        if plan in ("gather", "exchange"):
            for k in range(1, NDEV):
                peer = _peer(x, y, c, k)
                add(a, k - 1, src_refs[a].at[_flat_id(*peer)] if plan == "exchange" else src_refs[a], land_ref.at[me], peer)
        elif plan == "first":
            for slot, k in enumerate((1, 2, 4, 6)):
                add(a, slot, src_refs[a], land_ref.at[me], _peer(x, y, c, k))
        else:
            for slot, k in enumerate((2, 4, 6)):
                block = land_ref.at[_flat_id(*_peer(x, y, c, k))]
                add(a, slot, block, block, (x, y, 1 - c))
    return copies


def _send_start(srcs, lands, plan, after, name):
    ns, nl = len(srcs), len(lands)
    nsem = COPIES_PER_ARRAY[plan] * nl

    def body(*refs):
        for cp in _send_copies(plan, refs[:ns], refs[ns:ns + nl], refs[ns + nl + 1], refs[ns + nl + 2]):
            cp.start()
        refs[-1][...] = jnp.zeros_like(refs[-1])

    operands = [pltpu.with_memory_space_constraint(t, pltpu.HBM) for t in list(srcs) + list(lands)]
    outs = pl.pallas_call(
        body, name=name,
        out_shape=(pltpu.SemaphoreType.DMA((nsem,)), pltpu.SemaphoreType.DMA((nsem,)),
                   *[pltpu.HBM(t.shape, t.dtype) for t in operands[ns:]], jax.ShapeDtypeStruct((SMALL_ROWS, LANES), F32)),
        in_specs=[_HBM] * (ns + nl) + [pl.BlockSpec(memory_space=pl.ANY)],
        out_specs=(_SEM, _SEM, *[_HBM] * nl, pl.BlockSpec(memory_space=pltpu.VMEM)),
        input_output_aliases={ns + i: 2 + i for i in range(nl)},
        compiler_params=pltpu.CompilerParams(has_side_effects=_DATAFLOW),
    )(*operands, after)
    return (plan, outs[0], outs[1], operands[:ns], list(outs[2:2 + nl])), outs[-1]


def _send_wait(handle, after, name):
    plan, send_sems, recv_sems, srcs, lands = handle
    ns, nl = len(srcs), len(lands)

    def body(*refs):
        for cp in _send_copies(plan, refs[:ns], refs[ns:ns + nl], refs[ns + nl], refs[ns + nl + 1]):
            cp.wait_send()
            cp.wait_recv()

    outs = pl.pallas_call(
        body, name=name,
        out_shape=tuple(pltpu.HBM(t.shape, t.dtype) for t in lands),
        in_specs=[_HBM] * (ns + nl) + [_SEM, _SEM, pl.BlockSpec(memory_space=pl.ANY)],
        out_specs=tuple([_HBM] * nl),
        input_output_aliases={ns + i: i for i in range(nl)},
        compiler_params=pltpu.CompilerParams(has_side_effects=_DATAFLOW),
    )(*srcs, *lands, send_sems, recv_sems, after)
    return list(outs)


def _gather_two_level(first_handle, after, name):
    lands = _send_wait(first_handle, after, f"{name}_wait")
    forward, _ = _send_start([], lands, "forward", after, f"{name}_forward_start")
    return _send_wait(forward, after, f"{name}_forward_wait")


def _landing(own_block, me):
    return lax.dynamic_update_index_in_dim(lax.empty((NDEV,) + own_block.shape, own_block.dtype), own_block, me, 0)


def _spec(block, index_map):
    return pl.BlockSpec(block, index_map)


def _mm(grid, a, a_spec, bs, extras, outs, *, dims, acc_shape=None, epi=None, after=None, name):
    nk = grid[2]
    nb, ne, no = len(bs), len(extras), len(outs)
    nafter = 0 if after is None else 1
    if epi is None:
        epi = lambda accs, ex: [accs[0]]

    def body(*refs):
        a_ref, b_refs = refs[0], refs[1:1 + nb]
        e_refs = refs[1 + nb:1 + nb + ne]
        o_refs = refs[1 + nb + ne + nafter:1 + nb + ne + nafter + no]
        acc_refs = refs[1 + nb + ne + nafter + no:]
        def finish(accs):
            for o_ref, tile in zip(o_refs, epi(accs, [e_ref[...] for e_ref in e_refs])):
                o_ref[...] = tile.astype(o_ref.dtype)

        def product(b_ref):
            if len(a_ref.shape) == 2:
                return _dot(a_ref[...], b_ref[...], dims)
            total = _dot(a_ref[0], b_ref[0], dims)
            for t in range(1, a_ref.shape[0]):
                total = total + _dot(a_ref[t], b_ref[t], dims)
            return total

        if nk == 1:
            finish([product(b_ref) for b_ref in b_refs])
        else:
            k = pl.program_id(2)

            @pl.when(k == 0)
            def _():
                for acc_ref in acc_refs:
                    acc_ref[...] = jnp.zeros_like(acc_ref)

            for acc_ref, b_ref in zip(acc_refs, b_refs):
                acc_ref[...] += product(b_ref)

            @pl.when(k == nk - 1)
            def _():
                finish([acc_ref[...] for acc_ref in acc_refs])

    return pl.pallas_call(
        body, name=name,
        grid=grid,
        in_specs=[a_spec] + [s for _, s in bs] + [s for _, s in extras] + [pl.BlockSpec(memory_space=pl.ANY)] * nafter,
        out_specs=[s for _, _, s in outs],
        out_shape=[jax.ShapeDtypeStruct(shape, dt) for shape, dt, _ in outs],
        scratch_shapes=[pltpu.VMEM(acc_shape, F32) for _ in bs] if nk > 1 else [],
        compiler_params=pltpu.CompilerParams(dimension_semantics=("parallel", "parallel", "arbitrary")),
    )(a, *[b for b, _ in bs], *[e for e, _ in extras], *([after] if nafter else []))


def _mm2d(a, b, n_cols, *, dims, tm, tn, tk, out_dtype, epi=None, extras=(), after=None, name):
    m_rows, k_len = (a.shape[1], a.shape[0]) if dims == TN else a.shape
    assert m_rows % tm == 0 and n_cols % tn == 0 and k_len % tk == 0, (name, a.shape, n_cols, tm, tn, tk)
    a_spec = _spec((tk, tm), lambda i, j, k: (k, i)) if dims == TN else _spec((tm, tk), lambda i, j, k: (i, k))
    b_spec = _spec((tn, tk), lambda i, j, k: (j, k)) if dims == NT else _spec((tk, tn), lambda i, j, k: (k, j))
    tile = _spec((tm, tn), lambda i, j, k: (i, j))
    return _mm((m_rows // tm, n_cols // tn, k_len // tk), a, a_spec, [(b, b_spec)], [(e, tile) for e in extras],
               [((m_rows, n_cols), out_dtype, tile)], dims=dims, acc_shape=(tm, tn), epi=epi, after=after, name=name)[0]


def _rowwise(fn, ins, consts, outs, sums, *, tm, name, row_maps=None, after=None):
    m_rows = outs[0][2] if len(outs[0]) == 3 else ins[0][0].shape[0]
    n = m_rows // tm
    ni, nc, no = len(ins), len(consts), len(outs)
    nafter = 0 if after is None else 1
    row_maps = row_maps or [None] * ni

    def body(*refs):
        i = pl.program_id(0)
        in_tiles = [r[...] for r in refs[:ni]]
        const_values = [r[...] for r in refs[ni:ni + nc]]
        o_refs = refs[ni + nc + nafter:ni + nc + nafter + no]
        s_refs = refs[ni + nc + nafter + no:]
        out_tiles, sum_terms = fn(i, in_tiles, const_values)
        for o_ref, tile in zip(o_refs, out_tiles):
            o_ref[...] = tile.astype(o_ref.dtype)
        if s_refs:
            @pl.when(i == 0)
            def _():
                for s_ref in s_refs:
                    s_ref[...] = jnp.zeros_like(s_ref)

            for s_ref, term in zip(s_refs, sum_terms):
                s_ref[...] += term

    def in_spec(width, col, rmap):
        if rmap is None:
            return pl.BlockSpec((tm, width), lambda i: (i, col))
        return pl.BlockSpec((tm, width), lambda i: (rmap(i), col))

    return pl.pallas_call(
        body, name=name,
        grid=(n,),
        in_specs=[in_spec(w, col, rmap) for (_, w, col), rmap in zip(ins, row_maps)]
        + [pl.BlockSpec(cst.shape, lambda i: (0, 0)) for cst in consts] + [pl.BlockSpec(memory_space=pl.ANY)] * nafter,
        out_specs=[pl.BlockSpec((tm, o[0]), lambda i: (i, 0)) for o in outs]
        + [pl.BlockSpec(s, lambda i: (0, 0)) for s in sums],
        out_shape=[jax.ShapeDtypeStruct((m_rows, o[0]), o[1]) for o in outs]
        + [jax.ShapeDtypeStruct(s, F32) for s in sums],
        compiler_params=pltpu.CompilerParams(dimension_semantics=("arbitrary",)),
    )(*[arr for arr, _, _ in ins], *consts, *([after] if nafter else []))


def _whole(arr):
    return (arr, arr.shape[1], 0)


def _rms(x, gain):
    r = lax.rsqrt(jnp.mean(x * x, axis=-1, keepdims=True) + RMS_EPS)
    return x * r * gain


def _rms_bwd(x, gain, dy):
    r = lax.rsqrt(jnp.mean(x * x, axis=-1, keepdims=True) + RMS_EPS)
    u = dy * gain
    dx = r * u - x * (r * r * r) * jnp.mean(x * u, axis=-1, keepdims=True)
    return dx, dy * x * r


def _norm_fwd(h, gain, *, tm, name):
    def fn(i, tiles, consts):
        return [_rms(tiles[0], consts[0])], []
    return _rowwise(fn, [_whole(h)], [gain], [(h.shape[1], BF16)], [], tm=tm, name=name)[0]


def _norm_bwd(dn, h, gain, dh_in, *, tm, name, after=None):
    d = h.shape[1]

    def fn(i, tiles, consts):
        dx, dg_rows = _rms_bwd(tiles[1], consts[0], tiles[0])
        dh = tiles[2] + dx
        return [dh, dh, FFN_RESIDUAL_WEIGHT * dh], [jnp.sum(dg_rows, axis=0, keepdims=True)]

    return _rowwise(fn, [_whole(dn), _whole(h), _whole(dh_in)], [gain],
                    [(d, F32), (d, BF16), (d, BF16)], [(1, d)], tm=tm, name=name, after=after)


def _swiglu_epi(accs, ex):
    a, b = accs
    return [a, b, a * _sigmoid(a) * b]


def _swiglu_bwd_epi(accs, ex):
    ds = accs[0]
    a, b = ex[0].astype(F32), ex[1].astype(F32)
    sig = _sigmoid(a)
    silu = a * sig
    dsilu = sig * (1.0 + a * (1.0 - sig))
    return [jnp.stack([(ds * b * dsilu).astype(BF16), (ds * silu).astype(BF16)], axis=0)]


def _attn_mask(i, j, tile, pad, strict):
    row = i * tile + lax.broadcasted_iota(jnp.int32, (tile, tile), 0)
    col = j * tile + lax.broadcasted_iota(jnp.int32, (tile, tile), 1)
    causal = (col < row) if strict else (col <= row)
    return causal & ((col >= pad) | (row < pad))


FWD_GROUP = 4
BWD_GROUP = 2


def _walk_key_tiles(i, step, carry, widths, descending=False):
    diagonal = lambda j, c: step(j, c, True, 1)
    left = jnp.maximum(i - 1, 0)
    lo = 1
    if descending:
        carry = lax.fori_loop(jnp.maximum(i, 1), i + 1, diagonal, carry)
    else:
        carry = step(0, carry, True, 1)
    for nt in widths:
        count = left // nt
        if descending:
            top = lo + left
            carry = lax.fori_loop(0, count, lambda t, c, nt=nt, top=top: step(top - nt * (t + 1), c, False, nt), carry)
        else:
            carry = lax.fori_loop(0, count, lambda t, c, nt=nt, lo=lo: step(lo + nt * t, c, False, nt), carry)
            lo = lo + nt * count
        left = left - nt * count
    if descending:
        return step(0, carry, True, 1)
    return lax.fori_loop(jnp.maximum(i, 1), i + 1, diagonal, carry)


def _head_cols(g):
    return pl.ds(g * HEAD_DIM, HEAD_DIM)


def _key_rows(j, tile, nt):
    return pl.ds(pl.multiple_of(j * tile, tile), nt * tile)


def _key_tile(ref, j, tile, g, nt=1):
    return ref[_key_rows(j, tile, nt), _head_cols(g)]


def _key_bias(c_ref, g, j, nt):
    return c_ref[g, j] if nt == 1 else jnp.concatenate([c_ref[g, j + t] for t in range(nt)], axis=1)


def _head_specs(lp, tile, q_off, k_off, v_off, hp):
    gw = hp * HEAD_DIM
    q_spec = pl.BlockSpec((tile, gw), lambda h, i: (i, h + q_off // hp))
    k_spec = pl.BlockSpec((lp, gw), lambda h, i: (0, h + k_off // hp))
    v_spec = pl.BlockSpec((lp, gw), lambda h, i: (0, h + v_off // hp))
    return q_spec, k_spec, v_spec


def _fox_fwd(q, k, v, v_off, c_rows, *, tile, pad, name):
    lp = q.shape[0]
    nb = lp // tile
    hp = FWD_GROUP
    gw = hp * HEAD_DIM

    def body(q_ref, k_ref, v_ref, c_ref, o_ref, lse_ref):
        i = pl.program_id(1)
        qts = [q_ref[:, _head_cols(g)] for g in range(hp)]

        def step(j, carry, masked, nt):
            ok = _attn_mask(i, j, tile, pad, False) if masked else None
            out = []
            for g, (m, l, acc) in enumerate(carry):
                s = _dot(qts[g], _key_tile(k_ref, j, tile, g, nt), NT) * ATTN_SCALE - _key_bias(c_ref, g, j, nt)
                if masked:
                    s = jnp.where(ok, s, MASKED_LOGIT)
                m_new = jnp.maximum(m, jnp.max(s, axis=1, keepdims=True))
                p = jnp.exp(s - m_new)
                alpha = jnp.exp(m - m_new)
                l = alpha * l + jnp.sum(p, axis=1, keepdims=True)
                acc = alpha * acc + _dot(p.astype(BF16), _key_tile(v_ref, j, tile, g, nt), NN)
                out.append((m_new, l, acc))
            return tuple(out)

        init = (jnp.full((tile, 1), MASKED_LOGIT, F32), jnp.zeros((tile, 1), F32), jnp.zeros((tile, HEAD_DIM), F32))
        final = _walk_key_tiles(i, step, (init,) * hp, (4, 2, 1))
        for g, (m, l, acc) in enumerate(final):
            o_ref[:, _head_cols(g)] = (acc / l).astype(o_ref.dtype)
            lse_ref[g] = jnp.broadcast_to(m + jnp.log(l), (tile, LANES))

    q_spec, k_spec, v_spec = _head_specs(lp, tile, 0, 0, v_off, hp)
    return pl.pallas_call(
        body, name=name,
        grid=(HEADS // hp, nb),
        in_specs=[q_spec, k_spec, v_spec, pl.BlockSpec((hp, nb, 1, tile), lambda h, i: (h, 0, 0, 0))],
        out_specs=[pl.BlockSpec((tile, gw), lambda h, i: (i, h)),
                   pl.BlockSpec((hp, tile, LANES), lambda h, i: (h, i, 0))],
        out_shape=[jax.ShapeDtypeStruct((lp, BRANCH_WIDTH), BF16), jax.ShapeDtypeStruct((HEADS, lp, LANES), F32)],
        compiler_params=pltpu.CompilerParams(dimension_semantics=("parallel", "arbitrary")),
    )(q, k, v, c_rows)


def _fox_bwd(q, k, v, v_off, c_rows, o, do, lse, *, tile, pad, name):
    lp = q.shape[0]
    nb = lp // tile
    hp = BWD_GROUP
    gw = hp * HEAD_DIM

    def body(q_ref, k_ref, v_ref, c_ref, o_ref, do_ref, lse_ref, dq_ref, dk_ref, dv_ref, dc_ref, dk_acc, dv_acc, dc_acc):
        i = pl.program_id(1)

        @pl.when(i == 0)
        def _():
            dk_acc[...] = jnp.zeros_like(dk_acc)
            dv_acc[...] = jnp.zeros_like(dv_acc)
            dc_acc[...] = jnp.zeros_like(dc_acc)

        heads = range(hp)
        qts = [q_ref[:, _head_cols(g)] for g in heads]
        dots = [do_ref[:, _head_cols(g)] for g in heads]
        deltas = [jnp.sum(dots[g].astype(F32) * o_ref[:, _head_cols(g)].astype(F32), axis=1, keepdims=True) for g in heads]
        lse_cols = [lse_ref[g][:, :1] for g in heads]

        def step(j, dqs, masked, nt):
            rows = _key_rows(j, tile, nt)
            ok = _attn_mask(i, j, tile, pad, False) if masked else None
            out = []
            for g in heads:
                kt = _key_tile(k_ref, j, tile, g, nt)
                s = _dot(qts[g], kt, NT) * ATTN_SCALE - _key_bias(c_ref, g, j, nt)
                p = jnp.exp(s - lse_cols[g])
                if masked:
                    p = jnp.where(ok, p, 0.0)
                dp = _dot(dots[g], _key_tile(v_ref, j, tile, g, nt), NT)
                ds = p * (dp - deltas[g])
                dsb = ds.astype(BF16)
                dk_acc[rows, _head_cols(g)] += _dot(dsb, qts[g], TN)
                dv_acc[rows, _head_cols(g)] += _dot(p.astype(BF16), dots[g], TN)
                dc = -jnp.sum(ds, axis=0, keepdims=True)
                for t in range(nt):
                    dc_acc[g, j + t] += dc[:, t * tile:(t + 1) * tile]
                out.append(dqs[g] + _dot(dsb, kt, NN))
            return tuple(out)

        dqs = _walk_key_tiles(i, step, (jnp.zeros((tile, HEAD_DIM), F32),) * hp, (4, 2, 1))
        for g in heads:
            dq_ref[:, _head_cols(g)] = dqs[g] * ATTN_SCALE

        @pl.when(i == nb - 1)
        def _():
            dk_ref[...] = dk_acc[...] * ATTN_SCALE
            dv_ref[...] = dv_acc[...].astype(dv_ref.dtype)
            dc_ref[...] = dc_acc[...]

    q_spec, k_spec, v_spec = _head_specs(lp, tile, 0, 0, v_off, hp)
    tile_spec = pl.BlockSpec((tile, gw), lambda h, i: (i, h))
    head_spec = pl.BlockSpec((lp, gw), lambda h, i: (0, h))
    c_spec = pl.BlockSpec((hp, nb, 1, tile), lambda h, i: (h, 0, 0, 0))
    return pl.pallas_call(
        body, name=name,
        grid=(HEADS // hp, nb),
        in_specs=[q_spec, k_spec, v_spec, c_spec, tile_spec, tile_spec,
                  pl.BlockSpec((hp, tile, LANES), lambda h, i: (h, i, 0))],
        out_specs=[tile_spec, head_spec, head_spec, c_spec],
        out_shape=[jax.ShapeDtypeStruct((lp, BRANCH_WIDTH), F32), jax.ShapeDtypeStruct((lp, BRANCH_WIDTH), F32),
                   jax.ShapeDtypeStruct((lp, BRANCH_WIDTH), BF16), jax.ShapeDtypeStruct((HEADS, nb, 1, tile), F32)],
        scratch_shapes=[pltpu.VMEM((lp, gw), F32), pltpu.VMEM((lp, gw), F32),
                        pltpu.VMEM((hp, nb, 1, tile), F32)],
        compiler_params=pltpu.CompilerParams(dimension_semantics=("parallel", "arbitrary")),
    )(q, k, v, c_rows, o, do, lse)


def _later_matrix(tile):
    return (lax.broadcasted_iota(jnp.int32, (tile, tile), 0) > lax.broadcasted_iota(jnp.int32, (tile, tile), 1)).astype(BF16)


def _earlier_matrix(tile):
    return (lax.broadcasted_iota(jnp.int32, (tile, tile), 0) < lax.broadcasted_iota(jnp.int32, (tile, tile), 1)).astype(BF16)


def _running_sums(x, tri, suffix):
    tile = tri.shape[0]
    blocks = [x[:, b:b + tile] for b in range(0, x.shape[1], tile)]
    sums = [jnp.sum(blk, axis=1, keepdims=True) for blk in blocks]
    out = []
    for b, blk in enumerate(blocks):
        hi, lo = _split_bf16(blk)
        inside = _dot(hi, tri, NN) + _dot(lo, tri, NN)
        for other in (sums[b + 1:] if suffix else sums[:b]):
            inside = inside + other
        out.append(inside)
    total = sums[0]
    for other in sums[1:]:
        total = total + other
    return (out[0] if len(out) == 1 else jnp.concatenate(out, axis=1)), total


def _sb_tile(qt, kt, ok, later):
    z = _dot(qt, kt, NT) * ATTN_SCALE
    e, r, lg = _sigmoid_parts(z)
    sp = jnp.maximum(z, 0.0) + lg
    spm = sp if ok is None else jnp.where(ok, sp, 0.0)
    within, sp_here = _running_sums(spm, later, True)
    return z, e, r, sp, sp_here, within


def _sb_fwd(qkv, q_off, k_off, v_off, *, tile, pad, name):
    lp = qkv.shape[0]
    nb = lp // tile
    hp = FWD_GROUP
    gw = hp * HEAD_DIM

    def body(q_ref, k_ref, v_ref, o_ref, tot_ref):
        i = pl.program_id(1)
        qts = [q_ref[:, _head_cols(g)] for g in range(hp)]
        later = _later_matrix(tile)

        def step(j, carry, masked, nt):
            ok = _attn_mask(i, j, tile, pad, True) if masked else None
            out = []
            for g, (right, acc) in enumerate(carry):
                z, _, _, sp, sp_here, within = _sb_tile(qts[g], _key_tile(k_ref, j, tile, g, nt), ok, later)
                w = jnp.exp(z - sp - within - right)
                if masked:
                    w = jnp.where(ok, w, 0.0)
                acc = acc + _dot(w.astype(BF16), _key_tile(v_ref, j, tile, g, nt), NN)
                out.append((right + sp_here, acc))
            return tuple(out)

        init = (jnp.zeros((tile, 1), F32), jnp.zeros((tile, HEAD_DIM), F32))
        final = _walk_key_tiles(i, step, (init,) * hp, (4, 2, 1), descending=True)
        for g, (total, acc) in enumerate(final):
            o_ref[:, _head_cols(g)] = acc.astype(o_ref.dtype)
            tot_ref[g] = jnp.broadcast_to(total, (tile, LANES))

    q_spec, k_spec, v_spec = _head_specs(lp, tile, q_off, k_off, v_off, hp)
    return pl.pallas_call(
        body, name=name,
        grid=(HEADS // hp, nb),
        in_specs=[q_spec, k_spec, v_spec],
        out_specs=[pl.BlockSpec((tile, gw), lambda h, i: (i, h)),
                   pl.BlockSpec((hp, tile, LANES), lambda h, i: (h, i, 0))],
        out_shape=[jax.ShapeDtypeStruct((lp, BRANCH_WIDTH), BF16), jax.ShapeDtypeStruct((HEADS, lp, LANES), F32)],
        compiler_params=pltpu.CompilerParams(dimension_semantics=("parallel", "arbitrary")),
    )(qkv, qkv, qkv)


def _sb_bwd(qkv, q_off, k_off, v_off, do, total, *, tile, pad, name):
    lp = qkv.shape[0]
    nb = lp // tile
    hp = BWD_GROUP
    gw = hp * HEAD_DIM

    def body(q_ref, k_ref, v_ref, do_ref, tot_ref, dq_ref, dk_ref, dv_ref, dk_acc, dv_acc):
        i = pl.program_id(1)

        @pl.when(i == 0)
        def _():
            dk_acc[...] = jnp.zeros_like(dk_acc)
            dv_acc[...] = jnp.zeros_like(dv_acc)

        heads = range(hp)
        qts = [q_ref[:, _head_cols(g)] for g in heads]
        dots = [do_ref[:, _head_cols(g)] for g in heads]
        total_cols = [tot_ref[g][:, :1] for g in heads]
        later, earlier = _later_matrix(tile), _earlier_matrix(tile)

        def step(j, carry, masked, nt):
            rows = _key_rows(j, tile, nt)
            ok = _attn_mask(i, j, tile, pad, True) if masked else None
            out = []
            for g, (dq, sp_before, dlw_before) in enumerate(carry):
                kt = _key_tile(k_ref, j, tile, g, nt)
                z, e, r, sp, sp_here, within = _sb_tile(qts[g], kt, ok, later)
                right = total_cols[g] - sp_before - sp_here
                w = jnp.exp(z - sp - within - right)
                if masked:
                    w = jnp.where(ok, w, 0.0)
                dlw = w * _dot(dots[g], _key_tile(v_ref, j, tile, g, nt), NT)
                before, dlw_here = _running_sums(dlw, earlier, False)
                sig = jnp.where(z >= 0, r, e * r)
                dz = dlw * (1.0 - sig) - sig * (before + dlw_before)
                if masked:
                    dz = jnp.where(ok, dz, 0.0)
                dzb = dz.astype(BF16)
                dk_acc[rows, _head_cols(g)] += _dot(dzb, qts[g], TN)
                dv_acc[rows, _head_cols(g)] += _dot(w.astype(BF16), dots[g], TN)
                out.append((dq + _dot(dzb, kt, NN), sp_before + sp_here, dlw_before + dlw_here))
            return tuple(out)

        zero_col = jnp.zeros((tile, 1), F32)
        final = _walk_key_tiles(i, step, ((jnp.zeros((tile, HEAD_DIM), F32), zero_col, zero_col),) * hp, (4, 2, 1))
        for g in heads:
            dq_ref[:, _head_cols(g)] = (final[g][0] * ATTN_SCALE).astype(dq_ref.dtype)

        @pl.when(i == nb - 1)
        def _():
            dk_ref[...] = (dk_acc[...] * ATTN_SCALE).astype(dk_ref.dtype)
            dv_ref[...] = dv_acc[...].astype(dv_ref.dtype)

    q_spec, k_spec, v_spec = _head_specs(lp, tile, q_off, k_off, v_off, hp)
    tile_spec = pl.BlockSpec((tile, gw), lambda h, i: (i, h))
    head_spec = pl.BlockSpec((lp, gw), lambda h, i: (0, h))
    return pl.pallas_call(
        body, name=name,
        grid=(HEADS // hp, nb),
        in_specs=[q_spec, k_spec, v_spec, tile_spec, pl.BlockSpec((hp, tile, LANES), lambda h, i: (h, i, 0))],
        out_specs=[tile_spec, head_spec, head_spec],
        out_shape=[jax.ShapeDtypeStruct((lp, BRANCH_WIDTH), BF16)] * 3,
        scratch_shapes=[pltpu.VMEM((lp, gw), F32), pltpu.VMEM((lp, gw), F32)],
        compiler_params=pltpu.CompilerParams(dimension_semantics=("parallel", "arbitrary")),
    )(qkv, qkv, qkv, do, total)


def _cumsum_rows(x, *, tile, reverse, name):
    lp = x.shape[0]
    nb = lp // tile

    def body(x_ref, o_ref, carry):
        @pl.when(pl.program_id(0) == 0)
        def _():
            carry[...] = jnp.zeros_like(carry)

        r = lax.broadcasted_iota(jnp.int32, (tile, tile), 0)
        c = lax.broadcasted_iota(jnp.int32, (tile, tile), 1)
        tri = ((c >= r) if reverse else (c <= r)).astype(BF16)
        hi, lo = _split_bf16(x_ref[...])
        run = _dot(tri, hi, NN) + _dot(tri, lo, NN) + carry[...]
        o_ref[...] = run
        carry[...] = run[:1, :] if reverse else run[tile - 1:, :]

    order = (lambda i: (nb - 1 - i, 0)) if reverse else (lambda i: (i, 0))
    return pl.pallas_call(
        body, name=name,
        grid=(nb,),
        in_specs=[pl.BlockSpec((tile, LANES), order)],
        out_specs=pl.BlockSpec((tile, LANES), order),
        out_shape=jax.ShapeDtypeStruct((lp, LANES), F32),
        scratch_shapes=[pltpu.VMEM((1, LANES), F32)],
        compiler_params=pltpu.CompilerParams(dimension_semantics=("arbitrary",)),
    )(x)


def _log_sigmoid(x):
    return jnp.minimum(x, 0.0) - jnp.log(1.0 + jnp.exp(-jnp.abs(x)))


def _forget_mask(i, tm, pad):
    row = i * tm + lax.broadcasted_iota(jnp.int32, (tm, LANES), 0)
    lane = lax.broadcasted_iota(jnp.int32, (tm, LANES), 1)
    return (row >= pad) & (lane < HEADS)


def _fox_prep(proj_a, q_gain, k_gain, b_forget, *, tm, pad, name):
    w = BRANCH_WIDTH

    def fn(i, tiles, consts):
        pa = tiles[0]
        qs, ks = [], []
        for h in range(HEADS):
            lo = h * HEAD_DIM
            qs.append(_rms(pa[:, lo:lo + HEAD_DIM], consts[0][:, lo:lo + HEAD_DIM]))
            ks.append(_rms(pa[:, w + lo:w + lo + HEAD_DIM], consts[1][:, lo:lo + HEAD_DIM]))
        logf = jnp.where(_forget_mask(i, tm, pad), _log_sigmoid(pa[:, 2 * w:] + consts[2]), 0.0)
        return [jnp.concatenate(qs, axis=1), jnp.concatenate(ks, axis=1), logf], []

    return _rowwise(fn, [_whole(proj_a)], [q_gain, k_gain, b_forget],
                    [(w, BF16), (w, BF16), (LANES, F32)], [], tm=tm, name=name)


def _fox_prep_bwd(proj_a, dq, dk, dlogf, q_gain, k_gain, b_forget, *, tm, pad, name):
    w = BRANCH_WIDTH

    def fn(i, tiles, consts):
        pa, dqt, dkt, dlf = tiles
        dxs_q, dxs_k, dgs_q, dgs_k = [], [], [], []
        for h in range(HEADS):
            lo = h * HEAD_DIM
            dx, dg = _rms_bwd(pa[:, lo:lo + HEAD_DIM], consts[0][:, lo:lo + HEAD_DIM], dqt[:, lo:lo + HEAD_DIM])
            dxs_q.append(dx)
            dgs_q.append(jnp.sum(dg, axis=0, keepdims=True))
            dx, dg = _rms_bwd(pa[:, w + lo:w + lo + HEAD_DIM], consts[1][:, lo:lo + HEAD_DIM], dkt[:, lo:lo + HEAD_DIM])
            dxs_k.append(dx)
            dgs_k.append(jnp.sum(dg, axis=0, keepdims=True))
        xf = pa[:, 2 * w:] + consts[2]
        e, r, _ = _sigmoid_parts(xf)
        df = jnp.where(_forget_mask(i, tm, pad), dlf * jnp.where(xf >= 0, e * r, r), 0.0)
        return ([jnp.concatenate(dxs_q + dxs_k + [df], axis=1)],
                [jnp.concatenate(dgs_q, axis=1), jnp.concatenate(dgs_k, axis=1), jnp.sum(df, axis=0, keepdims=True)])

    return _rowwise(fn, [_whole(proj_a), _whole(dq), _whole(dk), _whole(dlogf)], [q_gain, k_gain, b_forget],
                    [(2 * w + LANES, BF16)], [(1, w), (1, w), (1, LANES)], tm=tm, name=name)


def _adamw_math(w, g, m, v):
    m = ADAM_B1 * m + (1.0 - ADAM_B1) * g
    v = ADAM_B2 * v + (1.0 - ADAM_B2) * (g * g)
    m_hat = m / (1.0 - ADAM_B1 ** ADAM_STEP)
    v_hat = v / (1.0 - ADAM_B2 ** ADAM_STEP)
    delta = -ADAM_LR * (m_hat / (jnp.sqrt(v_hat) + ADAM_EPS) + ADAM_WD * w)
    return delta, m, v


def _adamw_summed(parts, sel, w, m, v, *, name):
    rows, cols = w.shape
    tr = _pick(rows, [t for t in (512, 256, 128, 64, 32, 16, 8) if t * cols <= ADAMW_TILE_ELEMS])

    def body(p_ref, w_ref, m_ref, v_ref, g_out, d_out, m_out, v_out):
        g = p_ref[0].astype(F32)
        for j in range(1, NDEV):
            g = g + p_ref[j].astype(F32)
        delta, m_new, v_new = _adamw_math(w_ref[...], g, m_ref[...], v_ref[...])
        g_out[...] = g
        d_out[...] = delta
        m_out[...] = m_new
        v_out[...] = v_new

    spec = pl.BlockSpec((tr, cols), lambda i: (i, 0))
    if parts.ndim == 3:
        p_spec = pl.BlockSpec((NDEV, tr, cols), lambda i: (0, i, 0))
    else:
        p_spec = pl.BlockSpec((NDEV, None, tr, cols), lambda i: (0, sel, i, 0))
    return pl.pallas_call(
        body, name=name,
        grid=(rows // tr,),
        in_specs=[p_spec, spec, spec, spec],
        out_specs=[spec] * 4,
        out_shape=[jax.ShapeDtypeStruct((rows, cols), F32)] * 4,
        compiler_params=pltpu.CompilerParams(dimension_semantics=("parallel",)),
    )(parts, w, m, v)


def _adamw_plain(g, w, m, v, *, name):
    def body(g_ref, w_ref, m_ref, v_ref, d_out, m_out, v_out):
        delta, m_new, v_new = _adamw_math(w_ref[...], g_ref[...], m_ref[...], v_ref[...])
        d_out[...] = delta
        m_out[...] = m_new
        v_out[...] = v_new

    return pl.pallas_call(body, name=name, out_shape=[jax.ShapeDtypeStruct(w.shape, F32)] * 3)(g, w, m, v)


def _pack_rows(arrays, width, row_align):
    pieces, spans, at = [], [], 0
    for arr in arrays:
        flat = arr.reshape(-1)
        rows = -(-flat.shape[0] // (width * row_align)) * row_align
        flat = jnp.pad(flat, (0, rows * width - flat.shape[0]))
        pieces.append(flat.reshape(rows, width))
        spans.append((at, rows))
        at += rows
    return jnp.concatenate(pieces, axis=0), spans


def _unpack(rows2d, span, shape):
    at, rows = span
    size = 1
    for s in shape:
        size *= s
    return rows2d[at:at + rows].reshape(-1)[:size].reshape(shape)


def _join_cols(blocks):
    n, rows, cols = blocks.shape
    return jnp.transpose(blocks, (1, 0, 2)).reshape(rows, n * cols)


def _split_cols(full):
    rows, cols = full.shape
    return jnp.transpose(full.reshape(rows, NDEV, cols // NDEV), (1, 0, 2))


def _ffn_up(h, gain, w_gu, after, *, tm, tag):
    lp, d = h.shape
    f8 = w_gu.shape[3]
    n = _norm_fwd(h, gain, tm=_pick(lp, [256, 128]), name=f"{tag}_norm")
    hid = _spec((None, tm, f8), lambda i, j, k: (j, i, 0))
    a, b, s = _mm((lp // tm, NDEV, 1), n, _spec((tm, d), lambda i, j, k: (i, 0)),
                  [(w_gu, _spec((None, None, d, f8), lambda i, j, k: (j, 0, 0, 0))),
                   (w_gu, _spec((None, None, d, f8), lambda i, j, k: (j, 1, 0, 0)))],
                  [], [((NDEV, lp, f8), BF16, hid)] * 3, dims=NN, epi=_swiglu_epi, after=after, name=f"{tag}_up")
    return n, a, b, s


def _ffn_down(h, s, w_down, *, tm, tag):
    lp, d = h.shape
    f8 = w_down.shape[1]
    tn = _pick(d, [1024, 512, 256, 128])
    tile = _spec((tm, tn), lambda i, j, k: (i, j))
    return _mm((lp // tm, d // tn, NDEV // KSUB), s, _spec((KSUB, tm, f8), lambda i, j, k: (k, i, 0)),
               [(w_down, _spec((KSUB, f8, tn), lambda i, j, k: (k, 0, j)))], [(h, tile)], [((lp, d), F32, tile)],
               dims=NN, acc_shape=(tm, tn), epi=lambda accs, ex: [ex[0] + FFN_RESIDUAL_WEIGHT * accs[0]], name=f"{tag}_down")[0]


def _ffn_bwd_dw(dh_half, saved, w_down, on_down, *, tm, tag):
    n, a, b, s = saved
    lp, d = dh_half.shape
    f8 = w_down.shape[1]
    tkr = _pick(lp, [2176, 1088, 544, 256, 128])
    tn = _pick(d, [1024, 512, 256, 128])
    hid = _spec((None, tm, f8), lambda i, j, k: (j, i, 0))
    dab, = _mm((lp // tm, NDEV, 1), dh_half, _spec((tm, d), lambda i, j, k: (i, 0)),
               [(w_down, _spec((None, f8, d), lambda i, j, k: (j, 0, 0)))], [(a, hid), (b, hid)],
               [((2, NDEV, lp, f8), BF16, _spec((2, None, tm, f8), lambda i, j, k: (0, j, i, 0)))],
               dims=NT, epi=_swiglu_bwd_epi, name=f"{tag}_down_dx")
    dw_down, = _mm((NDEV, d // tn, lp // tkr), s, _spec((None, tkr, f8), lambda i, j, k: (i, k, 0)),
                   [(dh_half, _spec((tkr, tn), lambda i, j, k: (k, j)))], [],
                   [((NDEV, f8, d), BF16, _spec((None, f8, tn), lambda i, j, k: (i, 0, j)))],
                   dims=TN, acc_shape=(f8, tn), name=f"{tag}_down_dw")
    dw_gu, = _mm((d // tn, 2 * NDEV, lp // tkr), n, _spec((tkr, tn), lambda i, j, k: (k, i)),
                 [(dab, _spec((None, None, tkr, f8), lambda i, j, k: (j // NDEV, j % NDEV, k, 0)))], [],
                 [((NDEV, 2, d, f8), BF16, _spec((None, None, tn, f8), lambda i, j, k: (j % NDEV, j // NDEV, i, 0)))],
                 dims=TN, acc_shape=(tn, f8), after=on_down(dw_down), name=f"{tag}_gate_up_dw")
    return dab, dw_gu, dw_down


def _ffn_bwd_dx(dab, w_gu, after, *, tm, tag):
    lp, f8 = dab.shape[2], dab.shape[3]
    d = w_gu.shape[2]
    tn = _pick(d, [1024, 512, 256, 128])
    nsub = NDEV // KSUB
    return _mm((lp // tm, d // tn, 2 * nsub), dab, _spec((None, KSUB, tm, f8), lambda i, j, k: (k // nsub, k % nsub, i, 0)),
               [(w_gu, _spec((KSUB, None, tn, f8), lambda i, j, k: (k % nsub, k // nsub, j, 0)))], [],
               [((lp, d), F32, _spec((tm, tn), lambda i, j, k: (i, j)))],
               dims=NT, acc_shape=(tm, tn), after=after, name=f"{tag}_gate_up_dx")[0]


def kernel(x, meta_tokens, ffn1_norm, ffn1_w_gate, ffn1_w_up, ffn1_w_down, mix_norm, w_in, b_forget, fox_q_norm, fox_k_norm, w_branch_fox, w_branch_sb, w_out, ffn2_norm, ffn2_w_gate, ffn2_w_up, ffn2_w_down, loss_target, m_meta_tokens, m_ffn1_norm, m_ffn1_w_gate, m_ffn1_w_up, m_ffn1_w_down, m_mix_norm, m_w_in, m_b_forget, m_fox_q_norm, m_fox_k_norm, m_w_branch_fox, m_w_branch_sb, m_w_out, m_ffn2_norm, m_ffn2_w_gate, m_ffn2_w_up, m_ffn2_w_down, v_meta_tokens, v_ffn1_norm, v_ffn1_w_gate, v_ffn1_w_up, v_ffn1_w_down, v_mix_norm, v_w_in, v_b_forget, v_fox_q_norm, v_fox_k_norm, v_w_branch_fox, v_w_branch_sb, v_w_out, v_ffn2_norm, v_ffn2_w_gate, v_ffn2_w_up, v_ffn2_w_down):
    seq, d = x.shape[1], x.shape[2]
    d8 = d // NDEV
    w = BRANCH_WIDTH
    tile = 256 if seq % 256 == 0 else 128
    pad = tile - N_META
    lp = tile + seq
    tm = _pick(lp, [1088, 544, 256, 128])
    tr = _pick(tile, [256, 128])
    tkr = _pick(lp, [2176, 1088, 544, 256, 128])
    nb = lp // tile
    me = _flat_id(*_my_place())

    shards = [jnp.stack([ffn1_w_gate[0], ffn1_w_up[0]]).astype(BF16), ffn1_w_down[0].astype(BF16), w_in[0].astype(BF16),
              w_branch_fox[0].astype(BF16), w_branch_sb[0].astype(BF16), w_out[0].astype(BF16),
              jnp.stack([ffn2_w_gate[0], ffn2_w_up[0]]).astype(BF16), ffn2_w_down[0].astype(BF16)]
    w_gu1, = _allgather(shards[:1], "gather_ffn1")
    down1_copies, token = _send_start(shards[1:2], [_landing(s, me) for s in shards[1:2]], "first", w_gu1, "gather_down1_start")
    w_in_copies, token = _send_start(shards[2:3], [_landing(s, me) for s in shards[2:3]], "first", token, "gather_w_in_start")
    mixer_copies, token = _send_start(shards[3:6], [_landing(s, me) for s in shards[3:6]], "gather", token, "gather_mixer_start")
    ffn2_copies, token = _send_start(shards[6:], [_landing(s, me) for s in shards[6:]], "gather", token, "gather_ffn2_start")
    meta_full = _join_cols(_allgather([meta_tokens], "gather_meta", in_vmem=True)[0])

    h0 = jnp.concatenate([jnp.zeros((pad, d), F32), meta_full.astype(F32), x[0]], axis=0)
    saved1 = _ffn_up(h0, ffn1_norm, w_gu1, token, tm=tm, tag="ffn1")
    w_down1, = _gather_two_level(down1_copies, saved1[3], "gather_down1")
    h1 = _ffn_down(h0, saved1[3], w_down1, tm=tm, tag="ffn1")

    w_in_blocks, = _gather_two_level(w_in_copies, h1, "gather_w_in")
    wi = _join_cols(w_in_blocks)
    w_pa = jnp.concatenate([wi[:, :2 * w], jnp.pad(wi[:, 3 * w:3 * w + HEADS], ((0, 0), (0, LANES - HEADS)))], axis=1)
    w_pb = jnp.concatenate([wi[:, 2 * w:3 * w], wi[:, 3 * w + HEADS:]], axis=1)
    na, nbw = w_pa.shape[1], w_pb.shape[1]
    gate_blk = 4 * w // d

    n2 = _norm_fwd(h1, mix_norm, tm=tr, name="mix_norm")
    tma = _pick(lp, [544, 256, 128])
    tnd = _pick(d, [1024, 512, 256, 128])
    tnb = _pick(nbw, [512, 256, 128])
    proj_a = _mm2d(n2, w_pa, na, dims=NN, tm=tma, tn=na, tk=d, out_dtype=F32, name="proj_a")
    proj_b = _mm2d(n2, w_pb, nbw, dims=NN, tm=tm, tn=tnb, tk=d, out_dtype=BF16, name="proj_b")
    b_pad = jnp.pad(b_forget, ((0, 0), (0, LANES - HEADS)))
    q_gain, k_gain = fox_q_norm.reshape(1, w), fox_k_norm.reshape(1, w)
    fq, fk, logf = _fox_prep(proj_a, q_gain, k_gain, b_pad, tm=tr, pad=pad, name="fox_prep")
    c = _cumsum_rows(logf, tile=tile, reverse=False, name="forget_cumsum")
    c_rows = jnp.transpose(c[:, :HEADS]).reshape(HEADS, nb, 1, tile)
    o_fox, lse = _fox_fwd(fq, fk, proj_b, 0, c_rows, tile=tile, pad=pad, name="fox_fwd")
    o_sb, sb_total = _sb_fwd(proj_b, HEADS, 2 * HEADS, 3 * HEADS, tile=tile, pad=pad, name="sb_fwd")
    w_br_fox, w_br_sb, w_out_blocks = _send_wait(mixer_copies, o_sb, "gather_mixer_wait")
    w_out_full = w_out_blocks.reshape(d, d)

    def branch(o, w_blocks, name):
        return _mm((lp // tm, NDEV, 1), o, _spec((tm, w), lambda i, j, k: (i, 0)),
                   [(w_blocks, _spec((None, w, d8), lambda i, j, k: (j, 0, 0)))], [],
                   [((lp, d), BF16, _spec((tm, d8), lambda i, j, k: (i, j)))], dims=NN, name=name)[0]

    br_fox = branch(o_fox, w_br_fox, "branch_fox")
    br_sb = branch(o_sb, w_br_sb, "branch_sb")

    def merge_fn(i, tiles, consts):
        bf_, bs_, gf_, gs_ = [t.astype(F32) for t in tiles]
        return [_sigmoid(gf_) * bf_ + _sigmoid(gs_) * bs_], []

    gates_in = [(proj_b, d, gate_blk), (proj_b, d, gate_blk + 1)]
    merged, = _rowwise(merge_fn, [_whole(br_fox), _whole(br_sb)] + gates_in, [], [(d, BF16)], [], tm=tr, name="merge")
    h2 = _mm2d(merged, w_out_full, d, dims=NN, tm=tm, tn=tnd, tk=d, out_dtype=F32,
               epi=lambda accs, ex: [ex[0] + accs[0]], extras=[h1], name="out_proj")

    w_gu2, w_down2 = _send_wait(ffn2_copies, h2, "gather_ffn2_wait")
    saved3 = _ffn_up(h2, ffn2_norm, w_gu2, None, tm=tm, tag="ffn2")
    h3 = _ffn_down(h2, saved3[3], w_down2, tm=tm, tag="ffn2")

    skip = tile // tr

    def loss_fn(i, tiles, consts):
        real = i >= skip
        err = jnp.where(real, tiles[0] - tiles[1], 0.0)
        dy = err * (1.0 / d)
        part = 0.5 * jnp.sum(err * dy, axis=0, keepdims=True)
        return [dy, FFN_RESIDUAL_WEIGHT * dy], [part]

    dh3, dh3_half, loss_cols = _rowwise(
        loss_fn, [_whole(h3), _whole(loss_target[0])], [], [(d, F32, lp), (d, BF16, lp)], [(1, d)], tm=tr, name="loss",
        row_maps=[None, lambda i: jnp.maximum(i - skip, 0)])

    def own(g):
        return lax.dynamic_index_in_dim(g, me, 0, keepdims=False)

    def send_grads(grads_, name):
        return _send_start(grads_, [_landing(own(g), me) for g in grads_], "exchange", grads_[-1], name)

    sends = {}

    def send_piece(key):
        def on_ready(*gs):
            sends[key], token_ = send_grads(list(gs), f"exchange_{key}_start")
            return token_
        return on_ready

    dab3, dw_gu2, _ = _ffn_bwd_dw(dh3_half, saved3, w_down2, send_piece("down2"), tm=tm, tag="ffn2")
    dn3 = _ffn_bwd_dx(dab3, w_gu2, send_piece("gu2")(dw_gu2), tm=tm, tag="ffn2")
    dh2, dh2_bf, _, dg_ffn2 = _norm_bwd(dn3, h2, ffn2_norm, dh3, tm=tr, name="ffn2_norm_bwd")

    dmerged = _mm2d(dh2_bf, w_out_full, d, dims=NT, tm=tm, tn=tnd, tk=d, out_dtype=BF16, name="out_proj_dx")
    dw_out = _mm2d(merged, dh2_bf, d, dims=TN, tm=tnd, tn=tnd, tk=tkr, out_dtype=BF16, name="out_proj_dw")
    token = send_piece("out")(dw_out.reshape(NDEV, d8, d))

    def merge_bwd_fn(i, tiles, consts):
        dm, bf_, bs_, gf_, gs_ = [t.astype(F32) for t in tiles]
        sf, ss = _sigmoid(gf_), _sigmoid(gs_)
        return [dm * sf, dm * ss, dm * bf_ * sf * (1.0 - sf), dm * bs_ * ss * (1.0 - ss)], []

    dbr_fox, dbr_sb, dg_fox, dg_sb = _rowwise(
        merge_bwd_fn, [_whole(dmerged), _whole(br_fox), _whole(br_sb)] + gates_in, [], [(d, BF16)] * 4, [], tm=tr, after=token,
        name="merge_bwd")

    tnw = _pick(w, [512, 256, 128])

    def branch_dx(dbr, w_blocks, after, name):
        return _mm((lp // tm, w // tnw, NDEV), dbr, _spec((tm, d8), lambda i, j, k: (i, k)),
                   [(w_blocks, _spec((None, tnw, d8), lambda i, j, k: (k, j, 0)))], [],
                   [((lp, w), BF16, _spec((tm, tnw), lambda i, j, k: (i, j)))], dims=NT, acc_shape=(tm, tnw), after=after,
                   name=name)[0]

    def branch_dw(o, dbr, name):
        return _mm((w // tnw, NDEV, lp // tkr), o, _spec((tkr, tnw), lambda i, j, k: (k, i)),
                   [(dbr, _spec((tkr, d8), lambda i, j, k: (k, j)))], [],
                   [((NDEV, w, d8), BF16, _spec((None, tnw, d8), lambda i, j, k: (j, i, 0)))],
                   dims=TN, acc_shape=(tnw, d8), name=name)[0]

    token = send_piece("branch")(branch_dw(o_fox, dbr_fox, "branch_fox_dw"), branch_dw(o_sb, dbr_sb, "branch_sb_dw"))
    do_fox = branch_dx(dbr_fox, w_br_fox, token, "branch_fox_dx")
    do_sb = branch_dx(dbr_sb, w_br_sb, token, "branch_sb_dx")

    dfq, dfk, dfv, dc_rows = _fox_bwd(fq, fk, proj_b, 0, c_rows, o_fox, do_fox, lse, tile=tile, pad=pad, name="fox_bwd")
    dsq, dsk, dsv = _sb_bwd(proj_b, HEADS, 2 * HEADS, 3 * HEADS, do_sb, sb_total, tile=tile, pad=pad, name="sb_bwd")
    dc = jnp.pad(jnp.transpose(dc_rows.reshape(HEADS, lp)), ((0, 0), (0, LANES - HEADS)))
    dlogf = _cumsum_rows(dc, tile=tile, reverse=True, name="forget_cumsum_bwd")
    dproj_a, dg_q, dg_k, dg_b = _fox_prep_bwd(proj_a, dfq, dfk, dlogf, q_gain, k_gain, b_pad, tm=tr, pad=pad, name="fox_prep_bwd")
    dproj_b = jnp.concatenate([dfv, dsq, dsk, dsv, dg_fox, dg_sb], axis=1)

    dw_pa = _mm2d(n2, dproj_a, na, dims=TN, tm=tnd, tn=na, tk=_pick(lp, [544, 256, 128]), out_dtype=BF16, name="proj_a_dw")
    dw_pb = _mm2d(n2, dproj_b, nbw, dims=TN, tm=tnd, tn=tnb, tk=tkr, out_dtype=BF16, name="proj_b_dw")
    dw_in = jnp.concatenate([dw_pa[:, :2 * w], dw_pb[:, :w], dw_pa[:, 2 * w:2 * w + HEADS], dw_pb[:, w:]], axis=1)
    token = send_piece("w_in")(_split_cols(dw_in))
    dn2_a = _mm2d(dproj_a, w_pa, d, dims=NT, tm=tm, tn=tnd, tk=na, out_dtype=F32, after=token, name="proj_a_dx")
    dn2 = _mm2d(dproj_b, w_pb, d, dims=NT, tm=tm, tn=tnd, tk=_pick(nbw, [2048, 1024, 512, 256, 128]), out_dtype=F32,
                epi=lambda accs, ex: [accs[0] + ex[0]], extras=[dn2_a], name="proj_b_dx")
    dh1, _, dh1_half, dg_mix = _norm_bwd(dn2, h1, mix_norm, dh2, tm=tr, name="mix_norm_bwd")

    dab1, dw_gu1, _ = _ffn_bwd_dw(dh1_half, saved1, w_down1, send_piece("down1"), tm=tm, tag="ffn1")
    dn1 = _ffn_bwd_dx(dab1, w_gu1, send_piece("gu1")(dw_gu1), tm=tm, tag="ffn1")
    dh0, _, _, dg_ffn1 = _norm_bwd(dn1, h0, ffn1_norm, dh1, tm=tr, name="ffn1_norm_bwd")
    grad_x = dh0[tile:][None]

    r_down2, = _send_wait(sends["down2"], dh0, "exchange_down2_wait")
    r_gu2, = _send_wait(sends["gu2"], r_down2, "exchange_gu2_wait")
    r_out, = _send_wait(sends["out"], r_gu2, "exchange_out_wait")
    r_br_fox, r_br_sb = _send_wait(sends["branch"], r_out, "exchange_branch_wait")
    r_in, = _send_wait(sends["w_in"], r_br_sb, "exchange_w_in_wait")
    grads, deltas, new_ms, new_vs = {}, {}, {}, {}

    def adamw_big(entries):
        for k, (parts, sel, wt, mt, vt) in entries.items():
            g, dl, mn, vn = _adamw_summed(parts, sel, wt[0], mt[0], vt[0], name=f"adamw_{k}")
            grads[k], deltas[k], new_ms[k], new_vs[k] = g[None], dl[None], mn[None], vn[None]

    adamw_big(dict(ffn2_w_gate=(r_gu2, 0, ffn2_w_gate, m_ffn2_w_gate, v_ffn2_w_gate),
                   ffn2_w_up=(r_gu2, 1, ffn2_w_up, m_ffn2_w_up, v_ffn2_w_up),
                   ffn2_w_down=(r_down2, 0, ffn2_w_down, m_ffn2_w_down, v_ffn2_w_down),
                   w_in=(r_in, 0, w_in, m_w_in, v_w_in),
                   w_branch_fox=(r_br_fox, 0, w_branch_fox, m_w_branch_fox, v_w_branch_fox),
                   w_branch_sb=(r_br_sb, 0, w_branch_sb, m_w_branch_sb, v_w_branch_sb),
                   w_out=(r_out, 0, w_out, m_w_out, v_w_out)))
    done = sum(v[0, 0, :1] for v in new_vs.values())
    r_down1, = _send_wait(sends["down1"], done, "exchange_down1_wait")
    r_gu1, = _send_wait(sends["gu1"], r_down1, "exchange_gu1_wait")
    adamw_big(dict(ffn1_w_gate=(r_gu1, 0, ffn1_w_gate, m_ffn1_w_gate, v_ffn1_w_gate),
                   ffn1_w_up=(r_gu1, 1, ffn1_w_up, m_ffn1_w_up, v_ffn1_w_up),
                   ffn1_w_down=(r_down1, 0, ffn1_w_down, m_ffn1_w_down, v_ffn1_w_down)))

    small_parts = [dh0[pad:tile], dg_ffn1, dg_mix, dg_ffn2, dg_b[:, :HEADS], dg_q, dg_k, loss_cols]
    small_packed, small_spans = _pack_rows(small_parts, LANES, SMALL_ROWS)
    small_sum = _allsum_small(small_packed, "sum_small")
    g_meta_full, g_ffn1n, g_mixn, g_ffn2n, g_bf, g_qn, g_kn, loss_vec = [
        _unpack(small_sum, span, part.shape) for span, part in zip(small_spans, small_parts)]
    loss = jnp.sum(loss_vec)
    g_meta = lax.dynamic_slice_in_dim(g_meta_full, me * d8, d8, axis=1)
    g_qn, g_kn = g_qn.reshape(fox_q_norm.shape), g_kn.reshape(fox_k_norm.shape)

    small = dict(meta_tokens=(g_meta, meta_tokens, m_meta_tokens, v_meta_tokens),
                 ffn1_norm=(g_ffn1n, ffn1_norm, m_ffn1_norm, v_ffn1_norm),
                 mix_norm=(g_mixn, mix_norm, m_mix_norm, v_mix_norm),
                 b_forget=(g_bf, b_forget, m_b_forget, v_b_forget),
                 fox_q_norm=(g_qn, fox_q_norm, m_fox_q_norm, v_fox_q_norm),
                 fox_k_norm=(g_kn, fox_k_norm, m_fox_k_norm, v_fox_k_norm),
                 ffn2_norm=(g_ffn2n, ffn2_norm, m_ffn2_norm, v_ffn2_norm))
    for k, (g, wt, mt, vt) in small.items():
        flat = lambda t: t.reshape(-1, t.shape[-1])
        dl, mn, vn = _adamw_plain(flat(g), flat(wt), flat(mt), flat(vt), name=f"adamw_{k}")
        grads[k], deltas[k], new_ms[k], new_vs[k] = g, dl.reshape(wt.shape), mn.reshape(wt.shape), vn.reshape(wt.shape)

    order = ["meta_tokens", "ffn1_norm", "ffn1_w_gate", "ffn1_w_up", "ffn1_w_down", "mix_norm", "w_in", "b_forget",
             "fox_q_norm", "fox_k_norm", "w_branch_fox", "w_branch_sb", "w_out", "ffn2_norm", "ffn2_w_gate",
             "ffn2_w_up", "ffn2_w_down"]
    return (loss, grad_x, *[grads[k] for k in order], *[deltas[k] for k in order],
            *[new_ms[k] for k in order], *[new_vs[k] for k in order])
```

```python
import jax
import jax.numpy as jnp
from jax import lax
from jax.experimental import pallas as pl
from jax.experimental.pallas import tpu as pltpu

F32 = jnp.float32
BF16 = jnp.bfloat16
MESH = pl.DeviceIdType.MESH

NDEV = 8
N_META = 16
HEAD_DIM = 128
HEADS = 8
BRANCH_WIDTH = HEADS * HEAD_DIM
RMS_EPS = 1e-6
FFN_RESIDUAL_WEIGHT = 0.5
ATTN_SCALE = HEAD_DIM ** -0.5
MASKED_LOGIT = -1e30

ADAM_LR = 0.001
ADAM_B1 = 0.9
ADAM_B2 = 0.999
ADAM_EPS = 1e-08
ADAM_WD = 0.01
ADAM_STEP = 10

LANES = 128
SMALL_ROWS = 8
ADAMW_TILE_ELEMS = 160 * 1024
KSUB = 4


def _pick(n, prefs):
    for p in prefs:
        if p <= n and n % p == 0:
            return p
    return n


def _dot(a, b, dims):
    return lax.dot_general(a, b, (dims, ((), ())), preferred_element_type=F32)


NN = ((1,), (0,))
TN = ((0,), (0,))
NT = ((1,), (1,))


def _split_bf16(x):
    hi = x.astype(BF16)
    lo = (x - hi.astype(F32)).astype(BF16)
    return hi, lo


def _sigmoid_parts(z):
    e = jnp.exp(-jnp.abs(z))
    t = 1.0 + e
    return e, 1.0 / t, jnp.log(t)


def _sigmoid(x):
    return 0.5 * jnp.tanh(0.5 * x) + 0.5


def _my_place():
    return lax.axis_index("x"), lax.axis_index("y"), lax.axis_index("c")


def _flat_id(px, py, pc):
    return 4 * px + 2 * py + pc


def _peer(x, y, c, k):
    px = 1 - x if k & 4 else x
    py = 1 - y if k & 2 else y
    pc = 1 - c if k & 1 else c
    return px, py, pc


def _allgather(shards, name, in_vmem=False):
    n = len(shards)

    def body(*refs):
        x_refs, out_refs = refs[:n], refs[n:2 * n]
        send_sems, recv_sems, local_sems = refs[2 * n:]
        x, y, c = _my_place()
        me, sibling = (x, y, c), (x, y, 1 - c)
        chips = [(1 - x, y), (x, 1 - y), (1 - x, 1 - y)]

        def block(a, place):
            return out_refs[a].at[_flat_id(*place)]

        def copy(a, k, place, to, src=None):
            return pltpu.make_async_remote_copy(
                src_ref=block(a, place) if src is None else src, dst_ref=block(a, place),
                send_sem=send_sems.at[7 * a + k], recv_sem=recv_sems.at[7 * a + k], device_id=to, device_id_type=MESH)

        mine = [pltpu.make_async_copy(x_refs[a], block(a, me), local_sems.at[a]) for a in range(n)]
        for cp in mine:
            cp.start()
        first = []
        for a in range(n):
            first.append(copy(a, 0, me, sibling, src=x_refs[a]))
            first += [copy(a, 1 + j, me, (*chip, c), src=x_refs[a]) for j, chip in enumerate(chips)]
        for cp in first:
            cp.start()
        passed = []
        for j, chip in enumerate(chips):
            for a in range(n):
                copy(a, 1 + j, (*chip, c), me).wait_recv()
                passed.append(copy(a, 4 + j, (*chip, c), sibling))
                passed[-1].start()
        for a in range(n):
            copy(a, 0, sibling, me).wait_recv()
        for j, chip in enumerate(chips):
            for a in range(n):
                copy(a, 4 + j, (*chip, 1 - c), me).wait_recv()
        for cp in first + passed:
            cp.wait_send()
        for cp in mine:
            cp.wait()

    space = pltpu.VMEM if in_vmem else pl.ANY
    return pl.pallas_call(
        body, name=name,
        out_shape=[jax.ShapeDtypeStruct((NDEV,) + s.shape, s.dtype) for s in shards],
        in_specs=[pl.BlockSpec(memory_space=space)] * n,
        out_specs=[pl.BlockSpec(memory_space=space)] * n,
        scratch_shapes=[pltpu.SemaphoreType.DMA((7 * n,)), pltpu.SemaphoreType.DMA((7 * n,)), pltpu.SemaphoreType.DMA((n,))],
    )(*shards)


def _allsum_small(part, name):
    rows, cols = part.shape

    def body(p_ref, out_ref, buf, send_sems, recv_sems):
        x, y, c = _my_place()
        me = _flat_id(x, y, c)
        buf[me] = p_ref[...]
        copies = []
        for k in range(1, NDEV):
            copies.append(pltpu.make_async_remote_copy(
                src_ref=p_ref, dst_ref=buf.at[me], send_sem=send_sems.at[k - 1], recv_sem=recv_sems.at[k - 1],
                device_id=_peer(x, y, c, k), device_id_type=MESH))
        for cp in copies:
            cp.start()
        for cp in copies:
            cp.wait()
        total = buf[0]
        for j in range(1, NDEV):
            total = total + buf[j]
        out_ref[...] = total

    return pl.pallas_call(
        body, name=name,
        out_shape=jax.ShapeDtypeStruct((rows, cols), F32),
        in_specs=[pl.BlockSpec(memory_space=pltpu.VMEM)],
        out_specs=pl.BlockSpec(memory_space=pltpu.VMEM),
        scratch_shapes=[pltpu.VMEM((NDEV, rows, cols), F32),
                        pltpu.SemaphoreType.DMA((7,)), pltpu.SemaphoreType.DMA((7,))],
    )(part)


_HBM = pl.BlockSpec(memory_space=pltpu.HBM)
_SEM = pl.BlockSpec(memory_space=pltpu.SEMAPHORE)
_DATAFLOW = pltpu.SideEffectType.DATAFLOW_SIDE_EFFECTING


COPIES_PER_ARRAY = {"gather": 7, "exchange": 7, "first": 4, "forward": 3}


def _send_copies(plan, src_refs, land_refs, send_sems, recv_sems):
    x, y, c = _my_place()
    me = _flat_id(x, y, c)
    per = COPIES_PER_ARRAY[plan]
    copies = []

    def add(a, slot, src, dst, to):
        copies.append(pltpu.make_async_remote_copy(
            src_ref=src, dst_ref=dst, send_sem=send_sems.at[per * a + slot], recv_sem=recv_sems.at[per * a + slot],
            device_id=to, device_id_type=MESH))

    for a, land_ref in enumerate(land_refs):
        if plan in ("gather", "exchange"):
            for k in range(1, NDEV):
                peer = _peer(x, y, c, k)
                add(a, k - 1, src_refs[a].at[_flat_id(*peer)] if plan == "exchange" else src_refs[a], land_ref.at[me], peer)
        elif plan == "first":
            for slot, k in enumerate((1, 2, 4, 6)):
                add(a, slot, src_refs[a], land_ref.at[me], _peer(x, y, c, k))
        else:
            for slot, k in enumerate((2, 4, 6)):
                block = land_ref.at[_flat_id(*_peer(x, y, c, k))]
                add(a, slot, block, block, (x, y, 1 - c))
    return copies


def _send_start(srcs, lands, plan, after, name):
    ns, nl = len(srcs), len(lands)
    nsem = COPIES_PER_ARRAY[plan] * nl

    def body(*refs):
        for cp in _send_copies(plan, refs[:ns], refs[ns:ns + nl], refs[ns + nl + 1], refs[ns + nl + 2]):
            cp.start()
        refs[-1][...] = jnp.zeros_like(refs[-1])

    operands = [pltpu.with_memory_space_constraint(t, pltpu.HBM) for t in list(srcs) + list(lands)]
    outs = pl.pallas_call(
        body, name=name,
        out_shape=(pltpu.SemaphoreType.DMA((nsem,)), pltpu.SemaphoreType.DMA((nsem,)),
                   *[pltpu.HBM(t.shape, t.dtype) for t in operands[ns:]], jax.ShapeDtypeStruct((SMALL_ROWS, LANES), F32)),
        in_specs=[_HBM] * (ns + nl) + [pl.BlockSpec(memory_space=pl.ANY)],
        out_specs=(_SEM, _SEM, *[_HBM] * nl, pl.BlockSpec(memory_space=pltpu.VMEM)),
        input_output_aliases={ns + i: 2 + i for i in range(nl)},
        compiler_params=pltpu.CompilerParams(has_side_effects=_DATAFLOW),
    )(*operands, after)
    return (plan, outs[0], outs[1], operands[:ns], list(outs[2:2 + nl])), outs[-1]


def _send_wait(handle, after, name):
    plan, send_sems, recv_sems, srcs, lands = handle
    ns, nl = len(srcs), len(lands)

    def body(*refs):
        for cp in _send_copies(plan, refs[:ns], refs[ns:ns + nl], refs[ns + nl], refs[ns + nl + 1]):
            cp.wait_send()
            cp.wait_recv()

    outs = pl.pallas_call(
        body, name=name,
        out_shape=tuple(pltpu.HBM(t.shape, t.dtype) for t in lands),
        in_specs=[_HBM] * (ns + nl) + [_SEM, _SEM, pl.BlockSpec(memory_space=pl.ANY)],
        out_specs=tuple([_HBM] * nl),
        input_output_aliases={ns + i: i for i in range(nl)},
        compiler_params=pltpu.CompilerParams(has_side_effects=_DATAFLOW),
    )(*srcs, *lands, send_sems, recv_sems, after)
    return list(outs)


def _gather_two_level(first_handle, after, name):
    lands = _send_wait(first_handle, after, f"{name}_wait")
    forward, _ = _send_start([], lands, "forward", after, f"{name}_forward_start")
    return _send_wait(forward, after, f"{name}_forward_wait")


def _landing(own_block, me):
    return lax.dynamic_update_index_in_dim(lax.empty((NDEV,) + own_block.shape, own_block.dtype), own_block, me, 0)


def _spec(block, index_map):
    return pl.BlockSpec(block, index_map)


def _mm(grid, a, a_spec, bs, extras, outs, *, dims, acc_shape=None, epi=None, after=None, name):
    nk = grid[2]
    nb, ne, no = len(bs), len(extras), len(outs)
    nafter = 0 if after is None else 1
    if epi is None:
        epi = lambda accs, ex: [accs[0]]

    def body(*refs):
        a_ref, b_refs = refs[0], refs[1:1 + nb]
        e_refs = refs[1 + nb:1 + nb + ne]
        o_refs = refs[1 + nb + ne + nafter:1 + nb + ne + nafter + no]
        acc_refs = refs[1 + nb + ne + nafter + no:]
        def finish(accs):
            for o_ref, tile in zip(o_refs, epi(accs, [e_ref[...] for e_ref in e_refs])):
                o_ref[...] = tile.astype(o_ref.dtype)

        def product(b_ref):
            if len(a_ref.shape) == 2:
                return _dot(a_ref[...], b_ref[...], dims)
            total = _dot(a_ref[0], b_ref[0], dims)
            for t in range(1, a_ref.shape[0]):
                total = total + _dot(a_ref[t], b_ref[t], dims)
            return total

        if nk == 1:
            finish([product(b_ref) for b_ref in b_refs])
        else:
            k = pl.program_id(2)

            @pl.when(k == 0)
            def _():
                for acc_ref in acc_refs:
                    acc_ref[...] = jnp.zeros_like(acc_ref)

            for acc_ref, b_ref in zip(acc_refs, b_refs):
                acc_ref[...] += product(b_ref)

            @pl.when(k == nk - 1)
            def _():
                finish([acc_ref[...] for acc_ref in acc_refs])

    return pl.pallas_call(
        body, name=name,
        grid=grid,
        in_specs=[a_spec] + [s for _, s in bs] + [s for _, s in extras] + [pl.BlockSpec(memory_space=pl.ANY)] * nafter,
        out_specs=[s for _, _, s in outs],
        out_shape=[jax.ShapeDtypeStruct(shape, dt) for shape, dt, _ in outs],
        scratch_shapes=[pltpu.VMEM(acc_shape, F32) for _ in bs] if nk > 1 else [],
        compiler_params=pltpu.CompilerParams(dimension_semantics=("parallel", "parallel", "arbitrary")),
    )(a, *[b for b, _ in bs], *[e for e, _ in extras], *([after] if nafter else []))


def _mm2d(a, b, n_cols, *, dims, tm, tn, tk, out_dtype, epi=None, extras=(), after=None, name):
    m_rows, k_len = (a.shape[1], a.shape[0]) if dims == TN else a.shape
    assert m_rows % tm == 0 and n_cols % tn == 0 and k_len % tk == 0, (name, a.shape, n_cols, tm, tn, tk)
    a_spec = _spec((tk, tm), lambda i, j, k: (k, i)) if dims == TN else _spec((tm, tk), lambda i, j, k: (i, k))
    b_spec = _spec((tn, tk), lambda i, j, k: (j, k)) if dims == NT else _spec((tk, tn), lambda i, j, k: (k, j))
    tile = _spec((tm, tn), lambda i, j, k: (i, j))
    return _mm((m_rows // tm, n_cols // tn, k_len // tk), a, a_spec, [(b, b_spec)], [(e, tile) for e in extras],
               [((m_rows, n_cols), out_dtype, tile)], dims=dims, acc_shape=(tm, tn), epi=epi, after=after, name=name)[0]


def _rowwise(fn, ins, consts, outs, sums, *, tm, name, row_maps=None, after=None):
    m_rows = outs[0][2] if len(outs[0]) == 3 else ins[0][0].shape[0]
    n = m_rows // tm
    ni, nc, no = len(ins), len(consts), len(outs)
    nafter = 0 if after is None else 1
    row_maps = row_maps or [None] * ni

    def body(*refs):
        i = pl.program_id(0)
        in_tiles = [r[...] for r in refs[:ni]]
        const_values = [r[...] for r in refs[ni:ni + nc]]
        o_refs = refs[ni + nc + nafter:ni + nc + nafter + no]
        s_refs = refs[ni + nc + nafter + no:]
        out_tiles, sum_terms = fn(i, in_tiles, const_values)
        for o_ref, tile in zip(o_refs, out_tiles):
            o_ref[...] = tile.astype(o_ref.dtype)
        if s_refs:
            @pl.when(i == 0)
            def _():
                for s_ref in s_refs:
                    s_ref[...] = jnp.zeros_like(s_ref)

            for s_ref, term in zip(s_refs, sum_terms):
                s_ref[...] += term

    def in_spec(width, col, rmap):
        if rmap is None:
            return pl.BlockSpec((tm, width), lambda i: (i, col))
        return pl.BlockSpec((tm, width), lambda i: (rmap(i), col))

    return pl.pallas_call(
        body, name=name,
        grid=(n,),
        in_specs=[in_spec(w, col, rmap) for (_, w, col), rmap in zip(ins, row_maps)]
        + [pl.BlockSpec(cst.shape, lambda i: (0, 0)) for cst in consts] + [pl.BlockSpec(memory_space=pl.ANY)] * nafter,
        out_specs=[pl.BlockSpec((tm, o[0]), lambda i: (i, 0)) for o in outs]
        + [pl.BlockSpec(s, lambda i: (0, 0)) for s in sums],
        out_shape=[jax.ShapeDtypeStruct((m_rows, o[0]), o[1]) for o in outs]
        + [jax.ShapeDtypeStruct(s, F32) for s in sums],
        compiler_params=pltpu.CompilerParams(dimension_semantics=("arbitrary",)),
    )(*[arr for arr, _, _ in ins], *consts, *([after] if nafter else []))


def _whole(arr):
    return (arr, arr.shape[1], 0)


def _rms(x, gain):
    r = lax.rsqrt(jnp.mean(x * x, axis=-1, keepdims=True) + RMS_EPS)
    return x * r * gain


def _rms_bwd(x, gain, dy):
    r = lax.rsqrt(jnp.mean(x * x, axis=-1, keepdims=True) + RMS_EPS)
    u = dy * gain
    dx = r * u - x * (r * r * r) * jnp.mean(x * u, axis=-1, keepdims=True)
    return dx, dy * x * r


def _norm_fwd(h, gain, *, tm, name):
    def fn(i, tiles, consts):
        return [_rms(tiles[0], consts[0])], []
    return _rowwise(fn, [_whole(h)], [gain], [(h.shape[1], BF16)], [], tm=tm, name=name)[0]


def _norm_bwd(dn, h, gain, dh_in, low_scale, *, tm, name, after=None):
    d = h.shape[1]

    def fn(i, tiles, consts):
        dx, dg_rows = _rms_bwd(tiles[1], consts[0], tiles[0])
        dh = tiles[2] + dx
        return [dh] + ([] if low_scale is None else [low_scale * dh]), [jnp.sum(dg_rows, axis=0, keepdims=True)]

    outs = _rowwise(fn, [_whole(dn), _whole(h), _whole(dh_in)], [gain],
                    [(d, F32)] + ([] if low_scale is None else [(d, BF16)]), [(1, d)], tm=tm, name=name, after=after)
    return (outs[0], None, outs[1]) if low_scale is None else tuple(outs)


def _swiglu_epi(accs, ex):
    a, b = accs
    return [a, b, a * _sigmoid(a) * b]


def _swiglu_bwd_epi(accs, ex):
    ds = accs[0]
    a, b = ex[0].astype(F32), ex[1].astype(F32)
    sig = _sigmoid(a)
    silu = a * sig
    dsilu = sig * (1.0 + a * (1.0 - sig))
    return [jnp.stack([(ds * b * dsilu).astype(BF16), (ds * silu).astype(BF16)], axis=0)]


def _attn_mask(i, j, tile, pad, strict):
    row = i * tile + lax.broadcasted_iota(jnp.int32, (tile, tile), 0)
    col = j * tile + lax.broadcasted_iota(jnp.int32, (tile, tile), 1)
    causal = (col < row) if strict else (col <= row)
    return causal & ((col >= pad) | (row < pad))


FWD_GROUP = 4
BWD_GROUP = 2


def _walk_key_tiles(i, step, carry, widths, descending=False):
    diagonal = lambda j, c: step(j, c, True, 1)
    left = jnp.maximum(i - 1, 0)
    lo = 1
    if descending:
        carry = lax.fori_loop(jnp.maximum(i, 1), i + 1, diagonal, carry)
    else:
        carry = step(0, carry, True, 1)
    for nt in widths:
        count = left // nt
        if descending:
            top = lo + left
            carry = lax.fori_loop(0, count, lambda t, c, nt=nt, top=top: step(top - nt * (t + 1), c, False, nt), carry)
        else:
            carry = lax.fori_loop(0, count, lambda t, c, nt=nt, lo=lo: step(lo + nt * t, c, False, nt), carry)
            lo = lo + nt * count
        left = left - nt * count
    if descending:
        return step(0, carry, True, 1)
    return lax.fori_loop(jnp.maximum(i, 1), i + 1, diagonal, carry)


def _head_cols(g):
    return pl.ds(g * HEAD_DIM, HEAD_DIM)


def _key_rows(j, tile, nt):
    return pl.ds(pl.multiple_of(j * tile, tile), nt * tile)


def _key_tile(ref, j, tile, g, nt=1):
    return ref[_key_rows(j, tile, nt), _head_cols(g)]


def _key_bias(c_ref, g, j, nt):
    return c_ref[g, j] if nt == 1 else jnp.concatenate([c_ref[g, j + t] for t in range(nt)], axis=1)


def _head_specs(lp, tile, q_off, k_off, v_off, hp):
    gw = hp * HEAD_DIM
    q_spec = pl.BlockSpec((tile, gw), lambda h, i: (i, h + q_off // hp))
    k_spec = pl.BlockSpec((lp, gw), lambda h, i: (0, h + k_off // hp))
    v_spec = pl.BlockSpec((lp, gw), lambda h, i: (0, h + v_off // hp))
    return q_spec, k_spec, v_spec


def _fox_fwd(q, k, v, v_off, c_rows, *, tile, pad, name):
    lp = q.shape[0]
    nb = lp // tile
    hp = FWD_GROUP
    gw = hp * HEAD_DIM

    def body(q_ref, k_ref, v_ref, c_ref, o_ref, lse_ref):
        i = pl.program_id(1)
        qts = [q_ref[:, _head_cols(g)] for g in range(hp)]

        def step(j, carry, masked, nt):
            ok = _attn_mask(i, j, tile, pad, False) if masked else None
            out = []
            for g, (m, l, acc) in enumerate(carry):
                s = _dot(qts[g], _key_tile(k_ref, j, tile, g, nt), NT) * ATTN_SCALE - _key_bias(c_ref, g, j, nt)
                if masked:
                    s = jnp.where(ok, s, MASKED_LOGIT)
                m_new = jnp.maximum(m, jnp.max(s, axis=1, keepdims=True))
                p = jnp.exp(s - m_new)
                alpha = jnp.exp(m - m_new)
                l = alpha * l + jnp.sum(p, axis=1, keepdims=True)
                acc = alpha * acc + _dot(p.astype(BF16), _key_tile(v_ref, j, tile, g, nt), NN)
                out.append((m_new, l, acc))
            return tuple(out)

        init = (jnp.full((tile, 1), MASKED_LOGIT, F32), jnp.zeros((tile, 1), F32), jnp.zeros((tile, HEAD_DIM), F32))
        final = _walk_key_tiles(i, step, (init,) * hp, (4, 2, 1))
        for g, (m, l, acc) in enumerate(final):
            o_ref[:, _head_cols(g)] = (acc / l).astype(o_ref.dtype)
            lse_ref[g] = jnp.broadcast_to(m + jnp.log(l), (tile, LANES))

    q_spec, k_spec, v_spec = _head_specs(lp, tile, 0, 0, v_off, hp)
    return pl.pallas_call(
        body, name=name,
        grid=(HEADS // hp, nb),
        in_specs=[q_spec, k_spec, v_spec, pl.BlockSpec((hp, nb, 1, tile), lambda h, i: (h, 0, 0, 0))],
        out_specs=[pl.BlockSpec((tile, gw), lambda h, i: (i, h)),
                   pl.BlockSpec((hp, tile, LANES), lambda h, i: (h, i, 0))],
        out_shape=[jax.ShapeDtypeStruct((lp, BRANCH_WIDTH), BF16), jax.ShapeDtypeStruct((HEADS, lp, LANES), F32)],
        compiler_params=pltpu.CompilerParams(dimension_semantics=("parallel", "arbitrary")),
    )(q, k, v, c_rows)


def _fox_bwd(q, k, v, v_off, c_rows, o, do, lse, *, tile, pad, name):
    lp = q.shape[0]
    nb = lp // tile
    hp = BWD_GROUP
    gw = hp * HEAD_DIM

    def body(q_ref, k_ref, v_ref, c_ref, o_ref, do_ref, lse_ref, dq_ref, dk_ref, dv_ref, dc_ref, dk_acc, dv_acc, dc_acc):
        i = pl.program_id(1)

        @pl.when(i == 0)
        def _():
            dk_acc[...] = jnp.zeros_like(dk_acc)
            dv_acc[...] = jnp.zeros_like(dv_acc)
            dc_acc[...] = jnp.zeros_like(dc_acc)

        heads = range(hp)
        qts = [q_ref[:, _head_cols(g)] for g in heads]
        dots = [do_ref[:, _head_cols(g)] for g in heads]
        deltas = [jnp.sum(dots[g].astype(F32) * o_ref[:, _head_cols(g)].astype(F32), axis=1, keepdims=True) for g in heads]
        lse_cols = [lse_ref[g][:, :1] for g in heads]

        def step(j, dqs, masked, nt):
            rows = _key_rows(j, tile, nt)
            ok = _attn_mask(i, j, tile, pad, False) if masked else None
            out = []
            for g in heads:
                kt = _key_tile(k_ref, j, tile, g, nt)
                s = _dot(qts[g], kt, NT) * ATTN_SCALE - _key_bias(c_ref, g, j, nt)
                p = jnp.exp(s - lse_cols[g])
                if masked:
                    p = jnp.where(ok, p, 0.0)
                dp = _dot(dots[g], _key_tile(v_ref, j, tile, g, nt), NT)
                ds = p * (dp - deltas[g])
                dsb = ds.astype(BF16)
                dk_acc[rows, _head_cols(g)] += _dot(dsb, qts[g], TN)
                dv_acc[rows, _head_cols(g)] += _dot(p.astype(BF16), dots[g], TN)
                dc = -jnp.sum(ds, axis=0, keepdims=True)
                for t in range(nt):
                    dc_acc[g, j + t] += dc[:, t * tile:(t + 1) * tile]
                out.append(dqs[g] + _dot(dsb, kt, NN))
            return tuple(out)

        dqs = _walk_key_tiles(i, step, (jnp.zeros((tile, HEAD_DIM), F32),) * hp, (4, 2, 1))
        for g in heads:
            dq_ref[:, _head_cols(g)] = dqs[g] * ATTN_SCALE

        @pl.when(i == nb - 1)
        def _():
            dk_ref[...] = dk_acc[...] * ATTN_SCALE
            dv_ref[...] = dv_acc[...].astype(dv_ref.dtype)
            dc_ref[...] = dc_acc[...]

    q_spec, k_spec, v_spec = _head_specs(lp, tile, 0, 0, v_off, hp)
    tile_spec = pl.BlockSpec((tile, gw), lambda h, i: (i, h))
    head_spec = pl.BlockSpec((lp, gw), lambda h, i: (0, h))
    c_spec = pl.BlockSpec((hp, nb, 1, tile), lambda h, i: (h, 0, 0, 0))
    return pl.pallas_call(
        body, name=name,
        grid=(HEADS // hp, nb),
        in_specs=[q_spec, k_spec, v_spec, c_spec, tile_spec, tile_spec,
                  pl.BlockSpec((hp, tile, LANES), lambda h, i: (h, i, 0))],
        out_specs=[tile_spec, head_spec, head_spec, c_spec],
        out_shape=[jax.ShapeDtypeStruct((lp, BRANCH_WIDTH), F32), jax.ShapeDtypeStruct((lp, BRANCH_WIDTH), F32),
                   jax.ShapeDtypeStruct((lp, BRANCH_WIDTH), BF16), jax.ShapeDtypeStruct((HEADS, nb, 1, tile), F32)],
        scratch_shapes=[pltpu.VMEM((lp, gw), F32), pltpu.VMEM((lp, gw), F32),
                        pltpu.VMEM((hp, nb, 1, tile), F32)],
        compiler_params=pltpu.CompilerParams(dimension_semantics=("parallel", "arbitrary")),
    )(q, k, v, c_rows, o, do, lse)


def _later_matrix(tile):
    return (lax.broadcasted_iota(jnp.int32, (tile, tile), 0) > lax.broadcasted_iota(jnp.int32, (tile, tile), 1)).astype(BF16)


def _earlier_matrix(tile):
    return (lax.broadcasted_iota(jnp.int32, (tile, tile), 0) < lax.broadcasted_iota(jnp.int32, (tile, tile), 1)).astype(BF16)


def _running_sums(x, tri, suffix):
    tile = tri.shape[0]
    blocks = [x[:, b:b + tile] for b in range(0, x.shape[1], tile)]
    sums = [jnp.sum(blk, axis=1, keepdims=True) for blk in blocks]
    out = []
    for b, blk in enumerate(blocks):
        hi, lo = _split_bf16(blk)
        inside = _dot(hi, tri, NN) + _dot(lo, tri, NN)
        for other in (sums[b + 1:] if suffix else sums[:b]):
            inside = inside + other
        out.append(inside)
    total = sums[0]
    for other in sums[1:]:
        total = total + other
    return (out[0] if len(out) == 1 else jnp.concatenate(out, axis=1)), total


def _sb_tile(qt, kt, ok, later):
    z = _dot(qt, kt, NT) * ATTN_SCALE
    e, r, lg = _sigmoid_parts(z)
    sp = jnp.maximum(z, 0.0) + lg
    spm = sp if ok is None else jnp.where(ok, sp, 0.0)
    within, sp_here = _running_sums(spm, later, True)
    return z, e, r, sp, sp_here, within


def _sb_fwd(qkv, q_off, k_off, v_off, *, tile, pad, name):
    lp = qkv.shape[0]
    nb = lp // tile
    hp = FWD_GROUP
    gw = hp * HEAD_DIM

    def body(q_ref, k_ref, v_ref, o_ref, tot_ref):
        i = pl.program_id(1)
        qts = [q_ref[:, _head_cols(g)] for g in range(hp)]
        later = _later_matrix(tile)

        def step(j, carry, masked, nt):
            ok = _attn_mask(i, j, tile, pad, True) if masked else None
            out = []
            for g, (right, acc) in enumerate(carry):
                z, _, _, sp, sp_here, within = _sb_tile(qts[g], _key_tile(k_ref, j, tile, g, nt), ok, later)
                w = jnp.exp(z - sp - within - right)
                if masked:
                    w = jnp.where(ok, w, 0.0)
                acc = acc + _dot(w.astype(BF16), _key_tile(v_ref, j, tile, g, nt), NN)
                out.append((right + sp_here, acc))
            return tuple(out)

        init = (jnp.zeros((tile, 1), F32), jnp.zeros((tile, HEAD_DIM), F32))
        final = _walk_key_tiles(i, step, (init,) * hp, (4, 2, 1), descending=True)
        for g, (total, acc) in enumerate(final):
            o_ref[:, _head_cols(g)] = acc.astype(o_ref.dtype)
            tot_ref[g] = jnp.broadcast_to(total, (tile, LANES))

    q_spec, k_spec, v_spec = _head_specs(lp, tile, q_off, k_off, v_off, hp)
    return pl.pallas_call(
        body, name=name,
        grid=(HEADS // hp, nb),
        in_specs=[q_spec, k_spec, v_spec],
        out_specs=[pl.BlockSpec((tile, gw), lambda h, i: (i, h)),
                   pl.BlockSpec((hp, tile, LANES), lambda h, i: (h, i, 0))],
        out_shape=[jax.ShapeDtypeStruct((lp, BRANCH_WIDTH), BF16), jax.ShapeDtypeStruct((HEADS, lp, LANES), F32)],
        compiler_params=pltpu.CompilerParams(dimension_semantics=("parallel", "arbitrary")),
    )(qkv, qkv, qkv)


def _sb_bwd(qkv, q_off, k_off, v_off, do, total, *, tile, pad, name):
    lp = qkv.shape[0]
    nb = lp // tile
    hp = BWD_GROUP
    gw = hp * HEAD_DIM

    def body(q_ref, k_ref, v_ref, do_ref, tot_ref, dq_ref, dk_ref, dv_ref, dk_acc, dv_acc):
        i = pl.program_id(1)

        @pl.when(i == 0)
        def _():
            dk_acc[...] = jnp.zeros_like(dk_acc)
            dv_acc[...] = jnp.zeros_like(dv_acc)

        heads = range(hp)
        qts = [q_ref[:, _head_cols(g)] for g in heads]
        dots = [do_ref[:, _head_cols(g)] for g in heads]
        total_cols = [tot_ref[g][:, :1] for g in heads]
        later, earlier = _later_matrix(tile), _earlier_matrix(tile)

        def step(j, carry, masked, nt):
            rows = _key_rows(j, tile, nt)
            ok = _attn_mask(i, j, tile, pad, True) if masked else None
            out = []
            for g, (dq, sp_before, dlw_before) in enumerate(carry):
                kt = _key_tile(k_ref, j, tile, g, nt)
                z, e, r, sp, sp_here, within = _sb_tile(qts[g], kt, ok, later)
                right = total_cols[g] - sp_before - sp_here
                w = jnp.exp(z - sp - within - right)
                if masked:
                    w = jnp.where(ok, w, 0.0)
                dlw = w * _dot(dots[g], _key_tile(v_ref, j, tile, g, nt), NT)
                before, dlw_here = _running_sums(dlw, earlier, False)
                sig = jnp.where(z >= 0, r, e * r)
                dz = dlw * (1.0 - sig) - sig * (before + dlw_before)
                if masked:
                    dz = jnp.where(ok, dz, 0.0)
                dzb = dz.astype(BF16)
                dk_acc[rows, _head_cols(g)] += _dot(dzb, qts[g], TN)
                dv_acc[rows, _head_cols(g)] += _dot(w.astype(BF16), dots[g], TN)
                out.append((dq + _dot(dzb, kt, NN), sp_before + sp_here, dlw_before + dlw_here))
            return tuple(out)

        zero_col = jnp.zeros((tile, 1), F32)
        final = _walk_key_tiles(i, step, ((jnp.zeros((tile, HEAD_DIM), F32), zero_col, zero_col),) * hp, (4, 2, 1))
        for g in heads:
            dq_ref[:, _head_cols(g)] = (final[g][0] * ATTN_SCALE).astype(dq_ref.dtype)

        @pl.when(i == nb - 1)
        def _():
            dk_ref[...] = (dk_acc[...] * ATTN_SCALE).astype(dk_ref.dtype)
            dv_ref[...] = dv_acc[...].astype(dv_ref.dtype)

    q_spec, k_spec, v_spec = _head_specs(lp, tile, q_off, k_off, v_off, hp)
    tile_spec = pl.BlockSpec((tile, gw), lambda h, i: (i, h))
    head_spec = pl.BlockSpec((lp, gw), lambda h, i: (0, h))
    return pl.pallas_call(
        body, name=name,
        grid=(HEADS // hp, nb),
        in_specs=[q_spec, k_spec, v_spec, tile_spec, pl.BlockSpec((hp, tile, LANES), lambda h, i: (h, i, 0))],
        out_specs=[tile_spec, head_spec, head_spec],
        out_shape=[jax.ShapeDtypeStruct((lp, BRANCH_WIDTH), BF16)] * 3,
        scratch_shapes=[pltpu.VMEM((lp, gw), F32), pltpu.VMEM((lp, gw), F32)],
        compiler_params=pltpu.CompilerParams(dimension_semantics=("parallel", "arbitrary")),
    )(qkv, qkv, qkv, do, total)


def _cumsum_rows(x, *, tile, reverse, name):
    lp = x.shape[0]
    nb = lp // tile

    def body(x_ref, o_ref, carry):
        @pl.when(pl.program_id(0) == 0)
        def _():
            carry[...] = jnp.zeros_like(carry)

        r = lax.broadcasted_iota(jnp.int32, (tile, tile), 0)
        c = lax.broadcasted_iota(jnp.int32, (tile, tile), 1)
        tri = ((c >= r) if reverse else (c <= r)).astype(BF16)
        hi, lo = _split_bf16(x_ref[...])
        run = _dot(tri, hi, NN) + _dot(tri, lo, NN) + carry[...]
        o_ref[...] = run
        carry[...] = run[:1, :] if reverse else run[tile - 1:, :]

    order = (lambda i: (nb - 1 - i, 0)) if reverse else (lambda i: (i, 0))
    return pl.pallas_call(
        body, name=name,
        grid=(nb,),
        in_specs=[pl.BlockSpec((tile, LANES), order)],
        out_specs=pl.BlockSpec((tile, LANES), order),
        out_shape=jax.ShapeDtypeStruct((lp, LANES), F32),
        scratch_shapes=[pltpu.VMEM((1, LANES), F32)],
        compiler_params=pltpu.CompilerParams(dimension_semantics=("arbitrary",)),
    )(x)


def _log_sigmoid(x):
    return jnp.minimum(x, 0.0) - jnp.log(1.0 + jnp.exp(-jnp.abs(x)))


def _forget_mask(i, tm, pad):
    row = i * tm + lax.broadcasted_iota(jnp.int32, (tm, LANES), 0)
    lane = lax.broadcasted_iota(jnp.int32, (tm, LANES), 1)
    return (row >= pad) & (lane < HEADS)


def _fox_prep(proj_a, q_gain, k_gain, b_forget, *, tm, pad, name):
    w = BRANCH_WIDTH

    def fn(i, tiles, consts):
        pa = tiles[0]
        qs, ks = [], []
        for h in range(HEADS):
            lo = h * HEAD_DIM
            qs.append(_rms(pa[:, lo:lo + HEAD_DIM], consts[0][:, lo:lo + HEAD_DIM]))
            ks.append(_rms(pa[:, w + lo:w + lo + HEAD_DIM], consts[1][:, lo:lo + HEAD_DIM]))
        logf = jnp.where(_forget_mask(i, tm, pad), _log_sigmoid(pa[:, 2 * w:] + consts[2]), 0.0)
        return [jnp.concatenate(qs, axis=1), jnp.concatenate(ks, axis=1), logf], []

    return _rowwise(fn, [_whole(proj_a)], [q_gain, k_gain, b_forget],
                    [(w, BF16), (w, BF16), (LANES, F32)], [], tm=tm, name=name)


def _fox_prep_bwd(proj_a, dq, dk, dlogf, q_gain, k_gain, b_forget, *, tm, pad, name):
    w = BRANCH_WIDTH

    def fn(i, tiles, consts):
        pa, dqt, dkt, dlf = tiles
        dxs_q, dxs_k, dgs_q, dgs_k = [], [], [], []
        for h in range(HEADS):
            lo = h * HEAD_DIM
            dx, dg = _rms_bwd(pa[:, lo:lo + HEAD_DIM], consts[0][:, lo:lo + HEAD_DIM], dqt[:, lo:lo + HEAD_DIM])
            dxs_q.append(dx)
            dgs_q.append(jnp.sum(dg, axis=0, keepdims=True))
            dx, dg = _rms_bwd(pa[:, w + lo:w + lo + HEAD_DIM], consts[1][:, lo:lo + HEAD_DIM], dkt[:, lo:lo + HEAD_DIM])
            dxs_k.append(dx)
            dgs_k.append(jnp.sum(dg, axis=0, keepdims=True))
        xf = pa[:, 2 * w:] + consts[2]
        e, r, _ = _sigmoid_parts(xf)
        df = jnp.where(_forget_mask(i, tm, pad), dlf * jnp.where(xf >= 0, e * r, r), 0.0)
        return ([jnp.concatenate(dxs_q + dxs_k + [df], axis=1)],
                [jnp.concatenate(dgs_q, axis=1), jnp.concatenate(dgs_k, axis=1), jnp.sum(df, axis=0, keepdims=True)])

    return _rowwise(fn, [_whole(proj_a), _whole(dq), _whole(dk), _whole(dlogf)], [q_gain, k_gain, b_forget],
                    [(2 * w + LANES, BF16)], [(1, w), (1, w), (1, LANES)], tm=tm, name=name)


def _adamw_math(w, g, m, v):
    m = ADAM_B1 * m + (1.0 - ADAM_B1) * g
    v = ADAM_B2 * v + (1.0 - ADAM_B2) * (g * g)
    m_hat = m / (1.0 - ADAM_B1 ** ADAM_STEP)
    v_hat = v / (1.0 - ADAM_B2 ** ADAM_STEP)
    delta = -ADAM_LR * (m_hat / (jnp.sqrt(v_hat) + ADAM_EPS) + ADAM_WD * w)
    return delta, m, v


def _adamw_summed(parts, sel, w, m, v, *, name):
    rows, cols = w.shape
    tr = _pick(rows, [t for t in (512, 256, 128, 64, 32, 16, 8) if t * cols <= ADAMW_TILE_ELEMS])

    def body(p_ref, w_ref, m_ref, v_ref, g_out, d_out, m_out, v_out):
        g = p_ref[0].astype(F32)
        for j in range(1, NDEV):
            g = g + p_ref[j].astype(F32)
        delta, m_new, v_new = _adamw_math(w_ref[...], g, m_ref[...], v_ref[...])
        g_out[...] = g
        d_out[...] = delta
        m_out[...] = m_new
        v_out[...] = v_new

    spec = pl.BlockSpec((tr, cols), lambda i: (i, 0))
    if parts.ndim == 3:
        p_spec = pl.BlockSpec((NDEV, tr, cols), lambda i: (0, i, 0))
    else:
        p_spec = pl.BlockSpec((NDEV, None, tr, cols), lambda i: (0, sel, i, 0))
    return pl.pallas_call(
        body, name=name,
        grid=(rows // tr,),
        in_specs=[p_spec, spec, spec, spec],
        out_specs=[spec] * 4,
        out_shape=[jax.ShapeDtypeStruct((rows, cols), F32)] * 4,
        compiler_params=pltpu.CompilerParams(dimension_semantics=("parallel",)),
    )(parts, w, m, v)


def _adamw_plain(g, w, m, v, *, name):
    def body(g_ref, w_ref, m_ref, v_ref, d_out, m_out, v_out):
        delta, m_new, v_new = _adamw_math(w_ref[...], g_ref[...], m_ref[...], v_ref[...])
        d_out[...] = delta
        m_out[...] = m_new
        v_out[...] = v_new

    return pl.pallas_call(body, name=name, out_shape=[jax.ShapeDtypeStruct(w.shape, F32)] * 3)(g, w, m, v)


def _pack_rows(arrays, width, row_align):
    pieces, spans, at = [], [], 0
    for arr in arrays:
        flat = arr.reshape(-1)
        rows = -(-flat.shape[0] // (width * row_align)) * row_align
        flat = jnp.pad(flat, (0, rows * width - flat.shape[0]))
        pieces.append(flat.reshape(rows, width))
        spans.append((at, rows))
        at += rows
    return jnp.concatenate(pieces, axis=0), spans


def _unpack(rows2d, span, shape):
    at, rows = span
    size = 1
    for s in shape:
        size *= s
    return rows2d[at:at + rows].reshape(-1)[:size].reshape(shape)


def _join_cols(blocks):
    n, rows, cols = blocks.shape
    return jnp.transpose(blocks, (1, 0, 2)).reshape(rows, n * cols)


def _split_cols(full):
    rows, cols = full.shape
    return jnp.transpose(full.reshape(rows, NDEV, cols // NDEV), (1, 0, 2))


def _ffn_up(h, gain, w_gu, after, *, tm, tag):
    lp, d = h.shape
    f8 = w_gu.shape[3]
    n = _norm_fwd(h, gain, tm=_pick(lp, [256, 128]), name=f"{tag}_norm")
    hid = _spec((None, tm, f8), lambda i, j, k: (j, i, 0))
    a, b, s = _mm((lp // tm, NDEV, 1), n, _spec((tm, d), lambda i, j, k: (i, 0)),
                  [(w_gu, _spec((None, None, d, f8), lambda i, j, k: (j, 0, 0, 0))),
                   (w_gu, _spec((None, None, d, f8), lambda i, j, k: (j, 1, 0, 0)))],
                  [], [((NDEV, lp, f8), BF16, hid)] * 3, dims=NN, epi=_swiglu_epi, after=after, name=f"{tag}_up")
    return n, a, b, s


def _ffn_down(h, s, w_down, *, tm, tag):
    lp, d = h.shape
    f8 = w_down.shape[1]
    tn = _pick(d, [1024, 512, 256, 128])
    tile = _spec((tm, tn), lambda i, j, k: (i, j))
    return _mm((lp // tm, d // tn, NDEV // KSUB), s, _spec((KSUB, tm, f8), lambda i, j, k: (k, i, 0)),
               [(w_down, _spec((KSUB, f8, tn), lambda i, j, k: (k, 0, j)))], [(h, tile)], [((lp, d), F32, tile)],
               dims=NN, acc_shape=(tm, tn), epi=lambda accs, ex: [ex[0] + FFN_RESIDUAL_WEIGHT * accs[0]], name=f"{tag}_down")[0]


def _ffn_bwd_dw(dh_half, saved, w_down, on_down, *, tm, tag):
    n, a, b, s = saved
    lp, d = dh_half.shape
    f8 = w_down.shape[1]
    tkr = _pick(lp, [4352, 2176, 1088, 544, 256, 128])
    tn = _pick(d, [1024, 512, 256, 128])
    hid = _spec((None, tm, f8), lambda i, j, k: (j, i, 0))
    dab, = _mm((lp // tm, NDEV, 1), dh_half, _spec((tm, d), lambda i, j, k: (i, 0)),
               [(w_down, _spec((None, f8, d), lambda i, j, k: (j, 0, 0)))], [(a, hid), (b, hid)],
               [((2, NDEV, lp, f8), BF16, _spec((2, None, tm, f8), lambda i, j, k: (0, j, i, 0)))],
               dims=NT, epi=_swiglu_bwd_epi, name=f"{tag}_down_dx")
    dw_down, = _mm((NDEV, d // tn, lp // tkr), s, _spec((None, tkr, f8), lambda i, j, k: (i, k, 0)),
                   [(dh_half, _spec((tkr, tn), lambda i, j, k: (k, j)))], [],
                   [((NDEV, f8, d), BF16, _spec((None, f8, tn), lambda i, j, k: (i, 0, j)))],
                   dims=TN, acc_shape=(f8, tn), name=f"{tag}_down_dw")
    dw_gu, = _mm((d // tn, 2 * NDEV, lp // tkr), n, _spec((tkr, tn), lambda i, j, k: (k, i)),
                 [(dab, _spec((None, None, tkr, f8), lambda i, j, k: (j // NDEV, j % NDEV, k, 0)))], [],
                 [((NDEV, 2, d, f8), BF16, _spec((None, None, tn, f8), lambda i, j, k: (j % NDEV, j // NDEV, i, 0)))],
                 dims=TN, acc_shape=(tn, f8), after=on_down(dw_down), name=f"{tag}_gate_up_dw")
    return dab, dw_gu, dw_down


def _ffn_bwd_dx(dab, w_gu, after, *, tm, tag):
    lp, f8 = dab.shape[2], dab.shape[3]
    d = w_gu.shape[2]
    tn = _pick(d, [1024, 512, 256, 128])
    nsub = NDEV // KSUB
    return _mm((lp // tm, d // tn, 2 * nsub), dab, _spec((None, KSUB, tm, f8), lambda i, j, k: (k // nsub, k % nsub, i, 0)),
               [(w_gu, _spec((KSUB, None, tn, f8), lambda i, j, k: (k % nsub, k // nsub, j, 0)))], [],
               [((lp, d), F32, _spec((tm, tn), lambda i, j, k: (i, j)))],
               dims=NT, acc_shape=(tm, tn), after=after, name=f"{tag}_gate_up_dx")[0]


def kernel(x, meta_tokens, ffn1_norm, ffn1_w_gate, ffn1_w_up, ffn1_w_down, mix_norm, w_in, b_forget, fox_q_norm, fox_k_norm, w_branch_fox, w_branch_sb, w_out, ffn2_norm, ffn2_w_gate, ffn2_w_up, ffn2_w_down, loss_target, m_meta_tokens, m_ffn1_norm, m_ffn1_w_gate, m_ffn1_w_up, m_ffn1_w_down, m_mix_norm, m_w_in, m_b_forget, m_fox_q_norm, m_fox_k_norm, m_w_branch_fox, m_w_branch_sb, m_w_out, m_ffn2_norm, m_ffn2_w_gate, m_ffn2_w_up, m_ffn2_w_down, v_meta_tokens, v_ffn1_norm, v_ffn1_w_gate, v_ffn1_w_up, v_ffn1_w_down, v_mix_norm, v_w_in, v_b_forget, v_fox_q_norm, v_fox_k_norm, v_w_branch_fox, v_w_branch_sb, v_w_out, v_ffn2_norm, v_ffn2_w_gate, v_ffn2_w_up, v_ffn2_w_down):
    seq, d = x.shape[1], x.shape[2]
    d8 = d // NDEV
    w = BRANCH_WIDTH
    tile = 256 if seq % 256 == 0 else 128
    pad = tile - N_META
    lp = tile + seq
    tm = _pick(lp, [1088, 544, 256, 128])
    tr = _pick(tile, [256, 128])
    tkr = _pick(lp, [4352, 2176, 1088, 544, 256, 128])
    nb = lp // tile
    me = _flat_id(*_my_place())

    shards = [jnp.stack([ffn1_w_gate[0], ffn1_w_up[0]]).astype(BF16), ffn1_w_down[0].astype(BF16), w_in[0].astype(BF16),
              w_branch_fox[0].astype(BF16), w_branch_sb[0].astype(BF16), w_out[0].astype(BF16),
              jnp.stack([ffn2_w_gate[0], ffn2_w_up[0]]).astype(BF16), ffn2_w_down[0].astype(BF16)]
    w_gu1, = _allgather(shards[:1], "gather_ffn1")
    down1_copies, token = _send_start(shards[1:2], [_landing(s, me) for s in shards[1:2]], "first", w_gu1, "gather_down1_start")
    w_in_copies, token = _send_start(shards[2:3], [_landing(s, me) for s in shards[2:3]], "first", token, "gather_w_in_start")
    mixer_copies, token = _send_start(shards[3:6], [_landing(s, me) for s in shards[3:6]], "gather", token, "gather_mixer_start")
    ffn2_copies, token = _send_start(shards[6:], [_landing(s, me) for s in shards[6:]], "gather", token, "gather_ffn2_start")
    meta_full = _join_cols(_allgather([meta_tokens], "gather_meta", in_vmem=True)[0])

    h0 = jnp.concatenate([jnp.zeros((pad, d), F32), meta_full.astype(F32), x[0]], axis=0)
    saved1 = _ffn_up(h0, ffn1_norm, w_gu1, token, tm=tm, tag="ffn1")
    w_down1, = _gather_two_level(down1_copies, saved1[3], "gather_down1")
    h1 = _ffn_down(h0, saved1[3], w_down1, tm=tm, tag="ffn1")

    w_in_blocks, = _gather_two_level(w_in_copies, h1, "gather_w_in")
    wi = _join_cols(w_in_blocks)
    w_pa = jnp.concatenate([wi[:, :2 * w], jnp.pad(wi[:, 3 * w:3 * w + HEADS], ((0, 0), (0, LANES - HEADS)))], axis=1)
    w_pb = jnp.concatenate([wi[:, 2 * w:3 * w], wi[:, 3 * w + HEADS:]], axis=1)
    na, nbw = w_pa.shape[1], w_pb.shape[1]
    gate_blk = 4 * w // d

    n2 = _norm_fwd(h1, mix_norm, tm=tr, name="mix_norm")
    tma = _pick(lp, [544, 256, 128])
    tnd = _pick(d, [1024, 512, 256, 128])
    tnb = _pick(nbw, [512, 256, 128])
    proj_a = _mm2d(n2, w_pa, na, dims=NN, tm=tma, tn=na, tk=d, out_dtype=F32, name="proj_a")
    proj_b = _mm2d(n2, w_pb, nbw, dims=NN, tm=tm, tn=tnb, tk=d, out_dtype=BF16, name="proj_b")
    b_pad = jnp.pad(b_forget, ((0, 0), (0, LANES - HEADS)))
    q_gain, k_gain = fox_q_norm.reshape(1, w), fox_k_norm.reshape(1, w)
    fq, fk, logf = _fox_prep(proj_a, q_gain, k_gain, b_pad, tm=tr, pad=pad, name="fox_prep")
    c = _cumsum_rows(logf, tile=tile, reverse=False, name="forget_cumsum")
    c_rows = jnp.transpose(c[:, :HEADS]).reshape(HEADS, nb, 1, tile)
    o_fox, lse = _fox_fwd(fq, fk, proj_b, 0, c_rows, tile=tile, pad=pad, name="fox_fwd")
    o_sb, sb_total = _sb_fwd(proj_b, HEADS, 2 * HEADS, 3 * HEADS, tile=tile, pad=pad, name="sb_fwd")
    w_br_fox, w_br_sb, w_out_blocks = _send_wait(mixer_copies, o_sb, "gather_mixer_wait")
    w_out_full = w_out_blocks.reshape(d, d)

    def branch(o, w_blocks, name):
        return _mm((lp // tm, NDEV, 1), o, _spec((tm, w), lambda i, j, k: (i, 0)),
                   [(w_blocks, _spec((None, w, d8), lambda i, j, k: (j, 0, 0)))], [],
                   [((lp, d), BF16, _spec((tm, d8), lambda i, j, k: (i, j)))], dims=NN, name=name)[0]

    br_fox = branch(o_fox, w_br_fox, "branch_fox")
    br_sb = branch(o_sb, w_br_sb, "branch_sb")

    def merge_fn(i, tiles, consts):
        bf_, bs_, gf_, gs_ = [t.astype(F32) for t in tiles]
        return [_sigmoid(gf_) * bf_ + _sigmoid(gs_) * bs_], []

    gates_in = [(proj_b, d, gate_blk), (proj_b, d, gate_blk + 1)]
    merged, = _rowwise(merge_fn, [_whole(br_fox), _whole(br_sb)] + gates_in, [], [(d, BF16)], [], tm=tr, name="merge")
    h2 = _mm2d(merged, w_out_full, d, dims=NN, tm=tm, tn=tnd, tk=d, out_dtype=F32,
               epi=lambda accs, ex: [ex[0] + accs[0]], extras=[h1], name="out_proj")

    w_gu2, w_down2 = _send_wait(ffn2_copies, h2, "gather_ffn2_wait")
    saved3 = _ffn_up(h2, ffn2_norm, w_gu2, None, tm=tm, tag="ffn2")
    h3 = _ffn_down(h2, saved3[3], w_down2, tm=tm, tag="ffn2")

    skip = tile // tr

    def loss_fn(i, tiles, consts):
        real = i >= skip
        err = jnp.where(real, tiles[0] - tiles[1], 0.0)
        dy = err * (1.0 / d)
        part = 0.5 * jnp.sum(err * dy, axis=0, keepdims=True)
        return [dy, FFN_RESIDUAL_WEIGHT * dy], [part]

    dh3, dh3_half, loss_cols = _rowwise(
        loss_fn, [_whole(h3), _whole(loss_target[0])], [], [(d, F32, lp), (d, BF16, lp)], [(1, d)], tm=tr, name="loss",
        row_maps=[None, lambda i: jnp.maximum(i - skip, 0)])

    def own(g):
        return lax.dynamic_index_in_dim(g, me, 0, keepdims=False)

    def send_grads(grads_, name):
        return _send_start(grads_, [_landing(own(g), me) for g in grads_], "exchange", grads_[-1], name)

    sends = {}

    def send_piece(key):
        def on_ready(*gs):
            sends[key], token_ = send_grads(list(gs), f"exchange_{key}_start")
            return token_
        return on_ready

    dab3, dw_gu2, _ = _ffn_bwd_dw(dh3_half, saved3, w_down2, send_piece("down2"), tm=tm, tag="ffn2")
    dn3 = _ffn_bwd_dx(dab3, w_gu2, send_piece("gu2")(dw_gu2), tm=tm, tag="ffn2")
    dh2, dh2_bf, dg_ffn2 = _norm_bwd(dn3, h2, ffn2_norm, dh3, 1.0, tm=tr, name="ffn2_norm_bwd")

    dmerged = _mm2d(dh2_bf, w_out_full, d, dims=NT, tm=tm, tn=tnd, tk=d, out_dtype=BF16, name="out_proj_dx")
    dw_out = _mm2d(merged, dh2_bf, d, dims=TN, tm=tnd, tn=tnd, tk=tkr, out_dtype=BF16, name="out_proj_dw")
    token = send_piece("out")(dw_out.reshape(NDEV, d8, d))

    def merge_bwd_fn(i, tiles, consts):
        dm, bf_, bs_, gf_, gs_ = [t.astype(F32) for t in tiles]
        sf, ss = _sigmoid(gf_), _sigmoid(gs_)
        return [dm * sf, dm * ss, dm * bf_ * sf * (1.0 - sf), dm * bs_ * ss * (1.0 - ss)], []

    dbr_fox, dbr_sb, dg_fox, dg_sb = _rowwise(
        merge_bwd_fn, [_whole(dmerged), _whole(br_fox), _whole(br_sb)] + gates_in, [], [(d, BF16)] * 4, [], tm=tr, after=token,
        name="merge_bwd")

    tnw = _pick(w, [512, 256, 128])

    def branch_dx(dbr, w_blocks, after, name):
        return _mm((lp // tm, w // tnw, NDEV), dbr, _spec((tm, d8), lambda i, j, k: (i, k)),
                   [(w_blocks, _spec((None, tnw, d8), lambda i, j, k: (k, j, 0)))], [],
                   [((lp, w), BF16, _spec((tm, tnw), lambda i, j, k: (i, j)))], dims=NT, acc_shape=(tm, tnw), after=after,
                   name=name)[0]

    def branch_dw(o, dbr, name):
        return _mm((w // tnw, NDEV, lp // tkr), o, _spec((tkr, tnw), lambda i, j, k: (k, i)),
                   [(dbr, _spec((tkr, d8), lambda i, j, k: (k, j)))], [],
                   [((NDEV, w, d8), BF16, _spec((None, tnw, d8), lambda i, j, k: (j, i, 0)))],
                   dims=TN, acc_shape=(tnw, d8), name=name)[0]

    token = send_piece("branch")(branch_dw(o_fox, dbr_fox, "branch_fox_dw"), branch_dw(o_sb, dbr_sb, "branch_sb_dw"))
    do_fox = branch_dx(dbr_fox, w_br_fox, token, "branch_fox_dx")
    do_sb = branch_dx(dbr_sb, w_br_sb, token, "branch_sb_dx")

    dfq, dfk, dfv, dc_rows = _fox_bwd(fq, fk, proj_b, 0, c_rows, o_fox, do_fox, lse, tile=tile, pad=pad, name="fox_bwd")
    dsq, dsk, dsv = _sb_bwd(proj_b, HEADS, 2 * HEADS, 3 * HEADS, do_sb, sb_total, tile=tile, pad=pad, name="sb_bwd")
    dc = jnp.pad(jnp.transpose(dc_rows.reshape(HEADS, lp)), ((0, 0), (0, LANES - HEADS)))
    dlogf = _cumsum_rows(dc, tile=tile, reverse=True, name="forget_cumsum_bwd")
    dproj_a, dg_q, dg_k, dg_b = _fox_prep_bwd(proj_a, dfq, dfk, dlogf, q_gain, k_gain, b_pad, tm=tr, pad=pad, name="fox_prep_bwd")
    dproj_b = jnp.concatenate([dfv, dsq, dsk, dsv, dg_fox, dg_sb], axis=1)

    dw_pa = _mm2d(n2, dproj_a, na, dims=TN, tm=tnd, tn=na, tk=_pick(lp, [544, 256, 128]), out_dtype=BF16, name="proj_a_dw")
    dw_pb = _mm2d(n2, dproj_b, nbw, dims=TN, tm=tnd, tn=tnb, tk=tkr, out_dtype=BF16, name="proj_b_dw")
    dw_in = jnp.concatenate([dw_pa[:, :2 * w], dw_pb[:, :w], dw_pa[:, 2 * w:2 * w + HEADS], dw_pb[:, w:]], axis=1)
    token = send_piece("w_in")(_split_cols(dw_in))
    dn2_a = _mm2d(dproj_a, w_pa, d, dims=NT, tm=tm, tn=tnd, tk=na, out_dtype=F32, after=token, name="proj_a_dx")
    dn2 = _mm2d(dproj_b, w_pb, d, dims=NT, tm=tm, tn=tnd, tk=_pick(nbw, [2048, 1024, 512, 256, 128]), out_dtype=F32,
                epi=lambda accs, ex: [accs[0] + ex[0]], extras=[dn2_a], name="proj_b_dx")
    dh1, dh1_half, dg_mix = _norm_bwd(dn2, h1, mix_norm, dh2, FFN_RESIDUAL_WEIGHT, tm=tr, name="mix_norm_bwd")

    dab1, dw_gu1, _ = _ffn_bwd_dw(dh1_half, saved1, w_down1, send_piece("down1"), tm=tm, tag="ffn1")
    dn1 = _ffn_bwd_dx(dab1, w_gu1, send_piece("gu1")(dw_gu1), tm=tm, tag="ffn1")
    dh0, _, dg_ffn1 = _norm_bwd(dn1, h0, ffn1_norm, dh1, None, tm=tr, name="ffn1_norm_bwd")
    grad_x = dh0[tile:][None]

    r_down2, = _send_wait(sends["down2"], dh0, "exchange_down2_wait")
    r_gu2, = _send_wait(sends["gu2"], r_down2, "exchange_gu2_wait")
    r_out, = _send_wait(sends["out"], r_gu2, "exchange_out_wait")
    r_br_fox, r_br_sb = _send_wait(sends["branch"], r_out, "exchange_branch_wait")
    r_in, = _send_wait(sends["w_in"], r_br_sb, "exchange_w_in_wait")
    grads, deltas, new_ms, new_vs = {}, {}, {}, {}

    def adamw_big(entries):
        for k, (parts, sel, wt, mt, vt) in entries.items():
            g, dl, mn, vn = _adamw_summed(parts, sel, wt[0], mt[0], vt[0], name=f"adamw_{k}")
            grads[k], deltas[k], new_ms[k], new_vs[k] = g[None], dl[None], mn[None], vn[None]

    adamw_big(dict(ffn2_w_gate=(r_gu2, 0, ffn2_w_gate, m_ffn2_w_gate, v_ffn2_w_gate),
                   ffn2_w_up=(r_gu2, 1, ffn2_w_up, m_ffn2_w_up, v_ffn2_w_up),
                   ffn2_w_down=(r_down2, 0, ffn2_w_down, m_ffn2_w_down, v_ffn2_w_down),
                   w_in=(r_in, 0, w_in, m_w_in, v_w_in),
                   w_branch_fox=(r_br_fox, 0, w_branch_fox, m_w_branch_fox, v_w_branch_fox),
                   w_branch_sb=(r_br_sb, 0, w_branch_sb, m_w_branch_sb, v_w_branch_sb),
                   w_out=(r_out, 0, w_out, m_w_out, v_w_out)))
    done = sum(v[0, 0, :1] for v in new_vs.values())
    r_down1, = _send_wait(sends["down1"], done, "exchange_down1_wait")
    r_gu1, = _send_wait(sends["gu1"], r_down1, "exchange_gu1_wait")
    adamw_big(dict(ffn1_w_gate=(r_gu1, 0, ffn1_w_gate, m_ffn1_w_gate, v_ffn1_w_gate),
                   ffn1_w_up=(r_gu1, 1, ffn1_w_up, m_ffn1_w_up, v_ffn1_w_up),
                   ffn1_w_down=(r_down1, 0, ffn1_w_down, m_ffn1_w_down, v_ffn1_w_down)))

    small_parts = [dh0[pad:tile], dg_ffn1, dg_mix, dg_ffn2, dg_b[:, :HEADS], dg_q, dg_k, loss_cols]
    small_packed, small_spans = _pack_rows(small_parts, LANES, SMALL_ROWS)
    small_sum = _allsum_small(small_packed, "sum_small")
    g_meta_full, g_ffn1n, g_mixn, g_ffn2n, g_bf, g_qn, g_kn, loss_vec = [
        _unpack(small_sum, span, part.shape) for span, part in zip(small_spans, small_parts)]
    loss = jnp.sum(loss_vec)
    g_meta = lax.dynamic_slice_in_dim(g_meta_full, me * d8, d8, axis=1)
    g_qn, g_kn = g_qn.reshape(fox_q_norm.shape), g_kn.reshape(fox_k_norm.shape)

    small = dict(meta_tokens=(g_meta, meta_tokens, m_meta_tokens, v_meta_tokens),
                 ffn1_norm=(g_ffn1n, ffn1_norm, m_ffn1_norm, v_ffn1_norm),
                 mix_norm=(g_mixn, mix_norm, m_mix_norm, v_mix_norm),
                 b_forget=(g_bf, b_forget, m_b_forget, v_b_forget),
                 fox_q_norm=(g_qn, fox_q_norm, m_fox_q_norm, v_fox_q_norm),
                 fox_k_norm=(g_kn, fox_k_norm, m_fox_k_norm, v_fox_k_norm),
                 ffn2_norm=(g_ffn2n, ffn2_norm, m_ffn2_norm, v_ffn2_norm))
    for k, (g, wt, mt, vt) in small.items():
        flat = lambda t: t.reshape(-1, t.shape[-1])
        dl, mn, vn = _adamw_plain(flat(g), flat(wt), flat(mt), flat(vt), name=f"adamw_{k}")
        grads[k], deltas[k], new_ms[k], new_vs[k] = g, dl.reshape(wt.shape), mn.reshape(wt.shape), vn.reshape(wt.shape)

    order = ["meta_tokens", "ffn1_norm", "ffn1_w_gate", "ffn1_w_up", "ffn1_w_down", "mix_norm", "w_in", "b_forget",
             "fox_q_norm", "fox_k_norm", "w_branch_fox", "w_branch_sb", "w_out", "ffn2_norm", "ffn2_w_gate",
             "ffn2_w_up", "ffn2_w_down"]
    return (loss, grad_x, *[grads[k] for k in order], *[deltas[k] for k in order],
            *[new_ms[k] for k in order], *[new_vs[k] for k in order])
```

```python
import jax
import jax.numpy as jnp
from jax import lax
from jax.experimental import pallas as pl
from jax.experimental.pallas import tpu as pltpu

F32 = jnp.float32
BF16 = jnp.bfloat16
MESH = pl.DeviceIdType.MESH

NDEV = 8
N_META = 16
HEAD_DIM = 128
HEADS = 8
BRANCH_WIDTH = HEADS * HEAD_DIM
RMS_EPS = 1e-6
FFN_RESIDUAL_WEIGHT = 0.5
ATTN_SCALE = HEAD_DIM ** -0.5
MASKED_LOGIT = -1e30

ADAM_LR = 0.001
ADAM_B1 = 0.9
ADAM_B2 = 0.999
ADAM_EPS = 1e-08
ADAM_WD = 0.01
ADAM_STEP = 10

LANES = 128
SMALL_ROWS = 8
ADAMW_TILE_ELEMS = 160 * 1024
KSUB = 4


def _pick(n, prefs):
    for p in prefs:
        if p <= n and n % p == 0:
            return p
    return n


def _dot(a, b, dims):
    return lax.dot_general(a, b, (dims, ((), ())), preferred_element_type=F32)


NN = ((1,), (0,))
TN = ((0,), (0,))
NT = ((1,), (1,))


def _split_bf16(x):
    hi = x.astype(BF16)
    lo = (x - hi.astype(F32)).astype(BF16)
    return hi, lo


def _sigmoid_parts(z):
    e = jnp.exp(-jnp.abs(z))
    t = 1.0 + e
    return e, 1.0 / t, jnp.log(t)


def _sigmoid(x):
    return 0.5 * jnp.tanh(0.5 * x) + 0.5


def _my_place():
    return lax.axis_index("x"), lax.axis_index("y"), lax.axis_index("c")


def _flat_id(px, py, pc):
    return 4 * px + 2 * py + pc


def _peer(x, y, c, k):
    px = 1 - x if k & 4 else x
    py = 1 - y if k & 2 else y
    pc = 1 - c if k & 1 else c
    return px, py, pc


def _allgather(shards, name, in_vmem=False):
    n = len(shards)

    def body(*refs):
        x_refs, out_refs = refs[:n], refs[n:2 * n]
        send_sems, recv_sems, local_sems = refs[2 * n:]
        x, y, c = _my_place()
        me, sibling = (x, y, c), (x, y, 1 - c)
        chips = [(1 - x, y), (x, 1 - y), (1 - x, 1 - y)]

        def block(a, place):
            return out_refs[a].at[_flat_id(*place)]

        def copy(a, k, place, to, src=None):
            return pltpu.make_async_remote_copy(
                src_ref=block(a, place) if src is None else src, dst_ref=block(a, place),
                send_sem=send_sems.at[7 * a + k], recv_sem=recv_sems.at[7 * a + k], device_id=to, device_id_type=MESH)

        mine = [pltpu.make_async_copy(x_refs[a], block(a, me), local_sems.at[a]) for a in range(n)]
        for cp in mine:
            cp.start()
        first = []
        for a in range(n):
            first.append(copy(a, 0, me, sibling, src=x_refs[a]))
            first += [copy(a, 1 + j, me, (*chip, c), src=x_refs[a]) for j, chip in enumerate(chips)]
        for cp in first:
            cp.start()
        passed = []
        for j, chip in enumerate(chips):
            for a in range(n):
                copy(a, 1 + j, (*chip, c), me).wait_recv()
                passed.append(copy(a, 4 + j, (*chip, c), sibling))
                passed[-1].start()
        for a in range(n):
            copy(a, 0, sibling, me).wait_recv()
        for j, chip in enumerate(chips):
            for a in range(n):
                copy(a, 4 + j, (*chip, 1 - c), me).wait_recv()
        for cp in first + passed:
            cp.wait_send()
        for cp in mine:
            cp.wait()

    space = pltpu.VMEM if in_vmem else pl.ANY
    return pl.pallas_call(
        body, name=name,
        out_shape=[jax.ShapeDtypeStruct((NDEV,) + s.shape, s.dtype) for s in shards],
        in_specs=[pl.BlockSpec(memory_space=space)] * n,
        out_specs=[pl.BlockSpec(memory_space=space)] * n,
        scratch_shapes=[pltpu.SemaphoreType.DMA((7 * n,)), pltpu.SemaphoreType.DMA((7 * n,)), pltpu.SemaphoreType.DMA((n,))],
    )(*shards)


def _allsum_small(part, name):
    rows, cols = part.shape

    def body(p_ref, out_ref, buf, send_sems, recv_sems):
        x, y, c = _my_place()
        me = _flat_id(x, y, c)
        buf[me] = p_ref[...]
        copies = []
        for k in range(1, NDEV):
            copies.append(pltpu.make_async_remote_copy(
                src_ref=p_ref, dst_ref=buf.at[me], send_sem=send_sems.at[k - 1], recv_sem=recv_sems.at[k - 1],
                device_id=_peer(x, y, c, k), device_id_type=MESH))
        for cp in copies:
            cp.start()
        for cp in copies:
            cp.wait()
        total = buf[0]
        for j in range(1, NDEV):
            total = total + buf[j]
        out_ref[...] = total

    return pl.pallas_call(
        body, name=name,
        out_shape=jax.ShapeDtypeStruct((rows, cols), F32),
        in_specs=[pl.BlockSpec(memory_space=pltpu.VMEM)],
        out_specs=pl.BlockSpec(memory_space=pltpu.VMEM),
        scratch_shapes=[pltpu.VMEM((NDEV, rows, cols), F32),
                        pltpu.SemaphoreType.DMA((7,)), pltpu.SemaphoreType.DMA((7,))],
    )(part)


_HBM = pl.BlockSpec(memory_space=pltpu.HBM)
_SEM = pl.BlockSpec(memory_space=pltpu.SEMAPHORE)
_DATAFLOW = pltpu.SideEffectType.DATAFLOW_SIDE_EFFECTING


COPIES_PER_ARRAY = {"gather": 7, "exchange": 7, "first": 4, "forward": 3}


def _send_copies(plan, src_refs, land_refs, send_sems, recv_sems):
    x, y, c = _my_place()
    me = _flat_id(x, y, c)
    per = COPIES_PER_ARRAY[plan]
    copies = []

    def add(a, slot, src, dst, to):
        copies.append(pltpu.make_async_remote_copy(
            src_ref=src, dst_ref=dst, send_sem=send_sems.at[per * a + slot], recv_sem=recv_sems.at[per * a + slot],
            device_id=to, device_id_type=MESH))

    for a, land_ref in enumerate(land_refs):
        if plan in ("gather", "exchange"):
            for k in range(1, NDEV):
                peer = _peer(x, y, c, k)
                add(a, k - 1, src_refs[a].at[_flat_id(*peer)] if plan == "exchange" else src_refs[a], land_ref.at[me], peer)
        elif plan == "first":
            for slot, k in enumerate((1, 2, 4, 6)):
                add(a, slot, src_refs[a], land_ref.at[me], _peer(x, y, c, k))
        else:
            for slot, k in enumerate((2, 4, 6)):
                block = land_ref.at[_flat_id(*_peer(x, y, c, k))]
                add(a, slot, block, block, (x, y, 1 - c))
    return copies


def _send_start(srcs, lands, plan, after, name):
    ns, nl = len(srcs), len(lands)
    nsem = COPIES_PER_ARRAY[plan] * nl

    def body(*refs):
        for cp in _send_copies(plan, refs[:ns], refs[ns:ns + nl], refs[ns + nl + 1], refs[ns + nl + 2]):
            cp.start()
        refs[-1][...] = jnp.zeros_like(refs[-1])

    operands = [pltpu.with_memory_space_constraint(t, pltpu.HBM) for t in list(srcs) + list(lands)]
    outs = pl.pallas_call(
        body, name=name,
        out_shape=(pltpu.SemaphoreType.DMA((nsem,)), pltpu.SemaphoreType.DMA((nsem,)),
                   *[pltpu.HBM(t.shape, t.dtype) for t in operands[ns:]], jax.ShapeDtypeStruct((SMALL_ROWS, LANES), F32)),
        in_specs=[_HBM] * (ns + nl) + [pl.BlockSpec(memory_space=pl.ANY)],
        out_specs=(_SEM, _SEM, *[_HBM] * nl, pl.BlockSpec(memory_space=pltpu.VMEM)),
        input_output_aliases={ns + i: 2 + i for i in range(nl)},
        compiler_params=pltpu.CompilerParams(has_side_effects=_DATAFLOW),
    )(*operands, after)
    return (plan, outs[0], outs[1], operands[:ns], list(outs[2:2 + nl])), outs[-1]


def _send_wait(handle, after, name):
    plan, send_sems, recv_sems, srcs, lands = handle
    ns, nl = len(srcs), len(lands)

    def body(*refs):
        for cp in _send_copies(plan, refs[:ns], refs[ns:ns + nl], refs[ns + nl], refs[ns + nl + 1]):
            cp.wait_send()
            cp.wait_recv()

    outs = pl.pallas_call(
        body, name=name,
        out_shape=tuple(pltpu.HBM(t.shape, t.dtype) for t in lands),
        in_specs=[_HBM] * (ns + nl) + [_SEM, _SEM, pl.BlockSpec(memory_space=pl.ANY)],
        out_specs=tuple([_HBM] * nl),
        input_output_aliases={ns + i: i for i in range(nl)},
        compiler_params=pltpu.CompilerParams(has_side_effects=_DATAFLOW),
    )(*srcs, *lands, send_sems, recv_sems, after)
    return list(outs)


def _gather_two_level(first_handle, after, name):
    lands = _send_wait(first_handle, after, f"{name}_wait")
    forward, _ = _send_start([], lands, "forward", after, f"{name}_forward_start")
    return _send_wait(forward, after, f"{name}_forward_wait")


def _landing(own_block, me):
    return lax.dynamic_update_index_in_dim(lax.empty((NDEV,) + own_block.shape, own_block.dtype), own_block, me, 0)


def _spec(block, index_map):
    return pl.BlockSpec(block, index_map)


def _mm(grid, a, a_spec, bs, extras, outs, *, dims, acc_shape=None, epi=None, after=None, name):
    nk = grid[2]
    nb, ne, no = len(bs), len(extras), len(outs)
    nafter = 0 if after is None else 1
    if epi is None:
        epi = lambda accs, ex: [accs[0]]

    def body(*refs):
        a_ref, b_refs = refs[0], refs[1:1 + nb]
        e_refs = refs[1 + nb:1 + nb + ne]
        o_refs = refs[1 + nb + ne + nafter:1 + nb + ne + nafter + no]
        acc_refs = refs[1 + nb + ne + nafter + no:]
        def finish(accs):
            for o_ref, tile in zip(o_refs, epi(accs, [e_ref[...] for e_ref in e_refs])):
                o_ref[...] = tile.astype(o_ref.dtype)

        def product(b_ref):
            if len(a_ref.shape) == 2:
                return _dot(a_ref[...], b_ref[...], dims)
            total = _dot(a_ref[0], b_ref[0], dims)
            for t in range(1, a_ref.shape[0]):
                total = total + _dot(a_ref[t], b_ref[t], dims)
            return total

        if nk == 1:
            finish([product(b_ref) for b_ref in b_refs])
        else:
            k = pl.program_id(2)

            @pl.when(k == 0)
            def _():
                for acc_ref in acc_refs:
                    acc_ref[...] = jnp.zeros_like(acc_ref)

            for acc_ref, b_ref in zip(acc_refs, b_refs):
                acc_ref[...] += product(b_ref)

            @pl.when(k == nk - 1)
            def _():
                finish([acc_ref[...] for acc_ref in acc_refs])

    return pl.pallas_call(
        body, name=name,
        grid=grid,
        in_specs=[a_spec] + [s for _, s in bs] + [s for _, s in extras] + [pl.BlockSpec(memory_space=pl.ANY)] * nafter,
        out_specs=[s for _, _, s in outs],
        out_shape=[jax.ShapeDtypeStruct(shape, dt) for shape, dt, _ in outs],
        scratch_shapes=[pltpu.VMEM(acc_shape, F32) for _ in bs] if nk > 1 else [],
        compiler_params=pltpu.CompilerParams(dimension_semantics=("parallel", "parallel", "arbitrary")),
    )(a, *[b for b, _ in bs], *[e for e, _ in extras], *([after] if nafter else []))


def _mm2d(a, b, n_cols, *, dims, tm, tn, tk, out_dtype, epi=None, extras=(), after=None, name):
    m_rows, k_len = (a.shape[1], a.shape[0]) if dims == TN else a.shape
    assert m_rows % tm == 0 and n_cols % tn == 0 and k_len % tk == 0, (name, a.shape, n_cols, tm, tn, tk)
    a_spec = _spec((tk, tm), lambda i, j, k: (k, i)) if dims == TN else _spec((tm, tk), lambda i, j, k: (i, k))
    b_spec = _spec((tn, tk), lambda i, j, k: (j, k)) if dims == NT else _spec((tk, tn), lambda i, j, k: (k, j))
    tile = _spec((tm, tn), lambda i, j, k: (i, j))
    return _mm((m_rows // tm, n_cols // tn, k_len // tk), a, a_spec, [(b, b_spec)], [(e, tile) for e in extras],
               [((m_rows, n_cols), out_dtype, tile)], dims=dims, acc_shape=(tm, tn), epi=epi, after=after, name=name)[0]


def _rowwise(fn, ins, consts, outs, sums, *, tm, name, row_maps=None, after=None):
    m_rows = outs[0][2] if len(outs[0]) == 3 else ins[0][0].shape[0]
    n = m_rows // tm
    ni, nc, no = len(ins), len(consts), len(outs)
    nafter = 0 if after is None else 1
    row_maps = row_maps or [None] * ni

    def body(*refs):
        i = pl.program_id(0)
        in_tiles = [r[...] for r in refs[:ni]]
        const_values = [r[...] for r in refs[ni:ni + nc]]
        o_refs = refs[ni + nc + nafter:ni + nc + nafter + no]
        s_refs = refs[ni + nc + nafter + no:]
        out_tiles, sum_terms = fn(i, in_tiles, const_values)
        for o_ref, tile in zip(o_refs, out_tiles):
            o_ref[...] = tile.astype(o_ref.dtype)
        if s_refs:
            @pl.when(i == 0)
            def _():
                for s_ref in s_refs:
                    s_ref[...] = jnp.zeros_like(s_ref)

            for s_ref, term in zip(s_refs, sum_terms):
                s_ref[...] += term

    def in_spec(width, col, rmap):
        if rmap is None:
            return pl.BlockSpec((tm, width), lambda i: (i, col))
        return pl.BlockSpec((tm, width), lambda i: (rmap(i), col))

    return pl.pallas_call(
        body, name=name,
        grid=(n,),
        in_specs=[in_spec(w, col, rmap) for (_, w, col), rmap in zip(ins, row_maps)]
        + [pl.BlockSpec(cst.shape, lambda i: (0, 0)) for cst in consts] + [pl.BlockSpec(memory_space=pl.ANY)] * nafter,
        out_specs=[pl.BlockSpec((tm, o[0]), lambda i: (i, 0)) for o in outs]
        + [pl.BlockSpec(s, lambda i: (0, 0)) for s in sums],
        out_shape=[jax.ShapeDtypeStruct((m_rows, o[0]), o[1]) for o in outs]
        + [jax.ShapeDtypeStruct(s, F32) for s in sums],
        compiler_params=pltpu.CompilerParams(dimension_semantics=("arbitrary",)),
    )(*[arr for arr, _, _ in ins], *consts, *([after] if nafter else []))


def _whole(arr):
    return (arr, arr.shape[1], 0)


def _rms(x, gain):
    r = lax.rsqrt(jnp.mean(x * x, axis=-1, keepdims=True) + RMS_EPS)
    return x * r * gain


def _rms_bwd(x, gain, dy):
    r = lax.rsqrt(jnp.mean(x * x, axis=-1, keepdims=True) + RMS_EPS)
    u = dy * gain
    dx = r * u - x * (r * r * r) * jnp.mean(x * u, axis=-1, keepdims=True)
    return dx, dy * x * r


def _norm_fwd(h, gain, *, tm, name):
    def fn(i, tiles, consts):
        return [_rms(tiles[0], consts[0])], []
    return _rowwise(fn, [_whole(h)], [gain], [(h.shape[1], BF16)], [], tm=tm, name=name)[0]


def _norm_bwd(dn, h, gain, dh_in, low_scale, *, tm, name, after=None):
    d = h.shape[1]

    def fn(i, tiles, consts):
        dx, dg_rows = _rms_bwd(tiles[1], consts[0], tiles[0])
        dh = tiles[2] + dx
        return [dh] + ([] if low_scale is None else [low_scale * dh]), [jnp.sum(dg_rows, axis=0, keepdims=True)]

    outs = _rowwise(fn, [_whole(dn), _whole(h), _whole(dh_in)], [gain],
                    [(d, F32)] + ([] if low_scale is None else [(d, BF16)]), [(1, d)], tm=tm, name=name, after=after)
    return (outs[0], None, outs[1]) if low_scale is None else tuple(outs)


def _swiglu_epi(accs, ex):
    a, b = accs
    return [a, b, a * _sigmoid(a) * b]


def _swiglu_bwd_epi(accs, ex):
    ds = accs[0]
    a, b = ex[0].astype(F32), ex[1].astype(F32)
    sig = _sigmoid(a)
    silu = a * sig
    dsilu = sig * (1.0 + a * (1.0 - sig))
    return [jnp.stack([(ds * b * dsilu).astype(BF16), (ds * silu).astype(BF16)], axis=0)]


HEAD = "head"


def _attn_mask(i, j, tile, pad, strict, masked=True):
    if not masked:
        return None
    if masked == HEAD:
        return (tile // 2 + lax.broadcasted_iota(jnp.int32, (tile, tile // 2), 1)) >= pad
    row = i * tile + lax.broadcasted_iota(jnp.int32, (tile, tile), 0)
    col = j * tile + lax.broadcasted_iota(jnp.int32, (tile, tile), 1)
    causal = (col < row) if strict else (col <= row)
    return causal & ((col >= pad) | (row < pad))


FWD_GROUP = 4
BWD_GROUP = 2


def _walk_key_tiles(i, step, carry, widths, descending=False):
    diagonal = lambda j, c: step(j, c, True, 1)

    def tile_zero(c):
        c = lax.fori_loop(0, (i == 0).astype(jnp.int32), lambda _, cc: step(0, cc, True, 1), c)
        return lax.fori_loop(0, (i > 0).astype(jnp.int32), lambda _, cc: step(0, cc, HEAD, 1), c)

    left = jnp.maximum(i - 1, 0)
    lo = 1
    if descending:
        carry = lax.fori_loop(jnp.maximum(i, 1), i + 1, diagonal, carry)
    else:
        carry = tile_zero(carry)
    for nt in widths:
        count = left // nt
        if descending:
            top = lo + left
            carry = lax.fori_loop(0, count, lambda t, c, nt=nt, top=top: step(top - nt * (t + 1), c, False, nt), carry)
        else:
            carry = lax.fori_loop(0, count, lambda t, c, nt=nt, lo=lo: step(lo + nt * t, c, False, nt), carry)
            lo = lo + nt * count
        left = left - nt * count
    if descending:
        return tile_zero(carry)
    return lax.fori_loop(jnp.maximum(i, 1), i + 1, diagonal, carry)


def _head_cols(g):
    return pl.ds(g * HEAD_DIM, HEAD_DIM)


def _key_rows(j, tile, nt, masked=False):
    if masked == HEAD:
        return pl.ds(tile // 2, tile // 2)
    return pl.ds(pl.multiple_of(j * tile, tile), nt * tile)


def _key_tile(ref, j, tile, g, nt=1, masked=False):
    return ref[_key_rows(j, tile, nt, masked), _head_cols(g)]


def _key_bias(c_ref, g, j, nt, masked=False):
    if masked == HEAD:
        tile = c_ref.shape[-1]
        return c_ref[g, 0][:, tile // 2:]
    return c_ref[g, j] if nt == 1 else jnp.concatenate([c_ref[g, j + t] for t in range(nt)], axis=1)


def _head_specs(lp, tile, q_off, k_off, v_off, hp):
    gw = hp * HEAD_DIM
    q_spec = pl.BlockSpec((tile, gw), lambda h, i: (i, h + q_off // hp))
    k_spec = pl.BlockSpec((lp, gw), lambda h, i: (0, h + k_off // hp))
    v_spec = pl.BlockSpec((lp, gw), lambda h, i: (0, h + v_off // hp))
    return q_spec, k_spec, v_spec


def _fox_fwd(q, k, v, v_off, c_rows, *, tile, pad, name):
    lp = q.shape[0]
    nb = lp // tile
    hp = FWD_GROUP
    gw = hp * HEAD_DIM

    def body(q_ref, k_ref, v_ref, c_ref, o_ref, lse_ref):
        i = pl.program_id(1)
        qts = [q_ref[:, _head_cols(g)] for g in range(hp)]

        def step(j, carry, masked, nt):
            ok = _attn_mask(i, j, tile, pad, False, masked)
            out = []
            for g, (m, l, acc) in enumerate(carry):
                s = _dot(qts[g], _key_tile(k_ref, j, tile, g, nt, masked), NT) * ATTN_SCALE - _key_bias(c_ref, g, j, nt, masked)
                if masked:
                    s = jnp.where(ok, s, MASKED_LOGIT)
                m_new = jnp.maximum(m, jnp.max(s, axis=1, keepdims=True))
                p = jnp.exp(s - m_new)
                alpha = jnp.exp(m - m_new)
                l = alpha * l + jnp.sum(p, axis=1, keepdims=True)
                acc = alpha * acc + _dot(p.astype(BF16), _key_tile(v_ref, j, tile, g, nt, masked), NN)
                out.append((m_new, l, acc))
            return tuple(out)

        init = (jnp.full((tile, 1), MASKED_LOGIT, F32), jnp.zeros((tile, 1), F32), jnp.zeros((tile, HEAD_DIM), F32))
        final = _walk_key_tiles(i, step, (init,) * hp, (8, 4, 2, 1))
        for g, (m, l, acc) in enumerate(final):
            o_ref[:, _head_cols(g)] = (acc / l).astype(o_ref.dtype)
            lse_ref[g] = jnp.broadcast_to(m + jnp.log(l), (tile, LANES))

    q_spec, k_spec, v_spec = _head_specs(lp, tile, 0, 0, v_off, hp)
    return pl.pallas_call(
        body, name=name,
        grid=(HEADS // hp, nb),
        in_specs=[q_spec, k_spec, v_spec, pl.BlockSpec((hp, nb, 1, tile), lambda h, i: (h, 0, 0, 0))],
        out_specs=[pl.BlockSpec((tile, gw), lambda h, i: (i, h)),
                   pl.BlockSpec((hp, tile, LANES), lambda h, i: (h, i, 0))],
        out_shape=[jax.ShapeDtypeStruct((lp, BRANCH_WIDTH), BF16), jax.ShapeDtypeStruct((HEADS, lp, LANES), F32)],
        compiler_params=pltpu.CompilerParams(dimension_semantics=("parallel", "arbitrary")),
    )(q, k, v, c_rows)


def _fox_bwd(q, k, v, v_off, c_rows, o, do, lse, *, tile, pad, name):
    lp = q.shape[0]
    nb = lp // tile
    hp = BWD_GROUP
    gw = hp * HEAD_DIM

    def body(q_ref, k_ref, v_ref, c_ref, o_ref, do_ref, lse_ref, dq_ref, dk_ref, dv_ref, dc_ref, dk_acc, dv_acc, dc_acc):
        i = pl.program_id(1)

        @pl.when(i == 0)
        def _():
            dk_acc[...] = jnp.zeros_like(dk_acc)
            dv_acc[...] = jnp.zeros_like(dv_acc)
            dc_acc[...] = jnp.zeros_like(dc_acc)

        heads = range(hp)
        qts = [q_ref[:, _head_cols(g)] for g in heads]
        dots = [do_ref[:, _head_cols(g)] for g in heads]
        deltas = [jnp.sum(dots[g].astype(F32) * o_ref[:, _head_cols(g)].astype(F32), axis=1, keepdims=True) for g in heads]
        lse_cols = [lse_ref[g][:, :1] for g in heads]

        def step(j, dqs, masked, nt):
            rows = _key_rows(j, tile, nt, masked)
            ok = _attn_mask(i, j, tile, pad, False, masked)
            out = []
            for g in heads:
                kt = _key_tile(k_ref, j, tile, g, nt, masked)
                s = _dot(qts[g], kt, NT) * ATTN_SCALE - _key_bias(c_ref, g, j, nt, masked)
                p = jnp.exp(s - lse_cols[g])
                if masked:
                    p = jnp.where(ok, p, 0.0)
                dp = _dot(dots[g], _key_tile(v_ref, j, tile, g, nt, masked), NT)
                ds = p * (dp - deltas[g])
                dsb = ds.astype(BF16)
                dk_acc[rows, _head_cols(g)] += _dot(dsb, qts[g], TN)
                dv_acc[rows, _head_cols(g)] += _dot(p.astype(BF16), dots[g], TN)
                dc = -jnp.sum(ds, axis=0, keepdims=True)
                if masked == HEAD:
                    dc_acc[g, 0, :, pl.ds(tile // 2, tile // 2)] += dc
                else:
                    for t in range(nt):
                        dc_acc[g, j + t] += dc[:, t * tile:(t + 1) * tile]
                out.append(dqs[g] + _dot(dsb, kt, NN))
            return tuple(out)

        dqs = _walk_key_tiles(i, step, (jnp.zeros((tile, HEAD_DIM), F32),) * hp, (8, 4, 2, 1))
        for g in heads:
            dq_ref[:, _head_cols(g)] = dqs[g] * ATTN_SCALE

        @pl.when(i == nb - 1)
        def _():
            dk_ref[...] = dk_acc[...] * ATTN_SCALE
            dv_ref[...] = dv_acc[...].astype(dv_ref.dtype)
            dc_ref[...] = dc_acc[...]

    q_spec, k_spec, v_spec = _head_specs(lp, tile, 0, 0, v_off, hp)
    tile_spec = pl.BlockSpec((tile, gw), lambda h, i: (i, h))
    head_spec = pl.BlockSpec((lp, gw), lambda h, i: (0, h))
    c_spec = pl.BlockSpec((hp, nb, 1, tile), lambda h, i: (h, 0, 0, 0))
    return pl.pallas_call(
        body, name=name,
        grid=(HEADS // hp, nb),
        in_specs=[q_spec, k_spec, v_spec, c_spec, tile_spec, tile_spec,
                  pl.BlockSpec((hp, tile, LANES), lambda h, i: (h, i, 0))],
        out_specs=[tile_spec, head_spec, head_spec, c_spec],
        out_shape=[jax.ShapeDtypeStruct((lp, BRANCH_WIDTH), F32), jax.ShapeDtypeStruct((lp, BRANCH_WIDTH), F32),
                   jax.ShapeDtypeStruct((lp, BRANCH_WIDTH), BF16), jax.ShapeDtypeStruct((HEADS, nb, 1, tile), F32)],
        scratch_shapes=[pltpu.VMEM((lp, gw), F32), pltpu.VMEM((lp, gw), F32),
                        pltpu.VMEM((hp, nb, 1, tile), F32)],
        compiler_params=pltpu.CompilerParams(dimension_semantics=("parallel", "arbitrary")),
    )(q, k, v, c_rows, o, do, lse)


def _later_matrix(tile):
    return (lax.broadcasted_iota(jnp.int32, (tile, tile), 0) > lax.broadcasted_iota(jnp.int32, (tile, tile), 1)).astype(BF16)


def _earlier_matrix(tile):
    return (lax.broadcasted_iota(jnp.int32, (tile, tile), 0) < lax.broadcasted_iota(jnp.int32, (tile, tile), 1)).astype(BF16)


def _running_sums(x, tri, suffix):
    tile = tri.shape[0]
    blocks = [x[:, b:b + tile] for b in range(0, x.shape[1], tile)]
    sums = [jnp.sum(blk, axis=1, keepdims=True) for blk in blocks]
    out = []
    for b, blk in enumerate(blocks):
        hi, lo = _split_bf16(blk)
        inside = _dot(hi, tri, NN) + _dot(lo, tri, NN)
        for other in (sums[b + 1:] if suffix else sums[:b]):
            inside = inside + other
        out.append(inside)
    total = sums[0]
    for other in sums[1:]:
        total = total + other
    return (out[0] if len(out) == 1 else jnp.concatenate(out, axis=1)), total


def _sb_tile(qt, kt, ok, later):
    z = _dot(qt, kt, NT) * ATTN_SCALE
    e, r, lg = _sigmoid_parts(z)
    sp = jnp.maximum(z, 0.0) + lg
    spm = sp if ok is None else jnp.where(ok, sp, 0.0)
    within, sp_here = _running_sums(spm, later, True)
    return z, e, r, sp, sp_here, within


def _sb_fwd(qkv, q_off, k_off, v_off, *, tile, pad, name):
    lp = qkv.shape[0]
    nb = lp // tile
    hp = FWD_GROUP
    gw = hp * HEAD_DIM

    def body(q_ref, k_ref, v_ref, o_ref, tot_ref):
        i = pl.program_id(1)
        qts = [q_ref[:, _head_cols(g)] for g in range(hp)]
        later = dict.fromkeys((False, True), _later_matrix(tile)) | {HEAD: _later_matrix(tile // 2)}

        def step(j, carry, masked, nt):
            ok = _attn_mask(i, j, tile, pad, True, masked)
            out = []
            for g, (right, acc) in enumerate(carry):
                z, _, _, sp, sp_here, within = _sb_tile(qts[g], _key_tile(k_ref, j, tile, g, nt, masked), ok, later[masked])
                w = jnp.exp(z - sp - within - right)
                if masked:
                    w = jnp.where(ok, w, 0.0)
                acc = acc + _dot(w.astype(BF16), _key_tile(v_ref, j, tile, g, nt, masked), NN)
                out.append((right + sp_here, acc))
            return tuple(out)

        init = (jnp.zeros((tile, 1), F32), jnp.zeros((tile, HEAD_DIM), F32))
        final = _walk_key_tiles(i, step, (init,) * hp, (4, 2, 1), descending=True)
        for g, (total, acc) in enumerate(final):
            o_ref[:, _head_cols(g)] = acc.astype(o_ref.dtype)
            tot_ref[g] = jnp.broadcast_to(total, (tile, LANES))

    q_spec, k_spec, v_spec = _head_specs(lp, tile, q_off, k_off, v_off, hp)
    return pl.pallas_call(
        body, name=name,
        grid=(HEADS // hp, nb),
        in_specs=[q_spec, k_spec, v_spec],
        out_specs=[pl.BlockSpec((tile, gw), lambda h, i: (i, h)),
                   pl.BlockSpec((hp, tile, LANES), lambda h, i: (h, i, 0))],
        out_shape=[jax.ShapeDtypeStruct((lp, BRANCH_WIDTH), BF16), jax.ShapeDtypeStruct((HEADS, lp, LANES), F32)],
        compiler_params=pltpu.CompilerParams(dimension_semantics=("parallel", "arbitrary")),
    )(qkv, qkv, qkv)


def _sb_bwd(qkv, q_off, k_off, v_off, do, total, *, tile, pad, name):
    lp = qkv.shape[0]
    nb = lp // tile
    hp = BWD_GROUP
    gw = hp * HEAD_DIM

    def body(q_ref, k_ref, v_ref, do_ref, tot_ref, dq_ref, dk_ref, dv_ref, dk_acc, dv_acc):
        i = pl.program_id(1)

        @pl.when(i == 0)
        def _():
            dk_acc[...] = jnp.zeros_like(dk_acc)
            dv_acc[...] = jnp.zeros_like(dv_acc)

        heads = range(hp)
        qts = [q_ref[:, _head_cols(g)] for g in heads]
        dots = [do_ref[:, _head_cols(g)] for g in heads]
        total_cols = [tot_ref[g][:, :1] for g in heads]
        later = dict.fromkeys((False, True), _later_matrix(tile)) | {HEAD: _later_matrix(tile // 2)}
        earlier = dict.fromkeys((False, True), _earlier_matrix(tile)) | {HEAD: _earlier_matrix(tile // 2)}

        def step(j, carry, masked, nt):
            rows = _key_rows(j, tile, nt, masked)
            ok = _attn_mask(i, j, tile, pad, True, masked)
            out = []
            for g, (dq, sp_before, dlw_before) in enumerate(carry):
                kt = _key_tile(k_ref, j, tile, g, nt, masked)
                z, e, r, sp, sp_here, within = _sb_tile(qts[g], kt, ok, later[masked])
                right = total_cols[g] - sp_before - sp_here
                w = jnp.exp(z - sp - within - right)
                if masked:
                    w = jnp.where(ok, w, 0.0)
                dlw = w * _dot(dots[g], _key_tile(v_ref, j, tile, g, nt, masked), NT)
                before, dlw_here = _running_sums(dlw, earlier[masked], False)
                sig = jnp.where(z >= 0, r, e * r)
                dz = dlw * (1.0 - sig) - sig * (before + dlw_before)
                if masked:
                    dz = jnp.where(ok, dz, 0.0)
                dzb = dz.astype(BF16)
                dk_acc[rows, _head_cols(g)] += _dot(dzb, qts[g], TN)
                dv_acc[rows, _head_cols(g)] += _dot(w.astype(BF16), dots[g], TN)
                out.append((dq + _dot(dzb, kt, NN), sp_before + sp_here, dlw_before + dlw_here))
            return tuple(out)

        zero_col = jnp.zeros((tile, 1), F32)
        final = _walk_key_tiles(i, step, ((jnp.zeros((tile, HEAD_DIM), F32), zero_col, zero_col),) * hp, (4, 2, 1))
        for g in heads:
            dq_ref[:, _head_cols(g)] = (final[g][0] * ATTN_SCALE).astype(dq_ref.dtype)

        @pl.when(i == nb - 1)
        def _():
            dk_ref[...] = (dk_acc[...] * ATTN_SCALE).astype(dk_ref.dtype)
            dv_ref[...] = dv_acc[...].astype(dv_ref.dtype)

    q_spec, k_spec, v_spec = _head_specs(lp, tile, q_off, k_off, v_off, hp)
    tile_spec = pl.BlockSpec((tile, gw), lambda h, i: (i, h))
    head_spec = pl.BlockSpec((lp, gw), lambda h, i: (0, h))
    return pl.pallas_call(
        body, name=name,
        grid=(HEADS // hp, nb),
        in_specs=[q_spec, k_spec, v_spec, tile_spec, pl.BlockSpec((hp, tile, LANES), lambda h, i: (h, i, 0))],
        out_specs=[tile_spec, head_spec, head_spec],
        out_shape=[jax.ShapeDtypeStruct((lp, BRANCH_WIDTH), BF16)] * 3,
        scratch_shapes=[pltpu.VMEM((lp, gw), F32), pltpu.VMEM((lp, gw), F32)],
        compiler_params=pltpu.CompilerParams(dimension_semantics=("parallel", "arbitrary")),
    )(qkv, qkv, qkv, do, total)


def _cumsum_rows(x, *, tile, reverse, name):
    lp = x.shape[0]
    nb = lp // tile

    def body(x_ref, o_ref, carry):
        @pl.when(pl.program_id(0) == 0)
        def _():
            carry[...] = jnp.zeros_like(carry)

        r = lax.broadcasted_iota(jnp.int32, (tile, tile), 0)
        c = lax.broadcasted_iota(jnp.int32, (tile, tile), 1)
        tri = ((c >= r) if reverse else (c <= r)).astype(BF16)
        hi, lo = _split_bf16(x_ref[...])
        run = _dot(tri, hi, NN) + _dot(tri, lo, NN) + carry[...]
        o_ref[...] = run
        carry[...] = run[:1, :] if reverse else run[tile - 1:, :]

    order = (lambda i: (nb - 1 - i, 0)) if reverse else (lambda i: (i, 0))
    return pl.pallas_call(
        body, name=name,
        grid=(nb,),
        in_specs=[pl.BlockSpec((tile, LANES), order)],
        out_specs=pl.BlockSpec((tile, LANES), order),
        out_shape=jax.ShapeDtypeStruct((lp, LANES), F32),
        scratch_shapes=[pltpu.VMEM((1, LANES), F32)],
        compiler_params=pltpu.CompilerParams(dimension_semantics=("arbitrary",)),
    )(x)


def _log_sigmoid(x):
    return jnp.minimum(x, 0.0) - jnp.log(1.0 + jnp.exp(-jnp.abs(x)))


def _forget_mask(i, tm, pad):
    row = i * tm + lax.broadcasted_iota(jnp.int32, (tm, LANES), 0)
    lane = lax.broadcasted_iota(jnp.int32, (tm, LANES), 1)
    return (row >= pad) & (lane < HEADS)


def _fox_prep(proj_a, q_gain, k_gain, b_forget, *, tm, pad, name):
    w = BRANCH_WIDTH

    def fn(i, tiles, consts):
        pa = tiles[0]
        qs, ks = [], []
        for h in range(HEADS):
            lo = h * HEAD_DIM
            qs.append(_rms(pa[:, lo:lo + HEAD_DIM], consts[0][:, lo:lo + HEAD_DIM]))
            ks.append(_rms(pa[:, w + lo:w + lo + HEAD_DIM], consts[1][:, lo:lo + HEAD_DIM]))
        logf = jnp.where(_forget_mask(i, tm, pad), _log_sigmoid(pa[:, 2 * w:] + consts[2]), 0.0)
        return [jnp.concatenate(qs, axis=1), jnp.concatenate(ks, axis=1), logf], []

    return _rowwise(fn, [_whole(proj_a)], [q_gain, k_gain, b_forget],
                    [(w, BF16), (w, BF16), (LANES, F32)], [], tm=tm, name=name)


def _fox_prep_bwd(proj_a, dq, dk, dlogf, q_gain, k_gain, b_forget, *, tm, pad, name):
    w = BRANCH_WIDTH

    def fn(i, tiles, consts):
        pa, dqt, dkt, dlf = tiles
        dxs_q, dxs_k, dgs_q, dgs_k = [], [], [], []
        for h in range(HEADS):
            lo = h * HEAD_DIM
            dx, dg = _rms_bwd(pa[:, lo:lo + HEAD_DIM], consts[0][:, lo:lo + HEAD_DIM], dqt[:, lo:lo + HEAD_DIM])
            dxs_q.append(dx)
            dgs_q.append(jnp.sum(dg, axis=0, keepdims=True))
            dx, dg = _rms_bwd(pa[:, w + lo:w + lo + HEAD_DIM], consts[1][:, lo:lo + HEAD_DIM], dkt[:, lo:lo + HEAD_DIM])
            dxs_k.append(dx)
            dgs_k.append(jnp.sum(dg, axis=0, keepdims=True))
        xf = pa[:, 2 * w:] + consts[2]
        e, r, _ = _sigmoid_parts(xf)
        df = jnp.where(_forget_mask(i, tm, pad), dlf * jnp.where(xf >= 0, e * r, r), 0.0)
        return ([jnp.concatenate(dxs_q + dxs_k + [df], axis=1)],
                [jnp.concatenate(dgs_q, axis=1), jnp.concatenate(dgs_k, axis=1), jnp.sum(df, axis=0, keepdims=True)])

    return _rowwise(fn, [_whole(proj_a), _whole(dq), _whole(dk), _whole(dlogf)], [q_gain, k_gain, b_forget],
                    [(2 * w + LANES, BF16)], [(1, w), (1, w), (1, LANES)], tm=tm, name=name)


def _adamw_math(w, g, m, v):
    m = ADAM_B1 * m + (1.0 - ADAM_B1) * g
    v = ADAM_B2 * v + (1.0 - ADAM_B2) * (g * g)
    m_hat = m / (1.0 - ADAM_B1 ** ADAM_STEP)
    v_hat = v / (1.0 - ADAM_B2 ** ADAM_STEP)
    delta = -ADAM_LR * (m_hat / (jnp.sqrt(v_hat) + ADAM_EPS) + ADAM_WD * w)
    return delta, m, v


def _adamw_summed(parts, sel, w, m, v, *, name):
    rows, cols = w.shape
    tr = _pick(rows, [t for t in (512, 256, 128, 64, 32, 16, 8) if t * cols <= ADAMW_TILE_ELEMS])

    def body(p_ref, w_ref, m_ref, v_ref, g_out, d_out, m_out, v_out):
        g = p_ref[0].astype(F32)
        for j in range(1, NDEV):
            g = g + p_ref[j].astype(F32)
        delta, m_new, v_new = _adamw_math(w_ref[...], g, m_ref[...], v_ref[...])
        g_out[...] = g
        d_out[...] = delta
        m_out[...] = m_new
        v_out[...] = v_new

    spec = pl.BlockSpec((tr, cols), lambda i: (i, 0))
    if parts.ndim == 3:
        p_spec = pl.BlockSpec((NDEV, tr, cols), lambda i: (0, i, 0))
    else:
        p_spec = pl.BlockSpec((NDEV, None, tr, cols), lambda i: (0, sel, i, 0))
    return pl.pallas_call(
        body, name=name,
        grid=(rows // tr,),
        in_specs=[p_spec, spec, spec, spec],
        out_specs=[spec] * 4,
        out_shape=[jax.ShapeDtypeStruct((rows, cols), F32)] * 4,
        compiler_params=pltpu.CompilerParams(dimension_semantics=("parallel",)),
    )(parts, w, m, v)


def _adamw_plain(g, w, m, v, *, name):
    def body(g_ref, w_ref, m_ref, v_ref, d_out, m_out, v_out):
        delta, m_new, v_new = _adamw_math(w_ref[...], g_ref[...], m_ref[...], v_ref[...])
        d_out[...] = delta
        m_out[...] = m_new
        v_out[...] = v_new

    return pl.pallas_call(body, name=name, out_shape=[jax.ShapeDtypeStruct(w.shape, F32)] * 3)(g, w, m, v)


def _pack_rows(arrays, width, row_align):
    pieces, spans, at = [], [], 0
    for arr in arrays:
        flat = arr.reshape(-1)
        rows = -(-flat.shape[0] // (width * row_align)) * row_align
        flat = jnp.pad(flat, (0, rows * width - flat.shape[0]))
        pieces.append(flat.reshape(rows, width))
        spans.append((at, rows))
        at += rows
    return jnp.concatenate(pieces, axis=0), spans


def _unpack(rows2d, span, shape):
    at, rows = span
    size = 1
    for s in shape:
        size *= s
    return rows2d[at:at + rows].reshape(-1)[:size].reshape(shape)


def _join_cols(blocks):
    n, rows, cols = blocks.shape
    return jnp.transpose(blocks, (1, 0, 2)).reshape(rows, n * cols)


def _split_cols(full):
    rows, cols = full.shape
    return jnp.transpose(full.reshape(rows, NDEV, cols // NDEV), (1, 0, 2))


def _ffn_up(h, gain, w_gu, after, *, tm, tag):
    lp, d = h.shape
    f8 = w_gu.shape[3]
    n = _norm_fwd(h, gain, tm=_pick(lp, [256, 128]), name=f"{tag}_norm")
    hid = _spec((None, tm, f8), lambda i, j, k: (j, i, 0))
    a, b, s = _mm((lp // tm, NDEV, 1), n, _spec((tm, d), lambda i, j, k: (i, 0)),
                  [(w_gu, _spec((None, None, d, f8), lambda i, j, k: (j, 0, 0, 0))),
                   (w_gu, _spec((None, None, d, f8), lambda i, j, k: (j, 1, 0, 0)))],
                  [], [((NDEV, lp, f8), BF16, hid)] * 3, dims=NN, epi=_swiglu_epi, after=after, name=f"{tag}_up")
    return n, a, b, s


def _ffn_down(h, s, w_down, *, tm, tag):
    lp, d = h.shape
    f8 = w_down.shape[1]
    tn = _pick(d, [1024, 512, 256, 128])
    tile = _spec((tm, tn), lambda i, j, k: (i, j))
    return _mm((lp // tm, d // tn, NDEV // KSUB), s, _spec((KSUB, tm, f8), lambda i, j, k: (k, i, 0)),
               [(w_down, _spec((KSUB, f8, tn), lambda i, j, k: (k, 0, j)))], [(h, tile)], [((lp, d), F32, tile)],
               dims=NN, acc_shape=(tm, tn), epi=lambda accs, ex: [ex[0] + FFN_RESIDUAL_WEIGHT * accs[0]], name=f"{tag}_down")[0]


def _ffn_bwd_dw(dh_half, saved, w_down, on_down, *, tm, tag):
    n, a, b, s = saved
    lp, d = dh_half.shape
    f8 = w_down.shape[1]
    tkr = _pick(lp, [4352, 2176, 1088, 544, 256, 128])
    tn = _pick(d, [1024, 512, 256, 128])
    hid = _spec((None, tm, f8), lambda i, j, k: (j, i, 0))
    dab, = _mm((lp // tm, NDEV, 1), dh_half, _spec((tm, d), lambda i, j, k: (i, 0)),
               [(w_down, _spec((None, f8, d), lambda i, j, k: (j, 0, 0)))], [(a, hid), (b, hid)],
               [((2, NDEV, lp, f8), BF16, _spec((2, None, tm, f8), lambda i, j, k: (0, j, i, 0)))],
               dims=NT, epi=_swiglu_bwd_epi, name=f"{tag}_down_dx")
    dw_down, = _mm((NDEV, d // tn, lp // tkr), s, _spec((None, tkr, f8), lambda i, j, k: (i, k, 0)),
                   [(dh_half, _spec((tkr, tn), lambda i, j, k: (k, j)))], [],
                   [((NDEV, f8, d), BF16, _spec((None, f8, tn), lambda i, j, k: (i, 0, j)))],
                   dims=TN, acc_shape=(f8, tn), name=f"{tag}_down_dw")
    dw_gu, = _mm((d // tn, 2 * NDEV, lp // tkr), n, _spec((tkr, tn), lambda i, j, k: (k, i)),
                 [(dab, _spec((None, None, tkr, f8), lambda i, j, k: (j // NDEV, j % NDEV, k, 0)))], [],
                 [((NDEV, 2, d, f8), BF16, _spec((None, None, tn, f8), lambda i, j, k: (j % NDEV, j // NDEV, i, 0)))],
                 dims=TN, acc_shape=(tn, f8), after=on_down(dw_down), name=f"{tag}_gate_up_dw")
    return dab, dw_gu, dw_down


def _ffn_bwd_dx(dab, w_gu, after, *, tm, tag):
    lp, f8 = dab.shape[2], dab.shape[3]
    d = w_gu.shape[2]
    tn = _pick(d, [1024, 512, 256, 128])
    nsub = NDEV // KSUB
    return _mm((lp // tm, d // tn, 2 * nsub), dab, _spec((None, KSUB, tm, f8), lambda i, j, k: (k // nsub, k % nsub, i, 0)),
               [(w_gu, _spec((KSUB, None, tn, f8), lambda i, j, k: (k % nsub, k // nsub, j, 0)))], [],
               [((lp, d), F32, _spec((tm, tn), lambda i, j, k: (i, j)))],
               dims=NT, acc_shape=(tm, tn), after=after, name=f"{tag}_gate_up_dx")[0]


def kernel(x, meta_tokens, ffn1_norm, ffn1_w_gate, ffn1_w_up, ffn1_w_down, mix_norm, w_in, b_forget, fox_q_norm, fox_k_norm, w_branch_fox, w_branch_sb, w_out, ffn2_norm, ffn2_w_gate, ffn2_w_up, ffn2_w_down, loss_target, m_meta_tokens, m_ffn1_norm, m_ffn1_w_gate, m_ffn1_w_up, m_ffn1_w_down, m_mix_norm, m_w_in, m_b_forget, m_fox_q_norm, m_fox_k_norm, m_w_branch_fox, m_w_branch_sb, m_w_out, m_ffn2_norm, m_ffn2_w_gate, m_ffn2_w_up, m_ffn2_w_down, v_meta_tokens, v_ffn1_norm, v_ffn1_w_gate, v_ffn1_w_up, v_ffn1_w_down, v_mix_norm, v_w_in, v_b_forget, v_fox_q_norm, v_fox_k_norm, v_w_branch_fox, v_w_branch_sb, v_w_out, v_ffn2_norm, v_ffn2_w_gate, v_ffn2_w_up, v_ffn2_w_down):
    seq, d = x.shape[1], x.shape[2]
    d8 = d // NDEV
    w = BRANCH_WIDTH
    tile = 256 if seq % 256 == 0 else 128
    pad = tile - N_META
    lp = tile + seq
    tm = _pick(lp, [1088, 544, 256, 128])
    tr = _pick(tile, [256, 128])
    tkr = _pick(lp, [4352, 2176, 1088, 544, 256, 128])
    nb = lp // tile
    me = _flat_id(*_my_place())

    shards = [jnp.stack([ffn1_w_gate[0], ffn1_w_up[0]]).astype(BF16), ffn1_w_down[0].astype(BF16), w_in[0].astype(BF16),
              w_branch_fox[0].astype(BF16), w_branch_sb[0].astype(BF16), w_out[0].astype(BF16),
              jnp.stack([ffn2_w_gate[0], ffn2_w_up[0]]).astype(BF16), ffn2_w_down[0].astype(BF16)]
    w_gu1, = _allgather(shards[:1], "gather_ffn1")
    down1_copies, token = _send_start(shards[1:2], [_landing(s, me) for s in shards[1:2]], "first", w_gu1, "gather_down1_start")
    w_in_copies, token = _send_start(shards[2:3], [_landing(s, me) for s in shards[2:3]], "first", token, "gather_w_in_start")
    mixer_copies, token = _send_start(shards[3:6], [_landing(s, me) for s in shards[3:6]], "gather", token, "gather_mixer_start")
    ffn2_copies, token = _send_start(shards[6:], [_landing(s, me) for s in shards[6:]], "gather", token, "gather_ffn2_start")
    meta_full = _join_cols(_allgather([meta_tokens], "gather_meta", in_vmem=True)[0])

    h0 = jnp.concatenate([jnp.zeros((pad, d), F32), meta_full.astype(F32), x[0]], axis=0)
    saved1 = _ffn_up(h0, ffn1_norm, w_gu1, token, tm=tm, tag="ffn1")
    w_down1, = _gather_two_level(down1_copies, saved1[3], "gather_down1")
    h1 = _ffn_down(h0, saved1[3], w_down1, tm=tm, tag="ffn1")

    w_in_blocks, = _gather_two_level(w_in_copies, h1, "gather_w_in")
    wi = _join_cols(w_in_blocks)
    w_pa = jnp.concatenate([wi[:, :2 * w], jnp.pad(wi[:, 3 * w:3 * w + HEADS], ((0, 0), (0, LANES - HEADS)))], axis=1)
    w_pb = jnp.concatenate([wi[:, 2 * w:3 * w], wi[:, 3 * w + HEADS:]], axis=1)
    na, nbw = w_pa.shape[1], w_pb.shape[1]
    gate_blk = 4 * w // d

    n2 = _norm_fwd(h1, mix_norm, tm=tr, name="mix_norm")
    tma = _pick(lp, [544, 256, 128])
    tnd = _pick(d, [1024, 512, 256, 128])
    tnb = _pick(nbw, [512, 256, 128])
    proj_a = _mm2d(n2, w_pa, na, dims=NN, tm=tma, tn=na, tk=d, out_dtype=F32, name="proj_a")
    proj_b = _mm2d(n2, w_pb, nbw, dims=NN, tm=tm, tn=tnb, tk=d, out_dtype=BF16, name="proj_b")
    b_pad = jnp.pad(b_forget, ((0, 0), (0, LANES - HEADS)))
    q_gain, k_gain = fox_q_norm.reshape(1, w), fox_k_norm.reshape(1, w)
    fq, fk, logf = _fox_prep(proj_a, q_gain, k_gain, b_pad, tm=tr, pad=pad, name="fox_prep")
    c = _cumsum_rows(logf, tile=tile, reverse=False, name="forget_cumsum")
    c_rows = jnp.transpose(c[:, :HEADS]).reshape(HEADS, nb, 1, tile)
    o_fox, lse = _fox_fwd(fq, fk, proj_b, 0, c_rows, tile=tile, pad=pad, name="fox_fwd")
    o_sb, sb_total = _sb_fwd(proj_b, HEADS, 2 * HEADS, 3 * HEADS, tile=tile, pad=pad, name="sb_fwd")
    w_br_fox, w_br_sb, w_out_blocks = _send_wait(mixer_copies, o_sb, "gather_mixer_wait")
    w_out_full = w_out_blocks.reshape(d, d)

    def branch(o, w_blocks, name):
        return _mm((lp // tm, NDEV, 1), o, _spec((tm, w), lambda i, j, k: (i, 0)),
                   [(w_blocks, _spec((None, w, d8), lambda i, j, k: (j, 0, 0)))], [],
                   [((lp, d), BF16, _spec((tm, d8), lambda i, j, k: (i, j)))], dims=NN, name=name)[0]

    br_fox = branch(o_fox, w_br_fox, "branch_fox")
    br_sb = branch(o_sb, w_br_sb, "branch_sb")

    def merge_fn(i, tiles, consts):
        bf_, bs_, gf_, gs_ = [t.astype(F32) for t in tiles]
        return [_sigmoid(gf_) * bf_ + _sigmoid(gs_) * bs_], []

    gates_in = [(proj_b, d, gate_blk), (proj_b, d, gate_blk + 1)]
    merged, = _rowwise(merge_fn, [_whole(br_fox), _whole(br_sb)] + gates_in, [], [(d, BF16)], [], tm=tr, name="merge")
    h2 = _mm2d(merged, w_out_full, d, dims=NN, tm=tm, tn=tnd, tk=d, out_dtype=F32,
               epi=lambda accs, ex: [ex[0] + accs[0]], extras=[h1], name="out_proj")

    w_gu2, w_down2 = _send_wait(ffn2_copies, h2, "gather_ffn2_wait")
    saved3 = _ffn_up(h2, ffn2_norm, w_gu2, None, tm=tm, tag="ffn2")
    h3 = _ffn_down(h2, saved3[3], w_down2, tm=tm, tag="ffn2")

    skip = tile // tr

    def loss_fn(i, tiles, consts):
        real = i >= skip
        err = jnp.where(real, tiles[0] - tiles[1], 0.0)
        dy = err * (1.0 / d)
        part = 0.5 * jnp.sum(err * dy, axis=0, keepdims=True)
        return [dy, FFN_RESIDUAL_WEIGHT * dy], [part]

    dh3, dh3_half, loss_cols = _rowwise(
        loss_fn, [_whole(h3), _whole(loss_target[0])], [], [(d, F32, lp), (d, BF16, lp)], [(1, d)], tm=tr, name="loss",
        row_maps=[None, lambda i: jnp.maximum(i - skip, 0)])

    def own(g):
        return lax.dynamic_index_in_dim(g, me, 0, keepdims=False)

    def send_grads(grads_, name):
        return _send_start(grads_, [_landing(own(g), me) for g in grads_], "exchange", grads_[-1], name)

    sends = {}

    def send_piece(key):
        def on_ready(*gs):
            sends[key], token_ = send_grads(list(gs), f"exchange_{key}_start")
            return token_
        return on_ready

    dab3, dw_gu2, _ = _ffn_bwd_dw(dh3_half, saved3, w_down2, send_piece("down2"), tm=tm, tag="ffn2")
    dn3 = _ffn_bwd_dx(dab3, w_gu2, send_piece("gu2")(dw_gu2), tm=tm, tag="ffn2")
    dh2, dh2_bf, dg_ffn2 = _norm_bwd(dn3, h2, ffn2_norm, dh3, 1.0, tm=tr, name="ffn2_norm_bwd")

    dmerged = _mm2d(dh2_bf, w_out_full, d, dims=NT, tm=tm, tn=tnd, tk=d, out_dtype=BF16, name="out_proj_dx")
    dw_out = _mm2d(merged, dh2_bf, d, dims=TN, tm=tnd, tn=tnd, tk=tkr, out_dtype=BF16, name="out_proj_dw")
    token = send_piece("out")(dw_out.reshape(NDEV, d8, d))

    def merge_bwd_fn(i, tiles, consts):
        dm, bf_, bs_, gf_, gs_ = [t.astype(F32) for t in tiles]
        sf, ss = _sigmoid(gf_), _sigmoid(gs_)
        return [dm * sf, dm * ss, dm * bf_ * sf * (1.0 - sf), dm * bs_ * ss * (1.0 - ss)], []

    dbr_fox, dbr_sb, dg_fox, dg_sb = _rowwise(
        merge_bwd_fn, [_whole(dmerged), _whole(br_fox), _whole(br_sb)] + gates_in, [], [(d, BF16)] * 4, [], tm=tr, after=token,
        name="merge_bwd")

    tnw = _pick(w, [512, 256, 128])

    def branch_dx(dbr, w_blocks, after, name):
        return _mm((lp // tm, w // tnw, NDEV), dbr, _spec((tm, d8), lambda i, j, k: (i, k)),
                   [(w_blocks, _spec((None, tnw, d8), lambda i, j, k: (k, j, 0)))], [],
                   [((lp, w), BF16, _spec((tm, tnw), lambda i, j, k: (i, j)))], dims=NT, acc_shape=(tm, tnw), after=after,
                   name=name)[0]

    def branch_dw(o, dbr, name):
        return _mm((w // tnw, NDEV, lp // tkr), o, _spec((tkr, tnw), lambda i, j, k: (k, i)),
                   [(dbr, _spec((tkr, d8), lambda i, j, k: (k, j)))], [],
                   [((NDEV, w, d8), BF16, _spec((None, tnw, d8), lambda i, j, k: (j, i, 0)))],
                   dims=TN, acc_shape=(tnw, d8), name=name)[0]

    token = send_piece("branch")(branch_dw(o_fox, dbr_fox, "branch_fox_dw"), branch_dw(o_sb, dbr_sb, "branch_sb_dw"))
    do_fox = branch_dx(dbr_fox, w_br_fox, token, "branch_fox_dx")
    do_sb = branch_dx(dbr_sb, w_br_sb, token, "branch_sb_dx")

    dfq, dfk, dfv, dc_rows = _fox_bwd(fq, fk, proj_b, 0, c_rows, o_fox, do_fox, lse, tile=tile, pad=pad, name="fox_bwd")
    dsq, dsk, dsv = _sb_bwd(proj_b, HEADS, 2 * HEADS, 3 * HEADS, do_sb, sb_total, tile=tile, pad=pad, name="sb_bwd")
    dc = jnp.pad(jnp.transpose(dc_rows.reshape(HEADS, lp)), ((0, 0), (0, LANES - HEADS)))
    dlogf = _cumsum_rows(dc, tile=tile, reverse=True, name="forget_cumsum_bwd")
    dproj_a, dg_q, dg_k, dg_b = _fox_prep_bwd(proj_a, dfq, dfk, dlogf, q_gain, k_gain, b_pad, tm=tr, pad=pad, name="fox_prep_bwd")
    dproj_b = jnp.concatenate([dfv, dsq, dsk, dsv, dg_fox, dg_sb], axis=1)

    dw_pa = _mm2d(n2, dproj_a, na, dims=TN, tm=tnd, tn=na, tk=_pick(lp, [544, 256, 128]), out_dtype=BF16, name="proj_a_dw")
    dw_pb = _mm2d(n2, dproj_b, nbw, dims=TN, tm=tnd, tn=tnb, tk=tkr, out_dtype=BF16, name="proj_b_dw")
    dw_in = jnp.concatenate([dw_pa[:, :2 * w], dw_pb[:, :w], dw_pa[:, 2 * w:2 * w + HEADS], dw_pb[:, w:]], axis=1)
    token = send_piece("w_in")(_split_cols(dw_in))
    dn2_a = _mm2d(dproj_a, w_pa, d, dims=NT, tm=tm, tn=tnd, tk=na, out_dtype=F32, after=token, name="proj_a_dx")
    dn2 = _mm2d(dproj_b, w_pb, d, dims=NT, tm=tm, tn=tnd, tk=_pick(nbw, [2048, 1024, 512, 256, 128]), out_dtype=F32,
                epi=lambda accs, ex: [accs[0] + ex[0]], extras=[dn2_a], name="proj_b_dx")
    dh1, dh1_half, dg_mix = _norm_bwd(dn2, h1, mix_norm, dh2, FFN_RESIDUAL_WEIGHT, tm=tr, name="mix_norm_bwd")

    dab1, dw_gu1, _ = _ffn_bwd_dw(dh1_half, saved1, w_down1, send_piece("down1"), tm=tm, tag="ffn1")
    dn1 = _ffn_bwd_dx(dab1, w_gu1, send_piece("gu1")(dw_gu1), tm=tm, tag="ffn1")
    dh0, _, dg_ffn1 = _norm_bwd(dn1, h0, ffn1_norm, dh1, None, tm=tr, name="ffn1_norm_bwd")
    grad_x = dh0[tile:][None]

    r_down2, = _send_wait(sends["down2"], dh0, "exchange_down2_wait")
    r_gu2, = _send_wait(sends["gu2"], r_down2, "exchange_gu2_wait")
    r_out, = _send_wait(sends["out"], r_gu2, "exchange_out_wait")
    r_br_fox, r_br_sb = _send_wait(sends["branch"], r_out, "exchange_branch_wait")
    r_in, = _send_wait(sends["w_in"], r_br_sb, "exchange_w_in_wait")
    grads, deltas, new_ms, new_vs = {}, {}, {}, {}

    def adamw_big(entries):
        for k, (parts, sel, wt, mt, vt) in entries.items():
            g, dl, mn, vn = _adamw_summed(parts, sel, wt[0], mt[0], vt[0], name=f"adamw_{k}")
            grads[k], deltas[k], new_ms[k], new_vs[k] = g[None], dl[None], mn[None], vn[None]

    adamw_big(dict(ffn2_w_gate=(r_gu2, 0, ffn2_w_gate, m_ffn2_w_gate, v_ffn2_w_gate),
                   ffn2_w_up=(r_gu2, 1, ffn2_w_up, m_ffn2_w_up, v_ffn2_w_up),
                   ffn2_w_down=(r_down2, 0, ffn2_w_down, m_ffn2_w_down, v_ffn2_w_down),
                   w_in=(r_in, 0, w_in, m_w_in, v_w_in),
                   w_branch_fox=(r_br_fox, 0, w_branch_fox, m_w_branch_fox, v_w_branch_fox),
                   w_branch_sb=(r_br_sb, 0, w_branch_sb, m_w_branch_sb, v_w_branch_sb),
                   w_out=(r_out, 0, w_out, m_w_out, v_w_out)))
    done = sum(v[0, 0, :1] for v in new_vs.values())
    r_down1, = _send_wait(sends["down1"], done, "exchange_down1_wait")
    r_gu1, = _send_wait(sends["gu1"], r_down1, "exchange_gu1_wait")
    adamw_big(dict(ffn1_w_gate=(r_gu1, 0, ffn1_w_gate, m_ffn1_w_gate, v_ffn1_w_gate),
                   ffn1_w_up=(r_gu1, 1, ffn1_w_up, m_ffn1_w_up, v_ffn1_w_up),
                   ffn1_w_down=(r_down1, 0, ffn1_w_down, m_ffn1_w_down, v_ffn1_w_down)))

    small_parts = [dh0[pad:tile], dg_ffn1, dg_mix, dg_ffn2, dg_b[:, :HEADS], dg_q, dg_k, loss_cols]
    small_packed, small_spans = _pack_rows(small_parts, LANES, SMALL_ROWS)
    small_sum = _allsum_small(small_packed, "sum_small")
    g_meta_full, g_ffn1n, g_mixn, g_ffn2n, g_bf, g_qn, g_kn, loss_vec = [
        _unpack(small_sum, span, part.shape) for span, part in zip(small_spans, small_parts)]
    loss = jnp.sum(loss_vec)
    g_meta = lax.dynamic_slice_in_dim(g_meta_full, me * d8, d8, axis=1)
    g_qn, g_kn = g_qn.reshape(fox_q_norm.shape), g_kn.reshape(fox_k_norm.shape)

    small = dict(meta_tokens=(g_meta, meta_tokens, m_meta_tokens, v_meta_tokens),
                 ffn1_norm=(g_ffn1n, ffn1_norm, m_ffn1_norm, v_ffn1_norm),
                 mix_norm=(g_mixn, mix_norm, m_mix_norm, v_mix_norm),
                 b_forget=(g_bf, b_forget, m_b_forget, v_b_forget),
                 fox_q_norm=(g_qn, fox_q_norm, m_fox_q_norm, v_fox_q_norm),
                 fox_k_norm=(g_kn, fox_k_norm, m_fox_k_norm, v_fox_k_norm),
                 ffn2_norm=(g_ffn2n, ffn2_norm, m_ffn2_norm, v_ffn2_norm))
    for k, (g, wt, mt, vt) in small.items():
        flat = lambda t: t.reshape(-1, t.shape[-1])
        dl, mn, vn = _adamw_plain(flat(g), flat(wt), flat(mt), flat(vt), name=f"adamw_{k}")
        grads[k], deltas[k], new_ms[k], new_vs[k] = g, dl.reshape(wt.shape), mn.reshape(wt.shape), vn.reshape(wt.shape)

    order = ["meta_tokens", "ffn1_norm", "ffn1_w_gate", "ffn1_w_up", "ffn1_w_down", "mix_norm", "w_in", "b_forget",
             "fox_q_norm", "fox_k_norm", "w_branch_fox", "w_branch_sb", "w_out", "ffn2_norm", "ffn2_w_gate",
             "ffn2_w_up", "ffn2_w_down"]
    return (loss, grad_x, *[grads[k] for k in order], *[deltas[k] for k in order],
            *[new_ms[k] for k in order], *[new_vs[k] for k in order])
```

```python
import jax
import jax.numpy as jnp
from jax import lax
from jax.experimental import pallas as pl
from jax.experimental.pallas import tpu as pltpu

F32 = jnp.float32
BF16 = jnp.bfloat16
MESH = pl.DeviceIdType.MESH

NDEV = 8
N_META = 16
HEAD_DIM = 128
HEADS = 8
BRANCH_WIDTH = HEADS * HEAD_DIM
RMS_EPS = 1e-6
FFN_RESIDUAL_WEIGHT = 0.5
ATTN_SCALE = HEAD_DIM ** -0.5
MASKED_LOGIT = -1e30

ADAM_LR = 0.001
ADAM_B1 = 0.9
ADAM_B2 = 0.999
ADAM_EPS = 1e-08
ADAM_WD = 0.01
ADAM_STEP = 10

LANES = 128
SMALL_ROWS = 8
ADAMW_TILE_ELEMS = 160 * 1024
KSUB = 4


def _pick(n, prefs):
    for p in prefs:
        if p <= n and n % p == 0:
            return p
    return n


def _dot(a, b, dims):
    return lax.dot_general(a, b, (dims, ((), ())), preferred_element_type=F32)


NN = ((1,), (0,))
TN = ((0,), (0,))
NT = ((1,), (1,))


def _split_bf16(x):
    hi = x.astype(BF16)
    lo = (x - hi.astype(F32)).astype(BF16)
    return hi, lo


def _sigmoid_parts(z):
    e = jnp.exp(-jnp.abs(z))
    t = 1.0 + e
    return e, 1.0 / t, jnp.log(t)


def _sigmoid(x):
    return 0.5 * jnp.tanh(0.5 * x) + 0.5


def _my_place():
    return lax.axis_index("x"), lax.axis_index("y"), lax.axis_index("c")


def _flat_id(px, py, pc):
    return 4 * px + 2 * py + pc


def _peer(x, y, c, k):
    px = 1 - x if k & 4 else x
    py = 1 - y if k & 2 else y
    pc = 1 - c if k & 1 else c
    return px, py, pc


def _allgather(shards, name, in_vmem=False):
    n = len(shards)

    def body(*refs):
        x_refs, out_refs = refs[:n], refs[n:2 * n]
        send_sems, recv_sems, local_sems = refs[2 * n:]
        x, y, c = _my_place()
        me, sibling = (x, y, c), (x, y, 1 - c)
        chips = [(1 - x, y), (x, 1 - y), (1 - x, 1 - y)]

        def block(a, place):
            return out_refs[a].at[_flat_id(*place)]

        def copy(a, k, place, to, src=None):
            return pltpu.make_async_remote_copy(
                src_ref=block(a, place) if src is None else src, dst_ref=block(a, place),
                send_sem=send_sems.at[7 * a + k], recv_sem=recv_sems.at[7 * a + k], device_id=to, device_id_type=MESH)

        mine = [pltpu.make_async_copy(x_refs[a], block(a, me), local_sems.at[a]) for a in range(n)]
        for cp in mine:
            cp.start()
        first = []
        for a in range(n):
            first.append(copy(a, 0, me, sibling, src=x_refs[a]))
            first += [copy(a, 1 + j, me, (*chip, c), src=x_refs[a]) for j, chip in enumerate(chips)]
        for cp in first:
            cp.start()
        passed = []
        for j, chip in enumerate(chips):
            for a in range(n):
                copy(a, 1 + j, (*chip, c), me).wait_recv()
                passed.append(copy(a, 4 + j, (*chip, c), sibling))
                passed[-1].start()
        for a in range(n):
            copy(a, 0, sibling, me).wait_recv()
        for j, chip in enumerate(chips):
            for a in range(n):
                copy(a, 4 + j, (*chip, 1 - c), me).wait_recv()
        for cp in first + passed:
            cp.wait_send()
        for cp in mine:
            cp.wait()

    space = pltpu.VMEM if in_vmem else pl.ANY
    return pl.pallas_call(
        body, name=name,
        out_shape=[jax.ShapeDtypeStruct((NDEV,) + s.shape, s.dtype) for s in shards],
        in_specs=[pl.BlockSpec(memory_space=space)] * n,
        out_specs=[pl.BlockSpec(memory_space=space)] * n,
        scratch_shapes=[pltpu.SemaphoreType.DMA((7 * n,)), pltpu.SemaphoreType.DMA((7 * n,)), pltpu.SemaphoreType.DMA((n,))],
    )(*shards)


def _allsum_small(part, name):
    rows, cols = part.shape

    def body(p_ref, out_ref, buf, send_sems, recv_sems):
        x, y, c = _my_place()
        me = _flat_id(x, y, c)
        buf[me] = p_ref[...]
        copies = []
        for k in range(1, NDEV):
            copies.append(pltpu.make_async_remote_copy(
                src_ref=p_ref, dst_ref=buf.at[me], send_sem=send_sems.at[k - 1], recv_sem=recv_sems.at[k - 1],
                device_id=_peer(x, y, c, k), device_id_type=MESH))
        for cp in copies:
            cp.start()
        for cp in copies:
            cp.wait()
        total = buf[0]
        for j in range(1, NDEV):
            total = total + buf[j]
        out_ref[...] = total

    return pl.pallas_call(
        body, name=name,
        out_shape=jax.ShapeDtypeStruct((rows, cols), F32),
        in_specs=[pl.BlockSpec(memory_space=pltpu.VMEM)],
        out_specs=pl.BlockSpec(memory_space=pltpu.VMEM),
        scratch_shapes=[pltpu.VMEM((NDEV, rows, cols), F32),
                        pltpu.SemaphoreType.DMA((7,)), pltpu.SemaphoreType.DMA((7,))],
    )(part)


_HBM = pl.BlockSpec(memory_space=pltpu.HBM)
_SEM = pl.BlockSpec(memory_space=pltpu.SEMAPHORE)
_DATAFLOW = pltpu.SideEffectType.DATAFLOW_SIDE_EFFECTING


COPIES_PER_ARRAY = {"gather": 7, "exchange": 7, "first": 4, "forward": 3}


def _send_copies(plan, src_refs, land_refs, send_sems, recv_sems):
    x, y, c = _my_place()
    me = _flat_id(x, y, c)
    per = COPIES_PER_ARRAY[plan]
    copies = []

    def add(a, slot, src, dst, to):
        copies.append(pltpu.make_async_remote_copy(
            src_ref=src, dst_ref=dst, send_sem=send_sems.at[per * a + slot], recv_sem=recv_sems.at[per * a + slot],
            device_id=to, device_id_type=MESH))

    for a, land_ref in enumerate(land_refs):
        if plan in ("gather", "exchange"):
            for k in range(1, NDEV):
                peer = _peer(x, y, c, k)
                add(a, k - 1, src_refs[a].at[_flat_id(*peer)] if plan == "exchange" else src_refs[a], land_ref.at[me], peer)
        elif plan == "first":
            for slot, k in enumerate((1, 2, 4, 6)):
                add(a, slot, src_refs[a], land_ref.at[me], _peer(x, y, c, k))
        else:
            for slot, k in enumerate((2, 4, 6)):
                block = land_ref.at[_flat_id(*_peer(x, y, c, k))]
                add(a, slot, block, block, (x, y, 1 - c))
    return copies


def _send_start(srcs, lands, plan, after, name):
    ns, nl = len(srcs), len(lands)
    nsem = COPIES_PER_ARRAY[plan] * nl

    def body(*refs):
        for cp in _send_copies(plan, refs[:ns], refs[ns:ns + nl], refs[ns + nl + 1], refs[ns + nl + 2]):
            cp.start()
        refs[-1][...] = jnp.zeros_like(refs[-1])

    operands = [pltpu.with_memory_space_constraint(t, pltpu.HBM) for t in list(srcs) + list(lands)]
    outs = pl.pallas_call(
        body, name=name,
        out_shape=(pltpu.SemaphoreType.DMA((nsem,)), pltpu.SemaphoreType.DMA((nsem,)),
                   *[pltpu.HBM(t.shape, t.dtype) for t in operands[ns:]], jax.ShapeDtypeStruct((SMALL_ROWS, LANES), F32)),
        in_specs=[_HBM] * (ns + nl) + [pl.BlockSpec(memory_space=pl.ANY)],
        out_specs=(_SEM, _SEM, *[_HBM] * nl, pl.BlockSpec(memory_space=pltpu.VMEM)),
        input_output_aliases={ns + i: 2 + i for i in range(nl)},
        compiler_params=pltpu.CompilerParams(has_side_effects=_DATAFLOW),
    )(*operands, after)
    return (plan, outs[0], outs[1], operands[:ns], list(outs[2:2 + nl])), outs[-1]


def _send_wait(handle, after, name):
    plan, send_sems, recv_sems, srcs, lands = handle
    ns, nl = len(srcs), len(lands)

    def body(*refs):
        for cp in _send_copies(plan, refs[:ns], refs[ns:ns + nl], refs[ns + nl], refs[ns + nl + 1]):
            cp.wait_send()
            cp.wait_recv()

    outs = pl.pallas_call(
        body, name=name,
        out_shape=tuple(pltpu.HBM(t.shape, t.dtype) for t in lands),
        in_specs=[_HBM] * (ns + nl) + [_SEM, _SEM, pl.BlockSpec(memory_space=pl.ANY)],
        out_specs=tuple([_HBM] * nl),
        input_output_aliases={ns + i: i for i in range(nl)},
        compiler_params=pltpu.CompilerParams(has_side_effects=_DATAFLOW),
    )(*srcs, *lands, send_sems, recv_sems, after)
    return list(outs)


def _gather_two_level(first_handle, after, name):
    lands = _send_wait(first_handle, after, f"{name}_wait")
    forward, _ = _send_start([], lands, "forward", after, f"{name}_forward_start")
    return _send_wait(forward, after, f"{name}_forward_wait")


def _landing(own_block, me):
    return lax.dynamic_update_index_in_dim(lax.empty((NDEV,) + own_block.shape, own_block.dtype), own_block, me, 0)


def _spec(block, index_map):
    return pl.BlockSpec(block, index_map)


def _mm(grid, a, a_spec, bs, extras, outs, *, dims, acc_shape=None, epi=None, after=None, name):
    nk = grid[2]
    nb, ne, no = len(bs), len(extras), len(outs)
    nafter = 0 if after is None else 1
    if epi is None:
        epi = lambda accs, ex: [accs[0]]

    def body(*refs):
        a_ref, b_refs = refs[0], refs[1:1 + nb]
        e_refs = refs[1 + nb:1 + nb + ne]
        o_refs = refs[1 + nb + ne + nafter:1 + nb + ne + nafter + no]
        acc_refs = refs[1 + nb + ne + nafter + no:]
        def finish(accs):
            for o_ref, tile in zip(o_refs, epi(accs, [e_ref[...] for e_ref in e_refs])):
                o_ref[...] = tile.astype(o_ref.dtype)

        def product(b_ref):
            if len(a_ref.shape) == 2:
                return _dot(a_ref[...], b_ref[...], dims)
            total = _dot(a_ref[0], b_ref[0], dims)
            for t in range(1, a_ref.shape[0]):
                total = total + _dot(a_ref[t], b_ref[t], dims)
            return total

        if nk == 1:
            finish([product(b_ref) for b_ref in b_refs])
        else:
            k = pl.program_id(2)

            @pl.when(k == 0)
            def _():
                for acc_ref in acc_refs:
                    acc_ref[...] = jnp.zeros_like(acc_ref)

            for acc_ref, b_ref in zip(acc_refs, b_refs):
                acc_ref[...] += product(b_ref)

            @pl.when(k == nk - 1)
            def _():
                finish([acc_ref[...] for acc_ref in acc_refs])

    return pl.pallas_call(
        body, name=name,
        grid=grid,
        in_specs=[a_spec] + [s for _, s in bs] + [s for _, s in extras] + [pl.BlockSpec(memory_space=pl.ANY)] * nafter,
        out_specs=[s for _, _, s in outs],
        out_shape=[jax.ShapeDtypeStruct(shape, dt) for shape, dt, _ in outs],
        scratch_shapes=[pltpu.VMEM(acc_shape, F32) for _ in bs] if nk > 1 else [],
        compiler_params=pltpu.CompilerParams(dimension_semantics=("parallel", "parallel", "arbitrary")),
    )(a, *[b for b, _ in bs], *[e for e, _ in extras], *([after] if nafter else []))


def _mm2d(a, b, n_cols, *, dims, tm, tn, tk, out_dtype, epi=None, extras=(), after=None, name):
    m_rows, k_len = (a.shape[1], a.shape[0]) if dims == TN else a.shape
    assert m_rows % tm == 0 and n_cols % tn == 0 and k_len % tk == 0, (name, a.shape, n_cols, tm, tn, tk)
    a_spec = _spec((tk, tm), lambda i, j, k: (k, i)) if dims == TN else _spec((tm, tk), lambda i, j, k: (i, k))
    b_spec = _spec((tn, tk), lambda i, j, k: (j, k)) if dims == NT else _spec((tk, tn), lambda i, j, k: (k, j))
    tile = _spec((tm, tn), lambda i, j, k: (i, j))
    return _mm((m_rows // tm, n_cols // tn, k_len // tk), a, a_spec, [(b, b_spec)], [(e, tile) for e in extras],
               [((m_rows, n_cols), out_dtype, tile)], dims=dims, acc_shape=(tm, tn), epi=epi, after=after, name=name)[0]


def _rowwise(fn, ins, consts, outs, sums, *, tm, name, row_maps=None, after=None):
    m_rows = outs[0][2] if len(outs[0]) == 3 else ins[0][0].shape[0]
    n = m_rows // tm
    ni, nc, no = len(ins), len(consts), len(outs)
    nafter = 0 if after is None else 1
    row_maps = row_maps or [None] * ni

    def body(*refs):
        i = pl.program_id(0)
        in_tiles = [r[...] for r in refs[:ni]]
        const_values = [r[...] for r in refs[ni:ni + nc]]
        o_refs = refs[ni + nc + nafter:ni + nc + nafter + no]
        s_refs = refs[ni + nc + nafter + no:]
        out_tiles, sum_terms = fn(i, in_tiles, const_values)
        for o_ref, tile in zip(o_refs, out_tiles):
            o_ref[...] = tile.astype(o_ref.dtype)
        if s_refs:
            @pl.when(i == 0)
            def _():
                for s_ref in s_refs:
                    s_ref[...] = jnp.zeros_like(s_ref)

            for s_ref, term in zip(s_refs, sum_terms):
                s_ref[...] += term

    def in_spec(width, col, rmap):
        if rmap is None:
            return pl.BlockSpec((tm, width), lambda i: (i, col))
        return pl.BlockSpec((tm, width), lambda i: (rmap(i), col))

    return pl.pallas_call(
        body, name=name,
        grid=(n,),
        in_specs=[in_spec(w, col, rmap) for (_, w, col), rmap in zip(ins, row_maps)]
        + [pl.BlockSpec(cst.shape, lambda i: (0, 0)) for cst in consts] + [pl.BlockSpec(memory_space=pl.ANY)] * nafter,
        out_specs=[pl.BlockSpec((tm, o[0]), lambda i: (i, 0)) for o in outs]
        + [pl.BlockSpec(s, lambda i: (0, 0)) for s in sums],
        out_shape=[jax.ShapeDtypeStruct((m_rows, o[0]), o[1]) for o in outs]
        + [jax.ShapeDtypeStruct(s, F32) for s in sums],
        compiler_params=pltpu.CompilerParams(dimension_semantics=("arbitrary",)),
    )(*[arr for arr, _, _ in ins], *consts, *([after] if nafter else []))


def _whole(arr):
    return (arr, arr.shape[1], 0)


def _rms(x, gain):
    r = lax.rsqrt(jnp.mean(x * x, axis=-1, keepdims=True) + RMS_EPS)
    return x * r * gain


def _rms_bwd(x, gain, dy):
    r = lax.rsqrt(jnp.mean(x * x, axis=-1, keepdims=True) + RMS_EPS)
    u = dy * gain
    dx = r * u - x * (r * r * r) * jnp.mean(x * u, axis=-1, keepdims=True)
    return dx, dy * x * r


def _norm_fwd(h, gain, *, tm, name):
    def fn(i, tiles, consts):
        return [_rms(tiles[0], consts[0])], []
    return _rowwise(fn, [_whole(h)], [gain], [(h.shape[1], BF16)], [], tm=tm, name=name)[0]


def _norm_bwd(dn, h, gain, dh_in, low_scale, *, tm, name, after=None):
    d = h.shape[1]

    def fn(i, tiles, consts):
        dx, dg_rows = _rms_bwd(tiles[1], consts[0], tiles[0])
        dh = tiles[2] + dx
        return [dh] + ([] if low_scale is None else [low_scale * dh]), [jnp.sum(dg_rows, axis=0, keepdims=True)]

    outs = _rowwise(fn, [_whole(dn), _whole(h), _whole(dh_in)], [gain],
                    [(d, F32)] + ([] if low_scale is None else [(d, BF16)]), [(1, d)], tm=tm, name=name, after=after)
    return (outs[0], None, outs[1]) if low_scale is None else tuple(outs)


def _swiglu_epi(accs, ex):
    a, b = accs
    return [a, b, a * _sigmoid(a) * b]


def _swiglu_bwd_epi(accs, ex):
    ds = accs[0]
    a, b = ex[0].astype(F32), ex[1].astype(F32)
    sig = _sigmoid(a)
    silu = a * sig
    dsilu = sig * (1.0 + a * (1.0 - sig))
    return [jnp.stack([(ds * b * dsilu).astype(BF16), (ds * silu).astype(BF16)], axis=0)]


def _attn_mask(i, j, tile, pad, strict):
    row = i * tile + lax.broadcasted_iota(jnp.int32, (tile, tile), 0)
    col = j * tile + lax.broadcasted_iota(jnp.int32, (tile, tile), 1)
    causal = (col < row) if strict else (col <= row)
    return causal & ((col >= pad) | (row < pad))


FWD_GROUP = 4
BWD_GROUP = 2


def _walk_key_tiles(i, step, carry, widths, descending=False):
    diagonal = lambda j, c: step(j, c, True, 1)
    left = jnp.maximum(i - 1, 0)
    lo = 1
    if descending:
        carry = lax.fori_loop(jnp.maximum(i, 1), i + 1, diagonal, carry)
    else:
        carry = step(0, carry, True, 1)
    for nt in widths:
        count = left // nt
        if descending:
            top = lo + left
            carry = lax.fori_loop(0, count, lambda t, c, nt=nt, top=top: step(top - nt * (t + 1), c, False, nt), carry)
        else:
            carry = lax.fori_loop(0, count, lambda t, c, nt=nt, lo=lo: step(lo + nt * t, c, False, nt), carry)
            lo = lo + nt * count
        left = left - nt * count
    if descending:
        return step(0, carry, True, 1)
    return lax.fori_loop(jnp.maximum(i, 1), i + 1, diagonal, carry)


def _head_cols(g):
    return pl.ds(g * HEAD_DIM, HEAD_DIM)


def _key_rows(j, tile, nt):
    return pl.ds(pl.multiple_of(j * tile, tile), nt * tile)


def _key_tile(ref, j, tile, g, nt=1):
    return ref[_key_rows(j, tile, nt), _head_cols(g)]


def _key_bias(c_ref, g, j, nt):
    return c_ref[g, j] if nt == 1 else jnp.concatenate([c_ref[g, j + t] for t in range(nt)], axis=1)


def _head_specs(lp, tile, q_off, k_off, v_off, hp):
    gw = hp * HEAD_DIM
    q_spec = pl.BlockSpec((tile, gw), lambda h, i: (i, h + q_off // hp))
    k_spec = pl.BlockSpec((lp, gw), lambda h, i: (0, h + k_off // hp))
    v_spec = pl.BlockSpec((lp, gw), lambda h, i: (0, h + v_off // hp))
    return q_spec, k_spec, v_spec


def _fox_fwd(q, k, v, v_off, c_rows, *, tile, pad, name):
    lp = q.shape[0]
    nb = lp // tile
    hp = FWD_GROUP
    gw = hp * HEAD_DIM

    def body(q_ref, k_ref, v_ref, c_ref, o_ref, lse_ref):
        i = pl.program_id(1)
        qts = [q_ref[:, _head_cols(g)] for g in range(hp)]

        def step(j, carry, masked, nt):
            ok = _attn_mask(i, j, tile, pad, False) if masked else None
            out = []
            for g, (m, l, acc) in enumerate(carry):
                s = _dot(qts[g], _key_tile(k_ref, j, tile, g, nt), NT) * ATTN_SCALE - _key_bias(c_ref, g, j, nt)
                if masked:
                    s = jnp.where(ok, s, MASKED_LOGIT)
                m_new = jnp.maximum(m, jnp.max(s, axis=1, keepdims=True))
                p = jnp.exp(s - m_new)
                alpha = jnp.exp(m - m_new)
                l = alpha * l + jnp.sum(p, axis=1, keepdims=True)
                acc = alpha * acc + _dot(p.astype(BF16), _key_tile(v_ref, j, tile, g, nt), NN)
                out.append((m_new, l, acc))
            return tuple(out)

        init = (jnp.full((tile, 1), MASKED_LOGIT, F32), jnp.zeros((tile, 1), F32), jnp.zeros((tile, HEAD_DIM), F32))
        final = _walk_key_tiles(i, step, (init,) * hp, (4, 2, 1))
        for g, (m, l, acc) in enumerate(final):
            o_ref[:, _head_cols(g)] = (acc / l).astype(o_ref.dtype)
            lse_ref[g] = jnp.broadcast_to(m + jnp.log(l), (tile, LANES))

    q_spec, k_spec, v_spec = _head_specs(lp, tile, 0, 0, v_off, hp)
    return pl.pallas_call(
        body, name=name,
        grid=(HEADS // hp, nb),
        in_specs=[q_spec, k_spec, v_spec, pl.BlockSpec((hp, nb, 1, tile), lambda h, i: (h, 0, 0, 0))],
        out_specs=[pl.BlockSpec((tile, gw), lambda h, i: (i, h)),
                   pl.BlockSpec((hp, tile, LANES), lambda h, i: (h, i, 0))],
        out_shape=[jax.ShapeDtypeStruct((lp, BRANCH_WIDTH), BF16), jax.ShapeDtypeStruct((HEADS, lp, LANES), F32)],
        compiler_params=pltpu.CompilerParams(dimension_semantics=("parallel", "arbitrary")),
    )(q, k, v, c_rows)


def _fox_bwd(q, k, v, v_off, c_rows, o, do, lse, *, tile, pad, name):
    lp = q.shape[0]
    nb = lp // tile
    hp = BWD_GROUP
    gw = hp * HEAD_DIM

    def body(q_ref, k_ref, v_ref, c_ref, o_ref, do_ref, lse_ref, dq_ref, dk_ref, dv_ref, dc_ref, dk_acc, dv_acc, dc_acc):
        i = pl.program_id(1)

        @pl.when(i == 0)
        def _():
            dk_acc[...] = jnp.zeros_like(dk_acc)
            dv_acc[...] = jnp.zeros_like(dv_acc)
            dc_acc[...] = jnp.zeros_like(dc_acc)

        heads = range(hp)
        qts = [q_ref[:, _head_cols(g)] for g in heads]
        dots = [do_ref[:, _head_cols(g)] for g in heads]
        deltas = [jnp.sum(dots[g].astype(F32) * o_ref[:, _head_cols(g)].astype(F32), axis=1, keepdims=True) for g in heads]
        lse_cols = [lse_ref[g][:, :1] for g in heads]

        def step(j, dqs, masked, nt):
            rows = _key_rows(j, tile, nt)
            ok = _attn_mask(i, j, tile, pad, False) if masked else None
            out = []
            for g in heads:
                kt = _key_tile(k_ref, j, tile, g, nt)
                s = _dot(qts[g], kt, NT) * ATTN_SCALE - _key_bias(c_ref, g, j, nt)
                p = jnp.exp(s - lse_cols[g])
                if masked:
                    p = jnp.where(ok, p, 0.0)
                dp = _dot(dots[g], _key_tile(v_ref, j, tile, g, nt), NT)
                ds = p * (dp - deltas[g])
                dsb = ds.astype(BF16)
                dk_acc[rows, _head_cols(g)] += _dot(dsb, qts[g], TN)
                dv_acc[rows, _head_cols(g)] += _dot(p.astype(BF16), dots[g], TN)
                dc = -jnp.sum(ds, axis=0, keepdims=True)
                for t in range(nt):
                    dc_acc[g, j + t] += dc[:, t * tile:(t + 1) * tile]
                out.append(dqs[g] + _dot(dsb, kt, NN))
            return tuple(out)

        dqs = _walk_key_tiles(i, step, (jnp.zeros((tile, HEAD_DIM), F32),) * hp, (8, 4, 2, 1))
        for g in heads:
            dq_ref[:, _head_cols(g)] = dqs[g] * ATTN_SCALE

        @pl.when(i == nb - 1)
        def _():
            dk_ref[...] = dk_acc[...] * ATTN_SCALE
            dv_ref[...] = dv_acc[...].astype(dv_ref.dtype)
            dc_ref[...] = dc_acc[...]

    q_spec, k_spec, v_spec = _head_specs(lp, tile, 0, 0, v_off, hp)
    tile_spec = pl.BlockSpec((tile, gw), lambda h, i: (i, h))
    head_spec = pl.BlockSpec((lp, gw), lambda h, i: (0, h))
    c_spec = pl.BlockSpec((hp, nb, 1, tile), lambda h, i: (h, 0, 0, 0))
    return pl.pallas_call(
        body, name=name,
        grid=(HEADS // hp, nb),
        in_specs=[q_spec, k_spec, v_spec, c_spec, tile_spec, tile_spec,
                  pl.BlockSpec((hp, tile, LANES), lambda h, i: (h, i, 0))],
        out_specs=[tile_spec, head_spec, head_spec, c_spec],
        out_shape=[jax.ShapeDtypeStruct((lp, BRANCH_WIDTH), F32), jax.ShapeDtypeStruct((lp, BRANCH_WIDTH), F32),
                   jax.ShapeDtypeStruct((lp, BRANCH_WIDTH), BF16), jax.ShapeDtypeStruct((HEADS, nb, 1, tile), F32)],
        scratch_shapes=[pltpu.VMEM((lp, gw), F32), pltpu.VMEM((lp, gw), F32),
                        pltpu.VMEM((hp, nb, 1, tile), F32)],
        compiler_params=pltpu.CompilerParams(dimension_semantics=("parallel", "arbitrary")),
    )(q, k, v, c_rows, o, do, lse)


def _later_matrix(tile):
    return (lax.broadcasted_iota(jnp.int32, (tile, tile), 0) > lax.broadcasted_iota(jnp.int32, (tile, tile), 1)).astype(BF16)


def _earlier_matrix(tile):
    return (lax.broadcasted_iota(jnp.int32, (tile, tile), 0) < lax.broadcasted_iota(jnp.int32, (tile, tile), 1)).astype(BF16)


def _running_sums(x, tri, suffix):
    tile = tri.shape[0]
    blocks = [x[:, b:b + tile] for b in range(0, x.shape[1], tile)]
    sums = [jnp.sum(blk, axis=1, keepdims=True) for blk in blocks]
    out = []
    for b, blk in enumerate(blocks):
        hi, lo = _split_bf16(blk)
        inside = _dot(hi, tri, NN) + _dot(lo, tri, NN)
        for other in (sums[b + 1:] if suffix else sums[:b]):
            inside = inside + other
        out.append(inside)
    total = sums[0]
    for other in sums[1:]:
        total = total + other
    return (out[0] if len(out) == 1 else jnp.concatenate(out, axis=1)), total


def _sb_tile(qt, kt, ok, later):
    z = _dot(qt, kt, NT) * ATTN_SCALE
    e, r, lg = _sigmoid_parts(z)
    sp = jnp.maximum(z, 0.0) + lg
    spm = sp if ok is None else jnp.where(ok, sp, 0.0)
    within, sp_here = _running_sums(spm, later, True)
    return z, e, r, sp, sp_here, within


def _sb_fwd(qkv, q_off, k_off, v_off, *, tile, pad, name):
    lp = qkv.shape[0]
    nb = lp // tile
    hp = FWD_GROUP
    gw = hp * HEAD_DIM

    def body(q_ref, k_ref, v_ref, o_ref, tot_ref):
        i = pl.program_id(1)
        qts = [q_ref[:, _head_cols(g)] for g in range(hp)]
        later = _later_matrix(tile)

        def step(j, carry, masked, nt):
            ok = _attn_mask(i, j, tile, pad, True) if masked else None
            out = []
            for g, (right, acc) in enumerate(carry):
                z, _, _, sp, sp_here, within = _sb_tile(qts[g], _key_tile(k_ref, j, tile, g, nt), ok, later)
                w = jnp.exp(z - sp - within - right)
                if masked:
                    w = jnp.where(ok, w, 0.0)
                acc = acc + _dot(w.astype(BF16), _key_tile(v_ref, j, tile, g, nt), NN)
                out.append((right + sp_here, acc))
            return tuple(out)

        init = (jnp.zeros((tile, 1), F32), jnp.zeros((tile, HEAD_DIM), F32))
        final = _walk_key_tiles(i, step, (init,) * hp, (4, 2, 1), descending=True)
        for g, (total, acc) in enumerate(final):
            o_ref[:, _head_cols(g)] = acc.astype(o_ref.dtype)
            tot_ref[g] = jnp.broadcast_to(total, (tile, LANES))

    q_spec, k_spec, v_spec = _head_specs(lp, tile, q_off, k_off, v_off, hp)
    return pl.pallas_call(
        body, name=name,
        grid=(HEADS // hp, nb),
        in_specs=[q_spec, k_spec, v_spec],
        out_specs=[pl.BlockSpec((tile, gw), lambda h, i: (i, h)),
                   pl.BlockSpec((hp, tile, LANES), lambda h, i: (h, i, 0))],
        out_shape=[jax.ShapeDtypeStruct((lp, BRANCH_WIDTH), BF16), jax.ShapeDtypeStruct((HEADS, lp, LANES), F32)],
        compiler_params=pltpu.CompilerParams(dimension_semantics=("parallel", "arbitrary")),
    )(qkv, qkv, qkv)


def _sb_bwd(qkv, q_off, k_off, v_off, do, total, *, tile, pad, name):
    lp = qkv.shape[0]
    nb = lp // tile
    hp = BWD_GROUP
    gw = hp * HEAD_DIM

    def body(q_ref, k_ref, v_ref, do_ref, tot_ref, dq_ref, dk_ref, dv_ref, dk_acc, dv_acc):
        i = pl.program_id(1)

        @pl.when(i == 0)
        def _():
            dk_acc[...] = jnp.zeros_like(dk_acc)
            dv_acc[...] = jnp.zeros_like(dv_acc)

        heads = range(hp)
        qts = [q_ref[:, _head_cols(g)] for g in heads]
        dots = [do_ref[:, _head_cols(g)] for g in heads]
        total_cols = [tot_ref[g][:, :1] for g in heads]
        later, earlier = _later_matrix(tile), _earlier_matrix(tile)

        def step(j, carry, masked, nt):
            rows = _key_rows(j, tile, nt)
            ok = _attn_mask(i, j, tile, pad, True) if masked else None
            out = []
            for g, (dq, sp_before, dlw_before) in enumerate(carry):
                kt = _key_tile(k_ref, j, tile, g, nt)
                z, e, r, sp, sp_here, within = _sb_tile(qts[g], kt, ok, later)
                right = total_cols[g] - sp_before - sp_here
                w = jnp.exp(z - sp - within - right)
                if masked:
                    w = jnp.where(ok, w, 0.0)
                dlw = w * _dot(dots[g], _key_tile(v_ref, j, tile, g, nt), NT)
                before, dlw_here = _running_sums(dlw, earlier, False)
                sig = jnp.where(z >= 0, r, e * r)
                dz = dlw * (1.0 - sig) - sig * (before + dlw_before)
                if masked:
                    dz = jnp.where(ok, dz, 0.0)
                dzb = dz.astype(BF16)
                dk_acc[rows, _head_cols(g)] += _dot(dzb, qts[g], TN)
                dv_acc[rows, _head_cols(g)] += _dot(w.astype(BF16), dots[g], TN)
                out.append((dq + _dot(dzb, kt, NN), sp_before + sp_here, dlw_before + dlw_here))
            return tuple(out)

        zero_col = jnp.zeros((tile, 1), F32)
        final = _walk_key_tiles(i, step, ((jnp.zeros((tile, HEAD_DIM), F32), zero_col, zero_col),) * hp, (4, 2, 1))
        for g in heads:
            dq_ref[:, _head_cols(g)] = (final[g][0] * ATTN_SCALE).astype(dq_ref.dtype)

        @pl.when(i == nb - 1)
        def _():
            dk_ref[...] = (dk_acc[...] * ATTN_SCALE).astype(dk_ref.dtype)
            dv_ref[...] = dv_acc[...].astype(dv_ref.dtype)

    q_spec, k_spec, v_spec = _head_specs(lp, tile, q_off, k_off, v_off, hp)
    tile_spec = pl.BlockSpec((tile, gw), lambda h, i: (i, h))
    head_spec = pl.BlockSpec((lp, gw), lambda h, i: (0, h))
    return pl.pallas_call(
        body, name=name,
        grid=(HEADS // hp, nb),
        in_specs=[q_spec, k_spec, v_spec, tile_spec, pl.BlockSpec((hp, tile, LANES), lambda h, i: (h, i, 0))],
        out_specs=[tile_spec, head_spec, head_spec],
        out_shape=[jax.ShapeDtypeStruct((lp, BRANCH_WIDTH), BF16)] * 3,
        scratch_shapes=[pltpu.VMEM((lp, gw), F32), pltpu.VMEM((lp, gw), F32)],
        compiler_params=pltpu.CompilerParams(dimension_semantics=("parallel", "arbitrary")),
    )(qkv, qkv, qkv, do, total)


def _cumsum_rows(x, *, tile, reverse, name):
    lp = x.shape[0]
    nb = lp // tile

    def body(x_ref, o_ref, carry):
        @pl.when(pl.program_id(0) == 0)
        def _():
            carry[...] = jnp.zeros_like(carry)

        r = lax.broadcasted_iota(jnp.int32, (tile, tile), 0)
        c = lax.broadcasted_iota(jnp.int32, (tile, tile), 1)
        tri = ((c >= r) if reverse else (c <= r)).astype(BF16)
        hi, lo = _split_bf16(x_ref[...])
        run = _dot(tri, hi, NN) + _dot(tri, lo, NN) + carry[...]
        o_ref[...] = run
        carry[...] = run[:1, :] if reverse else run[tile - 1:, :]

    order = (lambda i: (nb - 1 - i, 0)) if reverse else (lambda i: (i, 0))
    return pl.pallas_call(
        body, name=name,
        grid=(nb,),
        in_specs=[pl.BlockSpec((tile, LANES), order)],
        out_specs=pl.BlockSpec((tile, LANES), order),
        out_shape=jax.ShapeDtypeStruct((lp, LANES), F32),
        scratch_shapes=[pltpu.VMEM((1, LANES), F32)],
        compiler_params=pltpu.CompilerParams(dimension_semantics=("arbitrary",)),
    )(x)


def _log_sigmoid(x):
    return jnp.minimum(x, 0.0) - jnp.log(1.0 + jnp.exp(-jnp.abs(x)))


def _forget_mask(i, tm, pad):
    row = i * tm + lax.broadcasted_iota(jnp.int32, (tm, LANES), 0)
    lane = lax.broadcasted_iota(jnp.int32, (tm, LANES), 1)
    return (row >= pad) & (lane < HEADS)


def _fox_prep(proj_a, q_gain, k_gain, b_forget, *, tm, pad, name):
    w = BRANCH_WIDTH

    def fn(i, tiles, consts):
        pa = tiles[0]
        qs, ks = [], []
        for h in range(HEADS):
            lo = h * HEAD_DIM
            qs.append(_rms(pa[:, lo:lo + HEAD_DIM], consts[0][:, lo:lo + HEAD_DIM]))
            ks.append(_rms(pa[:, w + lo:w + lo + HEAD_DIM], consts[1][:, lo:lo + HEAD_DIM]))
        logf = jnp.where(_forget_mask(i, tm, pad), _log_sigmoid(pa[:, 2 * w:] + consts[2]), 0.0)
        return [jnp.concatenate(qs, axis=1), jnp.concatenate(ks, axis=1), logf], []

    return _rowwise(fn, [_whole(proj_a)], [q_gain, k_gain, b_forget],
                    [(w, BF16), (w, BF16), (LANES, F32)], [], tm=tm, name=name)


def _fox_prep_bwd(proj_a, dq, dk, dlogf, q_gain, k_gain, b_forget, *, tm, pad, name):
    w = BRANCH_WIDTH

    def fn(i, tiles, consts):
        pa, dqt, dkt, dlf = tiles
        dxs_q, dxs_k, dgs_q, dgs_k = [], [], [], []
        for h in range(HEADS):
            lo = h * HEAD_DIM
            dx, dg = _rms_bwd(pa[:, lo:lo + HEAD_DIM], consts[0][:, lo:lo + HEAD_DIM], dqt[:, lo:lo + HEAD_DIM])
            dxs_q.append(dx)
            dgs_q.append(jnp.sum(dg, axis=0, keepdims=True))
            dx, dg = _rms_bwd(pa[:, w + lo:w + lo + HEAD_DIM], consts[1][:, lo:lo + HEAD_DIM], dkt[:, lo:lo + HEAD_DIM])
            dxs_k.append(dx)
            dgs_k.append(jnp.sum(dg, axis=0, keepdims=True))
        xf = pa[:, 2 * w:] + consts[2]
        e, r, _ = _sigmoid_parts(xf)
        df = jnp.where(_forget_mask(i, tm, pad), dlf * jnp.where(xf >= 0, e * r, r), 0.0)
        return ([jnp.concatenate(dxs_q + dxs_k + [df], axis=1)],
                [jnp.concatenate(dgs_q, axis=1), jnp.concatenate(dgs_k, axis=1), jnp.sum(df, axis=0, keepdims=True)])

    return _rowwise(fn, [_whole(proj_a), _whole(dq), _whole(dk), _whole(dlogf)], [q_gain, k_gain, b_forget],
                    [(2 * w + LANES, BF16)], [(1, w), (1, w), (1, LANES)], tm=tm, name=name)


def _adamw_math(w, g, m, v):
    m = ADAM_B1 * m + (1.0 - ADAM_B1) * g
    v = ADAM_B2 * v + (1.0 - ADAM_B2) * (g * g)
    m_hat = m / (1.0 - ADAM_B1 ** ADAM_STEP)
    v_hat = v / (1.0 - ADAM_B2 ** ADAM_STEP)
    delta = -ADAM_LR * (m_hat / (jnp.sqrt(v_hat) + ADAM_EPS) + ADAM_WD * w)
    return delta, m, v


def _adamw_summed(parts, sel, w, m, v, *, name):
    rows, cols = w.shape
    tr = _pick(rows, [t for t in (512, 256, 128, 64, 32, 16, 8) if t * cols <= ADAMW_TILE_ELEMS])

    def body(p_ref, w_ref, m_ref, v_ref, g_out, d_out, m_out, v_out):
        g = p_ref[0].astype(F32)
        for j in range(1, NDEV):
            g = g + p_ref[j].astype(F32)
        delta, m_new, v_new = _adamw_math(w_ref[...], g, m_ref[...], v_ref[...])
        g_out[...] = g
        d_out[...] = delta
        m_out[...] = m_new
        v_out[...] = v_new

    spec = pl.BlockSpec((tr, cols), lambda i: (i, 0))
    if parts.ndim == 3:
        p_spec = pl.BlockSpec((NDEV, tr, cols), lambda i: (0, i, 0))
    else:
        p_spec = pl.BlockSpec((NDEV, None, tr, cols), lambda i: (0, sel, i, 0))
    return pl.pallas_call(
        body, name=name,
        grid=(rows // tr,),
        in_specs=[p_spec, spec, spec, spec],
        out_specs=[spec] * 4,
        out_shape=[jax.ShapeDtypeStruct((rows, cols), F32)] * 4,
        compiler_params=pltpu.CompilerParams(dimension_semantics=("parallel",)),
    )(parts, w, m, v)


def _adamw_plain(g, w, m, v, *, name):
    def body(g_ref, w_ref, m_ref, v_ref, d_out, m_out, v_out):
        delta, m_new, v_new = _adamw_math(w_ref[...], g_ref[...], m_ref[...], v_ref[...])
        d_out[...] = delta
        m_out[...] = m_new
        v_out[...] = v_new

    return pl.pallas_call(body, name=name, out_shape=[jax.ShapeDtypeStruct(w.shape, F32)] * 3)(g, w, m, v)


def _pack_rows(arrays, width, row_align):
    pieces, spans, at = [], [], 0
    for arr in arrays:
        flat = arr.reshape(-1)
        rows = -(-flat.shape[0] // (width * row_align)) * row_align
        flat = jnp.pad(flat, (0, rows * width - flat.shape[0]))
        pieces.append(flat.reshape(rows, width))
        spans.append((at, rows))
        at += rows
    return jnp.concatenate(pieces, axis=0), spans


def _unpack(rows2d, span, shape):
    at, rows = span
    size = 1
    for s in shape:
        size *= s
    return rows2d[at:at + rows].reshape(-1)[:size].reshape(shape)


def _join_cols(blocks):
    n, rows, cols = blocks.shape
    return jnp.transpose(blocks, (1, 0, 2)).reshape(rows, n * cols)


def _split_cols(full):
    rows, cols = full.shape
    return jnp.transpose(full.reshape(rows, NDEV, cols // NDEV), (1, 0, 2))


def _ffn_up(h, gain, w_gu, after, *, tm, tag):
    lp, d = h.shape
    f8 = w_gu.shape[3]
    n = _norm_fwd(h, gain, tm=_pick(lp, [256, 128]), name=f"{tag}_norm")
    hid = _spec((None, tm, f8), lambda i, j, k: (j, i, 0))
    a, b, s = _mm((lp // tm, NDEV, 1), n, _spec((tm, d), lambda i, j, k: (i, 0)),
                  [(w_gu, _spec((None, None, d, f8), lambda i, j, k: (j, 0, 0, 0))),
                   (w_gu, _spec((None, None, d, f8), lambda i, j, k: (j, 1, 0, 0)))],
                  [], [((NDEV, lp, f8), BF16, hid)] * 3, dims=NN, epi=_swiglu_epi, after=after, name=f"{tag}_up")
    return n, a, b, s


def _ffn_down(h, s, w_down, *, tm, tag):
    lp, d = h.shape
    f8 = w_down.shape[1]
    tn = _pick(d, [1024, 512, 256, 128])
    tile = _spec((tm, tn), lambda i, j, k: (i, j))
    return _mm((lp // tm, d // tn, NDEV // KSUB), s, _spec((KSUB, tm, f8), lambda i, j, k: (k, i, 0)),
               [(w_down, _spec((KSUB, f8, tn), lambda i, j, k: (k, 0, j)))], [(h, tile)], [((lp, d), F32, tile)],
               dims=NN, acc_shape=(tm, tn), epi=lambda accs, ex: [ex[0] + FFN_RESIDUAL_WEIGHT * accs[0]], name=f"{tag}_down")[0]


def _ffn_bwd_dw(dh_half, saved, w_down, on_down, *, tm, tag):
    n, a, b, s = saved
    lp, d = dh_half.shape
    f8 = w_down.shape[1]
    tkr = _pick(lp, [4352, 2176, 1088, 544, 256, 128])
    tn = _pick(d, [1024, 512, 256, 128])
    hid = _spec((None, tm, f8), lambda i, j, k: (j, i, 0))
    dab, = _mm((lp // tm, NDEV, 1), dh_half, _spec((tm, d), lambda i, j, k: (i, 0)),
               [(w_down, _spec((None, f8, d), lambda i, j, k: (j, 0, 0)))], [(a, hid), (b, hid)],
               [((2, NDEV, lp, f8), BF16, _spec((2, None, tm, f8), lambda i, j, k: (0, j, i, 0)))],
               dims=NT, epi=_swiglu_bwd_epi, name=f"{tag}_down_dx")
    dw_down, = _mm((NDEV, d // tn, lp // tkr), s, _spec((None, tkr, f8), lambda i, j, k: (i, k, 0)),
                   [(dh_half, _spec((tkr, tn), lambda i, j, k: (k, j)))], [],
                   [((NDEV, f8, d), BF16, _spec((None, f8, tn), lambda i, j, k: (i, 0, j)))],
                   dims=TN, acc_shape=(f8, tn), name=f"{tag}_down_dw")
    dw_gu, = _mm((d // tn, 2 * NDEV, lp // tkr), n, _spec((tkr, tn), lambda i, j, k: (k, i)),
                 [(dab, _spec((None, None, tkr, f8), lambda i, j, k: (j // NDEV, j % NDEV, k, 0)))], [],
                 [((NDEV, 2, d, f8), BF16, _spec((None, None, tn, f8), lambda i, j, k: (j % NDEV, j // NDEV, i, 0)))],
                 dims=TN, acc_shape=(tn, f8), after=on_down(dw_down), name=f"{tag}_gate_up_dw")
    return dab, dw_gu, dw_down


def _ffn_bwd_dx(dab, w_gu, after, *, tm, tag):
    lp, f8 = dab.shape[2], dab.shape[3]
    d = w_gu.shape[2]
    tn = _pick(d, [1024, 512, 256, 128])
    nsub = NDEV // KSUB
    return _mm((lp // tm, d // tn, 2 * nsub), dab, _spec((None, KSUB, tm, f8), lambda i, j, k: (k // nsub, k % nsub, i, 0)),
               [(w_gu, _spec((KSUB, None, tn, f8), lambda i, j, k: (k % nsub, k // nsub, j, 0)))], [],
               [((lp, d), F32, _spec((tm, tn), lambda i, j, k: (i, j)))],
               dims=NT, acc_shape=(tm, tn), after=after, name=f"{tag}_gate_up_dx")[0]


def kernel(x, meta_tokens, ffn1_norm, ffn1_w_gate, ffn1_w_up, ffn1_w_down, mix_norm, w_in, b_forget, fox_q_norm, fox_k_norm, w_branch_fox, w_branch_sb, w_out, ffn2_norm, ffn2_w_gate, ffn2_w_up, ffn2_w_down, loss_target, m_meta_tokens, m_ffn1_norm, m_ffn1_w_gate, m_ffn1_w_up, m_ffn1_w_down, m_mix_norm, m_w_in, m_b_forget, m_fox_q_norm, m_fox_k_norm, m_w_branch_fox, m_w_branch_sb, m_w_out, m_ffn2_norm, m_ffn2_w_gate, m_ffn2_w_up, m_ffn2_w_down, v_meta_tokens, v_ffn1_norm, v_ffn1_w_gate, v_ffn1_w_up, v_ffn1_w_down, v_mix_norm, v_w_in, v_b_forget, v_fox_q_norm, v_fox_k_norm, v_w_branch_fox, v_w_branch_sb, v_w_out, v_ffn2_norm, v_ffn2_w_gate, v_ffn2_w_up, v_ffn2_w_down):
    seq, d = x.shape[1], x.shape[2]
    d8 = d // NDEV
    w = BRANCH_WIDTH
    tile = 256 if seq % 256 == 0 else 128
    pad = tile - N_META
    lp = tile + seq
    tm = _pick(lp, [1088, 544, 256, 128])
    tr = _pick(tile, [256, 128])
    tkr = _pick(lp, [4352, 2176, 1088, 544, 256, 128])
    nb = lp // tile
    me = _flat_id(*_my_place())

    shards = [jnp.stack([ffn1_w_gate[0], ffn1_w_up[0]]).astype(BF16), ffn1_w_down[0].astype(BF16), w_in[0].astype(BF16),
              w_branch_fox[0].astype(BF16), w_branch_sb[0].astype(BF16), w_out[0].astype(BF16),
              jnp.stack([ffn2_w_gate[0], ffn2_w_up[0]]).astype(BF16), ffn2_w_down[0].astype(BF16)]
    w_gu1, = _allgather(shards[:1], "gather_ffn1")
    down1_copies, token = _send_start(shards[1:2], [_landing(s, me) for s in shards[1:2]], "first", w_gu1, "gather_down1_start")
    w_in_copies, token = _send_start(shards[2:3], [_landing(s, me) for s in shards[2:3]], "first", token, "gather_w_in_start")
    mixer_copies, token = _send_start(shards[3:6], [_landing(s, me) for s in shards[3:6]], "gather", token, "gather_mixer_start")
    ffn2_copies, token = _send_start(shards[6:], [_landing(s, me) for s in shards[6:]], "gather", token, "gather_ffn2_start")
    meta_full = _join_cols(_allgather([meta_tokens], "gather_meta", in_vmem=True)[0])

    h0 = jnp.concatenate([jnp.zeros((pad, d), F32), meta_full.astype(F32), x[0]], axis=0)
    saved1 = _ffn_up(h0, ffn1_norm, w_gu1, token, tm=tm, tag="ffn1")
    w_down1, = _gather_two_level(down1_copies, saved1[3], "gather_down1")
    h1 = _ffn_down(h0, saved1[3], w_down1, tm=tm, tag="ffn1")

    w_in_blocks, = _gather_two_level(w_in_copies, h1, "gather_w_in")
    wi = _join_cols(w_in_blocks)
    w_pa = jnp.concatenate([wi[:, :2 * w], jnp.pad(wi[:, 3 * w:3 * w + HEADS], ((0, 0), (0, LANES - HEADS)))], axis=1)
    w_pb = jnp.concatenate([wi[:, 2 * w:3 * w], wi[:, 3 * w + HEADS:]], axis=1)
    na, nbw = w_pa.shape[1], w_pb.shape[1]
    gate_blk = 4 * w // d

    n2 = _norm_fwd(h1, mix_norm, tm=tr, name="mix_norm")
    tma = _pick(lp, [544, 256, 128])
    tnd = _pick(d, [1024, 512, 256, 128])
    tnb = _pick(nbw, [512, 256, 128])
    proj_a = _mm2d(n2, w_pa, na, dims=NN, tm=tma, tn=na, tk=d, out_dtype=F32, name="proj_a")
    proj_b = _mm2d(n2, w_pb, nbw, dims=NN, tm=tm, tn=tnb, tk=d, out_dtype=BF16, name="proj_b")
    b_pad = jnp.pad(b_forget, ((0, 0), (0, LANES - HEADS)))
    q_gain, k_gain = fox_q_norm.reshape(1, w), fox_k_norm.reshape(1, w)
    fq, fk, logf = _fox_prep(proj_a, q_gain, k_gain, b_pad, tm=tr, pad=pad, name="fox_prep")
    c = _cumsum_rows(logf, tile=tile, reverse=False, name="forget_cumsum")
    c_rows = jnp.transpose(c[:, :HEADS]).reshape(HEADS, nb, 1, tile)
    o_fox, lse = _fox_fwd(fq, fk, proj_b, 0, c_rows, tile=tile, pad=pad, name="fox_fwd")
    o_sb, sb_total = _sb_fwd(proj_b, HEADS, 2 * HEADS, 3 * HEADS, tile=tile, pad=pad, name="sb_fwd")
    w_br_fox, w_br_sb, w_out_blocks = _send_wait(mixer_copies, o_sb, "gather_mixer_wait")
    w_out_full = w_out_blocks.reshape(d, d)

    def branch(o, w_blocks, name):
        return _mm((lp // tm, NDEV, 1), o, _spec((tm, w), lambda i, j, k: (i, 0)),
                   [(w_blocks, _spec((None, w, d8), lambda i, j, k: (j, 0, 0)))], [],
                   [((lp, d), BF16, _spec((tm, d8), lambda i, j, k: (i, j)))], dims=NN, name=name)[0]

    br_fox = branch(o_fox, w_br_fox, "branch_fox")
    br_sb = branch(o_sb, w_br_sb, "branch_sb")

    def merge_fn(i, tiles, consts):
        bf_, bs_, gf_, gs_ = [t.astype(F32) for t in tiles]
        return [_sigmoid(gf_) * bf_ + _sigmoid(gs_) * bs_], []

    gates_in = [(proj_b, d, gate_blk), (proj_b, d, gate_blk + 1)]
    merged, = _rowwise(merge_fn, [_whole(br_fox), _whole(br_sb)] + gates_in, [], [(d, BF16)], [], tm=tr, name="merge")
    h2 = _mm2d(merged, w_out_full, d, dims=NN, tm=tm, tn=tnd, tk=d, out_dtype=F32,
               epi=lambda accs, ex: [ex[0] + accs[0]], extras=[h1], name="out_proj")

    w_gu2, w_down2 = _send_wait(ffn2_copies, h2, "gather_ffn2_wait")
    saved3 = _ffn_up(h2, ffn2_norm, w_gu2, None, tm=tm, tag="ffn2")
    h3 = _ffn_down(h2, saved3[3], w_down2, tm=tm, tag="ffn2")

    skip = tile // tr

    def loss_fn(i, tiles, consts):
        real = i >= skip
        err = jnp.where(real, tiles[0] - tiles[1], 0.0)
        dy = err * (1.0 / d)
        part = 0.5 * jnp.sum(err * dy, axis=0, keepdims=True)
        return [dy, FFN_RESIDUAL_WEIGHT * dy], [part]

    dh3, dh3_half, loss_cols = _rowwise(
        loss_fn, [_whole(h3), _whole(loss_target[0])], [], [(d, F32, lp), (d, BF16, lp)], [(1, d)], tm=tr, name="loss",
        row_maps=[None, lambda i: jnp.maximum(i - skip, 0)])

    def own(g):
        return lax.dynamic_index_in_dim(g, me, 0, keepdims=False)

    def send_grads(grads_, name):
        return _send_start(grads_, [_landing(own(g), me) for g in grads_], "exchange", grads_[-1], name)

    sends = {}

    def send_piece(key):
        def on_ready(*gs):
            sends[key], token_ = send_grads(list(gs), f"exchange_{key}_start")
            return token_
        return on_ready

    dab3, dw_gu2, _ = _ffn_bwd_dw(dh3_half, saved3, w_down2, send_piece("down2"), tm=tm, tag="ffn2")
    dn3 = _ffn_bwd_dx(dab3, w_gu2, send_piece("gu2")(dw_gu2), tm=tm, tag="ffn2")
    dh2, dh2_bf, dg_ffn2 = _norm_bwd(dn3, h2, ffn2_norm, dh3, 1.0, tm=tr, name="ffn2_norm_bwd")

    dmerged = _mm2d(dh2_bf, w_out_full, d, dims=NT, tm=tm, tn=tnd, tk=d, out_dtype=BF16, name="out_proj_dx")
    dw_out = _mm2d(merged, dh2_bf, d, dims=TN, tm=tnd, tn=tnd, tk=tkr, out_dtype=BF16, name="out_proj_dw")
    token = send_piece("out")(dw_out.reshape(NDEV, d8, d))

    def merge_bwd_fn(i, tiles, consts):
        dm, bf_, bs_, gf_, gs_ = [t.astype(F32) for t in tiles]
        sf, ss = _sigmoid(gf_), _sigmoid(gs_)
        return [dm * sf, dm * ss, dm * bf_ * sf * (1.0 - sf), dm * bs_ * ss * (1.0 - ss)], []

    dbr_fox, dbr_sb, dg_fox, dg_sb = _rowwise(
        merge_bwd_fn, [_whole(dmerged), _whole(br_fox), _whole(br_sb)] + gates_in, [], [(d, BF16)] * 4, [], tm=tr, after=token,
        name="merge_bwd")

    tnw = _pick(w, [512, 256, 128])

    def branch_dx(dbr, w_blocks, after, name):
        return _mm((lp // tm, w // tnw, NDEV), dbr, _spec((tm, d8), lambda i, j, k: (i, k)),
                   [(w_blocks, _spec((None, tnw, d8), lambda i, j, k: (k, j, 0)))], [],
                   [((lp, w), BF16, _spec((tm, tnw), lambda i, j, k: (i, j)))], dims=NT, acc_shape=(tm, tnw), after=after,
                   name=name)[0]

    def branch_dw(o, dbr, name):
        return _mm((w // tnw, NDEV, lp // tkr), o, _spec((tkr, tnw), lambda i, j, k: (k, i)),
                   [(dbr, _spec((tkr, d8), lambda i, j, k: (k, j)))], [],
                   [((NDEV, w, d8), BF16, _spec((None, tnw, d8), lambda i, j, k: (j, i, 0)))],
                   dims=TN, acc_shape=(tnw, d8), name=name)[0]

    token = send_piece("branch")(branch_dw(o_fox, dbr_fox, "branch_fox_dw"), branch_dw(o_sb, dbr_sb, "branch_sb_dw"))
    do_fox = branch_dx(dbr_fox, w_br_fox, token, "branch_fox_dx")
    do_sb = branch_dx(dbr_sb, w_br_sb, token, "branch_sb_dx")

    dfq, dfk, dfv, dc_rows = _fox_bwd(fq, fk, proj_b, 0, c_rows, o_fox, do_fox, lse, tile=tile, pad=pad, name="fox_bwd")
    dsq, dsk, dsv = _sb_bwd(proj_b, HEADS, 2 * HEADS, 3 * HEADS, do_sb, sb_total, tile=tile, pad=pad, name="sb_bwd")
    dc = jnp.pad(jnp.transpose(dc_rows.reshape(HEADS, lp)), ((0, 0), (0, LANES - HEADS)))
    dlogf = _cumsum_rows(dc, tile=tile, reverse=True, name="forget_cumsum_bwd")
    dproj_a, dg_q, dg_k, dg_b = _fox_prep_bwd(proj_a, dfq, dfk, dlogf, q_gain, k_gain, b_pad, tm=tr, pad=pad, name="fox_prep_bwd")
    dproj_b = jnp.concatenate([dfv, dsq, dsk, dsv, dg_fox, dg_sb], axis=1)

    dw_pa = _mm2d(n2, dproj_a, na, dims=TN, tm=tnd, tn=na, tk=_pick(lp, [544, 256, 128]), out_dtype=BF16, name="proj_a_dw")
    dw_pb = _mm2d(n2, dproj_b, nbw, dims=TN, tm=tnd, tn=tnb, tk=tkr, out_dtype=BF16, name="proj_b_dw")
    dw_in = jnp.concatenate([dw_pa[:, :2 * w], dw_pb[:, :w], dw_pa[:, 2 * w:2 * w + HEADS], dw_pb[:, w:]], axis=1)
    token = send_piece("w_in")(_split_cols(dw_in))
    dn2_a = _mm2d(dproj_a, w_pa, d, dims=NT, tm=tm, tn=tnd, tk=na, out_dtype=F32, after=token, name="proj_a_dx")
    dn2 = _mm2d(dproj_b, w_pb, d, dims=NT, tm=tm, tn=tnd, tk=_pick(nbw, [2048, 1024, 512, 256, 128]), out_dtype=F32,
                epi=lambda accs, ex: [accs[0] + ex[0]], extras=[dn2_a], name="proj_b_dx")
    dh1, dh1_half, dg_mix = _norm_bwd(dn2, h1, mix_norm, dh2, FFN_RESIDUAL_WEIGHT, tm=tr, name="mix_norm_bwd")

    dab1, dw_gu1, _ = _ffn_bwd_dw(dh1_half, saved1, w_down1, send_piece("down1"), tm=tm, tag="ffn1")
    dn1 = _ffn_bwd_dx(dab1, w_gu1, send_piece("gu1")(dw_gu1), tm=tm, tag="ffn1")
    dh0, _, dg_ffn1 = _norm_bwd(dn1, h0, ffn1_norm, dh1, None, tm=tr, name="ffn1_norm_bwd")
    grad_x = dh0[tile:][None]

    r_down2, = _send_wait(sends["down2"], dh0, "exchange_down2_wait")
    r_gu2, = _send_wait(sends["gu2"], r_down2, "exchange_gu2_wait")
    r_out, = _send_wait(sends["out"], r_gu2, "exchange_out_wait")
    r_br_fox, r_br_sb = _send_wait(sends["branch"], r_out, "exchange_branch_wait")
    r_in, = _send_wait(sends["w_in"], r_br_sb, "exchange_w_in_wait")
    grads, deltas, new_ms, new_vs = {}, {}, {}, {}

    def adamw_big(entries):
        for k, (parts, sel, wt, mt, vt) in entries.items():
            g, dl, mn, vn = _adamw_summed(parts, sel, wt[0], mt[0], vt[0], name=f"adamw_{k}")
            grads[k], deltas[k], new_ms[k], new_vs[k] = g[None], dl[None], mn[None], vn[None]

    adamw_big(dict(ffn2_w_gate=(r_gu2, 0, ffn2_w_gate, m_ffn2_w_gate, v_ffn2_w_gate),
                   ffn2_w_up=(r_gu2, 1, ffn2_w_up, m_ffn2_w_up, v_ffn2_w_up),
                   ffn2_w_down=(r_down2, 0, ffn2_w_down, m_ffn2_w_down, v_ffn2_w_down),
                   w_in=(r_in, 0, w_in, m_w_in, v_w_in),
                   w_branch_fox=(r_br_fox, 0, w_branch_fox, m_w_branch_fox, v_w_branch_fox),
                   w_branch_sb=(r_br_sb, 0, w_branch_sb, m_w_branch_sb, v_w_branch_sb),
                   w_out=(r_out, 0, w_out, m_w_out, v_w_out)))
    done = sum(v[0, 0, :1] for v in new_vs.values())
    r_down1, = _send_wait(sends["down1"], done, "exchange_down1_wait")
    r_gu1, = _send_wait(sends["gu1"], r_down1, "exchange_gu1_wait")
    adamw_big(dict(ffn1_w_gate=(r_gu1, 0, ffn1_w_gate, m_ffn1_w_gate, v_ffn1_w_gate),
                   ffn1_w_up=(r_gu1, 1, ffn1_w_up, m_ffn1_w_up, v_ffn1_w_up),
                   ffn1_w_down=(r_down1, 0, ffn1_w_down, m_ffn1_w_down, v_ffn1_w_down)))

    small_parts = [dh0[pad:tile], dg_ffn1, dg_mix, dg_ffn2, dg_b[:, :HEADS], dg_q, dg_k, loss_cols]
    small_packed, small_spans = _pack_rows(small_parts, LANES, SMALL_ROWS)
    small_sum = _allsum_small(small_packed, "sum_small")
    g_meta_full, g_ffn1n, g_mixn, g_ffn2n, g_bf, g_qn, g_kn, loss_vec = [
        _unpack(small_sum, span, part.shape) for span, part in zip(small_spans, small_parts)]
    loss = jnp.sum(loss_vec)
    g_meta = lax.dynamic_slice_in_dim(g_meta_full, me * d8, d8, axis=1)
    g_qn, g_kn = g_qn.reshape(fox_q_norm.shape), g_kn.reshape(fox_k_norm.shape)

    small = dict(meta_tokens=(g_meta, meta_tokens, m_meta_tokens, v_meta_tokens),
                 ffn1_norm=(g_ffn1n, ffn1_norm, m_ffn1_norm, v_ffn1_norm),
                 mix_norm=(g_mixn, mix_norm, m_mix_norm, v_mix_norm),
                 b_forget=(g_bf, b_forget, m_b_forget, v_b_forget),
                 fox_q_norm=(g_qn, fox_q_norm, m_fox_q_norm, v_fox_q_norm),
                 fox_k_norm=(g_kn, fox_k_norm, m_fox_k_norm, v_fox_k_norm),
                 ffn2_norm=(g_ffn2n, ffn2_norm, m_ffn2_norm, v_ffn2_norm))
    for k, (g, wt, mt, vt) in small.items():
        flat = lambda t: t.reshape(-1, t.shape[-1])
        dl, mn, vn = _adamw_plain(flat(g), flat(wt), flat(mt), flat(vt), name=f"adamw_{k}")
        grads[k], deltas[k], new_ms[k], new_vs[k] = g, dl.reshape(wt.shape), mn.reshape(wt.shape), vn.reshape(wt.shape)

    order = ["meta_tokens", "ffn1_norm", "ffn1_w_gate", "ffn1_w_up", "ffn1_w_down", "mix_norm", "w_in", "b_forget",
             "fox_q_norm", "fox_k_norm", "w_branch_fox", "w_branch_sb", "w_out", "ffn2_norm", "ffn2_w_gate",
             "ffn2_w_up", "ffn2_w_down"]
    return (loss, grad_x, *[grads[k] for k in order], *[deltas[k] for k in order],
            *[new_ms[k] for k in order], *[new_vs[k] for k in order])
```

```python
import jax
import jax.numpy as jnp
from jax import lax
from jax.experimental import pallas as pl
from jax.experimental.pallas import tpu as pltpu

F32 = jnp.float32
BF16 = jnp.bfloat16
MESH = pl.DeviceIdType.MESH

NDEV = 8
N_META = 16
HEAD_DIM = 128
HEADS = 8
BRANCH_WIDTH = HEADS * HEAD_DIM
RMS_EPS = 1e-6
FFN_RESIDUAL_WEIGHT = 0.5
ATTN_SCALE = HEAD_DIM ** -0.5
MASKED_LOGIT = -1e30

ADAM_LR = 0.001
ADAM_B1 = 0.9
ADAM_B2 = 0.999
ADAM_EPS = 1e-08
ADAM_WD = 0.01
ADAM_STEP = 10

LANES = 128
SMALL_ROWS = 8
ADAMW_TILE_ELEMS = 160 * 1024
KSUB = 4


def _pick(n, prefs):
    for p in prefs:
        if p <= n and n % p == 0:
            return p
    return n


def _dot(a, b, dims):
    return lax.dot_general(a, b, (dims, ((), ())), preferred_element_type=F32)


NN = ((1,), (0,))
TN = ((0,), (0,))
NT = ((1,), (1,))


def _split_bf16(x):
    hi = x.astype(BF16)
    lo = (x - hi.astype(F32)).astype(BF16)
    return hi, lo


def _sigmoid_parts(z):
    e = jnp.exp(-jnp.abs(z))
    t = 1.0 + e
    return e, 1.0 / t, jnp.log(t)


def _sigmoid(x):
    return 0.5 * jnp.tanh(0.5 * x) + 0.5


def _my_place():
    return lax.axis_index("x"), lax.axis_index("y"), lax.axis_index("c")


def _flat_id(px, py, pc):
    return 4 * px + 2 * py + pc


def _peer(x, y, c, k):
    px = 1 - x if k & 4 else x
    py = 1 - y if k & 2 else y
    pc = 1 - c if k & 1 else c
    return px, py, pc


def _allgather(shards, name, in_vmem=False):
    n = len(shards)

    def body(*refs):
        x_refs, out_refs = refs[:n], refs[n:2 * n]
        send_sems, recv_sems, local_sems = refs[2 * n:]
        x, y, c = _my_place()
        me, sibling = (x, y, c), (x, y, 1 - c)
        chips = [(1 - x, y), (x, 1 - y), (1 - x, 1 - y)]

        def block(a, place):
            return out_refs[a].at[_flat_id(*place)]

        def copy(a, k, place, to, src=None):
            return pltpu.make_async_remote_copy(
                src_ref=block(a, place) if src is None else src, dst_ref=block(a, place),
                send_sem=send_sems.at[7 * a + k], recv_sem=recv_sems.at[7 * a + k], device_id=to, device_id_type=MESH)

        mine = [pltpu.make_async_copy(x_refs[a], block(a, me), local_sems.at[a]) for a in range(n)]
        for cp in mine:
            cp.start()
        first = []
        for a in range(n):
            first.append(copy(a, 0, me, sibling, src=x_refs[a]))
            first += [copy(a, 1 + j, me, (*chip, c), src=x_refs[a]) for j, chip in enumerate(chips)]
        for cp in first:
            cp.start()
        passed = []
        for j, chip in enumerate(chips):
            for a in range(n):
                copy(a, 1 + j, (*chip, c), me).wait_recv()
                passed.append(copy(a, 4 + j, (*chip, c), sibling))
                passed[-1].start()
        for a in range(n):
            copy(a, 0, sibling, me).wait_recv()
        for j, chip in enumerate(chips):
            for a in range(n):
                copy(a, 4 + j, (*chip, 1 - c), me).wait_recv()
        for cp in first + passed:
            cp.wait_send()
        for cp in mine:
            cp.wait()

    space = pltpu.VMEM if in_vmem else pl.ANY
    return pl.pallas_call(
        body, name=name,
        out_shape=[jax.ShapeDtypeStruct((NDEV,) + s.shape, s.dtype) for s in shards],
        in_specs=[pl.BlockSpec(memory_space=space)] * n,
        out_specs=[pl.BlockSpec(memory_space=space)] * n,
        scratch_shapes=[pltpu.SemaphoreType.DMA((7 * n,)), pltpu.SemaphoreType.DMA((7 * n,)), pltpu.SemaphoreType.DMA((n,))],
    )(*shards)


def _allsum_small(part, name):
    rows, cols = part.shape

    def body(p_ref, out_ref, buf, send_sems, recv_sems):
        x, y, c = _my_place()
        me = _flat_id(x, y, c)
        buf[me] = p_ref[...]
        copies = []
        for k in range(1, NDEV):
            copies.append(pltpu.make_async_remote_copy(
                src_ref=p_ref, dst_ref=buf.at[me], send_sem=send_sems.at[k - 1], recv_sem=recv_sems.at[k - 1],
                device_id=_peer(x, y, c, k), device_id_type=MESH))
        for cp in copies:
            cp.start()
        for cp in copies:
            cp.wait()
        total = buf[0]
        for j in range(1, NDEV):
            total = total + buf[j]
        out_ref[...] = total

    return pl.pallas_call(
        body, name=name,
        out_shape=jax.ShapeDtypeStruct((rows, cols), F32),
        in_specs=[pl.BlockSpec(memory_space=pltpu.VMEM)],
        out_specs=pl.BlockSpec(memory_space=pltpu.VMEM),
        scratch_shapes=[pltpu.VMEM((NDEV, rows, cols), F32),
                        pltpu.SemaphoreType.DMA((7,)), pltpu.SemaphoreType.DMA((7,))],
    )(part)


_HBM = pl.BlockSpec(memory_space=pltpu.HBM)
_SEM = pl.BlockSpec(memory_space=pltpu.SEMAPHORE)
_DATAFLOW = pltpu.SideEffectType.DATAFLOW_SIDE_EFFECTING


COPIES_PER_ARRAY = {"gather": 7, "exchange": 7, "first": 4, "forward": 3}


def _send_copies(plan, src_refs, land_refs, send_sems, recv_sems):
    x, y, c = _my_place()
    me = _flat_id(x, y, c)
    per = COPIES_PER_ARRAY[plan]
    copies = []

    def add(a, slot, src, dst, to):
        copies.append(pltpu.make_async_remote_copy(
            src_ref=src, dst_ref=dst, send_sem=send_sems.at[per * a + slot], recv_sem=recv_sems.at[per * a + slot],
            device_id=to, device_id_type=MESH))

    for a, land_ref in enumerate(land_refs):
        if plan in ("gather", "exchange"):
            for k in range(1, NDEV):
                peer = _peer(x, y, c, k)
                add(a, k - 1, src_refs[a].at[_flat_id(*peer)] if plan == "exchange" else src_refs[a], land_ref.at[me], peer)
        elif plan == "first":
            for slot, k in enumerate((1, 2, 4, 6)):
                add(a, slot, src_refs[a], land_ref.at[me], _peer(x, y, c, k))
        else:
            for slot, k in enumerate((2, 4, 6)):
                block = land_ref.at[_flat_id(*_peer(x, y, c, k))]
                add(a, slot, block, block, (x, y, 1 - c))
    return copies


def _send_start(srcs, lands, plan, after, name):
    ns, nl = len(srcs), len(lands)
    nsem = COPIES_PER_ARRAY[plan] * nl

    def body(*refs):
        for cp in _send_copies(plan, refs[:ns], refs[ns:ns + nl], refs[ns + nl + 1], refs[ns + nl + 2]):
            cp.start()
        refs[-1][...] = jnp.zeros_like(refs[-1])

    operands = [pltpu.with_memory_space_constraint(t, pltpu.HBM) for t in list(srcs) + list(lands)]
    outs = pl.pallas_call(
        body, name=name,
        out_shape=(pltpu.SemaphoreType.DMA((nsem,)), pltpu.SemaphoreType.DMA((nsem,)),
                   *[pltpu.HBM(t.shape, t.dtype) for t in operands[ns:]], jax.ShapeDtypeStruct((SMALL_ROWS, LANES), F32)),
        in_specs=[_HBM] * (ns + nl) + [pl.BlockSpec(memory_space=pl.ANY)],
        out_specs=(_SEM, _SEM, *[_HBM] * nl, pl.BlockSpec(memory_space=pltpu.VMEM)),
        input_output_aliases={ns + i: 2 + i for i in range(nl)},
        compiler_params=pltpu.CompilerParams(has_side_effects=_DATAFLOW),
    )(*operands, after)
    return (plan, outs[0], outs[1], operands[:ns], list(outs[2:2 + nl])), outs[-1]


def _send_wait(handle, after, name):
    plan, send_sems, recv_sems, srcs, lands = handle
    ns, nl = len(srcs), len(lands)

    def body(*refs):
        for cp in _send_copies(plan, refs[:ns], refs[ns:ns + nl], refs[ns + nl], refs[ns + nl + 1]):
            cp.wait_send()
            cp.wait_recv()

    outs = pl.pallas_call(
        body, name=name,
        out_shape=tuple(pltpu.HBM(t.shape, t.dtype) for t in lands),
        in_specs=[_HBM] * (ns + nl) + [_SEM, _SEM, pl.BlockSpec(memory_space=pl.ANY)],
        out_specs=tuple([_HBM] * nl),
        input_output_aliases={ns + i: i for i in range(nl)},
        compiler_params=pltpu.CompilerParams(has_side_effects=_DATAFLOW),
    )(*srcs, *lands, send_sems, recv_sems, after)
    return list(outs)


def _gather_two_level(first_handle, after, name):
    lands = _send_wait(first_handle, after, f"{name}_wait")
    forward, _ = _send_start([], lands, "forward", after, f"{name}_forward_start")
    return _send_wait(forward, after, f"{name}_forward_wait")


def _landing(own_block, me):
    return lax.dynamic_update_index_in_dim(lax.empty((NDEV,) + own_block.shape, own_block.dtype), own_block, me, 0)


def _spec(block, index_map):
    return pl.BlockSpec(block, index_map)


def _mm(grid, a, a_spec, bs, extras, outs, *, dims, acc_shape=None, epi=None, after=None, name):
    nk = grid[2]
    nb, ne, no = len(bs), len(extras), len(outs)
    nafter = 0 if after is None else 1
    if epi is None:
        epi = lambda accs, ex: [accs[0]]

    def body(*refs):
        a_ref, b_refs = refs[0], refs[1:1 + nb]
        e_refs = refs[1 + nb:1 + nb + ne]
        o_refs = refs[1 + nb + ne + nafter:1 + nb + ne + nafter + no]
        acc_refs = refs[1 + nb + ne + nafter + no:]
        def finish(accs):
            for o_ref, tile in zip(o_refs, epi(accs, [e_ref[...] for e_ref in e_refs])):
                o_ref[...] = tile.astype(o_ref.dtype)

        def product(b_ref):
            if len(a_ref.shape) == 2:
                return _dot(a_ref[...], b_ref[...], dims)
            total = _dot(a_ref[0], b_ref[0], dims)
            for t in range(1, a_ref.shape[0]):
                total = total + _dot(a_ref[t], b_ref[t], dims)
            return total

        if nk == 1:
            finish([product(b_ref) for b_ref in b_refs])
        else:
            k = pl.program_id(2)

            @pl.when(k == 0)
            def _():
                for acc_ref in acc_refs:
                    acc_ref[...] = jnp.zeros_like(acc_ref)

            for acc_ref, b_ref in zip(acc_refs, b_refs):
                acc_ref[...] += product(b_ref)

            @pl.when(k == nk - 1)
            def _():
                finish([acc_ref[...] for acc_ref in acc_refs])

    return pl.pallas_call(
        body, name=name,
        grid=grid,
        in_specs=[a_spec] + [s for _, s in bs] + [s for _, s in extras] + [pl.BlockSpec(memory_space=pl.ANY)] * nafter,
        out_specs=[s for _, _, s in outs],
        out_shape=[jax.ShapeDtypeStruct(shape, dt) for shape, dt, _ in outs],
        scratch_shapes=[pltpu.VMEM(acc_shape, F32) for _ in bs] if nk > 1 else [],
        compiler_params=pltpu.CompilerParams(dimension_semantics=("parallel", "parallel", "arbitrary")),
    )(a, *[b for b, _ in bs], *[e for e, _ in extras], *([after] if nafter else []))


def _mm2d(a, b, n_cols, *, dims, tm, tn, tk, out_dtype, epi=None, extras=(), after=None, name):
    m_rows, k_len = (a.shape[1], a.shape[0]) if dims == TN else a.shape
    assert m_rows % tm == 0 and n_cols % tn == 0 and k_len % tk == 0, (name, a.shape, n_cols, tm, tn, tk)
    a_spec = _spec((tk, tm), lambda i, j, k: (k, i)) if dims == TN else _spec((tm, tk), lambda i, j, k: (i, k))
    b_spec = _spec((tn, tk), lambda i, j, k: (j, k)) if dims == NT else _spec((tk, tn), lambda i, j, k: (k, j))
    tile = _spec((tm, tn), lambda i, j, k: (i, j))
    return _mm((m_rows // tm, n_cols // tn, k_len // tk), a, a_spec, [(b, b_spec)], [(e, tile) for e in extras],
               [((m_rows, n_cols), out_dtype, tile)], dims=dims, acc_shape=(tm, tn), epi=epi, after=after, name=name)[0]


def _rowwise(fn, ins, consts, outs, sums, *, tm, name, row_maps=None, after=None):
    m_rows = outs[0][2] if len(outs[0]) == 3 else ins[0][0].shape[0]
    n = m_rows // tm
    ni, nc, no = len(ins), len(consts), len(outs)
    nafter = 0 if after is None else 1
    row_maps = row_maps or [None] * ni

    def body(*refs):
        i = pl.program_id(0)
        in_tiles = [r[...] for r in refs[:ni]]
        const_values = [r[...] for r in refs[ni:ni + nc]]
        o_refs = refs[ni + nc + nafter:ni + nc + nafter + no]
        s_refs = refs[ni + nc + nafter + no:]
        out_tiles, sum_terms = fn(i, in_tiles, const_values)
        for o_ref, tile in zip(o_refs, out_tiles):
            o_ref[...] = tile.astype(o_ref.dtype)
        if s_refs:
            @pl.when(i == 0)
            def _():
                for s_ref in s_refs:
                    s_ref[...] = jnp.zeros_like(s_ref)

            for s_ref, term in zip(s_refs, sum_terms):
                s_ref[...] += term

    def in_spec(width, col, rmap):
        if rmap is None:
            return pl.BlockSpec((tm, width), lambda i: (i, col))
        return pl.BlockSpec((tm, width), lambda i: (rmap(i), col))

    return pl.pallas_call(
        body, name=name,
        grid=(n,),
        in_specs=[in_spec(w, col, rmap) for (_, w, col), rmap in zip(ins, row_maps)]
        + [pl.BlockSpec(cst.shape, lambda i: (0, 0)) for cst in consts] + [pl.BlockSpec(memory_space=pl.ANY)] * nafter,
        out_specs=[pl.BlockSpec((tm, o[0]), lambda i: (i, 0)) for o in outs]
        + [pl.BlockSpec(s, lambda i: (0, 0)) for s in sums],
        out_shape=[jax.ShapeDtypeStruct((m_rows, o[0]), o[1]) for o in outs]
        + [jax.ShapeDtypeStruct(s, F32) for s in sums],
        compiler_params=pltpu.CompilerParams(dimension_semantics=("arbitrary",)),
    )(*[arr for arr, _, _ in ins], *consts, *([after] if nafter else []))


def _whole(arr):
    return (arr, arr.shape[1], 0)


def _rms(x, gain):
    r = lax.rsqrt(jnp.mean(x * x, axis=-1, keepdims=True) + RMS_EPS)
    return x * r * gain


def _rms_bwd(x, gain, dy):
    r = lax.rsqrt(jnp.mean(x * x, axis=-1, keepdims=True) + RMS_EPS)
    u = dy * gain
    dx = r * u - x * (r * r * r) * jnp.mean(x * u, axis=-1, keepdims=True)
    return dx, dy * x * r


def _norm_fwd(h, gain, *, tm, name):
    def fn(i, tiles, consts):
        return [_rms(tiles[0], consts[0])], []
    return _rowwise(fn, [_whole(h)], [gain], [(h.shape[1], BF16)], [], tm=tm, name=name)[0]


def _norm_bwd(dn, h, gain, dh_in, low_scale, *, tm, name, after=None, first=0, count=None):
    d = h.shape[1]
    rows = h.shape[0] if count is None else count * tm
    shift = lambda i: i + first

    def fn(i, tiles, consts):
        dx, dg_rows = _rms_bwd(tiles[1], consts[0], tiles[0])
        dh = tiles[2] + dx
        return [dh] + ([] if low_scale is None else [low_scale * dh]), [jnp.sum(dg_rows, axis=0, keepdims=True)]

    outs = _rowwise(fn, [_whole(dn), _whole(h), _whole(dh_in)], [gain],
                    [(d, F32, rows)] + ([] if low_scale is None else [(d, BF16, rows)]), [(1, d)], tm=tm, name=name,
                    row_maps=[shift] * 3, after=after)
    return (outs[0], None, outs[1]) if low_scale is None else tuple(outs)


def _swiglu_epi(accs, ex):
    a, b = accs
    return [a, b, a * _sigmoid(a) * b]


def _swiglu_bwd_epi(accs, ex):
    ds = accs[0]
    a, b = ex[0].astype(F32), ex[1].astype(F32)
    sig = _sigmoid(a)
    silu = a * sig
    dsilu = sig * (1.0 + a * (1.0 - sig))
    return [jnp.stack([(ds * b * dsilu).astype(BF16), (ds * silu).astype(BF16)], axis=0)]


def _attn_mask(i, j, tile, pad, strict):
    row = i * tile + lax.broadcasted_iota(jnp.int32, (tile, tile), 0)
    col = j * tile + lax.broadcasted_iota(jnp.int32, (tile, tile), 1)
    causal = (col < row) if strict else (col <= row)
    return causal & ((col >= pad) | (row < pad))


FWD_GROUP = 4
BWD_GROUP = 2


def _walk_key_tiles(i, step, carry, widths, descending=False):
    diagonal = lambda j, c: step(j, c, True, 1)
    left = jnp.maximum(i - 1, 0)
    lo = 1
    if descending:
        carry = lax.fori_loop(jnp.maximum(i, 1), i + 1, diagonal, carry)
    else:
        carry = step(0, carry, True, 1)
    for nt in widths:
        count = left // nt
        if descending:
            top = lo + left
            carry = lax.fori_loop(0, count, lambda t, c, nt=nt, top=top: step(top - nt * (t + 1), c, False, nt), carry)
        else:
            carry = lax.fori_loop(0, count, lambda t, c, nt=nt, lo=lo: step(lo + nt * t, c, False, nt), carry)
            lo = lo + nt * count
        left = left - nt * count
    if descending:
        return step(0, carry, True, 1)
    return lax.fori_loop(jnp.maximum(i, 1), i + 1, diagonal, carry)


def _head_cols(g):
    return pl.ds(g * HEAD_DIM, HEAD_DIM)


def _key_rows(j, tile, nt):
    return pl.ds(pl.multiple_of(j * tile, tile), nt * tile)


def _key_tile(ref, j, tile, g, nt=1):
    return ref[_key_rows(j, tile, nt), _head_cols(g)]


def _key_bias(c_ref, g, j, nt):
    return c_ref[g, j] if nt == 1 else jnp.concatenate([c_ref[g, j + t] for t in range(nt)], axis=1)


def _head_specs(lp, tile, q_off, k_off, v_off, hp):
    gw = hp * HEAD_DIM
    q_spec = pl.BlockSpec((tile, gw), lambda h, i: (i, h + q_off // hp))
    k_spec = pl.BlockSpec((lp, gw), lambda h, i: (0, h + k_off // hp))
    v_spec = pl.BlockSpec((lp, gw), lambda h, i: (0, h + v_off // hp))
    return q_spec, k_spec, v_spec


def _fox_fwd(q, k, v, v_off, c_rows, *, tile, pad, name):
    lp = q.shape[0]
    nb = lp // tile
    hp = FWD_GROUP
    gw = hp * HEAD_DIM

    def body(q_ref, k_ref, v_ref, c_ref, o_ref, lse_ref):
        i = pl.program_id(1)
        qts = [q_ref[:, _head_cols(g)] for g in range(hp)]

        def step(j, carry, masked, nt):
            ok = _attn_mask(i, j, tile, pad, False) if masked else None
            out = []
            for g, (m, l, acc) in enumerate(carry):
                s = _dot(qts[g], _key_tile(k_ref, j, tile, g, nt), NT) * ATTN_SCALE - _key_bias(c_ref, g, j, nt)
                if masked:
                    s = jnp.where(ok, s, MASKED_LOGIT)
                m_new = jnp.maximum(m, jnp.max(s, axis=1, keepdims=True))
                p = jnp.exp(s - m_new)
                alpha = jnp.exp(m - m_new)
                l = alpha * l + jnp.sum(p, axis=1, keepdims=True)
                acc = alpha * acc + _dot(p.astype(BF16), _key_tile(v_ref, j, tile, g, nt), NN)
                out.append((m_new, l, acc))
            return tuple(out)

        init = (jnp.full((tile, 1), MASKED_LOGIT, F32), jnp.zeros((tile, 1), F32), jnp.zeros((tile, HEAD_DIM), F32))
        final = _walk_key_tiles(i, step, (init,) * hp, (4, 2, 1))
        for g, (m, l, acc) in enumerate(final):
            o_ref[:, _head_cols(g)] = (acc / l).astype(o_ref.dtype)
            lse_ref[g] = jnp.broadcast_to(m + jnp.log(l), (tile, LANES))

    q_spec, k_spec, v_spec = _head_specs(lp, tile, 0, 0, v_off, hp)
    return pl.pallas_call(
        body, name=name,
        grid=(HEADS // hp, nb),
        in_specs=[q_spec, k_spec, v_spec, pl.BlockSpec((hp, nb, 1, tile), lambda h, i: (h, 0, 0, 0))],
        out_specs=[pl.BlockSpec((tile, gw), lambda h, i: (i, h)),
                   pl.BlockSpec((hp, tile, LANES), lambda h, i: (h, i, 0))],
        out_shape=[jax.ShapeDtypeStruct((lp, BRANCH_WIDTH), BF16), jax.ShapeDtypeStruct((HEADS, lp, LANES), F32)],
        compiler_params=pltpu.CompilerParams(dimension_semantics=("parallel", "arbitrary")),
    )(q, k, v, c_rows)


def _fox_bwd(q, k, v, v_off, c_rows, o, do, lse, *, tile, pad, name):
    lp = q.shape[0]
    nb = lp // tile
    hp = BWD_GROUP
    gw = hp * HEAD_DIM

    def body(q_ref, k_ref, v_ref, c_ref, o_ref, do_ref, lse_ref, dq_ref, dk_ref, dv_ref, dc_ref, dk_acc, dv_acc, dc_acc):
        i = pl.program_id(1)

        @pl.when(i == 0)
        def _():
            dk_acc[...] = jnp.zeros_like(dk_acc)
            dv_acc[...] = jnp.zeros_like(dv_acc)
            dc_acc[...] = jnp.zeros_like(dc_acc)

        heads = range(hp)
        qts = [q_ref[:, _head_cols(g)] for g in heads]
        dots = [do_ref[:, _head_cols(g)] for g in heads]
        deltas = [jnp.sum(dots[g].astype(F32) * o_ref[:, _head_cols(g)].astype(F32), axis=1, keepdims=True) for g in heads]
        lse_cols = [lse_ref[g][:, :1] for g in heads]

        def step(j, dqs, masked, nt):
            rows = _key_rows(j, tile, nt)
            ok = _attn_mask(i, j, tile, pad, False) if masked else None
            out = []
            for g in heads:
                kt = _key_tile(k_ref, j, tile, g, nt)
                s = _dot(qts[g], kt, NT) * ATTN_SCALE - _key_bias(c_ref, g, j, nt)
                p = jnp.exp(s - lse_cols[g])
                if masked:
                    p = jnp.where(ok, p, 0.0)
                dp = _dot(dots[g], _key_tile(v_ref, j, tile, g, nt), NT)
                ds = p * (dp - deltas[g])
                dsb = ds.astype(BF16)
                dk_acc[rows, _head_cols(g)] += _dot(dsb, qts[g], TN)
                dv_acc[rows, _head_cols(g)] += _dot(p.astype(BF16), dots[g], TN)
                dc = -jnp.sum(ds, axis=0, keepdims=True)
                for t in range(nt):
                    dc_acc[g, j + t] += dc[:, t * tile:(t + 1) * tile]
                out.append(dqs[g] + _dot(dsb, kt, NN))
            return tuple(out)

        dqs = _walk_key_tiles(i, step, (jnp.zeros((tile, HEAD_DIM), F32),) * hp, (8, 4, 2, 1))
        for g in heads:
            dq_ref[:, _head_cols(g)] = dqs[g] * ATTN_SCALE

        @pl.when(i == nb - 1)
        def _():
            dk_ref[...] = dk_acc[...] * ATTN_SCALE
            dv_ref[...] = dv_acc[...].astype(dv_ref.dtype)
            dc_ref[...] = dc_acc[...]

    q_spec, k_spec, v_spec = _head_specs(lp, tile, 0, 0, v_off, hp)
    tile_spec = pl.BlockSpec((tile, gw), lambda h, i: (i, h))
    head_spec = pl.BlockSpec((lp, gw), lambda h, i: (0, h))
    c_spec = pl.BlockSpec((hp, nb, 1, tile), lambda h, i: (h, 0, 0, 0))
    return pl.pallas_call(
        body, name=name,
        grid=(HEADS // hp, nb),
        in_specs=[q_spec, k_spec, v_spec, c_spec, tile_spec, tile_spec,
                  pl.BlockSpec((hp, tile, LANES), lambda h, i: (h, i, 0))],
        out_specs=[tile_spec, head_spec, head_spec, c_spec],
        out_shape=[jax.ShapeDtypeStruct((lp, BRANCH_WIDTH), F32), jax.ShapeDtypeStruct((lp, BRANCH_WIDTH), F32),
                   jax.ShapeDtypeStruct((lp, BRANCH_WIDTH), BF16), jax.ShapeDtypeStruct((HEADS, nb, 1, tile), F32)],
        scratch_shapes=[pltpu.VMEM((lp, gw), F32), pltpu.VMEM((lp, gw), F32),
                        pltpu.VMEM((hp, nb, 1, tile), F32)],
        compiler_params=pltpu.CompilerParams(dimension_semantics=("parallel", "arbitrary")),
    )(q, k, v, c_rows, o, do, lse)


def _later_matrix(tile):
    return (lax.broadcasted_iota(jnp.int32, (tile, tile), 0) > lax.broadcasted_iota(jnp.int32, (tile, tile), 1)).astype(BF16)


def _earlier_matrix(tile):
    return (lax.broadcasted_iota(jnp.int32, (tile, tile), 0) < lax.broadcasted_iota(jnp.int32, (tile, tile), 1)).astype(BF16)


def _running_sums(x, tri, suffix):
    tile = tri.shape[0]
    blocks = [x[:, b:b + tile] for b in range(0, x.shape[1], tile)]
    sums = [jnp.sum(blk, axis=1, keepdims=True) for blk in blocks]
    out = []
    for b, blk in enumerate(blocks):
        hi, lo = _split_bf16(blk)
        inside = _dot(hi, tri, NN) + _dot(lo, tri, NN)
        for other in (sums[b + 1:] if suffix else sums[:b]):
            inside = inside + other
        out.append(inside)
    total = sums[0]
    for other in sums[1:]:
        total = total + other
    return (out[0] if len(out) == 1 else jnp.concatenate(out, axis=1)), total


def _sb_tile(qt, kt, ok, later):
    z = _dot(qt, kt, NT) * ATTN_SCALE
    e, r, lg = _sigmoid_parts(z)
    sp = jnp.maximum(z, 0.0) + lg
    spm = sp if ok is None else jnp.where(ok, sp, 0.0)
    within, sp_here = _running_sums(spm, later, True)
    return z, e, r, sp, sp_here, within


def _sb_fwd(qkv, q_off, k_off, v_off, *, tile, pad, name):
    lp = qkv.shape[0]
    nb = lp // tile
    hp = FWD_GROUP
    gw = hp * HEAD_DIM

    def body(q_ref, k_ref, v_ref, o_ref, tot_ref):
        i = pl.program_id(1)
        qts = [q_ref[:, _head_cols(g)] for g in range(hp)]
        later = _later_matrix(tile)

        def step(j, carry, masked, nt):
            ok = _attn_mask(i, j, tile, pad, True) if masked else None
            out = []
            for g, (right, acc) in enumerate(carry):
                z, _, _, sp, sp_here, within = _sb_tile(qts[g], _key_tile(k_ref, j, tile, g, nt), ok, later)
                w = jnp.exp(z - sp - within - right)
                if masked:
                    w = jnp.where(ok, w, 0.0)
                acc = acc + _dot(w.astype(BF16), _key_tile(v_ref, j, tile, g, nt), NN)
                out.append((right + sp_here, acc))
            return tuple(out)

        init = (jnp.zeros((tile, 1), F32), jnp.zeros((tile, HEAD_DIM), F32))
        final = _walk_key_tiles(i, step, (init,) * hp, (4, 2, 1), descending=True)
        for g, (total, acc) in enumerate(final):
            o_ref[:, _head_cols(g)] = acc.astype(o_ref.dtype)
            tot_ref[g] = jnp.broadcast_to(total, (tile, LANES))

    q_spec, k_spec, v_spec = _head_specs(lp, tile, q_off, k_off, v_off, hp)
    return pl.pallas_call(
        body, name=name,
        grid=(HEADS // hp, nb),
        in_specs=[q_spec, k_spec, v_spec],
        out_specs=[pl.BlockSpec((tile, gw), lambda h, i: (i, h)),
                   pl.BlockSpec((hp, tile, LANES), lambda h, i: (h, i, 0))],
        out_shape=[jax.ShapeDtypeStruct((lp, BRANCH_WIDTH), BF16), jax.ShapeDtypeStruct((HEADS, lp, LANES), F32)],
        compiler_params=pltpu.CompilerParams(dimension_semantics=("parallel", "arbitrary")),
    )(qkv, qkv, qkv)


def _sb_bwd(qkv, q_off, k_off, v_off, do, total, *, tile, pad, name):
    lp = qkv.shape[0]
    nb = lp // tile
    hp = BWD_GROUP
    gw = hp * HEAD_DIM

    def body(q_ref, k_ref, v_ref, do_ref, tot_ref, dq_ref, dk_ref, dv_ref, dk_acc, dv_acc):
        i = pl.program_id(1)

        @pl.when(i == 0)
        def _():
            dk_acc[...] = jnp.zeros_like(dk_acc)
            dv_acc[...] = jnp.zeros_like(dv_acc)

        heads = range(hp)
        qts = [q_ref[:, _head_cols(g)] for g in heads]
        dots = [do_ref[:, _head_cols(g)] for g in heads]
        total_cols = [tot_ref[g][:, :1] for g in heads]
        later, earlier = _later_matrix(tile), _earlier_matrix(tile)

        def step(j, carry, masked, nt):
            rows = _key_rows(j, tile, nt)
            ok = _attn_mask(i, j, tile, pad, True) if masked else None
            out = []
            for g, (dq, sp_before, dlw_before) in enumerate(carry):
                kt = _key_tile(k_ref, j, tile, g, nt)
                z, e, r, sp, sp_here, within = _sb_tile(qts[g], kt, ok, later)
                right = total_cols[g] - sp_before - sp_here
                w = jnp.exp(z - sp - within - right)
                if masked:
                    w = jnp.where(ok, w, 0.0)
                dlw = w * _dot(dots[g], _key_tile(v_ref, j, tile, g, nt), NT)
                before, dlw_here = _running_sums(dlw, earlier, False)
                sig = jnp.where(z >= 0, r, e * r)
                dz = dlw * (1.0 - sig) - sig * (before + dlw_before)
                if masked:
                    dz = jnp.where(ok, dz, 0.0)
                dzb = dz.astype(BF16)
                dk_acc[rows, _head_cols(g)] += _dot(dzb, qts[g], TN)
                dv_acc[rows, _head_cols(g)] += _dot(w.astype(BF16), dots[g], TN)
                out.append((dq + _dot(dzb, kt, NN), sp_before + sp_here, dlw_before + dlw_here))
            return tuple(out)

        zero_col = jnp.zeros((tile, 1), F32)
        final = _walk_key_tiles(i, step, ((jnp.zeros((tile, HEAD_DIM), F32), zero_col, zero_col),) * hp, (4, 2, 1))
        for g in heads:
            dq_ref[:, _head_cols(g)] = (final[g][0] * ATTN_SCALE).astype(dq_ref.dtype)

        @pl.when(i == nb - 1)
        def _():
            dk_ref[...] = (dk_acc[...] * ATTN_SCALE).astype(dk_ref.dtype)
            dv_ref[...] = dv_acc[...].astype(dv_ref.dtype)

    q_spec, k_spec, v_spec = _head_specs(lp, tile, q_off, k_off, v_off, hp)
    tile_spec = pl.BlockSpec((tile, gw), lambda h, i: (i, h))
    head_spec = pl.BlockSpec((lp, gw), lambda h, i: (0, h))
    return pl.pallas_call(
        body, name=name,
        grid=(HEADS // hp, nb),
        in_specs=[q_spec, k_spec, v_spec, tile_spec, pl.BlockSpec((hp, tile, LANES), lambda h, i: (h, i, 0))],
        out_specs=[tile_spec, head_spec, head_spec],
        out_shape=[jax.ShapeDtypeStruct((lp, BRANCH_WIDTH), BF16)] * 3,
        scratch_shapes=[pltpu.VMEM((lp, gw), F32), pltpu.VMEM((lp, gw), F32)],
        compiler_params=pltpu.CompilerParams(dimension_semantics=("parallel", "arbitrary")),
    )(qkv, qkv, qkv, do, total)


def _cumsum_rows(x, *, tile, reverse, name):
    lp = x.shape[0]
    nb = lp // tile

    def body(x_ref, o_ref, carry):
        @pl.when(pl.program_id(0) == 0)
        def _():
            carry[...] = jnp.zeros_like(carry)

        r = lax.broadcasted_iota(jnp.int32, (tile, tile), 0)
        c = lax.broadcasted_iota(jnp.int32, (tile, tile), 1)
        tri = ((c >= r) if reverse else (c <= r)).astype(BF16)
        hi, lo = _split_bf16(x_ref[...])
        run = _dot(tri, hi, NN) + _dot(tri, lo, NN) + carry[...]
        o_ref[...] = run
        carry[...] = run[:1, :] if reverse else run[tile - 1:, :]

    order = (lambda i: (nb - 1 - i, 0)) if reverse else (lambda i: (i, 0))
    return pl.pallas_call(
        body, name=name,
        grid=(nb,),
        in_specs=[pl.BlockSpec((tile, LANES), order)],
        out_specs=pl.BlockSpec((tile, LANES), order),
        out_shape=jax.ShapeDtypeStruct((lp, LANES), F32),
        scratch_shapes=[pltpu.VMEM((1, LANES), F32)],
        compiler_params=pltpu.CompilerParams(dimension_semantics=("arbitrary",)),
    )(x)


def _log_sigmoid(x):
    return jnp.minimum(x, 0.0) - jnp.log(1.0 + jnp.exp(-jnp.abs(x)))


def _forget_mask(i, tm, pad):
    row = i * tm + lax.broadcasted_iota(jnp.int32, (tm, LANES), 0)
    lane = lax.broadcasted_iota(jnp.int32, (tm, LANES), 1)
    return (row >= pad) & (lane < HEADS)


def _fox_prep(proj_a, q_gain, k_gain, b_forget, *, tm, pad, name):
    w = BRANCH_WIDTH

    def fn(i, tiles, consts):
        pa = tiles[0]
        qs, ks = [], []
        for h in range(HEADS):
            lo = h * HEAD_DIM
            qs.append(_rms(pa[:, lo:lo + HEAD_DIM], consts[0][:, lo:lo + HEAD_DIM]))
            ks.append(_rms(pa[:, w + lo:w + lo + HEAD_DIM], consts[1][:, lo:lo + HEAD_DIM]))
        logf = jnp.where(_forget_mask(i, tm, pad), _log_sigmoid(pa[:, 2 * w:] + consts[2]), 0.0)
        return [jnp.concatenate(qs, axis=1), jnp.concatenate(ks, axis=1), logf], []

    return _rowwise(fn, [_whole(proj_a)], [q_gain, k_gain, b_forget],
                    [(w, BF16), (w, BF16), (LANES, F32)], [], tm=tm, name=name)


def _fox_prep_bwd(proj_a, dq, dk, dlogf, q_gain, k_gain, b_forget, *, tm, pad, name):
    w = BRANCH_WIDTH

    def fn(i, tiles, consts):
        pa, dqt, dkt, dlf = tiles
        dxs_q, dxs_k, dgs_q, dgs_k = [], [], [], []
        for h in range(HEADS):
            lo = h * HEAD_DIM
            dx, dg = _rms_bwd(pa[:, lo:lo + HEAD_DIM], consts[0][:, lo:lo + HEAD_DIM], dqt[:, lo:lo + HEAD_DIM])
            dxs_q.append(dx)
            dgs_q.append(jnp.sum(dg, axis=0, keepdims=True))
            dx, dg = _rms_bwd(pa[:, w + lo:w + lo + HEAD_DIM], consts[1][:, lo:lo + HEAD_DIM], dkt[:, lo:lo + HEAD_DIM])
            dxs_k.append(dx)
            dgs_k.append(jnp.sum(dg, axis=0, keepdims=True))
        xf = pa[:, 2 * w:] + consts[2]
        e, r, _ = _sigmoid_parts(xf)
        df = jnp.where(_forget_mask(i, tm, pad), dlf * jnp.where(xf >= 0, e * r, r), 0.0)
        return ([jnp.concatenate(dxs_q + dxs_k + [df], axis=1)],
                [jnp.concatenate(dgs_q, axis=1), jnp.concatenate(dgs_k, axis=1), jnp.sum(df, axis=0, keepdims=True)])

    return _rowwise(fn, [_whole(proj_a), _whole(dq), _whole(dk), _whole(dlogf)], [q_gain, k_gain, b_forget],
                    [(2 * w + LANES, BF16)], [(1, w), (1, w), (1, LANES)], tm=tm, name=name)


def _adamw_math(w, g, m, v):
    m = ADAM_B1 * m + (1.0 - ADAM_B1) * g
    v = ADAM_B2 * v + (1.0 - ADAM_B2) * (g * g)
    m_hat = m / (1.0 - ADAM_B1 ** ADAM_STEP)
    v_hat = v / (1.0 - ADAM_B2 ** ADAM_STEP)
    delta = -ADAM_LR * (m_hat / (jnp.sqrt(v_hat) + ADAM_EPS) + ADAM_WD * w)
    return delta, m, v


def _adamw_summed(parts, sel, w, m, v, *, name):
    rows, cols = w.shape
    tr = _pick(rows, [t for t in (512, 256, 128, 64, 32, 16, 8) if t * cols <= ADAMW_TILE_ELEMS])

    def body(p_ref, w_ref, m_ref, v_ref, g_out, d_out, m_out, v_out):
        g = p_ref[0].astype(F32)
        for j in range(1, NDEV):
            g = g + p_ref[j].astype(F32)
        delta, m_new, v_new = _adamw_math(w_ref[...], g, m_ref[...], v_ref[...])
        g_out[...] = g
        d_out[...] = delta
        m_out[...] = m_new
        v_out[...] = v_new

    spec = pl.BlockSpec((tr, cols), lambda i: (i, 0))
    if parts.ndim == 3:
        p_spec = pl.BlockSpec((NDEV, tr, cols), lambda i: (0, i, 0))
    else:
        p_spec = pl.BlockSpec((NDEV, None, tr, cols), lambda i: (0, sel, i, 0))
    return pl.pallas_call(
        body, name=name,
        grid=(rows // tr,),
        in_specs=[p_spec, spec, spec, spec],
        out_specs=[spec] * 4,
        out_shape=[jax.ShapeDtypeStruct((rows, cols), F32)] * 4,
        compiler_params=pltpu.CompilerParams(dimension_semantics=("parallel",)),
    )(parts, w, m, v)


def _adamw_plain(g, w, m, v, *, name):
    def body(g_ref, w_ref, m_ref, v_ref, d_out, m_out, v_out):
        delta, m_new, v_new = _adamw_math(w_ref[...], g_ref[...], m_ref[...], v_ref[...])
        d_out[...] = delta
        m_out[...] = m_new
        v_out[...] = v_new

    return pl.pallas_call(body, name=name, out_shape=[jax.ShapeDtypeStruct(w.shape, F32)] * 3)(g, w, m, v)


def _pack_rows(arrays, width, row_align):
    pieces, spans, at = [], [], 0
    for arr in arrays:
        flat = arr.reshape(-1)
        rows = -(-flat.shape[0] // (width * row_align)) * row_align
        flat = jnp.pad(flat, (0, rows * width - flat.shape[0]))
        pieces.append(flat.reshape(rows, width))
        spans.append((at, rows))
        at += rows
    return jnp.concatenate(pieces, axis=0), spans


def _unpack(rows2d, span, shape):
    at, rows = span
    size = 1
    for s in shape:
        size *= s
    return rows2d[at:at + rows].reshape(-1)[:size].reshape(shape)


def _join_cols(blocks):
    n, rows, cols = blocks.shape
    return jnp.transpose(blocks, (1, 0, 2)).reshape(rows, n * cols)


def _split_cols(full):
    rows, cols = full.shape
    return jnp.transpose(full.reshape(rows, NDEV, cols // NDEV), (1, 0, 2))


def _ffn_up(h, gain, w_gu, after, *, tm, tag):
    lp, d = h.shape
    f8 = w_gu.shape[3]
    n = _norm_fwd(h, gain, tm=_pick(lp, [256, 128]), name=f"{tag}_norm")
    hid = _spec((None, tm, f8), lambda i, j, k: (j, i, 0))
    a, b, s = _mm((lp // tm, NDEV, 1), n, _spec((tm, d), lambda i, j, k: (i, 0)),
                  [(w_gu, _spec((None, None, d, f8), lambda i, j, k: (j, 0, 0, 0))),
                   (w_gu, _spec((None, None, d, f8), lambda i, j, k: (j, 1, 0, 0)))],
                  [], [((NDEV, lp, f8), BF16, hid)] * 3, dims=NN, epi=_swiglu_epi, after=after, name=f"{tag}_up")
    return n, a, b, s


def _ffn_down(h, s, w_down, *, tm, tag):
    lp, d = h.shape
    f8 = w_down.shape[1]
    tn = _pick(d, [1024, 512, 256, 128])
    tile = _spec((tm, tn), lambda i, j, k: (i, j))
    return _mm((lp // tm, d // tn, NDEV // KSUB), s, _spec((KSUB, tm, f8), lambda i, j, k: (k, i, 0)),
               [(w_down, _spec((KSUB, f8, tn), lambda i, j, k: (k, 0, j)))], [(h, tile)], [((lp, d), F32, tile)],
               dims=NN, acc_shape=(tm, tn), epi=lambda accs, ex: [ex[0] + FFN_RESIDUAL_WEIGHT * accs[0]], name=f"{tag}_down")[0]


def _ffn_bwd_dw(dh_half, saved, w_down, on_down, *, tm, tag):
    n, a, b, s = saved
    lp, d = dh_half.shape
    f8 = w_down.shape[1]
    tkr = _pick(lp, [4352, 2176, 1088, 544, 256, 128])
    tn = _pick(d, [1024, 512, 256, 128])
    hid = _spec((None, tm, f8), lambda i, j, k: (j, i, 0))
    dab, = _mm((lp // tm, NDEV, 1), dh_half, _spec((tm, d), lambda i, j, k: (i, 0)),
               [(w_down, _spec((None, f8, d), lambda i, j, k: (j, 0, 0)))], [(a, hid), (b, hid)],
               [((2, NDEV, lp, f8), BF16, _spec((2, None, tm, f8), lambda i, j, k: (0, j, i, 0)))],
               dims=NT, epi=_swiglu_bwd_epi, name=f"{tag}_down_dx")
    dw_down, = _mm((NDEV, d // tn, lp // tkr), s, _spec((None, tkr, f8), lambda i, j, k: (i, k, 0)),
                   [(dh_half, _spec((tkr, tn), lambda i, j, k: (k, j)))], [],
                   [((NDEV, f8, d), BF16, _spec((None, f8, tn), lambda i, j, k: (i, 0, j)))],
                   dims=TN, acc_shape=(f8, tn), name=f"{tag}_down_dw")
    dw_gu, = _mm((d // tn, 2 * NDEV, lp // tkr), n, _spec((tkr, tn), lambda i, j, k: (k, i)),
                 [(dab, _spec((None, None, tkr, f8), lambda i, j, k: (j // NDEV, j % NDEV, k, 0)))], [],
                 [((NDEV, 2, d, f8), BF16, _spec((None, None, tn, f8), lambda i, j, k: (j % NDEV, j // NDEV, i, 0)))],
                 dims=TN, acc_shape=(tn, f8), after=on_down(dw_down), name=f"{tag}_gate_up_dw")
    return dab, dw_gu, dw_down


def _ffn_bwd_dx(dab, w_gu, after, *, tm, tag):
    lp, f8 = dab.shape[2], dab.shape[3]
    d = w_gu.shape[2]
    tn = _pick(d, [1024, 512, 256, 128])
    nsub = NDEV // KSUB
    return _mm((lp // tm, d // tn, 2 * nsub), dab, _spec((None, KSUB, tm, f8), lambda i, j, k: (k // nsub, k % nsub, i, 0)),
               [(w_gu, _spec((KSUB, None, tn, f8), lambda i, j, k: (k % nsub, k // nsub, j, 0)))], [],
               [((lp, d), F32, _spec((tm, tn), lambda i, j, k: (i, j)))],
               dims=NT, acc_shape=(tm, tn), after=after, name=f"{tag}_gate_up_dx")[0]


def kernel(x, meta_tokens, ffn1_norm, ffn1_w_gate, ffn1_w_up, ffn1_w_down, mix_norm, w_in, b_forget, fox_q_norm, fox_k_norm, w_branch_fox, w_branch_sb, w_out, ffn2_norm, ffn2_w_gate, ffn2_w_up, ffn2_w_down, loss_target, m_meta_tokens, m_ffn1_norm, m_ffn1_w_gate, m_ffn1_w_up, m_ffn1_w_down, m_mix_norm, m_w_in, m_b_forget, m_fox_q_norm, m_fox_k_norm, m_w_branch_fox, m_w_branch_sb, m_w_out, m_ffn2_norm, m_ffn2_w_gate, m_ffn2_w_up, m_ffn2_w_down, v_meta_tokens, v_ffn1_norm, v_ffn1_w_gate, v_ffn1_w_up, v_ffn1_w_down, v_mix_norm, v_w_in, v_b_forget, v_fox_q_norm, v_fox_k_norm, v_w_branch_fox, v_w_branch_sb, v_w_out, v_ffn2_norm, v_ffn2_w_gate, v_ffn2_w_up, v_ffn2_w_down):
    seq, d = x.shape[1], x.shape[2]
    d8 = d // NDEV
    w = BRANCH_WIDTH
    tile = 256 if seq % 256 == 0 else 128
    pad = tile - N_META
    lp = tile + seq
    tm = _pick(lp, [1088, 544, 256, 128])
    tr = _pick(tile, [256, 128])
    tkr = _pick(lp, [4352, 2176, 1088, 544, 256, 128])
    nb = lp // tile
    me = _flat_id(*_my_place())

    shards = [jnp.stack([ffn1_w_gate[0], ffn1_w_up[0]]).astype(BF16), ffn1_w_down[0].astype(BF16), w_in[0].astype(BF16),
              w_branch_fox[0].astype(BF16), w_branch_sb[0].astype(BF16), w_out[0].astype(BF16),
              jnp.stack([ffn2_w_gate[0], ffn2_w_up[0]]).astype(BF16), ffn2_w_down[0].astype(BF16)]
    w_gu1, = _allgather(shards[:1], "gather_ffn1")
    down1_copies, token = _send_start(shards[1:2], [_landing(s, me) for s in shards[1:2]], "first", w_gu1, "gather_down1_start")
    w_in_copies, token = _send_start(shards[2:3], [_landing(s, me) for s in shards[2:3]], "first", token, "gather_w_in_start")
    mixer_copies, token = _send_start(shards[3:6], [_landing(s, me) for s in shards[3:6]], "gather", token, "gather_mixer_start")
    ffn2_copies, token = _send_start(shards[6:], [_landing(s, me) for s in shards[6:]], "gather", token, "gather_ffn2_start")
    meta_full = _join_cols(_allgather([meta_tokens], "gather_meta", in_vmem=True)[0])

    h0 = jnp.concatenate([jnp.zeros((pad, d), F32), meta_full.astype(F32), x[0]], axis=0)
    saved1 = _ffn_up(h0, ffn1_norm, w_gu1, token, tm=tm, tag="ffn1")
    w_down1, = _gather_two_level(down1_copies, saved1[3], "gather_down1")
    h1 = _ffn_down(h0, saved1[3], w_down1, tm=tm, tag="ffn1")

    w_in_blocks, = _gather_two_level(w_in_copies, h1, "gather_w_in")
    wi = _join_cols(w_in_blocks)
    w_pa = jnp.concatenate([wi[:, :2 * w], jnp.pad(wi[:, 3 * w:3 * w + HEADS], ((0, 0), (0, LANES - HEADS)))], axis=1)
    w_pb = jnp.concatenate([wi[:, 2 * w:3 * w], wi[:, 3 * w + HEADS:]], axis=1)
    na, nbw = w_pa.shape[1], w_pb.shape[1]
    gate_blk = 4 * w // d

    n2 = _norm_fwd(h1, mix_norm, tm=tr, name="mix_norm")
    tma = _pick(lp, [544, 256, 128])
    tnd = _pick(d, [1024, 512, 256, 128])
    tnb = _pick(nbw, [512, 256, 128])
    proj_a = _mm2d(n2, w_pa, na, dims=NN, tm=tma, tn=na, tk=d, out_dtype=F32, name="proj_a")
    proj_b = _mm2d(n2, w_pb, nbw, dims=NN, tm=tm, tn=tnb, tk=d, out_dtype=BF16, name="proj_b")
    b_pad = jnp.pad(b_forget, ((0, 0), (0, LANES - HEADS)))
    q_gain, k_gain = fox_q_norm.reshape(1, w), fox_k_norm.reshape(1, w)
    fq, fk, logf = _fox_prep(proj_a, q_gain, k_gain, b_pad, tm=tr, pad=pad, name="fox_prep")
    c = _cumsum_rows(logf, tile=tile, reverse=False, name="forget_cumsum")
    c_rows = jnp.transpose(c[:, :HEADS]).reshape(HEADS, nb, 1, tile)
    o_fox, lse = _fox_fwd(fq, fk, proj_b, 0, c_rows, tile=tile, pad=pad, name="fox_fwd")
    o_sb, sb_total = _sb_fwd(proj_b, HEADS, 2 * HEADS, 3 * HEADS, tile=tile, pad=pad, name="sb_fwd")
    w_br_fox, w_br_sb, w_out_blocks = _send_wait(mixer_copies, o_sb, "gather_mixer_wait")
    w_out_full = w_out_blocks.reshape(d, d)

    def branch(o, w_blocks, name):
        return _mm((lp // tm, NDEV, 1), o, _spec((tm, w), lambda i, j, k: (i, 0)),
                   [(w_blocks, _spec((None, w, d8), lambda i, j, k: (j, 0, 0)))], [],
                   [((lp, d), BF16, _spec((tm, d8), lambda i, j, k: (i, j)))], dims=NN, name=name)[0]

    br_fox = branch(o_fox, w_br_fox, "branch_fox")
    br_sb = branch(o_sb, w_br_sb, "branch_sb")

    def merge_fn(i, tiles, consts):
        bf_, bs_, gf_, gs_ = [t.astype(F32) for t in tiles]
        return [_sigmoid(gf_) * bf_ + _sigmoid(gs_) * bs_], []

    gates_in = [(proj_b, d, gate_blk), (proj_b, d, gate_blk + 1)]
    merged, = _rowwise(merge_fn, [_whole(br_fox), _whole(br_sb)] + gates_in, [], [(d, BF16)], [], tm=tr, name="merge")
    h2 = _mm2d(merged, w_out_full, d, dims=NN, tm=tm, tn=tnd, tk=d, out_dtype=F32,
               epi=lambda accs, ex: [ex[0] + accs[0]], extras=[h1], name="out_proj")

    w_gu2, w_down2 = _send_wait(ffn2_copies, h2, "gather_ffn2_wait")
    saved3 = _ffn_up(h2, ffn2_norm, w_gu2, None, tm=tm, tag="ffn2")
    h3 = _ffn_down(h2, saved3[3], w_down2, tm=tm, tag="ffn2")

    skip = tile // tr

    def loss_fn(i, tiles, consts):
        real = i >= skip
        err = jnp.where(real, tiles[0] - tiles[1], 0.0)
        dy = err * (1.0 / d)
        part = 0.5 * jnp.sum(err * dy, axis=0, keepdims=True)
        return [dy, FFN_RESIDUAL_WEIGHT * dy], [part]

    dh3, dh3_half, loss_cols = _rowwise(
        loss_fn, [_whole(h3), _whole(loss_target[0])], [], [(d, F32, lp), (d, BF16, lp)], [(1, d)], tm=tr, name="loss",
        row_maps=[None, lambda i: jnp.maximum(i - skip, 0)])

    def own(g):
        return lax.dynamic_index_in_dim(g, me, 0, keepdims=False)

    def send_grads(grads_, name):
        return _send_start(grads_, [_landing(own(g), me) for g in grads_], "exchange", grads_[-1], name)

    sends = {}

    def send_piece(key):
        def on_ready(*gs):
            sends[key], token_ = send_grads(list(gs), f"exchange_{key}_start")
            return token_
        return on_ready

    dab3, dw_gu2, _ = _ffn_bwd_dw(dh3_half, saved3, w_down2, send_piece("down2"), tm=tm, tag="ffn2")
    dn3 = _ffn_bwd_dx(dab3, w_gu2, send_piece("gu2")(dw_gu2), tm=tm, tag="ffn2")
    dh2, dh2_bf, dg_ffn2 = _norm_bwd(dn3, h2, ffn2_norm, dh3, 1.0, tm=tr, name="ffn2_norm_bwd")

    dmerged = _mm2d(dh2_bf, w_out_full, d, dims=NT, tm=tm, tn=tnd, tk=d, out_dtype=BF16, name="out_proj_dx")
    dw_out = _mm2d(merged, dh2_bf, d, dims=TN, tm=tnd, tn=tnd, tk=tkr, out_dtype=BF16, name="out_proj_dw")
    token = send_piece("out")(dw_out.reshape(NDEV, d8, d))

    def merge_bwd_fn(i, tiles, consts):
        dm, bf_, bs_, gf_, gs_ = [t.astype(F32) for t in tiles]
        sf, ss = _sigmoid(gf_), _sigmoid(gs_)
        return [dm * sf, dm * ss, dm * bf_ * sf * (1.0 - sf), dm * bs_ * ss * (1.0 - ss)], []

    dbr_fox, dbr_sb, dg_fox, dg_sb = _rowwise(
        merge_bwd_fn, [_whole(dmerged), _whole(br_fox), _whole(br_sb)] + gates_in, [], [(d, BF16)] * 4, [], tm=tr, after=token,
        name="merge_bwd")

    tnw = _pick(w, [512, 256, 128])

    def branch_dx(dbr, w_blocks, after, name):
        return _mm((lp // tm, w // tnw, NDEV), dbr, _spec((tm, d8), lambda i, j, k: (i, k)),
                   [(w_blocks, _spec((None, tnw, d8), lambda i, j, k: (k, j, 0)))], [],
                   [((lp, w), BF16, _spec((tm, tnw), lambda i, j, k: (i, j)))], dims=NT, acc_shape=(tm, tnw), after=after,
                   name=name)[0]

    def branch_dw(o, dbr, name):
        return _mm((w // tnw, NDEV, lp // tkr), o, _spec((tkr, tnw), lambda i, j, k: (k, i)),
                   [(dbr, _spec((tkr, d8), lambda i, j, k: (k, j)))], [],
                   [((NDEV, w, d8), BF16, _spec((None, tnw, d8), lambda i, j, k: (j, i, 0)))],
                   dims=TN, acc_shape=(tnw, d8), name=name)[0]

    token = send_piece("branch")(branch_dw(o_fox, dbr_fox, "branch_fox_dw"), branch_dw(o_sb, dbr_sb, "branch_sb_dw"))
    do_fox = branch_dx(dbr_fox, w_br_fox, token, "branch_fox_dx")
    do_sb = branch_dx(dbr_sb, w_br_sb, token, "branch_sb_dx")

    dfq, dfk, dfv, dc_rows = _fox_bwd(fq, fk, proj_b, 0, c_rows, o_fox, do_fox, lse, tile=tile, pad=pad, name="fox_bwd")
    dsq, dsk, dsv = _sb_bwd(proj_b, HEADS, 2 * HEADS, 3 * HEADS, do_sb, sb_total, tile=tile, pad=pad, name="sb_bwd")
    dc = jnp.pad(jnp.transpose(dc_rows.reshape(HEADS, lp)), ((0, 0), (0, LANES - HEADS)))
    dlogf = _cumsum_rows(dc, tile=tile, reverse=True, name="forget_cumsum_bwd")
    dproj_a, dg_q, dg_k, dg_b = _fox_prep_bwd(proj_a, dfq, dfk, dlogf, q_gain, k_gain, b_pad, tm=tr, pad=pad, name="fox_prep_bwd")
    dproj_b = jnp.concatenate([dfv, dsq, dsk, dsv, dg_fox, dg_sb], axis=1)

    dw_pa = _mm2d(n2, dproj_a, na, dims=TN, tm=tnd, tn=na, tk=_pick(lp, [544, 256, 128]), out_dtype=BF16, name="proj_a_dw")
    dw_pb = _mm2d(n2, dproj_b, nbw, dims=TN, tm=tnd, tn=tnb, tk=tkr, out_dtype=BF16, name="proj_b_dw")
    dw_in = jnp.concatenate([dw_pa[:, :2 * w], dw_pb[:, :w], dw_pa[:, 2 * w:2 * w + HEADS], dw_pb[:, w:]], axis=1)
    token = send_piece("w_in")(_split_cols(dw_in))
    dn2_a = _mm2d(dproj_a, w_pa, d, dims=NT, tm=tm, tn=tnd, tk=na, out_dtype=F32, after=token, name="proj_a_dx")
    dn2 = _mm2d(dproj_b, w_pb, d, dims=NT, tm=tm, tn=tnd, tk=_pick(nbw, [2048, 1024, 512, 256, 128]), out_dtype=F32,
                epi=lambda accs, ex: [accs[0] + ex[0]], extras=[dn2_a], name="proj_b_dx")
    dh1, dh1_half, dg_mix = _norm_bwd(dn2, h1, mix_norm, dh2, FFN_RESIDUAL_WEIGHT, tm=tr, name="mix_norm_bwd")

    dab1, dw_gu1, _ = _ffn_bwd_dw(dh1_half, saved1, w_down1, send_piece("down1"), tm=tm, tag="ffn1")
    dn1 = _ffn_bwd_dx(dab1, w_gu1, send_piece("gu1")(dw_gu1), tm=tm, tag="ffn1")
    dh0, _, dg_ffn1_tokens = _norm_bwd(dn1, h0, ffn1_norm, dh1, None, tm=tile, name="ffn1_norm_bwd", first=1, count=seq // tile)
    dh0_meta, _, dg_ffn1_meta = _norm_bwd(dn1, h0, ffn1_norm, dh1, None, tm=tile, name="ffn1_norm_bwd_meta", count=1)
    dg_ffn1 = dg_ffn1_tokens + dg_ffn1_meta
    grad_x = dh0[None]

    r_down2, = _send_wait(sends["down2"], dh0, "exchange_down2_wait")
    r_gu2, = _send_wait(sends["gu2"], r_down2, "exchange_gu2_wait")
    r_out, = _send_wait(sends["out"], r_gu2, "exchange_out_wait")
    r_br_fox, r_br_sb = _send_wait(sends["branch"], r_out, "exchange_branch_wait")
    r_in, = _send_wait(sends["w_in"], r_br_sb, "exchange_w_in_wait")
    grads, deltas, new_ms, new_vs = {}, {}, {}, {}

    def adamw_big(entries):
        for k, (parts, sel, wt, mt, vt) in entries.items():
            g, dl, mn, vn = _adamw_summed(parts, sel, wt[0], mt[0], vt[0], name=f"adamw_{k}")
            grads[k], deltas[k], new_ms[k], new_vs[k] = g[None], dl[None], mn[None], vn[None]

    adamw_big(dict(ffn2_w_gate=(r_gu2, 0, ffn2_w_gate, m_ffn2_w_gate, v_ffn2_w_gate),
                   ffn2_w_up=(r_gu2, 1, ffn2_w_up, m_ffn2_w_up, v_ffn2_w_up),
                   ffn2_w_down=(r_down2, 0, ffn2_w_down, m_ffn2_w_down, v_ffn2_w_down),
                   w_in=(r_in, 0, w_in, m_w_in, v_w_in),
                   w_branch_fox=(r_br_fox, 0, w_branch_fox, m_w_branch_fox, v_w_branch_fox),
                   w_branch_sb=(r_br_sb, 0, w_branch_sb, m_w_branch_sb, v_w_branch_sb),
                   w_out=(r_out, 0, w_out, m_w_out, v_w_out)))
    done = sum(v[0, 0, :1] for v in new_vs.values())
    r_down1, = _send_wait(sends["down1"], done, "exchange_down1_wait")
    r_gu1, = _send_wait(sends["gu1"], r_down1, "exchange_gu1_wait")
    adamw_big(dict(ffn1_w_gate=(r_gu1, 0, ffn1_w_gate, m_ffn1_w_gate, v_ffn1_w_gate),
                   ffn1_w_up=(r_gu1, 1, ffn1_w_up, m_ffn1_w_up, v_ffn1_w_up),
                   ffn1_w_down=(r_down1, 0, ffn1_w_down, m_ffn1_w_down, v_ffn1_w_down)))

    small_parts = [dh0_meta[pad:tile], dg_ffn1, dg_mix, dg_ffn2, dg_b[:, :HEADS], dg_q, dg_k, loss_cols]
    small_packed, small_spans = _pack_rows(small_parts, LANES, SMALL_ROWS)
    small_sum = _allsum_small(small_packed, "sum_small")
    g_meta_full, g_ffn1n, g_mixn, g_ffn2n, g_bf, g_qn, g_kn, loss_vec = [
        _unpack(small_sum, span, part.shape) for span, part in zip(small_spans, small_parts)]
    loss = jnp.sum(loss_vec)
    g_meta = lax.dynamic_slice_in_dim(g_meta_full, me * d8, d8, axis=1)
    g_qn, g_kn = g_qn.reshape(fox_q_norm.shape), g_kn.reshape(fox_k_norm.shape)

    small = dict(meta_tokens=(g_meta, meta_tokens, m_meta_tokens, v_meta_tokens),
                 ffn1_norm=(g_ffn1n, ffn1_norm, m_ffn1_norm, v_ffn1_norm),
                 mix_norm=(g_mixn, mix_norm, m_mix_norm, v_mix_norm),
                 b_forget=(g_bf, b_forget, m_b_forget, v_b_forget),
                 fox_q_norm=(g_qn, fox_q_norm, m_fox_q_norm, v_fox_q_norm),
                 fox_k_norm=(g_kn, fox_k_norm, m_fox_k_norm, v_fox_k_norm),
                 ffn2_norm=(g_ffn2n, ffn2_norm, m_ffn2_norm, v_ffn2_norm))
    for k, (g, wt, mt, vt) in small.items():
        flat = lambda t: t.reshape(-1, t.shape[-1])
        dl, mn, vn = _adamw_plain(flat(g), flat(wt), flat(mt), flat(vt), name=f"adamw_{k}")
        grads[k], deltas[k], new_ms[k], new_vs[k] = g, dl.reshape(wt.shape), mn.reshape(wt.shape), vn.reshape(wt.shape)

    order = ["meta_tokens", "ffn1_norm", "ffn1_w_gate", "ffn1_w_up", "ffn1_w_down", "mix_norm", "w_in", "b_forget",
             "fox_q_norm", "fox_k_norm", "w_branch_fox", "w_branch_sb", "w_out", "ffn2_norm", "ffn2_w_gate",
             "ffn2_w_up", "ffn2_w_down"]
    return (loss, grad_x, *[grads[k] for k in order], *[deltas[k] for k in order],
            *[new_ms[k] for k in order], *[new_vs[k] for k in order])
```

```python
import jax
import jax.numpy as jnp
from jax import lax
from jax.experimental import pallas as pl
from jax.experimental.pallas import tpu as pltpu

F32 = jnp.float32
BF16 = jnp.bfloat16
MESH = pl.DeviceIdType.MESH

NDEV = 8
N_META = 16
HEAD_DIM = 128
HEADS = 8
BRANCH_WIDTH = HEADS * HEAD_DIM
RMS_EPS = 1e-6
FFN_RESIDUAL_WEIGHT = 0.5
ATTN_SCALE = HEAD_DIM ** -0.5
MASKED_LOGIT = -1e30

ADAM_LR = 0.001
ADAM_B1 = 0.9
ADAM_B2 = 0.999
ADAM_EPS = 1e-08
ADAM_WD = 0.01
ADAM_STEP = 10

LANES = 128
SMALL_ROWS = 8
ADAMW_TILE_ELEMS = 160 * 1024
KSUB = 4


def _pick(n, prefs):
    for p in prefs:
        if p <= n and n % p == 0:
            return p
    return n


def _dot(a, b, dims):
    return lax.dot_general(a, b, (dims, ((), ())), preferred_element_type=F32)


NN = ((1,), (0,))
TN = ((0,), (0,))
NT = ((1,), (1,))


def _split_bf16(x):
    hi = x.astype(BF16)
    lo = (x - hi.astype(F32)).astype(BF16)
    return hi, lo


def _sigmoid_parts(z):
    e = jnp.exp(-jnp.abs(z))
    t = 1.0 + e
    return e, 1.0 / t, jnp.log(t)


def _sigmoid(x):
    return 0.5 * jnp.tanh(0.5 * x) + 0.5


def _my_place():
    return lax.axis_index("x"), lax.axis_index("y"), lax.axis_index("c")


def _flat_id(px, py, pc):
    return 4 * px + 2 * py + pc


def _peer(x, y, c, k):
    px = 1 - x if k & 4 else x
    py = 1 - y if k & 2 else y
    pc = 1 - c if k & 1 else c
    return px, py, pc


def _allgather(shards, name, in_vmem=False):
    n = len(shards)

    def body(*refs):
        x_refs, out_refs = refs[:n], refs[n:2 * n]
        send_sems, recv_sems, local_sems = refs[2 * n:]
        x, y, c = _my_place()
        me, sibling = (x, y, c), (x, y, 1 - c)
        chips = [(1 - x, y), (x, 1 - y), (1 - x, 1 - y)]

        def block(a, place):
            return out_refs[a].at[_flat_id(*place)]

        def copy(a, k, place, to, src=None):
            return pltpu.make_async_remote_copy(
                src_ref=block(a, place) if src is None else src, dst_ref=block(a, place),
                send_sem=send_sems.at[7 * a + k], recv_sem=recv_sems.at[7 * a + k], device_id=to, device_id_type=MESH)

        mine = [pltpu.make_async_copy(x_refs[a], block(a, me), local_sems.at[a]) for a in range(n)]
        for cp in mine:
            cp.start()
        first = []
        for a in range(n):
            first.append(copy(a, 0, me, sibling, src=x_refs[a]))
            first += [copy(a, 1 + j, me, (*chip, c), src=x_refs[a]) for j, chip in enumerate(chips)]
        for cp in first:
            cp.start()
        passed = []
        for j, chip in enumerate(chips):
            for a in range(n):
                copy(a, 1 + j, (*chip, c), me).wait_recv()
                passed.append(copy(a, 4 + j, (*chip, c), sibling))
                passed[-1].start()
        for a in range(n):
            copy(a, 0, sibling, me).wait_recv()
        for j, chip in enumerate(chips):
            for a in range(n):
                copy(a, 4 + j, (*chip, 1 - c), me).wait_recv()
        for cp in first + passed:
            cp.wait_send()
        for cp in mine:
            cp.wait()

    space = pltpu.VMEM if in_vmem else pl.ANY
    return pl.pallas_call(
        body, name=name,
        out_shape=[jax.ShapeDtypeStruct((NDEV,) + s.shape, s.dtype) for s in shards],
        in_specs=[pl.BlockSpec(memory_space=space)] * n,
        out_specs=[pl.BlockSpec(memory_space=space)] * n,
        scratch_shapes=[pltpu.SemaphoreType.DMA((7 * n,)), pltpu.SemaphoreType.DMA((7 * n,)), pltpu.SemaphoreType.DMA((n,))],
    )(*shards)


def _allsum_small(part, name):
    rows, cols = part.shape

    def body(p_ref, out_ref, buf, send_sems, recv_sems):
        x, y, c = _my_place()
        me = _flat_id(x, y, c)
        buf[me] = p_ref[...]
        copies = []
        for k in range(1, NDEV):
            copies.append(pltpu.make_async_remote_copy(
                src_ref=p_ref, dst_ref=buf.at[me], send_sem=send_sems.at[k - 1], recv_sem=recv_sems.at[k - 1],
                device_id=_peer(x, y, c, k), device_id_type=MESH))
        for cp in copies:
            cp.start()
        for cp in copies:
            cp.wait()
        total = buf[0]
        for j in range(1, NDEV):
            total = total + buf[j]
        out_ref[...] = total

    return pl.pallas_call(
        body, name=name,
        out_shape=jax.ShapeDtypeStruct((rows, cols), F32),
        in_specs=[pl.BlockSpec(memory_space=pltpu.VMEM)],
        out_specs=pl.BlockSpec(memory_space=pltpu.VMEM),
        scratch_shapes=[pltpu.VMEM((NDEV, rows, cols), F32),
                        pltpu.SemaphoreType.DMA((7,)), pltpu.SemaphoreType.DMA((7,))],
    )(part)


_HBM = pl.BlockSpec(memory_space=pltpu.HBM)
_SEM = pl.BlockSpec(memory_space=pltpu.SEMAPHORE)
_DATAFLOW = pltpu.SideEffectType.DATAFLOW_SIDE_EFFECTING


COPIES_PER_ARRAY = {"gather": 7, "exchange": 7, "first": 4, "forward": 3}


def _send_copies(plan, src_refs, land_refs, send_sems, recv_sems):
    x, y, c = _my_place()
    me = _flat_id(x, y, c)
    per = COPIES_PER_ARRAY[plan]
    copies = []

    def add(a, slot, src, dst, to):
        copies.append(pltpu.make_async_remote_copy(
            src_ref=src, dst_ref=dst, send_sem=send_sems.at[per * a + slot], recv_sem=recv_sems.at[per * a + slot],
            device_id=to, device_id_type=MESH))

    for a, land_ref in enumerate(land_refs):
        if plan in ("gather", "exchange"):
            for k in range(1, NDEV):
                peer = _peer(x, y, c, k)
                add(a, k - 1, src_refs[a].at[_flat_id(*peer)] if plan == "exchange" else src_refs[a], land_ref.at[me], peer)
        elif plan == "first":
            for slot, k in enumerate((1, 2, 4, 6)):
                add(a, slot, src_refs[a], land_ref.at[me], _peer(x, y, c, k))
        else:
            for slot, k in enumerate((2, 4, 6)):
                block = land_ref.at[_flat_id(*_peer(x, y, c, k))]
                add(a, slot, block, block, (x, y, 1 - c))
    return copies


def _send_start(srcs, lands, plan, after, name):
    ns, nl = len(srcs), len(lands)
    nsem = COPIES_PER_ARRAY[plan] * nl

    def body(*refs):
        for cp in _send_copies(plan, refs[:ns], refs[ns:ns + nl], refs[ns + nl + 1], refs[ns + nl + 2]):
            cp.start()
        refs[-1][...] = jnp.zeros_like(refs[-1])

    operands = [pltpu.with_memory_space_constraint(t, pltpu.HBM) for t in list(srcs) + list(lands)]
    outs = pl.pallas_call(
        body, name=name,
        out_shape=(pltpu.SemaphoreType.DMA((nsem,)), pltpu.SemaphoreType.DMA((nsem,)),
                   *[pltpu.HBM(t.shape, t.dtype) for t in operands[ns:]], jax.ShapeDtypeStruct((SMALL_ROWS, LANES), F32)),
        in_specs=[_HBM] * (ns + nl) + [pl.BlockSpec(memory_space=pl.ANY)],
        out_specs=(_SEM, _SEM, *[_HBM] * nl, pl.BlockSpec(memory_space=pltpu.VMEM)),
        input_output_aliases={ns + i: 2 + i for i in range(nl)},
        compiler_params=pltpu.CompilerParams(has_side_effects=_DATAFLOW),
    )(*operands, after)
    return (plan, outs[0], outs[1], operands[:ns], list(outs[2:2 + nl])), outs[-1]


def _send_wait(handle, after, name):
    plan, send_sems, recv_sems, srcs, lands = handle
    ns, nl = len(srcs), len(lands)

    def body(*refs):
        for cp in _send_copies(plan, refs[:ns], refs[ns:ns + nl], refs[ns + nl], refs[ns + nl + 1]):
            cp.wait_send()
            cp.wait_recv()

    outs = pl.pallas_call(
        body, name=name,
        out_shape=tuple(pltpu.HBM(t.shape, t.dtype) for t in lands),
        in_specs=[_HBM] * (ns + nl) + [_SEM, _SEM, pl.BlockSpec(memory_space=pl.ANY)],
        out_specs=tuple([_HBM] * nl),
        input_output_aliases={ns + i: i for i in range(nl)},
        compiler_params=pltpu.CompilerParams(has_side_effects=_DATAFLOW),
    )(*srcs, *lands, send_sems, recv_sems, after)
    return list(outs)


def _gather_two_level(first_handle, after, name):
    lands = _send_wait(first_handle, after, f"{name}_wait")
    forward, _ = _send_start([], lands, "forward", after, f"{name}_forward_start")
    return _send_wait(forward, after, f"{name}_forward_wait")


def _landing(own_block, me):
    return lax.dynamic_update_index_in_dim(lax.empty((NDEV,) + own_block.shape, own_block.dtype), own_block, me, 0)


def _spec(block, index_map):
    return pl.BlockSpec(block, index_map)


def _mm(grid, a, a_spec, bs, extras, outs, *, dims, acc_shape=None, epi=None, after=None, name):
    nk = grid[2]
    nb, ne, no = len(bs), len(extras), len(outs)
    nafter = 0 if after is None else 1
    if epi is None:
        epi = lambda accs, ex: [accs[0]]

    def body(*refs):
        a_ref, b_refs = refs[0], refs[1:1 + nb]
        e_refs = refs[1 + nb:1 + nb + ne]
        o_refs = refs[1 + nb + ne + nafter:1 + nb + ne + nafter + no]
        acc_refs = refs[1 + nb + ne + nafter + no:]
        def finish(accs):
            for o_ref, tile in zip(o_refs, epi(accs, [e_ref[...] for e_ref in e_refs])):
                o_ref[...] = tile.astype(o_ref.dtype)

        def product(b_ref):
            if len(a_ref.shape) == 2:
                return _dot(a_ref[...], b_ref[...], dims)
            total = _dot(a_ref[0], b_ref[0], dims)
            for t in range(1, a_ref.shape[0]):
                total = total + _dot(a_ref[t], b_ref[t], dims)
            return total

        if nk == 1:
            finish([product(b_ref) for b_ref in b_refs])
        else:
            k = pl.program_id(2)

            @pl.when(k == 0)
            def _():
                for acc_ref in acc_refs:
                    acc_ref[...] = jnp.zeros_like(acc_ref)

            for acc_ref, b_ref in zip(acc_refs, b_refs):
                acc_ref[...] += product(b_ref)

            @pl.when(k == nk - 1)
            def _():
                finish([acc_ref[...] for acc_ref in acc_refs])

    return pl.pallas_call(
        body, name=name,
        grid=grid,
        in_specs=[a_spec] + [s for _, s in bs] + [s for _, s in extras] + [pl.BlockSpec(memory_space=pl.ANY)] * nafter,
        out_specs=[s for _, _, s in outs],
        out_shape=[jax.ShapeDtypeStruct(shape, dt) for shape, dt, _ in outs],
        scratch_shapes=[pltpu.VMEM(acc_shape, F32) for _ in bs] if nk > 1 else [],
        compiler_params=pltpu.CompilerParams(dimension_semantics=("parallel", "parallel", "arbitrary")),
    )(a, *[b for b, _ in bs], *[e for e, _ in extras], *([after] if nafter else []))


def _mm2d(a, b, n_cols, *, dims, tm, tn, tk, out_dtype, epi=None, extras=(), after=None, name):
    m_rows, k_len = (a.shape[1], a.shape[0]) if dims == TN else a.shape
    assert m_rows % tm == 0 and n_cols % tn == 0 and k_len % tk == 0, (name, a.shape, n_cols, tm, tn, tk)
    a_spec = _spec((tk, tm), lambda i, j, k: (k, i)) if dims == TN else _spec((tm, tk), lambda i, j, k: (i, k))
    b_spec = _spec((tn, tk), lambda i, j, k: (j, k)) if dims == NT else _spec((tk, tn), lambda i, j, k: (k, j))
    tile = _spec((tm, tn), lambda i, j, k: (i, j))
    return _mm((m_rows // tm, n_cols // tn, k_len // tk), a, a_spec, [(b, b_spec)], [(e, tile) for e in extras],
               [((m_rows, n_cols), out_dtype, tile)], dims=dims, acc_shape=(tm, tn), epi=epi, after=after, name=name)[0]


def _rowwise(fn, ins, consts, outs, sums, *, tm, name, row_maps=None, after=None):
    m_rows = outs[0][2] if len(outs[0]) == 3 else ins[0][0].shape[0]
    n = m_rows // tm
    ni, nc, no = len(ins), len(consts), len(outs)
    nafter = 0 if after is None else 1
    row_maps = row_maps or [None] * ni

    def body(*refs):
        i = pl.program_id(0)
        in_tiles = [r[...] for r in refs[:ni]]
        const_values = [r[...] for r in refs[ni:ni + nc]]
        o_refs = refs[ni + nc + nafter:ni + nc + nafter + no]
        s_refs = refs[ni + nc + nafter + no:]
        out_tiles, sum_terms = fn(i, in_tiles, const_values)
        for o_ref, tile in zip(o_refs, out_tiles):
            o_ref[...] = tile.astype(o_ref.dtype)
        if s_refs:
            @pl.when(i == 0)
            def _():
                for s_ref in s_refs:
                    s_ref[...] = jnp.zeros_like(s_ref)

            for s_ref, term in zip(s_refs, sum_terms):
                s_ref[...] += term

    def in_spec(width, col, rmap):
        if rmap is None:
            return pl.BlockSpec((tm, width), lambda i: (i, col))
        return pl.BlockSpec((tm, width), lambda i: (rmap(i), col))

    return pl.pallas_call(
        body, name=name,
        grid=(n,),
        in_specs=[in_spec(w, col, rmap) for (_, w, col), rmap in zip(ins, row_maps)]
        + [pl.BlockSpec(cst.shape, lambda i: (0, 0)) for cst in consts] + [pl.BlockSpec(memory_space=pl.ANY)] * nafter,
        out_specs=[pl.BlockSpec((tm, o[0]), lambda i: (i, 0)) for o in outs]
        + [pl.BlockSpec(s, lambda i: (0, 0)) for s in sums],
        out_shape=[jax.ShapeDtypeStruct((m_rows, o[0]), o[1]) for o in outs]
        + [jax.ShapeDtypeStruct(s, F32) for s in sums],
        compiler_params=pltpu.CompilerParams(dimension_semantics=("arbitrary",)),
    )(*[arr for arr, _, _ in ins], *consts, *([after] if nafter else []))


def _whole(arr):
    return (arr, arr.shape[1], 0)


def _rms(x, gain):
    r = lax.rsqrt(jnp.mean(x * x, axis=-1, keepdims=True) + RMS_EPS)
    return x * r * gain


def _rms_bwd(x, gain, dy):
    r = lax.rsqrt(jnp.mean(x * x, axis=-1, keepdims=True) + RMS_EPS)
    u = dy * gain
    dx = r * u - x * (r * r * r) * jnp.mean(x * u, axis=-1, keepdims=True)
    return dx, dy * x * r


def _norm_fwd(h, gain, *, tm, name):
    def fn(i, tiles, consts):
        return [_rms(tiles[0], consts[0])], []
    return _rowwise(fn, [_whole(h)], [gain], [(h.shape[1], BF16)], [], tm=tm, name=name)[0]


def _norm_bwd(dn, h, gain, dh_in, low_scale, *, tm, name, after=None, first=0, count=None):
    d = h.shape[1]
    rows = h.shape[0] if count is None else count * tm
    shift = lambda i: i + first

    def fn(i, tiles, consts):
        dx, dg_rows = _rms_bwd(tiles[1], consts[0], tiles[0])
        dh = tiles[2] + dx
        return [dh] + ([] if low_scale is None else [low_scale * dh]), [jnp.sum(dg_rows, axis=0, keepdims=True)]

    outs = _rowwise(fn, [_whole(dn), _whole(h), _whole(dh_in)], [gain],
                    [(d, F32, rows)] + ([] if low_scale is None else [(d, BF16, rows)]), [(1, d)], tm=tm, name=name,
                    row_maps=[shift] * 3, after=after)
    return (outs[0], None, outs[1]) if low_scale is None else tuple(outs)


def _swiglu_epi(accs, ex):
    a, b = accs
    return [a, b, a * _sigmoid(a) * b]


def _swiglu_bwd_epi(accs, ex):
    ds = accs[0]
    a, b = ex[0].astype(F32), ex[1].astype(F32)
    sig = _sigmoid(a)
    silu = a * sig
    dsilu = sig * (1.0 + a * (1.0 - sig))
    return [jnp.stack([(ds * b * dsilu).astype(BF16), (ds * silu).astype(BF16)], axis=0)]


def _attn_mask(i, j, tile, pad, strict):
    row = i * tile + lax.broadcasted_iota(jnp.int32, (tile, tile), 0)
    col = j * tile + lax.broadcasted_iota(jnp.int32, (tile, tile), 1)
    causal = (col < row) if strict else (col <= row)
    return causal & ((col >= pad) | (row < pad))


FWD_GROUP = 4
BWD_GROUP = 2


def _walk_key_tiles(i, step, carry, widths, descending=False):
    diagonal = lambda j, c: step(j, c, True, 1)
    left = jnp.maximum(i - 1, 0)
    lo = 1
    if descending:
        carry = lax.fori_loop(jnp.maximum(i, 1), i + 1, diagonal, carry)
    else:
        carry = step(0, carry, True, 1)
    for nt in widths:
        count = left // nt
        if descending:
            top = lo + left
            carry = lax.fori_loop(0, count, lambda t, c, nt=nt, top=top: step(top - nt * (t + 1), c, False, nt), carry)
        else:
            carry = lax.fori_loop(0, count, lambda t, c, nt=nt, lo=lo: step(lo + nt * t, c, False, nt), carry)
            lo = lo + nt * count
        left = left - nt * count
    if descending:
        return step(0, carry, True, 1)
    return lax.fori_loop(jnp.maximum(i, 1), i + 1, diagonal, carry)


def _head_cols(g):
    return pl.ds(g * HEAD_DIM, HEAD_DIM)


def _key_rows(j, tile, nt):
    return pl.ds(pl.multiple_of(j * tile, tile), nt * tile)


def _key_tile(ref, j, tile, g, nt=1):
    return ref[_key_rows(j, tile, nt), _head_cols(g)]


def _key_bias(c_ref, g, j, nt):
    return c_ref[g, j] if nt == 1 else jnp.concatenate([c_ref[g, j + t] for t in range(nt)], axis=1)


def _head_specs(lp, tile, q_off, k_off, v_off, hp):
    gw = hp * HEAD_DIM
    q_spec = pl.BlockSpec((tile, gw), lambda h, i: (i, h + q_off // hp))
    k_spec = pl.BlockSpec((lp, gw), lambda h, i: (0, h + k_off // hp))
    v_spec = pl.BlockSpec((lp, gw), lambda h, i: (0, h + v_off // hp))
    return q_spec, k_spec, v_spec


def _fox_fwd(q, k, v, v_off, c_rows, *, tile, pad, name):
    lp = q.shape[0]
    nb = lp // tile
    hp = FWD_GROUP
    gw = hp * HEAD_DIM

    def body(q_ref, k_ref, v_ref, c_ref, o_ref, lse_ref):
        i = pl.program_id(1)
        qts = [q_ref[:, _head_cols(g)] for g in range(hp)]

        def step(j, carry, masked, nt):
            ok = _attn_mask(i, j, tile, pad, False) if masked else None
            out = []
            for g, (m, l, acc) in enumerate(carry):
                s = _dot(qts[g], _key_tile(k_ref, j, tile, g, nt), NT) * ATTN_SCALE - _key_bias(c_ref, g, j, nt)
                if masked:
                    s = jnp.where(ok, s, MASKED_LOGIT)
                m_new = jnp.maximum(m, jnp.max(s, axis=1, keepdims=True))
                p = jnp.exp(s - m_new)
                alpha = jnp.exp(m - m_new)
                l = alpha * l + jnp.sum(p, axis=1, keepdims=True)
                acc = alpha * acc + _dot(p.astype(BF16), _key_tile(v_ref, j, tile, g, nt), NN)
                out.append((m_new, l, acc))
            return tuple(out)

        init = (jnp.full((tile, 1), MASKED_LOGIT, F32), jnp.zeros((tile, 1), F32), jnp.zeros((tile, HEAD_DIM), F32))
        final = _walk_key_tiles(i, step, (init,) * hp, (4, 2, 1))
        for g, (m, l, acc) in enumerate(final):
            o_ref[:, _head_cols(g)] = (acc / l).astype(o_ref.dtype)
            lse_ref[g] = jnp.broadcast_to(m + jnp.log(l), (tile, LANES))

    q_spec, k_spec, v_spec = _head_specs(lp, tile, 0, 0, v_off, hp)
    return pl.pallas_call(
        body, name=name,
        grid=(HEADS // hp, nb),
        in_specs=[q_spec, k_spec, v_spec, pl.BlockSpec((hp, nb, 1, tile), lambda h, i: (h, 0, 0, 0))],
        out_specs=[pl.BlockSpec((tile, gw), lambda h, i: (i, h)),
                   pl.BlockSpec((hp, tile, LANES), lambda h, i: (h, i, 0))],
        out_shape=[jax.ShapeDtypeStruct((lp, BRANCH_WIDTH), BF16), jax.ShapeDtypeStruct((HEADS, lp, LANES), F32)],
        compiler_params=pltpu.CompilerParams(dimension_semantics=("parallel", "arbitrary")),
    )(q, k, v, c_rows)


def _fox_bwd(q, k, v, v_off, c_rows, o, do, lse, *, tile, pad, name):
    lp = q.shape[0]
    nb = lp // tile
    hp = BWD_GROUP
    gw = hp * HEAD_DIM

    def body(q_ref, k_ref, v_ref, c_ref, o_ref, do_ref, lse_ref, dq_ref, dk_ref, dv_ref, dc_ref, dk_acc, dv_acc, dc_acc):
        i = pl.program_id(1)

        @pl.when(i == 0)
        def _():
            dk_acc[...] = jnp.zeros_like(dk_acc)
            dv_acc[...] = jnp.zeros_like(dv_acc)
            dc_acc[...] = jnp.zeros_like(dc_acc)

        heads = range(hp)
        qts = [q_ref[:, _head_cols(g)] for g in heads]
        dots = [do_ref[:, _head_cols(g)] for g in heads]
        deltas = [jnp.sum(dots[g].astype(F32) * o_ref[:, _head_cols(g)].astype(F32), axis=1, keepdims=True) for g in heads]
        lse_cols = [lse_ref[g][:, :1] for g in heads]

        def step(j, dqs, masked, nt):
            rows = _key_rows(j, tile, nt)
            ok = _attn_mask(i, j, tile, pad, False) if masked else None
            out = []
            for g in heads:
                kt = _key_tile(k_ref, j, tile, g, nt)
                s = _dot(qts[g], kt, NT) * ATTN_SCALE - _key_bias(c_ref, g, j, nt)
                p = jnp.exp(s - lse_cols[g])
                if masked:
                    p = jnp.where(ok, p, 0.0)
                dp = _dot(dots[g], _key_tile(v_ref, j, tile, g, nt), NT)
                ds = p * (dp - deltas[g])
                dsb = ds.astype(BF16)
                dk_acc[rows, _head_cols(g)] += _dot(dsb, qts[g], TN)
                dv_acc[rows, _head_cols(g)] += _dot(p.astype(BF16), dots[g], TN)
                dc = -jnp.sum(ds, axis=0, keepdims=True)
                for t in range(nt):
                    dc_acc[g, j + t] += dc[:, t * tile:(t + 1) * tile]
                out.append(dqs[g] + _dot(dsb, kt, NN))
            return tuple(out)

        dqs = _walk_key_tiles(i, step, (jnp.zeros((tile, HEAD_DIM), F32),) * hp, (8, 4, 2, 1))
        for g in heads:
            dq_ref[:, _head_cols(g)] = dqs[g] * ATTN_SCALE

        @pl.when(i == nb - 1)
        def _():
            dk_ref[...] = dk_acc[...] * ATTN_SCALE
            dv_ref[...] = dv_acc[...].astype(dv_ref.dtype)
            dc_ref[...] = dc_acc[...]

    q_spec, k_spec, v_spec = _head_specs(lp, tile, 0, 0, v_off, hp)
    tile_spec = pl.BlockSpec((tile, gw), lambda h, i: (i, h))
    head_spec = pl.BlockSpec((lp, gw), lambda h, i: (0, h))
    c_spec = pl.BlockSpec((hp, nb, 1, tile), lambda h, i: (h, 0, 0, 0))
    return pl.pallas_call(
        body, name=name,
        grid=(HEADS // hp, nb),
        in_specs=[q_spec, k_spec, v_spec, c_spec, tile_spec, tile_spec,
                  pl.BlockSpec((hp, tile, LANES), lambda h, i: (h, i, 0))],
        out_specs=[tile_spec, head_spec, head_spec, c_spec],
        out_shape=[jax.ShapeDtypeStruct((lp, BRANCH_WIDTH), F32), jax.ShapeDtypeStruct((lp, BRANCH_WIDTH), F32),
                   jax.ShapeDtypeStruct((lp, BRANCH_WIDTH), BF16), jax.ShapeDtypeStruct((HEADS, nb, 1, tile), F32)],
        scratch_shapes=[pltpu.VMEM((lp, gw), F32), pltpu.VMEM((lp, gw), F32),
                        pltpu.VMEM((hp, nb, 1, tile), F32)],
        compiler_params=pltpu.CompilerParams(dimension_semantics=("parallel", "arbitrary")),
    )(q, k, v, c_rows, o, do, lse)


def _later_matrix(tile):
    return (lax.broadcasted_iota(jnp.int32, (tile, tile), 0) > lax.broadcasted_iota(jnp.int32, (tile, tile), 1)).astype(BF16)


def _earlier_matrix(tile):
    return (lax.broadcasted_iota(jnp.int32, (tile, tile), 0) < lax.broadcasted_iota(jnp.int32, (tile, tile), 1)).astype(BF16)


def _running_sums(x, tri, suffix):
    tile = tri.shape[0]
    blocks = [x[:, b:b + tile] for b in range(0, x.shape[1], tile)]
    sums = [jnp.sum(blk, axis=1, keepdims=True) for blk in blocks]
    out = []
    for b, blk in enumerate(blocks):
        hi, lo = _split_bf16(blk)
        inside = _dot(hi, tri, NN) + _dot(lo, tri, NN)
        for other in (sums[b + 1:] if suffix else sums[:b]):
            inside = inside + other
        out.append(inside)
    total = sums[0]
    for other in sums[1:]:
        total = total + other
    return (out[0] if len(out) == 1 else jnp.concatenate(out, axis=1)), total


def _sb_tile(qt, kt, ok, later):
    z = _dot(qt, kt, NT) * ATTN_SCALE
    e, r, lg = _sigmoid_parts(z)
    sp = jnp.maximum(z, 0.0) + lg
    spm = sp if ok is None else jnp.where(ok, sp, 0.0)
    within, sp_here = _running_sums(spm, later, True)
    return z, e, r, sp, sp_here, within


def _sb_fwd(qkv, q_off, k_off, v_off, *, tile, pad, name):
    lp = qkv.shape[0]
    nb = lp // tile
    hp = FWD_GROUP
    gw = hp * HEAD_DIM

    def body(q_ref, k_ref, v_ref, o_ref, tot_ref):
        i = pl.program_id(1)
        qts = [q_ref[:, _head_cols(g)] for g in range(hp)]
        later = _later_matrix(tile)

        def step(j, carry, masked, nt):
            ok = _attn_mask(i, j, tile, pad, True) if masked else None
            out = []
            for g, (right, acc) in enumerate(carry):
                z, _, _, sp, sp_here, within = _sb_tile(qts[g], _key_tile(k_ref, j, tile, g, nt), ok, later)
                w = jnp.exp(z - sp - within - right)
                if masked:
                    w = jnp.where(ok, w, 0.0)
                acc = acc + _dot(w.astype(BF16), _key_tile(v_ref, j, tile, g, nt), NN)
                out.append((right + sp_here, acc))
            return tuple(out)

        init = (jnp.zeros((tile, 1), F32), jnp.zeros((tile, HEAD_DIM), F32))
        final = _walk_key_tiles(i, step, (init,) * hp, (4, 2, 1), descending=True)
        for g, (total, acc) in enumerate(final):
            o_ref[:, _head_cols(g)] = acc.astype(o_ref.dtype)
            tot_ref[g] = jnp.broadcast_to(total, (tile, LANES))

    q_spec, k_spec, v_spec = _head_specs(lp, tile, q_off, k_off, v_off, hp)
    return pl.pallas_call(
        body, name=name,
        grid=(HEADS // hp, nb),
        in_specs=[q_spec, k_spec, v_spec],
        out_specs=[pl.BlockSpec((tile, gw), lambda h, i: (i, h)),
                   pl.BlockSpec((hp, tile, LANES), lambda h, i: (h, i, 0))],
        out_shape=[jax.ShapeDtypeStruct((lp, BRANCH_WIDTH), BF16), jax.ShapeDtypeStruct((HEADS, lp, LANES), F32)],
        compiler_params=pltpu.CompilerParams(dimension_semantics=("parallel", "arbitrary")),
    )(qkv, qkv, qkv)


def _sb_bwd(qkv, q_off, k_off, v_off, do, total, *, tile, pad, name):
    lp = qkv.shape[0]
    nb = lp // tile
    hp = BWD_GROUP
    gw = hp * HEAD_DIM

    def body(q_ref, k_ref, v_ref, do_ref, tot_ref, dq_ref, dk_ref, dv_ref, dk_acc, dv_acc):
        i = pl.program_id(1)

        @pl.when(i == 0)
        def _():
            dk_acc[...] = jnp.zeros_like(dk_acc)
            dv_acc[...] = jnp.zeros_like(dv_acc)

        heads = range(hp)
        qts = [q_ref[:, _head_cols(g)] for g in heads]
        dots = [do_ref[:, _head_cols(g)] for g in heads]
        total_cols = [tot_ref[g][:, :1] for g in heads]
        later, earlier = _later_matrix(tile), _earlier_matrix(tile)

        def step(j, carry, masked, nt):
            rows = _key_rows(j, tile, nt)
            ok = _attn_mask(i, j, tile, pad, True) if masked else None
            out = []
            for g, (dq, sp_before, dlw_before) in enumerate(carry):
                kt = _key_tile(k_ref, j, tile, g, nt)
                z, e, r, sp, sp_here, within = _sb_tile(qts[g], kt, ok, later)
                right = total_cols[g] - sp_before - sp_here
                w = jnp.exp(z - sp - within - right)
                if masked:
                    w = jnp.where(ok, w, 0.0)
                dlw = w * _dot(dots[g], _key_tile(v_ref, j, tile, g, nt), NT)
                before, dlw_here = _running_sums(dlw, earlier, False)
                sig = jnp.where(z >= 0, r, e * r)
                dz = dlw * (1.0 - sig) - sig * (before + dlw_before)
                if masked:
                    dz = jnp.where(ok, dz, 0.0)
                dzb = dz.astype(BF16)
                dk_acc[rows, _head_cols(g)] += _dot(dzb, qts[g], TN)
                dv_acc[rows, _head_cols(g)] += _dot(w.astype(BF16), dots[g], TN)
                out.append((dq + _dot(dzb, kt, NN), sp_before + sp_here, dlw_before + dlw_here))
            return tuple(out)

        zero_col = jnp.zeros((tile, 1), F32)
        final = _walk_key_tiles(i, step, ((jnp.zeros((tile, HEAD_DIM), F32), zero_col, zero_col),) * hp, (8, 4, 2, 1))
        for g in heads:
            dq_ref[:, _head_cols(g)] = (final[g][0] * ATTN_SCALE).astype(dq_ref.dtype)

        @pl.when(i == nb - 1)
        def _():
            dk_ref[...] = (dk_acc[...] * ATTN_SCALE).astype(dk_ref.dtype)
            dv_ref[...] = dv_acc[...].astype(dv_ref.dtype)

    q_spec, k_spec, v_spec = _head_specs(lp, tile, q_off, k_off, v_off, hp)
    tile_spec = pl.BlockSpec((tile, gw), lambda h, i: (i, h))
    head_spec = pl.BlockSpec((lp, gw), lambda h, i: (0, h))
    return pl.pallas_call(
        body, name=name,
        grid=(HEADS // hp, nb),
        in_specs=[q_spec, k_spec, v_spec, tile_spec, pl.BlockSpec((hp, tile, LANES), lambda h, i: (h, i, 0))],
        out_specs=[tile_spec, head_spec, head_spec],
        out_shape=[jax.ShapeDtypeStruct((lp, BRANCH_WIDTH), BF16)] * 3,
        scratch_shapes=[pltpu.VMEM((lp, gw), F32), pltpu.VMEM((lp, gw), F32)],
        compiler_params=pltpu.CompilerParams(dimension_semantics=("parallel", "arbitrary")),
    )(qkv, qkv, qkv, do, total)


def _cumsum_rows(x, *, tile, reverse, name):
    lp = x.shape[0]
    nb = lp // tile

    def body(x_ref, o_ref, carry):
        @pl.when(pl.program_id(0) == 0)
        def _():
            carry[...] = jnp.zeros_like(carry)

        r = lax.broadcasted_iota(jnp.int32, (tile, tile), 0)
        c = lax.broadcasted_iota(jnp.int32, (tile, tile), 1)
        tri = ((c >= r) if reverse else (c <= r)).astype(BF16)
        hi, lo = _split_bf16(x_ref[...])
        run = _dot(tri, hi, NN) + _dot(tri, lo, NN) + carry[...]
        o_ref[...] = run
        carry[...] = run[:1, :] if reverse else run[tile - 1:, :]

    order = (lambda i: (nb - 1 - i, 0)) if reverse else (lambda i: (i, 0))
    return pl.pallas_call(
        body, name=name,
        grid=(nb,),
        in_specs=[pl.BlockSpec((tile, LANES), order)],
        out_specs=pl.BlockSpec((tile, LANES), order),
        out_shape=jax.ShapeDtypeStruct((lp, LANES), F32),
        scratch_shapes=[pltpu.VMEM((1, LANES), F32)],
        compiler_params=pltpu.CompilerParams(dimension_semantics=("arbitrary",)),
    )(x)


def _log_sigmoid(x):
    return jnp.minimum(x, 0.0) - jnp.log(1.0 + jnp.exp(-jnp.abs(x)))


def _forget_mask(i, tm, pad):
    row = i * tm + lax.broadcasted_iota(jnp.int32, (tm, LANES), 0)
    lane = lax.broadcasted_iota(jnp.int32, (tm, LANES), 1)
    return (row >= pad) & (lane < HEADS)


def _fox_prep(proj_a, q_gain, k_gain, b_forget, *, tm, pad, name):
    w = BRANCH_WIDTH

    def fn(i, tiles, consts):
        pa = tiles[0]
        qs, ks = [], []
        for h in range(HEADS):
            lo = h * HEAD_DIM
            qs.append(_rms(pa[:, lo:lo + HEAD_DIM], consts[0][:, lo:lo + HEAD_DIM]))
            ks.append(_rms(pa[:, w + lo:w + lo + HEAD_DIM], consts[1][:, lo:lo + HEAD_DIM]))
        logf = jnp.where(_forget_mask(i, tm, pad), _log_sigmoid(pa[:, 2 * w:] + consts[2]), 0.0)
        return [jnp.concatenate(qs, axis=1), jnp.concatenate(ks, axis=1), logf], []

    return _rowwise(fn, [_whole(proj_a)], [q_gain, k_gain, b_forget],
                    [(w, BF16), (w, BF16), (LANES, F32)], [], tm=tm, name=name)


def _fox_prep_bwd(proj_a, dq, dk, dlogf, q_gain, k_gain, b_forget, *, tm, pad, name):
    w = BRANCH_WIDTH

    def fn(i, tiles, consts):
        pa, dqt, dkt, dlf = tiles
        dxs_q, dxs_k, dgs_q, dgs_k = [], [], [], []
        for h in range(HEADS):
            lo = h * HEAD_DIM
            dx, dg = _rms_bwd(pa[:, lo:lo + HEAD_DIM], consts[0][:, lo:lo + HEAD_DIM], dqt[:, lo:lo + HEAD_DIM])
            dxs_q.append(dx)
            dgs_q.append(jnp.sum(dg, axis=0, keepdims=True))
            dx, dg = _rms_bwd(pa[:, w + lo:w + lo + HEAD_DIM], consts[1][:, lo:lo + HEAD_DIM], dkt[:, lo:lo + HEAD_DIM])
            dxs_k.append(dx)
            dgs_k.append(jnp.sum(dg, axis=0, keepdims=True))
        xf = pa[:, 2 * w:] + consts[2]
        e, r, _ = _sigmoid_parts(xf)
        df = jnp.where(_forget_mask(i, tm, pad), dlf * jnp.where(xf >= 0, e * r, r), 0.0)
        return ([jnp.concatenate(dxs_q + dxs_k + [df], axis=1)],
                [jnp.concatenate(dgs_q, axis=1), jnp.concatenate(dgs_k, axis=1), jnp.sum(df, axis=0, keepdims=True)])

    return _rowwise(fn, [_whole(proj_a), _whole(dq), _whole(dk), _whole(dlogf)], [q_gain, k_gain, b_forget],
                    [(2 * w + LANES, BF16)], [(1, w), (1, w), (1, LANES)], tm=tm, name=name)


def _adamw_math(w, g, m, v):
    m = ADAM_B1 * m + (1.0 - ADAM_B1) * g
    v = ADAM_B2 * v + (1.0 - ADAM_B2) * (g * g)
    m_hat = m / (1.0 - ADAM_B1 ** ADAM_STEP)
    v_hat = v / (1.0 - ADAM_B2 ** ADAM_STEP)
    delta = -ADAM_LR * (m_hat / (jnp.sqrt(v_hat) + ADAM_EPS) + ADAM_WD * w)
    return delta, m, v


def _adamw_summed(parts, sel, w, m, v, *, name):
    rows, cols = w.shape
    tr = _pick(rows, [t for t in (512, 256, 128, 64, 32, 16, 8) if t * cols <= ADAMW_TILE_ELEMS])

    def body(p_ref, w_ref, m_ref, v_ref, g_out, d_out, m_out, v_out):
        g = p_ref[0].astype(F32)
        for j in range(1, NDEV):
            g = g + p_ref[j].astype(F32)
        delta, m_new, v_new = _adamw_math(w_ref[...], g, m_ref[...], v_ref[...])
        g_out[...] = g
        d_out[...] = delta
        m_out[...] = m_new
        v_out[...] = v_new

    spec = pl.BlockSpec((tr, cols), lambda i: (i, 0))
    if parts.ndim == 3:
        p_spec = pl.BlockSpec((NDEV, tr, cols), lambda i: (0, i, 0))
    else:
        p_spec = pl.BlockSpec((NDEV, None, tr, cols), lambda i: (0, sel, i, 0))
    return pl.pallas_call(
        body, name=name,
        grid=(rows // tr,),
        in_specs=[p_spec, spec, spec, spec],
        out_specs=[spec] * 4,
        out_shape=[jax.ShapeDtypeStruct((rows, cols), F32)] * 4,
        compiler_params=pltpu.CompilerParams(dimension_semantics=("parallel",)),
    )(parts, w, m, v)


def _adamw_plain(g, w, m, v, *, name):
    def body(g_ref, w_ref, m_ref, v_ref, d_out, m_out, v_out):
        delta, m_new, v_new = _adamw_math(w_ref[...], g_ref[...], m_ref[...], v_ref[...])
        d_out[...] = delta
        m_out[...] = m_new
        v_out[...] = v_new

    return pl.pallas_call(body, name=name, out_shape=[jax.ShapeDtypeStruct(w.shape, F32)] * 3)(g, w, m, v)


def _pack_rows(arrays, width, row_align):
    pieces, spans, at = [], [], 0
    for arr in arrays:
        flat = arr.reshape(-1)
        rows = -(-flat.shape[0] // (width * row_align)) * row_align
        flat = jnp.pad(flat, (0, rows * width - flat.shape[0]))
        pieces.append(flat.reshape(rows, width))
        spans.append((at, rows))
        at += rows
    return jnp.concatenate(pieces, axis=0), spans


def _unpack(rows2d, span, shape):
    at, rows = span
    size = 1
    for s in shape:
        size *= s
    return rows2d[at:at + rows].reshape(-1)[:size].reshape(shape)


def _join_cols(blocks):
    n, rows, cols = blocks.shape
    return jnp.transpose(blocks, (1, 0, 2)).reshape(rows, n * cols)


def _split_cols(full):
    rows, cols = full.shape
    return jnp.transpose(full.reshape(rows, NDEV, cols // NDEV), (1, 0, 2))


def _ffn_up(h, gain, w_gu, after, *, tm, tag):
    lp, d = h.shape
    f8 = w_gu.shape[3]
    n = _norm_fwd(h, gain, tm=_pick(lp, [256, 128]), name=f"{tag}_norm")
    hid = _spec((None, tm, f8), lambda i, j, k: (j, i, 0))
    a, b, s = _mm((lp // tm, NDEV, 1), n, _spec((tm, d), lambda i, j, k: (i, 0)),
                  [(w_gu, _spec((None, None, d, f8), lambda i, j, k: (j, 0, 0, 0))),
                   (w_gu, _spec((None, None, d, f8), lambda i, j, k: (j, 1, 0, 0)))],
                  [], [((NDEV, lp, f8), BF16, hid)] * 3, dims=NN, epi=_swiglu_epi, after=after, name=f"{tag}_up")
    return n, a, b, s


def _ffn_down(h, s, w_down, *, tm, tag):
    lp, d = h.shape
    f8 = w_down.shape[1]
    tn = _pick(d, [1024, 512, 256, 128])
    tile = _spec((tm, tn), lambda i, j, k: (i, j))
    return _mm((lp // tm, d // tn, NDEV // KSUB), s, _spec((KSUB, tm, f8), lambda i, j, k: (k, i, 0)),
               [(w_down, _spec((KSUB, f8, tn), lambda i, j, k: (k, 0, j)))], [(h, tile)], [((lp, d), F32, tile)],
               dims=NN, acc_shape=(tm, tn), epi=lambda accs, ex: [ex[0] + FFN_RESIDUAL_WEIGHT * accs[0]], name=f"{tag}_down")[0]


def _ffn_bwd_dw(dh_half, saved, w_down, on_down, *, tm, tag):
    n, a, b, s = saved
    lp, d = dh_half.shape
    f8 = w_down.shape[1]
    tkr = _pick(lp, [4352, 2176, 1088, 544, 256, 128])
    tn = _pick(d, [1024, 512, 256, 128])
    hid = _spec((None, tm, f8), lambda i, j, k: (j, i, 0))
    dab, = _mm((lp // tm, NDEV, 1), dh_half, _spec((tm, d), lambda i, j, k: (i, 0)),
               [(w_down, _spec((None, f8, d), lambda i, j, k: (j, 0, 0)))], [(a, hid), (b, hid)],
               [((2, NDEV, lp, f8), BF16, _spec((2, None, tm, f8), lambda i, j, k: (0, j, i, 0)))],
               dims=NT, epi=_swiglu_bwd_epi, name=f"{tag}_down_dx")
    dw_down, = _mm((NDEV, d // tn, lp // tkr), s, _spec((None, tkr, f8), lambda i, j, k: (i, k, 0)),
                   [(dh_half, _spec((tkr, tn), lambda i, j, k: (k, j)))], [],
                   [((NDEV, f8, d), BF16, _spec((None, f8, tn), lambda i, j, k: (i, 0, j)))],
                   dims=TN, acc_shape=(f8, tn), name=f"{tag}_down_dw")
    dw_gu, = _mm((d // tn, 2 * NDEV, lp // tkr), n, _spec((tkr, tn), lambda i, j, k: (k, i)),
                 [(dab, _spec((None, None, tkr, f8), lambda i, j, k: (j // NDEV, j % NDEV, k, 0)))], [],
                 [((NDEV, 2, d, f8), BF16, _spec((None, None, tn, f8), lambda i, j, k: (j % NDEV, j // NDEV, i, 0)))],
                 dims=TN, acc_shape=(tn, f8), after=on_down(dw_down), name=f"{tag}_gate_up_dw")
    return dab, dw_gu, dw_down


def _ffn_bwd_dx(dab, w_gu, after, *, tm, tag):
    lp, f8 = dab.shape[2], dab.shape[3]
    d = w_gu.shape[2]
    tn = _pick(d, [1024, 512, 256, 128])
    nsub = NDEV // KSUB
    return _mm((lp // tm, d // tn, 2 * nsub), dab, _spec((None, KSUB, tm, f8), lambda i, j, k: (k // nsub, k % nsub, i, 0)),
               [(w_gu, _spec((KSUB, None, tn, f8), lambda i, j, k: (k % nsub, k // nsub, j, 0)))], [],
               [((lp, d), F32, _spec((tm, tn), lambda i, j, k: (i, j)))],
               dims=NT, acc_shape=(tm, tn), after=after, name=f"{tag}_gate_up_dx")[0]


def kernel(x, meta_tokens, ffn1_norm, ffn1_w_gate, ffn1_w_up, ffn1_w_down, mix_norm, w_in, b_forget, fox_q_norm, fox_k_norm, w_branch_fox, w_branch_sb, w_out, ffn2_norm, ffn2_w_gate, ffn2_w_up, ffn2_w_down, loss_target, m_meta_tokens, m_ffn1_norm, m_ffn1_w_gate, m_ffn1_w_up, m_ffn1_w_down, m_mix_norm, m_w_in, m_b_forget, m_fox_q_norm, m_fox_k_norm, m_w_branch_fox, m_w_branch_sb, m_w_out, m_ffn2_norm, m_ffn2_w_gate, m_ffn2_w_up, m_ffn2_w_down, v_meta_tokens, v_ffn1_norm, v_ffn1_w_gate, v_ffn1_w_up, v_ffn1_w_down, v_mix_norm, v_w_in, v_b_forget, v_fox_q_norm, v_fox_k_norm, v_w_branch_fox, v_w_branch_sb, v_w_out, v_ffn2_norm, v_ffn2_w_gate, v_ffn2_w_up, v_ffn2_w_down):
    seq, d = x.shape[1], x.shape[2]
    d8 = d // NDEV
    w = BRANCH_WIDTH
    tile = 256 if seq % 256 == 0 else 128
    pad = tile - N_META
    lp = tile + seq
    tm = _pick(lp, [1088, 544, 256, 128])
    tr = _pick(tile, [256, 128])
    tkr = _pick(lp, [4352, 2176, 1088, 544, 256, 128])
    nb = lp // tile
    me = _flat_id(*_my_place())

    shards = [jnp.stack([ffn1_w_gate[0], ffn1_w_up[0]]).astype(BF16), ffn1_w_down[0].astype(BF16), w_in[0].astype(BF16),
              w_branch_fox[0].astype(BF16), w_branch_sb[0].astype(BF16), w_out[0].astype(BF16),
              jnp.stack([ffn2_w_gate[0], ffn2_w_up[0]]).astype(BF16), ffn2_w_down[0].astype(BF16)]
    w_gu1, = _allgather(shards[:1], "gather_ffn1")
    down1_copies, token = _send_start(shards[1:2], [_landing(s, me) for s in shards[1:2]], "first", w_gu1, "gather_down1_start")
    w_in_copies, token = _send_start(shards[2:3], [_landing(s, me) for s in shards[2:3]], "first", token, "gather_w_in_start")
    mixer_copies, token = _send_start(shards[3:6], [_landing(s, me) for s in shards[3:6]], "gather", token, "gather_mixer_start")
    ffn2_copies, token = _send_start(shards[6:], [_landing(s, me) for s in shards[6:]], "gather", token, "gather_ffn2_start")
    meta_full = _join_cols(_allgather([meta_tokens], "gather_meta", in_vmem=True)[0])

    h0 = jnp.concatenate([jnp.zeros((pad, d), F32), meta_full.astype(F32), x[0]], axis=0)
    saved1 = _ffn_up(h0, ffn1_norm, w_gu1, token, tm=tm, tag="ffn1")
    w_down1, = _gather_two_level(down1_copies, saved1[3], "gather_down1")
    h1 = _ffn_down(h0, saved1[3], w_down1, tm=tm, tag="ffn1")

    w_in_blocks, = _gather_two_level(w_in_copies, h1, "gather_w_in")
    wi = _join_cols(w_in_blocks)
    w_pa = jnp.concatenate([wi[:, :2 * w], jnp.pad(wi[:, 3 * w:3 * w + HEADS], ((0, 0), (0, LANES - HEADS)))], axis=1)
    w_pb = jnp.concatenate([wi[:, 2 * w:3 * w], wi[:, 3 * w + HEADS:]], axis=1)
    na, nbw = w_pa.shape[1], w_pb.shape[1]
    gate_blk = 4 * w // d

    n2 = _norm_fwd(h1, mix_norm, tm=tr, name="mix_norm")
    tma = _pick(lp, [544, 256, 128])
    tnd = _pick(d, [1024, 512, 256, 128])
    tnb = _pick(nbw, [512, 256, 128])
    proj_a = _mm2d(n2, w_pa, na, dims=NN, tm=tma, tn=na, tk=d, out_dtype=F32, name="proj_a")
    proj_b = _mm2d(n2, w_pb, nbw, dims=NN, tm=tm, tn=tnb, tk=d, out_dtype=BF16, name="proj_b")
    b_pad = jnp.pad(b_forget, ((0, 0), (0, LANES - HEADS)))
    q_gain, k_gain = fox_q_norm.reshape(1, w), fox_k_norm.reshape(1, w)
    fq, fk, logf = _fox_prep(proj_a, q_gain, k_gain, b_pad, tm=tr, pad=pad, name="fox_prep")
    c = _cumsum_rows(logf, tile=tile, reverse=False, name="forget_cumsum")
    c_rows = jnp.transpose(c[:, :HEADS]).reshape(HEADS, nb, 1, tile)
    o_fox, lse = _fox_fwd(fq, fk, proj_b, 0, c_rows, tile=tile, pad=pad, name="fox_fwd")
    o_sb, sb_total = _sb_fwd(proj_b, HEADS, 2 * HEADS, 3 * HEADS, tile=tile, pad=pad, name="sb_fwd")
    w_br_fox, w_br_sb, w_out_blocks = _send_wait(mixer_copies, o_sb, "gather_mixer_wait")
    w_out_full = w_out_blocks.reshape(d, d)

    def branch(o, w_blocks, name):
        return _mm((lp // tm, NDEV, 1), o, _spec((tm, w), lambda i, j, k: (i, 0)),
                   [(w_blocks, _spec((None, w, d8), lambda i, j, k: (j, 0, 0)))], [],
                   [((lp, d), BF16, _spec((tm, d8), lambda i, j, k: (i, j)))], dims=NN, name=name)[0]

    br_fox = branch(o_fox, w_br_fox, "branch_fox")
    br_sb = branch(o_sb, w_br_sb, "branch_sb")

    def merge_fn(i, tiles, consts):
        bf_, bs_, gf_, gs_ = [t.astype(F32) for t in tiles]
        return [_sigmoid(gf_) * bf_ + _sigmoid(gs_) * bs_], []

    gates_in = [(proj_b, d, gate_blk), (proj_b, d, gate_blk + 1)]
    merged, = _rowwise(merge_fn, [_whole(br_fox), _whole(br_sb)] + gates_in, [], [(d, BF16)], [], tm=tr, name="merge")
    h2 = _mm2d(merged, w_out_full, d, dims=NN, tm=tm, tn=tnd, tk=d, out_dtype=F32,
               epi=lambda accs, ex: [ex[0] + accs[0]], extras=[h1], name="out_proj")

    w_gu2, w_down2 = _send_wait(ffn2_copies, h2, "gather_ffn2_wait")
    saved3 = _ffn_up(h2, ffn2_norm, w_gu2, None, tm=tm, tag="ffn2")
    h3 = _ffn_down(h2, saved3[3], w_down2, tm=tm, tag="ffn2")

    skip = tile // tr

    def loss_fn(i, tiles, consts):
        real = i >= skip
        err = jnp.where(real, tiles[0] - tiles[1], 0.0)
        dy = err * (1.0 / d)
        part = 0.5 * jnp.sum(err * dy, axis=0, keepdims=True)
        return [dy, FFN_RESIDUAL_WEIGHT * dy], [part]

    dh3, dh3_half, loss_cols = _rowwise(
        loss_fn, [_whole(h3), _whole(loss_target[0])], [], [(d, F32, lp), (d, BF16, lp)], [(1, d)], tm=tr, name="loss",
        row_maps=[None, lambda i: jnp.maximum(i - skip, 0)])

    def own(g):
        return lax.dynamic_index_in_dim(g, me, 0, keepdims=False)

    def send_grads(grads_, name):
        return _send_start(grads_, [_landing(own(g), me) for g in grads_], "exchange", grads_[-1], name)

    sends = {}

    def send_piece(key):
        def on_ready(*gs):
            sends[key], token_ = send_grads(list(gs), f"exchange_{key}_start")
            return token_
        return on_ready

    dab3, dw_gu2, _ = _ffn_bwd_dw(dh3_half, saved3, w_down2, send_piece("down2"), tm=tm, tag="ffn2")
    dn3 = _ffn_bwd_dx(dab3, w_gu2, send_piece("gu2")(dw_gu2), tm=tm, tag="ffn2")
    dh2, dh2_bf, dg_ffn2 = _norm_bwd(dn3, h2, ffn2_norm, dh3, 1.0, tm=tr, name="ffn2_norm_bwd")

    dmerged = _mm2d(dh2_bf, w_out_full, d, dims=NT, tm=tm, tn=tnd, tk=d, out_dtype=BF16, name="out_proj_dx")
    dw_out = _mm2d(merged, dh2_bf, d, dims=TN, tm=tnd, tn=tnd, tk=tkr, out_dtype=BF16, name="out_proj_dw")
    token = send_piece("out")(dw_out.reshape(NDEV, d8, d))

    def merge_bwd_fn(i, tiles, consts):
        dm, bf_, bs_, gf_, gs_ = [t.astype(F32) for t in tiles]
        sf, ss = _sigmoid(gf_), _sigmoid(gs_)
        return [dm * sf, dm * ss, dm * bf_ * sf * (1.0 - sf), dm * bs_ * ss * (1.0 - ss)], []

    dbr_fox, dbr_sb, dg_fox, dg_sb = _rowwise(
        merge_bwd_fn, [_whole(dmerged), _whole(br_fox), _whole(br_sb)] + gates_in, [], [(d, BF16)] * 4, [], tm=tr, after=token,
        name="merge_bwd")

    tnw = _pick(w, [512, 256, 128])

    def branch_dx(dbr, w_blocks, after, name):
        return _mm((lp // tm, w // tnw, NDEV), dbr, _spec((tm, d8), lambda i, j, k: (i, k)),
                   [(w_blocks, _spec((None, tnw, d8), lambda i, j, k: (k, j, 0)))], [],
                   [((lp, w), BF16, _spec((tm, tnw), lambda i, j, k: (i, j)))], dims=NT, acc_shape=(tm, tnw), after=after,
                   name=name)[0]

    def branch_dw(o, dbr, name):
        return _mm((w // tnw, NDEV, lp // tkr), o, _spec((tkr, tnw), lambda i, j, k: (k, i)),
                   [(dbr, _spec((tkr, d8), lambda i, j, k: (k, j)))], [],
                   [((NDEV, w, d8), BF16, _spec((None, tnw, d8), lambda i, j, k: (j, i, 0)))],
                   dims=TN, acc_shape=(tnw, d8), name=name)[0]

    token = send_piece("branch")(branch_dw(o_fox, dbr_fox, "branch_fox_dw"), branch_dw(o_sb, dbr_sb, "branch_sb_dw"))
    do_fox = branch_dx(dbr_fox, w_br_fox, token, "branch_fox_dx")
    do_sb = branch_dx(dbr_sb, w_br_sb, token, "branch_sb_dx")

    dfq, dfk, dfv, dc_rows = _fox_bwd(fq, fk, proj_b, 0, c_rows, o_fox, do_fox, lse, tile=tile, pad=pad, name="fox_bwd")
    dsq, dsk, dsv = _sb_bwd(proj_b, HEADS, 2 * HEADS, 3 * HEADS, do_sb, sb_total, tile=tile, pad=pad, name="sb_bwd")
    dc = jnp.pad(jnp.transpose(dc_rows.reshape(HEADS, lp)), ((0, 0), (0, LANES - HEADS)))
    dlogf = _cumsum_rows(dc, tile=tile, reverse=True, name="forget_cumsum_bwd")
    dproj_a, dg_q, dg_k, dg_b = _fox_prep_bwd(proj_a, dfq, dfk, dlogf, q_gain, k_gain, b_pad, tm=tr, pad=pad, name="fox_prep_bwd")
    dproj_b = jnp.concatenate([dfv, dsq, dsk, dsv, dg_fox, dg_sb], axis=1)

    dw_pa = _mm2d(n2, dproj_a, na, dims=TN, tm=tnd, tn=na, tk=_pick(lp, [544, 256, 128]), out_dtype=BF16, name="proj_a_dw")
    dw_pb = _mm2d(n2, dproj_b, nbw, dims=TN, tm=tnd, tn=tnb, tk=tkr, out_dtype=BF16, name="proj_b_dw")
    dw_in = jnp.concatenate([dw_pa[:, :2 * w], dw_pb[:, :w], dw_pa[:, 2 * w:2 * w + HEADS], dw_pb[:, w:]], axis=1)
    token = send_piece("w_in")(_split_cols(dw_in))
    dn2_a = _mm2d(dproj_a, w_pa, d, dims=NT, tm=tm, tn=tnd, tk=na, out_dtype=F32, after=token, name="proj_a_dx")
    dn2 = _mm2d(dproj_b, w_pb, d, dims=NT, tm=tm, tn=tnd, tk=_pick(nbw, [2048, 1024, 512, 256, 128]), out_dtype=F32,
                epi=lambda accs, ex: [accs[0] + ex[0]], extras=[dn2_a], name="proj_b_dx")
    dh1, dh1_half, dg_mix = _norm_bwd(dn2, h1, mix_norm, dh2, FFN_RESIDUAL_WEIGHT, tm=tr, name="mix_norm_bwd")

    dab1, dw_gu1, _ = _ffn_bwd_dw(dh1_half, saved1, w_down1, send_piece("down1"), tm=tm, tag="ffn1")
    dn1 = _ffn_bwd_dx(dab1, w_gu1, send_piece("gu1")(dw_gu1), tm=tm, tag="ffn1")
    dh0, _, dg_ffn1_tokens = _norm_bwd(dn1, h0, ffn1_norm, dh1, None, tm=tile, name="ffn1_norm_bwd", first=1, count=seq // tile)
    dh0_meta, _, dg_ffn1_meta = _norm_bwd(dn1, h0, ffn1_norm, dh1, None, tm=tile, name="ffn1_norm_bwd_meta", count=1)
    dg_ffn1 = dg_ffn1_tokens + dg_ffn1_meta
    grad_x = dh0[None]

    r_down2, = _send_wait(sends["down2"], dh0, "exchange_down2_wait")
    r_gu2, = _send_wait(sends["gu2"], r_down2, "exchange_gu2_wait")
    r_out, = _send_wait(sends["out"], r_gu2, "exchange_out_wait")
    r_br_fox, r_br_sb = _send_wait(sends["branch"], r_out, "exchange_branch_wait")
    r_in, = _send_wait(sends["w_in"], r_br_sb, "exchange_w_in_wait")
    grads, deltas, new_ms, new_vs = {}, {}, {}, {}

    def adamw_big(entries):
        for k, (parts, sel, wt, mt, vt) in entries.items():
            g, dl, mn, vn = _adamw_summed(parts, sel, wt[0], mt[0], vt[0], name=f"adamw_{k}")
            grads[k], deltas[k], new_ms[k], new_vs[k] = g[None], dl[None], mn[None], vn[None]

    adamw_big(dict(ffn2_w_gate=(r_gu2, 0, ffn2_w_gate, m_ffn2_w_gate, v_ffn2_w_gate),
                   ffn2_w_up=(r_gu2, 1, ffn2_w_up, m_ffn2_w_up, v_ffn2_w_up),
                   ffn2_w_down=(r_down2, 0, ffn2_w_down, m_ffn2_w_down, v_ffn2_w_down),
                   w_in=(r_in, 0, w_in, m_w_in, v_w_in),
                   w_branch_fox=(r_br_fox, 0, w_branch_fox, m_w_branch_fox, v_w_branch_fox),
                   w_branch_sb=(r_br_sb, 0, w_branch_sb, m_w_branch_sb, v_w_branch_sb),
                   w_out=(r_out, 0, w_out, m_w_out, v_w_out)))
    done = sum(v[0, 0, :1] for v in new_vs.values())
    r_down1, = _send_wait(sends["down1"], done, "exchange_down1_wait")
    r_gu1, = _send_wait(sends["gu1"], r_down1, "exchange_gu1_wait")
    adamw_big(dict(ffn1_w_gate=(r_gu1, 0, ffn1_w_gate, m_ffn1_w_gate, v_ffn1_w_gate),
                   ffn1_w_up=(r_gu1, 1, ffn1_w_up, m_ffn1_w_up, v_ffn1_w_up),
                   ffn1_w_down=(r_down1, 0, ffn1_w_down, m_ffn1_w_down, v_ffn1_w_down)))

    small_parts = [dh0_meta[pad:tile], dg_ffn1, dg_mix, dg_ffn2, dg_b[:, :HEADS], dg_q, dg_k, loss_cols]
    small_packed, small_spans = _pack_rows(small_parts, LANES, SMALL_ROWS)
    small_sum = _allsum_small(small_packed, "sum_small")
    g_meta_full, g_ffn1n, g_mixn, g_ffn2n, g_bf, g_qn, g_kn, loss_vec = [
        _unpack(small_sum, span, part.shape) for span, part in zip(small_spans, small_parts)]
    loss = jnp.sum(loss_vec)
    g_meta = lax.dynamic_slice_in_dim(g_meta_full, me * d8, d8, axis=1)
    g_qn, g_kn = g_qn.reshape(fox_q_norm.shape), g_kn.reshape(fox_k_norm.shape)

    small = dict(meta_tokens=(g_meta, meta_tokens, m_meta_tokens, v_meta_tokens),
                 ffn1_norm=(g_ffn1n, ffn1_norm, m_ffn1_norm, v_ffn1_norm),
                 mix_norm=(g_mixn, mix_norm, m_mix_norm, v_mix_norm),
                 b_forget=(g_bf, b_forget, m_b_forget, v_b_forget),
                 fox_q_norm=(g_qn, fox_q_norm, m_fox_q_norm, v_fox_q_norm),
                 fox_k_norm=(g_kn, fox_k_norm, m_fox_k_norm, v_fox_k_norm),
                 ffn2_norm=(g_ffn2n, ffn2_norm, m_ffn2_norm, v_ffn2_norm))
    for k, (g, wt, mt, vt) in small.items():
        flat = lambda t: t.reshape(-1, t.shape[-1])
        dl, mn, vn = _adamw_plain(flat(g), flat(wt), flat(mt), flat(vt), name=f"adamw_{k}")
        grads[k], deltas[k], new_ms[k], new_vs[k] = g, dl.reshape(wt.shape), mn.reshape(wt.shape), vn.reshape(wt.shape)

    order = ["meta_tokens", "ffn1_norm", "ffn1_w_gate", "ffn1_w_up", "ffn1_w_down", "mix_norm", "w_in", "b_forget",
             "fox_q_norm", "fox_k_norm", "w_branch_fox", "w_branch_sb", "w_out", "ffn2_norm", "ffn2_w_gate",
             "ffn2_w_up", "ffn2_w_down"]
    return (loss, grad_x, *[grads[k] for k in order], *[deltas[k] for k in order],
            *[new_ms[k] for k in order], *[new_vs[k] for k in order])
```
